```python
import math
import jax, jax.numpy as jnp
from jax import lax
import numpy as np

D_MODEL = 1024
BATCH = 16
SEQ = 256
DEPTH = 2
DEC_BATCH = 2
DEC_SEQ = 4096
PAST_LEN = 256

GRID_W = 64
HEAD_DIM = 64
CONV_CH = D_MODEL // 2
CONV_WIDTH = 3
DIFF_QK = 2 * HEAD_DIM
DIFF_V = 2 * HEAD_DIM
DIFF_HEADS = D_MODEL // (2 * DIFF_V)
NA_HEADS = D_MODEL // (2 * HEAD_DIM)
NA_WIN_R = 8
NA_WIN_C = 16
GQA_Q_HEADS = D_MODEL // (2 * HEAD_DIM)
GQA_KV_HEADS = 2
GQA_GROUP = GQA_Q_HEADS // GQA_KV_HEADS
NA_W = NA_HEADS * HEAD_DIM
GQA_Q_W = GQA_Q_HEADS * HEAD_DIM
GQA_KV_W = GQA_KV_HEADS * HEAD_DIM
AB_SPLITS = (CONV_CH, 2 * CONV_CH, 3 * CONV_CH, 3 * CONV_CH + DIFF_HEADS * DIFF_QK, 3 * CONV_CH + 2 * DIFF_HEADS * DIFF_QK)
AB_IN = 3 * CONV_CH + 2 * DIFF_HEADS * DIFF_QK + DIFF_HEADS * DIFF_V
AB_OUT = CONV_CH + DIFF_HEADS * DIFF_V
CD_SPLITS = (NA_W, 2 * NA_W, 3 * NA_W, 3 * NA_W + GQA_Q_W, 3 * NA_W + GQA_Q_W + GQA_KV_W)
CD_IN = 3 * NA_W + GQA_Q_W + 2 * GQA_KV_W
CD_OUT = NA_W + GQA_Q_W
N_EXPERTS = 32
TOP_K = 4
D_FF = D_MODEL
SWIGLU_LIMIT = 7.0
SWIGLU_ALPHA = 1.702
ROPE_THETA = 10000.0
Q_BLOCK = 128
N_EVEN = (DEPTH + 1) // 2
N_ODD = DEPTH // 2
DEEPNORM_ALPHA = (2 * DEPTH) ** 0.25
DEEPNORM_BETA = (8 * DEPTH) ** -0.25
LN_EPS = 1e-5
RMS_EPS = 1e-6
CACHE_NAMES = ('diff_k', 'diff_v', 'na_k', 'na_v', 'gqa_k', 'gqa_v')

kernel_name = 'hybrid_diffusion_conv_diffattn_natten_gqa_moe_step'


def layer_norm(x, g, b):
    xf = x.astype(jnp.float32)
    mu = jnp.mean(xf, axis=-1, keepdims=True)
    var = jnp.mean(jnp.square(xf - mu), axis=-1, keepdims=True)
    return ((xf - mu) * lax.rsqrt(var + LN_EPS) * g.astype(jnp.float32) + b.astype(jnp.float32)).astype(x.dtype)


def rms_norm(x, g):
    xf = x.astype(jnp.float32)
    return (xf * lax.rsqrt(jnp.mean(xf * xf, axis=-1, keepdims=True) + RMS_EPS) * g.astype(jnp.float32)).astype(x.dtype)


def modulation(cond, w_mod, b_mod):
    m = jax.nn.silu(cond) @ w_mod + b_mod
    return jnp.split(m, 6, axis=-1)


def axial_rope(n_tokens):
    t = jnp.arange(n_tokens)
    pos_r = (t // GRID_W).astype(jnp.float32)
    pos_c = (t % GRID_W).astype(jnp.float32)
    half = HEAD_DIM // 2
    inv = ROPE_THETA ** (-jnp.arange(0, half, 2, dtype=jnp.float32) / half)
    ang = jnp.concatenate([pos_r[:, None] * inv, pos_c[:, None] * inv], axis=-1)
    return jnp.cos(ang), jnp.sin(ang)


def apply_rope(x, cos, sin):
    xp = x.reshape(x.shape[:-1] + (HEAD_DIM // 2, 2)).astype(jnp.float32)
    x0, x1 = xp[..., 0], xp[..., 1]
    out = jnp.stack([x0 * cos - x1 * sin, x0 * sin + x1 * cos], axis=-1)
    return out.reshape(x.shape).astype(x.dtype)


def centred_short_conv(h, w):
    n = h.shape[1]
    hp = jnp.pad(h, ((0, 0), (1, 1), (0, 0)))
    return hp[:, 0:n] * w[0] + hp[:, 1:n + 1] * w[1] + hp[:, 2:n + 2] * w[2]


def sweep_query_blocks(block_fn, q):
    b, h, nq = q.shape[:3]
    rest = q.shape[3:]
    qb = jnp.moveaxis(q.reshape((b, h, nq // Q_BLOCK, Q_BLOCK) + rest), 2, 0)
    out = jnp.moveaxis(lax.map(block_fn, qb), 0, 2)
    return out.reshape((b, h, nq) + out.shape[4:])


def diff_attention(q, k_all, v_all, lam):
    scale = HEAD_DIM ** -0.5
    def block(qb):
        s = jnp.einsum('bhqid,bhkid->bhiqk', qb, k_all).astype(jnp.float32) * scale
        p = jax.nn.softmax(s, axis=-1)
        a = p[:, :, 0] - lam * p[:, :, 1]
        return jnp.einsum('bhqk,bhkd->bhqd', a.astype(v_all.dtype), v_all)
    return sweep_query_blocks(block, q)


def gqa_attention(q, k_all, v_all):
    scale = HEAD_DIM ** -0.5
    def block(qb):
        s = jnp.einsum('bhqgd,bhkd->bhgqk', qb, k_all).astype(jnp.float32) * scale
        p = jax.nn.softmax(s, axis=-1)
        return jnp.einsum('bhgqk,bhkd->bhqgd', p.astype(v_all.dtype), v_all)
    return sweep_query_blocks(block, q)


def neighbourhood_attention(q, k, v, k_ctx, v_ctx, rpb):
    b, h, n, dh = q.shape
    rows = n // GRID_W
    kr = min(NA_WIN_R, rows)
    kg = k.reshape(b, h, rows, GRID_W, dh)
    vg = v.reshape(b, h, rows, GRID_W, dh)
    q_rows = jnp.moveaxis(q.reshape(b, h, rows, GRID_W, dh), 2, 0)
    r_idx = jnp.arange(rows)
    row_start = jnp.clip(r_idx - kr // 2, 0, rows - kr)
    cols = jnp.arange(GRID_W)
    col_start = jnp.clip(cols - NA_WIN_C // 2, 0, GRID_W - NA_WIN_C)
    col_idx = col_start[:, None] + jnp.arange(NA_WIN_C)
    rpb_c = rpb[:, :, col_idx - cols[:, None] + NA_WIN_C - 1]
    scale = HEAD_DIM ** -0.5
    n_win = kr * NA_WIN_C
    def row_block(args):
        q_row, r0, rs = args
        k_win = lax.dynamic_slice_in_dim(kg, rs, kr, axis=2)[:, :, :, col_idx]
        v_win = lax.dynamic_slice_in_dim(vg, rs, kr, axis=2)[:, :, :, col_idx]
        bias = rpb_c[:, rs + jnp.arange(kr) - r0 + NA_WIN_R - 1]
        s_win = jnp.einsum('bhcd,bhicjd->bhcij', q_row, k_win).astype(jnp.float32) * scale
        s_win = s_win + jnp.transpose(bias, (0, 2, 1, 3)).astype(jnp.float32)[None]
        s_ctx = jnp.einsum('bhcd,bhld->bhcl', q_row, k_ctx).astype(jnp.float32) * scale
        p = jax.nn.softmax(jnp.concatenate([s_win.reshape(b, h, GRID_W, n_win), s_ctx], axis=-1), axis=-1)
        p = p.astype(v.dtype)
        p_win = p[..., :n_win].reshape(b, h, GRID_W, kr, NA_WIN_C)
        return (jnp.einsum('bhcij,bhicjd->bhcd', p_win, v_win)
                + jnp.einsum('bhcl,bhld->bhcd', p[..., n_win:], v_ctx))
    out = lax.map(row_block, (q_rows, r_idx, row_start))
    return jnp.moveaxis(out, 0, 2).reshape(b, h, n, dh)


def ab_mixer(u, ctx, W, j, layer_idx):
    b, n = u.shape[:2]
    lam_init = 0.8 - 0.6 * math.exp(-0.3 * layer_idx)
    p = u @ W['ab_w_in'][j]
    gate_b, gate_c, h_in, q, k, v = jnp.split(p, AB_SPLITS, axis=-1)
    conv_out = gate_b * centred_short_conv(gate_c * h_in, W['ab_conv_w'][j])
    q = q.reshape(b, n, DIFF_HEADS, 2, HEAD_DIM).transpose(0, 2, 1, 3, 4)
    k = k.reshape(b, n, DIFF_HEADS, 2, HEAD_DIM).transpose(0, 2, 1, 3, 4)
    v = v.reshape(b, n, DIFF_HEADS, DIFF_V).transpose(0, 2, 1, 3)
    f32 = jnp.float32
    lam = (jnp.exp(jnp.sum(W['ab_lambda_q1'][j].astype(f32) * W['ab_lambda_k1'][j].astype(f32)))
           - jnp.exp(jnp.sum(W['ab_lambda_q2'][j].astype(f32) * W['ab_lambda_k2'][j].astype(f32))) + lam_init)
    if ctx is None:
        attn = diff_attention(q, k, v, lam)
        produced = (k.reshape(b, DIFF_HEADS, n, DIFF_QK), v)
    else:
        k_ctx, v_ctx = ctx
        cos, sin = axial_rope(n)
        q = apply_rope(q, cos[:, None], sin[:, None])
        k = apply_rope(k, cos[:, None], sin[:, None])
        k_all = jnp.concatenate([k, k_ctx.reshape(b, DIFF_HEADS, k_ctx.shape[2], 2, HEAD_DIM)], axis=2)
        v_all = jnp.concatenate([v, v_ctx], axis=2)
        attn = diff_attention(q, k_all, v_all, lam)
        produced = ()
    attn = rms_norm(attn, W['ab_subln_g'][j]) * (1.0 - lam_init)
    attn = attn.transpose(0, 2, 1, 3).reshape(b, n, DIFF_HEADS * DIFF_V)
    return jnp.concatenate([conv_out, attn], axis=-1) @ W['ab_w_out'][j], produced


def cd_mixer(u, ctx, W, j):
    b, n = u.shape[:2]
    p = u @ W['cd_w_in'][j]
    nq, nk, nv, gq, gk, gv = jnp.split(p, CD_SPLITS, axis=-1)
    def heads(t, hh):
        return t.reshape(b, n, hh, HEAD_DIM).transpose(0, 2, 1, 3)
    nq, nk, nv = heads(nq, NA_HEADS), heads(nk, NA_HEADS), heads(nv, NA_HEADS)
    gq = rms_norm(gq.reshape(b, n, GQA_KV_HEADS, GQA_GROUP, HEAD_DIM).transpose(0, 2, 1, 3, 4), W['cd_q_norm_g'][j])
    gk = rms_norm(heads(gk, GQA_KV_HEADS), W['cd_k_norm_g'][j])
    gv = heads(gv, GQA_KV_HEADS)
    if ctx is None:
        na_out = gqa_attention(nq[:, :, :, None], nk, nv)[:, :, :, 0]
        g_out = gqa_attention(gq, gk, gv)
        produced = (nk, nv, gk, gv)
    else:
        na_k_ctx, na_v_ctx, g_k_ctx, g_v_ctx = ctx
        na_out = neighbourhood_attention(nq, nk, nv, na_k_ctx, na_v_ctx, W['cd_na_rpb'][j])
        cos, sin = axial_rope(n)
        gq = apply_rope(gq, cos[:, None], sin[:, None])
        gk = apply_rope(gk, cos, sin)
        g_out = gqa_attention(gq, jnp.concatenate([gk, g_k_ctx], axis=2), jnp.concatenate([gv, g_v_ctx], axis=2))
        produced = ()
    merged = jnp.concatenate([na_out.transpose(0, 2, 1, 3).reshape(b, n, NA_W),
                              g_out.transpose(0, 2, 1, 3, 4).reshape(b, n, GQA_Q_W)], axis=-1)
    return merged @ W['cd_w_out'][j], produced


def moe_ffn(u, router_w, router_b, w_gate_up, b_gate_up, w_down, b_down):
    b, n, d = u.shape
    xt = u.reshape(b * n, d)
    logits = (xt @ router_w + router_b).astype(jnp.float32)
    top_val, top_idx = lax.top_k(logits, TOP_K)
    weights = jax.nn.softmax(top_val, axis=-1)
    combine = jnp.einsum('tk,tke->te', weights, jax.nn.one_hot(top_idx, N_EXPERTS, dtype=jnp.float32))
    out = jnp.zeros((b * n, d), jnp.float32)
    for e in range(N_EXPERTS):
        gu = xt @ w_gate_up[e] + b_gate_up[e]
        gate = jnp.minimum(gu[:, 0::2], SWIGLU_LIMIT)
        up = jnp.clip(gu[:, 1::2], -SWIGLU_LIMIT, SWIGLU_LIMIT)
        hid = (up + 1.0) * gate * jax.nn.sigmoid(SWIGLU_ALPHA * gate)
        out = out + combine[:, e:e + 1] * (hid @ w_down[e] + b_down[e])
    return out.astype(u.dtype).reshape(b, n, d)


def run_trunk(x, cond, W, caches):
    made = {nm: [] for nm in CACHE_NAMES}
    for l in range(DEPTH):
        j = l // 2
        shift1, scale1, gate1, shift2, scale2, gate2 = modulation(cond, W['w_mod'][l], W['b_mod'][l])
        u = x * (1 + scale1) + shift1
        if l % 2 == 0:
            names = ('diff_k', 'diff_v')
            ctx = None if caches is None else tuple(caches[nm][:, j] for nm in names)
            h, produced = ab_mixer(u, ctx, W, j, l)
        else:
            names = ('na_k', 'na_v', 'gqa_k', 'gqa_v')
            ctx = None if caches is None else tuple(caches[nm][:, j] for nm in names)
            h, produced = cd_mixer(u, ctx, W, j)
        for nm, t in zip(names, produced):
            made[nm].append(t)
        x = layer_norm(DEEPNORM_ALPHA * x + gate1 * h, W['ln1_g'][l], W['ln1_b'][l])
        f = moe_ffn(x * (1 + scale2) + shift2, W['router_w'][l], W['router_b'][l],
                    W['w_gate_up'][l], W['b_gate_up'][l], W['w_down'][l], W['b_down'][l])
        x = layer_norm(DEEPNORM_ALPHA * x + gate2 * f, W['ln2_g'][l], W['ln2_b'][l])
    stacked = tuple(jnp.stack(made[nm], axis=1) for nm in CACHE_NAMES) if caches is None else ()
    return x, stacked


def setup_inputs(seed: int = 0) -> dict:
    key = jax.random.key(seed)
    kit = iter(jax.random.split(key, 40))
    def nrm(shape, s):
        return jax.random.normal(next(kit), shape, jnp.float32) * s
    D = D_MODEL
    return {
        'x_prompt': nrm((BATCH, SEQ, D), 1.0),
        'x_sample': nrm((DEC_BATCH, DEC_SEQ, D), 1.0),
        'c': nrm((DEC_BATCH, D), 1.0),
        'c_ctx': nrm((D,), 1.0),
        'cache_diff_k': nrm((DEC_BATCH, N_EVEN, DIFF_HEADS, PAST_LEN, DIFF_QK), 1.0),
        'cache_diff_v': nrm((DEC_BATCH, N_EVEN, DIFF_HEADS, PAST_LEN, DIFF_V), 1.0),
        'cache_na_k': nrm((DEC_BATCH, N_ODD, NA_HEADS, PAST_LEN, HEAD_DIM), 1.0),
        'cache_na_v': nrm((DEC_BATCH, N_ODD, NA_HEADS, PAST_LEN, HEAD_DIM), 1.0),
        'cache_gqa_k': nrm((DEC_BATCH, N_ODD, GQA_KV_HEADS, PAST_LEN, HEAD_DIM), 1.0),
        'cache_gqa_v': nrm((DEC_BATCH, N_ODD, GQA_KV_HEADS, PAST_LEN, HEAD_DIM), 1.0),
        'w_mod': nrm((DEPTH, D, 6 * D), 0.5 * D ** -0.5),
        'b_mod': nrm((DEPTH, 6 * D), 0.02),
        'ln1_g': 1.0 + nrm((DEPTH, D), 0.02),
        'ln1_b': nrm((DEPTH, D), 0.02),
        'ln2_g': 1.0 + nrm((DEPTH, D), 0.02),
        'ln2_b': nrm((DEPTH, D), 0.02),
        'ab_w_in': nrm((N_EVEN, D, AB_IN), D ** -0.5),
        'ab_conv_w': nrm((N_EVEN, CONV_WIDTH, CONV_CH), CONV_WIDTH ** -0.5),
        'ab_lambda_q1': nrm((N_EVEN, HEAD_DIM), 0.1),
        'ab_lambda_k1': nrm((N_EVEN, HEAD_DIM), 0.1),
        'ab_lambda_q2': nrm((N_EVEN, HEAD_DIM), 0.1),
        'ab_lambda_k2': nrm((N_EVEN, HEAD_DIM), 0.1),
        'ab_subln_g': 1.0 + nrm((N_EVEN, DIFF_V), 0.02),
        'ab_w_out': nrm((N_EVEN, AB_OUT, D), DEEPNORM_BETA * AB_OUT ** -0.5),
        'cd_w_in': nrm((N_ODD, D, CD_IN), D ** -0.5),
        'cd_na_rpb': nrm((N_ODD, NA_HEADS, 2 * NA_WIN_R - 1, 2 * NA_WIN_C - 1), 0.1),
        'cd_q_norm_g': 1.0 + nrm((N_ODD, HEAD_DIM), 0.02),
        'cd_k_norm_g': 1.0 + nrm((N_ODD, HEAD_DIM), 0.02),
        'cd_w_out': nrm((N_ODD, CD_OUT, D), DEEPNORM_BETA * CD_OUT ** -0.5),
        'router_w': nrm((DEPTH, D, N_EXPERTS), D ** -0.5),
        'router_b': nrm((DEPTH, N_EXPERTS), 0.01),
        'w_gate_up': nrm((DEPTH, N_EXPERTS, D, 2 * D_FF), D ** -0.5),
        'b_gate_up': nrm((DEPTH, N_EXPERTS, 2 * D_FF), 0.01),
        'w_down': nrm((DEPTH, N_EXPERTS, D_FF, D), DEEPNORM_BETA * D_FF ** -0.5),
        'b_down': nrm((DEPTH, N_EXPERTS, D), 0.01),
    }


def reference(x_prompt, x_sample, c, c_ctx, cache_diff_k, cache_diff_v, cache_na_k, cache_na_v, cache_gqa_k, cache_gqa_v,
              w_mod, b_mod, ln1_g, ln1_b, ln2_g, ln2_b, ab_w_in, ab_conv_w, ab_lambda_q1, ab_lambda_k1, ab_lambda_q2,
              ab_lambda_k2, ab_subln_g, ab_w_out, cd_w_in, cd_na_rpb, cd_q_norm_g, cd_k_norm_g, cd_w_out,
              router_w, router_b, w_gate_up, b_gate_up, w_down, b_down):
    W = {'w_mod': w_mod, 'b_mod': b_mod, 'ln1_g': ln1_g, 'ln1_b': ln1_b, 'ln2_g': ln2_g, 'ln2_b': ln2_b,
         'ab_w_in': ab_w_in, 'ab_conv_w': ab_conv_w, 'ab_lambda_q1': ab_lambda_q1, 'ab_lambda_k1': ab_lambda_k1,
         'ab_lambda_q2': ab_lambda_q2, 'ab_lambda_k2': ab_lambda_k2, 'ab_subln_g': ab_subln_g, 'ab_w_out': ab_w_out,
         'cd_w_in': cd_w_in, 'cd_na_rpb': cd_na_rpb, 'cd_q_norm_g': cd_q_norm_g, 'cd_k_norm_g': cd_k_norm_g,
         'cd_w_out': cd_w_out, 'router_w': router_w, 'router_b': router_b, 'w_gate_up': w_gate_up,
         'b_gate_up': b_gate_up, 'w_down': w_down, 'b_down': b_down}
    y_prompt, produced = run_trunk(x_prompt, c_ctx[None, None, :], W, None)
    new_diff_k, new_diff_v, new_na_k, new_na_v, new_gqa_k, new_gqa_v = produced
    caches = {'diff_k': cache_diff_k, 'diff_v': cache_diff_v, 'na_k': cache_na_k, 'na_v': cache_na_v,
              'gqa_k': cache_gqa_k, 'gqa_v': cache_gqa_v}
    y_sample, _ = run_trunk(x_sample, c[:, None, :], W, caches)
    return (y_prompt, y_sample, new_diff_k, new_diff_v, new_na_k, new_na_v, new_gqa_k, new_gqa_v)
```

```python
import jax
import jax.numpy as jnp
from jax import lax
from jax.experimental import pallas as pl
from jax.experimental.pallas import tpu as pltpu

F32 = jnp.float32
BF16 = jnp.bfloat16

D = 1024
B_P, N_P = 16, 256
B_S, N_S = 2, 4096
PAST = 256
T_P, T_S = B_P * N_P, B_S * N_S
T = T_P + T_S
TM = 256
NT_P, NT_S, NT = T_P // TM, T_S // TM, T // TM
TILES_PER_GRID = N_S // TM
GRID_W = 64
GRID_H = N_S // GRID_W
HD = 64
DEPTH = 2
N_EXPERTS = 32
TOP_K = 4
D_FF = 1024
NA_WIN_R, NA_WIN_C = 8, 16
NA_QROWS = 4
NA_KROWS = 12
SWIGLU_LIMIT = 7.0
SWIGLU_ALPHA = 1.702
ROPE_THETA = 10000.0
DEEPNORM_ALPHA = (2 * DEPTH) ** 0.25
LN_EPS = 1e-5
RMS_EPS = 1e-6
QK_SCALE = HD ** -0.5
NEG = -1e30
MOE_TM = 512
LANES = 128

VMEM_LIMIT = 56 * 1024 * 1024


def _params(*sem):
    return pltpu.CompilerParams(dimension_semantics=sem, vmem_limit_bytes=VMEM_LIMIT)


def _split(x):
    hi = x.astype(BF16)
    lo = (x - hi.astype(F32)).astype(BF16)
    return hi, lo


def _dot(a, b):
    return jnp.dot(a, b, preferred_element_type=F32)


def _dot3(a, b):
    ah, al = _split(a)
    bh, bl = _split(b)
    return _dot(ah, bh) + (_dot(ah, bl) + _dot(al, bh))


def _cond_row(i):
    return jnp.where(i < NT_P, 0, 1 + (i - NT_P) // TILES_PER_GRID)


def _layer_norm(z, g, b):
    mu = jnp.mean(z, axis=-1, keepdims=True)
    zc = z - mu
    var = jnp.mean(zc * zc, axis=-1, keepdims=True)
    return zc * lax.rsqrt(var + LN_EPS) * g + b


def _low_half(rows):
    return lax.broadcasted_iota(jnp.int32, (rows, LANES), 1) < HD


def _mod_kernel(c_ref, w_ref, b_ref, o_ref):
    c = c_ref[...]
    o_ref[...] = _dot3(c * jax.nn.sigmoid(c), w_ref[...]) + b_ref[...]


def _modulation(cond8, w_mod, b_mod):
    return pl.pallas_call(
        _mod_kernel,
        grid=(DEPTH, 6),
        in_specs=[pl.BlockSpec((8, D), lambda l, j: (0, 0)),
                  pl.BlockSpec((None, D, D), lambda l, j: (l, 0, j)),
                  pl.BlockSpec((None, 1, D), lambda l, j: (l, 0, j))],
        out_specs=pl.BlockSpec((None, 8, D), lambda l, j: (l, 0, j)),
        out_shape=jax.ShapeDtypeStruct((DEPTH, 8, 6 * D), F32),
        compiler_params=_params("arbitrary", "arbitrary"),
        name="modulation",
    )(cond8, w_mod, b_mod.reshape(DEPTH, 1, 6 * D))


def _mod_spec(layer, chunk):
    return pl.BlockSpec((None, 8, D), lambda i, _l=layer, _c=chunk: (_l, 0, _c))


def _full(shape):
    return pl.BlockSpec(shape, lambda i: (0,) * len(shape))


def _rope_tables():
    t = jnp.arange(N_S)
    pos_r = (t // GRID_W).astype(F32)
    pos_c = (t % GRID_W).astype(F32)
    half = HD // 2
    inv = ROPE_THETA ** (-jnp.arange(0, half, 2, dtype=F32) / half)
    ang = jnp.concatenate([pos_r[:, None] * inv, pos_c[:, None] * inv], axis=-1)
    lane = jnp.arange(LANES)
    idx = (lane % HD) // 2
    cos = jnp.cos(ang)[:, idx]
    sin = jnp.sin(ang)[:, idx]
    even = (lane % 2) == 0
    return cos, jnp.where(even, -sin, 0.0), jnp.where(even, 0.0, sin)


def _rope(x, a, b, c):
    return x * a + pltpu.roll(x, LANES - 1, axis=1) * b + pltpu.roll(x, 1, axis=1) * c


def _rope_spec():
    return pl.BlockSpec((TM, LANES), lambda i: (jnp.maximum(i - NT_P, 0) % TILES_PER_GRID, 0))


def _x_specs():
    return [pl.BlockSpec((TM, D), lambda i: (jnp.minimum(i, NT_P - 1), 0)),
            pl.BlockSpec((TM, D), lambda i: (jnp.maximum(i - NT_P, 0), 0))]


def _cache_spec(heads, width):
    return pl.BlockSpec((None, None, heads, N_P, width), lambda i: (jnp.minimum(i, NT_P - 1), 0, 0, 0, 0))


def _row_spec(width):
    return pl.BlockSpec((TM, width), lambda i: (i, 0))


def _hm_spec(n):
    return pl.BlockSpec((n, TM, LANES), lambda i: (0, i, 0))


AB_Q0, AB_K0, AB_V0 = 1536, 2048, 2560


def _tile(p, col0, j):
    return p[:, col0 + j * LANES:col0 + (j + 1) * LANES]


def _ab_in_kernel(xp_ref, xs_ref, sh_ref, sc_ref, w_ref, ra_ref, rb_ref, rc_ref,
                  gb_ref, y_ref, q_ref, k_ref, v_ref, kc_ref, vc_ref):
    i = pl.program_id(0)
    is_p = i < NT_P
    ci = _cond_row(i)
    x = jnp.where(is_p, xp_ref[...], xs_ref[...])
    u = x * (1.0 + sc_ref[pl.ds(ci, 1), :]) + sh_ref[pl.ds(ci, 1), :]
    p = _dot(u.astype(BF16), w_ref[...])
    gb_ref[...] = p[:, 0:512]
    y_ref[...] = p[:, 512:1024] * p[:, 1024:1536]
    for h in range(4):
        v_ref[h] = _tile(p, AB_V0, h).astype(BF16)

    @pl.when(is_p)
    def _():
        for h in range(4):
            q_ref[h] = (_tile(p, AB_Q0, h) * QK_SCALE).astype(BF16)
            k_ref[h] = _tile(p, AB_K0, h).astype(BF16)
            kc_ref[h] = _tile(p, AB_K0, h)
            vc_ref[h] = _tile(p, AB_V0, h)

    @pl.when(jnp.logical_not(is_p))
    def _():
        a, b, c = ra_ref[...], rb_ref[...], rc_ref[...]
        for h in range(4):
            q_ref[h] = (_rope(_tile(p, AB_Q0, h), a, b, c) * QK_SCALE).astype(BF16)
            k_ref[h] = _rope(_tile(p, AB_K0, h), a, b, c).astype(BF16)


def _ab_in(xp, xs, mod, w_in, rope):
    hm = jax.ShapeDtypeStruct((4, T, LANES), BF16)
    cache = jax.ShapeDtypeStruct((B_P, 1, 4, N_P, LANES), F32)
    half = jax.ShapeDtypeStruct((T, 512), F32)
    return pl.pallas_call(
        _ab_in_kernel,
        grid=(NT,),
        in_specs=_x_specs() + [_mod_spec(0, 0), _mod_spec(0, 1), _full((D, 3072)),
                               _rope_spec(), _rope_spec(), _rope_spec()],
        out_specs=[_row_spec(512), _row_spec(512), _hm_spec(4), _hm_spec(4), _hm_spec(4),
                   _cache_spec(4, LANES), _cache_spec(4, LANES)],
        out_shape=[half, half, hm, hm, hm, cache, cache],
        compiler_params=_params("arbitrary"),
        name="ab_in_proj",
    )(xp, xs, mod, mod, w_in, *rope)


CD_NQ, CD_NK, CD_NV, CD_GQ, CD_GK, CD_GV = 0, 512, 1024, 1536, 2048, 2176


def _seg_mean64(s):
    r = lax.broadcasted_iota(jnp.int32, (LANES, LANES), 0) // HD
    c = lax.broadcasted_iota(jnp.int32, (LANES, LANES), 1) // HD
    seg = jnp.where(r == c, 1.0, 0.0).astype(BF16)
    hi, lo = _split(s)
    return (_dot(hi, seg) + _dot(lo, seg)) * (1.0 / HD)


def _rms64(x, g):
    return x * lax.rsqrt(_seg_mean64(x * x) + RMS_EPS) * g


def _dup_halves(x, lo):
    r = pltpu.roll(x, HD, axis=1)
    return jnp.where(lo, x, r), jnp.where(lo, r, x)


def _cd_in_kernel(x_ref, sh_ref, sc_ref, w_ref, qg_ref, kg_ref, ra_ref, rb_ref, rc_ref,
                  nq_ref, nk_ref, nv_ref, gq_ref, gk_ref, gv_ref,
                  nkc_ref, nvc_ref, gkc_ref, gvc_ref):
    i = pl.program_id(0)
    is_p = i < NT_P
    ci = _cond_row(i)
    u = x_ref[...] * (1.0 + sc_ref[pl.ds(ci, 1), :]) + sh_ref[pl.ds(ci, 1), :]
    p = _dot(u.astype(BF16), w_ref[...])
    lo = _low_half(TM)
    for j in range(4):
        nq_ref[j] = (_tile(p, CD_NQ, j) * QK_SCALE).astype(BF16)
        nk_ref[j] = _tile(p, CD_NK, j).astype(BF16)
        nv_ref[j] = _tile(p, CD_NV, j).astype(BF16)
    gq = [_rms64(_tile(p, CD_GQ, j), qg_ref[...]) for j in range(4)]
    gk = _rms64(_tile(p, CD_GK, 0), kg_ref[...])
    gv = _tile(p, CD_GV, 0)
    v0, v1 = _dup_halves(gv, lo)
    gv_ref[0] = v0.astype(BF16)
    gv_ref[1] = v1.astype(BF16)

    def emit(gq, gk):
        for j in range(4):
            gq_ref[j] = (gq[j] * QK_SCALE).astype(BF16)
        k0, k1 = _dup_halves(gk, lo)
        gk_ref[0] = k0.astype(BF16)
        gk_ref[1] = k1.astype(BF16)
        return k0, k1

    @pl.when(is_p)
    def _():
        k0, k1 = emit(gq, gk)
        gkc_ref[0] = k0[:, 0:HD]
        gkc_ref[1] = k1[:, 0:HD]
        gvc_ref[0] = v0[:, 0:HD]
        gvc_ref[1] = v1[:, 0:HD]
        for j in range(4):
            for src, dst in ((CD_NK, nkc_ref), (CD_NV, nvc_ref)):
                a, b = _dup_halves(_tile(p, src, j), lo)
                dst[2 * j] = a[:, 0:HD]
                dst[2 * j + 1] = b[:, 0:HD]

    @pl.when(jnp.logical_not(is_p))
    def _():
        a, b, c = ra_ref[...], rb_ref[...], rc_ref[...]
        emit([_rope(g, a, b, c) for g in gq], _rope(gk, a, b, c))


def _cd_in(x, mod, w_in, qg, kg, rope):
    hm4 = jax.ShapeDtypeStruct((4, T, LANES), BF16)
    hm2 = jax.ShapeDtypeStruct((2, T, LANES), BF16)
    c8 = jax.ShapeDtypeStruct((B_P, 1, 8, N_P, HD), F32)
    c2 = jax.ShapeDtypeStruct((B_P, 1, 2, N_P, HD), F32)
    return pl.pallas_call(
        _cd_in_kernel,
        grid=(NT,),
        in_specs=[_row_spec(D), _mod_spec(1, 0), _mod_spec(1, 1), _full((D, 2304)),
                  _full((1, LANES)), _full((1, LANES)), _rope_spec(), _rope_spec(), _rope_spec()],
        out_specs=[_hm_spec(4), _hm_spec(4), _hm_spec(4), _hm_spec(4), _hm_spec(2), _hm_spec(2),
                   _cache_spec(8, HD), _cache_spec(8, HD), _cache_spec(2, HD), _cache_spec(2, HD)],
        out_shape=[hm4, hm4, hm4, hm4, hm2, hm2, c8, c8, c2, c2],
        compiler_params=_params("arbitrary"),
        name="cd_in_proj",
    )(x, mod, mod, w_in, qg, kg, *rope)


def _stack_pairs(q_ref, n_q, tq):
    lo = _low_half(tq)
    parts = []
    for j in range(n_q):
        q = q_ref[j]
        zero = jnp.zeros_like(q)
        parts += [jnp.where(lo, q, zero), jnp.where(lo, zero, q)]
    return jnp.concatenate(parts, axis=0), lo


def _qk(qs, kb):
    return lax.dot_general(qs, kb, (((1,), (1,)), ((), ())), preferred_element_type=F32)


def _ctx_tile(ref, mode):
    if mode == "wide":
        x = ref[...]
    elif mode == "pair":
        x = jnp.concatenate([ref[0], ref[1]], axis=1)
    else:
        x = jnp.concatenate([ref[...], ref[...]], axis=1)
    return x.astype(BF16)


def _flash_pair_kernel(*refs, n_q, tq, nk, tk, ctx_mode, post, lam_init):
    it = iter(refs)
    q_ref, k_ref, v_ref = next(it), next(it), next(it)
    kc_ref, vc_ref = (next(it), next(it)) if ctx_mode else (None, None)
    lam_ref, g_ref = (next(it), next(it)) if post == "diff" else (None, None)
    o_ref = next(it)

    qs, lo = _stack_pairs(q_ref, n_q, tq)
    rows = 2 * n_q * tq

    def step(kb, vb, carry):
        m, l, acc = carry
        s = _qk(qs, kb)
        m_new = jnp.maximum(m, jnp.max(s, axis=1, keepdims=True))
        alpha = jnp.exp(m - m_new)
        p = jnp.exp(s - m_new)
        l = alpha * l + jnp.sum(p, axis=1, keepdims=True)
        acc = alpha * acc + _dot(p.astype(BF16), vb)
        return m_new, l, acc

    carry = (jnp.full((rows, 1), NEG, F32), jnp.zeros((rows, 1), F32), jnp.zeros((rows, LANES), F32))
    if nk == tk:
        carry = step(k_ref[...], v_ref[...], carry)
    else:
        def body(c, carry):
            start = pl.multiple_of(c * tk, tk)
            return step(k_ref[pl.ds(start, tk), :], v_ref[pl.ds(start, tk), :], carry)
        carry = lax.fori_loop(0, nk // tk, body, carry)
    if ctx_mode:
        carry = step(_ctx_tile(kc_ref, ctx_mode), _ctx_tile(vc_ref, ctx_mode), carry)
    _, l, acc = carry
    o = acc / l

    if post == "diff":
        lp = lam_ref[...]
        lam = (jnp.exp(jnp.sum(lp[0:1] * lp[1:2], axis=1, keepdims=True))
               - jnp.exp(jnp.sum(lp[2:3] * lp[3:4], axis=1, keepdims=True)) + lam_init)
        a = o[0:tq] - lam * o[tq:2 * tq]
        ms = jnp.mean(a * a, axis=-1, keepdims=True)
        o_ref[...] = (a * lax.rsqrt(ms + RMS_EPS) * g_ref[...] * (1.0 - lam_init)).astype(o_ref.dtype)
    else:
        for j in range(n_q):
            o_ref[:, j * LANES:(j + 1) * LANES] = jnp.where(
                lo, o[2 * j * tq:(2 * j + 1) * tq], o[(2 * j + 1) * tq:(2 * j + 2) * tq]).astype(o_ref.dtype)


def _flash_pair(q, k, v, *, latent, out_cols, col0, post, ctx=None, ctx_mode=None, diff=None,
                lam_init=0.0, prev=None):
    groups = k.shape[0]
    n_q = q.shape[0] // groups
    if latent:
        tq, nk, tk = TM, N_S, 512
        grid = (B_S, groups, TILES_PER_GRID)
        qrow = lambda b, g, i: NT_P + b * TILES_PER_GRID + i
        krow = lambda b, g, i: T_P // N_S + b
    else:
        tq, nk, tk = N_P, N_P, N_P
        grid = (B_P, groups, 1)
        qrow = lambda b, g, i: b
        krow = lambda b, g, i: b
    in_specs = [pl.BlockSpec((n_q, tq, LANES), lambda b, g, i: (g, qrow(b, g, i), 0)),
                pl.BlockSpec((None, nk, LANES), lambda b, g, i: (g, krow(b, g, i), 0)),
                pl.BlockSpec((None, nk, LANES), lambda b, g, i: (g, krow(b, g, i), 0))]
    args = [q, k, v]
    if ctx is not None:
        if ctx_mode == "wide":
            spec = pl.BlockSpec((None, None, None, PAST, LANES), lambda b, g, i: (b, 0, g, 0, 0))
        elif ctx_mode == "pair":
            spec = pl.BlockSpec((None, None, 2, PAST, HD), lambda b, g, i: (b, 0, g, 0, 0))
        else:
            spec = pl.BlockSpec((None, None, None, PAST, HD), lambda b, g, i: (b, 0, g, 0, 0))
        in_specs += [spec, spec]
        args += list(ctx)
    if diff is not None:
        in_specs += [pl.BlockSpec((4, HD), lambda b, g, i: (0, 0)),
                     pl.BlockSpec((1, LANES), lambda b, g, i: (0, 0))]
        args += list(diff)
    aliases = {}
    if prev is not None:
        aliases = {len(args): 0}
        in_specs.append(pl.BlockSpec(memory_space=pl.ANY))
        args.append(prev)

    def kernel(*refs):
        if prev is not None:
            refs = refs[:-2] + refs[-1:]
        _flash_pair_kernel(*refs, n_q=n_q, tq=tq, nk=nk, tk=tk, ctx_mode=ctx_mode if ctx is not None else None,
                           post=post, lam_init=lam_init)

    return pl.pallas_call(
        kernel,
        grid=grid,
        in_specs=in_specs,
        out_specs=pl.BlockSpec((tq, n_q * LANES), lambda b, g, i: (qrow(b, g, i), col0 + g)),
        out_shape=jax.ShapeDtypeStruct((T, out_cols), BF16),
        input_output_aliases=aliases,
        compiler_params=_params("arbitrary", "arbitrary", "arbitrary"),
        name="attn_" + post + ("_latent" if latent else "_context") + str(n_q),
    )(*args)


NA_TQ = NA_QROWS * GRID_W
NA_TK = NA_KROWS * GRID_W


def _na_bias(rpb):
    out = []
    qc = jnp.arange(GRID_W)
    cs = jnp.clip(qc - NA_WIN_C // 2, 0, GRID_W - NA_WIN_C)
    col_ok = (qc[None, :] >= cs[:, None]) & (qc[None, :] < cs[:, None] + NA_WIN_C)
    dc = jnp.clip(qc[None, :] - qc[:, None] + NA_WIN_C - 1, 0, 2 * NA_WIN_C - 2)
    for r0, k0 in ((0, 0), (NA_QROWS, 0), (GRID_H - NA_QROWS, GRID_H - NA_KROWS)):
        qr = r0 + jnp.arange(NA_QROWS)
        kr = k0 + jnp.arange(NA_KROWS)
        rs = jnp.clip(qr - NA_WIN_R // 2, 0, GRID_H - NA_WIN_R)
        row_ok = (kr[None, :] >= rs[:, None]) & (kr[None, :] < rs[:, None] + NA_WIN_R)
        dr = jnp.clip(kr[None, :] - qr[:, None] + NA_WIN_R - 1, 0, 2 * NA_WIN_R - 2)
        b = rpb[:, dr[:, None, :, None], dc[None, :, None, :]]
        ok = row_ok[:, None, :, None] & col_ok[None, :, None, :]
        b = jnp.where(ok[None], b, NEG)
        out.append(b.reshape(4, 2 * NA_TQ, NA_TK))
    return jnp.stack(out)


def _na_kernel(q_ref, k_ref, v_ref, kc_ref, vc_ref, bm_ref, _, o_ref):
    i = pl.program_id(2)
    qs, lo = _stack_pairs(q_ref, 1, NA_TQ)
    k0 = jnp.clip(i * NA_QROWS - NA_WIN_R // 2, 0, GRID_H - NA_KROWS)
    start = pl.multiple_of(k0 * GRID_W, GRID_W)
    kw = k_ref[pl.ds(start, NA_TK), :]
    vw = v_ref[pl.ds(start, NA_TK), :]
    s_w = _qk(qs, kw) + bm_ref[...]
    s_c = _qk(qs, _ctx_tile(kc_ref, "pair"))
    m = jnp.maximum(jnp.max(s_w, axis=1, keepdims=True), jnp.max(s_c, axis=1, keepdims=True))
    p_w = jnp.exp(s_w - m)
    p_c = jnp.exp(s_c - m)
    l = jnp.sum(p_w, axis=1, keepdims=True) + jnp.sum(p_c, axis=1, keepdims=True)
    o = (_dot(p_w.astype(BF16), vw) + _dot(p_c.astype(BF16), _ctx_tile(vc_ref, "pair"))) / l
    o_ref[...] = jnp.where(lo, o[0:NA_TQ], o[NA_TQ:2 * NA_TQ]).astype(o_ref.dtype)


def _na_latent(q, k, v, kc, vc, bm, prev):
    nblk = N_S // NA_TQ
    qrow = lambda b, g, i: T_P // NA_TQ + b * nblk + i
    krow = lambda b, g, i: T_P // N_S + b
    cfg = lambda i: jnp.where(i == 0, 0, jnp.where(i == nblk - 1, 2, 1))
    ctx_spec = pl.BlockSpec((None, None, 2, PAST, HD), lambda b, g, i: (b, 0, g, 0, 0))
    return pl.pallas_call(
        _na_kernel,
        grid=(B_S, 4, nblk),
        in_specs=[pl.BlockSpec((1, NA_TQ, LANES), lambda b, g, i: (g, qrow(b, g, i), 0)),
                  pl.BlockSpec((None, N_S, LANES), lambda b, g, i: (g, krow(b, g, i), 0)),
                  pl.BlockSpec((None, N_S, LANES), lambda b, g, i: (g, krow(b, g, i), 0)),
                  ctx_spec, ctx_spec,
                  pl.BlockSpec((None, None, 2 * NA_TQ, NA_TK), lambda b, g, i: (cfg(i), g, 0, 0)),
                  pl.BlockSpec(memory_space=pl.ANY)],
        out_specs=pl.BlockSpec((NA_TQ, LANES), lambda b, g, i: (qrow(b, g, i), g)),
        out_shape=jax.ShapeDtypeStruct((T, D), BF16),
        input_output_aliases={6: 0},
        compiler_params=_params("arbitrary", "arbitrary", "arbitrary"),
        name="attn_window_latent",
    )(q, k, v, kc, vc, bm, prev)


def _top4_weights(logits):
    lane = lax.broadcasted_iota(jnp.int32, logits.shape, 1).astype(F32)
    rest = logits
    tops, hots = [], []
    for _ in range(TOP_K):
        m = jnp.max(rest, axis=1, keepdims=True)
        first = jnp.min(jnp.where(rest == m, lane, float(LANES)), axis=1, keepdims=True)
        hot = lane == first
        tops.append(m)
        hots.append(hot)
        rest = jnp.where(hot, -jnp.inf, rest)
    es = [jnp.exp(m - tops[0]) for m in tops]
    denom = es[0] + es[1] + es[2] + es[3]
    w = jnp.zeros_like(logits)
    for e, hot in zip(es, hots):
        w = jnp.where(hot, e / denom, w)
    return w


def _mixer_tail(x, h, ci, g1_ref, sh2_ref, sc2_ref, lng_ref, lnb_ref, rw_ref, rb_ref, x1_ref, u2_ref, cw_ref):
    x1 = _layer_norm(DEEPNORM_ALPHA * x + g1_ref[pl.ds(ci, 1), :] * h, lng_ref[...], lnb_ref[...])
    x1_ref[...] = x1
    u2 = x1 * (1.0 + sc2_ref[pl.ds(ci, 1), :]) + sh2_ref[pl.ds(ci, 1), :]
    u2_ref[...] = u2.astype(u2_ref.dtype)
    cw_ref[...] = _top4_weights(_dot3(u2, rw_ref[...]) + rb_ref[...])


def _ab_out_kernel(xp_ref, xs_ref, gb_ref, y_ref, yp_ref, yn_ref, cv_ref, at_ref, wo_ref,
                   g1_ref, sh2_ref, sc2_ref, lng_ref, lnb_ref, rw_ref, rb_ref, x1_ref, u2_ref, cw_ref):
    i = pl.program_id(0)
    is_p = i < NT_P
    ci = _cond_row(i)
    x = jnp.where(is_p, xp_ref[...], xs_ref[...])
    j = (i - NT_P) % TILES_PER_GRID
    first = jnp.logical_or(is_p, j == 0)
    last = jnp.logical_or(is_p, j == TILES_PER_GRID - 1)
    y = y_ref[...]
    row = lax.broadcasted_iota(jnp.int32, y.shape, 0)
    before = jnp.where(first, 0.0, yp_ref[7:8, :])
    after = jnp.where(last, 0.0, yn_ref[0:1, :])
    y_prev = jnp.where(row == 0, before, pltpu.roll(y, 1, axis=0))
    y_next = jnp.where(row == TM - 1, after, pltpu.roll(y, TM - 1, axis=0))
    cv = cv_ref[...]
    conv = gb_ref[...] * (y_prev * cv[0:1] + y * cv[1:2] + y_next * cv[2:3])
    h = _dot(conv.astype(BF16), wo_ref[0:512, :]) + _dot(at_ref[...], wo_ref[512:1024, :])
    _mixer_tail(x, h, ci, g1_ref, sh2_ref, sc2_ref, lng_ref, lnb_ref, rw_ref, rb_ref, x1_ref, u2_ref, cw_ref)


def _cd_out_kernel(x_ref, mg_ref, wo_ref, g1_ref, sh2_ref, sc2_ref, lng_ref, lnb_ref, rw_ref, rb_ref,
                   x1_ref, u2_ref, cw_ref):
    ci = _cond_row(pl.program_id(0))
    h = _dot(mg_ref[...], wo_ref[...])
    _mixer_tail(x_ref[...], h, ci, g1_ref, sh2_ref, sc2_ref, lng_ref, lnb_ref, rw_ref, rb_ref,
                x1_ref, u2_ref, cw_ref)


def _tail_specs(layer):
    return [_mod_spec(layer, 2), _mod_spec(layer, 3), _mod_spec(layer, 4),
            _full((1, D)), _full((1, D)), _full((D, LANES)), _full((1, LANES))]


_TAIL_OUT_SPECS = [_row_spec(D), _row_spec(D), _row_spec(LANES)]
_TAIL_OUT_SHAPES = [jax.ShapeDtypeStruct((T, D), F32), jax.ShapeDtypeStruct((T, D), BF16),
                    jax.ShapeDtypeStruct((T, LANES), F32)]


def _ab_out(xp, xs, gb, y, conv_w, attn, w_out, mod, tail):
    halo_prev = pl.BlockSpec((8, 512), lambda i: (jnp.maximum(i * (TM // 8) - 1, 0), 0))
    halo_next = pl.BlockSpec((8, 512), lambda i: (jnp.minimum((i + 1) * (TM // 8), T // 8 - 1), 0))
    return pl.pallas_call(
        _ab_out_kernel,
        grid=(NT,),
        in_specs=_x_specs() + [_row_spec(512), _row_spec(512), halo_prev, halo_next, _full((3, 512)),
                               _row_spec(512), _full((D, D))] + _tail_specs(0),
        out_specs=_TAIL_OUT_SPECS,
        out_shape=_TAIL_OUT_SHAPES,
        compiler_params=_params("arbitrary"),
        name="ab_out_proj",
    )(xp, xs, gb, y, y, y, conv_w, attn, w_out, mod, mod, mod, *tail)


def _cd_out(x, merged, w_out, mod, tail):
    return pl.pallas_call(
        _cd_out_kernel,
        grid=(NT,),
        in_specs=[_row_spec(D), _row_spec(D), _full((D, D))] + _tail_specs(1),
        out_specs=_TAIL_OUT_SPECS,
        out_shape=_TAIL_OUT_SHAPES,
        compiler_params=_params("arbitrary"),
        name="cd_out_proj",
    )(x, merged, w_out, mod, mod, mod, *tail)


def _mix_down(wd_ref, wmix_ref):
    half = D_FF // 2
    for c in range(D // LANES):
        wmix_ref[c, pl.ds(0, half, stride=2), :] = wd_ref[0:half, c * LANES:(c + 1) * LANES]
        wmix_ref[c, pl.ds(1, half, stride=2), :] = wd_ref[half:D_FF, c * LANES:(c + 1) * LANES]


def _expert_ffn(u, wgu_ref, bgu_ref, wmix_ref, bd_ref):
    rows = u.shape[0]
    ga = _dot(u, wgu_ref[:, 0:D_FF].astype(BF16)) + bgu_ref[:, 0:D_FF]
    gb = _dot(u, wgu_ref[:, D_FF:2 * D_FF].astype(BF16)) + bgu_ref[:, D_FF:2 * D_FF]
    even = (lax.broadcasted_iota(jnp.int32, (rows, LANES), 1) % 2) == 0
    hid = []
    for c in range(D_FF // LANES):
        a = ga[:, c * LANES:(c + 1) * LANES]
        b = gb[:, c * LANES:(c + 1) * LANES]
        gate = jnp.where(even, a, pltpu.roll(b, 1, axis=1))
        up = jnp.where(even, pltpu.roll(a, LANES - 1, axis=1), b)
        gate = jnp.minimum(gate, SWIGLU_LIMIT)
        up = jnp.clip(up, -SWIGLU_LIMIT, SWIGLU_LIMIT)
        hid.append(((up + 1.0) * gate * jax.nn.sigmoid(SWIGLU_ALPHA * gate)).astype(BF16))
    hid = jnp.concatenate(hid, axis=1)
    wd = jnp.concatenate([wmix_ref[c] for c in range(D // LANES)], axis=1).astype(BF16)
    return _dot(hid, wd) + bd_ref[...]


def _moe_dense_kernel(u_ref, cw_ref, wgu_ref, bgu_ref, wd_ref, bd_ref, o_ref, wmix_ref):
    e = pl.program_id(1)

    @pl.when(e == 0)
    def _():
        o_ref[...] = jnp.zeros_like(o_ref)

    _mix_down(wd_ref, wmix_ref)
    y = _expert_ffn(u_ref[...], wgu_ref, bgu_ref, wmix_ref, bd_ref)
    lane = lax.broadcasted_iota(jnp.int32, cw_ref.shape, 1)
    w = jnp.sum(jnp.where(lane == e, cw_ref[...], 0.0), axis=1, keepdims=True)
    o_ref[...] += w * y


def _moe_dense(layer, u2, cw, w_gate_up, b_gate_up, w_down, b_down):
    return pl.pallas_call(
        _moe_dense_kernel,
        grid=(T // MOE_TM, N_EXPERTS),
        in_specs=[pl.BlockSpec((MOE_TM, D), lambda t, e: (t, 0)),
                  pl.BlockSpec((MOE_TM, LANES), lambda t, e: (t, 0)),
                  pl.BlockSpec((None, None, D, 2 * D_FF), lambda t, e: (layer, e, 0, 0)),
                  pl.BlockSpec((None, None, 1, 2 * D_FF), lambda t, e: (layer, e, 0, 0)),
                  pl.BlockSpec((None, None, D_FF, D), lambda t, e: (layer, e, 0, 0)),
                  pl.BlockSpec((None, None, 1, D), lambda t, e: (layer, e, 0, 0))],
        out_specs=pl.BlockSpec((MOE_TM, D), lambda t, e: (t, 0)),
        out_shape=jax.ShapeDtypeStruct((T, D), F32),
        scratch_shapes=[pltpu.VMEM((D // LANES, D_FF, LANES), F32)],
        compiler_params=_params("arbitrary", "arbitrary"),
        name="moe_dense",
    )(u2, cw, w_gate_up, b_gate_up.reshape(DEPTH, N_EXPERTS, 1, 2 * D_FF), w_down,
      b_down.reshape(DEPTH, N_EXPERTS, 1, D))


def _ln2_kernel(x1_ref, f_ref, g2_ref, lng_ref, lnb_ref, o_ref):
    ci = _cond_row(pl.program_id(0))
    z = DEEPNORM_ALPHA * x1_ref[...] + g2_ref[pl.ds(ci, 1), :] * f_ref[...]
    o_ref[...] = _layer_norm(z, lng_ref[...], lnb_ref[...])


def _ln2_final_kernel(x1_ref, f_ref, g2_ref, lng_ref, lnb_ref, op_ref, os_ref):
    i = pl.program_id(0)
    ci = _cond_row(i)
    z = DEEPNORM_ALPHA * x1_ref[...] + g2_ref[pl.ds(ci, 1), :] * f_ref[...]
    out = _layer_norm(z, lng_ref[...], lnb_ref[...])

    @pl.when(i < NT_P)
    def _():
        op_ref[...] = out

    @pl.when(i >= NT_P)
    def _():
        os_ref[...] = out


def _ln2(layer, x1, f, mod, g, b):
    in_specs = [_row_spec(D), _row_spec(D), _mod_spec(layer, 5), _full((1, D)), _full((1, D))]
    if layer < DEPTH - 1:
        return pl.pallas_call(
            _ln2_kernel, grid=(NT,), in_specs=in_specs, out_specs=_row_spec(D),
            out_shape=jax.ShapeDtypeStruct((T, D), F32),
            compiler_params=_params("arbitrary"), name="post_norm2",
        )(x1, f, mod, g, b)
    return pl.pallas_call(
        _ln2_final_kernel, grid=(NT,), in_specs=in_specs, out_specs=_x_specs(),
        out_shape=[jax.ShapeDtypeStruct((T_P, D), F32), jax.ShapeDtypeStruct((T_S, D), F32)],
        compiler_params=_params("arbitrary"), name="post_norm2_final",
    )(x1, f, mod, g, b)


def _router_tail(l, ln1_g, ln1_b, router_w, router_b):
    rw = jnp.pad(router_w[l], ((0, 0), (0, LANES - N_EXPERTS)))
    rb = jnp.pad(router_b[l], (0, LANES - N_EXPERTS), constant_values=NEG)
    return ln1_g[l][None], ln1_b[l][None], rw, rb[None]


def kernel(x_prompt, x_sample, c, c_ctx, cache_diff_k, cache_diff_v, cache_na_k, cache_na_v, cache_gqa_k, cache_gqa_v, w_mod, b_mod, ln1_g, ln1_b, ln2_g, ln2_b, ab_w_in, ab_conv_w, ab_lambda_q1, ab_lambda_k1, ab_lambda_q2, ab_lambda_k2, ab_subln_g, ab_w_out, cd_w_in, cd_na_rpb, cd_q_norm_g, cd_k_norm_g, cd_w_out, router_w, router_b, w_gate_up, b_gate_up, w_down, b_down):
    xp = x_prompt.reshape(T_P, D)
    xs = x_sample.reshape(T_S, D)
    cond8 = jnp.concatenate([c_ctx[None], c, jnp.zeros((8 - 1 - B_S, D), F32)], axis=0)
    mod = _modulation(cond8, w_mod, b_mod)
    rope = _rope_tables()
    experts = (w_gate_up, b_gate_up, w_down, b_down)

    gb, y, q, k, v, new_diff_k, new_diff_v = _ab_in(xp, xs, mod, ab_w_in[0].astype(BF16), rope)
    lam_init = 0.8 - 0.6 * 1.0
    diff = (jnp.stack([ab_lambda_q1[0], ab_lambda_k1[0], ab_lambda_q2[0], ab_lambda_k2[0]]), ab_subln_g[0][None])
    attn = _flash_pair(q, k, v, latent=False, out_cols=512, col0=0, post="diff", diff=diff, lam_init=lam_init)
    attn = _flash_pair(q, k, v, latent=True, out_cols=512, col0=0, post="diff", diff=diff, lam_init=lam_init,
                       ctx=(cache_diff_k, cache_diff_v), ctx_mode="wide", prev=attn)
    x1, u2, cw = _ab_out(xp, xs, gb, y, ab_conv_w[0], attn, ab_w_out[0].astype(BF16), mod,
                         _router_tail(0, ln1_g, ln1_b, router_w, router_b))
    f = _moe_dense(0, u2, cw, *experts)
    x = _ln2(0, x1, f, mod, ln2_g[0][None], ln2_b[0][None])

    qg = jnp.tile(cd_q_norm_g[0], 2)[None]
    kg = jnp.tile(cd_k_norm_g[0], 2)[None]
    nq, nk, nv, gq, gk, gv, new_na_k, new_na_v, new_gqa_k, new_gqa_v = _cd_in(
        x, mod, cd_w_in[0].astype(BF16), qg, kg, rope)
    merged = _flash_pair(nq, nk, nv, latent=False, out_cols=D, col0=0, post="select")
    merged = _flash_pair(gq, gk, gv, latent=False, out_cols=D, col0=2, post="select", prev=merged)
    merged = _na_latent(nq, nk, nv, cache_na_k, cache_na_v, _na_bias(cd_na_rpb[0]), merged)
    merged = _flash_pair(gq, gk, gv, latent=True, out_cols=D, col0=2, post="select",
                         ctx=(cache_gqa_k, cache_gqa_v), ctx_mode="dup", prev=merged)
    x1, u2, cw = _cd_out(x, merged, cd_w_out[0].astype(BF16), mod,
                         _router_tail(1, ln1_g, ln1_b, router_w, router_b))
    f = _moe_dense(1, u2, cw, *experts)
    y_p, y_s = _ln2(1, x1, f, mod, ln2_g[1][None], ln2_b[1][None])

    return (y_p.reshape(B_P, N_P, D), y_s.reshape(B_S, N_S, D), new_diff_k, new_diff_v,
            new_na_k, new_na_v, new_gqa_k, new_gqa_v)
```

```python
import jax
import jax.numpy as jnp
from jax import lax
from jax.experimental import pallas as pl
from jax.experimental.pallas import tpu as pltpu

F32 = jnp.float32
BF16 = jnp.bfloat16

D = 1024
B_P, N_P = 16, 256
B_S, N_S = 2, 4096
PAST = 256
T_P, T_S = B_P * N_P, B_S * N_S
T = T_P + T_S
TM = 256
NT_P, NT_S, NT = T_P // TM, T_S // TM, T // TM
TILES_PER_GRID = N_S // TM
GRID_W = 64
GRID_H = N_S // GRID_W
HD = 64
DEPTH = 2
N_EXPERTS = 32
TOP_K = 4
D_FF = 1024
NA_WIN_R, NA_WIN_C = 8, 16
NA_QROWS = 4
NA_KROWS = 12
SWIGLU_LIMIT = 7.0
SWIGLU_ALPHA = 1.702
ROPE_THETA = 10000.0
DEEPNORM_ALPHA = (2 * DEPTH) ** 0.25
LN_EPS = 1e-5
RMS_EPS = 1e-6
QK_SCALE = HD ** -0.5
NEG = -1e30
MOE_TM = 256
MOE_PAIRS = TOP_K * T
MOE_TILES = MOE_PAIRS // MOE_TM + N_EXPERTS
LANES = 128

VMEM_LIMIT = 56 * 1024 * 1024


def _params(*sem):
    return pltpu.CompilerParams(dimension_semantics=sem, vmem_limit_bytes=VMEM_LIMIT)


def _split(x):
    hi = x.astype(BF16)
    lo = (x - hi.astype(F32)).astype(BF16)
    return hi, lo


def _dot(a, b):
    return jnp.dot(a, b, preferred_element_type=F32)


def _dot3(a, b):
    ah, al = _split(a)
    bh, bl = _split(b)
    return _dot(ah, bh) + (_dot(ah, bl) + _dot(al, bh))


def _cond_row(i):
    return jnp.where(i < NT_P, 0, 1 + (i - NT_P) // TILES_PER_GRID)


def _layer_norm(z, g, b):
    mu = jnp.mean(z, axis=-1, keepdims=True)
    zc = z - mu
    var = jnp.mean(zc * zc, axis=-1, keepdims=True)
    return zc * lax.rsqrt(var + LN_EPS) * g + b


def _low_half(rows):
    return lax.broadcasted_iota(jnp.int32, (rows, LANES), 1) < HD


def _mod_kernel(c_ref, w_ref, b_ref, o_ref):
    c = c_ref[...]
    o_ref[...] = _dot3(c * jax.nn.sigmoid(c), w_ref[...]) + b_ref[...]


def _modulation(cond8, w_mod, b_mod):
    return pl.pallas_call(
        _mod_kernel,
        grid=(DEPTH, 6),
        in_specs=[pl.BlockSpec((8, D), lambda l, j: (0, 0)),
                  pl.BlockSpec((None, D, D), lambda l, j: (l, 0, j)),
                  pl.BlockSpec((None, 1, D), lambda l, j: (l, 0, j))],
        out_specs=pl.BlockSpec((None, 8, D), lambda l, j: (l, 0, j)),
        out_shape=jax.ShapeDtypeStruct((DEPTH, 8, 6 * D), F32),
        compiler_params=_params("arbitrary", "arbitrary"),
        name="modulation",
    )(cond8, w_mod, b_mod.reshape(DEPTH, 1, 6 * D))


def _mod_spec(layer, chunk):
    return pl.BlockSpec((None, 8, D), lambda i, _l=layer, _c=chunk: (_l, 0, _c))


def _full(shape):
    return pl.BlockSpec(shape, lambda i: (0,) * len(shape))


def _rope_tables():
    t = jnp.arange(N_S)
    pos_r = (t // GRID_W).astype(F32)
    pos_c = (t % GRID_W).astype(F32)
    half = HD // 2
    inv = ROPE_THETA ** (-jnp.arange(0, half, 2, dtype=F32) / half)
    inv_lane = jnp.tile(jnp.repeat(inv, 2), 2 * LANES // HD)
    lane = jnp.arange(LANES)
    by_row = (lane % HD) < half
    ang = jnp.where(by_row[None], pos_r[:, None], pos_c[:, None]) * inv_lane[None]
    cos = jnp.cos(ang)
    sin = jnp.sin(ang)
    even = (lane % 2) == 0
    return cos, jnp.where(even, -sin, 0.0), jnp.where(even, 0.0, sin)


def _rope(x, a, b, c):
    return x * a + pltpu.roll(x, LANES - 1, axis=1) * b + pltpu.roll(x, 1, axis=1) * c


def _rope_spec():
    return pl.BlockSpec((TM, LANES), lambda i: (jnp.maximum(i - NT_P, 0) % TILES_PER_GRID, 0))


def _x_specs():
    return [pl.BlockSpec((TM, D), lambda i: (jnp.minimum(i, NT_P - 1), 0)),
            pl.BlockSpec((TM, D), lambda i: (jnp.maximum(i - NT_P, 0), 0))]


def _cache_spec(heads, width):
    return pl.BlockSpec((None, None, heads, N_P, width), lambda i: (jnp.minimum(i, NT_P - 1), 0, 0, 0, 0))


def _row_spec(width):
    return pl.BlockSpec((TM, width), lambda i: (i, 0))


def _hm_spec(n):
    return pl.BlockSpec((n, TM, LANES), lambda i: (0, i, 0))


AB_Q0, AB_K0, AB_V0 = 1536, 2048, 2560


def _tile(p, col0, j):
    return p[:, col0 + j * LANES:col0 + (j + 1) * LANES]


def _ab_in_kernel(xp_ref, xs_ref, sh_ref, sc_ref, w_ref, ra_ref, rb_ref, rc_ref,
                  gb_ref, y_ref, q_ref, k_ref, v_ref, kc_ref, vc_ref):
    i = pl.program_id(0)
    is_p = i < NT_P
    ci = _cond_row(i)
    x = jnp.where(is_p, xp_ref[...], xs_ref[...])
    u = x * (1.0 + sc_ref[pl.ds(ci, 1), :]) + sh_ref[pl.ds(ci, 1), :]
    p = _dot(u.astype(BF16), w_ref[...])
    gb_ref[...] = p[:, 0:512]
    y_ref[...] = p[:, 512:1024] * p[:, 1024:1536]
    for h in range(4):
        v_ref[h] = _tile(p, AB_V0, h).astype(BF16)

    @pl.when(is_p)
    def _():
        for h in range(4):
            q_ref[h] = (_tile(p, AB_Q0, h) * QK_SCALE).astype(BF16)
            k_ref[h] = _tile(p, AB_K0, h).astype(BF16)
            kc_ref[h] = _tile(p, AB_K0, h)
            vc_ref[h] = _tile(p, AB_V0, h)

    @pl.when(jnp.logical_not(is_p))
    def _():
        a, b, c = ra_ref[...], rb_ref[...], rc_ref[...]
        for h in range(4):
            q_ref[h] = (_rope(_tile(p, AB_Q0, h), a, b, c) * QK_SCALE).astype(BF16)
            k_ref[h] = _rope(_tile(p, AB_K0, h), a, b, c).astype(BF16)


def _ab_in(xp, xs, mod, w_in, rope):
    hm = jax.ShapeDtypeStruct((4, T, LANES), BF16)
    cache = jax.ShapeDtypeStruct((B_P, 1, 4, N_P, LANES), F32)
    half = jax.ShapeDtypeStruct((T, 512), F32)
    return pl.pallas_call(
        _ab_in_kernel,
        grid=(NT,),
        in_specs=_x_specs() + [_mod_spec(0, 0), _mod_spec(0, 1), _full((D, 3072)),
                               _rope_spec(), _rope_spec(), _rope_spec()],
        out_specs=[_row_spec(512), _row_spec(512), _hm_spec(4), _hm_spec(4), _hm_spec(4),
                   _cache_spec(4, LANES), _cache_spec(4, LANES)],
        out_shape=[half, half, hm, hm, hm, cache, cache],
        compiler_params=_params("arbitrary"),
        name="ab_in_proj",
    )(xp, xs, mod, mod, w_in, *rope)


CD_NQ, CD_NK, CD_NV, CD_GQ, CD_GK, CD_GV = 0, 512, 1024, 1536, 2048, 2176


def _seg_mean64(s):
    r = lax.broadcasted_iota(jnp.int32, (LANES, LANES), 0) // HD
    c = lax.broadcasted_iota(jnp.int32, (LANES, LANES), 1) // HD
    seg = jnp.where(r == c, 1.0, 0.0).astype(BF16)
    hi, lo = _split(s)
    return (_dot(hi, seg) + _dot(lo, seg)) * (1.0 / HD)


def _rms64(x, g):
    return x * lax.rsqrt(_seg_mean64(x * x) + RMS_EPS) * g


def _dup_halves(x, lo):
    r = pltpu.roll(x, HD, axis=1)
    return jnp.where(lo, x, r), jnp.where(lo, r, x)


def _cd_in_kernel(x_ref, sh_ref, sc_ref, w_ref, qg_ref, kg_ref, ra_ref, rb_ref, rc_ref,
                  nq_ref, nk_ref, nv_ref, gq_ref, gk_ref, gv_ref,
                  nkc_ref, nvc_ref, gkc_ref, gvc_ref):
    i = pl.program_id(0)
    is_p = i < NT_P
    ci = _cond_row(i)
    u = x_ref[...] * (1.0 + sc_ref[pl.ds(ci, 1), :]) + sh_ref[pl.ds(ci, 1), :]
    p = _dot(u.astype(BF16), w_ref[...])
    lo = _low_half(TM)
    for j in range(4):
        nq_ref[j] = (_tile(p, CD_NQ, j) * QK_SCALE).astype(BF16)
        nk_ref[j] = _tile(p, CD_NK, j).astype(BF16)
        nv_ref[j] = _tile(p, CD_NV, j).astype(BF16)
    gq = [_rms64(_tile(p, CD_GQ, j), qg_ref[...]) for j in range(4)]
    gk = _rms64(_tile(p, CD_GK, 0), kg_ref[...])
    gv = _tile(p, CD_GV, 0)
    v0, v1 = _dup_halves(gv, lo)
    gv_ref[0] = v0.astype(BF16)
    gv_ref[1] = v1.astype(BF16)

    def emit(gq, gk):
        for j in range(4):
            gq_ref[j] = (gq[j] * QK_SCALE).astype(BF16)
        k0, k1 = _dup_halves(gk, lo)
        gk_ref[0] = k0.astype(BF16)
        gk_ref[1] = k1.astype(BF16)
        return k0, k1

    @pl.when(is_p)
    def _():
        k0, k1 = emit(gq, gk)
        gkc_ref[0] = k0[:, 0:HD]
        gkc_ref[1] = k1[:, 0:HD]
        gvc_ref[0] = v0[:, 0:HD]
        gvc_ref[1] = v1[:, 0:HD]
        for j in range(4):
            for src, dst in ((CD_NK, nkc_ref), (CD_NV, nvc_ref)):
                a, b = _dup_halves(_tile(p, src, j), lo)
                dst[2 * j] = a[:, 0:HD]
                dst[2 * j + 1] = b[:, 0:HD]

    @pl.when(jnp.logical_not(is_p))
    def _():
        a, b, c = ra_ref[...], rb_ref[...], rc_ref[...]
        emit([_rope(g, a, b, c) for g in gq], _rope(gk, a, b, c))


def _cd_in(x, mod, w_in, qg, kg, rope):
    hm4 = jax.ShapeDtypeStruct((4, T, LANES), BF16)
    hm2 = jax.ShapeDtypeStruct((2, T, LANES), BF16)
    c8 = jax.ShapeDtypeStruct((B_P, 1, 8, N_P, HD), F32)
    c2 = jax.ShapeDtypeStruct((B_P, 1, 2, N_P, HD), F32)
    return pl.pallas_call(
        _cd_in_kernel,
        grid=(NT,),
        in_specs=[_row_spec(D), _mod_spec(1, 0), _mod_spec(1, 1), _full((D, 2304)),
                  _full((1, LANES)), _full((1, LANES)), _rope_spec(), _rope_spec(), _rope_spec()],
        out_specs=[_hm_spec(4), _hm_spec(4), _hm_spec(4), _hm_spec(4), _hm_spec(2), _hm_spec(2),
                   _cache_spec(8, HD), _cache_spec(8, HD), _cache_spec(2, HD), _cache_spec(2, HD)],
        out_shape=[hm4, hm4, hm4, hm4, hm2, hm2, c8, c8, c2, c2],
        compiler_params=_params("arbitrary"),
        name="cd_in_proj",
    )(x, mod, mod, w_in, qg, kg, *rope)


def _stack_pairs(q_ref, n_q, tq):
    lo = _low_half(tq)
    parts = []
    for j in range(n_q):
        q = q_ref[j]
        zero = jnp.zeros_like(q)
        parts += [jnp.where(lo, q, zero), jnp.where(lo, zero, q)]
    return jnp.concatenate(parts, axis=0), lo


def _qk(qs, kb):
    return lax.dot_general(qs, kb, (((1,), (1,)), ((), ())), preferred_element_type=F32)


def _ctx_tile(ref, mode):
    if mode == "wide":
        x = ref[...]
    elif mode == "pair":
        x = jnp.concatenate([ref[0], ref[1]], axis=1)
    else:
        x = jnp.concatenate([ref[...], ref[...]], axis=1)
    return x.astype(BF16)


def _flash_pair_kernel(*refs, n_q, tq, nk, tk, ctx_mode, post, lam_init):
    it = iter(refs)
    q_ref, k_ref, v_ref = next(it), next(it), next(it)
    kc_ref, vc_ref = (next(it), next(it)) if ctx_mode else (None, None)
    lam_ref, g_ref = (next(it), next(it)) if post == "diff" else (None, None)
    o_ref = next(it)

    qs, lo = _stack_pairs(q_ref, n_q, tq)
    rows = 2 * n_q * tq

    def step(kb, vb, carry):
        m, l, acc = carry
        s = _qk(qs, kb)
        m_new = jnp.maximum(m, jnp.max(s, axis=1, keepdims=True))
        alpha = jnp.exp(m - m_new)
        p = jnp.exp(s - m_new)
        l = alpha * l + jnp.sum(p, axis=1, keepdims=True)
        acc = alpha * acc + _dot(p.astype(BF16), vb)
        return m_new, l, acc

    carry = (jnp.full((rows, 1), NEG, F32), jnp.zeros((rows, 1), F32), jnp.zeros((rows, LANES), F32))
    if nk == tk:
        carry = step(k_ref[...], v_ref[...], carry)
    else:
        def body(c, carry):
            start = pl.multiple_of(c * tk, tk)
            return step(k_ref[pl.ds(start, tk), :], v_ref[pl.ds(start, tk), :], carry)
        carry = lax.fori_loop(0, nk // tk, body, carry)
    if ctx_mode:
        carry = step(_ctx_tile(kc_ref, ctx_mode), _ctx_tile(vc_ref, ctx_mode), carry)
    _, l, acc = carry
    o = acc / l

    if post == "diff":
        lp = lam_ref[...]
        lam = (jnp.exp(jnp.sum(lp[0:1] * lp[1:2], axis=1, keepdims=True))
               - jnp.exp(jnp.sum(lp[2:3] * lp[3:4], axis=1, keepdims=True)) + lam_init)
        a = o[0:tq] - lam * o[tq:2 * tq]
        ms = jnp.mean(a * a, axis=-1, keepdims=True)
        o_ref[...] = (a * lax.rsqrt(ms + RMS_EPS) * g_ref[...] * (1.0 - lam_init)).astype(o_ref.dtype)
    else:
        for j in range(n_q):
            o_ref[:, j * LANES:(j + 1) * LANES] = jnp.where(
                lo, o[2 * j * tq:(2 * j + 1) * tq], o[(2 * j + 1) * tq:(2 * j + 2) * tq]).astype(o_ref.dtype)


def _flash_pair(q, k, v, *, latent, out_cols, col0, post, ctx=None, ctx_mode=None, diff=None,
                lam_init=0.0, prev=None):
    groups = k.shape[0]
    n_q = q.shape[0] // groups
    if latent:
        tq, nk, tk = TM, N_S, 512
        grid = (B_S, groups, TILES_PER_GRID)
        qrow = lambda b, g, i: NT_P + b * TILES_PER_GRID + i
        krow = lambda b, g, i: T_P // N_S + b
    else:
        tq, nk, tk = N_P, N_P, N_P
        grid = (B_P, groups, 1)
        qrow = lambda b, g, i: b
        krow = lambda b, g, i: b
    in_specs = [pl.BlockSpec((n_q, tq, LANES), lambda b, g, i: (g, qrow(b, g, i), 0)),
                pl.BlockSpec((None, nk, LANES), lambda b, g, i: (g, krow(b, g, i), 0)),
                pl.BlockSpec((None, nk, LANES), lambda b, g, i: (g, krow(b, g, i), 0))]
    args = [q, k, v]
    if ctx is not None:
        if ctx_mode == "wide":
            spec = pl.BlockSpec((None, None, None, PAST, LANES), lambda b, g, i: (b, 0, g, 0, 0))
        elif ctx_mode == "pair":
            spec = pl.BlockSpec((None, None, 2, PAST, HD), lambda b, g, i: (b, 0, g, 0, 0))
        else:
            spec = pl.BlockSpec((None, None, None, PAST, HD), lambda b, g, i: (b, 0, g, 0, 0))
        in_specs += [spec, spec]
        args += list(ctx)
    if diff is not None:
        in_specs += [pl.BlockSpec((4, HD), lambda b, g, i: (0, 0)),
                     pl.BlockSpec((1, LANES), lambda b, g, i: (0, 0))]
        args += list(diff)
    aliases = {}
    if prev is not None:
        aliases = {len(args): 0}
        in_specs.append(pl.BlockSpec(memory_space=pl.ANY))
        args.append(prev)

    def kernel(*refs):
        if prev is not None:
            refs = refs[:-2] + refs[-1:]
        _flash_pair_kernel(*refs, n_q=n_q, tq=tq, nk=nk, tk=tk, ctx_mode=ctx_mode if ctx is not None else None,
                           post=post, lam_init=lam_init)

    return pl.pallas_call(
        kernel,
        grid=grid,
        in_specs=in_specs,
        out_specs=pl.BlockSpec((tq, n_q * LANES), lambda b, g, i: (qrow(b, g, i), col0 + g)),
        out_shape=jax.ShapeDtypeStruct((T, out_cols), BF16),
        input_output_aliases=aliases,
        compiler_params=_params("arbitrary", "arbitrary", "arbitrary"),
        name="attn_" + post + ("_latent" if latent else "_context") + str(n_q),
    )(*args)


NA_TQ = NA_QROWS * GRID_W
NA_TK = NA_KROWS * GRID_W


NA_BLOCK_POS = ((0, 0), (NA_QROWS, 0), (GRID_H - NA_QROWS, GRID_H - NA_KROWS))
N_DR = 2 * NA_WIN_R - 1
N_DC = 2 * NA_WIN_C - 1


def _na_bias_kernel(rpb_ref, o_ref):
    qc = lax.broadcasted_iota(jnp.int32, (GRID_W, LANES), 0)
    lane = lax.broadcasted_iota(jnp.int32, (GRID_W, LANES), 1)
    kc = lane % GRID_W
    cs = jnp.clip(qc - NA_WIN_C // 2, 0, GRID_W - NA_WIN_C)
    col_ok = jnp.logical_and(kc >= cs, kc < cs + NA_WIN_C)
    lo = lane < GRID_W
    neg = jnp.full((GRID_W, LANES), NEG, F32)
    for head in range(2):
        toeplitz = []
        for dr in range(N_DR):
            r = jnp.broadcast_to(rpb_ref[head, dr:dr + 1, :], (GRID_W, LANES))
            t = jnp.where(lo, pltpu.roll(r, LANES - (NA_WIN_C - 1), axis=1, stride=1, stride_axis=0),
                          pltpu.roll(r, GRID_W - (NA_WIN_C - 1), axis=1, stride=1, stride_axis=0))
            toeplitz.append(jnp.where(col_ok, t, neg))
        for pos, (r0, k0) in enumerate(NA_BLOCK_POS):
            for i in range(NA_QROWS):
                qr = r0 + i
                rs = min(max(qr - NA_WIN_R // 2, 0), GRID_H - NA_WIN_R)
                tiles = [toeplitz[k0 + j - qr + NA_WIN_R - 1] if rs <= k0 + j < rs + NA_WIN_R else neg
                         for j in range(NA_KROWS)]
                for jp in range(NA_KROWS // 2):
                    o_ref[pos, pl.ds(head * NA_TQ + i * GRID_W, GRID_W), pl.ds(jp * LANES, LANES)] = jnp.where(
                        lo, tiles[2 * jp], tiles[2 * jp + 1])


def _na_bias(rpb):
    rpb_pad = jnp.pad(rpb, ((0, 0), (0, 16 - N_DR), (0, LANES - N_DC)), constant_values=NEG)
    return pl.pallas_call(
        _na_bias_kernel,
        grid=(4,),
        in_specs=[pl.BlockSpec((2, 16, LANES), lambda g: (g, 0, 0))],
        out_specs=pl.BlockSpec((3, None, 2 * NA_TQ, NA_TK), lambda g: (0, g, 0, 0)),
        out_shape=jax.ShapeDtypeStruct((3, 4, 2 * NA_TQ, NA_TK), F32),
        compiler_params=_params("arbitrary"),
        name="na_bias_table",
    )(rpb_pad)


def _na_kernel(q_ref, k_ref, v_ref, kc_ref, vc_ref, bm_ref, _, o_ref):
    i = pl.program_id(2)
    qs, lo = _stack_pairs(q_ref, 1, NA_TQ)
    k0 = jnp.clip(i * NA_QROWS - NA_WIN_R // 2, 0, GRID_H - NA_KROWS)
    start = pl.multiple_of(k0 * GRID_W, GRID_W)
    kw = k_ref[pl.ds(start, NA_TK), :]
    vw = v_ref[pl.ds(start, NA_TK), :]
    s_w = _qk(qs, kw) + bm_ref[...]
    s_c = _qk(qs, _ctx_tile(kc_ref, "pair"))
    m = jnp.maximum(jnp.max(s_w, axis=1, keepdims=True), jnp.max(s_c, axis=1, keepdims=True))
    p_w = jnp.exp(s_w - m)
    p_c = jnp.exp(s_c - m)
    l = jnp.sum(p_w, axis=1, keepdims=True) + jnp.sum(p_c, axis=1, keepdims=True)
    o = (_dot(p_w.astype(BF16), vw) + _dot(p_c.astype(BF16), _ctx_tile(vc_ref, "pair"))) / l
    o_ref[...] = jnp.where(lo, o[0:NA_TQ], o[NA_TQ:2 * NA_TQ]).astype(o_ref.dtype)


def _na_latent(q, k, v, kc, vc, bm, prev):
    nblk = N_S // NA_TQ
    qrow = lambda b, g, i: T_P // NA_TQ + b * nblk + i
    krow = lambda b, g, i: T_P // N_S + b
    cfg = lambda i: jnp.where(i == 0, 0, jnp.where(i == nblk - 1, 2, 1))
    ctx_spec = pl.BlockSpec((None, None, 2, PAST, HD), lambda b, g, i: (b, 0, g, 0, 0))
    return pl.pallas_call(
        _na_kernel,
        grid=(B_S, 4, nblk),
        in_specs=[pl.BlockSpec((1, NA_TQ, LANES), lambda b, g, i: (g, qrow(b, g, i), 0)),
                  pl.BlockSpec((None, N_S, LANES), lambda b, g, i: (g, krow(b, g, i), 0)),
                  pl.BlockSpec((None, N_S, LANES), lambda b, g, i: (g, krow(b, g, i), 0)),
                  ctx_spec, ctx_spec,
                  pl.BlockSpec((None, None, 2 * NA_TQ, NA_TK), lambda b, g, i: (cfg(i), g, 0, 0)),
                  pl.BlockSpec(memory_space=pl.ANY)],
        out_specs=pl.BlockSpec((NA_TQ, LANES), lambda b, g, i: (qrow(b, g, i), g)),
        out_shape=jax.ShapeDtypeStruct((T, D), BF16),
        input_output_aliases={6: 0},
        compiler_params=_params("arbitrary", "arbitrary", "arbitrary"),
        name="attn_window_latent",
    )(q, k, v, kc, vc, bm, prev)


def _top4(logits):
    lane = lax.broadcasted_iota(jnp.int32, logits.shape, 1).astype(F32)
    rest = logits
    tops, firsts = [], []
    for _ in range(TOP_K):
        m = jnp.max(rest, axis=1, keepdims=True)
        first = jnp.min(jnp.where(rest == m, lane, float(LANES)), axis=1, keepdims=True)
        tops.append(m)
        firsts.append(first)
        rest = jnp.where(lane == first, -jnp.inf, rest)
    es = [jnp.exp(m - tops[0]) for m in tops]
    denom = es[0] + es[1] + es[2] + es[3]
    idx = jnp.zeros_like(logits)
    w = jnp.zeros_like(logits)
    for k in range(TOP_K):
        idx = jnp.where(lane == float(k), firsts[k], idx)
        w = jnp.where(lane == float(k), es[k] / denom, w)
    return idx.astype(jnp.int32), w


def _mixer_tail(x, h, ci, g1_ref, sh2_ref, sc2_ref, lng_ref, lnb_ref, rw_ref, rb_ref,
                x1_ref, u2_ref, idx_ref, w_ref):
    x1 = _layer_norm(DEEPNORM_ALPHA * x + g1_ref[pl.ds(ci, 1), :] * h, lng_ref[...], lnb_ref[...])
    x1_ref[...] = x1
    u2 = x1 * (1.0 + sc2_ref[pl.ds(ci, 1), :]) + sh2_ref[pl.ds(ci, 1), :]
    u2_ref[...] = u2
    idx_ref[...], w_ref[...] = _top4(_dot3(u2, rw_ref[...]) + rb_ref[...])


def _ab_out_kernel(xp_ref, xs_ref, gb_ref, y_ref, yp_ref, yn_ref, cv_ref, at_ref, wo_ref,
                   g1_ref, sh2_ref, sc2_ref, lng_ref, lnb_ref, rw_ref, rb_ref, *out_refs):
    i = pl.program_id(0)
    is_p = i < NT_P
    ci = _cond_row(i)
    x = jnp.where(is_p, xp_ref[...], xs_ref[...])
    j = (i - NT_P) % TILES_PER_GRID
    first = jnp.logical_or(is_p, j == 0)
    last = jnp.logical_or(is_p, j == TILES_PER_GRID - 1)
    y = y_ref[...]
    row = lax.broadcasted_iota(jnp.int32, y.shape, 0)
    before = jnp.where(first, 0.0, yp_ref[7:8, :])
    after = jnp.where(last, 0.0, yn_ref[0:1, :])
    y_prev = jnp.where(row == 0, before, pltpu.roll(y, 1, axis=0))
    y_next = jnp.where(row == TM - 1, after, pltpu.roll(y, TM - 1, axis=0))
    cv = cv_ref[...]
    conv = gb_ref[...] * (y_prev * cv[0:1] + y * cv[1:2] + y_next * cv[2:3])
    h = _dot(conv.astype(BF16), wo_ref[0:512, :]) + _dot(at_ref[...], wo_ref[512:1024, :])
    _mixer_tail(x, h, ci, g1_ref, sh2_ref, sc2_ref, lng_ref, lnb_ref, rw_ref, rb_ref, *out_refs)


def _cd_out_kernel(x_ref, mg_ref, wo_ref, g1_ref, sh2_ref, sc2_ref, lng_ref, lnb_ref, rw_ref, rb_ref,
                   *out_refs):
    ci = _cond_row(pl.program_id(0))
    h = _dot(mg_ref[...], wo_ref[...])
    _mixer_tail(x_ref[...], h, ci, g1_ref, sh2_ref, sc2_ref, lng_ref, lnb_ref, rw_ref, rb_ref, *out_refs)


def _tail_specs(layer):
    return [_mod_spec(layer, 2), _mod_spec(layer, 3), _mod_spec(layer, 4),
            _full((1, D)), _full((1, D)), _full((D, LANES)), _full((1, LANES))]


_TAIL_OUT_SPECS = [_row_spec(D), _row_spec(D), _row_spec(LANES), _row_spec(LANES)]
_TAIL_OUT_SHAPES = [jax.ShapeDtypeStruct((T, D), F32), jax.ShapeDtypeStruct((T, D), F32),
                    jax.ShapeDtypeStruct((T, LANES), jnp.int32), jax.ShapeDtypeStruct((T, LANES), F32)]


def _ab_out(xp, xs, gb, y, conv_w, attn, w_out, mod, tail):
    halo_prev = pl.BlockSpec((8, 512), lambda i: (jnp.maximum(i * (TM // 8) - 1, 0), 0))
    halo_next = pl.BlockSpec((8, 512), lambda i: (jnp.minimum((i + 1) * (TM // 8), T // 8 - 1), 0))
    return pl.pallas_call(
        _ab_out_kernel,
        grid=(NT,),
        in_specs=_x_specs() + [_row_spec(512), _row_spec(512), halo_prev, halo_next, _full((3, 512)),
                               _row_spec(512), _full((D, D))] + _tail_specs(0),
        out_specs=_TAIL_OUT_SPECS,
        out_shape=_TAIL_OUT_SHAPES,
        compiler_params=_params("arbitrary"),
        name="ab_out_proj",
    )(xp, xs, gb, y, y, y, conv_w, attn, w_out, mod, mod, mod, *tail)


def _cd_out(x, merged, w_out, mod, tail):
    return pl.pallas_call(
        _cd_out_kernel,
        grid=(NT,),
        in_specs=[_row_spec(D), _row_spec(D), _full((D, D))] + _tail_specs(1),
        out_specs=_TAIL_OUT_SPECS,
        out_shape=_TAIL_OUT_SHAPES,
        compiler_params=_params("arbitrary"),
        name="cd_out_proj",
    )(x, merged, w_out, mod, mod, mod, *tail)


def _mix_down(wd_ref, wmix_ref):
    half = D_FF // 2
    for c in range(D // LANES):
        wmix_ref[c, pl.ds(0, half, stride=2), :] = wd_ref[0:half, c * LANES:(c + 1) * LANES]
        wmix_ref[c, pl.ds(1, half, stride=2), :] = wd_ref[half:D_FF, c * LANES:(c + 1) * LANES]


def _prep_expert(wgu_ref, wd_ref, wgu_bf, wmix_ref, wd_bf):
    for c in range(4):
        wgu_bf[:, c * 512:(c + 1) * 512] = wgu_ref[:, c * 512:(c + 1) * 512].astype(BF16)
    _mix_down(wd_ref, wmix_ref)
    for c in range(D // LANES):
        wd_bf[:, c * LANES:(c + 1) * LANES] = wmix_ref[c].astype(BF16)


def _expert_ffn(u, wgu_bf, bgu_ref, wd_bf, bd_ref):
    rows = u.shape[0]
    ga = _dot(u, wgu_bf[:, 0:D_FF]) + bgu_ref[:, 0:D_FF]
    gb = _dot(u, wgu_bf[:, D_FF:2 * D_FF]) + bgu_ref[:, D_FF:2 * D_FF]
    even = (lax.broadcasted_iota(jnp.int32, (rows, LANES), 1) % 2) == 0
    hid = []
    for c in range(D_FF // LANES):
        a = ga[:, c * LANES:(c + 1) * LANES]
        b = gb[:, c * LANES:(c + 1) * LANES]
        gate = jnp.where(even, a, pltpu.roll(b, 1, axis=1))
        up = jnp.where(even, pltpu.roll(a, LANES - 1, axis=1), b)
        gate = jnp.minimum(gate, SWIGLU_LIMIT)
        up = jnp.clip(up, -SWIGLU_LIMIT, SWIGLU_LIMIT)
        hid.append(((up + 1.0) * gate * jax.nn.sigmoid(SWIGLU_ALPHA * gate)).astype(BF16))
    hid = jnp.concatenate(hid, axis=1)
    return _dot(hid, wd_bf[...]) + bd_ref[...]


def _moe_plan(idx):
    e = idx[:, :TOP_K].reshape(-1)
    order = jnp.argsort(e, stable=True).astype(jnp.int32)
    experts = jnp.arange(N_EXPERTS, dtype=jnp.int32)
    counts = jnp.sum((e[:, None] == experts[None]).astype(jnp.int32), axis=0)
    tiles_e = (counts + MOE_TM - 1) // MOE_TM
    tile_end = jnp.cumsum(tiles_e)
    first_pair = jnp.cumsum(counts) - counts
    j = jnp.arange(MOE_TILES, dtype=jnp.int32)
    tile_expert = jnp.minimum(jnp.sum((j[:, None] >= tile_end[None]).astype(jnp.int32), axis=1), N_EXPERTS - 1)
    lane = jnp.arange(MOE_TM, dtype=jnp.int32)[None]
    rank = (j - (tile_end - tiles_e)[tile_expert])[:, None] * MOE_TM + lane
    valid = rank < counts[tile_expert][:, None]
    pair = order[jnp.clip(first_pair[tile_expert][:, None] + rank, 0, MOE_PAIRS - 1)]
    spread = jnp.broadcast_to(lane, rank.shape)
    tok = jnp.where(valid, pair // TOP_K, spread)
    dst = jnp.where(valid, (pair % TOP_K) * T + pair // TOP_K, MOE_PAIRS + spread)
    return (tile_expert, tile_end[-1:].astype(jnp.int32),
            tok.astype(jnp.int32)[:, None, :], dst.astype(jnp.int32)[:, None, :])


def _moe_kernel(te_ref, nu_ref, tok_ref, tokn_ref, dst_ref, u_hbm, wgu_ref, bgu_ref, wd_ref, bd_ref,
                y_hbm, xbuf, ybuf, wgu_bf, wmix_ref, wd_bf, gsem, ssem):
    j = pl.program_id(0)
    n_used = nu_ref[0]
    slot = j % 2

    def gather_rows(idx_ref, s):
        for r in range(MOE_TM):
            pltpu.make_async_copy(u_hbm.at[pl.ds(idx_ref[0, r], 1), :], xbuf.at[s, pl.ds(r, 1), :],
                                  gsem.at[s]).start()

    def gather_wait(s):
        pltpu.make_async_copy(u_hbm.at[pl.ds(0, MOE_TM), :], xbuf.at[s], gsem.at[s]).wait()

    def scatter_wait(s):
        pltpu.make_async_copy(ybuf.at[s], y_hbm.at[pl.ds(0, MOE_TM), :], ssem.at[s]).wait()

    @pl.when(j < n_used)
    def _():
        @pl.when(j == 0)
        def _():
            gather_rows(tok_ref, 0)

        @pl.when(j + 1 < n_used)
        def _():
            gather_rows(tokn_ref, 1 - slot)

        gather_wait(slot)

        @pl.when(j >= 2)
        def _():
            scatter_wait(slot)

        @pl.when(jnp.logical_or(j == 0, te_ref[j] != te_ref[jnp.maximum(j - 1, 0)]))
        def _():
            _prep_expert(wgu_ref, wd_ref, wgu_bf, wmix_ref, wd_bf)

        ybuf[slot] = _expert_ffn(xbuf[slot].astype(BF16), wgu_bf, bgu_ref, wd_bf, bd_ref)
        for r in range(MOE_TM):
            pltpu.make_async_copy(ybuf.at[slot, pl.ds(r, 1), :], y_hbm.at[pl.ds(dst_ref[0, r], 1), :],
                                  ssem.at[slot]).start()

    @pl.when(j == MOE_TILES - 1)
    def _():
        @pl.when(n_used >= 1)
        def _():
            scatter_wait((n_used - 1) % 2)

        @pl.when(n_used >= 2)
        def _():
            scatter_wait(n_used % 2)


def _moe(layer, u2, idx, w_gate_up, b_gate_up, w_down, b_down):
    tile_expert, n_used, tok, dst = _moe_plan(idx)
    smem = lambda f: pl.BlockSpec((None, 1, MOE_TM), f, memory_space=pltpu.SMEM)
    grid_spec = pltpu.PrefetchScalarGridSpec(
        num_scalar_prefetch=2,
        grid=(MOE_TILES,),
        in_specs=[smem(lambda j, te, nu: (j, 0, 0)),
                  smem(lambda j, te, nu: (jnp.minimum(j + 1, MOE_TILES - 1), 0, 0)),
                  smem(lambda j, te, nu: (j, 0, 0)),
                  pl.BlockSpec(memory_space=pl.ANY),
                  pl.BlockSpec((None, None, D, 2 * D_FF), lambda j, te, nu: (layer, te[j], 0, 0)),
                  pl.BlockSpec((None, None, 1, 2 * D_FF), lambda j, te, nu: (layer, te[j], 0, 0)),
                  pl.BlockSpec((None, None, D_FF, D), lambda j, te, nu: (layer, te[j], 0, 0)),
                  pl.BlockSpec((None, None, 1, D), lambda j, te, nu: (layer, te[j], 0, 0))],
        out_specs=pl.BlockSpec(memory_space=pl.ANY),
        scratch_shapes=[pltpu.VMEM((2, MOE_TM, D), F32), pltpu.VMEM((2, MOE_TM, D), F32),
                        pltpu.VMEM((D, 2 * D_FF), BF16), pltpu.VMEM((D // LANES, D_FF, LANES), F32),
                        pltpu.VMEM((D_FF, D), BF16),
                        pltpu.SemaphoreType.DMA((2,)), pltpu.SemaphoreType.DMA((2,))])
    return pl.pallas_call(
        _moe_kernel,
        grid_spec=grid_spec,
        out_shape=jax.ShapeDtypeStruct((MOE_PAIRS + MOE_TM, D), F32),
        compiler_params=_params("arbitrary"),
        name="moe_experts",
    )(tile_expert, n_used, tok, tok, dst, u2, w_gate_up, b_gate_up.reshape(DEPTH, N_EXPERTS, 1, 2 * D_FF),
      w_down, b_down.reshape(DEPTH, N_EXPERTS, 1, D))


def _ln2_value(x1_ref, y_refs, w_ref, g2_ref, lng_ref, lnb_ref):
    ci = _cond_row(pl.program_id(0))
    w = w_ref[...]
    f = w[:, 0:1] * y_refs[0][...]
    for k in range(1, TOP_K):
        f = f + w[:, k:k + 1] * y_refs[k][...]
    z = DEEPNORM_ALPHA * x1_ref[...] + g2_ref[pl.ds(ci, 1), :] * f
    return _layer_norm(z, lng_ref[...], lnb_ref[...])


def _ln2_kernel(x1_ref, y0, y1, y2, y3, w_ref, g2_ref, lng_ref, lnb_ref, o_ref):
    o_ref[...] = _ln2_value(x1_ref, (y0, y1, y2, y3), w_ref, g2_ref, lng_ref, lnb_ref)


def _ln2_final_kernel(x1_ref, y0, y1, y2, y3, w_ref, g2_ref, lng_ref, lnb_ref, op_ref, os_ref):
    i = pl.program_id(0)
    out = _ln2_value(x1_ref, (y0, y1, y2, y3), w_ref, g2_ref, lng_ref, lnb_ref)

    @pl.when(i < NT_P)
    def _():
        op_ref[...] = out

    @pl.when(i >= NT_P)
    def _():
        os_ref[...] = out


def _ln2(layer, x1, yk, w, mod, g, b):
    y_specs = [pl.BlockSpec((TM, D), lambda i, _k=k: (_k * NT + i, 0)) for k in range(TOP_K)]
    in_specs = [_row_spec(D)] + y_specs + [_row_spec(LANES), _mod_spec(layer, 5), _full((1, D)), _full((1, D))]
    args = (x1, yk, yk, yk, yk, w, mod, g, b)
    if layer < DEPTH - 1:
        return pl.pallas_call(
            _ln2_kernel, grid=(NT,), in_specs=in_specs, out_specs=_row_spec(D),
            out_shape=jax.ShapeDtypeStruct((T, D), F32),
            compiler_params=_params("arbitrary"), name="post_norm2",
        )(*args)
    return pl.pallas_call(
        _ln2_final_kernel, grid=(NT,), in_specs=in_specs, out_specs=_x_specs(),
        out_shape=[jax.ShapeDtypeStruct((T_P, D), F32), jax.ShapeDtypeStruct((T_S, D), F32)],
        compiler_params=_params("arbitrary"), name="post_norm2_final",
    )(*args)


def _router_tail(l, ln1_g, ln1_b, router_w, router_b):
    rw = jnp.pad(router_w[l], ((0, 0), (0, LANES - N_EXPERTS)))
    rb = jnp.pad(router_b[l], (0, LANES - N_EXPERTS), constant_values=NEG)
    return ln1_g[l][None], ln1_b[l][None], rw, rb[None]


def kernel(x_prompt, x_sample, c, c_ctx, cache_diff_k, cache_diff_v, cache_na_k, cache_na_v, cache_gqa_k, cache_gqa_v, w_mod, b_mod, ln1_g, ln1_b, ln2_g, ln2_b, ab_w_in, ab_conv_w, ab_lambda_q1, ab_lambda_k1, ab_lambda_q2, ab_lambda_k2, ab_subln_g, ab_w_out, cd_w_in, cd_na_rpb, cd_q_norm_g, cd_k_norm_g, cd_w_out, router_w, router_b, w_gate_up, b_gate_up, w_down, b_down):
    xp = x_prompt.reshape(T_P, D)
    xs = x_sample.reshape(T_S, D)
    cond8 = jnp.concatenate([c_ctx[None], c, jnp.zeros((8 - 1 - B_S, D), F32)], axis=0)
    mod = _modulation(cond8, w_mod, b_mod)
    rope = _rope_tables()
    experts = (w_gate_up, b_gate_up, w_down, b_down)

    gb, y, q, k, v, new_diff_k, new_diff_v = _ab_in(xp, xs, mod, ab_w_in[0].astype(BF16), rope)
    lam_init = 0.8 - 0.6 * 1.0
    diff = (jnp.stack([ab_lambda_q1[0], ab_lambda_k1[0], ab_lambda_q2[0], ab_lambda_k2[0]]), ab_subln_g[0][None])
    attn = _flash_pair(q, k, v, latent=False, out_cols=512, col0=0, post="diff", diff=diff, lam_init=lam_init)
    attn = _flash_pair(q, k, v, latent=True, out_cols=512, col0=0, post="diff", diff=diff, lam_init=lam_init,
                       ctx=(cache_diff_k, cache_diff_v), ctx_mode="wide", prev=attn)
    x1, u2, idx, w = _ab_out(xp, xs, gb, y, ab_conv_w[0], attn, ab_w_out[0].astype(BF16), mod,
                             _router_tail(0, ln1_g, ln1_b, router_w, router_b))
    x = _ln2(0, x1, _moe(0, u2, idx, *experts), w, mod, ln2_g[0][None], ln2_b[0][None])

    qg = jnp.tile(cd_q_norm_g[0], 2)[None]
    kg = jnp.tile(cd_k_norm_g[0], 2)[None]
    nq, nk, nv, gq, gk, gv, new_na_k, new_na_v, new_gqa_k, new_gqa_v = _cd_in(
        x, mod, cd_w_in[0].astype(BF16), qg, kg, rope)
    merged = _flash_pair(nq, nk, nv, latent=False, out_cols=D, col0=0, post="select")
    merged = _flash_pair(gq, gk, gv, latent=False, out_cols=D, col0=2, post="select", prev=merged)
    merged = _na_latent(nq, nk, nv, cache_na_k, cache_na_v, _na_bias(cd_na_rpb[0]), merged)
    merged = _flash_pair(gq, gk, gv, latent=True, out_cols=D, col0=2, post="select",
                         ctx=(cache_gqa_k, cache_gqa_v), ctx_mode="dup", prev=merged)
    x1, u2, idx, w = _cd_out(x, merged, cd_w_out[0].astype(BF16), mod,
                             _router_tail(1, ln1_g, ln1_b, router_w, router_b))
    y_p, y_s = _ln2(1, x1, _moe(1, u2, idx, *experts), w, mod, ln2_g[1][None], ln2_b[1][None])

    return (y_p.reshape(B_P, N_P, D), y_s.reshape(B_S, N_S, D), new_diff_k, new_diff_v,
            new_na_k, new_na_v, new_gqa_k, new_gqa_v)
```

```python
import jax
import jax.numpy as jnp
from jax import lax
from jax.experimental import pallas as pl
from jax.experimental.pallas import tpu as pltpu

F32 = jnp.float32
BF16 = jnp.bfloat16

D = 1024
B_P, N_P = 16, 256
B_S, N_S = 2, 4096
PAST = 256
T_P, T_S = B_P * N_P, B_S * N_S
T = T_P + T_S
TM = 256
NT_P, NT_S, NT = T_P // TM, T_S // TM, T // TM
TILES_PER_GRID = N_S // TM
GRID_W = 64
GRID_H = N_S // GRID_W
HD = 64
DEPTH = 2
N_EXPERTS = 32
TOP_K = 4
D_FF = 1024
NA_WIN_R, NA_WIN_C = 8, 16
NA_QROWS = 4
NA_KROWS = 12
SWIGLU_LIMIT = 7.0
SWIGLU_ALPHA = 1.702
ROPE_THETA = 10000.0
DEEPNORM_ALPHA = (2 * DEPTH) ** 0.25
LN_EPS = 1e-5
RMS_EPS = 1e-6
QK_SCALE = HD ** -0.5
NEG = -1e30
MOE_TM = 256
MOE_PAIRS = TOP_K * T
MOE_TILES = MOE_PAIRS // MOE_TM + N_EXPERTS
LANES = 128
SUB = 8

VMEM_LIMIT = 56 * 1024 * 1024


def _params(*sem):
    return pltpu.CompilerParams(dimension_semantics=sem, vmem_limit_bytes=VMEM_LIMIT)


def _split(x):
    hi = x.astype(BF16)
    lo = (x - hi.astype(F32)).astype(BF16)
    return hi, lo


def _dot(a, b):
    return jnp.dot(a, b, preferred_element_type=F32)


def _dot3(a, b):
    ah, al = _split(a)
    bh, bl = _split(b)
    return _dot(ah, bh) + (_dot(ah, bl) + _dot(al, bh))


def _cond_row(i):
    return jnp.where(i < NT_P, 0, 1 + (i - NT_P) // TILES_PER_GRID)


def _layer_norm(z, g, b):
    mu = jnp.mean(z, axis=-1, keepdims=True)
    zc = z - mu
    var = jnp.mean(zc * zc, axis=-1, keepdims=True)
    return zc * lax.rsqrt(var + LN_EPS) * g + b


def _low_half(rows):
    return lax.broadcasted_iota(jnp.int32, (rows, LANES), 1) < HD


def _mod_kernel(c_ref, w_ref, b_ref, o_ref):
    c = c_ref[...]
    o_ref[...] = _dot3(c * jax.nn.sigmoid(c), w_ref[...]) + b_ref[...]


def _modulation(cond8, w_mod, b_mod):
    return pl.pallas_call(
        _mod_kernel,
        grid=(DEPTH, 6),
        in_specs=[pl.BlockSpec((8, D), lambda l, j: (0, 0)),
                  pl.BlockSpec((None, D, D), lambda l, j: (l, 0, j)),
                  pl.BlockSpec((None, 1, D), lambda l, j: (l, 0, j))],
        out_specs=pl.BlockSpec((None, 8, D), lambda l, j: (l, 0, j)),
        out_shape=jax.ShapeDtypeStruct((DEPTH, 8, 6 * D), F32),
        compiler_params=_params("arbitrary", "arbitrary"),
        name="modulation",
    )(cond8, w_mod, b_mod.reshape(DEPTH, 1, 6 * D))


def _mod_spec(layer, chunk):
    return pl.BlockSpec((None, 8, D), lambda i, _l=layer, _c=chunk: (_l, 0, _c))


def _full(shape):
    return pl.BlockSpec(shape, lambda i: (0,) * len(shape))


def _rope_tables():
    t = jnp.arange(N_S)
    pos_r = (t // GRID_W).astype(F32)
    pos_c = (t % GRID_W).astype(F32)
    half = HD // 2
    inv = ROPE_THETA ** (-jnp.arange(0, half, 2, dtype=F32) / half)
    inv_lane = jnp.tile(jnp.repeat(inv, 2), 2 * LANES // HD)
    lane = jnp.arange(LANES)
    by_row = (lane % HD) < half
    ang = jnp.where(by_row[None], pos_r[:, None], pos_c[:, None]) * inv_lane[None]
    cos = jnp.cos(ang)
    sin = jnp.sin(ang)
    even = (lane % 2) == 0
    return cos, jnp.where(even, -sin, 0.0), jnp.where(even, 0.0, sin)


def _rope(x, a, b, c):
    return x * a + pltpu.roll(x, LANES - 1, axis=1) * b + pltpu.roll(x, 1, axis=1) * c


def _rope_spec():
    return pl.BlockSpec((TM, LANES), lambda i: (jnp.maximum(i - NT_P, 0) % TILES_PER_GRID, 0))


def _x_specs():
    return [pl.BlockSpec((TM, D), lambda i: (jnp.minimum(i, NT_P - 1), 0)),
            pl.BlockSpec((TM, D), lambda i: (jnp.maximum(i - NT_P, 0), 0))]


def _cache_spec(heads, width):
    return pl.BlockSpec((None, None, heads, N_P, width), lambda i: (jnp.minimum(i, NT_P - 1), 0, 0, 0, 0))


def _row_spec(width):
    return pl.BlockSpec((TM, width), lambda i: (i, 0))


def _hm_spec(n):
    return pl.BlockSpec((n, TM, LANES), lambda i: (0, i, 0))


AB_Q0, AB_K0, AB_V0 = 1536, 2048, 2560


def _tile(p, col0, j):
    return p[:, col0 + j * LANES:col0 + (j + 1) * LANES]


def _ab_in_kernel(xp_ref, xs_ref, sh_ref, sc_ref, w_ref, ra_ref, rb_ref, rc_ref,
                  gb_ref, y_ref, q_ref, k_ref, v_ref, kc_ref, vc_ref):
    i = pl.program_id(0)
    is_p = i < NT_P
    ci = _cond_row(i)
    x = jnp.where(is_p, xp_ref[...], xs_ref[...])
    u = x * (1.0 + sc_ref[pl.ds(ci, 1), :]) + sh_ref[pl.ds(ci, 1), :]
    p = _dot(u.astype(BF16), w_ref[...])
    gb_ref[...] = p[:, 0:512]
    y_ref[...] = p[:, 512:1024] * p[:, 1024:1536]
    for h in range(4):
        v_ref[h] = _tile(p, AB_V0, h).astype(BF16)

    @pl.when(is_p)
    def _():
        for h in range(4):
            q_ref[h] = (_tile(p, AB_Q0, h) * QK_SCALE).astype(BF16)
            k_ref[h] = _tile(p, AB_K0, h).astype(BF16)
            kc_ref[h] = _tile(p, AB_K0, h)
            vc_ref[h] = _tile(p, AB_V0, h)

    @pl.when(jnp.logical_not(is_p))
    def _():
        a, b, c = ra_ref[...], rb_ref[...], rc_ref[...]
        for h in range(4):
            q_ref[h] = (_rope(_tile(p, AB_Q0, h), a, b, c) * QK_SCALE).astype(BF16)
            k_ref[h] = _rope(_tile(p, AB_K0, h), a, b, c).astype(BF16)


def _ab_in(xp, xs, mod, w_in, rope):
    hm = jax.ShapeDtypeStruct((4, T, LANES), BF16)
    cache = jax.ShapeDtypeStruct((B_P, 1, 4, N_P, LANES), F32)
    half = jax.ShapeDtypeStruct((T, 512), F32)
    return pl.pallas_call(
        _ab_in_kernel,
        grid=(NT,),
        in_specs=_x_specs() + [_mod_spec(0, 0), _mod_spec(0, 1), _full((D, 3072)),
                               _rope_spec(), _rope_spec(), _rope_spec()],
        out_specs=[_row_spec(512), _row_spec(512), _hm_spec(4), _hm_spec(4), _hm_spec(4),
                   _cache_spec(4, LANES), _cache_spec(4, LANES)],
        out_shape=[half, half, hm, hm, hm, cache, cache],
        compiler_params=_params("arbitrary"),
        name="ab_in_proj",
    )(xp, xs, mod, mod, w_in, *rope)


CD_NQ, CD_NK, CD_NV, CD_GQ, CD_GK, CD_GV = 0, 512, 1024, 1536, 2048, 2176


def _seg_mean64(s):
    r = lax.broadcasted_iota(jnp.int32, (LANES, LANES), 0) // HD
    c = lax.broadcasted_iota(jnp.int32, (LANES, LANES), 1) // HD
    seg = jnp.where(r == c, 1.0, 0.0).astype(BF16)
    hi, lo = _split(s)
    return (_dot(hi, seg) + _dot(lo, seg)) * (1.0 / HD)


def _rms64(x, g):
    return x * lax.rsqrt(_seg_mean64(x * x) + RMS_EPS) * g


def _dup_halves(x, lo):
    r = pltpu.roll(x, HD, axis=1)
    return jnp.where(lo, x, r), jnp.where(lo, r, x)


def _cd_in_kernel(x_ref, sh_ref, sc_ref, w_ref, qg_ref, kg_ref, ra_ref, rb_ref, rc_ref,
                  nq_ref, nk_ref, nv_ref, gq_ref, gk_ref, gv_ref,
                  nkc_ref, nvc_ref, gkc_ref, gvc_ref):
    i = pl.program_id(0)
    is_p = i < NT_P
    ci = _cond_row(i)
    u = x_ref[...] * (1.0 + sc_ref[pl.ds(ci, 1), :]) + sh_ref[pl.ds(ci, 1), :]
    p = _dot(u.astype(BF16), w_ref[...])
    lo = _low_half(TM)
    for j in range(4):
        nq_ref[j] = (_tile(p, CD_NQ, j) * QK_SCALE).astype(BF16)
        nk_ref[j] = _tile(p, CD_NK, j).astype(BF16)
        nv_ref[j] = _tile(p, CD_NV, j).astype(BF16)
    gq = [_rms64(_tile(p, CD_GQ, j), qg_ref[...]) for j in range(4)]
    gk = _rms64(_tile(p, CD_GK, 0), kg_ref[...])
    gv = _tile(p, CD_GV, 0)
    v0, v1 = _dup_halves(gv, lo)
    gv_ref[0] = v0.astype(BF16)
    gv_ref[1] = v1.astype(BF16)

    def emit(gq, gk):
        for j in range(4):
            gq_ref[j] = (gq[j] * QK_SCALE).astype(BF16)
        k0, k1 = _dup_halves(gk, lo)
        gk_ref[0] = k0.astype(BF16)
        gk_ref[1] = k1.astype(BF16)
        return k0, k1

    @pl.when(is_p)
    def _():
        k0, k1 = emit(gq, gk)
        gkc_ref[0] = k0[:, 0:HD]
        gkc_ref[1] = k1[:, 0:HD]
        gvc_ref[0] = v0[:, 0:HD]
        gvc_ref[1] = v1[:, 0:HD]
        for j in range(4):
            for src, dst in ((CD_NK, nkc_ref), (CD_NV, nvc_ref)):
                a, b = _dup_halves(_tile(p, src, j), lo)
                dst[2 * j] = a[:, 0:HD]
                dst[2 * j + 1] = b[:, 0:HD]

    @pl.when(jnp.logical_not(is_p))
    def _():
        a, b, c = ra_ref[...], rb_ref[...], rc_ref[...]
        emit([_rope(g, a, b, c) for g in gq], _rope(gk, a, b, c))


def _cd_in(x, mod, w_in, qg, kg, rope):
    hm4 = jax.ShapeDtypeStruct((4, T, LANES), BF16)
    hm2 = jax.ShapeDtypeStruct((2, T, LANES), BF16)
    c8 = jax.ShapeDtypeStruct((B_P, 1, 8, N_P, HD), F32)
    c2 = jax.ShapeDtypeStruct((B_P, 1, 2, N_P, HD), F32)
    return pl.pallas_call(
        _cd_in_kernel,
        grid=(NT,),
        in_specs=[_row_spec(D), _mod_spec(1, 0), _mod_spec(1, 1), _full((D, 2304)),
                  _full((1, LANES)), _full((1, LANES)), _rope_spec(), _rope_spec(), _rope_spec()],
        out_specs=[_hm_spec(4), _hm_spec(4), _hm_spec(4), _hm_spec(4), _hm_spec(2), _hm_spec(2),
                   _cache_spec(8, HD), _cache_spec(8, HD), _cache_spec(2, HD), _cache_spec(2, HD)],
        out_shape=[hm4, hm4, hm4, hm4, hm2, hm2, c8, c8, c2, c2],
        compiler_params=_params("arbitrary"),
        name="cd_in_proj",
    )(x, mod, mod, w_in, qg, kg, *rope)


def _stack_pairs(q_ref, n_q, tq):
    lo = _low_half(tq)
    parts = []
    for j in range(n_q):
        q = q_ref[j]
        zero = jnp.zeros_like(q)
        parts += [jnp.where(lo, q, zero), jnp.where(lo, zero, q)]
    return jnp.concatenate(parts, axis=0), lo


def _qk(qs, kb):
    return lax.dot_general(qs, kb, (((1,), (1,)), ((), ())), preferred_element_type=F32)


def _ctx_tile(ref, mode):
    if mode == "wide":
        x = ref[...]
    elif mode == "pair":
        x = jnp.concatenate([ref[0], ref[1]], axis=1)
    else:
        x = jnp.concatenate([ref[...], ref[...]], axis=1)
    return x.astype(BF16)


def _flash_pair_kernel(*refs, n_q, tq, nk, tk, ctx_mode, post, lam_init):
    it = iter(refs)
    q_ref, k_ref, v_ref = next(it), next(it), next(it)
    kc_ref, vc_ref = (next(it), next(it)) if ctx_mode else (None, None)
    lam_ref, g_ref = (next(it), next(it)) if post == "diff" else (None, None)
    o_ref = next(it)

    qs, lo = _stack_pairs(q_ref, n_q, tq)
    rows = 2 * n_q * tq

    def step(kb, vb, carry):
        m, l, acc = carry
        s = _qk(qs, kb)
        m_new = jnp.maximum(m, jnp.max(s, axis=1, keepdims=True))
        alpha = jnp.exp(m - m_new)
        p = jnp.exp(s - m_new)
        l = alpha * l + jnp.sum(p, axis=1, keepdims=True)
        acc = alpha * acc + _dot(p.astype(BF16), vb)
        return m_new, l, acc

    carry = (jnp.full((rows, 1), NEG, F32), jnp.zeros((rows, 1), F32), jnp.zeros((rows, LANES), F32))
    for c in range(nk // tk):
        carry = step(k_ref[c * tk:(c + 1) * tk, :], v_ref[c * tk:(c + 1) * tk, :], carry)
    if ctx_mode:
        carry = step(_ctx_tile(kc_ref, ctx_mode), _ctx_tile(vc_ref, ctx_mode), carry)
    _, l, acc = carry
    o = acc / l

    if post == "diff":
        lp = lam_ref[...]
        lam = (jnp.exp(jnp.sum(lp[0:1] * lp[1:2], axis=1, keepdims=True))
               - jnp.exp(jnp.sum(lp[2:3] * lp[3:4], axis=1, keepdims=True)) + lam_init)
        a = o[0:tq] - lam * o[tq:2 * tq]
        ms = jnp.mean(a * a, axis=-1, keepdims=True)
        o_ref[...] = (a * lax.rsqrt(ms + RMS_EPS) * g_ref[...] * (1.0 - lam_init)).astype(o_ref.dtype)
    else:
        for j in range(n_q):
            o_ref[:, j * LANES:(j + 1) * LANES] = jnp.where(
                lo, o[2 * j * tq:(2 * j + 1) * tq], o[(2 * j + 1) * tq:(2 * j + 2) * tq]).astype(o_ref.dtype)


def _flash_pair(q, k, v, *, latent, out_cols, col0, post, ctx=None, ctx_mode=None, diff=None,
                lam_init=0.0, prev=None):
    groups = k.shape[0]
    n_q = q.shape[0] // groups
    if latent:
        tq, nk, tk = TM, N_S, 1024
        grid = (B_S, groups, TILES_PER_GRID)
        qrow = lambda b, g, i: NT_P + b * TILES_PER_GRID + i
        krow = lambda b, g, i: T_P // N_S + b
    else:
        tq, nk, tk = N_P, N_P, N_P
        grid = (B_P, groups, 1)
        qrow = lambda b, g, i: b
        krow = lambda b, g, i: b
    in_specs = [pl.BlockSpec((n_q, tq, LANES), lambda b, g, i: (g, qrow(b, g, i), 0)),
                pl.BlockSpec((None, nk, LANES), lambda b, g, i: (g, krow(b, g, i), 0)),
                pl.BlockSpec((None, nk, LANES), lambda b, g, i: (g, krow(b, g, i), 0))]
    args = [q, k, v]
    if ctx is not None:
        if ctx_mode == "wide":
            spec = pl.BlockSpec((None, None, None, PAST, LANES), lambda b, g, i: (b, 0, g, 0, 0))
        elif ctx_mode == "pair":
            spec = pl.BlockSpec((None, None, 2, PAST, HD), lambda b, g, i: (b, 0, g, 0, 0))
        else:
            spec = pl.BlockSpec((None, None, None, PAST, HD), lambda b, g, i: (b, 0, g, 0, 0))
        in_specs += [spec, spec]
        args += list(ctx)
    if diff is not None:
        in_specs += [pl.BlockSpec((4, HD), lambda b, g, i: (0, 0)),
                     pl.BlockSpec((1, LANES), lambda b, g, i: (0, 0))]
        args += list(diff)
    aliases = {}
    if prev is not None:
        aliases = {len(args): 0}
        in_specs.append(pl.BlockSpec(memory_space=pl.ANY))
        args.append(prev)

    def kernel(*refs):
        if prev is not None:
            refs = refs[:-2] + refs[-1:]
        _flash_pair_kernel(*refs, n_q=n_q, tq=tq, nk=nk, tk=tk, ctx_mode=ctx_mode if ctx is not None else None,
                           post=post, lam_init=lam_init)

    return pl.pallas_call(
        kernel,
        grid=grid,
        in_specs=in_specs,
        out_specs=pl.BlockSpec((tq, n_q * LANES), lambda b, g, i: (qrow(b, g, i), col0 + g)),
        out_shape=jax.ShapeDtypeStruct((T, out_cols), BF16),
        input_output_aliases=aliases,
        compiler_params=_params("arbitrary", "arbitrary", "arbitrary"),
        name="attn_" + post + ("_latent" if latent else "_context") + str(n_q),
    )(*args)


NA_TQ = NA_QROWS * GRID_W
NA_TK = NA_KROWS * GRID_W


NA_BLOCK_POS = ((0, 0), (NA_QROWS, 0), (GRID_H - NA_QROWS, GRID_H - NA_KROWS))
N_DR = 2 * NA_WIN_R - 1
N_DC = 2 * NA_WIN_C - 1


def _na_bias_kernel(rpb_ref, o_ref):
    qc = lax.broadcasted_iota(jnp.int32, (GRID_W, LANES), 0)
    lane = lax.broadcasted_iota(jnp.int32, (GRID_W, LANES), 1)
    kc = lane % GRID_W
    cs = jnp.clip(qc - NA_WIN_C // 2, 0, GRID_W - NA_WIN_C)
    col_ok = jnp.logical_and(kc >= cs, kc < cs + NA_WIN_C)
    lo = lane < GRID_W
    neg = jnp.full((GRID_W, LANES), NEG, F32)
    for head in range(2):
        toeplitz = []
        for dr in range(N_DR):
            r = jnp.broadcast_to(rpb_ref[head, dr:dr + 1, :], (GRID_W, LANES))
            t = jnp.where(lo, pltpu.roll(r, LANES - (NA_WIN_C - 1), axis=1, stride=1, stride_axis=0),
                          pltpu.roll(r, GRID_W - (NA_WIN_C - 1), axis=1, stride=1, stride_axis=0))
            toeplitz.append(jnp.where(col_ok, t, neg))
        for pos, (r0, k0) in enumerate(NA_BLOCK_POS):
            for i in range(NA_QROWS):
                qr = r0 + i
                rs = min(max(qr - NA_WIN_R // 2, 0), GRID_H - NA_WIN_R)
                tiles = [toeplitz[k0 + j - qr + NA_WIN_R - 1] if rs <= k0 + j < rs + NA_WIN_R else neg
                         for j in range(NA_KROWS)]
                for jp in range(NA_KROWS // 2):
                    o_ref[pos, pl.ds(head * NA_TQ + i * GRID_W, GRID_W), pl.ds(jp * LANES, LANES)] = jnp.where(
                        lo, tiles[2 * jp], tiles[2 * jp + 1])


def _na_bias(rpb):
    rpb_pad = jnp.pad(rpb, ((0, 0), (0, 16 - N_DR), (0, LANES - N_DC)), constant_values=NEG)
    return pl.pallas_call(
        _na_bias_kernel,
        grid=(4,),
        in_specs=[pl.BlockSpec((2, 16, LANES), lambda g: (g, 0, 0))],
        out_specs=pl.BlockSpec((3, None, 2 * NA_TQ, NA_TK), lambda g: (0, g, 0, 0)),
        out_shape=jax.ShapeDtypeStruct((3, 4, 2 * NA_TQ, NA_TK), F32),
        compiler_params=_params("arbitrary"),
        name="na_bias_table",
    )(rpb_pad)


def _na_kernel(q_ref, k_ref, v_ref, kc_ref, vc_ref, bm_ref, _, o_ref):
    i = pl.program_id(2)
    qs, lo = _stack_pairs(q_ref, 1, NA_TQ)
    k0 = jnp.clip(i * NA_QROWS - NA_WIN_R // 2, 0, GRID_H - NA_KROWS)
    start = pl.multiple_of(k0 * GRID_W, GRID_W)
    kw = k_ref[pl.ds(start, NA_TK), :]
    vw = v_ref[pl.ds(start, NA_TK), :]
    s_w = _qk(qs, kw) + bm_ref[...]
    s_c = _qk(qs, _ctx_tile(kc_ref, "pair"))
    m = jnp.maximum(jnp.max(s_w, axis=1, keepdims=True), jnp.max(s_c, axis=1, keepdims=True))
    p_w = jnp.exp(s_w - m)
    p_c = jnp.exp(s_c - m)
    l = jnp.sum(p_w, axis=1, keepdims=True) + jnp.sum(p_c, axis=1, keepdims=True)
    o = (_dot(p_w.astype(BF16), vw) + _dot(p_c.astype(BF16), _ctx_tile(vc_ref, "pair"))) / l
    o_ref[...] = jnp.where(lo, o[0:NA_TQ], o[NA_TQ:2 * NA_TQ]).astype(o_ref.dtype)


def _na_latent(q, k, v, kc, vc, bm, prev):
    nblk = N_S // NA_TQ
    qrow = lambda b, g, i: T_P // NA_TQ + b * nblk + i
    krow = lambda b, g, i: T_P // N_S + b
    cfg = lambda i: jnp.where(i == 0, 0, jnp.where(i == nblk - 1, 2, 1))
    ctx_spec = pl.BlockSpec((None, None, 2, PAST, HD), lambda b, g, i: (b, 0, g, 0, 0))
    return pl.pallas_call(
        _na_kernel,
        grid=(B_S, 4, nblk),
        in_specs=[pl.BlockSpec((1, NA_TQ, LANES), lambda b, g, i: (g, qrow(b, g, i), 0)),
                  pl.BlockSpec((None, N_S, LANES), lambda b, g, i: (g, krow(b, g, i), 0)),
                  pl.BlockSpec((None, N_S, LANES), lambda b, g, i: (g, krow(b, g, i), 0)),
                  ctx_spec, ctx_spec,
                  pl.BlockSpec((None, None, 2 * NA_TQ, NA_TK), lambda b, g, i: (cfg(i), g, 0, 0)),
                  pl.BlockSpec(memory_space=pl.ANY)],
        out_specs=pl.BlockSpec((NA_TQ, LANES), lambda b, g, i: (qrow(b, g, i), g)),
        out_shape=jax.ShapeDtypeStruct((T, D), BF16),
        input_output_aliases={6: 0},
        compiler_params=_params("arbitrary", "arbitrary", "arbitrary"),
        name="attn_window_latent",
    )(q, k, v, kc, vc, bm, prev)


def _top4(logits):
    lane = lax.broadcasted_iota(jnp.int32, logits.shape, 1).astype(F32)
    rest = logits
    tops, firsts = [], []
    for _ in range(TOP_K):
        m = jnp.max(rest, axis=1, keepdims=True)
        first = jnp.min(jnp.where(rest == m, lane, float(LANES)), axis=1, keepdims=True)
        tops.append(m)
        firsts.append(first)
        rest = jnp.where(lane == first, -jnp.inf, rest)
    es = [jnp.exp(m - tops[0]) for m in tops]
    denom = es[0] + es[1] + es[2] + es[3]
    idx = jnp.zeros_like(logits)
    w = jnp.zeros_like(logits)
    for k in range(TOP_K):
        idx = jnp.where(lane == float(k), firsts[k], idx)
        w = jnp.where(lane == float(k), es[k] / denom, w)
    return idx.astype(jnp.int32), w


def _mixer_tail(x, h, ci, g1_ref, sh2_ref, sc2_ref, lng_ref, lnb_ref, rw_ref, rb_ref,
                x1_ref, u2_ref, idx_ref, w_ref):
    x1 = _layer_norm(DEEPNORM_ALPHA * x + g1_ref[pl.ds(ci, 1), :] * h, lng_ref[...], lnb_ref[...])
    x1_ref[...] = x1
    u2 = x1 * (1.0 + sc2_ref[pl.ds(ci, 1), :]) + sh2_ref[pl.ds(ci, 1), :]
    for c in range(SUB):
        u2_ref[pl.ds(c, TM, stride=SUB), :] = u2[:, c * LANES:(c + 1) * LANES]
    idx_ref[...], w_ref[...] = _top4(_dot3(u2, rw_ref[...]) + rb_ref[...])


def _ab_out_kernel(xp_ref, xs_ref, gb_ref, y_ref, yp_ref, yn_ref, cv_ref, at_ref, wo_ref,
                   g1_ref, sh2_ref, sc2_ref, lng_ref, lnb_ref, rw_ref, rb_ref, *out_refs):
    i = pl.program_id(0)
    is_p = i < NT_P
    ci = _cond_row(i)
    x = jnp.where(is_p, xp_ref[...], xs_ref[...])
    j = (i - NT_P) % TILES_PER_GRID
    first = jnp.logical_or(is_p, j == 0)
    last = jnp.logical_or(is_p, j == TILES_PER_GRID - 1)
    y = y_ref[...]
    row = lax.broadcasted_iota(jnp.int32, y.shape, 0)
    before = jnp.where(first, 0.0, yp_ref[7:8, :])
    after = jnp.where(last, 0.0, yn_ref[0:1, :])
    y_prev = jnp.where(row == 0, before, pltpu.roll(y, 1, axis=0))
    y_next = jnp.where(row == TM - 1, after, pltpu.roll(y, TM - 1, axis=0))
    cv = cv_ref[...]
    conv = gb_ref[...] * (y_prev * cv[0:1] + y * cv[1:2] + y_next * cv[2:3])
    h = _dot(conv.astype(BF16), wo_ref[0:512, :]) + _dot(at_ref[...], wo_ref[512:1024, :])
    _mixer_tail(x, h, ci, g1_ref, sh2_ref, sc2_ref, lng_ref, lnb_ref, rw_ref, rb_ref, *out_refs)


def _cd_out_kernel(x_ref, mg_ref, wo_ref, g1_ref, sh2_ref, sc2_ref, lng_ref, lnb_ref, rw_ref, rb_ref,
                   *out_refs):
    ci = _cond_row(pl.program_id(0))
    h = _dot(mg_ref[...], wo_ref[...])
    _mixer_tail(x_ref[...], h, ci, g1_ref, sh2_ref, sc2_ref, lng_ref, lnb_ref, rw_ref, rb_ref, *out_refs)


def _tail_specs(layer):
    return [_mod_spec(layer, 2), _mod_spec(layer, 3), _mod_spec(layer, 4),
            _full((1, D)), _full((1, D)), _full((D, LANES)), _full((1, LANES))]


_TAIL_OUT_SPECS = [_row_spec(D), pl.BlockSpec((TM * SUB, LANES), lambda i: (i, 0)), _row_spec(LANES),
                   _row_spec(LANES)]
_TAIL_OUT_SHAPES = [jax.ShapeDtypeStruct((T, D), F32), jax.ShapeDtypeStruct((T * SUB, LANES), F32),
                    jax.ShapeDtypeStruct((T, LANES), jnp.int32), jax.ShapeDtypeStruct((T, LANES), F32)]


def _ab_out(xp, xs, gb, y, conv_w, attn, w_out, mod, tail):
    halo_prev = pl.BlockSpec((8, 512), lambda i: (jnp.maximum(i * (TM // 8) - 1, 0), 0))
    halo_next = pl.BlockSpec((8, 512), lambda i: (jnp.minimum((i + 1) * (TM // 8), T // 8 - 1), 0))
    return pl.pallas_call(
        _ab_out_kernel,
        grid=(NT,),
        in_specs=_x_specs() + [_row_spec(512), _row_spec(512), halo_prev, halo_next, _full((3, 512)),
                               _row_spec(512), _full((D, D))] + _tail_specs(0),
        out_specs=_TAIL_OUT_SPECS,
        out_shape=_TAIL_OUT_SHAPES,
        compiler_params=_params("arbitrary"),
        name="ab_out_proj",
    )(xp, xs, gb, y, y, y, conv_w, attn, w_out, mod, mod, mod, *tail)


def _cd_out(x, merged, w_out, mod, tail):
    return pl.pallas_call(
        _cd_out_kernel,
        grid=(NT,),
        in_specs=[_row_spec(D), _row_spec(D), _full((D, D))] + _tail_specs(1),
        out_specs=_TAIL_OUT_SPECS,
        out_shape=_TAIL_OUT_SHAPES,
        compiler_params=_params("arbitrary"),
        name="cd_out_proj",
    )(x, merged, w_out, mod, mod, mod, *tail)


def _mix_down(wd_ref, wmix_ref):
    half = D_FF // 2
    for c in range(D // LANES):
        wmix_ref[c, pl.ds(0, half, stride=2), :] = wd_ref[0:half, c * LANES:(c + 1) * LANES]
        wmix_ref[c, pl.ds(1, half, stride=2), :] = wd_ref[half:D_FF, c * LANES:(c + 1) * LANES]


def _prep_expert(wgu_ref, wd_ref, wgu_bf, wmix_ref, wd_bf):
    for c in range(4):
        wgu_bf[:, c * 512:(c + 1) * 512] = wgu_ref[:, c * 512:(c + 1) * 512].astype(BF16)
    _mix_down(wd_ref, wmix_ref)
    for c in range(D // LANES):
        wd_bf[:, c * LANES:(c + 1) * LANES] = wmix_ref[c].astype(BF16)


def _expert_ffn(u, wgu_bf, bgu_ref, wd_bf, bd_ref):
    rows = u.shape[0]
    ga = _dot(u, wgu_bf[:, 0:D_FF]) + bgu_ref[:, 0:D_FF]
    gb = _dot(u, wgu_bf[:, D_FF:2 * D_FF]) + bgu_ref[:, D_FF:2 * D_FF]
    even = (lax.broadcasted_iota(jnp.int32, (rows, LANES), 1) % 2) == 0
    hid = []
    for c in range(D_FF // LANES):
        a = ga[:, c * LANES:(c + 1) * LANES]
        b = gb[:, c * LANES:(c + 1) * LANES]
        gate = jnp.where(even, a, pltpu.roll(b, 1, axis=1))
        up = jnp.where(even, pltpu.roll(a, LANES - 1, axis=1), b)
        gate = jnp.minimum(gate, SWIGLU_LIMIT)
        up = jnp.clip(up, -SWIGLU_LIMIT, SWIGLU_LIMIT)
        hid.append(((up + 1.0) * gate * jax.nn.sigmoid(SWIGLU_ALPHA * gate)).astype(BF16))
    hid = jnp.concatenate(hid, axis=1)
    return _dot(hid, wd_bf[...]) + bd_ref[...]


def _moe_plan(idx):
    e = idx[:, :TOP_K].reshape(-1)
    order = jnp.argsort(e, stable=True).astype(jnp.int32)
    experts = jnp.arange(N_EXPERTS, dtype=jnp.int32)
    counts = jnp.sum((e[:, None] == experts[None]).astype(jnp.int32), axis=0)
    tiles_e = (counts + MOE_TM - 1) // MOE_TM
    tile_end = jnp.cumsum(tiles_e)
    first_pair = jnp.cumsum(counts) - counts
    j = jnp.arange(MOE_TILES, dtype=jnp.int32)
    tile_expert = jnp.minimum(jnp.sum((j[:, None] >= tile_end[None]).astype(jnp.int32), axis=1), N_EXPERTS - 1)
    lane = jnp.arange(MOE_TM, dtype=jnp.int32)[None]
    rank = (j - (tile_end - tiles_e)[tile_expert])[:, None] * MOE_TM + lane
    valid = rank < counts[tile_expert][:, None]
    pair = order[jnp.clip(first_pair[tile_expert][:, None] + rank, 0, MOE_PAIRS - 1)]
    spread = jnp.broadcast_to(lane, rank.shape)
    tok = jnp.where(valid, pair // TOP_K, spread) * SUB
    dst = jnp.where(valid, (pair % TOP_K) * T + pair // TOP_K, MOE_PAIRS + spread) * SUB
    spare = (MOE_PAIRS + spread[:1]) * SUB
    dst_prev = jnp.concatenate([spare, dst[:-1]], axis=0)
    as_smem = lambda a: a.astype(jnp.int32)[:, None, :]
    return tile_expert, tile_end[-1:].astype(jnp.int32), as_smem(tok), as_smem(dst), as_smem(dst_prev)


def _tile_rows(ref, slot):
    return jnp.concatenate([ref[slot, pl.ds(c, MOE_TM, stride=SUB), :] for c in range(SUB)], axis=1)


def _moe_kernel(te_ref, nu_ref, tok_ref, tokn_ref, dst_ref, dstp_ref, u_hbm, wgu_ref, bgu_ref, wd_ref, bd_ref,
                y_hbm, xbuf, ybuf, wgu_bf, wmix_ref, wd_bf, gsem, ssem):
    j = pl.program_id(0)
    n_used = nu_ref[0]
    slot = j % 2
    other = 1 - slot

    def gather(idx_ref, s):
        for r in range(MOE_TM):
            src = u_hbm.at[pl.ds(pl.multiple_of(idx_ref[0, r], SUB), SUB), :]
            pltpu.make_async_copy(src, xbuf.at[s, pl.ds(r * SUB, SUB), :], gsem.at[s]).start(priority=r % 2)

    def scatter(idx_ref, s):
        for r in range(MOE_TM):
            dst = y_hbm.at[pl.ds(pl.multiple_of(idx_ref[0, r], SUB), SUB), :]
            pltpu.make_async_copy(ybuf.at[s, pl.ds(r * SUB, SUB), :], dst, ssem.at[s]).start(priority=r % 2)

    def gather_wait(s):
        pltpu.make_async_copy(u_hbm.at[pl.ds(0, MOE_TM * SUB), :], xbuf.at[s], gsem.at[s]).wait()

    def scatter_wait(s):
        pltpu.make_async_copy(ybuf.at[s], y_hbm.at[pl.ds(0, MOE_TM * SUB), :], ssem.at[s]).wait()

    @pl.when(j == 0)
    def _():
        ybuf[...] = jnp.zeros_like(ybuf)
        gather(tok_ref, 0)
        scatter(dstp_ref, 0)

    @pl.when(jnp.logical_and(j < n_used, jnp.logical_or(j == 0, te_ref[j] != te_ref[jnp.maximum(j - 1, 0)])))
    def _():
        _prep_expert(wgu_ref, wd_ref, wgu_bf, wmix_ref, wd_bf)

    @pl.when(j < n_used)
    def _():
        gather_wait(slot)
        scatter_wait(slot)
        gather(tokn_ref, other)
        scatter(dstp_ref, other)
        y = _expert_ffn(_tile_rows(xbuf, slot).astype(BF16), wgu_bf, bgu_ref, wd_bf, bd_ref)
        for c in range(SUB):
            ybuf[slot, pl.ds(c, MOE_TM, stride=SUB), :] = y[:, c * LANES:(c + 1) * LANES]

    @pl.when(j == n_used - 1)
    def _():
        scatter(dst_ref, slot)
        scatter_wait(other)
        scatter_wait(slot)
        gather_wait(other)


def _moe(layer, u2, idx, w_gate_up, b_gate_up, w_down, b_down):
    tile_expert, n_used, tok, dst, dst_prev = _moe_plan(idx)
    smem = lambda f: pl.BlockSpec((None, 1, MOE_TM), f, memory_space=pltpu.SMEM)
    grid_spec = pltpu.PrefetchScalarGridSpec(
        num_scalar_prefetch=2,
        grid=(MOE_TILES,),
        in_specs=[smem(lambda j, te, nu: (j, 0, 0)),
                  smem(lambda j, te, nu: (jnp.minimum(j + 1, MOE_TILES - 1), 0, 0)),
                  smem(lambda j, te, nu: (j, 0, 0)),
                  smem(lambda j, te, nu: (j, 0, 0)),
                  pl.BlockSpec(memory_space=pl.ANY),
                  pl.BlockSpec((None, None, D, 2 * D_FF), lambda j, te, nu: (layer, te[j], 0, 0)),
                  pl.BlockSpec((None, None, 1, 2 * D_FF), lambda j, te, nu: (layer, te[j], 0, 0)),
                  pl.BlockSpec((None, None, D_FF, D), lambda j, te, nu: (layer, te[j], 0, 0)),
                  pl.BlockSpec((None, None, 1, D), lambda j, te, nu: (layer, te[j], 0, 0))],
        out_specs=pl.BlockSpec(memory_space=pl.ANY),
        scratch_shapes=[pltpu.VMEM((2, MOE_TM * SUB, LANES), F32), pltpu.VMEM((2, MOE_TM * SUB, LANES), F32),
                        pltpu.VMEM((D, 2 * D_FF), BF16), pltpu.VMEM((D // LANES, D_FF, LANES), F32),
                        pltpu.VMEM((D_FF, D), BF16),
                        pltpu.SemaphoreType.DMA((2,)), pltpu.SemaphoreType.DMA((2,))])
    return pl.pallas_call(
        _moe_kernel,
        grid_spec=grid_spec,
        out_shape=jax.ShapeDtypeStruct(((MOE_PAIRS + MOE_TM) * SUB, LANES), F32),
        compiler_params=_params("arbitrary"),
        name="moe_experts",
    )(tile_expert, n_used, tok, tok, dst, dst_prev, u2, w_gate_up,
      b_gate_up.reshape(DEPTH, N_EXPERTS, 1, 2 * D_FF), w_down, b_down.reshape(DEPTH, N_EXPERTS, 1, D))


def _ln2_value(x1_ref, y_refs, w_ref, g2_ref, lng_ref, lnb_ref):
    ci = _cond_row(pl.program_id(0))
    w = w_ref[...]
    rows = lambda ref: jnp.concatenate([ref[pl.ds(c, TM, stride=SUB), :] for c in range(SUB)], axis=1)
    f = w[:, 0:1] * rows(y_refs[0])
    for k in range(1, TOP_K):
        f = f + w[:, k:k + 1] * rows(y_refs[k])
    z = DEEPNORM_ALPHA * x1_ref[...] + g2_ref[pl.ds(ci, 1), :] * f
    return _layer_norm(z, lng_ref[...], lnb_ref[...])


def _ln2_kernel(x1_ref, y0, y1, y2, y3, w_ref, g2_ref, lng_ref, lnb_ref, o_ref):
    o_ref[...] = _ln2_value(x1_ref, (y0, y1, y2, y3), w_ref, g2_ref, lng_ref, lnb_ref)


def _ln2_final_kernel(x1_ref, y0, y1, y2, y3, w_ref, g2_ref, lng_ref, lnb_ref, op_ref, os_ref):
    i = pl.program_id(0)
    out = _ln2_value(x1_ref, (y0, y1, y2, y3), w_ref, g2_ref, lng_ref, lnb_ref)

    @pl.when(i < NT_P)
    def _():
        op_ref[...] = out

    @pl.when(i >= NT_P)
    def _():
        os_ref[...] = out


def _ln2(layer, x1, yk, w, mod, g, b):
    y_specs = [pl.BlockSpec((TM * SUB, LANES), lambda i, _k=k: (_k * NT + i, 0)) for k in range(TOP_K)]
    in_specs = [_row_spec(D)] + y_specs + [_row_spec(LANES), _mod_spec(layer, 5), _full((1, D)), _full((1, D))]
    args = (x1, yk, yk, yk, yk, w, mod, g, b)
    if layer < DEPTH - 1:
        return pl.pallas_call(
            _ln2_kernel, grid=(NT,), in_specs=in_specs, out_specs=_row_spec(D),
            out_shape=jax.ShapeDtypeStruct((T, D), F32),
            compiler_params=_params("arbitrary"), name="post_norm2",
        )(*args)
    return pl.pallas_call(
        _ln2_final_kernel, grid=(NT,), in_specs=in_specs, out_specs=_x_specs(),
        out_shape=[jax.ShapeDtypeStruct((T_P, D), F32), jax.ShapeDtypeStruct((T_S, D), F32)],
        compiler_params=_params("arbitrary"), name="post_norm2_final",
    )(*args)


def _router_tail(l, ln1_g, ln1_b, router_w, router_b):
    rw = jnp.pad(router_w[l], ((0, 0), (0, LANES - N_EXPERTS)))
    rb = jnp.pad(router_b[l], (0, LANES - N_EXPERTS), constant_values=NEG)
    return ln1_g[l][None], ln1_b[l][None], rw, rb[None]


def kernel(x_prompt, x_sample, c, c_ctx, cache_diff_k, cache_diff_v, cache_na_k, cache_na_v, cache_gqa_k, cache_gqa_v, w_mod, b_mod, ln1_g, ln1_b, ln2_g, ln2_b, ab_w_in, ab_conv_w, ab_lambda_q1, ab_lambda_k1, ab_lambda_q2, ab_lambda_k2, ab_subln_g, ab_w_out, cd_w_in, cd_na_rpb, cd_q_norm_g, cd_k_norm_g, cd_w_out, router_w, router_b, w_gate_up, b_gate_up, w_down, b_down):
    xp = x_prompt.reshape(T_P, D)
    xs = x_sample.reshape(T_S, D)
    cond8 = jnp.concatenate([c_ctx[None], c, jnp.zeros((8 - 1 - B_S, D), F32)], axis=0)
    mod = _modulation(cond8, w_mod, b_mod)
    rope = _rope_tables()
    experts = (w_gate_up, b_gate_up, w_down, b_down)

    gb, y, q, k, v, new_diff_k, new_diff_v = _ab_in(xp, xs, mod, ab_w_in[0].astype(BF16), rope)
    lam_init = 0.8 - 0.6 * 1.0
    diff = (jnp.stack([ab_lambda_q1[0], ab_lambda_k1[0], ab_lambda_q2[0], ab_lambda_k2[0]]), ab_subln_g[0][None])
    attn = _flash_pair(q, k, v, latent=False, out_cols=512, col0=0, post="diff", diff=diff, lam_init=lam_init)
    attn = _flash_pair(q, k, v, latent=True, out_cols=512, col0=0, post="diff", diff=diff, lam_init=lam_init,
                       ctx=(cache_diff_k, cache_diff_v), ctx_mode="wide", prev=attn)
    x1, u2, idx, w = _ab_out(xp, xs, gb, y, ab_conv_w[0], attn, ab_w_out[0].astype(BF16), mod,
                             _router_tail(0, ln1_g, ln1_b, router_w, router_b))
    x = _ln2(0, x1, _moe(0, u2, idx, *experts), w, mod, ln2_g[0][None], ln2_b[0][None])

    qg = jnp.tile(cd_q_norm_g[0], 2)[None]
    kg = jnp.tile(cd_k_norm_g[0], 2)[None]
    nq, nk, nv, gq, gk, gv, new_na_k, new_na_v, new_gqa_k, new_gqa_v = _cd_in(
        x, mod, cd_w_in[0].astype(BF16), qg, kg, rope)
    merged = _flash_pair(nq, nk, nv, latent=False, out_cols=D, col0=0, post="select")
    merged = _flash_pair(gq, gk, gv, latent=False, out_cols=D, col0=2, post="select", prev=merged)
    merged = _na_latent(nq, nk, nv, cache_na_k, cache_na_v, _na_bias(cd_na_rpb[0]), merged)
    merged = _flash_pair(gq, gk, gv, latent=True, out_cols=D, col0=2, post="select",
                         ctx=(cache_gqa_k, cache_gqa_v), ctx_mode="dup", prev=merged)
    x1, u2, idx, w = _cd_out(x, merged, cd_w_out[0].astype(BF16), mod,
                             _router_tail(1, ln1_g, ln1_b, router_w, router_b))
    y_p, y_s = _ln2(1, x1, _moe(1, u2, idx, *experts), w, mod, ln2_g[1][None], ln2_b[1][None])

    return (y_p.reshape(B_P, N_P, D), y_s.reshape(B_S, N_S, D), new_diff_k, new_diff_v,
            new_na_k, new_na_v, new_gqa_k, new_gqa_v)
```

```python
import jax
import jax.numpy as jnp
from jax import lax
from jax.experimental import pallas as pl
from jax.experimental.pallas import tpu as pltpu

F32 = jnp.float32
BF16 = jnp.bfloat16

D = 1024
B_P, N_P = 16, 256
B_S, N_S = 2, 4096
PAST = 256
T_P, T_S = B_P * N_P, B_S * N_S
T = T_P + T_S
TM = 256
NT_P, NT_S, NT = T_P // TM, T_S // TM, T // TM
TILES_PER_GRID = N_S // TM
GRID_W = 64
GRID_H = N_S // GRID_W
HD = 64
DEPTH = 2
N_EXPERTS = 32
TOP_K = 4
D_FF = 1024
NA_WIN_R, NA_WIN_C = 8, 16
NA_QROWS = 4
NA_KROWS = 12
SWIGLU_LIMIT = 7.0
SWIGLU_ALPHA = 1.702
ROPE_THETA = 10000.0
DEEPNORM_ALPHA = (2 * DEPTH) ** 0.25
LN_EPS = 1e-5
RMS_EPS = 1e-6
QK_SCALE = HD ** -0.5
NEG = -1e30
MOE_TM = 256
MOE_PAIRS = TOP_K * T
MOE_TILES = MOE_PAIRS // MOE_TM + N_EXPERTS
LANES = 128
SUB = 8

VMEM_LIMIT = 56 * 1024 * 1024


def _params(*sem):
    return pltpu.CompilerParams(dimension_semantics=sem, vmem_limit_bytes=VMEM_LIMIT)


def _split(x):
    hi = x.astype(BF16)
    lo = (x - hi.astype(F32)).astype(BF16)
    return hi, lo


def _dot(a, b):
    return jnp.dot(a, b, preferred_element_type=F32)


def _dot3(a, b):
    ah, al = _split(a)
    bh, bl = _split(b)
    return _dot(ah, bh) + (_dot(ah, bl) + _dot(al, bh))


def _cond_row(i):
    return jnp.where(i < NT_P, 0, 1 + (i - NT_P) // TILES_PER_GRID)


def _layer_norm(z, g, b):
    mu = jnp.mean(z, axis=-1, keepdims=True)
    zc = z - mu
    var = jnp.mean(zc * zc, axis=-1, keepdims=True)
    return zc * lax.rsqrt(var + LN_EPS) * g + b


def _low_half(rows):
    return lax.broadcasted_iota(jnp.int32, (rows, LANES), 1) < HD


def _mod_kernel(c_ref, w_ref, b_ref, o_ref):
    c = c_ref[...]
    o_ref[...] = _dot3(c * jax.nn.sigmoid(c), w_ref[...]) + b_ref[...]


def _modulation(cond8, w_mod, b_mod):
    return pl.pallas_call(
        _mod_kernel,
        grid=(DEPTH, 6),
        in_specs=[pl.BlockSpec((8, D), lambda l, j: (0, 0)),
                  pl.BlockSpec((None, D, D), lambda l, j: (l, 0, j)),
                  pl.BlockSpec((None, 1, D), lambda l, j: (l, 0, j))],
        out_specs=pl.BlockSpec((None, 8, D), lambda l, j: (l, 0, j)),
        out_shape=jax.ShapeDtypeStruct((DEPTH, 8, 6 * D), F32),
        compiler_params=_params("arbitrary", "arbitrary"),
        name="modulation",
    )(cond8, w_mod, b_mod.reshape(DEPTH, 1, 6 * D))


def _mod_spec(layer, chunk):
    return pl.BlockSpec((None, 8, D), lambda i, _l=layer, _c=chunk: (_l, 0, _c))


def _full(shape):
    return pl.BlockSpec(shape, lambda i: (0,) * len(shape))


def _rope_tables():
    t = jnp.arange(N_S)
    pos_r = (t // GRID_W).astype(F32)
    pos_c = (t % GRID_W).astype(F32)
    half = HD // 2
    inv = ROPE_THETA ** (-jnp.arange(0, half, 2, dtype=F32) / half)
    inv_lane = jnp.tile(jnp.repeat(inv, 2), 2 * LANES // HD)
    lane = jnp.arange(LANES)
    by_row = (lane % HD) < half
    ang = jnp.where(by_row[None], pos_r[:, None], pos_c[:, None]) * inv_lane[None]
    cos = jnp.cos(ang)
    sin = jnp.sin(ang)
    even = (lane % 2) == 0
    return cos, jnp.where(even, -sin, 0.0), jnp.where(even, 0.0, sin)


def _rope(x, a, b, c):
    return x * a + pltpu.roll(x, LANES - 1, axis=1) * b + pltpu.roll(x, 1, axis=1) * c


def _rope_spec():
    return pl.BlockSpec((TM, LANES), lambda i: (jnp.maximum(i - NT_P, 0) % TILES_PER_GRID, 0))


def _x_specs():
    return [pl.BlockSpec((TM, D), lambda i: (jnp.minimum(i, NT_P - 1), 0)),
            pl.BlockSpec((TM, D), lambda i: (jnp.maximum(i - NT_P, 0), 0))]


def _cache_spec(heads, width):
    return pl.BlockSpec((None, None, heads, N_P, width), lambda i: (jnp.minimum(i, NT_P - 1), 0, 0, 0, 0))


def _row_spec(width):
    return pl.BlockSpec((TM, width), lambda i: (i, 0))


def _hm_spec(n):
    return pl.BlockSpec((n, TM, LANES), lambda i: (0, i, 0))


AB_Q0, AB_K0, AB_V0 = 1536, 2048, 2560


def _tile(p, col0, j):
    return p[:, col0 + j * LANES:col0 + (j + 1) * LANES]


def _ab_in_kernel(xp_ref, xs_ref, sh_ref, sc_ref, w_ref, ra_ref, rb_ref, rc_ref,
                  gb_ref, y_ref, q_ref, k_ref, v_ref, kc_ref, vc_ref):
    i = pl.program_id(0)
    is_p = i < NT_P
    ci = _cond_row(i)
    x = jnp.where(is_p, xp_ref[...], xs_ref[...])
    u = x * (1.0 + sc_ref[pl.ds(ci, 1), :]) + sh_ref[pl.ds(ci, 1), :]
    p = _dot(u.astype(BF16), w_ref[...])
    gb_ref[...] = p[:, 0:512]
    y_ref[...] = p[:, 512:1024] * p[:, 1024:1536]
    for h in range(4):
        v_ref[h] = _tile(p, AB_V0, h).astype(BF16)

    @pl.when(is_p)
    def _():
        for h in range(4):
            q_ref[h] = (_tile(p, AB_Q0, h) * QK_SCALE).astype(BF16)
            k_ref[h] = _tile(p, AB_K0, h).astype(BF16)
            kc_ref[h] = _tile(p, AB_K0, h)
            vc_ref[h] = _tile(p, AB_V0, h)

    @pl.when(jnp.logical_not(is_p))
    def _():
        a, b, c = ra_ref[...], rb_ref[...], rc_ref[...]
        for h in range(4):
            q_ref[h] = (_rope(_tile(p, AB_Q0, h), a, b, c) * QK_SCALE).astype(BF16)
            k_ref[h] = _rope(_tile(p, AB_K0, h), a, b, c).astype(BF16)


def _ab_in(xp, xs, mod, w_in, rope):
    hm = jax.ShapeDtypeStruct((4, T, LANES), BF16)
    cache = jax.ShapeDtypeStruct((B_P, 1, 4, N_P, LANES), F32)
    half = jax.ShapeDtypeStruct((T, 512), F32)
    return pl.pallas_call(
        _ab_in_kernel,
        grid=(NT,),
        in_specs=_x_specs() + [_mod_spec(0, 0), _mod_spec(0, 1), _full((D, 3072)),
                               _rope_spec(), _rope_spec(), _rope_spec()],
        out_specs=[_row_spec(512), _row_spec(512), _hm_spec(4), _hm_spec(4), _hm_spec(4),
                   _cache_spec(4, LANES), _cache_spec(4, LANES)],
        out_shape=[half, half, hm, hm, hm, cache, cache],
        compiler_params=_params("arbitrary"),
        name="ab_in_proj",
    )(xp, xs, mod, mod, w_in, *rope)


CD_NQ, CD_NK, CD_NV, CD_GQ, CD_GK, CD_GV = 0, 512, 1024, 1536, 2048, 2176


def _seg_mean64(s):
    r = lax.broadcasted_iota(jnp.int32, (LANES, LANES), 0) // HD
    c = lax.broadcasted_iota(jnp.int32, (LANES, LANES), 1) // HD
    seg = jnp.where(r == c, 1.0, 0.0).astype(BF16)
    hi, lo = _split(s)
    return (_dot(hi, seg) + _dot(lo, seg)) * (1.0 / HD)


def _rms64(x, g):
    return x * lax.rsqrt(_seg_mean64(x * x) + RMS_EPS) * g


def _dup_halves(x, lo):
    r = pltpu.roll(x, HD, axis=1)
    return jnp.where(lo, x, r), jnp.where(lo, r, x)


def _cd_in_kernel(x_ref, sh_ref, sc_ref, w_ref, qg_ref, kg_ref, ra_ref, rb_ref, rc_ref,
                  nq_ref, nk_ref, nv_ref, gq_ref, gk_ref, gv_ref,
                  nkc_ref, nvc_ref, gkc_ref, gvc_ref):
    i = pl.program_id(0)
    is_p = i < NT_P
    ci = _cond_row(i)
    u = x_ref[...] * (1.0 + sc_ref[pl.ds(ci, 1), :]) + sh_ref[pl.ds(ci, 1), :]
    p = _dot(u.astype(BF16), w_ref[...])
    lo = _low_half(TM)
    for j in range(4):
        nq_ref[j] = (_tile(p, CD_NQ, j) * QK_SCALE).astype(BF16)
        nk_ref[j] = _tile(p, CD_NK, j).astype(BF16)
        nv_ref[j] = _tile(p, CD_NV, j).astype(BF16)
    gq = [_rms64(_tile(p, CD_GQ, j), qg_ref[...]) for j in range(4)]
    gk = _rms64(_tile(p, CD_GK, 0), kg_ref[...])
    gv = _tile(p, CD_GV, 0)
    v0, v1 = _dup_halves(gv, lo)
    gv_ref[0] = v0.astype(BF16)
    gv_ref[1] = v1.astype(BF16)

    def emit(gq, gk):
        for j in range(4):
            gq_ref[j] = (gq[j] * QK_SCALE).astype(BF16)
        k0, k1 = _dup_halves(gk, lo)
        gk_ref[0] = k0.astype(BF16)
        gk_ref[1] = k1.astype(BF16)
        return k0, k1

    @pl.when(is_p)
    def _():
        k0, k1 = emit(gq, gk)
        gkc_ref[0] = k0[:, 0:HD]
        gkc_ref[1] = k1[:, 0:HD]
        gvc_ref[0] = v0[:, 0:HD]
        gvc_ref[1] = v1[:, 0:HD]
        for j in range(4):
            for src, dst in ((CD_NK, nkc_ref), (CD_NV, nvc_ref)):
                a, b = _dup_halves(_tile(p, src, j), lo)
                dst[2 * j] = a[:, 0:HD]
                dst[2 * j + 1] = b[:, 0:HD]

    @pl.when(jnp.logical_not(is_p))
    def _():
        a, b, c = ra_ref[...], rb_ref[...], rc_ref[...]
        emit([_rope(g, a, b, c) for g in gq], _rope(gk, a, b, c))


def _cd_in(x, mod, w_in, qg, kg, rope):
    hm4 = jax.ShapeDtypeStruct((4, T, LANES), BF16)
    hm2 = jax.ShapeDtypeStruct((2, T, LANES), BF16)
    c8 = jax.ShapeDtypeStruct((B_P, 1, 8, N_P, HD), F32)
    c2 = jax.ShapeDtypeStruct((B_P, 1, 2, N_P, HD), F32)
    return pl.pallas_call(
        _cd_in_kernel,
        grid=(NT,),
        in_specs=[_row_spec(D), _mod_spec(1, 0), _mod_spec(1, 1), _full((D, 2304)),
                  _full((1, LANES)), _full((1, LANES)), _rope_spec(), _rope_spec(), _rope_spec()],
        out_specs=[_hm_spec(4), _hm_spec(4), _hm_spec(4), _hm_spec(4), _hm_spec(2), _hm_spec(2),
                   _cache_spec(8, HD), _cache_spec(8, HD), _cache_spec(2, HD), _cache_spec(2, HD)],
        out_shape=[hm4, hm4, hm4, hm4, hm2, hm2, c8, c8, c2, c2],
        compiler_params=_params("arbitrary"),
        name="cd_in_proj",
    )(x, mod, mod, w_in, qg, kg, *rope)


def _stack_pairs(q_ref, n_q, tq):
    lo = _low_half(tq)
    parts = []
    for j in range(n_q):
        q = q_ref[j]
        zero = jnp.zeros_like(q)
        parts += [jnp.where(lo, q, zero), jnp.where(lo, zero, q)]
    return jnp.concatenate(parts, axis=0), lo


def _qk(qs, kb):
    return lax.dot_general(qs, kb, (((1,), (1,)), ((), ())), preferred_element_type=F32)


def _ctx_tile(ref, mode):
    if mode == "wide":
        x = ref[...]
    elif mode == "pair":
        x = jnp.concatenate([ref[0], ref[1]], axis=1)
    else:
        x = jnp.concatenate([ref[...], ref[...]], axis=1)
    return x.astype(BF16)


def _flash_pair_kernel(*refs, n_q, tq, nk, tk, ctx_mode, post, lam_init):
    it = iter(refs)
    q_ref, k_ref, v_ref = next(it), next(it), next(it)
    kc_ref, vc_ref = (next(it), next(it)) if ctx_mode else (None, None)
    lam_ref, g_ref = (next(it), next(it)) if post == "diff" else (None, None)
    o_ref = next(it)

    qs, lo = _stack_pairs(q_ref, n_q, tq)
    rows = 2 * n_q * tq

    def step(kb, vb, carry):
        m, l, acc = carry
        s = _qk(qs, kb)
        m_new = jnp.maximum(m, jnp.max(s, axis=1, keepdims=True))
        alpha = jnp.exp(m - m_new)
        p = jnp.exp(s - m_new)
        l = alpha * l + jnp.sum(p, axis=1, keepdims=True)
        acc = alpha * acc + _dot(p.astype(BF16), vb)
        return m_new, l, acc

    carry = (jnp.full((rows, 1), NEG, F32), jnp.zeros((rows, 1), F32), jnp.zeros((rows, LANES), F32))
    for c in range(nk // tk):
        carry = step(k_ref[c * tk:(c + 1) * tk, :], v_ref[c * tk:(c + 1) * tk, :], carry)
    if ctx_mode:
        carry = step(_ctx_tile(kc_ref, ctx_mode), _ctx_tile(vc_ref, ctx_mode), carry)
    _, l, acc = carry
    o = acc / l

    if post == "diff":
        lp = lam_ref[...]
        lam = (jnp.exp(jnp.sum(lp[0:1] * lp[1:2], axis=1, keepdims=True))
               - jnp.exp(jnp.sum(lp[2:3] * lp[3:4], axis=1, keepdims=True)) + lam_init)
        a = o[0:tq] - lam * o[tq:2 * tq]
        ms = jnp.mean(a * a, axis=-1, keepdims=True)
        o_ref[...] = (a * lax.rsqrt(ms + RMS_EPS) * g_ref[...] * (1.0 - lam_init)).astype(o_ref.dtype)
    else:
        for j in range(n_q):
            o_ref[:, j * LANES:(j + 1) * LANES] = jnp.where(
                lo, o[2 * j * tq:(2 * j + 1) * tq], o[(2 * j + 1) * tq:(2 * j + 2) * tq]).astype(o_ref.dtype)


def _flash_pair(q, k, v, *, latent, out_cols, col0, post, ctx=None, ctx_mode=None, diff=None,
                lam_init=0.0, prev=None):
    groups = k.shape[0]
    n_q = q.shape[0] // groups
    if latent:
        tq, nk, tk = TM, N_S, 1024
        grid = (B_S, groups, TILES_PER_GRID)
        qrow = lambda b, g, i: NT_P + b * TILES_PER_GRID + i
        krow = lambda b, g, i: T_P // N_S + b
    else:
        tq, nk, tk = N_P, N_P, N_P
        grid = (B_P, groups, 1)
        qrow = lambda b, g, i: b
        krow = lambda b, g, i: b
    in_specs = [pl.BlockSpec((n_q, tq, LANES), lambda b, g, i: (g, qrow(b, g, i), 0)),
                pl.BlockSpec((None, nk, LANES), lambda b, g, i: (g, krow(b, g, i), 0)),
                pl.BlockSpec((None, nk, LANES), lambda b, g, i: (g, krow(b, g, i), 0))]
    args = [q, k, v]
    if ctx is not None:
        if ctx_mode == "wide":
            spec = pl.BlockSpec((None, None, None, PAST, LANES), lambda b, g, i: (b, 0, g, 0, 0))
        elif ctx_mode == "pair":
            spec = pl.BlockSpec((None, None, 2, PAST, HD), lambda b, g, i: (b, 0, g, 0, 0))
        else:
            spec = pl.BlockSpec((None, None, None, PAST, HD), lambda b, g, i: (b, 0, g, 0, 0))
        in_specs += [spec, spec]
        args += list(ctx)
    if diff is not None:
        in_specs += [pl.BlockSpec((4, HD), lambda b, g, i: (0, 0)),
                     pl.BlockSpec((1, LANES), lambda b, g, i: (0, 0))]
        args += list(diff)
    aliases = {}
    if prev is not None:
        aliases = {len(args): 0}
        in_specs.append(pl.BlockSpec(memory_space=pl.ANY))
        args.append(prev)

    def kernel(*refs):
        if prev is not None:
            refs = refs[:-2] + refs[-1:]
        _flash_pair_kernel(*refs, n_q=n_q, tq=tq, nk=nk, tk=tk, ctx_mode=ctx_mode if ctx is not None else None,
                           post=post, lam_init=lam_init)

    return pl.pallas_call(
        kernel,
        grid=grid,
        in_specs=in_specs,
        out_specs=pl.BlockSpec((tq, n_q * LANES), lambda b, g, i: (qrow(b, g, i), col0 + g)),
        out_shape=jax.ShapeDtypeStruct((T, out_cols), BF16),
        input_output_aliases=aliases,
        compiler_params=_params("arbitrary", "arbitrary", "arbitrary"),
        name="attn_" + post + ("_latent" if latent else "_context") + str(n_q),
    )(*args)


NA_TQ = NA_QROWS * GRID_W
NA_TK = NA_KROWS * GRID_W


NA_BLOCK_POS = ((0, 0), (NA_QROWS, 0), (GRID_H - NA_QROWS, GRID_H - NA_KROWS))
N_DR = 2 * NA_WIN_R - 1
N_DC = 2 * NA_WIN_C - 1


def _na_bias_kernel(rpb_ref, o_ref):
    qc = lax.broadcasted_iota(jnp.int32, (GRID_W, LANES), 0)
    lane = lax.broadcasted_iota(jnp.int32, (GRID_W, LANES), 1)
    kc = lane % GRID_W
    cs = jnp.clip(qc - NA_WIN_C // 2, 0, GRID_W - NA_WIN_C)
    col_ok = jnp.logical_and(kc >= cs, kc < cs + NA_WIN_C)
    lo = lane < GRID_W
    neg = jnp.full((GRID_W, LANES), NEG, F32)
    for head in range(2):
        toeplitz = []
        for dr in range(N_DR):
            r = jnp.broadcast_to(rpb_ref[head, dr:dr + 1, :], (GRID_W, LANES))
            t = jnp.where(lo, pltpu.roll(r, LANES - (NA_WIN_C - 1), axis=1, stride=1, stride_axis=0),
                          pltpu.roll(r, GRID_W - (NA_WIN_C - 1), axis=1, stride=1, stride_axis=0))
            toeplitz.append(jnp.where(col_ok, t, neg))
        for pos, (r0, k0) in enumerate(NA_BLOCK_POS):
            for i in range(NA_QROWS):
                qr = r0 + i
                rs = min(max(qr - NA_WIN_R // 2, 0), GRID_H - NA_WIN_R)
                tiles = [toeplitz[k0 + j - qr + NA_WIN_R - 1] if rs <= k0 + j < rs + NA_WIN_R else neg
                         for j in range(NA_KROWS)]
                for jp in range(NA_KROWS // 2):
                    o_ref[pos, pl.ds(head * NA_TQ + i * GRID_W, GRID_W), pl.ds(jp * LANES, LANES)] = jnp.where(
                        lo, tiles[2 * jp], tiles[2 * jp + 1])


def _na_bias(rpb):
    rpb_pad = jnp.pad(rpb, ((0, 0), (0, 16 - N_DR), (0, LANES - N_DC)), constant_values=NEG)
    return pl.pallas_call(
        _na_bias_kernel,
        grid=(4,),
        in_specs=[pl.BlockSpec((2, 16, LANES), lambda g: (g, 0, 0))],
        out_specs=pl.BlockSpec((3, None, 2 * NA_TQ, NA_TK), lambda g: (0, g, 0, 0)),
        out_shape=jax.ShapeDtypeStruct((3, 4, 2 * NA_TQ, NA_TK), F32),
        compiler_params=_params("arbitrary"),
        name="na_bias_table",
    )(rpb_pad)


def _na_kernel(q_ref, k_ref, v_ref, kc_ref, vc_ref, bm_ref, _, o_ref):
    i = pl.program_id(2)
    qs, lo = _stack_pairs(q_ref, 1, NA_TQ)
    k0 = jnp.clip(i * NA_QROWS - NA_WIN_R // 2, 0, GRID_H - NA_KROWS)
    start = pl.multiple_of(k0 * GRID_W, GRID_W)
    kw = k_ref[pl.ds(start, NA_TK), :]
    vw = v_ref[pl.ds(start, NA_TK), :]
    s_w = _qk(qs, kw) + bm_ref[...]
    s_c = _qk(qs, _ctx_tile(kc_ref, "pair"))
    m = jnp.maximum(jnp.max(s_w, axis=1, keepdims=True), jnp.max(s_c, axis=1, keepdims=True))
    p_w = jnp.exp(s_w - m)
    p_c = jnp.exp(s_c - m)
    l = jnp.sum(p_w, axis=1, keepdims=True) + jnp.sum(p_c, axis=1, keepdims=True)
    o = (_dot(p_w.astype(BF16), vw) + _dot(p_c.astype(BF16), _ctx_tile(vc_ref, "pair"))) / l
    o_ref[...] = jnp.where(lo, o[0:NA_TQ], o[NA_TQ:2 * NA_TQ]).astype(o_ref.dtype)


def _na_latent(q, k, v, kc, vc, bm, prev):
    nblk = N_S // NA_TQ
    qrow = lambda b, g, i: T_P // NA_TQ + b * nblk + i
    krow = lambda b, g, i: T_P // N_S + b
    cfg = lambda i: jnp.where(i == 0, 0, jnp.where(i == nblk - 1, 2, 1))
    ctx_spec = pl.BlockSpec((None, None, 2, PAST, HD), lambda b, g, i: (b, 0, g, 0, 0))
    return pl.pallas_call(
        _na_kernel,
        grid=(B_S, 4, nblk),
        in_specs=[pl.BlockSpec((1, NA_TQ, LANES), lambda b, g, i: (g, qrow(b, g, i), 0)),
                  pl.BlockSpec((None, N_S, LANES), lambda b, g, i: (g, krow(b, g, i), 0)),
                  pl.BlockSpec((None, N_S, LANES), lambda b, g, i: (g, krow(b, g, i), 0)),
                  ctx_spec, ctx_spec,
                  pl.BlockSpec((None, None, 2 * NA_TQ, NA_TK), lambda b, g, i: (cfg(i), g, 0, 0)),
                  pl.BlockSpec(memory_space=pl.ANY)],
        out_specs=pl.BlockSpec((NA_TQ, LANES), lambda b, g, i: (qrow(b, g, i), g)),
        out_shape=jax.ShapeDtypeStruct((T, D), BF16),
        input_output_aliases={6: 0},
        compiler_params=_params("arbitrary", "arbitrary", "arbitrary"),
        name="attn_window_latent",
    )(q, k, v, kc, vc, bm, prev)


def _top4(logits):
    lane = lax.broadcasted_iota(jnp.int32, logits.shape, 1).astype(F32)
    rest = logits
    tops, firsts = [], []
    for _ in range(TOP_K):
        m = jnp.max(rest, axis=1, keepdims=True)
        first = jnp.min(jnp.where(rest == m, lane, float(LANES)), axis=1, keepdims=True)
        tops.append(m)
        firsts.append(first)
        rest = jnp.where(lane == first, -jnp.inf, rest)
    es = [jnp.exp(m - tops[0]) for m in tops]
    denom = es[0] + es[1] + es[2] + es[3]
    idx = jnp.zeros_like(logits)
    w = jnp.zeros_like(logits)
    for k in range(TOP_K):
        idx = jnp.where(lane == float(k), firsts[k], idx)
        w = jnp.where(lane == float(k), es[k] / denom, w)
    return idx.astype(jnp.int32), w


def _mixer_tail(x, h, ci, g1_ref, sh2_ref, sc2_ref, lng_ref, lnb_ref, rw_ref, rb_ref,
                x1_ref, u2_ref, idx_ref, w_ref):
    x1 = _layer_norm(DEEPNORM_ALPHA * x + g1_ref[pl.ds(ci, 1), :] * h, lng_ref[...], lnb_ref[...])
    x1_ref[...] = x1
    u2 = x1 * (1.0 + sc2_ref[pl.ds(ci, 1), :]) + sh2_ref[pl.ds(ci, 1), :]
    for c in range(SUB):
        u2_ref[pl.ds(c, TM, stride=SUB), :] = u2[:, c * LANES:(c + 1) * LANES]
    idx_ref[...], w_ref[...] = _top4(_dot3(u2, rw_ref[...]) + rb_ref[...])


def _ab_out_kernel(xp_ref, xs_ref, gb_ref, y_ref, yp_ref, yn_ref, cv_ref, at_ref, wo_ref,
                   g1_ref, sh2_ref, sc2_ref, lng_ref, lnb_ref, rw_ref, rb_ref, *out_refs):
    i = pl.program_id(0)
    is_p = i < NT_P
    ci = _cond_row(i)
    x = jnp.where(is_p, xp_ref[...], xs_ref[...])
    j = (i - NT_P) % TILES_PER_GRID
    first = jnp.logical_or(is_p, j == 0)
    last = jnp.logical_or(is_p, j == TILES_PER_GRID - 1)
    y = y_ref[...]
    row = lax.broadcasted_iota(jnp.int32, y.shape, 0)
    before = jnp.where(first, 0.0, yp_ref[7:8, :])
    after = jnp.where(last, 0.0, yn_ref[0:1, :])
    y_prev = jnp.where(row == 0, before, pltpu.roll(y, 1, axis=0))
    y_next = jnp.where(row == TM - 1, after, pltpu.roll(y, TM - 1, axis=0))
    cv = cv_ref[...]
    conv = gb_ref[...] * (y_prev * cv[0:1] + y * cv[1:2] + y_next * cv[2:3])
    h = _dot(conv.astype(BF16), wo_ref[0:512, :]) + _dot(at_ref[...], wo_ref[512:1024, :])
    _mixer_tail(x, h, ci, g1_ref, sh2_ref, sc2_ref, lng_ref, lnb_ref, rw_ref, rb_ref, *out_refs)


def _cd_out_kernel(x_ref, mg_ref, wo_ref, g1_ref, sh2_ref, sc2_ref, lng_ref, lnb_ref, rw_ref, rb_ref,
                   *out_refs):
    ci = _cond_row(pl.program_id(0))
    h = _dot(mg_ref[...], wo_ref[...])
    _mixer_tail(x_ref[...], h, ci, g1_ref, sh2_ref, sc2_ref, lng_ref, lnb_ref, rw_ref, rb_ref, *out_refs)


def _tail_specs(layer):
    return [_mod_spec(layer, 2), _mod_spec(layer, 3), _mod_spec(layer, 4),
            _full((1, D)), _full((1, D)), _full((D, LANES)), _full((1, LANES))]


_TAIL_OUT_SPECS = [_row_spec(D), pl.BlockSpec((TM * SUB, LANES), lambda i: (i, 0)), _row_spec(LANES),
                   _row_spec(LANES)]
_TAIL_OUT_SHAPES = [jax.ShapeDtypeStruct((T, D), F32), jax.ShapeDtypeStruct((T * SUB, LANES), F32),
                    jax.ShapeDtypeStruct((T, LANES), jnp.int32), jax.ShapeDtypeStruct((T, LANES), F32)]


def _ab_out(xp, xs, gb, y, conv_w, attn, w_out, mod, tail):
    halo_prev = pl.BlockSpec((8, 512), lambda i: (jnp.maximum(i * (TM // 8) - 1, 0), 0))
    halo_next = pl.BlockSpec((8, 512), lambda i: (jnp.minimum((i + 1) * (TM // 8), T // 8 - 1), 0))
    return pl.pallas_call(
        _ab_out_kernel,
        grid=(NT,),
        in_specs=_x_specs() + [_row_spec(512), _row_spec(512), halo_prev, halo_next, _full((3, 512)),
                               _row_spec(512), _full((D, D))] + _tail_specs(0),
        out_specs=_TAIL_OUT_SPECS,
        out_shape=_TAIL_OUT_SHAPES,
        compiler_params=_params("arbitrary"),
        name="ab_out_proj",
    )(xp, xs, gb, y, y, y, conv_w, attn, w_out, mod, mod, mod, *tail)


def _cd_out(x, merged, w_out, mod, tail):
    return pl.pallas_call(
        _cd_out_kernel,
        grid=(NT,),
        in_specs=[_row_spec(D), _row_spec(D), _full((D, D))] + _tail_specs(1),
        out_specs=_TAIL_OUT_SPECS,
        out_shape=_TAIL_OUT_SHAPES,
        compiler_params=_params("arbitrary"),
        name="cd_out_proj",
    )(x, merged, w_out, mod, mod, mod, *tail)


def _mix_down(wd_ref, wmix_ref):
    half = D_FF // 2
    for c in range(D // LANES):
        wmix_ref[c, pl.ds(0, half, stride=2), :] = wd_ref[0:half, c * LANES:(c + 1) * LANES]
        wmix_ref[c, pl.ds(1, half, stride=2), :] = wd_ref[half:D_FF, c * LANES:(c + 1) * LANES]


def _prep_expert(wgu_ref, wd_ref, wgu_bf, wmix_ref, wd_bf):
    for c in range(4):
        wgu_bf[:, c * 512:(c + 1) * 512] = wgu_ref[:, c * 512:(c + 1) * 512].astype(BF16)
    _mix_down(wd_ref, wmix_ref)
    for c in range(D // LANES):
        wd_bf[:, c * LANES:(c + 1) * LANES] = wmix_ref[c].astype(BF16)


def _expert_ffn(u, wgu_bf, bgu_ref, wd_bf, bd_ref):
    rows = u.shape[0]
    ga = _dot(u, wgu_bf[:, 0:D_FF]) + bgu_ref[:, 0:D_FF]
    gb = _dot(u, wgu_bf[:, D_FF:2 * D_FF]) + bgu_ref[:, D_FF:2 * D_FF]
    even = (lax.broadcasted_iota(jnp.int32, (rows, LANES), 1) % 2) == 0
    hid = []
    for c in range(D_FF // LANES):
        a = ga[:, c * LANES:(c + 1) * LANES]
        b = gb[:, c * LANES:(c + 1) * LANES]
        gate = jnp.where(even, a, pltpu.roll(b, 1, axis=1))
        up = jnp.where(even, pltpu.roll(a, LANES - 1, axis=1), b)
        gate = jnp.minimum(gate, SWIGLU_LIMIT)
        up = jnp.clip(up, -SWIGLU_LIMIT, SWIGLU_LIMIT)
        hid.append(((up + 1.0) * gate * jax.nn.sigmoid(SWIGLU_ALPHA * gate)).astype(BF16))
    hid = jnp.concatenate(hid, axis=1)
    return _dot(hid, wd_bf[...]) + bd_ref[...]


def _moe_plan(idx):
    e = idx[:, :TOP_K].reshape(-1)
    order = jnp.argsort(e, stable=True).astype(jnp.int32)
    experts = jnp.arange(N_EXPERTS, dtype=jnp.int32)
    counts = jnp.sum((e[:, None] == experts[None]).astype(jnp.int32), axis=0)
    tiles_e = (counts + MOE_TM - 1) // MOE_TM
    tile_end = jnp.cumsum(tiles_e)
    first_pair = jnp.cumsum(counts) - counts
    j = jnp.arange(MOE_TILES, dtype=jnp.int32)
    tile_expert = jnp.minimum(jnp.sum((j[:, None] >= tile_end[None]).astype(jnp.int32), axis=1), N_EXPERTS - 1)
    lane = jnp.arange(MOE_TM, dtype=jnp.int32)[None]
    rank = (j - (tile_end - tiles_e)[tile_expert])[:, None] * MOE_TM + lane
    valid = rank < counts[tile_expert][:, None]
    pair = order[jnp.clip(first_pair[tile_expert][:, None] + rank, 0, MOE_PAIRS - 1)]
    spread = jnp.broadcast_to(lane, rank.shape)
    tok = jnp.where(valid, pair // TOP_K, spread)
    dst = jnp.where(valid, (pair % TOP_K) * T + pair // TOP_K, MOE_PAIRS + spread)
    spare = MOE_PAIRS + spread[:1]
    dst_prev = jnp.concatenate([spare, dst[:-1]], axis=0)
    as_smem = lambda a: a.astype(jnp.int32)[:, None, :]
    return tile_expert, tile_end[-1:].astype(jnp.int32), as_smem(tok), as_smem(dst), as_smem(dst_prev)


def _tile_rows(ref, slot):
    return jnp.concatenate([ref[slot, pl.ds(c, MOE_TM, stride=SUB), :] for c in range(SUB)], axis=1)


def _moe_kernel(te_ref, nu_ref, tok_ref, tokn_ref, dst_ref, dstp_ref, u_hbm, wgu_ref, bgu_ref, wd_ref, bd_ref,
                y_hbm, xbuf, ybuf, wgu_bf, wmix_ref, wd_bf, gsem, ssem):
    j = pl.program_id(0)
    n_used = nu_ref[0]

    def gather(idx_ref, s):
        for r in range(MOE_TM):
            pltpu.make_async_copy(u_hbm.at[idx_ref[0, r]], xbuf.at[s, pl.ds(r * SUB, SUB), :],
                                  gsem.at[s]).start(priority=r % 2)

    def scatter(idx_ref, s):
        for r in range(MOE_TM):
            pltpu.make_async_copy(ybuf.at[s, pl.ds(r * SUB, SUB), :], y_hbm.at[idx_ref[0, r]],
                                  ssem.at[s]).start(priority=r % 2)

    def gather_wait(s):
        pltpu.make_async_copy(ybuf.at[s], xbuf.at[s], gsem.at[s]).wait()

    def scatter_wait(s):
        pltpu.make_async_copy(xbuf.at[s], ybuf.at[s], ssem.at[s]).wait()

    @pl.when(j == 0)
    def _():
        ybuf[...] = jnp.zeros_like(ybuf)
        gather(tok_ref, 0)
        scatter(dstp_ref, 0)

    @pl.when(jnp.logical_and(j < n_used, jnp.logical_or(j == 0, te_ref[j] != te_ref[jnp.maximum(j - 1, 0)])))
    def _():
        _prep_expert(wgu_ref, wd_ref, wgu_bf, wmix_ref, wd_bf)

    def tile_step(slot):
        other = 1 - slot
        gather_wait(slot)
        scatter_wait(slot)
        scatter(dstp_ref, other)
        x = _tile_rows(xbuf, slot).astype(BF16)
        gather(tokn_ref, other)
        y = _expert_ffn(x, wgu_bf, bgu_ref, wd_bf, bd_ref)
        for c in range(SUB):
            ybuf[slot, pl.ds(c, MOE_TM, stride=SUB), :] = y[:, c * LANES:(c + 1) * LANES]

    def drain(slot):
        other = 1 - slot
        scatter(dst_ref, slot)
        scatter_wait(other)
        scatter_wait(slot)
        gather_wait(other)

    for slot in (0, 1):
        @pl.when(jnp.logical_and(j < n_used, j % 2 == slot))
        def _(slot=slot):
            tile_step(slot)

        @pl.when(jnp.logical_and(j == n_used - 1, j % 2 == slot))
        def _(slot=slot):
            drain(slot)


def _moe(layer, u2, idx, w_gate_up, b_gate_up, w_down, b_down):
    tile_expert, n_used, tok, dst, dst_prev = _moe_plan(idx)
    smem = lambda f: pl.BlockSpec((None, 1, MOE_TM), f, memory_space=pltpu.SMEM)
    grid_spec = pltpu.PrefetchScalarGridSpec(
        num_scalar_prefetch=2,
        grid=(MOE_TILES,),
        in_specs=[smem(lambda j, te, nu: (j, 0, 0)),
                  smem(lambda j, te, nu: (jnp.minimum(j + 1, MOE_TILES - 1), 0, 0)),
                  smem(lambda j, te, nu: (j, 0, 0)),
                  smem(lambda j, te, nu: (j, 0, 0)),
                  pl.BlockSpec(memory_space=pl.ANY),
                  pl.BlockSpec((None, None, D, 2 * D_FF), lambda j, te, nu: (layer, te[j], 0, 0)),
                  pl.BlockSpec((None, None, 1, 2 * D_FF), lambda j, te, nu: (layer, te[j], 0, 0)),
                  pl.BlockSpec((None, None, D_FF, D), lambda j, te, nu: (layer, te[j], 0, 0)),
                  pl.BlockSpec((None, None, 1, D), lambda j, te, nu: (layer, te[j], 0, 0))],
        out_specs=pl.BlockSpec(memory_space=pl.ANY),
        scratch_shapes=[pltpu.VMEM((2, MOE_TM * SUB, LANES), F32), pltpu.VMEM((2, MOE_TM * SUB, LANES), F32),
                        pltpu.VMEM((D, 2 * D_FF), BF16), pltpu.VMEM((D // LANES, D_FF, LANES), F32),
                        pltpu.VMEM((D_FF, D), BF16),
                        pltpu.SemaphoreType.DMA((2,)), pltpu.SemaphoreType.DMA((2,))])
    return pl.pallas_call(
        _moe_kernel,
        grid_spec=grid_spec,
        out_shape=jax.ShapeDtypeStruct((MOE_PAIRS + MOE_TM, SUB, LANES), F32),
        compiler_params=_params("arbitrary"),
        name="moe_experts",
    )(tile_expert, n_used, tok, tok, dst, dst_prev, u2.reshape(T, SUB, LANES), w_gate_up,
      b_gate_up.reshape(DEPTH, N_EXPERTS, 1, 2 * D_FF), w_down, b_down.reshape(DEPTH, N_EXPERTS, 1, D))


def _ln2_value(x1_ref, y_refs, w_ref, g2_ref, lng_ref, lnb_ref):
    ci = _cond_row(pl.program_id(0))
    w = w_ref[...]
    rows = lambda ref: jnp.concatenate([ref[pl.ds(c, TM, stride=SUB), :] for c in range(SUB)], axis=1)
    f = w[:, 0:1] * rows(y_refs[0])
    for k in range(1, TOP_K):
        f = f + w[:, k:k + 1] * rows(y_refs[k])
    z = DEEPNORM_ALPHA * x1_ref[...] + g2_ref[pl.ds(ci, 1), :] * f
    return _layer_norm(z, lng_ref[...], lnb_ref[...])


def _ln2_kernel(x1_ref, y0, y1, y2, y3, w_ref, g2_ref, lng_ref, lnb_ref, o_ref):
    o_ref[...] = _ln2_value(x1_ref, (y0, y1, y2, y3), w_ref, g2_ref, lng_ref, lnb_ref)


def _ln2_final_kernel(x1_ref, y0, y1, y2, y3, w_ref, g2_ref, lng_ref, lnb_ref, op_ref, os_ref):
    i = pl.program_id(0)
    out = _ln2_value(x1_ref, (y0, y1, y2, y3), w_ref, g2_ref, lng_ref, lnb_ref)

    @pl.when(i < NT_P)
    def _():
        op_ref[...] = out

    @pl.when(i >= NT_P)
    def _():
        os_ref[...] = out


def _ln2(layer, x1, yk, w, mod, g, b):
    y_specs = [pl.BlockSpec((TM * SUB, LANES), lambda i, _k=k: (_k * NT + i, 0)) for k in range(TOP_K)]
    in_specs = [_row_spec(D)] + y_specs + [_row_spec(LANES), _mod_spec(layer, 5), _full((1, D)), _full((1, D))]
    yk = yk.reshape((MOE_PAIRS + MOE_TM) * SUB, LANES)
    args = (x1, yk, yk, yk, yk, w, mod, g, b)
    if layer < DEPTH - 1:
        return pl.pallas_call(
            _ln2_kernel, grid=(NT,), in_specs=in_specs, out_specs=_row_spec(D),
            out_shape=jax.ShapeDtypeStruct((T, D), F32),
            compiler_params=_params("arbitrary"), name="post_norm2",
        )(*args)
    return pl.pallas_call(
        _ln2_final_kernel, grid=(NT,), in_specs=in_specs, out_specs=_x_specs(),
        out_shape=[jax.ShapeDtypeStruct((T_P, D), F32), jax.ShapeDtypeStruct((T_S, D), F32)],
        compiler_params=_params("arbitrary"), name="post_norm2_final",
    )(*args)


def _router_tail(l, ln1_g, ln1_b, router_w, router_b):
    rw = jnp.pad(router_w[l], ((0, 0), (0, LANES - N_EXPERTS)))
    rb = jnp.pad(router_b[l], (0, LANES - N_EXPERTS), constant_values=NEG)
    return ln1_g[l][None], ln1_b[l][None], rw, rb[None]


def kernel(x_prompt, x_sample, c, c_ctx, cache_diff_k, cache_diff_v, cache_na_k, cache_na_v, cache_gqa_k, cache_gqa_v, w_mod, b_mod, ln1_g, ln1_b, ln2_g, ln2_b, ab_w_in, ab_conv_w, ab_lambda_q1, ab_lambda_k1, ab_lambda_q2, ab_lambda_k2, ab_subln_g, ab_w_out, cd_w_in, cd_na_rpb, cd_q_norm_g, cd_k_norm_g, cd_w_out, router_w, router_b, w_gate_up, b_gate_up, w_down, b_down):
    xp = x_prompt.reshape(T_P, D)
    xs = x_sample.reshape(T_S, D)
    cond8 = jnp.concatenate([c_ctx[None], c, jnp.zeros((8 - 1 - B_S, D), F32)], axis=0)
    mod = _modulation(cond8, w_mod, b_mod)
    rope = _rope_tables()
    experts = (w_gate_up, b_gate_up, w_down, b_down)

    gb, y, q, k, v, new_diff_k, new_diff_v = _ab_in(xp, xs, mod, ab_w_in[0].astype(BF16), rope)
    lam_init = 0.8 - 0.6 * 1.0
    diff = (jnp.stack([ab_lambda_q1[0], ab_lambda_k1[0], ab_lambda_q2[0], ab_lambda_k2[0]]), ab_subln_g[0][None])
    attn = _flash_pair(q, k, v, latent=False, out_cols=512, col0=0, post="diff", diff=diff, lam_init=lam_init)
    attn = _flash_pair(q, k, v, latent=True, out_cols=512, col0=0, post="diff", diff=diff, lam_init=lam_init,
                       ctx=(cache_diff_k, cache_diff_v), ctx_mode="wide", prev=attn)
    x1, u2, idx, w = _ab_out(xp, xs, gb, y, ab_conv_w[0], attn, ab_w_out[0].astype(BF16), mod,
                             _router_tail(0, ln1_g, ln1_b, router_w, router_b))
    x = _ln2(0, x1, _moe(0, u2, idx, *experts), w, mod, ln2_g[0][None], ln2_b[0][None])

    qg = jnp.tile(cd_q_norm_g[0], 2)[None]
    kg = jnp.tile(cd_k_norm_g[0], 2)[None]
    nq, nk, nv, gq, gk, gv, new_na_k, new_na_v, new_gqa_k, new_gqa_v = _cd_in(
        x, mod, cd_w_in[0].astype(BF16), qg, kg, rope)
    merged = _flash_pair(nq, nk, nv, latent=False, out_cols=D, col0=0, post="select")
    merged = _flash_pair(gq, gk, gv, latent=False, out_cols=D, col0=2, post="select", prev=merged)
    merged = _na_latent(nq, nk, nv, cache_na_k, cache_na_v, _na_bias(cd_na_rpb[0]), merged)
    merged = _flash_pair(gq, gk, gv, latent=True, out_cols=D, col0=2, post="select",
                         ctx=(cache_gqa_k, cache_gqa_v), ctx_mode="dup", prev=merged)
    x1, u2, idx, w = _cd_out(x, merged, cd_w_out[0].astype(BF16), mod,
                             _router_tail(1, ln1_g, ln1_b, router_w, router_b))
    y_p, y_s = _ln2(1, x1, _moe(1, u2, idx, *experts), w, mod, ln2_g[1][None], ln2_b[1][None])

    return (y_p.reshape(B_P, N_P, D), y_s.reshape(B_S, N_S, D), new_diff_k, new_diff_v,
            new_na_k, new_na_v, new_gqa_k, new_gqa_v)
```

```python
import jax
import jax.numpy as jnp
import numpy as np
from jax import lax
from jax.experimental import pallas as pl
from jax.experimental.pallas import tpu as pltpu

F32 = jnp.float32
BF16 = jnp.bfloat16

D = 1024
B_P, N_P = 16, 256
B_S, N_S = 2, 4096
PAST = 256
T_P, T_S = B_P * N_P, B_S * N_S
T = T_P + T_S
TM = 256
NT_P, NT_S, NT = T_P // TM, T_S // TM, T // TM
TILES_PER_GRID = N_S // TM
GRID_W = 64
GRID_H = N_S // GRID_W
HD = 64
DEPTH = 2
N_EXPERTS = 32
TOP_K = 4
D_FF = 1024
NA_WIN_R, NA_WIN_C = 8, 16
NA_QROWS = 4
NA_KROWS = 12
SWIGLU_LIMIT = 7.0
SWIGLU_ALPHA = 1.702
ROPE_THETA = 10000.0
DEEPNORM_ALPHA = (2 * DEPTH) ** 0.25
LN_EPS = 1e-5
RMS_EPS = 1e-6
QK_SCALE = HD ** -0.5
NEG = -1e30
MOE_TM = 256
MOE_PAIRS = TOP_K * T
MOE_TILES = MOE_PAIRS // MOE_TM + N_EXPERTS
PAIRS_TM = TOP_K * TM
RUN_SIZES = tuple(TM >> b for b in range(TM.bit_length()))
LANES = 128
SUB = 8

VMEM_LIMIT = 56 * 1024 * 1024


def _params(*sem):
    return pltpu.CompilerParams(dimension_semantics=sem, vmem_limit_bytes=VMEM_LIMIT)


def _split(x):
    hi = x.astype(BF16)
    lo = (x - hi.astype(F32)).astype(BF16)
    return hi, lo


def _dot(a, b):
    return jnp.dot(a, b, preferred_element_type=F32)


def _dot3(a, b):
    ah, al = _split(a)
    bh, bl = _split(b)
    return _dot(ah, bh) + (_dot(ah, bl) + _dot(al, bh))


def _cond_row(i):
    return jnp.where(i < NT_P, 0, 1 + (i - NT_P) // TILES_PER_GRID)


def _layer_norm(z, g, b):
    mu = jnp.mean(z, axis=-1, keepdims=True)
    zc = z - mu
    var = jnp.mean(zc * zc, axis=-1, keepdims=True)
    return zc * lax.rsqrt(var + LN_EPS) * g + b


def _low_half(rows):
    return lax.broadcasted_iota(jnp.int32, (rows, LANES), 1) < HD


def _mod_kernel(c_ref, w_ref, b_ref, o_ref):
    c = c_ref[...]
    o_ref[...] = _dot3(c * jax.nn.sigmoid(c), w_ref[...]) + b_ref[...]


def _modulation(cond8, w_mod, b_mod):
    return pl.pallas_call(
        _mod_kernel,
        grid=(DEPTH, 6),
        in_specs=[pl.BlockSpec((8, D), lambda l, j: (0, 0)),
                  pl.BlockSpec((None, D, D), lambda l, j: (l, 0, j)),
                  pl.BlockSpec((None, 1, D), lambda l, j: (l, 0, j))],
        out_specs=pl.BlockSpec((None, 8, D), lambda l, j: (l, 0, j)),
        out_shape=jax.ShapeDtypeStruct((DEPTH, 8, 6 * D), F32),
        compiler_params=_params("arbitrary", "arbitrary"),
        name="modulation",
    )(cond8, w_mod, b_mod.reshape(DEPTH, 1, 6 * D))


def _mod_spec(layer, chunk):
    return pl.BlockSpec((None, 8, D), lambda i, _l=layer, _c=chunk: (_l, 0, _c))


def _full(shape):
    return pl.BlockSpec(shape, lambda i: (0,) * len(shape))


def _rope_tables():
    t = np.arange(N_S)
    half = HD // 2
    inv = ROPE_THETA ** (-np.arange(0, half, 2, dtype=np.float64) / half)
    inv_lane = np.tile(np.repeat(inv, 2), 2 * LANES // HD)
    lane = np.arange(LANES)
    by_row = (lane % HD) < half
    ang = np.where(by_row[None], (t // GRID_W)[:, None], (t % GRID_W)[:, None]) * inv_lane[None]
    cos, sin = np.cos(ang), np.sin(ang)
    even = (lane % 2) == 0
    return tuple(jnp.asarray(a, F32) for a in (cos, np.where(even, -sin, 0.0), np.where(even, 0.0, sin)))


def _rope(x, a, b, c):
    return x * a + pltpu.roll(x, LANES - 1, axis=1) * b + pltpu.roll(x, 1, axis=1) * c


def _rope_spec():
    return pl.BlockSpec((TM, LANES), lambda i: (jnp.maximum(i - NT_P, 0) % TILES_PER_GRID, 0))


def _x_specs():
    return [pl.BlockSpec((TM, D), lambda i: (jnp.minimum(i, NT_P - 1), 0)),
            pl.BlockSpec((TM, D), lambda i: (jnp.maximum(i - NT_P, 0), 0))]


def _cache_spec(heads, width):
    return pl.BlockSpec((None, None, heads, N_P, width), lambda i: (jnp.minimum(i, NT_P - 1), 0, 0, 0, 0))


def _row_spec(width):
    return pl.BlockSpec((TM, width), lambda i: (i, 0))


def _hm_spec(n):
    return pl.BlockSpec((n, TM, LANES), lambda i: (0, i, 0))


AB_Q0, AB_K0, AB_V0 = 1536, 2048, 2560


def _tile(p, col0, j):
    return p[:, col0 + j * LANES:col0 + (j + 1) * LANES]


def _ab_in_kernel(xp_ref, xs_ref, sh_ref, sc_ref, w_ref, ra_ref, rb_ref, rc_ref,
                  gb_ref, y_ref, q_ref, k_ref, v_ref, kc_ref, vc_ref):
    i = pl.program_id(0)
    is_p = i < NT_P
    ci = _cond_row(i)
    x = jnp.where(is_p, xp_ref[...], xs_ref[...])
    u = x * (1.0 + sc_ref[pl.ds(ci, 1), :]) + sh_ref[pl.ds(ci, 1), :]
    p = _dot(u.astype(BF16), w_ref[...])
    gb_ref[...] = p[:, 0:512]
    y_ref[...] = p[:, 512:1024] * p[:, 1024:1536]
    for h in range(4):
        v_ref[h] = _tile(p, AB_V0, h).astype(BF16)

    @pl.when(is_p)
    def _():
        for h in range(4):
            q_ref[h] = (_tile(p, AB_Q0, h) * QK_SCALE).astype(BF16)
            k_ref[h] = _tile(p, AB_K0, h).astype(BF16)
            kc_ref[h] = _tile(p, AB_K0, h)
            vc_ref[h] = _tile(p, AB_V0, h)

    @pl.when(jnp.logical_not(is_p))
    def _():
        a, b, c = ra_ref[...], rb_ref[...], rc_ref[...]
        for h in range(4):
            q_ref[h] = (_rope(_tile(p, AB_Q0, h), a, b, c) * QK_SCALE).astype(BF16)
            k_ref[h] = _rope(_tile(p, AB_K0, h), a, b, c).astype(BF16)


def _ab_in(xp, xs, mod, w_in, rope):
    hm = jax.ShapeDtypeStruct((4, T, LANES), BF16)
    cache = jax.ShapeDtypeStruct((B_P, 1, 4, N_P, LANES), F32)
    half = jax.ShapeDtypeStruct((T, 512), F32)
    return pl.pallas_call(
        _ab_in_kernel,
        grid=(NT,),
        in_specs=_x_specs() + [_mod_spec(0, 0), _mod_spec(0, 1), _full((D, 3072)),
                               _rope_spec(), _rope_spec(), _rope_spec()],
        out_specs=[_row_spec(512), _row_spec(512), _hm_spec(4), _hm_spec(4), _hm_spec(4),
                   _cache_spec(4, LANES), _cache_spec(4, LANES)],
        out_shape=[half, half, hm, hm, hm, cache, cache],
        compiler_params=_params("arbitrary"),
        name="ab_in_proj",
    )(xp, xs, mod, mod, w_in, *rope)


CD_NQ, CD_NK, CD_NV, CD_GQ, CD_GK, CD_GV = 0, 512, 1024, 1536, 2048, 2176


def _seg_mean64(s):
    r = lax.broadcasted_iota(jnp.int32, (LANES, LANES), 0) // HD
    c = lax.broadcasted_iota(jnp.int32, (LANES, LANES), 1) // HD
    seg = jnp.where(r == c, 1.0, 0.0).astype(BF16)
    hi, lo = _split(s)
    return (_dot(hi, seg) + _dot(lo, seg)) * (1.0 / HD)


def _rms64(x, g):
    return x * lax.rsqrt(_seg_mean64(x * x) + RMS_EPS) * g


def _dup_halves(x, lo):
    r = pltpu.roll(x, HD, axis=1)
    return jnp.where(lo, x, r), jnp.where(lo, r, x)


def _cd_in_kernel(x_ref, sh_ref, sc_ref, w_ref, qg_ref, kg_ref, ra_ref, rb_ref, rc_ref,
                  nq_ref, nk_ref, nv_ref, gq_ref, gk_ref, gv_ref,
                  nkc_ref, nvc_ref, gkc_ref, gvc_ref):
    i = pl.program_id(0)
    is_p = i < NT_P
    ci = _cond_row(i)
    u = x_ref[...] * (1.0 + sc_ref[pl.ds(ci, 1), :]) + sh_ref[pl.ds(ci, 1), :]
    p = _dot(u.astype(BF16), w_ref[...])
    lo = _low_half(TM)
    for j in range(4):
        nq_ref[j] = (_tile(p, CD_NQ, j) * QK_SCALE).astype(BF16)
        nk_ref[j] = _tile(p, CD_NK, j).astype(BF16)
        nv_ref[j] = _tile(p, CD_NV, j).astype(BF16)
    gq = [_rms64(_tile(p, CD_GQ, j), qg_ref[...]) for j in range(4)]
    gk = _rms64(_tile(p, CD_GK, 0), kg_ref[...])
    gv = _tile(p, CD_GV, 0)
    v0, v1 = _dup_halves(gv, lo)
    gv_ref[0] = v0.astype(BF16)
    gv_ref[1] = v1.astype(BF16)

    def emit(gq, gk):
        for j in range(4):
            gq_ref[j] = (gq[j] * QK_SCALE).astype(BF16)
        k0, k1 = _dup_halves(gk, lo)
        gk_ref[0] = k0.astype(BF16)
        gk_ref[1] = k1.astype(BF16)
        return k0, k1

    @pl.when(is_p)
    def _():
        k0, k1 = emit(gq, gk)
        gkc_ref[0] = k0[:, 0:HD]
        gkc_ref[1] = k1[:, 0:HD]
        gvc_ref[0] = v0[:, 0:HD]
        gvc_ref[1] = v1[:, 0:HD]
        for j in range(4):
            for src, dst in ((CD_NK, nkc_ref), (CD_NV, nvc_ref)):
                a, b = _dup_halves(_tile(p, src, j), lo)
                dst[2 * j] = a[:, 0:HD]
                dst[2 * j + 1] = b[:, 0:HD]

    @pl.when(jnp.logical_not(is_p))
    def _():
        a, b, c = ra_ref[...], rb_ref[...], rc_ref[...]
        emit([_rope(g, a, b, c) for g in gq], _rope(gk, a, b, c))


def _cd_in(x, mod, w_in, qg, kg, rope):
    hm4 = jax.ShapeDtypeStruct((4, T, LANES), BF16)
    hm2 = jax.ShapeDtypeStruct((2, T, LANES), BF16)
    c8 = jax.ShapeDtypeStruct((B_P, 1, 8, N_P, HD), F32)
    c2 = jax.ShapeDtypeStruct((B_P, 1, 2, N_P, HD), F32)
    return pl.pallas_call(
        _cd_in_kernel,
        grid=(NT,),
        in_specs=[_row_spec(D), _mod_spec(1, 0), _mod_spec(1, 1), _full((D, 2304)),
                  _full((1, LANES)), _full((1, LANES)), _rope_spec(), _rope_spec(), _rope_spec()],
        out_specs=[_hm_spec(4), _hm_spec(4), _hm_spec(4), _hm_spec(4), _hm_spec(2), _hm_spec(2),
                   _cache_spec(8, HD), _cache_spec(8, HD), _cache_spec(2, HD), _cache_spec(2, HD)],
        out_shape=[hm4, hm4, hm4, hm4, hm2, hm2, c8, c8, c2, c2],
        compiler_params=_params("arbitrary"),
        name="cd_in_proj",
    )(x, mod, mod, w_in, qg, kg, *rope)


def _stack_pairs(q_ref, n_q, tq):
    lo = _low_half(tq)
    parts = []
    for j in range(n_q):
        q = q_ref[j]
        zero = jnp.zeros_like(q)
        parts += [jnp.where(lo, q, zero), jnp.where(lo, zero, q)]
    return jnp.concatenate(parts, axis=0), lo


def _qk(qs, kb):
    return lax.dot_general(qs, kb, (((1,), (1,)), ((), ())), preferred_element_type=F32)


def _ctx_tile(ref, mode):
    if mode == "wide":
        x = ref[...]
    elif mode == "pair":
        x = jnp.concatenate([ref[0], ref[1]], axis=1)
    else:
        x = jnp.concatenate([ref[...], ref[...]], axis=1)
    return x.astype(BF16)


def _flash_pair_kernel(*refs, n_q, tq, nk, tk, ctx_mode, post, lam_init):
    it = iter(refs)
    q_ref, k_ref, v_ref = next(it), next(it), next(it)
    kc_ref, vc_ref = (next(it), next(it)) if ctx_mode else (None, None)
    lam_ref, g_ref = (next(it), next(it)) if post == "diff" else (None, None)
    o_ref = next(it)

    qs, lo = _stack_pairs(q_ref, n_q, tq)
    rows = 2 * n_q * tq

    def step(kb, vb, carry):
        m, l, acc = carry
        s = _qk(qs, kb)
        m_new = jnp.maximum(m, jnp.max(s, axis=1, keepdims=True))
        alpha = jnp.exp(m - m_new)
        p = jnp.exp(s - m_new)
        l = alpha * l + jnp.sum(p, axis=1, keepdims=True)
        acc = alpha * acc + _dot(p.astype(BF16), vb)
        return m_new, l, acc

    carry = (jnp.full((rows, 1), NEG, F32), jnp.zeros((rows, 1), F32), jnp.zeros((rows, LANES), F32))
    for c in range(nk // tk):
        carry = step(k_ref[c * tk:(c + 1) * tk, :], v_ref[c * tk:(c + 1) * tk, :], carry)
    if ctx_mode:
        carry = step(_ctx_tile(kc_ref, ctx_mode), _ctx_tile(vc_ref, ctx_mode), carry)
    _, l, acc = carry
    o = acc / l

    if post == "diff":
        lp = lam_ref[...]
        lam = (jnp.exp(jnp.sum(lp[0:1] * lp[1:2], axis=1, keepdims=True))
               - jnp.exp(jnp.sum(lp[2:3] * lp[3:4], axis=1, keepdims=True)) + lam_init)
        a = o[0:tq] - lam * o[tq:2 * tq]
        ms = jnp.mean(a * a, axis=-1, keepdims=True)
        o_ref[...] = (a * lax.rsqrt(ms + RMS_EPS) * g_ref[...] * (1.0 - lam_init)).astype(o_ref.dtype)
    else:
        for j in range(n_q):
            o_ref[:, j * LANES:(j + 1) * LANES] = jnp.where(
                lo, o[2 * j * tq:(2 * j + 1) * tq], o[(2 * j + 1) * tq:(2 * j + 2) * tq]).astype(o_ref.dtype)


def _flash_pair(q, k, v, *, latent, out_cols, col0, post, ctx=None, ctx_mode=None, diff=None,
                lam_init=0.0, prev=None):
    groups = k.shape[0]
    n_q = q.shape[0] // groups
    if latent:
        tq, nk, tk = TM, N_S, 1024
        grid = (B_S, groups, TILES_PER_GRID)
        qrow = lambda b, g, i: NT_P + b * TILES_PER_GRID + i
        krow = lambda b, g, i: T_P // N_S + b
    else:
        tq, nk, tk = N_P, N_P, N_P
        grid = (B_P, groups, 1)
        qrow = lambda b, g, i: b
        krow = lambda b, g, i: b
    in_specs = [pl.BlockSpec((n_q, tq, LANES), lambda b, g, i: (g, qrow(b, g, i), 0)),
                pl.BlockSpec((None, nk, LANES), lambda b, g, i: (g, krow(b, g, i), 0)),
                pl.BlockSpec((None, nk, LANES), lambda b, g, i: (g, krow(b, g, i), 0))]
    args = [q, k, v]
    if ctx is not None:
        if ctx_mode == "wide":
            spec = pl.BlockSpec((None, None, None, PAST, LANES), lambda b, g, i: (b, 0, g, 0, 0))
        elif ctx_mode == "pair":
            spec = pl.BlockSpec((None, None, 2, PAST, HD), lambda b, g, i: (b, 0, g, 0, 0))
        else:
            spec = pl.BlockSpec((None, None, None, PAST, HD), lambda b, g, i: (b, 0, g, 0, 0))
        in_specs += [spec, spec]
        args += list(ctx)
    if diff is not None:
        in_specs += [pl.BlockSpec((4, HD), lambda b, g, i: (0, 0)),
                     pl.BlockSpec((1, LANES), lambda b, g, i: (0, 0))]
        args += list(diff)
    aliases = {}
    if prev is not None:
        aliases = {len(args): 0}
        in_specs.append(pl.BlockSpec(memory_space=pl.ANY))
        args.append(prev)

    def kernel(*refs):
        if prev is not None:
            refs = refs[:-2] + refs[-1:]
        _flash_pair_kernel(*refs, n_q=n_q, tq=tq, nk=nk, tk=tk, ctx_mode=ctx_mode if ctx is not None else None,
                           post=post, lam_init=lam_init)

    return pl.pallas_call(
        kernel,
        grid=grid,
        in_specs=in_specs,
        out_specs=pl.BlockSpec((tq, n_q * LANES), lambda b, g, i: (qrow(b, g, i), col0 + g)),
        out_shape=jax.ShapeDtypeStruct((T, out_cols), BF16),
        input_output_aliases=aliases,
        compiler_params=_params("arbitrary", "arbitrary", "arbitrary"),
        name="attn_" + post + ("_latent" if latent else "_context") + str(n_q),
    )(*args)


NA_TQ = NA_QROWS * GRID_W
NA_TK = NA_KROWS * GRID_W


NA_BLOCK_POS = ((0, 0), (NA_QROWS, 0), (GRID_H - NA_QROWS, GRID_H - NA_KROWS))
N_DR = 2 * NA_WIN_R - 1
N_DC = 2 * NA_WIN_C - 1


def _na_bias_kernel(rpb_ref, o_ref):
    qc = lax.broadcasted_iota(jnp.int32, (GRID_W, LANES), 0)
    lane = lax.broadcasted_iota(jnp.int32, (GRID_W, LANES), 1)
    kc = lane % GRID_W
    cs = jnp.clip(qc - NA_WIN_C // 2, 0, GRID_W - NA_WIN_C)
    col_ok = jnp.logical_and(kc >= cs, kc < cs + NA_WIN_C)
    lo = lane < GRID_W
    neg = jnp.full((GRID_W, LANES), NEG, F32)
    for head in range(2):
        toeplitz = []
        for dr in range(N_DR):
            r = jnp.broadcast_to(rpb_ref[head, dr:dr + 1, :], (GRID_W, LANES))
            t = jnp.where(lo, pltpu.roll(r, LANES - (NA_WIN_C - 1), axis=1, stride=1, stride_axis=0),
                          pltpu.roll(r, GRID_W - (NA_WIN_C - 1), axis=1, stride=1, stride_axis=0))
            toeplitz.append(jnp.where(col_ok, t, neg))
        for pos, (r0, k0) in enumerate(NA_BLOCK_POS):
            for i in range(NA_QROWS):
                qr = r0 + i
                rs = min(max(qr - NA_WIN_R // 2, 0), GRID_H - NA_WIN_R)
                tiles = [toeplitz[k0 + j - qr + NA_WIN_R - 1] if rs <= k0 + j < rs + NA_WIN_R else neg
                         for j in range(NA_KROWS)]
                for jp in range(NA_KROWS // 2):
                    o_ref[pos, pl.ds(head * NA_TQ + i * GRID_W, GRID_W), pl.ds(jp * LANES, LANES)] = jnp.where(
                        lo, tiles[2 * jp], tiles[2 * jp + 1])


def _na_bias(rpb):
    rpb_pad = jnp.pad(rpb, ((0, 0), (0, 16 - N_DR), (0, LANES - N_DC)), constant_values=NEG)
    return pl.pallas_call(
        _na_bias_kernel,
        grid=(4,),
        in_specs=[pl.BlockSpec((2, 16, LANES), lambda g: (g, 0, 0))],
        out_specs=pl.BlockSpec((3, None, 2 * NA_TQ, NA_TK), lambda g: (0, g, 0, 0)),
        out_shape=jax.ShapeDtypeStruct((3, 4, 2 * NA_TQ, NA_TK), F32),
        compiler_params=_params("arbitrary"),
        name="na_bias_table",
    )(rpb_pad)


def _na_kernel(q_ref, k_ref, v_ref, kc_ref, vc_ref, bm_ref, _, o_ref):
    i = pl.program_id(2)
    qs, lo = _stack_pairs(q_ref, 1, NA_TQ)
    k0 = jnp.clip(i * NA_QROWS - NA_WIN_R // 2, 0, GRID_H - NA_KROWS)
    start = pl.multiple_of(k0 * GRID_W, GRID_W)
    kw = k_ref[pl.ds(start, NA_TK), :]
    vw = v_ref[pl.ds(start, NA_TK), :]
    s_w = _qk(qs, kw) + bm_ref[...]
    s_c = _qk(qs, _ctx_tile(kc_ref, "pair"))
    m = jnp.maximum(jnp.max(s_w, axis=1, keepdims=True), jnp.max(s_c, axis=1, keepdims=True))
    p_w = jnp.exp(s_w - m)
    p_c = jnp.exp(s_c - m)
    l = jnp.sum(p_w, axis=1, keepdims=True) + jnp.sum(p_c, axis=1, keepdims=True)
    o = (_dot(p_w.astype(BF16), vw) + _dot(p_c.astype(BF16), _ctx_tile(vc_ref, "pair"))) / l
    o_ref[...] = jnp.where(lo, o[0:NA_TQ], o[NA_TQ:2 * NA_TQ]).astype(o_ref.dtype)


def _na_latent(q, k, v, kc, vc, bm, prev):
    nblk = N_S // NA_TQ
    qrow = lambda b, g, i: T_P // NA_TQ + b * nblk + i
    krow = lambda b, g, i: T_P // N_S + b
    cfg = lambda i: jnp.where(i == 0, 0, jnp.where(i == nblk - 1, 2, 1))
    ctx_spec = pl.BlockSpec((None, None, 2, PAST, HD), lambda b, g, i: (b, 0, g, 0, 0))
    return pl.pallas_call(
        _na_kernel,
        grid=(B_S, 4, nblk),
        in_specs=[pl.BlockSpec((1, NA_TQ, LANES), lambda b, g, i: (g, qrow(b, g, i), 0)),
                  pl.BlockSpec((None, N_S, LANES), lambda b, g, i: (g, krow(b, g, i), 0)),
                  pl.BlockSpec((None, N_S, LANES), lambda b, g, i: (g, krow(b, g, i), 0)),
                  ctx_spec, ctx_spec,
                  pl.BlockSpec((None, None, 2 * NA_TQ, NA_TK), lambda b, g, i: (cfg(i), g, 0, 0)),
                  pl.BlockSpec(memory_space=pl.ANY)],
        out_specs=pl.BlockSpec((NA_TQ, LANES), lambda b, g, i: (qrow(b, g, i), g)),
        out_shape=jax.ShapeDtypeStruct((T, D), BF16),
        input_output_aliases={6: 0},
        compiler_params=_params("arbitrary", "arbitrary", "arbitrary"),
        name="attn_window_latent",
    )(q, k, v, kc, vc, bm, prev)


def _top4(logits):
    lane = lax.broadcasted_iota(jnp.int32, logits.shape, 1).astype(F32)
    rest = logits
    tops, firsts = [], []
    for _ in range(TOP_K):
        m = jnp.max(rest, axis=1, keepdims=True)
        first = jnp.min(jnp.where(rest == m, lane, float(LANES)), axis=1, keepdims=True)
        tops.append(m)
        firsts.append(first)
        rest = jnp.where(lane == first, -jnp.inf, rest)
    es = [jnp.exp(m - tops[0]) for m in tops]
    denom = es[0] + es[1] + es[2] + es[3]
    w = jnp.zeros_like(logits)
    for k in range(TOP_K):
        w = jnp.where(lane == float(k), es[k] / denom, w)
    return firsts, w


def _local_sort(firsts):
    lane = lax.broadcasted_iota(jnp.int32, (TM, LANES), 1).astype(F32)
    hots = [lane == f for f in firsts]
    sel = jnp.zeros((TM, LANES), F32)
    for hot in hots:
        sel = jnp.where(hot, 1.0, sel)
    r = lax.broadcasted_iota(jnp.int32, (TM, TM), 0)
    c = lax.broadcasted_iota(jnp.int32, (TM, TM), 1)
    earlier = _dot(jnp.where(c < r, 1.0, 0.0).astype(BF16), sel.astype(BF16))
    cnt = jnp.sum(sel, axis=0, keepdims=True)
    r = lax.broadcasted_iota(jnp.int32, (LANES, LANES), 0)
    c = lax.broadcasted_iota(jnp.int32, (LANES, LANES), 1)
    start = _dot(jnp.broadcast_to(cnt, (SUB, LANES)).astype(BF16), jnp.where(r < c, 1.0, 0.0).astype(BF16))[0:1]
    place = start + earlier
    pos = jnp.zeros((TM, LANES), F32)
    for k, hot in enumerate(hots):
        pos = jnp.where(lane == float(k), jnp.sum(jnp.where(hot, place, 0.0), axis=1, keepdims=True), pos)
    return pos.astype(jnp.int32), cnt.astype(jnp.int32)


def _mixer_tail(x, h, ci, g1_ref, sh2_ref, sc2_ref, lng_ref, lnb_ref, rw_ref, rb_ref,
                x1_ref, u2_ref, pos_ref, w_ref, cnt_ref):
    x1 = _layer_norm(DEEPNORM_ALPHA * x + g1_ref[pl.ds(ci, 1), :] * h, lng_ref[...], lnb_ref[...])
    x1_ref[...] = x1
    u2 = x1 * (1.0 + sc2_ref[pl.ds(ci, 1), :]) + sh2_ref[pl.ds(ci, 1), :]
    u2_ref[...] = u2.astype(BF16)
    firsts, w_ref[...] = _top4(_dot3(u2, rw_ref[...]) + rb_ref[...])
    pos, cnt = _local_sort(firsts)
    pos_ref[...] = pos
    cnt_ref[...] = jnp.broadcast_to(cnt, (SUB, LANES))


def _ab_out_kernel(xp_ref, xs_ref, gb_ref, y_ref, yp_ref, yn_ref, cv_ref, at_ref, wo_ref,
                   g1_ref, sh2_ref, sc2_ref, lng_ref, lnb_ref, rw_ref, rb_ref, *out_refs):
    i = pl.program_id(0)
    is_p = i < NT_P
    ci = _cond_row(i)
    x = jnp.where(is_p, xp_ref[...], xs_ref[...])
    j = (i - NT_P) % TILES_PER_GRID
    first = jnp.logical_or(is_p, j == 0)
    last = jnp.logical_or(is_p, j == TILES_PER_GRID - 1)
    y = y_ref[...]
    row = lax.broadcasted_iota(jnp.int32, y.shape, 0)
    before = jnp.where(first, 0.0, yp_ref[7:8, :])
    after = jnp.where(last, 0.0, yn_ref[0:1, :])
    y_prev = jnp.where(row == 0, before, pltpu.roll(y, 1, axis=0))
    y_next = jnp.where(row == TM - 1, after, pltpu.roll(y, TM - 1, axis=0))
    cv = cv_ref[...]
    conv = gb_ref[...] * (y_prev * cv[0:1] + y * cv[1:2] + y_next * cv[2:3])
    h = _dot(conv.astype(BF16), wo_ref[0:512, :]) + _dot(at_ref[...], wo_ref[512:1024, :])
    _mixer_tail(x, h, ci, g1_ref, sh2_ref, sc2_ref, lng_ref, lnb_ref, rw_ref, rb_ref, *out_refs)


def _cd_out_kernel(x_ref, mg_ref, wo_ref, g1_ref, sh2_ref, sc2_ref, lng_ref, lnb_ref, rw_ref, rb_ref,
                   *out_refs):
    ci = _cond_row(pl.program_id(0))
    h = _dot(mg_ref[...], wo_ref[...])
    _mixer_tail(x_ref[...], h, ci, g1_ref, sh2_ref, sc2_ref, lng_ref, lnb_ref, rw_ref, rb_ref, *out_refs)


def _tail_specs(layer):
    return [_mod_spec(layer, 2), _mod_spec(layer, 3), _mod_spec(layer, 4),
            _full((1, D)), _full((1, D)), _full((D, LANES)), _full((1, LANES))]


_TAIL_OUT_SPECS = [_row_spec(D), _row_spec(D), _row_spec(LANES), _row_spec(LANES),
                   pl.BlockSpec((None, SUB, LANES), lambda i: (i, 0, 0))]
_TAIL_OUT_SHAPES = [jax.ShapeDtypeStruct((T, D), F32), jax.ShapeDtypeStruct((T, D), BF16),
                    jax.ShapeDtypeStruct((T, LANES), jnp.int32), jax.ShapeDtypeStruct((T, LANES), F32),
                    jax.ShapeDtypeStruct((NT, SUB, LANES), jnp.int32)]


def _ab_out(xp, xs, gb, y, conv_w, attn, w_out, mod, tail):
    halo_prev = pl.BlockSpec((8, 512), lambda i: (jnp.maximum(i * (TM // 8) - 1, 0), 0))
    halo_next = pl.BlockSpec((8, 512), lambda i: (jnp.minimum((i + 1) * (TM // 8), T // 8 - 1), 0))
    return pl.pallas_call(
        _ab_out_kernel,
        grid=(NT,),
        in_specs=_x_specs() + [_row_spec(512), _row_spec(512), halo_prev, halo_next, _full((3, 512)),
                               _row_spec(512), _full((D, D))] + _tail_specs(0),
        out_specs=_TAIL_OUT_SPECS,
        out_shape=_TAIL_OUT_SHAPES,
        compiler_params=_params("arbitrary"),
        name="ab_out_proj",
    )(xp, xs, gb, y, y, y, conv_w, attn, w_out, mod, mod, mod, *tail)


def _cd_out(x, merged, w_out, mod, tail):
    return pl.pallas_call(
        _cd_out_kernel,
        grid=(NT,),
        in_specs=[_row_spec(D), _row_spec(D), _full((D, D))] + _tail_specs(1),
        out_specs=_TAIL_OUT_SPECS,
        out_shape=_TAIL_OUT_SHAPES,
        compiler_params=_params("arbitrary"),
        name="cd_out_proj",
    )(x, merged, w_out, mod, mod, mod, *tail)


def _mix_down(wd_ref, wmix_ref):
    half = D_FF // 2
    for c in range(D // LANES):
        wmix_ref[c, pl.ds(0, half, stride=2), :] = wd_ref[0:half, c * LANES:(c + 1) * LANES]
        wmix_ref[c, pl.ds(1, half, stride=2), :] = wd_ref[half:D_FF, c * LANES:(c + 1) * LANES]


def _prep_expert(wgu_ref, wd_ref, wgu_bf, wmix_ref, wd_bf):
    for c in range(4):
        wgu_bf[:, c * 512:(c + 1) * 512] = wgu_ref[:, c * 512:(c + 1) * 512].astype(BF16)
    _mix_down(wd_ref, wmix_ref)
    for c in range(D // LANES):
        wd_bf[:, c * LANES:(c + 1) * LANES] = wmix_ref[c].astype(BF16)


def _expert_ffn(u, wgu_bf, bgu_ref, wd_bf, bd_ref):
    rows = u.shape[0]
    ga = _dot(u, wgu_bf[:, 0:D_FF]) + bgu_ref[:, 0:D_FF]
    gb = _dot(u, wgu_bf[:, D_FF:2 * D_FF]) + bgu_ref[:, D_FF:2 * D_FF]
    even = (lax.broadcasted_iota(jnp.int32, (rows, LANES), 1) % 2) == 0
    hid = []
    for c in range(D_FF // LANES):
        a = ga[:, c * LANES:(c + 1) * LANES]
        b = gb[:, c * LANES:(c + 1) * LANES]
        gate = jnp.where(even, a, pltpu.roll(b, 1, axis=1))
        up = jnp.where(even, pltpu.roll(a, LANES - 1, axis=1), b)
        gate = jnp.minimum(gate, SWIGLU_LIMIT)
        up = jnp.clip(up, -SWIGLU_LIMIT, SWIGLU_LIMIT)
        hid.append(((up + 1.0) * gate * jax.nn.sigmoid(SWIGLU_ALPHA * gate)).astype(BF16))
    hid = jnp.concatenate(hid, axis=1)
    return _dot(hid, wd_bf[...]) + bd_ref[...]


def _moe_plan(cnt):
    c = cnt[:, 0, :N_EXPERTS]
    counts = jnp.sum(c, axis=0)
    tiles_e = (counts + MOE_TM - 1) // MOE_TM
    tile_end = jnp.cumsum(tiles_e)
    first_row = (tile_end - tiles_e) * MOE_TM
    run_start = first_row[None] + jnp.cumsum(c, axis=0) - c
    j = jnp.arange(MOE_TILES, dtype=jnp.int32)
    tile_expert = jnp.minimum(jnp.sum((j[:, None] >= tile_end[None]).astype(jnp.int32), axis=1), N_EXPERTS - 1)
    i32 = lambda a: a.astype(jnp.int32)
    return dict(tile_expert=i32(tile_expert), n_used=i32(tile_end[-1:]),
                run_start=i32(run_start)[:, None, :], run_len=i32(c)[:, None, :],
                pad_start=i32(first_row + counts)[None, None, :], pad_len=i32(tiles_e * MOE_TM - counts)[None, None, :])


def _for_each_run(start_ref, len_ref, copy):
    def body(e, local):
        n = len_ref[0, e]
        g = start_ref[0, e]
        for size in RUN_SIZES:
            covered = jnp.bitwise_and(n, -2 * size)

            @pl.when(jnp.bitwise_and(n, size) != 0)
            def _(covered=covered, size=size):
                copy(local + covered, g + covered, size)
        return local + n
    lax.fori_loop(0, N_EXPERTS, body, jnp.int32(0))


def _rows(ref, start, size):
    start = 0 if isinstance(start, int) and start == 0 else pl.multiple_of(start * SUB, SUB)
    return ref.at[pl.ds(start, size * SUB), :]


def _from_tiles(ref, rows):
    return jnp.concatenate([ref[pl.ds(c, rows, stride=SUB), :] for c in range(SUB)], axis=1)


def _to_tiles(ref, x):
    for c in range(SUB):
        ref[pl.ds(c, x.shape[0], stride=SUB), :] = x[:, c * LANES:(c + 1) * LANES]


def _run_spec(index):
    return pl.BlockSpec((None, 1, N_EXPERTS), index, memory_space=pltpu.SMEM)


def _dispatch_kernel(rs_ref, rl_ref, ps_ref, pn_ref, u_ref, pos_ref, xs_hbm, stage, zeros, sem, zsem):
    i = pl.program_id(0)
    slot = i % 2

    def wait(s):
        pltpu.make_async_copy(stage.at[s], stage.at[s], sem.at[s]).wait()

    @pl.when(i == 0)
    def _():
        zeros[...] = jnp.zeros_like(zeros)

        def pad(local, g, size):
            cp = pltpu.make_async_copy(_rows(zeros, 0, size), _rows(xs_hbm, g, size), zsem.at[0])
            cp.start()
            cp.wait()
        _for_each_run(ps_ref, pn_ref, pad)

    @pl.when(i >= 2)
    def _():
        wait(slot)

    place = pos_ref[...].astype(F32).T
    s = lax.broadcasted_iota(jnp.int32, (PAIRS_TM, TM), 0).astype(F32)
    onehot = jnp.zeros((PAIRS_TM, TM), F32)
    for k in range(TOP_K):
        onehot = jnp.where(s == place[k:k + 1, :], 1.0, onehot)
    _to_tiles(stage.at[slot], _dot(onehot.astype(BF16), u_ref[...]))
    _for_each_run(rs_ref, rl_ref, lambda l, g, size: pltpu.make_async_copy(
        _rows(stage.at[slot], l, size), _rows(xs_hbm, g, size), sem.at[slot]).start())

    @pl.when(i == NT - 1)
    def _():
        wait(1 - slot)
        wait(slot)


def _dispatch(u2, pos, plan):
    return pl.pallas_call(
        _dispatch_kernel,
        grid=(NT,),
        in_specs=[_run_spec(lambda i: (i, 0, 0)), _run_spec(lambda i: (i, 0, 0)),
                  _run_spec(lambda i: (0, 0, 0)), _run_spec(lambda i: (0, 0, 0)),
                  _row_spec(D), _row_spec(LANES)],
        out_specs=pl.BlockSpec(memory_space=pl.ANY),
        out_shape=jax.ShapeDtypeStruct((MOE_TILES * MOE_TM * SUB, LANES), F32),
        scratch_shapes=[pltpu.VMEM((2, PAIRS_TM * SUB, LANES), F32), pltpu.VMEM((TM * SUB, LANES), F32),
                        pltpu.SemaphoreType.DMA((2,)), pltpu.SemaphoreType.DMA((1,))],
        compiler_params=_params("arbitrary"),
        name="moe_dispatch",
    )(plan["run_start"], plan["run_len"], plan["pad_start"], plan["pad_len"], u2, pos)


def _moe_kernel(te_ref, nu_ref, x_ref, wgu_ref, bgu_ref, wd_ref, bd_ref, y_ref, wgu_bf, wmix_ref, wd_bf):
    j = pl.program_id(0)
    used = j < nu_ref[0]

    @pl.when(jnp.logical_and(used, jnp.logical_or(j == 0, te_ref[j] != te_ref[jnp.maximum(j - 1, 0)])))
    def _():
        _prep_expert(wgu_ref, wd_ref, wgu_bf, wmix_ref, wd_bf)

    @pl.when(used)
    def _():
        x = _from_tiles(x_ref, MOE_TM).astype(BF16)
        _to_tiles(y_ref, _expert_ffn(x, wgu_bf, bgu_ref, wd_bf, bd_ref))


def _moe(layer, xs, plan, w_gate_up, b_gate_up, w_down, b_down):
    rows = pl.BlockSpec((MOE_TM * SUB, LANES), lambda j, te, nu: (jnp.minimum(j, nu[0] - 1), 0))
    grid_spec = pltpu.PrefetchScalarGridSpec(
        num_scalar_prefetch=2,
        grid=(MOE_TILES,),
        in_specs=[rows,
                  pl.BlockSpec((None, None, D, 2 * D_FF), lambda j, te, nu: (layer, te[j], 0, 0)),
                  pl.BlockSpec((None, None, 1, 2 * D_FF), lambda j, te, nu: (layer, te[j], 0, 0)),
                  pl.BlockSpec((None, None, D_FF, D), lambda j, te, nu: (layer, te[j], 0, 0)),
                  pl.BlockSpec((None, None, 1, D), lambda j, te, nu: (layer, te[j], 0, 0))],
        out_specs=rows,
        scratch_shapes=[pltpu.VMEM((D, 2 * D_FF), BF16), pltpu.VMEM((D // LANES, D_FF, LANES), F32),
                        pltpu.VMEM((D_FF, D), BF16)])
    return pl.pallas_call(
        _moe_kernel,
        grid_spec=grid_spec,
        out_shape=jax.ShapeDtypeStruct((MOE_TILES * MOE_TM * SUB, LANES), F32),
        compiler_params=_params("arbitrary"),
        name="moe_experts",
    )(plan["tile_expert"], plan["n_used"], xs, w_gate_up, b_gate_up.reshape(DEPTH, N_EXPERTS, 1, 2 * D_FF),
      w_down, b_down.reshape(DEPTH, N_EXPERTS, 1, D))


def _combine_kernel(rs_ref, rl_ref, rsn_ref, rln_ref, x1_ref, pos_ref, w_ref, g2_ref, lng_ref, lnb_ref, ys_hbm,
                    *rest, final):
    out_refs, (stage, sem) = rest[:-2], rest[-2:]
    i = pl.program_id(0)
    slot = i % 2
    other = 1 - slot

    def fetch(start_ref, len_ref, s):
        _for_each_run(start_ref, len_ref, lambda l, g, size: pltpu.make_async_copy(
            _rows(ys_hbm, g, size), _rows(stage.at[s], l, size), sem.at[s]).start())

    def wait(s):
        pltpu.make_async_copy(stage.at[s], stage.at[s], sem.at[s]).wait()

    @pl.when(i == 0)
    def _():
        fetch(rs_ref, rl_ref, 0)

    fetch(rsn_ref, rln_ref, other)
    wait(slot)
    y = _from_tiles(stage.at[slot], PAIRS_TM).astype(BF16)
    place = pos_ref[...].astype(F32)
    w = w_ref[...]
    s = lax.broadcasted_iota(jnp.int32, (TM, PAIRS_TM), 1).astype(F32)
    pick = jnp.zeros((TM, PAIRS_TM), F32)
    for k in range(TOP_K):
        pick = jnp.where(s == place[:, k:k + 1], w[:, k:k + 1], pick)
    f = _dot(pick.astype(BF16), y)
    ci = _cond_row(i)
    z = DEEPNORM_ALPHA * x1_ref[...] + g2_ref[pl.ds(ci, 1), :] * f
    out = _layer_norm(z, lng_ref[...], lnb_ref[...])
    if final:
        @pl.when(i < NT_P)
        def _():
            out_refs[0][...] = out

        @pl.when(i >= NT_P)
        def _():
            out_refs[1][...] = out
    else:
        out_refs[0][...] = out

    @pl.when(i == NT - 1)
    def _():
        wait(other)


def _combine(layer, x1, ys, pos, w, plan, mod, g, b):
    final = layer == DEPTH - 1
    nxt = lambda i: (jnp.minimum(i + 1, NT - 1), 0, 0)
    cur = lambda i: (i, 0, 0)
    if final:
        out_specs = _x_specs()
        out_shape = [jax.ShapeDtypeStruct((T_P, D), F32), jax.ShapeDtypeStruct((T_S, D), F32)]
    else:
        out_specs = _row_spec(D)
        out_shape = jax.ShapeDtypeStruct((T, D), F32)

    def kernel(*refs):
        _combine_kernel(*refs, final=final)

    return pl.pallas_call(
        kernel,
        grid=(NT,),
        in_specs=[_run_spec(cur), _run_spec(cur), _run_spec(nxt), _run_spec(nxt),
                  _row_spec(D), _row_spec(LANES), _row_spec(LANES), _mod_spec(layer, 5), _full((1, D)), _full((1, D)),
                  pl.BlockSpec(memory_space=pl.ANY)],
        out_specs=out_specs,
        out_shape=out_shape,
        scratch_shapes=[pltpu.VMEM((2, PAIRS_TM * SUB, LANES), F32), pltpu.SemaphoreType.DMA((2,))],
        compiler_params=_params("arbitrary"),
        name="moe_combine_norm" + ("_final" if final else ""),
    )(plan["run_start"], plan["run_len"], plan["run_start"], plan["run_len"], x1, pos, w, mod, g, b, ys)


def _moe_layer(layer, routed, experts, mod, ln2_g, ln2_b):
    x1, u2, pos, w, cnt = routed
    plan = _moe_plan(cnt)
    ys = _moe(layer, _dispatch(u2, pos, plan), plan, *experts)
    return _combine(layer, x1, ys, pos, w, plan, mod, ln2_g[layer][None], ln2_b[layer][None])


def _router_tail(l, ln1_g, ln1_b, router_w, router_b):
    rw = jnp.pad(router_w[l], ((0, 0), (0, LANES - N_EXPERTS)))
    rb = jnp.pad(router_b[l], (0, LANES - N_EXPERTS), constant_values=NEG)
    return ln1_g[l][None], ln1_b[l][None], rw, rb[None]


def kernel(x_prompt, x_sample, c, c_ctx, cache_diff_k, cache_diff_v, cache_na_k, cache_na_v, cache_gqa_k, cache_gqa_v, w_mod, b_mod, ln1_g, ln1_b, ln2_g, ln2_b, ab_w_in, ab_conv_w, ab_lambda_q1, ab_lambda_k1, ab_lambda_q2, ab_lambda_k2, ab_subln_g, ab_w_out, cd_w_in, cd_na_rpb, cd_q_norm_g, cd_k_norm_g, cd_w_out, router_w, router_b, w_gate_up, b_gate_up, w_down, b_down):
    xp = x_prompt.reshape(T_P, D)
    xs = x_sample.reshape(T_S, D)
    cond8 = jnp.concatenate([c_ctx[None], c, jnp.zeros((8 - 1 - B_S, D), F32)], axis=0)
    mod = _modulation(cond8, w_mod, b_mod)
    rope = _rope_tables()
    experts = (w_gate_up, b_gate_up, w_down, b_down)

    gb, y, q, k, v, new_diff_k, new_diff_v = _ab_in(xp, xs, mod, ab_w_in[0].astype(BF16), rope)
    lam_init = 0.8 - 0.6 * 1.0
    diff = (jnp.stack([ab_lambda_q1[0], ab_lambda_k1[0], ab_lambda_q2[0], ab_lambda_k2[0]]), ab_subln_g[0][None])
    attn = _flash_pair(q, k, v, latent=False, out_cols=512, col0=0, post="diff", diff=diff, lam_init=lam_init)
    attn = _flash_pair(q, k, v, latent=True, out_cols=512, col0=0, post="diff", diff=diff, lam_init=lam_init,
                       ctx=(cache_diff_k, cache_diff_v), ctx_mode="wide", prev=attn)
    routed = _ab_out(xp, xs, gb, y, ab_conv_w[0], attn, ab_w_out[0].astype(BF16), mod,
                     _router_tail(0, ln1_g, ln1_b, router_w, router_b))
    x = _moe_layer(0, routed, experts, mod, ln2_g, ln2_b)

    qg = jnp.tile(cd_q_norm_g[0], 2)[None]
    kg = jnp.tile(cd_k_norm_g[0], 2)[None]
    nq, nk, nv, gq, gk, gv, new_na_k, new_na_v, new_gqa_k, new_gqa_v = _cd_in(
        x, mod, cd_w_in[0].astype(BF16), qg, kg, rope)
    merged = _flash_pair(nq, nk, nv, latent=False, out_cols=D, col0=0, post="select")
    merged = _flash_pair(gq, gk, gv, latent=False, out_cols=D, col0=2, post="select", prev=merged)
    merged = _na_latent(nq, nk, nv, cache_na_k, cache_na_v, _na_bias(cd_na_rpb[0]), merged)
    merged = _flash_pair(gq, gk, gv, latent=True, out_cols=D, col0=2, post="select",
                         ctx=(cache_gqa_k, cache_gqa_v), ctx_mode="dup", prev=merged)
    routed = _cd_out(x, merged, cd_w_out[0].astype(BF16), mod,
                     _router_tail(1, ln1_g, ln1_b, router_w, router_b))
    y_p, y_s = _moe_layer(1, routed, experts, mod, ln2_g, ln2_b)

    return (y_p.reshape(B_P, N_P, D), y_s.reshape(B_S, N_S, D), new_diff_k, new_diff_v,
            new_na_k, new_na_v, new_gqa_k, new_gqa_v)
```

```python
import jax
import jax.numpy as jnp
import numpy as np
from jax import lax
from jax.experimental import pallas as pl
from jax.experimental.pallas import tpu as pltpu

F32 = jnp.float32
BF16 = jnp.bfloat16

D = 1024
B_P, N_P = 16, 256
B_S, N_S = 2, 4096
PAST = 256
T_P, T_S = B_P * N_P, B_S * N_S
T = T_P + T_S
TM = 256
NT_P, NT_S, NT = T_P // TM, T_S // TM, T // TM
TILES_PER_GRID = N_S // TM
GRID_W = 64
GRID_H = N_S // GRID_W
HD = 64
DEPTH = 2
N_EXPERTS = 32
TOP_K = 4
D_FF = 1024
NA_WIN_R, NA_WIN_C = 8, 16
NA_QROWS = 4
NA_KROWS = 12
SWIGLU_LIMIT = 7.0
SWIGLU_ALPHA = 1.702
ROPE_THETA = 10000.0
DEEPNORM_ALPHA = (2 * DEPTH) ** 0.25
LN_EPS = 1e-5
RMS_EPS = 1e-6
QK_SCALE = HD ** -0.5
NEG = -1e30
MOE_TM = 256
MOE_PAIRS = TOP_K * T
MOE_TILES = MOE_PAIRS // MOE_TM + N_EXPERTS
PAIRS_TM = TOP_K * TM
RUN_SIZES = tuple(TM >> b for b in range(TM.bit_length()))
LANES = 128
SUB = 8

VMEM_LIMIT = 56 * 1024 * 1024


def _params(*sem):
    return pltpu.CompilerParams(dimension_semantics=sem, vmem_limit_bytes=VMEM_LIMIT)


def _split(x):
    hi = x.astype(BF16)
    lo = (x - hi.astype(F32)).astype(BF16)
    return hi, lo


def _dot(a, b):
    return jnp.dot(a, b, preferred_element_type=F32)


def _dot3(a, b):
    ah, al = _split(a)
    bh, bl = _split(b)
    return _dot(ah, bh) + (_dot(ah, bl) + _dot(al, bh))


def _cond_row(i):
    return jnp.where(i < NT_P, 0, 1 + (i - NT_P) // TILES_PER_GRID)


def _layer_norm(z, g, b):
    mu = jnp.mean(z, axis=-1, keepdims=True)
    zc = z - mu
    var = jnp.mean(zc * zc, axis=-1, keepdims=True)
    return zc * lax.rsqrt(var + LN_EPS) * g + b


def _low_half(rows):
    return lax.broadcasted_iota(jnp.int32, (rows, LANES), 1) < HD


def _mod_kernel(c_ref, w_ref, b_ref, o_ref):
    c = c_ref[...]
    o_ref[...] = _dot3(c * jax.nn.sigmoid(c), w_ref[...]) + b_ref[...]


def _modulation(cond8, w_mod, b_mod):
    return pl.pallas_call(
        _mod_kernel,
        grid=(DEPTH, 6),
        in_specs=[pl.BlockSpec((8, D), lambda l, j: (0, 0)),
                  pl.BlockSpec((None, D, D), lambda l, j: (l, 0, j)),
                  pl.BlockSpec((None, 1, D), lambda l, j: (l, 0, j))],
        out_specs=pl.BlockSpec((None, 8, D), lambda l, j: (l, 0, j)),
        out_shape=jax.ShapeDtypeStruct((DEPTH, 8, 6 * D), F32),
        compiler_params=_params("arbitrary", "arbitrary"),
        name="modulation",
    )(cond8, w_mod, b_mod.reshape(DEPTH, 1, 6 * D))


def _mod_spec(layer, chunk):
    return pl.BlockSpec((None, 8, D), lambda i, _l=layer, _c=chunk: (_l, 0, _c))


def _full(shape):
    return pl.BlockSpec(shape, lambda i: (0,) * len(shape))


def _rope_tables():
    t = np.arange(N_S)
    half = HD // 2
    inv = ROPE_THETA ** (-np.arange(0, half, 2, dtype=np.float64) / half)
    inv_lane = np.tile(np.repeat(inv, 2), 2 * LANES // HD)
    lane = np.arange(LANES)
    by_row = (lane % HD) < half
    ang = np.where(by_row[None], (t // GRID_W)[:, None], (t % GRID_W)[:, None]) * inv_lane[None]
    cos, sin = np.cos(ang), np.sin(ang)
    even = (lane % 2) == 0
    return tuple(jnp.asarray(a, F32) for a in (cos, np.where(even, -sin, 0.0), np.where(even, 0.0, sin)))


def _rope(x, a, b, c):
    return x * a + pltpu.roll(x, LANES - 1, axis=1) * b + pltpu.roll(x, 1, axis=1) * c


def _rope_spec():
    return pl.BlockSpec((TM, LANES), lambda i: (jnp.maximum(i - NT_P, 0) % TILES_PER_GRID, 0))


def _x_specs():
    return [pl.BlockSpec((TM, D), lambda i: (jnp.minimum(i, NT_P - 1), 0)),
            pl.BlockSpec((TM, D), lambda i: (jnp.maximum(i - NT_P, 0), 0))]


def _cache_spec(heads, width):
    return pl.BlockSpec((None, None, heads, N_P, width), lambda i: (jnp.minimum(i, NT_P - 1), 0, 0, 0, 0))


def _row_spec(width):
    return pl.BlockSpec((TM, width), lambda i: (i, 0))


def _hm_spec(n):
    return pl.BlockSpec((n, TM, LANES), lambda i: (0, i, 0))


AB_Q0, AB_K0, AB_V0 = 1536, 2048, 2560


def _tile(p, col0, j):
    return p[:, col0 + j * LANES:col0 + (j + 1) * LANES]


def _ab_in_kernel(xp_ref, xs_ref, sh_ref, sc_ref, w_ref, ra_ref, rb_ref, rc_ref,
                  gb_ref, y_ref, q_ref, k_ref, v_ref, kc_ref, vc_ref):
    i = pl.program_id(0)
    is_p = i < NT_P
    ci = _cond_row(i)
    x = jnp.where(is_p, xp_ref[...], xs_ref[...])
    u = x * (1.0 + sc_ref[pl.ds(ci, 1), :]) + sh_ref[pl.ds(ci, 1), :]
    p = _dot(u.astype(BF16), w_ref[...])
    gb_ref[...] = p[:, 0:512]
    y_ref[...] = p[:, 512:1024] * p[:, 1024:1536]
    for h in range(4):
        v_ref[h] = _tile(p, AB_V0, h).astype(BF16)

    @pl.when(is_p)
    def _():
        for h in range(4):
            q_ref[h] = (_tile(p, AB_Q0, h) * QK_SCALE).astype(BF16)
            k_ref[h] = _tile(p, AB_K0, h).astype(BF16)
            kc_ref[h] = _tile(p, AB_K0, h)
            vc_ref[h] = _tile(p, AB_V0, h)

    @pl.when(jnp.logical_not(is_p))
    def _():
        a, b, c = ra_ref[...], rb_ref[...], rc_ref[...]
        for h in range(4):
            q_ref[h] = (_rope(_tile(p, AB_Q0, h), a, b, c) * QK_SCALE).astype(BF16)
            k_ref[h] = _rope(_tile(p, AB_K0, h), a, b, c).astype(BF16)


def _ab_in(xp, xs, mod, w_in, rope):
    hm = jax.ShapeDtypeStruct((4, T, LANES), BF16)
    cache = jax.ShapeDtypeStruct((B_P, 1, 4, N_P, LANES), F32)
    half = jax.ShapeDtypeStruct((T, 512), F32)
    return pl.pallas_call(
        _ab_in_kernel,
        grid=(NT,),
        in_specs=_x_specs() + [_mod_spec(0, 0), _mod_spec(0, 1), _full((D, 3072)),
                               _rope_spec(), _rope_spec(), _rope_spec()],
        out_specs=[_row_spec(512), _row_spec(512), _hm_spec(4), _hm_spec(4), _hm_spec(4),
                   _cache_spec(4, LANES), _cache_spec(4, LANES)],
        out_shape=[half, half, hm, hm, hm, cache, cache],
        compiler_params=_params("arbitrary"),
        name="ab_in_proj",
    )(xp, xs, mod, mod, w_in, *rope)


CD_NQ, CD_NK, CD_NV, CD_GQ, CD_GK, CD_GV = 0, 512, 1024, 1536, 2048, 2176


def _seg_mean64(s):
    r = lax.broadcasted_iota(jnp.int32, (LANES, LANES), 0) // HD
    c = lax.broadcasted_iota(jnp.int32, (LANES, LANES), 1) // HD
    seg = jnp.where(r == c, 1.0, 0.0).astype(BF16)
    hi, lo = _split(s)
    return (_dot(hi, seg) + _dot(lo, seg)) * (1.0 / HD)


def _rms64(x, g):
    return x * lax.rsqrt(_seg_mean64(x * x) + RMS_EPS) * g


def _dup_halves(x, lo):
    r = pltpu.roll(x, HD, axis=1)
    return jnp.where(lo, x, r), jnp.where(lo, r, x)


def _cd_in_kernel(x_ref, sh_ref, sc_ref, w_ref, qg_ref, kg_ref, ra_ref, rb_ref, rc_ref,
                  nq_ref, nk_ref, nv_ref, gq_ref, gk_ref, gv_ref,
                  nkc_ref, nvc_ref, gkc_ref, gvc_ref):
    i = pl.program_id(0)
    is_p = i < NT_P
    ci = _cond_row(i)
    u = x_ref[...] * (1.0 + sc_ref[pl.ds(ci, 1), :]) + sh_ref[pl.ds(ci, 1), :]
    p = _dot(u.astype(BF16), w_ref[...])
    lo = _low_half(TM)
    for j in range(4):
        nq_ref[j] = (_tile(p, CD_NQ, j) * QK_SCALE).astype(BF16)
        nk_ref[j] = _tile(p, CD_NK, j).astype(BF16)
        nv_ref[j] = _tile(p, CD_NV, j).astype(BF16)
    gq = [_rms64(_tile(p, CD_GQ, j), qg_ref[...]) for j in range(4)]
    gk = _rms64(_tile(p, CD_GK, 0), kg_ref[...])
    gv = _tile(p, CD_GV, 0)
    v0, v1 = _dup_halves(gv, lo)
    gv_ref[0] = v0.astype(BF16)
    gv_ref[1] = v1.astype(BF16)

    def emit(gq, gk):
        for j in range(4):
            gq_ref[j] = (gq[j] * QK_SCALE).astype(BF16)
        k0, k1 = _dup_halves(gk, lo)
        gk_ref[0] = k0.astype(BF16)
        gk_ref[1] = k1.astype(BF16)
        return k0, k1

    @pl.when(is_p)
    def _():
        k0, k1 = emit(gq, gk)
        gkc_ref[0] = k0[:, 0:HD]
        gkc_ref[1] = k1[:, 0:HD]
        gvc_ref[0] = v0[:, 0:HD]
        gvc_ref[1] = v1[:, 0:HD]
        for j in range(4):
            for src, dst in ((CD_NK, nkc_ref), (CD_NV, nvc_ref)):
                a, b = _dup_halves(_tile(p, src, j), lo)
                dst[2 * j] = a[:, 0:HD]
                dst[2 * j + 1] = b[:, 0:HD]

    @pl.when(jnp.logical_not(is_p))
    def _():
        a, b, c = ra_ref[...], rb_ref[...], rc_ref[...]
        emit([_rope(g, a, b, c) for g in gq], _rope(gk, a, b, c))


def _cd_in(x, mod, w_in, qg, kg, rope):
    hm4 = jax.ShapeDtypeStruct((4, T, LANES), BF16)
    hm2 = jax.ShapeDtypeStruct((2, T, LANES), BF16)
    c8 = jax.ShapeDtypeStruct((B_P, 1, 8, N_P, HD), F32)
    c2 = jax.ShapeDtypeStruct((B_P, 1, 2, N_P, HD), F32)
    return pl.pallas_call(
        _cd_in_kernel,
        grid=(NT,),
        in_specs=[_row_spec(D), _mod_spec(1, 0), _mod_spec(1, 1), _full((D, 2304)),
                  _full((1, LANES)), _full((1, LANES)), _rope_spec(), _rope_spec(), _rope_spec()],
        out_specs=[_hm_spec(4), _hm_spec(4), _hm_spec(4), _hm_spec(4), _hm_spec(2), _hm_spec(2),
                   _cache_spec(8, HD), _cache_spec(8, HD), _cache_spec(2, HD), _cache_spec(2, HD)],
        out_shape=[hm4, hm4, hm4, hm4, hm2, hm2, c8, c8, c2, c2],
        compiler_params=_params("arbitrary"),
        name="cd_in_proj",
    )(x, mod, mod, w_in, qg, kg, *rope)


def _stack_pairs(q_ref, n_q, tq):
    lo = _low_half(tq)
    parts = []
    for j in range(n_q):
        q = q_ref[j]
        zero = jnp.zeros_like(q)
        parts += [jnp.where(lo, q, zero), jnp.where(lo, zero, q)]
    return jnp.concatenate(parts, axis=0), lo


def _qk(qs, kb):
    return lax.dot_general(qs, kb, (((1,), (1,)), ((), ())), preferred_element_type=F32)


def _ctx_tile(ref, mode):
    if mode == "wide":
        x = ref[...]
    elif mode == "pair":
        x = jnp.concatenate([ref[0], ref[1]], axis=1)
    else:
        x = jnp.concatenate([ref[...], ref[...]], axis=1)
    return x.astype(BF16)


def _flash_pair_kernel(*refs, n_q, tq, nk, tk, ctx_mode, post, lam_init):
    it = iter(refs)
    q_ref, k_ref, v_ref = next(it), next(it), next(it)
    kc_ref, vc_ref = (next(it), next(it)) if ctx_mode else (None, None)
    lam_ref, g_ref = (next(it), next(it)) if post == "diff" else (None, None)
    o_ref = next(it)

    qs, lo = _stack_pairs(q_ref, n_q, tq)
    rows = 2 * n_q * tq

    def step(kb, vb, carry):
        m, l, acc = carry
        s = _qk(qs, kb)
        m_new = jnp.maximum(m, jnp.max(s, axis=1, keepdims=True))
        alpha = jnp.exp(m - m_new)
        p = jnp.exp(s - m_new)
        l = alpha * l + jnp.sum(p, axis=1, keepdims=True)
        acc = alpha * acc + _dot(p.astype(BF16), vb)
        return m_new, l, acc

    carry = (jnp.full((rows, 1), NEG, F32), jnp.zeros((rows, 1), F32), jnp.zeros((rows, LANES), F32))
    for c in range(nk // tk):
        carry = step(k_ref[c * tk:(c + 1) * tk, :], v_ref[c * tk:(c + 1) * tk, :], carry)
    if ctx_mode:
        carry = step(_ctx_tile(kc_ref, ctx_mode), _ctx_tile(vc_ref, ctx_mode), carry)
    _, l, acc = carry
    o = acc / l

    if post == "diff":
        lp = lam_ref[...]
        lam = (jnp.exp(jnp.sum(lp[0:1] * lp[1:2], axis=1, keepdims=True))
               - jnp.exp(jnp.sum(lp[2:3] * lp[3:4], axis=1, keepdims=True)) + lam_init)
        a = o[0:tq] - lam * o[tq:2 * tq]
        ms = jnp.mean(a * a, axis=-1, keepdims=True)
        o_ref[...] = (a * lax.rsqrt(ms + RMS_EPS) * g_ref[...] * (1.0 - lam_init)).astype(o_ref.dtype)
    else:
        for j in range(n_q):
            o_ref[:, j * LANES:(j + 1) * LANES] = jnp.where(
                lo, o[2 * j * tq:(2 * j + 1) * tq], o[(2 * j + 1) * tq:(2 * j + 2) * tq]).astype(o_ref.dtype)


def _flash_pair(q, k, v, *, latent, out_cols, col0, post, ctx=None, ctx_mode=None, diff=None,
                lam_init=0.0, prev=None):
    groups = k.shape[0]
    n_q = q.shape[0] // groups
    if latent:
        tq, nk, tk = TM, N_S, 1024
        grid = (B_S, groups, TILES_PER_GRID)
        qrow = lambda b, g, i: NT_P + b * TILES_PER_GRID + i
        krow = lambda b, g, i: T_P // N_S + b
    else:
        tq, nk, tk = N_P, N_P, N_P
        grid = (B_P, groups, 1)
        qrow = lambda b, g, i: b
        krow = lambda b, g, i: b
    in_specs = [pl.BlockSpec((n_q, tq, LANES), lambda b, g, i: (g, qrow(b, g, i), 0)),
                pl.BlockSpec((None, nk, LANES), lambda b, g, i: (g, krow(b, g, i), 0)),
                pl.BlockSpec((None, nk, LANES), lambda b, g, i: (g, krow(b, g, i), 0))]
    args = [q, k, v]
    if ctx is not None:
        if ctx_mode == "wide":
            spec = pl.BlockSpec((None, None, None, PAST, LANES), lambda b, g, i: (b, 0, g, 0, 0))
        elif ctx_mode == "pair":
            spec = pl.BlockSpec((None, None, 2, PAST, HD), lambda b, g, i: (b, 0, g, 0, 0))
        else:
            spec = pl.BlockSpec((None, None, None, PAST, HD), lambda b, g, i: (b, 0, g, 0, 0))
        in_specs += [spec, spec]
        args += list(ctx)
    if diff is not None:
        in_specs += [pl.BlockSpec((4, HD), lambda b, g, i: (0, 0)),
                     pl.BlockSpec((1, LANES), lambda b, g, i: (0, 0))]
        args += list(diff)
    aliases = {}
    if prev is not None:
        aliases = {len(args): 0}
        in_specs.append(pl.BlockSpec(memory_space=pl.ANY))
        args.append(prev)

    def kernel(*refs):
        if prev is not None:
            refs = refs[:-2] + refs[-1:]
        _flash_pair_kernel(*refs, n_q=n_q, tq=tq, nk=nk, tk=tk, ctx_mode=ctx_mode if ctx is not None else None,
                           post=post, lam_init=lam_init)

    return pl.pallas_call(
        kernel,
        grid=grid,
        in_specs=in_specs,
        out_specs=pl.BlockSpec((tq, n_q * LANES), lambda b, g, i: (qrow(b, g, i), col0 + g)),
        out_shape=jax.ShapeDtypeStruct((T, out_cols), BF16),
        input_output_aliases=aliases,
        compiler_params=_params("arbitrary", "arbitrary", "arbitrary"),
        name="attn_" + post + ("_latent" if latent else "_context") + str(n_q),
    )(*args)


NA_TQ = NA_QROWS * GRID_W
NA_TK = NA_KROWS * GRID_W


NA_BLOCK_POS = ((0, 0), (NA_QROWS, 0), (GRID_H - NA_QROWS, GRID_H - NA_KROWS))
N_DR = 2 * NA_WIN_R - 1
N_DC = 2 * NA_WIN_C - 1


def _na_bias_kernel(rpb_ref, o_ref):
    qc = lax.broadcasted_iota(jnp.int32, (GRID_W, LANES), 0)
    lane = lax.broadcasted_iota(jnp.int32, (GRID_W, LANES), 1)
    kc = lane % GRID_W
    cs = jnp.clip(qc - NA_WIN_C // 2, 0, GRID_W - NA_WIN_C)
    col_ok = jnp.logical_and(kc >= cs, kc < cs + NA_WIN_C)
    lo = lane < GRID_W
    neg = jnp.full((GRID_W, LANES), NEG, F32)
    for head in range(2):
        toeplitz = []
        for dr in range(N_DR):
            r = jnp.broadcast_to(rpb_ref[head, dr:dr + 1, :], (GRID_W, LANES))
            t = jnp.where(lo, pltpu.roll(r, LANES - (NA_WIN_C - 1), axis=1, stride=1, stride_axis=0),
                          pltpu.roll(r, GRID_W - (NA_WIN_C - 1), axis=1, stride=1, stride_axis=0))
            toeplitz.append(jnp.where(col_ok, t, neg))
        for pos, (r0, k0) in enumerate(NA_BLOCK_POS):
            for i in range(NA_QROWS):
                qr = r0 + i
                rs = min(max(qr - NA_WIN_R // 2, 0), GRID_H - NA_WIN_R)
                tiles = [toeplitz[k0 + j - qr + NA_WIN_R - 1] if rs <= k0 + j < rs + NA_WIN_R else neg
                         for j in range(NA_KROWS)]
                for jp in range(NA_KROWS // 2):
                    o_ref[pos, pl.ds(head * NA_TQ + i * GRID_W, GRID_W), pl.ds(jp * LANES, LANES)] = jnp.where(
                        lo, tiles[2 * jp], tiles[2 * jp + 1])


def _na_bias(rpb):
    rpb_pad = jnp.pad(rpb, ((0, 0), (0, 16 - N_DR), (0, LANES - N_DC)), constant_values=NEG)
    return pl.pallas_call(
        _na_bias_kernel,
        grid=(4,),
        in_specs=[pl.BlockSpec((2, 16, LANES), lambda g: (g, 0, 0))],
        out_specs=pl.BlockSpec((3, None, 2 * NA_TQ, NA_TK), lambda g: (0, g, 0, 0)),
        out_shape=jax.ShapeDtypeStruct((3, 4, 2 * NA_TQ, NA_TK), F32),
        compiler_params=_params("arbitrary"),
        name="na_bias_table",
    )(rpb_pad)


def _na_kernel(q_ref, k_ref, v_ref, kc_ref, vc_ref, bm_ref, _, o_ref):
    i = pl.program_id(2)
    qs, lo = _stack_pairs(q_ref, 1, NA_TQ)
    k0 = jnp.clip(i * NA_QROWS - NA_WIN_R // 2, 0, GRID_H - NA_KROWS)
    start = pl.multiple_of(k0 * GRID_W, GRID_W)
    kw = k_ref[pl.ds(start, NA_TK), :]
    vw = v_ref[pl.ds(start, NA_TK), :]
    s_w = _qk(qs, kw) + bm_ref[...]
    s_c = _qk(qs, _ctx_tile(kc_ref, "pair"))
    m = jnp.maximum(jnp.max(s_w, axis=1, keepdims=True), jnp.max(s_c, axis=1, keepdims=True))
    p_w = jnp.exp(s_w - m)
    p_c = jnp.exp(s_c - m)
    l = jnp.sum(p_w, axis=1, keepdims=True) + jnp.sum(p_c, axis=1, keepdims=True)
    o = (_dot(p_w.astype(BF16), vw) + _dot(p_c.astype(BF16), _ctx_tile(vc_ref, "pair"))) / l
    o_ref[...] = jnp.where(lo, o[0:NA_TQ], o[NA_TQ:2 * NA_TQ]).astype(o_ref.dtype)


def _na_latent(q, k, v, kc, vc, bm, prev):
    nblk = N_S // NA_TQ
    qrow = lambda b, g, i: T_P // NA_TQ + b * nblk + i
    krow = lambda b, g, i: T_P // N_S + b
    cfg = lambda i: jnp.where(i == 0, 0, jnp.where(i == nblk - 1, 2, 1))
    ctx_spec = pl.BlockSpec((None, None, 2, PAST, HD), lambda b, g, i: (b, 0, g, 0, 0))
    return pl.pallas_call(
        _na_kernel,
        grid=(B_S, 4, nblk),
        in_specs=[pl.BlockSpec((1, NA_TQ, LANES), lambda b, g, i: (g, qrow(b, g, i), 0)),
                  pl.BlockSpec((None, N_S, LANES), lambda b, g, i: (g, krow(b, g, i), 0)),
                  pl.BlockSpec((None, N_S, LANES), lambda b, g, i: (g, krow(b, g, i), 0)),
                  ctx_spec, ctx_spec,
                  pl.BlockSpec((None, None, 2 * NA_TQ, NA_TK), lambda b, g, i: (cfg(i), g, 0, 0)),
                  pl.BlockSpec(memory_space=pl.ANY)],
        out_specs=pl.BlockSpec((NA_TQ, LANES), lambda b, g, i: (qrow(b, g, i), g)),
        out_shape=jax.ShapeDtypeStruct((T, D), BF16),
        input_output_aliases={6: 0},
        compiler_params=_params("arbitrary", "arbitrary", "arbitrary"),
        name="attn_window_latent",
    )(q, k, v, kc, vc, bm, prev)


def _top4(logits):
    lane = lax.broadcasted_iota(jnp.int32, logits.shape, 1).astype(F32)
    rest = logits
    tops, firsts = [], []
    for _ in range(TOP_K):
        m = jnp.max(rest, axis=1, keepdims=True)
        first = jnp.min(jnp.where(rest == m, lane, float(LANES)), axis=1, keepdims=True)
        tops.append(m)
        firsts.append(first)
        rest = jnp.where(lane == first, -jnp.inf, rest)
    es = [jnp.exp(m - tops[0]) for m in tops]
    denom = es[0] + es[1] + es[2] + es[3]
    w = jnp.zeros_like(logits)
    for k in range(TOP_K):
        w = jnp.where(lane == float(k), es[k] / denom, w)
    return firsts, w


def _local_sort(firsts):
    lane = lax.broadcasted_iota(jnp.int32, (TM, LANES), 1).astype(F32)
    hots = [lane == f for f in firsts]
    sel = jnp.zeros((TM, LANES), F32)
    for hot in hots:
        sel = jnp.where(hot, 1.0, sel)
    r = lax.broadcasted_iota(jnp.int32, (TM, TM), 0)
    c = lax.broadcasted_iota(jnp.int32, (TM, TM), 1)
    earlier = _dot(jnp.where(c < r, 1.0, 0.0).astype(BF16), sel.astype(BF16))
    cnt = jnp.sum(sel, axis=0, keepdims=True)
    r = lax.broadcasted_iota(jnp.int32, (LANES, LANES), 0)
    c = lax.broadcasted_iota(jnp.int32, (LANES, LANES), 1)
    start = _dot(jnp.broadcast_to(cnt, (SUB, LANES)).astype(BF16), jnp.where(r < c, 1.0, 0.0).astype(BF16))[0:1]
    place = start + earlier
    pos = jnp.zeros((TM, LANES), F32)
    for k, hot in enumerate(hots):
        pos = jnp.where(lane == float(k), jnp.sum(jnp.where(hot, place, 0.0), axis=1, keepdims=True), pos)
    return pos.astype(jnp.int32), cnt.astype(jnp.int32)


def _mixer_tail(x, h, ci, g1_ref, sh2_ref, sc2_ref, lng_ref, lnb_ref, rw_ref, rb_ref,
                x1_ref, u2_ref, pos_ref, w_ref, cnt_ref):
    x1 = _layer_norm(DEEPNORM_ALPHA * x + g1_ref[pl.ds(ci, 1), :] * h, lng_ref[...], lnb_ref[...])
    x1_ref[...] = x1
    u2 = x1 * (1.0 + sc2_ref[pl.ds(ci, 1), :]) + sh2_ref[pl.ds(ci, 1), :]
    u2_ref[...] = u2.astype(BF16)
    firsts, w_ref[...] = _top4(_dot3(u2, rw_ref[...]) + rb_ref[...])
    pos, cnt = _local_sort(firsts)
    pos_ref[...] = pos
    cnt_ref[...] = jnp.broadcast_to(cnt, (SUB, LANES))


def _ab_out_kernel(xp_ref, xs_ref, gb_ref, y_ref, yp_ref, yn_ref, cv_ref, at_ref, wo_ref,
                   g1_ref, sh2_ref, sc2_ref, lng_ref, lnb_ref, rw_ref, rb_ref, *out_refs):
    i = pl.program_id(0)
    is_p = i < NT_P
    ci = _cond_row(i)
    x = jnp.where(is_p, xp_ref[...], xs_ref[...])
    j = (i - NT_P) % TILES_PER_GRID
    first = jnp.logical_or(is_p, j == 0)
    last = jnp.logical_or(is_p, j == TILES_PER_GRID - 1)
    y = y_ref[...]
    row = lax.broadcasted_iota(jnp.int32, y.shape, 0)
    before = jnp.where(first, 0.0, yp_ref[7:8, :])
    after = jnp.where(last, 0.0, yn_ref[0:1, :])
    y_prev = jnp.where(row == 0, before, pltpu.roll(y, 1, axis=0))
    y_next = jnp.where(row == TM - 1, after, pltpu.roll(y, TM - 1, axis=0))
    cv = cv_ref[...]
    conv = gb_ref[...] * (y_prev * cv[0:1] + y * cv[1:2] + y_next * cv[2:3])
    h = _dot(conv.astype(BF16), wo_ref[0:512, :]) + _dot(at_ref[...], wo_ref[512:1024, :])
    _mixer_tail(x, h, ci, g1_ref, sh2_ref, sc2_ref, lng_ref, lnb_ref, rw_ref, rb_ref, *out_refs)


def _cd_out_kernel(x_ref, mg_ref, wo_ref, g1_ref, sh2_ref, sc2_ref, lng_ref, lnb_ref, rw_ref, rb_ref,
                   *out_refs):
    ci = _cond_row(pl.program_id(0))
    h = _dot(mg_ref[...], wo_ref[...])
    _mixer_tail(x_ref[...], h, ci, g1_ref, sh2_ref, sc2_ref, lng_ref, lnb_ref, rw_ref, rb_ref, *out_refs)


def _tail_specs(layer):
    return [_mod_spec(layer, 2), _mod_spec(layer, 3), _mod_spec(layer, 4),
            _full((1, D)), _full((1, D)), _full((D, LANES)), _full((1, LANES))]


_TAIL_OUT_SPECS = [_row_spec(D), _row_spec(D), _row_spec(LANES), _row_spec(LANES),
                   pl.BlockSpec((None, SUB, LANES), lambda i: (i, 0, 0))]
_TAIL_OUT_SHAPES = [jax.ShapeDtypeStruct((T, D), F32), jax.ShapeDtypeStruct((T, D), BF16),
                    jax.ShapeDtypeStruct((T, LANES), jnp.int32), jax.ShapeDtypeStruct((T, LANES), F32),
                    jax.ShapeDtypeStruct((NT, SUB, LANES), jnp.int32)]


def _ab_out(xp, xs, gb, y, conv_w, attn, w_out, mod, tail):
    halo_prev = pl.BlockSpec((8, 512), lambda i: (jnp.maximum(i * (TM // 8) - 1, 0), 0))
    halo_next = pl.BlockSpec((8, 512), lambda i: (jnp.minimum((i + 1) * (TM // 8), T // 8 - 1), 0))
    return pl.pallas_call(
        _ab_out_kernel,
        grid=(NT,),
        in_specs=_x_specs() + [_row_spec(512), _row_spec(512), halo_prev, halo_next, _full((3, 512)),
                               _row_spec(512), _full((D, D))] + _tail_specs(0),
        out_specs=_TAIL_OUT_SPECS,
        out_shape=_TAIL_OUT_SHAPES,
        compiler_params=_params("arbitrary"),
        name="ab_out_proj",
    )(xp, xs, gb, y, y, y, conv_w, attn, w_out, mod, mod, mod, *tail)


def _cd_out(x, merged, w_out, mod, tail):
    return pl.pallas_call(
        _cd_out_kernel,
        grid=(NT,),
        in_specs=[_row_spec(D), _row_spec(D), _full((D, D))] + _tail_specs(1),
        out_specs=_TAIL_OUT_SPECS,
        out_shape=_TAIL_OUT_SHAPES,
        compiler_params=_params("arbitrary"),
        name="cd_out_proj",
    )(x, merged, w_out, mod, mod, mod, *tail)


def _mix_down(wd_ref, wmix_ref):
    half = D_FF // 2
    for c in range(D // LANES):
        wmix_ref[c, pl.ds(0, half, stride=2), :] = wd_ref[0:half, c * LANES:(c + 1) * LANES]
        wmix_ref[c, pl.ds(1, half, stride=2), :] = wd_ref[half:D_FF, c * LANES:(c + 1) * LANES]


def _prep_expert(wgu_ref, wd_ref, wgu_bf, wmix_ref, wd_bf):
    for c in range(4):
        wgu_bf[:, c * 512:(c + 1) * 512] = wgu_ref[:, c * 512:(c + 1) * 512].astype(BF16)
    _mix_down(wd_ref, wmix_ref)
    for c in range(D // LANES):
        wd_bf[:, c * LANES:(c + 1) * LANES] = wmix_ref[c].astype(BF16)


def _expert_ffn(u, wgu_bf, bgu_ref, wd_bf, bd_ref):
    rows = u.shape[0]
    ga = _dot(u, wgu_bf[:, 0:D_FF]) + bgu_ref[:, 0:D_FF]
    gb = _dot(u, wgu_bf[:, D_FF:2 * D_FF]) + bgu_ref[:, D_FF:2 * D_FF]
    even = (lax.broadcasted_iota(jnp.int32, (rows, LANES), 1) % 2) == 0
    hid = []
    for c in range(D_FF // LANES):
        a = ga[:, c * LANES:(c + 1) * LANES]
        b = gb[:, c * LANES:(c + 1) * LANES]
        gate = jnp.where(even, a, pltpu.roll(b, 1, axis=1))
        up = jnp.where(even, pltpu.roll(a, LANES - 1, axis=1), b)
        gate = jnp.minimum(gate, SWIGLU_LIMIT)
        up = jnp.clip(up, -SWIGLU_LIMIT, SWIGLU_LIMIT)
        hid.append(((up + 1.0) * gate * jax.nn.sigmoid(SWIGLU_ALPHA * gate)).astype(BF16))
    hid = jnp.concatenate(hid, axis=1)
    return _dot(hid, wd_bf[...]) + bd_ref[...]


def _moe_plan(cnt):
    c = cnt[:, 0, :N_EXPERTS]
    counts = jnp.sum(c, axis=0)
    tiles_e = (counts + MOE_TM - 1) // MOE_TM
    tile_end = jnp.cumsum(tiles_e)
    first_row = (tile_end - tiles_e) * MOE_TM
    run_start = first_row[None] + jnp.cumsum(c, axis=0) - c
    j = jnp.arange(MOE_TILES, dtype=jnp.int32)
    tile_expert = jnp.minimum(jnp.sum((j[:, None] >= tile_end[None]).astype(jnp.int32), axis=1), N_EXPERTS - 1)
    i32 = lambda a: a.astype(jnp.int32)
    experts = jnp.arange(N_EXPERTS, dtype=jnp.int32)
    owns = tiles_e > 0
    seg = jnp.cumsum(owns.astype(jnp.int32)) - owns.astype(jnp.int32)
    later = jnp.where((experts[None] > experts[:, None]) & owns[None], experts[None], N_EXPERTS)
    nxt = jnp.min(later, axis=1)
    nxt = jnp.where(nxt == N_EXPERTS, -1, nxt)
    return dict(tile_expert=i32(tile_expert), n_used=i32(tile_end[-1:]),
                seg_parity=i32(seg[tile_expert] % 2), next_expert=i32(nxt[tile_expert]),
                run_start=i32(run_start)[:, None, :], run_len=i32(c)[:, None, :],
                pad_start=i32(first_row + counts)[None, None, :], pad_len=i32(tiles_e * MOE_TM - counts)[None, None, :])


def _for_each_run(start_ref, len_ref, copy):
    def body(e, local):
        n = len_ref[0, e]
        g = start_ref[0, e]
        for size in RUN_SIZES:
            covered = jnp.bitwise_and(n, -2 * size)

            @pl.when(jnp.bitwise_and(n, size) != 0)
            def _(covered=covered, size=size):
                copy(local + covered, g + covered, size)
        return local + n
    lax.fori_loop(0, N_EXPERTS, body, jnp.int32(0))


def _rows(ref, start, size):
    start = 0 if isinstance(start, int) and start == 0 else pl.multiple_of(start * SUB, SUB)
    return ref.at[pl.ds(start, size * SUB), :]


def _from_tiles(ref, rows):
    return jnp.concatenate([ref[pl.ds(c, rows, stride=SUB), :] for c in range(SUB)], axis=1)


def _to_tiles(ref, x):
    for c in range(SUB):
        ref[pl.ds(c, x.shape[0], stride=SUB), :] = x[:, c * LANES:(c + 1) * LANES]


def _run_spec(index):
    return pl.BlockSpec((None, 1, N_EXPERTS), index, memory_space=pltpu.SMEM)


def _dispatch_kernel(rs_ref, rl_ref, ps_ref, pn_ref, u_ref, pos_ref, xs_hbm, stage, zeros, sem, zsem):
    i = pl.program_id(0)
    slot = i % 2

    def wait(s):
        pltpu.make_async_copy(stage.at[s], stage.at[s], sem.at[s]).wait()

    @pl.when(i == 0)
    def _():
        zeros[...] = jnp.zeros_like(zeros)

        def pad(local, g, size):
            cp = pltpu.make_async_copy(_rows(zeros, 0, size), _rows(xs_hbm, g, size), zsem.at[0])
            cp.start()
            cp.wait()
        _for_each_run(ps_ref, pn_ref, pad)

    @pl.when(i >= 2)
    def _():
        wait(slot)

    place = pos_ref[...].astype(F32).T
    s = lax.broadcasted_iota(jnp.int32, (PAIRS_TM, TM), 0).astype(F32)
    onehot = jnp.zeros((PAIRS_TM, TM), F32)
    for k in range(TOP_K):
        onehot = jnp.where(s == place[k:k + 1, :], 1.0, onehot)
    _to_tiles(stage.at[slot], _dot(onehot.astype(BF16), u_ref[...]))
    _for_each_run(rs_ref, rl_ref, lambda l, g, size: pltpu.make_async_copy(
        _rows(stage.at[slot], l, size), _rows(xs_hbm, g, size), sem.at[slot]).start())

    @pl.when(i == NT - 1)
    def _():
        wait(1 - slot)
        wait(slot)


def _dispatch(u2, pos, plan):
    return pl.pallas_call(
        _dispatch_kernel,
        grid=(NT,),
        in_specs=[_run_spec(lambda i: (i, 0, 0)), _run_spec(lambda i: (i, 0, 0)),
                  _run_spec(lambda i: (0, 0, 0)), _run_spec(lambda i: (0, 0, 0)),
                  _row_spec(D), _row_spec(LANES)],
        out_specs=pl.BlockSpec(memory_space=pl.ANY),
        out_shape=jax.ShapeDtypeStruct((MOE_TILES * MOE_TM * SUB, LANES), F32),
        scratch_shapes=[pltpu.VMEM((2, PAIRS_TM * SUB, LANES), F32), pltpu.VMEM((TM * SUB, LANES), F32),
                        pltpu.SemaphoreType.DMA((2,)), pltpu.SemaphoreType.DMA((1,))],
        compiler_params=_params("arbitrary"),
        name="moe_dispatch",
    )(plan["run_start"], plan["run_len"], plan["pad_start"], plan["pad_len"], u2, pos)


def _moe_kernel(te_ref, nu_ref, par_ref, nxt_ref, x_ref, bgu_ref, bd_ref, wgu_hbm, wd_hbm, y_ref,
                wgu_f, wd_f, wgu_bf, wmix_ref, wd_bf, sem, *, layer):
    j = pl.program_id(0)
    used = j < nu_ref[0]
    buf = par_ref[j]

    def copies(e, s):
        return (pltpu.make_async_copy(wgu_hbm.at[layer, e], wgu_f.at[s], sem.at[s]),
                pltpu.make_async_copy(wd_hbm.at[layer, e], wd_f.at[s], sem.at[s]))

    @pl.when(jnp.logical_and(used, j == 0))
    def _():
        for cp in copies(te_ref[0], 0):
            cp.start()

    @pl.when(jnp.logical_and(used, jnp.logical_or(j == 0, te_ref[j] != te_ref[jnp.maximum(j - 1, 0)])))
    def _():
        for cp in copies(te_ref[j], buf):
            cp.wait()

        @pl.when(nxt_ref[j] >= 0)
        def _():
            for cp in copies(nxt_ref[j], 1 - buf):
                cp.start()

        _prep_expert(wgu_f.at[buf], wd_f.at[buf], wgu_bf, wmix_ref, wd_bf)

    @pl.when(used)
    def _():
        x = _from_tiles(x_ref, MOE_TM).astype(BF16)
        _to_tiles(y_ref, _expert_ffn(x, wgu_bf, bgu_ref, wd_bf, bd_ref))


def _moe(layer, xs, plan, w_gate_up, b_gate_up, w_down, b_down):
    rows = pl.BlockSpec((MOE_TM * SUB, LANES), lambda j, te, nu, par, nxt: (jnp.minimum(j, nu[0] - 1), 0))
    grid_spec = pltpu.PrefetchScalarGridSpec(
        num_scalar_prefetch=4,
        grid=(MOE_TILES,),
        in_specs=[rows,
                  pl.BlockSpec((None, None, 1, 2 * D_FF), lambda j, te, nu, par, nxt: (layer, te[j], 0, 0)),
                  pl.BlockSpec((None, None, 1, D), lambda j, te, nu, par, nxt: (layer, te[j], 0, 0)),
                  pl.BlockSpec(memory_space=pl.ANY), pl.BlockSpec(memory_space=pl.ANY)],
        out_specs=rows,
        scratch_shapes=[pltpu.VMEM((2, D, 2 * D_FF), F32), pltpu.VMEM((2, D_FF, D), F32),
                        pltpu.VMEM((D, 2 * D_FF), BF16), pltpu.VMEM((D // LANES, D_FF, LANES), F32),
                        pltpu.VMEM((D_FF, D), BF16), pltpu.SemaphoreType.DMA((2,))])

    def kernel(*refs):
        _moe_kernel(*refs, layer=layer)

    return pl.pallas_call(
        kernel,
        grid_spec=grid_spec,
        out_shape=jax.ShapeDtypeStruct((MOE_TILES * MOE_TM * SUB, LANES), F32),
        compiler_params=_params("arbitrary"),
        name="moe_experts",
    )(plan["tile_expert"], plan["n_used"], plan["seg_parity"], plan["next_expert"], xs,
      b_gate_up.reshape(DEPTH, N_EXPERTS, 1, 2 * D_FF), b_down.reshape(DEPTH, N_EXPERTS, 1, D), w_gate_up, w_down)


def _combine_kernel(rs_ref, rl_ref, rsn_ref, rln_ref, x1_ref, pos_ref, w_ref, g2_ref, lng_ref, lnb_ref, ys_hbm,
                    *rest, final):
    out_refs, (stage, sem) = rest[:-2], rest[-2:]
    i = pl.program_id(0)
    slot = i % 2
    other = 1 - slot

    def fetch(start_ref, len_ref, s):
        _for_each_run(start_ref, len_ref, lambda l, g, size: pltpu.make_async_copy(
            _rows(ys_hbm, g, size), _rows(stage.at[s], l, size), sem.at[s]).start())

    def wait(s):
        pltpu.make_async_copy(stage.at[s], stage.at[s], sem.at[s]).wait()

    @pl.when(i == 0)
    def _():
        fetch(rs_ref, rl_ref, 0)

    fetch(rsn_ref, rln_ref, other)
    wait(slot)
    y = _from_tiles(stage.at[slot], PAIRS_TM).astype(BF16)
    place = pos_ref[...].astype(F32)
    w = w_ref[...]
    s = lax.broadcasted_iota(jnp.int32, (TM, PAIRS_TM), 1).astype(F32)
    pick = jnp.zeros((TM, PAIRS_TM), F32)
    for k in range(TOP_K):
        pick = jnp.where(s == place[:, k:k + 1], w[:, k:k + 1], pick)
    f = _dot(pick.astype(BF16), y)
    ci = _cond_row(i)
    z = DEEPNORM_ALPHA * x1_ref[...] + g2_ref[pl.ds(ci, 1), :] * f
    out = _layer_norm(z, lng_ref[...], lnb_ref[...])
    if final:
        @pl.when(i < NT_P)
        def _():
            out_refs[0][...] = out

        @pl.when(i >= NT_P)
        def _():
            out_refs[1][...] = out
    else:
        out_refs[0][...] = out

    @pl.when(i == NT - 1)
    def _():
        wait(other)


def _combine(layer, x1, ys, pos, w, plan, mod, g, b):
    final = layer == DEPTH - 1
    nxt = lambda i: (jnp.minimum(i + 1, NT - 1), 0, 0)
    cur = lambda i: (i, 0, 0)
    if final:
        out_specs = _x_specs()
        out_shape = [jax.ShapeDtypeStruct((T_P, D), F32), jax.ShapeDtypeStruct((T_S, D), F32)]
    else:
        out_specs = _row_spec(D)
        out_shape = jax.ShapeDtypeStruct((T, D), F32)

    def kernel(*refs):
        _combine_kernel(*refs, final=final)

    return pl.pallas_call(
        kernel,
        grid=(NT,),
        in_specs=[_run_spec(cur), _run_spec(cur), _run_spec(nxt), _run_spec(nxt),
                  _row_spec(D), _row_spec(LANES), _row_spec(LANES), _mod_spec(layer, 5), _full((1, D)), _full((1, D)),
                  pl.BlockSpec(memory_space=pl.ANY)],
        out_specs=out_specs,
        out_shape=out_shape,
        scratch_shapes=[pltpu.VMEM((2, PAIRS_TM * SUB, LANES), F32), pltpu.SemaphoreType.DMA((2,))],
        compiler_params=_params("arbitrary"),
        name="moe_combine_norm" + ("_final" if final else ""),
    )(plan["run_start"], plan["run_len"], plan["run_start"], plan["run_len"], x1, pos, w, mod, g, b, ys)


def _moe_layer(layer, routed, experts, mod, ln2_g, ln2_b):
    x1, u2, pos, w, cnt = routed
    plan = _moe_plan(cnt)
    ys = _moe(layer, _dispatch(u2, pos, plan), plan, *experts)
    return _combine(layer, x1, ys, pos, w, plan, mod, ln2_g[layer][None], ln2_b[layer][None])


def _router_tail(l, ln1_g, ln1_b, router_w, router_b):
    rw = jnp.pad(router_w[l], ((0, 0), (0, LANES - N_EXPERTS)))
    rb = jnp.pad(router_b[l], (0, LANES - N_EXPERTS), constant_values=NEG)
    return ln1_g[l][None], ln1_b[l][None], rw, rb[None]


def kernel(x_prompt, x_sample, c, c_ctx, cache_diff_k, cache_diff_v, cache_na_k, cache_na_v, cache_gqa_k, cache_gqa_v, w_mod, b_mod, ln1_g, ln1_b, ln2_g, ln2_b, ab_w_in, ab_conv_w, ab_lambda_q1, ab_lambda_k1, ab_lambda_q2, ab_lambda_k2, ab_subln_g, ab_w_out, cd_w_in, cd_na_rpb, cd_q_norm_g, cd_k_norm_g, cd_w_out, router_w, router_b, w_gate_up, b_gate_up, w_down, b_down):
    xp = x_prompt.reshape(T_P, D)
    xs = x_sample.reshape(T_S, D)
    cond8 = jnp.concatenate([c_ctx[None], c, jnp.zeros((8 - 1 - B_S, D), F32)], axis=0)
    mod = _modulation(cond8, w_mod, b_mod)
    rope = _rope_tables()
    experts = (w_gate_up, b_gate_up, w_down, b_down)

    gb, y, q, k, v, new_diff_k, new_diff_v = _ab_in(xp, xs, mod, ab_w_in[0].astype(BF16), rope)
    lam_init = 0.8 - 0.6 * 1.0
    diff = (jnp.stack([ab_lambda_q1[0], ab_lambda_k1[0], ab_lambda_q2[0], ab_lambda_k2[0]]), ab_subln_g[0][None])
    attn = _flash_pair(q, k, v, latent=False, out_cols=512, col0=0, post="diff", diff=diff, lam_init=lam_init)
    attn = _flash_pair(q, k, v, latent=True, out_cols=512, col0=0, post="diff", diff=diff, lam_init=lam_init,
                       ctx=(cache_diff_k, cache_diff_v), ctx_mode="wide", prev=attn)
    routed = _ab_out(xp, xs, gb, y, ab_conv_w[0], attn, ab_w_out[0].astype(BF16), mod,
                     _router_tail(0, ln1_g, ln1_b, router_w, router_b))
    x = _moe_layer(0, routed, experts, mod, ln2_g, ln2_b)

    qg = jnp.tile(cd_q_norm_g[0], 2)[None]
    kg = jnp.tile(cd_k_norm_g[0], 2)[None]
    nq, nk, nv, gq, gk, gv, new_na_k, new_na_v, new_gqa_k, new_gqa_v = _cd_in(
        x, mod, cd_w_in[0].astype(BF16), qg, kg, rope)
    merged = _flash_pair(nq, nk, nv, latent=False, out_cols=D, col0=0, post="select")
    merged = _flash_pair(gq, gk, gv, latent=False, out_cols=D, col0=2, post="select", prev=merged)
    merged = _na_latent(nq, nk, nv, cache_na_k, cache_na_v, _na_bias(cd_na_rpb[0]), merged)
    merged = _flash_pair(gq, gk, gv, latent=True, out_cols=D, col0=2, post="select",
                         ctx=(cache_gqa_k, cache_gqa_v), ctx_mode="dup", prev=merged)
    routed = _cd_out(x, merged, cd_w_out[0].astype(BF16), mod,
                     _router_tail(1, ln1_g, ln1_b, router_w, router_b))
    y_p, y_s = _moe_layer(1, routed, experts, mod, ln2_g, ln2_b)

    return (y_p.reshape(B_P, N_P, D), y_s.reshape(B_S, N_S, D), new_diff_k, new_diff_v,
            new_na_k, new_na_v, new_gqa_k, new_gqa_v)
```

```python
import jax
import jax.numpy as jnp
import numpy as np
from jax import lax
from jax.experimental import pallas as pl
from jax.experimental.pallas import tpu as pltpu

F32 = jnp.float32
BF16 = jnp.bfloat16

D = 1024
B_P, N_P = 16, 256
B_S, N_S = 2, 4096
PAST = 256
T_P, T_S = B_P * N_P, B_S * N_S
T = T_P + T_S
TM = 256
HALF_TM = TM // 2
NT_P, NT_S, NT = T_P // TM, T_S // TM, T // TM
TILES_PER_GRID = N_S // TM
GRID_W = 64
GRID_H = N_S // GRID_W
HD = 64
DEPTH = 2
N_EXPERTS = 32
TOP_K = 4
D_FF = 1024
NA_WIN_R, NA_WIN_C = 8, 16
NA_QROWS = 4
NA_KROWS = 12
SWIGLU_LIMIT = 7.0
SWIGLU_ALPHA = 1.702
ROPE_THETA = 10000.0
DEEPNORM_ALPHA = (2 * DEPTH) ** 0.25
LN_EPS = 1e-5
RMS_EPS = 1e-6
LOG2E = 1.4426950408889634
QK_SCALE = HD ** -0.5 * LOG2E
NEG = -1e30
MOE_TM = 256
MOE_PAIRS = TOP_K * T
MOE_TILES = MOE_PAIRS // MOE_TM + N_EXPERTS
PAIRS_TM = TOP_K * TM
RUN_SIZES = tuple(TM >> b for b in range(TM.bit_length()))
LANES = 128
SUB = 8

VMEM_LIMIT = 56 * 1024 * 1024


def _params(*sem):
    return pltpu.CompilerParams(dimension_semantics=sem, vmem_limit_bytes=VMEM_LIMIT)


def _split(x):
    hi = x.astype(BF16)
    lo = (x - hi.astype(F32)).astype(BF16)
    return hi, lo


def _dot(a, b):
    return jnp.dot(a, b, preferred_element_type=F32)


def _dot3(a, b):
    ah, al = _split(a)
    bh, bl = _split(b)
    return _dot(ah, bh) + (_dot(ah, bl) + _dot(al, bh))


def _cond_row(i):
    return jnp.where(i < NT_P, 0, 1 + (i - NT_P) // TILES_PER_GRID)


def _layer_norm(z, g, b):
    mu = jnp.mean(z, axis=-1, keepdims=True)
    zc = z - mu
    var = jnp.mean(zc * zc, axis=-1, keepdims=True)
    return zc * lax.rsqrt(var + LN_EPS) * g + b


def _low_half(rows):
    return lax.broadcasted_iota(jnp.int32, (rows, LANES), 1) < HD


def _mod_kernel(c_ref, w_ref, b_ref, o_ref):
    c = c_ref[...]
    o_ref[...] = _dot3(c * jax.nn.sigmoid(c), w_ref[...]) + b_ref[...]


def _modulation(cond8, w_mod, b_mod):
    return pl.pallas_call(
        _mod_kernel,
        grid=(DEPTH, 6),
        in_specs=[pl.BlockSpec((8, D), lambda l, j: (0, 0)),
                  pl.BlockSpec((None, D, D), lambda l, j: (l, 0, j)),
                  pl.BlockSpec((None, 1, D), lambda l, j: (l, 0, j))],
        out_specs=pl.BlockSpec((None, 8, D), lambda l, j: (l, 0, j)),
        out_shape=jax.ShapeDtypeStruct((DEPTH, 8, 6 * D), F32),
        compiler_params=_params("arbitrary", "arbitrary"),
        name="modulation",
    )(cond8, w_mod, b_mod.reshape(DEPTH, 1, 6 * D))


def _mod_spec(layer, chunk):
    return pl.BlockSpec((None, 8, D), lambda i, _l=layer, _c=chunk: (_l, 0, _c))


def _full(shape):
    return pl.BlockSpec(shape, lambda i: (0,) * len(shape))


def _rope_tables():
    t = np.arange(N_S)
    half = HD // 2
    inv = ROPE_THETA ** (-np.arange(0, half, 2, dtype=np.float64) / half)
    inv_lane = np.tile(np.repeat(inv, 2), 2 * LANES // HD)
    lane = np.arange(LANES)
    by_row = (lane % HD) < half
    ang = np.where(by_row[None], (t // GRID_W)[:, None], (t % GRID_W)[:, None]) * inv_lane[None]
    cos, sin = np.cos(ang), np.sin(ang)
    even = (lane % 2) == 0
    return tuple(jnp.asarray(a, F32) for a in (cos, np.where(even, -sin, 0.0), np.where(even, 0.0, sin)))


def _rope(x, a, b, c):
    return x * a + pltpu.roll(x, LANES - 1, axis=1) * b + pltpu.roll(x, 1, axis=1) * c


def _rope_or_identity(identity, ra_ref, rb_ref, rc_ref):
    return (jnp.where(identity, 1.0, ra_ref[...]), jnp.where(identity, 0.0, rb_ref[...]),
            jnp.where(identity, 0.0, rc_ref[...]))


def _rope_spec():
    return pl.BlockSpec((TM, LANES), lambda i: (jnp.maximum(i - NT_P, 0) % TILES_PER_GRID, 0))


def _x_specs():
    return [pl.BlockSpec((TM, D), lambda i: (jnp.minimum(i, NT_P - 1), 0)),
            pl.BlockSpec((TM, D), lambda i: (jnp.maximum(i - NT_P, 0), 0))]


def _cache_spec(heads, width):
    return pl.BlockSpec((None, None, heads, N_P, width), lambda i: (jnp.minimum(i, NT_P - 1), 0, 0, 0, 0))


def _row_spec(width):
    return pl.BlockSpec((TM, width), lambda i: (i, 0))


def _hm_spec(n):
    return pl.BlockSpec((n, TM, LANES), lambda i: (0, i, 0))


AB_Q0, AB_K0, AB_V0 = 1536, 2048, 2560


def _tile(p, col0, j):
    return p[:, col0 + j * LANES:col0 + (j + 1) * LANES]


def _ab_in_kernel(xp_ref, xs_ref, sh_ref, sc_ref, w_ref, ra_ref, rb_ref, rc_ref,
                  gb_ref, y_ref, q_ref, k_ref, v_ref, kc_ref, vc_ref):
    i = pl.program_id(0)
    is_p = i < NT_P
    ci = _cond_row(i)
    x = jnp.where(is_p, xp_ref[...], xs_ref[...])
    u = x * (1.0 + sc_ref[pl.ds(ci, 1), :]) + sh_ref[pl.ds(ci, 1), :]
    p = _dot(u.astype(BF16), w_ref[...])
    gb_ref[...] = p[:, 0:512]
    y_ref[...] = p[:, 512:1024] * p[:, 1024:1536]
    a, b, c = _rope_or_identity(is_p, ra_ref, rb_ref, rc_ref)
    for h in range(4):
        v_ref[h] = _tile(p, AB_V0, h).astype(BF16)
        q_ref[h] = (_rope(_tile(p, AB_Q0, h), a, b, c) * QK_SCALE).astype(BF16)
        k_ref[h] = _rope(_tile(p, AB_K0, h), a, b, c).astype(BF16)

    @pl.when(is_p)
    def _():
        for h in range(4):
            kc_ref[h] = _tile(p, AB_K0, h)
            vc_ref[h] = _tile(p, AB_V0, h)


def _ab_in(xp, xs, mod, w_in, rope):
    hm = jax.ShapeDtypeStruct((4, T, LANES), BF16)
    cache = jax.ShapeDtypeStruct((B_P, 1, 4, N_P, LANES), F32)
    half = jax.ShapeDtypeStruct((T, 512), F32)
    return pl.pallas_call(
        _ab_in_kernel,
        grid=(NT,),
        in_specs=_x_specs() + [_mod_spec(0, 0), _mod_spec(0, 1), _full((D, 3072)),
                               _rope_spec(), _rope_spec(), _rope_spec()],
        out_specs=[_row_spec(512), _row_spec(512), _hm_spec(4), _hm_spec(4), _hm_spec(4),
                   _cache_spec(4, LANES), _cache_spec(4, LANES)],
        out_shape=[half, half, hm, hm, hm, cache, cache],
        compiler_params=_params("arbitrary"),
        name="ab_in_proj",
    )(xp, xs, mod, mod, w_in, *rope)


CD_NQ, CD_NK, CD_NV, CD_GQ, CD_GK, CD_GV = 0, 512, 1024, 1536, 2048, 2176


def _seg_mean64(s):
    r = lax.broadcasted_iota(jnp.int32, (LANES, LANES), 0) // HD
    c = lax.broadcasted_iota(jnp.int32, (LANES, LANES), 1) // HD
    seg = jnp.where(r == c, 1.0, 0.0).astype(BF16)
    hi, lo = _split(s)
    return (_dot(hi, seg) + _dot(lo, seg)) * (1.0 / HD)


def _rms64(x, g):
    return x * lax.rsqrt(_seg_mean64(x * x) + RMS_EPS) * g


def _dup_halves(x, lo):
    r = pltpu.roll(x, HD, axis=1)
    return jnp.where(lo, x, r), jnp.where(lo, r, x)


def _cd_in_kernel(x_ref, sh_ref, sc_ref, w_ref, qg_ref, kg_ref, ra_ref, rb_ref, rc_ref,
                  nq_ref, nk_ref, nv_ref, gq_ref, gk_ref, gv_ref,
                  nkc_ref, nvc_ref, gkc_ref, gvc_ref):
    i = pl.program_id(0)
    is_p = i < NT_P
    ci = _cond_row(i)
    u = x_ref[...] * (1.0 + sc_ref[pl.ds(ci, 1), :]) + sh_ref[pl.ds(ci, 1), :]
    p = _dot(u.astype(BF16), w_ref[...])
    lo = _low_half(TM)
    for j in range(4):
        nq_ref[j] = (_tile(p, CD_NQ, j) * QK_SCALE).astype(BF16)
        nk_ref[j] = _tile(p, CD_NK, j).astype(BF16)
        nv_ref[j] = _tile(p, CD_NV, j).astype(BF16)
    gq = [_rms64(_tile(p, CD_GQ, j), qg_ref[...]) for j in range(4)]
    gk = _rms64(_tile(p, CD_GK, 0), kg_ref[...])
    gv = _tile(p, CD_GV, 0)
    v0, v1 = _dup_halves(gv, lo)
    gv_ref[0] = v0.astype(BF16)
    gv_ref[1] = v1.astype(BF16)
    a, b, c = _rope_or_identity(is_p, ra_ref, rb_ref, rc_ref)
    for j in range(4):
        gq_ref[j] = (_rope(gq[j], a, b, c) * QK_SCALE).astype(BF16)
    k0, k1 = _dup_halves(_rope(gk, a, b, c), lo)
    gk_ref[0] = k0.astype(BF16)
    gk_ref[1] = k1.astype(BF16)

    @pl.when(is_p)
    def _():
        gkc_ref[0] = k0[:, 0:HD]
        gkc_ref[1] = k1[:, 0:HD]
        gvc_ref[0] = v0[:, 0:HD]
        gvc_ref[1] = v1[:, 0:HD]
        for j in range(4):
            for src, dst in ((CD_NK, nkc_ref), (CD_NV, nvc_ref)):
                a, b = _dup_halves(_tile(p, src, j), lo)
                dst[2 * j] = a[:, 0:HD]
                dst[2 * j + 1] = b[:, 0:HD]


def _cd_in(x, mod, w_in, qg, kg, rope):
    hm4 = jax.ShapeDtypeStruct((4, T, LANES), BF16)
    hm2 = jax.ShapeDtypeStruct((2, T, LANES), BF16)
    c8 = jax.ShapeDtypeStruct((B_P, 1, 8, N_P, HD), F32)
    c2 = jax.ShapeDtypeStruct((B_P, 1, 2, N_P, HD), F32)
    return pl.pallas_call(
        _cd_in_kernel,
        grid=(NT,),
        in_specs=[_row_spec(D), _mod_spec(1, 0), _mod_spec(1, 1), _full((D, 2304)),
                  _full((1, LANES)), _full((1, LANES)), _rope_spec(), _rope_spec(), _rope_spec()],
        out_specs=[_hm_spec(4), _hm_spec(4), _hm_spec(4), _hm_spec(4), _hm_spec(2), _hm_spec(2),
                   _cache_spec(8, HD), _cache_spec(8, HD), _cache_spec(2, HD), _cache_spec(2, HD)],
        out_shape=[hm4, hm4, hm4, hm4, hm2, hm2, c8, c8, c2, c2],
        compiler_params=_params("arbitrary"),
        name="cd_in_proj",
    )(x, mod, mod, w_in, qg, kg, *rope)


def _stack_pairs(q_ref, n_q, tq):
    lo = _low_half(tq)
    parts = []
    for j in range(n_q):
        q = q_ref[j]
        zero = jnp.zeros_like(q)
        parts += [jnp.where(lo, q, zero), jnp.where(lo, zero, q)]
    return jnp.concatenate(parts, axis=0), lo


def _qk(qs, kb):
    return lax.dot_general(qs, kb, (((1,), (1,)), ((), ())), preferred_element_type=F32)


def _ctx_tile(ref, mode):
    if mode == "wide":
        x = ref[...]
    elif mode == "pair":
        x = jnp.concatenate([ref[0], ref[1]], axis=1)
    else:
        x = jnp.concatenate([ref[...], ref[...]], axis=1)
    return x.astype(BF16)


def _flash_pair_kernel(*refs, n_q, tq, nk, tk, ctx_mode, post, lam_init):
    it = iter(refs)
    q_ref, k_ref, v_ref = next(it), next(it), next(it)
    kc_ref, vc_ref = (next(it), next(it)) if ctx_mode else (None, None)
    lam_ref, g_ref = (next(it), next(it)) if post == "diff" else (None, None)
    o_ref = next(it)

    qs, lo = _stack_pairs(q_ref, n_q, tq)
    rows = 2 * n_q * tq

    def step(kb, vb, carry):
        m, l, acc = carry
        s = _qk(qs, kb)
        m_new = jnp.maximum(m, jnp.max(s, axis=1, keepdims=True))
        alpha = jnp.exp2(m - m_new)
        p = jnp.exp2(s - m_new)
        l = alpha * l + jnp.sum(p, axis=1, keepdims=True)
        acc = alpha * acc + _dot(p.astype(BF16), vb)
        return m_new, l, acc

    carry = (jnp.full((rows, 1), NEG, F32), jnp.zeros((rows, 1), F32), jnp.zeros((rows, LANES), F32))
    for c in range(nk // tk):
        carry = step(k_ref[c * tk:(c + 1) * tk, :], v_ref[c * tk:(c + 1) * tk, :], carry)
    if ctx_mode:
        carry = step(_ctx_tile(kc_ref, ctx_mode), _ctx_tile(vc_ref, ctx_mode), carry)
    _, l, acc = carry
    o = acc / l

    if post == "diff":
        lp = lam_ref[...]
        lam = (jnp.exp(jnp.sum(lp[0:1] * lp[1:2], axis=1, keepdims=True))
               - jnp.exp(jnp.sum(lp[2:3] * lp[3:4], axis=1, keepdims=True)) + lam_init)
        a = o[0:tq] - lam * o[tq:2 * tq]
        ms = jnp.mean(a * a, axis=-1, keepdims=True)
        o_ref[...] = (a * lax.rsqrt(ms + RMS_EPS) * g_ref[...] * (1.0 - lam_init)).astype(o_ref.dtype)
    else:
        for j in range(n_q):
            o_ref[:, j * LANES:(j + 1) * LANES] = jnp.where(
                lo, o[2 * j * tq:(2 * j + 1) * tq], o[(2 * j + 1) * tq:(2 * j + 2) * tq]).astype(o_ref.dtype)


def _flash_pair(q, k, v, *, latent, out_cols, col0, post, ctx=None, ctx_mode=None, diff=None,
                lam_init=0.0, prev=None):
    groups = k.shape[0]
    n_q = q.shape[0] // groups
    if latent:
        tq, nk, tk = TM, N_S, 1024
        grid = (B_S, groups, TILES_PER_GRID)
        qrow = lambda b, g, i: NT_P + b * TILES_PER_GRID + i
        krow = lambda b, g, i: T_P // N_S + b
    else:
        tq, nk, tk = N_P, N_P, N_P
        grid = (B_P, groups, 1)
        qrow = lambda b, g, i: b
        krow = lambda b, g, i: b
    in_specs = [pl.BlockSpec((n_q, tq, LANES), lambda b, g, i: (g, qrow(b, g, i), 0)),
                pl.BlockSpec((None, nk, LANES), lambda b, g, i: (g, krow(b, g, i), 0)),
                pl.BlockSpec((None, nk, LANES), lambda b, g, i: (g, krow(b, g, i), 0))]
    args = [q, k, v]
    if ctx is not None:
        if ctx_mode == "wide":
            spec = pl.BlockSpec((None, None, None, PAST, LANES), lambda b, g, i: (b, 0, g, 0, 0))
        elif ctx_mode == "pair":
            spec = pl.BlockSpec((None, None, 2, PAST, HD), lambda b, g, i: (b, 0, g, 0, 0))
        else:
            spec = pl.BlockSpec((None, None, None, PAST, HD), lambda b, g, i: (b, 0, g, 0, 0))
        in_specs += [spec, spec]
        args += list(ctx)
    if diff is not None:
        in_specs += [pl.BlockSpec((4, HD), lambda b, g, i: (0, 0)),
                     pl.BlockSpec((1, LANES), lambda b, g, i: (0, 0))]
        args += list(diff)
    aliases = {}
    if prev is not None:
        aliases = {len(args): 0}
        in_specs.append(pl.BlockSpec(memory_space=pl.ANY))
        args.append(prev)

    def kernel(*refs):
        if prev is not None:
            refs = refs[:-2] + refs[-1:]
        _flash_pair_kernel(*refs, n_q=n_q, tq=tq, nk=nk, tk=tk, ctx_mode=ctx_mode if ctx is not None else None,
                           post=post, lam_init=lam_init)

    return pl.pallas_call(
        kernel,
        grid=grid,
        in_specs=in_specs,
        out_specs=pl.BlockSpec((tq, n_q * LANES), lambda b, g, i: (qrow(b, g, i), col0 + g)),
        out_shape=jax.ShapeDtypeStruct((T, out_cols), BF16),
        input_output_aliases=aliases,
        compiler_params=_params("arbitrary", "arbitrary", "arbitrary"),
        name="attn_" + post + ("_latent" if latent else "_context") + str(n_q),
    )(*args)


NA_TQ = NA_QROWS * GRID_W
NA_TK = NA_KROWS * GRID_W


NA_BLOCK_POS = ((0, 0), (NA_QROWS, 0), (GRID_H - NA_QROWS, GRID_H - NA_KROWS))
N_DR = 2 * NA_WIN_R - 1
N_DC = 2 * NA_WIN_C - 1


def _na_bias_kernel(rpb_ref, o_ref):
    qc = lax.broadcasted_iota(jnp.int32, (GRID_W, LANES), 0)
    lane = lax.broadcasted_iota(jnp.int32, (GRID_W, LANES), 1)
    kc = lane % GRID_W
    cs = jnp.clip(qc - NA_WIN_C // 2, 0, GRID_W - NA_WIN_C)
    col_ok = jnp.logical_and(kc >= cs, kc < cs + NA_WIN_C)
    lo = lane < GRID_W
    neg = jnp.full((GRID_W, LANES), NEG, F32)
    for head in range(2):
        toeplitz = []
        for dr in range(N_DR):
            r = jnp.broadcast_to(rpb_ref[head, dr:dr + 1, :] * LOG2E, (GRID_W, LANES))
            t = jnp.where(lo, pltpu.roll(r, LANES - (NA_WIN_C - 1), axis=1, stride=1, stride_axis=0),
                          pltpu.roll(r, GRID_W - (NA_WIN_C - 1), axis=1, stride=1, stride_axis=0))
            toeplitz.append(jnp.where(col_ok, t, neg))
        for pos, (r0, k0) in enumerate(NA_BLOCK_POS):
            for i in range(NA_QROWS):
                qr = r0 + i
                rs = min(max(qr - NA_WIN_R // 2, 0), GRID_H - NA_WIN_R)
                tiles = [toeplitz[k0 + j - qr + NA_WIN_R - 1] if rs <= k0 + j < rs + NA_WIN_R else neg
                         for j in range(NA_KROWS)]
                for jp in range(NA_KROWS // 2):
                    o_ref[pos, pl.ds(head * NA_TQ + i * GRID_W, GRID_W), pl.ds(jp * LANES, LANES)] = jnp.where(
                        lo, tiles[2 * jp], tiles[2 * jp + 1])


def _na_bias(rpb):
    rpb_pad = jnp.pad(rpb, ((0, 0), (0, 16 - N_DR), (0, LANES - N_DC)), constant_values=NEG)
    return pl.pallas_call(
        _na_bias_kernel,
        grid=(4,),
        in_specs=[pl.BlockSpec((2, 16, LANES), lambda g: (g, 0, 0))],
        out_specs=pl.BlockSpec((3, None, 2 * NA_TQ, NA_TK), lambda g: (0, g, 0, 0)),
        out_shape=jax.ShapeDtypeStruct((3, 4, 2 * NA_TQ, NA_TK), F32),
        compiler_params=_params("arbitrary"),
        name="na_bias_table",
    )(rpb_pad)


def _na_kernel(q_ref, k_ref, v_ref, kc_ref, vc_ref, bm_ref, _, o_ref):
    i = pl.program_id(2)
    qs, lo = _stack_pairs(q_ref, 1, NA_TQ)
    k0 = jnp.clip(i * NA_QROWS - NA_WIN_R // 2, 0, GRID_H - NA_KROWS)
    start = pl.multiple_of(k0 * GRID_W, GRID_W)
    kw = k_ref[pl.ds(start, NA_TK), :]
    vw = v_ref[pl.ds(start, NA_TK), :]
    s_w = _qk(qs, kw) + bm_ref[...]
    s_c = _qk(qs, _ctx_tile(kc_ref, "pair"))
    m = jnp.maximum(jnp.max(s_w, axis=1, keepdims=True), jnp.max(s_c, axis=1, keepdims=True))
    p_w = jnp.exp2(s_w - m)
    p_c = jnp.exp2(s_c - m)
    l = jnp.sum(p_w, axis=1, keepdims=True) + jnp.sum(p_c, axis=1, keepdims=True)
    o = (_dot(p_w.astype(BF16), vw) + _dot(p_c.astype(BF16), _ctx_tile(vc_ref, "pair"))) / l
    o_ref[...] = jnp.where(lo, o[0:NA_TQ], o[NA_TQ:2 * NA_TQ]).astype(o_ref.dtype)


def _na_latent(q, k, v, kc, vc, bm, prev):
    nblk = N_S // NA_TQ
    qrow = lambda b, g, i: T_P // NA_TQ + b * nblk + i
    krow = lambda b, g, i: T_P // N_S + b
    cfg = lambda i: jnp.where(i == 0, 0, jnp.where(i == nblk - 1, 2, 1))
    ctx_spec = pl.BlockSpec((None, None, 2, PAST, HD), lambda b, g, i: (b, 0, g, 0, 0))
    return pl.pallas_call(
        _na_kernel,
        grid=(B_S, 4, nblk),
        in_specs=[pl.BlockSpec((1, NA_TQ, LANES), lambda b, g, i: (g, qrow(b, g, i), 0)),
                  pl.BlockSpec((None, N_S, LANES), lambda b, g, i: (g, krow(b, g, i), 0)),
                  pl.BlockSpec((None, N_S, LANES), lambda b, g, i: (g, krow(b, g, i), 0)),
                  ctx_spec, ctx_spec,
                  pl.BlockSpec((None, None, 2 * NA_TQ, NA_TK), lambda b, g, i: (cfg(i), g, 0, 0)),
                  pl.BlockSpec(memory_space=pl.ANY)],
        out_specs=pl.BlockSpec((NA_TQ, LANES), lambda b, g, i: (qrow(b, g, i), g)),
        out_shape=jax.ShapeDtypeStruct((T, D), BF16),
        input_output_aliases={6: 0},
        compiler_params=_params("arbitrary", "arbitrary", "arbitrary"),
        name="attn_window_latent",
    )(q, k, v, kc, vc, bm, prev)


def _top4(logits):
    lane = lax.broadcasted_iota(jnp.int32, logits.shape, 1).astype(F32)
    rest = logits
    tops, firsts = [], []
    for _ in range(TOP_K):
        m = jnp.max(rest, axis=1, keepdims=True)
        first = jnp.min(jnp.where(rest == m, lane, float(LANES)), axis=1, keepdims=True)
        tops.append(m)
        firsts.append(first)
        rest = jnp.where(lane == first, -jnp.inf, rest)
    es = [jnp.exp(m - tops[0]) for m in tops]
    denom = es[0] + es[1] + es[2] + es[3]
    w = jnp.zeros_like(logits)
    for k in range(TOP_K):
        w = jnp.where(lane == float(k), es[k] / denom, w)
    return firsts, w


def _local_sort(firsts):
    lane = lax.broadcasted_iota(jnp.int32, (TM, LANES), 1).astype(F32)
    hots = [lane == f for f in firsts]
    sel = jnp.zeros((TM, LANES), F32)
    for hot in hots:
        sel = jnp.where(hot, 1.0, sel)
    r = lax.broadcasted_iota(jnp.int32, (TM, TM), 0)
    c = lax.broadcasted_iota(jnp.int32, (TM, TM), 1)
    earlier = _dot(jnp.where(c < r, 1.0, 0.0).astype(BF16), sel.astype(BF16))
    cnt = jnp.sum(sel, axis=0, keepdims=True)
    r = lax.broadcasted_iota(jnp.int32, (LANES, LANES), 0)
    c = lax.broadcasted_iota(jnp.int32, (LANES, LANES), 1)
    start = _dot(jnp.broadcast_to(cnt, (SUB, LANES)).astype(BF16), jnp.where(r < c, 1.0, 0.0).astype(BF16))[0:1]
    place = start + earlier
    pos = jnp.zeros((TM, LANES), F32)
    for k, hot in enumerate(hots):
        pos = jnp.where(lane == float(k), jnp.sum(jnp.where(hot, place, 0.0), axis=1, keepdims=True), pos)
    return pos.astype(jnp.int32), cnt.astype(jnp.int32)


def _mixer_tail(x_of, h_of, ci, g1_ref, sh2_ref, sc2_ref, lng_ref, lnb_ref, rw_ref, rb_ref,
                x1_ref, u2_ref, pos_ref, w_ref, cnt_ref):
    firsts = []
    for half in range(2):
        rows = slice(half * HALF_TM, (half + 1) * HALF_TM)
        x1 = _layer_norm(DEEPNORM_ALPHA * x_of(rows) + g1_ref[pl.ds(ci, 1), :] * h_of(rows),
                         lng_ref[...], lnb_ref[...])
        x1_ref[rows, :] = x1
        u2 = x1 * (1.0 + sc2_ref[pl.ds(ci, 1), :]) + sh2_ref[pl.ds(ci, 1), :]
        u2_ref[rows, :] = u2.astype(BF16)
        first, w_ref[rows, :] = _top4(_dot3(u2, rw_ref[...]) + rb_ref[...])
        firsts.append(first)
    pos, cnt = _local_sort([jnp.concatenate([firsts[0][k], firsts[1][k]], axis=0) for k in range(TOP_K)])
    pos_ref[...] = pos
    cnt_ref[...] = jnp.broadcast_to(cnt, (SUB, LANES))


def _ab_out_kernel(xp_ref, xs_ref, gb_ref, y_ref, yp_ref, yn_ref, cv_ref, at_ref, wo_ref,
                   g1_ref, sh2_ref, sc2_ref, lng_ref, lnb_ref, rw_ref, rb_ref, *out_refs):
    i = pl.program_id(0)
    is_p = i < NT_P
    ci = _cond_row(i)
    j = (i - NT_P) % TILES_PER_GRID
    first = jnp.logical_or(is_p, j == 0)
    last = jnp.logical_or(is_p, j == TILES_PER_GRID - 1)
    y = y_ref[...]
    row = lax.broadcasted_iota(jnp.int32, y.shape, 0)
    before = jnp.where(first, 0.0, yp_ref[7:8, :])
    after = jnp.where(last, 0.0, yn_ref[0:1, :])
    y_prev = jnp.where(row == 0, before, pltpu.roll(y, 1, axis=0))
    y_next = jnp.where(row == TM - 1, after, pltpu.roll(y, TM - 1, axis=0))
    cv = cv_ref[...]
    conv = (gb_ref[...] * (y_prev * cv[0:1] + y * cv[1:2] + y_next * cv[2:3])).astype(BF16)
    _mixer_tail(lambda r: jnp.where(is_p, xp_ref[r, :], xs_ref[r, :]),
                lambda r: _dot(conv[r], wo_ref[0:512, :]) + _dot(at_ref[r, :], wo_ref[512:1024, :]),
                ci, g1_ref, sh2_ref, sc2_ref, lng_ref, lnb_ref, rw_ref, rb_ref, *out_refs)


def _cd_out_kernel(x_ref, mg_ref, wo_ref, g1_ref, sh2_ref, sc2_ref, lng_ref, lnb_ref, rw_ref, rb_ref,
                   *out_refs):
    ci = _cond_row(pl.program_id(0))
    _mixer_tail(lambda r: x_ref[r, :], lambda r: _dot(mg_ref[r, :], wo_ref[...]),
                ci, g1_ref, sh2_ref, sc2_ref, lng_ref, lnb_ref, rw_ref, rb_ref, *out_refs)


def _tail_specs(layer):
    return [_mod_spec(layer, 2), _mod_spec(layer, 3), _mod_spec(layer, 4),
            _full((1, D)), _full((1, D)), _full((D, LANES)), _full((1, LANES))]


_TAIL_OUT_SPECS = [_row_spec(D), _row_spec(D), _row_spec(LANES), _row_spec(LANES),
                   pl.BlockSpec((None, SUB, LANES), lambda i: (i, 0, 0))]
_TAIL_OUT_SHAPES = [jax.ShapeDtypeStruct((T, D), F32), jax.ShapeDtypeStruct((T, D), BF16),
                    jax.ShapeDtypeStruct((T, LANES), jnp.int32), jax.ShapeDtypeStruct((T, LANES), F32),
                    jax.ShapeDtypeStruct((NT, SUB, LANES), jnp.int32)]


def _ab_out(xp, xs, gb, y, conv_w, attn, w_out, mod, tail):
    halo_prev = pl.BlockSpec((8, 512), lambda i: (jnp.maximum(i * (TM // 8) - 1, 0), 0))
    halo_next = pl.BlockSpec((8, 512), lambda i: (jnp.minimum((i + 1) * (TM // 8), T // 8 - 1), 0))
    return pl.pallas_call(
        _ab_out_kernel,
        grid=(NT,),
        in_specs=_x_specs() + [_row_spec(512), _row_spec(512), halo_prev, halo_next, _full((3, 512)),
                               _row_spec(512), _full((D, D))] + _tail_specs(0),
        out_specs=_TAIL_OUT_SPECS,
        out_shape=_TAIL_OUT_SHAPES,
        compiler_params=_params("arbitrary"),
        name="ab_out_proj",
    )(xp, xs, gb, y, y, y, conv_w, attn, w_out, mod, mod, mod, *tail)


def _cd_out(x, merged, w_out, mod, tail):
    return pl.pallas_call(
        _cd_out_kernel,
        grid=(NT,),
        in_specs=[_row_spec(D), _row_spec(D), _full((D, D))] + _tail_specs(1),
        out_specs=_TAIL_OUT_SPECS,
        out_shape=_TAIL_OUT_SHAPES,
        compiler_params=_params("arbitrary"),
        name="cd_out_proj",
    )(x, merged, w_out, mod, mod, mod, *tail)


def _mix_down(wd_ref, wmix_ref):
    half = D_FF // 2
    for c in range(D // LANES):
        wmix_ref[c, pl.ds(0, half, stride=2), :] = wd_ref[0:half, c * LANES:(c + 1) * LANES]
        wmix_ref[c, pl.ds(1, half, stride=2), :] = wd_ref[half:D_FF, c * LANES:(c + 1) * LANES]


def _prep_expert(wgu_ref, wd_ref, wgu_bf, wmix_ref, wd_bf):
    for c in range(4):
        wgu_bf[:, c * 512:(c + 1) * 512] = wgu_ref[:, c * 512:(c + 1) * 512].astype(BF16)
    _mix_down(wd_ref, wmix_ref)
    for c in range(D // LANES):
        wd_bf[:, c * LANES:(c + 1) * LANES] = wmix_ref[c].astype(BF16)


def _expert_ffn(u, wgu_bf, bgu_ref, wd_bf, bd_ref):
    rows = u.shape[0]
    ga = _dot(u, wgu_bf[:, 0:D_FF]) + bgu_ref[:, 0:D_FF]
    gb = _dot(u, wgu_bf[:, D_FF:2 * D_FF]) + bgu_ref[:, D_FF:2 * D_FF]
    even = (lax.broadcasted_iota(jnp.int32, (rows, LANES), 1) % 2) == 0
    hid = []
    for c in range(D_FF // LANES):
        a = ga[:, c * LANES:(c + 1) * LANES]
        b = gb[:, c * LANES:(c + 1) * LANES]
        gate = jnp.where(even, a, pltpu.roll(b, 1, axis=1))
        up = jnp.where(even, pltpu.roll(a, LANES - 1, axis=1), b)
        gate = jnp.minimum(gate, SWIGLU_LIMIT)
        up = jnp.clip(up, -SWIGLU_LIMIT, SWIGLU_LIMIT)
        hid.append(((up + 1.0) * gate * jax.nn.sigmoid(SWIGLU_ALPHA * gate)).astype(BF16))
    hid = jnp.concatenate(hid, axis=1)
    return _dot(hid, wd_bf[...]) + bd_ref[...]


def _moe_plan(cnt):
    c = cnt[:, 0, :N_EXPERTS]
    counts = jnp.sum(c, axis=0)
    tiles_e = (counts + MOE_TM - 1) // MOE_TM
    tile_end = jnp.cumsum(tiles_e)
    first_row = (tile_end - tiles_e) * MOE_TM
    run_start = first_row[None] + jnp.cumsum(c, axis=0) - c
    j = jnp.arange(MOE_TILES, dtype=jnp.int32)
    tile_expert = jnp.minimum(jnp.sum((j[:, None] >= tile_end[None]).astype(jnp.int32), axis=1), N_EXPERTS - 1)
    i32 = lambda a: a.astype(jnp.int32)
    experts = jnp.arange(N_EXPERTS, dtype=jnp.int32)
    owns = tiles_e > 0
    seg = jnp.cumsum(owns.astype(jnp.int32)) - owns.astype(jnp.int32)
    later = jnp.where((experts[None] > experts[:, None]) & owns[None], experts[None], N_EXPERTS)
    nxt = jnp.min(later, axis=1)
    nxt = jnp.where(nxt == N_EXPERTS, -1, nxt)
    return dict(tile_expert=i32(tile_expert), n_used=i32(tile_end[-1:]),
                seg_parity=i32(seg[tile_expert] % 2), next_expert=i32(nxt[tile_expert]),
                run_start=i32(run_start)[:, None, :], run_len=i32(c)[:, None, :],
                pad_start=i32(first_row + counts)[None, None, :], pad_len=i32(tiles_e * MOE_TM - counts)[None, None, :])


def _for_each_run(start_ref, len_ref, copy):
    def body(e, local):
        n = len_ref[0, e]
        g = start_ref[0, e]

        def pieces(sizes):
            for size in sizes:
                covered = jnp.bitwise_and(n, -2 * size)

                @pl.when(jnp.bitwise_and(n, size) != 0)
                def _(covered=covered, size=size):
                    copy(local + covered, g + covered, size)

        @pl.when(n >= RUN_SIZES[2])
        def _():
            pieces(RUN_SIZES[:3])

        pieces(RUN_SIZES[3:])
        return local + n
    lax.fori_loop(0, N_EXPERTS, body, jnp.int32(0))


def _rows(ref, start, size):
    start = 0 if isinstance(start, int) and start == 0 else pl.multiple_of(start * SUB, SUB)
    return ref.at[pl.ds(start, size * SUB), :]


def _from_tiles(ref, rows):
    return jnp.concatenate([ref[pl.ds(c, rows, stride=SUB), :] for c in range(SUB)], axis=1)


def _to_tiles(ref, x):
    for c in range(SUB):
        ref[pl.ds(c, x.shape[0], stride=SUB), :] = x[:, c * LANES:(c + 1) * LANES]


def _run_spec(index):
    return pl.BlockSpec((None, 1, N_EXPERTS), index, memory_space=pltpu.SMEM)


def _dispatch_kernel(rs_ref, rl_ref, ps_ref, pn_ref, u_ref, pos_ref, xs_hbm, stage, zeros, sem, zsem):
    i = pl.program_id(0)
    slot = i % 2

    def wait(s):
        pltpu.make_async_copy(stage.at[s], stage.at[s], sem.at[s]).wait()

    @pl.when(i == 0)
    def _():
        zeros[...] = jnp.zeros_like(zeros)

        def pad(local, g, size):
            return pltpu.make_async_copy(_rows(zeros, 0, size), _rows(xs_hbm, g, size), zsem.at[0])
        _for_each_run(ps_ref, pn_ref, lambda *a: pad(*a).start())
        _for_each_run(ps_ref, pn_ref, lambda *a: pad(*a).wait())

    @pl.when(i >= 2)
    def _():
        wait(slot)

    place = pos_ref[...].astype(F32).T
    s = lax.broadcasted_iota(jnp.int32, (PAIRS_TM, TM), 0).astype(F32)
    onehot = jnp.zeros((PAIRS_TM, TM), F32)
    for k in range(TOP_K):
        onehot = jnp.where(s == place[k:k + 1, :], 1.0, onehot)
    _to_tiles(stage.at[slot], _dot(onehot.astype(BF16), u_ref[...]))
    _for_each_run(rs_ref, rl_ref, lambda l, g, size: pltpu.make_async_copy(
        _rows(stage.at[slot], l, size), _rows(xs_hbm, g, size), sem.at[slot]).start())

    @pl.when(i == NT - 1)
    def _():
        wait(1 - slot)
        wait(slot)


def _dispatch(u2, pos, plan):
    return pl.pallas_call(
        _dispatch_kernel,
        grid=(NT,),
        in_specs=[_run_spec(lambda i: (i, 0, 0)), _run_spec(lambda i: (i, 0, 0)),
                  _run_spec(lambda i: (0, 0, 0)), _run_spec(lambda i: (0, 0, 0)),
                  _row_spec(D), _row_spec(LANES)],
        out_specs=pl.BlockSpec(memory_space=pl.ANY),
        out_shape=jax.ShapeDtypeStruct((MOE_TILES * MOE_TM * SUB, LANES), F32),
        scratch_shapes=[pltpu.VMEM((2, PAIRS_TM * SUB, LANES), F32), pltpu.VMEM((TM * SUB, LANES), F32),
                        pltpu.SemaphoreType.DMA((2,)), pltpu.SemaphoreType.DMA((1,))],
        compiler_params=_params("arbitrary"),
        name="moe_dispatch",
    )(plan["run_start"], plan["run_len"], plan["pad_start"], plan["pad_len"], u2, pos)


def _moe_kernel(te_ref, nu_ref, par_ref, nxt_ref, x_ref, bgu_ref, bd_ref, wgu_hbm, wd_hbm, y_ref,
                wgu_f, wd_f, wgu_bf, wmix_ref, wd_bf, sem, *, layer):
    j = pl.program_id(0)
    used = j < nu_ref[0]
    buf = par_ref[j]

    def copies(e, s):
        return (pltpu.make_async_copy(wgu_hbm.at[layer, e], wgu_f.at[s], sem.at[s]),
                pltpu.make_async_copy(wd_hbm.at[layer, e], wd_f.at[s], sem.at[s]))

    @pl.when(jnp.logical_and(used, j == 0))
    def _():
        for cp in copies(te_ref[0], 0):
            cp.start()

    @pl.when(jnp.logical_and(used, jnp.logical_or(j == 0, te_ref[j] != te_ref[jnp.maximum(j - 1, 0)])))
    def _():
        for cp in copies(te_ref[j], buf):
            cp.wait()

        @pl.when(nxt_ref[j] >= 0)
        def _():
            for cp in copies(nxt_ref[j], 1 - buf):
                cp.start()

        _prep_expert(wgu_f.at[buf], wd_f.at[buf], wgu_bf, wmix_ref, wd_bf)

    @pl.when(used)
    def _():
        x = _from_tiles(x_ref, MOE_TM).astype(BF16)
        _to_tiles(y_ref, _expert_ffn(x, wgu_bf, bgu_ref, wd_bf, bd_ref))


def _moe(layer, xs, plan, w_gate_up, b_gate_up, w_down, b_down):
    rows = pl.BlockSpec((MOE_TM * SUB, LANES), lambda j, te, nu, par, nxt: (jnp.minimum(j, nu[0] - 1), 0))
    grid_spec = pltpu.PrefetchScalarGridSpec(
        num_scalar_prefetch=4,
        grid=(MOE_TILES,),
        in_specs=[rows,
                  pl.BlockSpec((None, None, 1, 2 * D_FF), lambda j, te, nu, par, nxt: (layer, te[j], 0, 0)),
                  pl.BlockSpec((None, None, 1, D), lambda j, te, nu, par, nxt: (layer, te[j], 0, 0)),
                  pl.BlockSpec(memory_space=pl.ANY), pl.BlockSpec(memory_space=pl.ANY)],
        out_specs=rows,
        scratch_shapes=[pltpu.VMEM((2, D, 2 * D_FF), F32), pltpu.VMEM((2, D_FF, D), F32),
                        pltpu.VMEM((D, 2 * D_FF), BF16), pltpu.VMEM((D // LANES, D_FF, LANES), F32),
                        pltpu.VMEM((D_FF, D), BF16), pltpu.SemaphoreType.DMA((2,))])

    def kernel(*refs):
        _moe_kernel(*refs, layer=layer)

    return pl.pallas_call(
        kernel,
        grid_spec=grid_spec,
        out_shape=jax.ShapeDtypeStruct((MOE_TILES * MOE_TM * SUB, LANES), F32),
        compiler_params=_params("arbitrary"),
        name="moe_experts",
    )(plan["tile_expert"], plan["n_used"], plan["seg_parity"], plan["next_expert"], xs,
      b_gate_up.reshape(DEPTH, N_EXPERTS, 1, 2 * D_FF), b_down.reshape(DEPTH, N_EXPERTS, 1, D), w_gate_up, w_down)


def _combine_kernel(rs_ref, rl_ref, rsn_ref, rln_ref, x1_ref, pos_ref, w_ref, g2_ref, lng_ref, lnb_ref, ys_hbm,
                    *rest, final):
    out_refs, (stage, sem) = rest[:-2], rest[-2:]
    i = pl.program_id(0)
    slot = i % 2
    other = 1 - slot

    def fetch(start_ref, len_ref, s):
        _for_each_run(start_ref, len_ref, lambda l, g, size: pltpu.make_async_copy(
            _rows(ys_hbm, g, size), _rows(stage.at[s], l, size), sem.at[s]).start())

    def wait(s):
        pltpu.make_async_copy(stage.at[s], stage.at[s], sem.at[s]).wait()

    @pl.when(i == 0)
    def _():
        fetch(rs_ref, rl_ref, 0)

    fetch(rsn_ref, rln_ref, other)
    wait(slot)
    y = _from_tiles(stage.at[slot], PAIRS_TM).astype(BF16)
    place = pos_ref[...].astype(F32)
    w = w_ref[...]
    s = lax.broadcasted_iota(jnp.int32, (TM, PAIRS_TM), 1).astype(F32)
    pick = jnp.zeros((TM, PAIRS_TM), F32)
    for k in range(TOP_K):
        pick = jnp.where(s == place[:, k:k + 1], w[:, k:k + 1], pick)
    f = _dot(pick.astype(BF16), y)
    ci = _cond_row(i)
    z = DEEPNORM_ALPHA * x1_ref[...] + g2_ref[pl.ds(ci, 1), :] * f
    out = _layer_norm(z, lng_ref[...], lnb_ref[...])
    if final:
        @pl.when(i < NT_P)
        def _():
            out_refs[0][...] = out

        @pl.when(i >= NT_P)
        def _():
            out_refs[1][...] = out
    else:
        out_refs[0][...] = out

    @pl.when(i == NT - 1)
    def _():
        wait(other)


def _combine(layer, x1, ys, pos, w, plan, mod, g, b):
    final = layer == DEPTH - 1
    nxt = lambda i: (jnp.minimum(i + 1, NT - 1), 0, 0)
    cur = lambda i: (i, 0, 0)
    if final:
        out_specs = _x_specs()
        out_shape = [jax.ShapeDtypeStruct((T_P, D), F32), jax.ShapeDtypeStruct((T_S, D), F32)]
    else:
        out_specs = _row_spec(D)
        out_shape = jax.ShapeDtypeStruct((T, D), F32)

    def kernel(*refs):
        _combine_kernel(*refs, final=final)

    return pl.pallas_call(
        kernel,
        grid=(NT,),
        in_specs=[_run_spec(cur), _run_spec(cur), _run_spec(nxt), _run_spec(nxt),
                  _row_spec(D), _row_spec(LANES), _row_spec(LANES), _mod_spec(layer, 5), _full((1, D)), _full((1, D)),
                  pl.BlockSpec(memory_space=pl.ANY)],
        out_specs=out_specs,
        out_shape=out_shape,
        scratch_shapes=[pltpu.VMEM((2, PAIRS_TM * SUB, LANES), F32), pltpu.SemaphoreType.DMA((2,))],
        compiler_params=_params("arbitrary"),
        name="moe_combine_norm" + ("_final" if final else ""),
    )(plan["run_start"], plan["run_len"], plan["run_start"], plan["run_len"], x1, pos, w, mod, g, b, ys)


def _moe_layer(layer, routed, experts, mod, ln2_g, ln2_b):
    x1, u2, pos, w, cnt = routed
    plan = _moe_plan(cnt)
    ys = _moe(layer, _dispatch(u2, pos, plan), plan, *experts)
    return _combine(layer, x1, ys, pos, w, plan, mod, ln2_g[layer][None], ln2_b[layer][None])


def _router_tail(l, ln1_g, ln1_b, router_w, router_b):
    rw = jnp.pad(router_w[l], ((0, 0), (0, LANES - N_EXPERTS)))
    rb = jnp.pad(router_b[l], (0, LANES - N_EXPERTS), constant_values=NEG)
    return ln1_g[l][None], ln1_b[l][None], rw, rb[None]


def kernel(x_prompt, x_sample, c, c_ctx, cache_diff_k, cache_diff_v, cache_na_k, cache_na_v, cache_gqa_k, cache_gqa_v, w_mod, b_mod, ln1_g, ln1_b, ln2_g, ln2_b, ab_w_in, ab_conv_w, ab_lambda_q1, ab_lambda_k1, ab_lambda_q2, ab_lambda_k2, ab_subln_g, ab_w_out, cd_w_in, cd_na_rpb, cd_q_norm_g, cd_k_norm_g, cd_w_out, router_w, router_b, w_gate_up, b_gate_up, w_down, b_down):
    xp = x_prompt.reshape(T_P, D)
    xs = x_sample.reshape(T_S, D)
    cond8 = jnp.concatenate([c_ctx[None], c, jnp.zeros((8 - 1 - B_S, D), F32)], axis=0)
    mod = _modulation(cond8, w_mod, b_mod)
    rope = _rope_tables()
    experts = (w_gate_up, b_gate_up, w_down, b_down)

    gb, y, q, k, v, new_diff_k, new_diff_v = _ab_in(xp, xs, mod, ab_w_in[0].astype(BF16), rope)
    lam_init = 0.8 - 0.6 * 1.0
    diff = (jnp.stack([ab_lambda_q1[0], ab_lambda_k1[0], ab_lambda_q2[0], ab_lambda_k2[0]]), ab_subln_g[0][None])
    attn = _flash_pair(q, k, v, latent=False, out_cols=512, col0=0, post="diff", diff=diff, lam_init=lam_init)
    attn = _flash_pair(q, k, v, latent=True, out_cols=512, col0=0, post="diff", diff=diff, lam_init=lam_init,
                       ctx=(cache_diff_k, cache_diff_v), ctx_mode="wide", prev=attn)
    routed = _ab_out(xp, xs, gb, y, ab_conv_w[0], attn, ab_w_out[0].astype(BF16), mod,
                     _router_tail(0, ln1_g, ln1_b, router_w, router_b))
    x = _moe_layer(0, routed, experts, mod, ln2_g, ln2_b)

    qg = jnp.tile(cd_q_norm_g[0], 2)[None]
    kg = jnp.tile(cd_k_norm_g[0], 2)[None]
    nq, nk, nv, gq, gk, gv, new_na_k, new_na_v, new_gqa_k, new_gqa_v = _cd_in(
        x, mod, cd_w_in[0].astype(BF16), qg, kg, rope)
    merged = _flash_pair(nq, nk, nv, latent=False, out_cols=D, col0=0, post="select")
    merged = _flash_pair(gq, gk, gv, latent=False, out_cols=D, col0=2, post="select", prev=merged)
    merged = _na_latent(nq, nk, nv, cache_na_k, cache_na_v, _na_bias(cd_na_rpb[0]), merged)
    merged = _flash_pair(gq, gk, gv, latent=True, out_cols=D, col0=2, post="select",
                         ctx=(cache_gqa_k, cache_gqa_v), ctx_mode="dup", prev=merged)
    routed = _cd_out(x, merged, cd_w_out[0].astype(BF16), mod,
                     _router_tail(1, ln1_g, ln1_b, router_w, router_b))
    y_p, y_s = _moe_layer(1, routed, experts, mod, ln2_g, ln2_b)

    return (y_p.reshape(B_P, N_P, D), y_s.reshape(B_S, N_S, D), new_diff_k, new_diff_v,
            new_na_k, new_na_v, new_gqa_k, new_gqa_v)
```

```python
import jax
import jax.numpy as jnp
import numpy as np
from jax import lax
from jax.experimental import pallas as pl
from jax.experimental.pallas import tpu as pltpu

F32 = jnp.float32
BF16 = jnp.bfloat16

D = 1024
B_P, N_P = 16, 256
B_S, N_S = 2, 4096
PAST = 256
T_P, T_S = B_P * N_P, B_S * N_S
T = T_P + T_S
TM = 256
NT_P, NT_S, NT = T_P // TM, T_S // TM, T // TM
TILES_PER_GRID = N_S // TM
GRID_W = 64
GRID_H = N_S // GRID_W
HD = 64
DEPTH = 2
N_EXPERTS = 32
TOP_K = 4
D_FF = 1024
NA_WIN_R, NA_WIN_C = 8, 16
NA_QROWS = 4
NA_KROWS = 12
SWIGLU_LIMIT = 7.0
SWIGLU_ALPHA = 1.702
ROPE_THETA = 10000.0
DEEPNORM_ALPHA = (2 * DEPTH) ** 0.25
LN_EPS = 1e-5
RMS_EPS = 1e-6
LOG2E = 1.4426950408889634
QK_SCALE = HD ** -0.5 * LOG2E
NEG = -1e30
MOE_TM = 256
MOE_PAIRS = TOP_K * T
MOE_TILES = MOE_PAIRS // MOE_TM + N_EXPERTS
PAIRS_TM = TOP_K * TM
RUN_SIZES = tuple(TM >> b for b in range(TM.bit_length()))
LANES = 128
SUB = 8

VMEM_LIMIT = 56 * 1024 * 1024


def _params(*sem):
    return pltpu.CompilerParams(dimension_semantics=sem, vmem_limit_bytes=VMEM_LIMIT)


def _split(x):
    hi = x.astype(BF16)
    lo = (x - hi.astype(F32)).astype(BF16)
    return hi, lo


def _dot(a, b):
    return jnp.dot(a, b, preferred_element_type=F32)


def _dot3(a, b):
    ah, al = _split(a)
    bh, bl = _split(b)
    return _dot(ah, bh) + (_dot(ah, bl) + _dot(al, bh))


def _cond_row(i):
    return jnp.where(i < NT_P, 0, 1 + (i - NT_P) // TILES_PER_GRID)


def _layer_norm(z, g, b):
    mu = jnp.mean(z, axis=-1, keepdims=True)
    zc = z - mu
    var = jnp.mean(zc * zc, axis=-1, keepdims=True)
    return zc * lax.rsqrt(var + LN_EPS) * g + b


def _low_half(rows):
    return lax.broadcasted_iota(jnp.int32, (rows, LANES), 1) < HD


def _mod_kernel(c_ref, w_ref, b_ref, o_ref):
    c = c_ref[...]
    o_ref[...] = _dot3(c * jax.nn.sigmoid(c), w_ref[...]) + b_ref[...]


def _modulation(cond8, w_mod, b_mod):
    return pl.pallas_call(
        _mod_kernel,
        grid=(DEPTH, 6),
        in_specs=[pl.BlockSpec((8, D), lambda l, j: (0, 0)),
                  pl.BlockSpec((None, D, D), lambda l, j: (l, 0, j)),
                  pl.BlockSpec((None, 1, D), lambda l, j: (l, 0, j))],
        out_specs=pl.BlockSpec((None, 8, D), lambda l, j: (l, 0, j)),
        out_shape=jax.ShapeDtypeStruct((DEPTH, 8, 6 * D), F32),
        compiler_params=_params("arbitrary", "arbitrary"),
        name="modulation",
    )(cond8, w_mod, b_mod.reshape(DEPTH, 1, 6 * D))


def _mod_spec(layer, chunk):
    return pl.BlockSpec((None, 8, D), lambda i, _l=layer, _c=chunk: (_l, 0, _c))


def _full(shape):
    return pl.BlockSpec(shape, lambda i: (0,) * len(shape))


def _rope_tables():
    t = np.arange(N_S)
    half = HD // 2
    inv = ROPE_THETA ** (-np.arange(0, half, 2, dtype=np.float64) / half)
    inv_lane = np.tile(np.repeat(inv, 2), 2 * LANES // HD)
    lane = np.arange(LANES)
    by_row = (lane % HD) < half
    ang = np.where(by_row[None], (t // GRID_W)[:, None], (t % GRID_W)[:, None]) * inv_lane[None]
    cos, sin = np.cos(ang), np.sin(ang)
    even = (lane % 2) == 0
    return tuple(jnp.asarray(a, F32) for a in (cos, np.where(even, -sin, 0.0), np.where(even, 0.0, sin)))


def _rope(x, a, b, c):
    return x * a + pltpu.roll(x, LANES - 1, axis=1) * b + pltpu.roll(x, 1, axis=1) * c


def _rope_or_identity(identity, ra_ref, rb_ref, rc_ref):
    return (jnp.where(identity, 1.0, ra_ref[...]), jnp.where(identity, 0.0, rb_ref[...]),
            jnp.where(identity, 0.0, rc_ref[...]))


def _rope_spec():
    return pl.BlockSpec((TM, LANES), lambda i: (jnp.maximum(i - NT_P, 0) % TILES_PER_GRID, 0))


def _x_specs():
    return [pl.BlockSpec((TM, D), lambda i: (jnp.minimum(i, NT_P - 1), 0)),
            pl.BlockSpec((TM, D), lambda i: (jnp.maximum(i - NT_P, 0), 0))]


def _cache_spec(heads, width):
    return pl.BlockSpec((None, None, heads, N_P, width), lambda i: (jnp.minimum(i, NT_P - 1), 0, 0, 0, 0))


def _row_spec(width):
    return pl.BlockSpec((TM, width), lambda i: (i, 0))


def _hm_spec(n):
    return pl.BlockSpec((n, TM, LANES), lambda i: (0, i, 0))


AB_Q0, AB_K0, AB_V0 = 1536, 2048, 2560


def _tile(p, col0, j):
    return p[:, col0 + j * LANES:col0 + (j + 1) * LANES]


def _ab_in_kernel(xp_ref, xs_ref, sh_ref, sc_ref, w_ref, ra_ref, rb_ref, rc_ref,
                  gb_ref, y_ref, q_ref, k_ref, v_ref, kc_ref, vc_ref):
    i = pl.program_id(0)
    is_p = i < NT_P
    ci = _cond_row(i)
    x = jnp.where(is_p, xp_ref[...], xs_ref[...])
    u = x * (1.0 + sc_ref[pl.ds(ci, 1), :]) + sh_ref[pl.ds(ci, 1), :]
    p = _dot(u.astype(BF16), w_ref[...])
    gb_ref[...] = p[:, 0:512]
    y_ref[...] = p[:, 512:1024] * p[:, 1024:1536]
    a, b, c = _rope_or_identity(is_p, ra_ref, rb_ref, rc_ref)
    for h in range(4):
        v_ref[h] = _tile(p, AB_V0, h).astype(BF16)
        q_ref[h] = (_rope(_tile(p, AB_Q0, h), a, b, c) * QK_SCALE).astype(BF16)
        k_ref[h] = _rope(_tile(p, AB_K0, h), a, b, c).astype(BF16)

    @pl.when(is_p)
    def _():
        for h in range(4):
            kc_ref[h] = _tile(p, AB_K0, h)
            vc_ref[h] = _tile(p, AB_V0, h)


def _ab_in(xp, xs, mod, w_in, rope):
    hm = jax.ShapeDtypeStruct((4, T, LANES), BF16)
    cache = jax.ShapeDtypeStruct((B_P, 1, 4, N_P, LANES), F32)
    half = jax.ShapeDtypeStruct((T, 512), F32)
    return pl.pallas_call(
        _ab_in_kernel,
        grid=(NT,),
        in_specs=_x_specs() + [_mod_spec(0, 0), _mod_spec(0, 1), _full((D, 3072)),
                               _rope_spec(), _rope_spec(), _rope_spec()],
        out_specs=[_row_spec(512), _row_spec(512), _hm_spec(4), _hm_spec(4), _hm_spec(4),
                   _cache_spec(4, LANES), _cache_spec(4, LANES)],
        out_shape=[half, half, hm, hm, hm, cache, cache],
        compiler_params=_params("arbitrary"),
        name="ab_in_proj",
    )(xp, xs, mod, mod, w_in, *rope)


CD_NQ, CD_NK, CD_NV, CD_GQ, CD_GK, CD_GV = 0, 512, 1024, 1536, 2048, 2176


def _seg_mean64(s):
    r = lax.broadcasted_iota(jnp.int32, (LANES, LANES), 0) // HD
    c = lax.broadcasted_iota(jnp.int32, (LANES, LANES), 1) // HD
    seg = jnp.where(r == c, 1.0, 0.0).astype(BF16)
    hi, lo = _split(s)
    return (_dot(hi, seg) + _dot(lo, seg)) * (1.0 / HD)


def _rms64(x, g):
    return x * lax.rsqrt(_seg_mean64(x * x) + RMS_EPS) * g


def _dup_halves(x, lo):
    r = pltpu.roll(x, HD, axis=1)
    return jnp.where(lo, x, r), jnp.where(lo, r, x)


def _cd_in_kernel(x_ref, sh_ref, sc_ref, w_ref, qg_ref, kg_ref, ra_ref, rb_ref, rc_ref,
                  nq_ref, nk_ref, nv_ref, gq_ref, gk_ref, gv_ref,
                  nkc_ref, nvc_ref, gkc_ref, gvc_ref):
    i = pl.program_id(0)
    is_p = i < NT_P
    ci = _cond_row(i)
    u = x_ref[...] * (1.0 + sc_ref[pl.ds(ci, 1), :]) + sh_ref[pl.ds(ci, 1), :]
    p = _dot(u.astype(BF16), w_ref[...])
    lo = _low_half(TM)
    for j in range(4):
        nq_ref[j] = (_tile(p, CD_NQ, j) * QK_SCALE).astype(BF16)
        nk_ref[j] = _tile(p, CD_NK, j).astype(BF16)
        nv_ref[j] = _tile(p, CD_NV, j).astype(BF16)
    gq = [_rms64(_tile(p, CD_GQ, j), qg_ref[...]) for j in range(4)]
    gk = _rms64(_tile(p, CD_GK, 0), kg_ref[...])
    gv = _tile(p, CD_GV, 0)
    v0, v1 = _dup_halves(gv, lo)
    gv_ref[0] = v0.astype(BF16)
    gv_ref[1] = v1.astype(BF16)
    a, b, c = _rope_or_identity(is_p, ra_ref, rb_ref, rc_ref)
    for j in range(4):
        gq_ref[j] = (_rope(gq[j], a, b, c) * QK_SCALE).astype(BF16)
    k0, k1 = _dup_halves(_rope(gk, a, b, c), lo)
    gk_ref[0] = k0.astype(BF16)
    gk_ref[1] = k1.astype(BF16)

    @pl.when(is_p)
    def _():
        gkc_ref[0] = k0[:, 0:HD]
        gkc_ref[1] = k1[:, 0:HD]
        gvc_ref[0] = v0[:, 0:HD]
        gvc_ref[1] = v1[:, 0:HD]
        for j in range(4):
            for src, dst in ((CD_NK, nkc_ref), (CD_NV, nvc_ref)):
                a, b = _dup_halves(_tile(p, src, j), lo)
                dst[2 * j] = a[:, 0:HD]
                dst[2 * j + 1] = b[:, 0:HD]


def _cd_in(x, mod, w_in, qg, kg, rope):
    hm4 = jax.ShapeDtypeStruct((4, T, LANES), BF16)
    hm2 = jax.ShapeDtypeStruct((2, T, LANES), BF16)
    c8 = jax.ShapeDtypeStruct((B_P, 1, 8, N_P, HD), F32)
    c2 = jax.ShapeDtypeStruct((B_P, 1, 2, N_P, HD), F32)
    return pl.pallas_call(
        _cd_in_kernel,
        grid=(NT,),
        in_specs=[_row_spec(D), _mod_spec(1, 0), _mod_spec(1, 1), _full((D, 2304)),
                  _full((1, LANES)), _full((1, LANES)), _rope_spec(), _rope_spec(), _rope_spec()],
        out_specs=[_hm_spec(4), _hm_spec(4), _hm_spec(4), _hm_spec(4), _hm_spec(2), _hm_spec(2),
                   _cache_spec(8, HD), _cache_spec(8, HD), _cache_spec(2, HD), _cache_spec(2, HD)],
        out_shape=[hm4, hm4, hm4, hm4, hm2, hm2, c8, c8, c2, c2],
        compiler_params=_params("arbitrary"),
        name="cd_in_proj",
    )(x, mod, mod, w_in, qg, kg, *rope)


def _stack_pairs(q_ref, n_q, tq):
    lo = _low_half(tq)
    parts = []
    for j in range(n_q):
        q = q_ref[j]
        zero = jnp.zeros_like(q)
        parts += [jnp.where(lo, q, zero), jnp.where(lo, zero, q)]
    return jnp.concatenate(parts, axis=0), lo


def _qk(qs, kb):
    return lax.dot_general(qs, kb, (((1,), (1,)), ((), ())), preferred_element_type=F32)


def _ctx_tile(ref, mode):
    if mode == "wide":
        x = ref[...]
    elif mode == "pair":
        x = jnp.concatenate([ref[0], ref[1]], axis=1)
    else:
        x = jnp.concatenate([ref[...], ref[...]], axis=1)
    return x.astype(BF16)


def _with_ones(v):
    lane = lax.broadcasted_iota(jnp.int32, v.shape, 1)
    return jnp.concatenate([v, jnp.where(lane == 0, 1.0, 0.0).astype(v.dtype)], axis=1)


def _flash_pair_kernel(*refs, n_q, tq, nk, tk, ctx_mode, post, lam_init):
    it = iter(refs)
    q_ref, k_ref, v_ref = next(it), next(it), next(it)
    kc_ref, vc_ref = (next(it), next(it)) if ctx_mode else (None, None)
    lam_ref, g_ref = (next(it), next(it)) if post == "diff" else (None, None)
    o_ref = next(it)

    qs, lo = _stack_pairs(q_ref, n_q, tq)
    rows = 2 * n_q * tq

    def step(kb, vb, carry):
        m, acc = carry
        s = _qk(qs, kb)
        m_new = jnp.maximum(m, jnp.max(s, axis=1, keepdims=True))
        p = jnp.exp2((s - m_new).astype(BF16))
        return m_new, jnp.exp2(m - m_new) * acc + _dot(p, _with_ones(vb))

    carry = (jnp.full((rows, 1), NEG, F32), jnp.zeros((rows, 2 * LANES), F32))
    for c in range(nk // tk):
        carry = step(k_ref[c * tk:(c + 1) * tk, :], v_ref[c * tk:(c + 1) * tk, :], carry)
    if ctx_mode:
        carry = step(_ctx_tile(kc_ref, ctx_mode), _ctx_tile(vc_ref, ctx_mode), carry)
    _, acc = carry
    o = acc[:, 0:LANES] / acc[:, LANES:LANES + 1]

    if post == "diff":
        lp = lam_ref[...]
        lam = (jnp.exp(jnp.sum(lp[0:1] * lp[1:2], axis=1, keepdims=True))
               - jnp.exp(jnp.sum(lp[2:3] * lp[3:4], axis=1, keepdims=True)) + lam_init)
        a = o[0:tq] - lam * o[tq:2 * tq]
        ms = jnp.mean(a * a, axis=-1, keepdims=True)
        o_ref[...] = (a * lax.rsqrt(ms + RMS_EPS) * g_ref[...] * (1.0 - lam_init)).astype(o_ref.dtype)
    else:
        for j in range(n_q):
            o_ref[:, j * LANES:(j + 1) * LANES] = jnp.where(
                lo, o[2 * j * tq:(2 * j + 1) * tq], o[(2 * j + 1) * tq:(2 * j + 2) * tq]).astype(o_ref.dtype)


def _flash_pair(q, k, v, *, latent, out_cols, col0, post, ctx=None, ctx_mode=None, diff=None,
                lam_init=0.0, prev=None):
    groups = k.shape[0]
    n_q = q.shape[0] // groups
    if latent:
        tq, nk, tk = TM, N_S, 1024
        grid = (B_S, groups, TILES_PER_GRID)
        qrow = lambda b, g, i: NT_P + b * TILES_PER_GRID + i
        krow = lambda b, g, i: T_P // N_S + b
    else:
        tq, nk, tk = N_P, N_P, N_P
        grid = (B_P, groups, 1)
        qrow = lambda b, g, i: b
        krow = lambda b, g, i: b
    in_specs = [pl.BlockSpec((n_q, tq, LANES), lambda b, g, i: (g, qrow(b, g, i), 0)),
                pl.BlockSpec((None, nk, LANES), lambda b, g, i: (g, krow(b, g, i), 0)),
                pl.BlockSpec((None, nk, LANES), lambda b, g, i: (g, krow(b, g, i), 0))]
    args = [q, k, v]
    if ctx is not None:
        if ctx_mode == "wide":
            spec = pl.BlockSpec((None, None, None, PAST, LANES), lambda b, g, i: (b, 0, g, 0, 0))
        elif ctx_mode == "pair":
            spec = pl.BlockSpec((None, None, 2, PAST, HD), lambda b, g, i: (b, 0, g, 0, 0))
        else:
            spec = pl.BlockSpec((None, None, None, PAST, HD), lambda b, g, i: (b, 0, g, 0, 0))
        in_specs += [spec, spec]
        args += list(ctx)
    if diff is not None:
        in_specs += [pl.BlockSpec((4, HD), lambda b, g, i: (0, 0)),
                     pl.BlockSpec((1, LANES), lambda b, g, i: (0, 0))]
        args += list(diff)
    aliases = {}
    if prev is not None:
        aliases = {len(args): 0}
        in_specs.append(pl.BlockSpec(memory_space=pl.ANY))
        args.append(prev)

    def kernel(*refs):
        if prev is not None:
            refs = refs[:-2] + refs[-1:]
        _flash_pair_kernel(*refs, n_q=n_q, tq=tq, nk=nk, tk=tk, ctx_mode=ctx_mode if ctx is not None else None,
                           post=post, lam_init=lam_init)

    return pl.pallas_call(
        kernel,
        grid=grid,
        in_specs=in_specs,
        out_specs=pl.BlockSpec((tq, n_q * LANES), lambda b, g, i: (qrow(b, g, i), col0 + g)),
        out_shape=jax.ShapeDtypeStruct((T, out_cols), BF16),
        input_output_aliases=aliases,
        compiler_params=_params("arbitrary", "arbitrary", "arbitrary"),
        name="attn_" + post + ("_latent" if latent else "_context") + str(n_q),
    )(*args)


NA_TQ = NA_QROWS * GRID_W
NA_TK = NA_KROWS * GRID_W


NA_BLOCK_POS = ((0, 0), (NA_QROWS, 0), (GRID_H - NA_QROWS, GRID_H - NA_KROWS))
N_DR = 2 * NA_WIN_R - 1
N_DC = 2 * NA_WIN_C - 1


def _na_bias_kernel(rpb_ref, o_ref):
    qc = lax.broadcasted_iota(jnp.int32, (GRID_W, LANES), 0)
    lane = lax.broadcasted_iota(jnp.int32, (GRID_W, LANES), 1)
    kc = lane % GRID_W
    cs = jnp.clip(qc - NA_WIN_C // 2, 0, GRID_W - NA_WIN_C)
    col_ok = jnp.logical_and(kc >= cs, kc < cs + NA_WIN_C)
    lo = lane < GRID_W
    neg = jnp.full((GRID_W, LANES), NEG, F32)
    for head in range(2):
        toeplitz = []
        for dr in range(N_DR):
            r = jnp.broadcast_to(rpb_ref[head, dr:dr + 1, :] * LOG2E, (GRID_W, LANES))
            t = jnp.where(lo, pltpu.roll(r, LANES - (NA_WIN_C - 1), axis=1, stride=1, stride_axis=0),
                          pltpu.roll(r, GRID_W - (NA_WIN_C - 1), axis=1, stride=1, stride_axis=0))
            toeplitz.append(jnp.where(col_ok, t, neg))
        for pos, (r0, k0) in enumerate(NA_BLOCK_POS):
            for i in range(NA_QROWS):
                qr = r0 + i
                rs = min(max(qr - NA_WIN_R // 2, 0), GRID_H - NA_WIN_R)
                tiles = [toeplitz[k0 + j - qr + NA_WIN_R - 1] if rs <= k0 + j < rs + NA_WIN_R else neg
                         for j in range(NA_KROWS)]
                for jp in range(NA_KROWS // 2):
                    o_ref[pos, pl.ds(head * NA_TQ + i * GRID_W, GRID_W), pl.ds(jp * LANES, LANES)] = jnp.where(
                        lo, tiles[2 * jp], tiles[2 * jp + 1])


def _na_bias(rpb):
    rpb_pad = jnp.pad(rpb, ((0, 0), (0, 16 - N_DR), (0, LANES - N_DC)), constant_values=NEG)
    return pl.pallas_call(
        _na_bias_kernel,
        grid=(4,),
        in_specs=[pl.BlockSpec((2, 16, LANES), lambda g: (g, 0, 0))],
        out_specs=pl.BlockSpec((3, None, 2 * NA_TQ, NA_TK), lambda g: (0, g, 0, 0)),
        out_shape=jax.ShapeDtypeStruct((3, 4, 2 * NA_TQ, NA_TK), F32),
        compiler_params=_params("arbitrary"),
        name="na_bias_table",
    )(rpb_pad)


def _na_kernel(q_ref, k_ref, v_ref, kc_ref, vc_ref, bm_ref, _, o_ref):
    i = pl.program_id(2)
    qs, lo = _stack_pairs(q_ref, 1, NA_TQ)
    k0 = jnp.clip(i * NA_QROWS - NA_WIN_R // 2, 0, GRID_H - NA_KROWS)
    start = pl.multiple_of(k0 * GRID_W, GRID_W)
    kw = k_ref[pl.ds(start, NA_TK), :]
    vw = v_ref[pl.ds(start, NA_TK), :]
    s_w = _qk(qs, kw) + bm_ref[...]
    s_c = _qk(qs, _ctx_tile(kc_ref, "pair"))
    m = jnp.maximum(jnp.max(s_w, axis=1, keepdims=True), jnp.max(s_c, axis=1, keepdims=True))
    p_w = jnp.exp2((s_w - m).astype(BF16))
    p_c = jnp.exp2((s_c - m).astype(BF16))
    acc = _dot(p_w, _with_ones(vw)) + _dot(p_c, _with_ones(_ctx_tile(vc_ref, "pair")))
    o = acc[:, 0:LANES] / acc[:, LANES:LANES + 1]
    o_ref[...] = jnp.where(lo, o[0:NA_TQ], o[NA_TQ:2 * NA_TQ]).astype(o_ref.dtype)


def _na_latent(q, k, v, kc, vc, bm, prev):
    nblk = N_S // NA_TQ
    qrow = lambda b, g, i: T_P // NA_TQ + b * nblk + i
    krow = lambda b, g, i: T_P // N_S + b
    cfg = lambda i: jnp.where(i == 0, 0, jnp.where(i == nblk - 1, 2, 1))
    ctx_spec = pl.BlockSpec((None, None, 2, PAST, HD), lambda b, g, i: (b, 0, g, 0, 0))
    return pl.pallas_call(
        _na_kernel,
        grid=(B_S, 4, nblk),
        in_specs=[pl.BlockSpec((1, NA_TQ, LANES), lambda b, g, i: (g, qrow(b, g, i), 0)),
                  pl.BlockSpec((None, N_S, LANES), lambda b, g, i: (g, krow(b, g, i), 0)),
                  pl.BlockSpec((None, N_S, LANES), lambda b, g, i: (g, krow(b, g, i), 0)),
                  ctx_spec, ctx_spec,
                  pl.BlockSpec((None, None, 2 * NA_TQ, NA_TK), lambda b, g, i: (cfg(i), g, 0, 0)),
                  pl.BlockSpec(memory_space=pl.ANY)],
        out_specs=pl.BlockSpec((NA_TQ, LANES), lambda b, g, i: (qrow(b, g, i), g)),
        out_shape=jax.ShapeDtypeStruct((T, D), BF16),
        input_output_aliases={6: 0},
        compiler_params=_params("arbitrary", "arbitrary", "arbitrary"),
        name="attn_window_latent",
    )(q, k, v, kc, vc, bm, prev)


def _top4(logits):
    lane = lax.broadcasted_iota(jnp.int32, logits.shape, 1).astype(F32)
    rest = logits
    tops, firsts = [], []
    for _ in range(TOP_K):
        m = jnp.max(rest, axis=1, keepdims=True)
        first = jnp.min(jnp.where(rest == m, lane, float(LANES)), axis=1, keepdims=True)
        tops.append(m)
        firsts.append(first)
        rest = jnp.where(lane == first, -jnp.inf, rest)
    es = [jnp.exp(m - tops[0]) for m in tops]
    denom = es[0] + es[1] + es[2] + es[3]
    w = jnp.zeros_like(logits)
    for k in range(TOP_K):
        w = jnp.where(lane == float(k), es[k] / denom, w)
    return firsts, w


def _local_sort(firsts):
    lane = lax.broadcasted_iota(jnp.int32, (TM, LANES), 1).astype(F32)
    hots = [lane == f for f in firsts]
    sel = jnp.zeros((TM, LANES), F32)
    for hot in hots:
        sel = jnp.where(hot, 1.0, sel)
    r = lax.broadcasted_iota(jnp.int32, (TM, TM), 0)
    c = lax.broadcasted_iota(jnp.int32, (TM, TM), 1)
    earlier = _dot(jnp.where(c < r, 1.0, 0.0).astype(BF16), sel.astype(BF16))
    cnt = jnp.sum(sel, axis=0, keepdims=True)
    r = lax.broadcasted_iota(jnp.int32, (LANES, LANES), 0)
    c = lax.broadcasted_iota(jnp.int32, (LANES, LANES), 1)
    start = _dot(jnp.broadcast_to(cnt, (SUB, LANES)).astype(BF16), jnp.where(r < c, 1.0, 0.0).astype(BF16))[0:1]
    place = start + earlier
    pos = jnp.zeros((TM, LANES), F32)
    for k, hot in enumerate(hots):
        pos = jnp.where(lane == float(k), jnp.sum(jnp.where(hot, place, 0.0), axis=1, keepdims=True), pos)
    return pos.astype(jnp.int32), cnt.astype(jnp.int32)


def _mixer_tail(x_of, h_of, ci, g1_ref, sh2_ref, sc2_ref, lng_ref, lnb_ref, rw_ref, rb_ref,
                x1_ref, u2_ref, pos_ref, w_ref, cnt_ref):
    rows = slice(0, TM)
    x1 = _layer_norm(DEEPNORM_ALPHA * x_of(rows) + g1_ref[pl.ds(ci, 1), :] * h_of(rows), lng_ref[...], lnb_ref[...])
    x1_ref[...] = x1
    u2 = x1 * (1.0 + sc2_ref[pl.ds(ci, 1), :]) + sh2_ref[pl.ds(ci, 1), :]
    u2_ref[...] = u2.astype(BF16)
    firsts, w_ref[...] = _top4(_dot3(u2, rw_ref[...]) + rb_ref[...])
    pos, cnt = _local_sort(firsts)
    pos_ref[...] = pos
    cnt_ref[...] = jnp.broadcast_to(cnt, (SUB, LANES))


def _ab_out_kernel(xp_ref, xs_ref, gb_ref, y_ref, yp_ref, yn_ref, cv_ref, at_ref, wo_ref,
                   g1_ref, sh2_ref, sc2_ref, lng_ref, lnb_ref, rw_ref, rb_ref, *out_refs):
    i = pl.program_id(0)
    is_p = i < NT_P
    ci = _cond_row(i)
    j = (i - NT_P) % TILES_PER_GRID
    first = jnp.logical_or(is_p, j == 0)
    last = jnp.logical_or(is_p, j == TILES_PER_GRID - 1)
    y = y_ref[...]
    row = lax.broadcasted_iota(jnp.int32, y.shape, 0)
    before = jnp.where(first, 0.0, yp_ref[7:8, :])
    after = jnp.where(last, 0.0, yn_ref[0:1, :])
    y_prev = jnp.where(row == 0, before, pltpu.roll(y, 1, axis=0))
    y_next = jnp.where(row == TM - 1, after, pltpu.roll(y, TM - 1, axis=0))
    cv = cv_ref[...]
    conv = (gb_ref[...] * (y_prev * cv[0:1] + y * cv[1:2] + y_next * cv[2:3])).astype(BF16)
    _mixer_tail(lambda r: jnp.where(is_p, xp_ref[r, :], xs_ref[r, :]),
                lambda r: _dot(conv[r], wo_ref[0:512, :]) + _dot(at_ref[r, :], wo_ref[512:1024, :]),
                ci, g1_ref, sh2_ref, sc2_ref, lng_ref, lnb_ref, rw_ref, rb_ref, *out_refs)


def _cd_out_kernel(x_ref, mg_ref, wo_ref, g1_ref, sh2_ref, sc2_ref, lng_ref, lnb_ref, rw_ref, rb_ref,
                   *out_refs):
    ci = _cond_row(pl.program_id(0))
    _mixer_tail(lambda r: x_ref[r, :], lambda r: _dot(mg_ref[r, :], wo_ref[...]),
                ci, g1_ref, sh2_ref, sc2_ref, lng_ref, lnb_ref, rw_ref, rb_ref, *out_refs)


def _tail_specs(layer):
    return [_mod_spec(layer, 2), _mod_spec(layer, 3), _mod_spec(layer, 4),
            _full((1, D)), _full((1, D)), _full((D, LANES)), _full((1, LANES))]


_TAIL_OUT_SPECS = [_row_spec(D), _row_spec(D), _row_spec(LANES), _row_spec(LANES),
                   pl.BlockSpec((None, SUB, LANES), lambda i: (i, 0, 0))]
_TAIL_OUT_SHAPES = [jax.ShapeDtypeStruct((T, D), F32), jax.ShapeDtypeStruct((T, D), BF16),
                    jax.ShapeDtypeStruct((T, LANES), jnp.int32), jax.ShapeDtypeStruct((T, LANES), F32),
                    jax.ShapeDtypeStruct((NT, SUB, LANES), jnp.int32)]


def _ab_out(xp, xs, gb, y, conv_w, attn, w_out, mod, tail):
    halo_prev = pl.BlockSpec((8, 512), lambda i: (jnp.maximum(i * (TM // 8) - 1, 0), 0))
    halo_next = pl.BlockSpec((8, 512), lambda i: (jnp.minimum((i + 1) * (TM // 8), T // 8 - 1), 0))
    return pl.pallas_call(
        _ab_out_kernel,
        grid=(NT,),
        in_specs=_x_specs() + [_row_spec(512), _row_spec(512), halo_prev, halo_next, _full((3, 512)),
                               _row_spec(512), _full((D, D))] + _tail_specs(0),
        out_specs=_TAIL_OUT_SPECS,
        out_shape=_TAIL_OUT_SHAPES,
        compiler_params=_params("arbitrary"),
        name="ab_out_proj",
    )(xp, xs, gb, y, y, y, conv_w, attn, w_out, mod, mod, mod, *tail)


def _cd_out(x, merged, w_out, mod, tail):
    return pl.pallas_call(
        _cd_out_kernel,
        grid=(NT,),
        in_specs=[_row_spec(D), _row_spec(D), _full((D, D))] + _tail_specs(1),
        out_specs=_TAIL_OUT_SPECS,
        out_shape=_TAIL_OUT_SHAPES,
        compiler_params=_params("arbitrary"),
        name="cd_out_proj",
    )(x, merged, w_out, mod, mod, mod, *tail)


def _mix_down(wd_ref, wmix_ref):
    half = D_FF // 2
    for c in range(D // LANES):
        wmix_ref[c, pl.ds(0, half, stride=2), :] = wd_ref[0:half, c * LANES:(c + 1) * LANES]
        wmix_ref[c, pl.ds(1, half, stride=2), :] = wd_ref[half:D_FF, c * LANES:(c + 1) * LANES]


def _prep_expert(wgu_ref, wd_ref, wgu_bf, wmix_ref, wd_bf):
    for c in range(4):
        wgu_bf[:, c * 512:(c + 1) * 512] = wgu_ref[:, c * 512:(c + 1) * 512].astype(BF16)
    _mix_down(wd_ref, wmix_ref)
    for c in range(D // LANES):
        wd_bf[:, c * LANES:(c + 1) * LANES] = wmix_ref[c].astype(BF16)


def _expert_ffn(u, wgu_bf, bgu_ref, wd_bf, bd_ref):
    rows = u.shape[0]
    ga = _dot(u, wgu_bf[:, 0:D_FF]) + bgu_ref[:, 0:D_FF]
    gb = _dot(u, wgu_bf[:, D_FF:2 * D_FF]) + bgu_ref[:, D_FF:2 * D_FF]
    even = (lax.broadcasted_iota(jnp.int32, (rows, LANES), 1) % 2) == 0
    hid = []
    for c in range(D_FF // LANES):
        a = ga[:, c * LANES:(c + 1) * LANES]
        b = gb[:, c * LANES:(c + 1) * LANES]
        gate = jnp.where(even, a, pltpu.roll(b, 1, axis=1))
        up = jnp.where(even, pltpu.roll(a, LANES - 1, axis=1), b)
        gate = jnp.minimum(gate, SWIGLU_LIMIT)
        up = jnp.clip(up, -SWIGLU_LIMIT, SWIGLU_LIMIT)
        hid.append(((up + 1.0) * gate * jax.nn.sigmoid(SWIGLU_ALPHA * gate)).astype(BF16))
    hid = jnp.concatenate(hid, axis=1)
    return _dot(hid, wd_bf[...]) + bd_ref[...]


def _moe_plan(cnt):
    c = cnt[:, 0, :N_EXPERTS]
    counts = jnp.sum(c, axis=0)
    tiles_e = (counts + MOE_TM - 1) // MOE_TM
    tile_end = jnp.cumsum(tiles_e)
    first_row = (tile_end - tiles_e) * MOE_TM
    run_start = first_row[None] + jnp.cumsum(c, axis=0) - c
    j = jnp.arange(MOE_TILES, dtype=jnp.int32)
    tile_expert = jnp.minimum(jnp.sum((j[:, None] >= tile_end[None]).astype(jnp.int32), axis=1), N_EXPERTS - 1)
    i32 = lambda a: a.astype(jnp.int32)
    experts = jnp.arange(N_EXPERTS, dtype=jnp.int32)
    owns = tiles_e > 0
    seg = jnp.cumsum(owns.astype(jnp.int32)) - owns.astype(jnp.int32)
    later = jnp.where((experts[None] > experts[:, None]) & owns[None], experts[None], N_EXPERTS)
    nxt = jnp.min(later, axis=1)
    nxt = jnp.where(nxt == N_EXPERTS, -1, nxt)
    per_tile = lambda v: jnp.sum(jnp.where(tile_expert[:, None] == experts[None], v[None], 0), axis=1)
    return dict(tile_expert=i32(tile_expert), n_used=i32(tile_end[-1:]),
                seg_parity=i32(per_tile(seg % 2)), next_expert=i32(per_tile(nxt)),
                run_start=i32(run_start)[:, None, :], run_len=i32(c)[:, None, :],
                pad_start=i32(first_row + counts)[None, None, :], pad_len=i32(tiles_e * MOE_TM - counts)[None, None, :])


def _for_each_run(start_ref, len_ref, copy):
    def body(e, local):
        n = len_ref[0, e]
        g = start_ref[0, e]

        def pieces(sizes):
            for size in sizes:
                covered = jnp.bitwise_and(n, -2 * size)

                @pl.when(jnp.bitwise_and(n, size) != 0)
                def _(covered=covered, size=size):
                    copy(local + covered, g + covered, size)

        @pl.when(n >= RUN_SIZES[2])
        def _():
            pieces(RUN_SIZES[:3])

        pieces(RUN_SIZES[3:])
        return local + n
    lax.fori_loop(0, N_EXPERTS, body, jnp.int32(0))


def _rows(ref, start, size):
    start = 0 if isinstance(start, int) and start == 0 else pl.multiple_of(start * SUB, SUB)
    return ref.at[pl.ds(start, size * SUB), :]


def _from_tiles(ref, rows):
    return jnp.concatenate([ref[pl.ds(c, rows, stride=SUB), :] for c in range(SUB)], axis=1)


def _to_tiles(ref, x):
    for c in range(SUB):
        ref[pl.ds(c, x.shape[0], stride=SUB), :] = x[:, c * LANES:(c + 1) * LANES]


def _run_spec(index):
    return pl.BlockSpec((None, 1, N_EXPERTS), index, memory_space=pltpu.SMEM)


def _dispatch_kernel(rs_ref, rl_ref, ps_ref, pn_ref, u_ref, pos_ref, xs_hbm, stage, zeros, sem, zsem):
    i = pl.program_id(0)
    slot = i % 2

    def wait(s):
        pltpu.make_async_copy(stage.at[s], stage.at[s], sem.at[s]).wait()

    @pl.when(i == 0)
    def _():
        zeros[...] = jnp.zeros_like(zeros)

        def pad(local, g, size):
            return pltpu.make_async_copy(_rows(zeros, 0, size), _rows(xs_hbm, g, size), zsem.at[0])
        _for_each_run(ps_ref, pn_ref, lambda *a: pad(*a).start())
        _for_each_run(ps_ref, pn_ref, lambda *a: pad(*a).wait())

    @pl.when(i >= 2)
    def _():
        wait(slot)

    place = pos_ref[...].astype(F32).T
    s = lax.broadcasted_iota(jnp.int32, (PAIRS_TM, TM), 0).astype(F32)
    onehot = jnp.zeros((PAIRS_TM, TM), F32)
    for k in range(TOP_K):
        onehot = jnp.where(s == place[k:k + 1, :], 1.0, onehot)
    _to_tiles(stage.at[slot], _dot(onehot.astype(BF16), u_ref[...]))
    _for_each_run(rs_ref, rl_ref, lambda l, g, size: pltpu.make_async_copy(
        _rows(stage.at[slot], l, size), _rows(xs_hbm, g, size), sem.at[slot]).start())

    @pl.when(i == NT - 1)
    def _():
        wait(1 - slot)
        wait(slot)


def _dispatch(u2, pos, plan):
    return pl.pallas_call(
        _dispatch_kernel,
        grid=(NT,),
        in_specs=[_run_spec(lambda i: (i, 0, 0)), _run_spec(lambda i: (i, 0, 0)),
                  _run_spec(lambda i: (0, 0, 0)), _run_spec(lambda i: (0, 0, 0)),
                  _row_spec(D), _row_spec(LANES)],
        out_specs=pl.BlockSpec(memory_space=pl.ANY),
        out_shape=jax.ShapeDtypeStruct((MOE_TILES * MOE_TM * SUB, LANES), F32),
        scratch_shapes=[pltpu.VMEM((2, PAIRS_TM * SUB, LANES), F32), pltpu.VMEM((TM * SUB, LANES), F32),
                        pltpu.SemaphoreType.DMA((2,)), pltpu.SemaphoreType.DMA((1,))],
        compiler_params=_params("arbitrary"),
        name="moe_dispatch",
    )(plan["run_start"], plan["run_len"], plan["pad_start"], plan["pad_len"], u2, pos)


def _moe_kernel(te_ref, nu_ref, par_ref, nxt_ref, x_ref, bgu_ref, bd_ref, wgu_hbm, wd_hbm, y_ref,
                wgu_f, wd_f, wgu_bf, wmix_ref, wd_bf, sem, *, layer):
    j = pl.program_id(0)
    used = j < nu_ref[0]
    buf = par_ref[j]

    def copies(e, s):
        return (pltpu.make_async_copy(wgu_hbm.at[layer, e], wgu_f.at[s], sem.at[s]),
                pltpu.make_async_copy(wd_hbm.at[layer, e], wd_f.at[s], sem.at[s]))

    @pl.when(jnp.logical_and(used, j == 0))
    def _():
        for cp in copies(te_ref[0], 0):
            cp.start()

    @pl.when(jnp.logical_and(used, jnp.logical_or(j == 0, te_ref[j] != te_ref[jnp.maximum(j - 1, 0)])))
    def _():
        for cp in copies(te_ref[j], buf):
            cp.wait()

        @pl.when(nxt_ref[j] >= 0)
        def _():
            for cp in copies(nxt_ref[j], 1 - buf):
                cp.start()

        _prep_expert(wgu_f.at[buf], wd_f.at[buf], wgu_bf, wmix_ref, wd_bf)

    @pl.when(used)
    def _():
        x = _from_tiles(x_ref, MOE_TM).astype(BF16)
        _to_tiles(y_ref, _expert_ffn(x, wgu_bf, bgu_ref, wd_bf, bd_ref))


def _moe(layer, xs, plan, w_gate_up, b_gate_up, w_down, b_down):
    rows = pl.BlockSpec((MOE_TM * SUB, LANES), lambda j, te, nu, par, nxt: (jnp.minimum(j, nu[0] - 1), 0))
    grid_spec = pltpu.PrefetchScalarGridSpec(
        num_scalar_prefetch=4,
        grid=(MOE_TILES,),
        in_specs=[rows,
                  pl.BlockSpec((None, None, 1, 2 * D_FF), lambda j, te, nu, par, nxt: (layer, te[j], 0, 0)),
                  pl.BlockSpec((None, None, 1, D), lambda j, te, nu, par, nxt: (layer, te[j], 0, 0)),
                  pl.BlockSpec(memory_space=pl.ANY), pl.BlockSpec(memory_space=pl.ANY)],
        out_specs=rows,
        scratch_shapes=[pltpu.VMEM((2, D, 2 * D_FF), F32), pltpu.VMEM((2, D_FF, D), F32),
                        pltpu.VMEM((D, 2 * D_FF), BF16), pltpu.VMEM((D // LANES, D_FF, LANES), F32),
                        pltpu.VMEM((D_FF, D), BF16), pltpu.SemaphoreType.DMA((2,))])

    def kernel(*refs):
        _moe_kernel(*refs, layer=layer)

    return pl.pallas_call(
        kernel,
        grid_spec=grid_spec,
        out_shape=jax.ShapeDtypeStruct((MOE_TILES * MOE_TM * SUB, LANES), F32),
        compiler_params=_params("arbitrary"),
        name="moe_experts",
    )(plan["tile_expert"], plan["n_used"], plan["seg_parity"], plan["next_expert"], xs,
      b_gate_up.reshape(DEPTH, N_EXPERTS, 1, 2 * D_FF), b_down.reshape(DEPTH, N_EXPERTS, 1, D), w_gate_up, w_down)


def _combine_kernel(rs_ref, rl_ref, rsn_ref, rln_ref, x1_ref, pos_ref, w_ref, g2_ref, lng_ref, lnb_ref, ys_hbm,
                    *rest, final):
    out_refs, (stage, sem) = rest[:-2], rest[-2:]
    i = pl.program_id(0)
    slot = i % 2
    other = 1 - slot

    def fetch(start_ref, len_ref, s):
        _for_each_run(start_ref, len_ref, lambda l, g, size: pltpu.make_async_copy(
            _rows(ys_hbm, g, size), _rows(stage.at[s], l, size), sem.at[s]).start())

    def wait(s):
        pltpu.make_async_copy(stage.at[s], stage.at[s], sem.at[s]).wait()

    @pl.when(i == 0)
    def _():
        fetch(rs_ref, rl_ref, 0)

    fetch(rsn_ref, rln_ref, other)
    wait(slot)
    y = _from_tiles(stage.at[slot], PAIRS_TM).astype(BF16)
    place = pos_ref[...].astype(F32)
    w = w_ref[...]
    s = lax.broadcasted_iota(jnp.int32, (TM, PAIRS_TM), 1).astype(F32)
    pick = jnp.zeros((TM, PAIRS_TM), F32)
    for k in range(TOP_K):
        pick = jnp.where(s == place[:, k:k + 1], w[:, k:k + 1], pick)
    f = _dot(pick.astype(BF16), y)
    ci = _cond_row(i)
    z = DEEPNORM_ALPHA * x1_ref[...] + g2_ref[pl.ds(ci, 1), :] * f
    out = _layer_norm(z, lng_ref[...], lnb_ref[...])
    if final:
        @pl.when(i < NT_P)
        def _():
            out_refs[0][...] = out

        @pl.when(i >= NT_P)
        def _():
            out_refs[1][...] = out
    else:
        out_refs[0][...] = out

    @pl.when(i == NT - 1)
    def _():
        wait(other)


def _combine(layer, x1, ys, pos, w, plan, mod, g, b):
    final = layer == DEPTH - 1
    nxt = lambda i: (jnp.minimum(i + 1, NT - 1), 0, 0)
    cur = lambda i: (i, 0, 0)
    if final:
        out_specs = _x_specs()
        out_shape = [jax.ShapeDtypeStruct((T_P, D), F32), jax.ShapeDtypeStruct((T_S, D), F32)]
    else:
        out_specs = _row_spec(D)
        out_shape = jax.ShapeDtypeStruct((T, D), F32)

    def kernel(*refs):
        _combine_kernel(*refs, final=final)

    return pl.pallas_call(
        kernel,
        grid=(NT,),
        in_specs=[_run_spec(cur), _run_spec(cur), _run_spec(nxt), _run_spec(nxt),
                  _row_spec(D), _row_spec(LANES), _row_spec(LANES), _mod_spec(layer, 5), _full((1, D)), _full((1, D)),
                  pl.BlockSpec(memory_space=pl.ANY)],
        out_specs=out_specs,
        out_shape=out_shape,
        scratch_shapes=[pltpu.VMEM((2, PAIRS_TM * SUB, LANES), F32), pltpu.SemaphoreType.DMA((2,))],
        compiler_params=_params("arbitrary"),
        name="moe_combine_norm" + ("_final" if final else ""),
    )(plan["run_start"], plan["run_len"], plan["run_start"], plan["run_len"], x1, pos, w, mod, g, b, ys)


def _moe_layer(layer, routed, experts, mod, ln2_g, ln2_b):
    x1, u2, pos, w, cnt = routed
    plan = _moe_plan(cnt)
    ys = _moe(layer, _dispatch(u2, pos, plan), plan, *experts)
    return _combine(layer, x1, ys, pos, w, plan, mod, ln2_g[layer][None], ln2_b[layer][None])


def _router_tail(l, ln1_g, ln1_b, router_w, router_b):
    rw = jnp.pad(router_w[l], ((0, 0), (0, LANES - N_EXPERTS)))
    rb = jnp.pad(router_b[l], (0, LANES - N_EXPERTS), constant_values=NEG)
    return ln1_g[l][None], ln1_b[l][None], rw, rb[None]


def kernel(x_prompt, x_sample, c, c_ctx, cache_diff_k, cache_diff_v, cache_na_k, cache_na_v, cache_gqa_k, cache_gqa_v, w_mod, b_mod, ln1_g, ln1_b, ln2_g, ln2_b, ab_w_in, ab_conv_w, ab_lambda_q1, ab_lambda_k1, ab_lambda_q2, ab_lambda_k2, ab_subln_g, ab_w_out, cd_w_in, cd_na_rpb, cd_q_norm_g, cd_k_norm_g, cd_w_out, router_w, router_b, w_gate_up, b_gate_up, w_down, b_down):
    xp = x_prompt.reshape(T_P, D)
    xs = x_sample.reshape(T_S, D)
    cond8 = jnp.concatenate([c_ctx[None], c, jnp.zeros((8 - 1 - B_S, D), F32)], axis=0)
    mod = _modulation(cond8, w_mod, b_mod)
    rope = _rope_tables()
    experts = (w_gate_up, b_gate_up, w_down, b_down)

    gb, y, q, k, v, new_diff_k, new_diff_v = _ab_in(xp, xs, mod, ab_w_in[0].astype(BF16), rope)
    lam_init = 0.8 - 0.6 * 1.0
    diff = (jnp.stack([ab_lambda_q1[0], ab_lambda_k1[0], ab_lambda_q2[0], ab_lambda_k2[0]]), ab_subln_g[0][None])
    attn = _flash_pair(q, k, v, latent=False, out_cols=512, col0=0, post="diff", diff=diff, lam_init=lam_init)
    attn = _flash_pair(q, k, v, latent=True, out_cols=512, col0=0, post="diff", diff=diff, lam_init=lam_init,
                       ctx=(cache_diff_k, cache_diff_v), ctx_mode="wide", prev=attn)
    routed = _ab_out(xp, xs, gb, y, ab_conv_w[0], attn, ab_w_out[0].astype(BF16), mod,
                     _router_tail(0, ln1_g, ln1_b, router_w, router_b))
    x = _moe_layer(0, routed, experts, mod, ln2_g, ln2_b)

    qg = jnp.tile(cd_q_norm_g[0], 2)[None]
    kg = jnp.tile(cd_k_norm_g[0], 2)[None]
    nq, nk, nv, gq, gk, gv, new_na_k, new_na_v, new_gqa_k, new_gqa_v = _cd_in(
        x, mod, cd_w_in[0].astype(BF16), qg, kg, rope)
    merged = _flash_pair(nq, nk, nv, latent=False, out_cols=D, col0=0, post="select")
    merged = _flash_pair(gq, gk, gv, latent=False, out_cols=D, col0=2, post="select", prev=merged)
    merged = _na_latent(nq, nk, nv, cache_na_k, cache_na_v, _na_bias(cd_na_rpb[0]), merged)
    merged = _flash_pair(gq, gk, gv, latent=True, out_cols=D, col0=2, post="select",
                         ctx=(cache_gqa_k, cache_gqa_v), ctx_mode="dup", prev=merged)
    routed = _cd_out(x, merged, cd_w_out[0].astype(BF16), mod,
                     _router_tail(1, ln1_g, ln1_b, router_w, router_b))
    y_p, y_s = _moe_layer(1, routed, experts, mod, ln2_g, ln2_b)

    return (y_p.reshape(B_P, N_P, D), y_s.reshape(B_S, N_S, D), new_diff_k, new_diff_v,
            new_na_k, new_na_v, new_gqa_k, new_gqa_v)
```

```python
import jax
import jax.numpy as jnp
import numpy as np
from jax import lax
from jax.experimental import pallas as pl
from jax.experimental.pallas import tpu as pltpu

F32 = jnp.float32
BF16 = jnp.bfloat16

D = 1024
B_P, N_P = 16, 256
B_S, N_S = 2, 4096
PAST = 256
T_P, T_S = B_P * N_P, B_S * N_S
T = T_P + T_S
TM = 256
NT_P, NT_S, NT = T_P // TM, T_S // TM, T // TM
TILES_PER_GRID = N_S // TM
GRID_W = 64
GRID_H = N_S // GRID_W
HD = 64
DEPTH = 2
N_EXPERTS = 32
TOP_K = 4
D_FF = 1024
NA_WIN_R, NA_WIN_C = 8, 16
NA_QROWS = 4
NA_KROWS = 12
SWIGLU_LIMIT = 7.0
SWIGLU_ALPHA = 1.702
ROPE_THETA = 10000.0
DEEPNORM_ALPHA = (2 * DEPTH) ** 0.25
LN_EPS = 1e-5
RMS_EPS = 1e-6
LOG2E = 1.4426950408889634
QK_SCALE = HD ** -0.5 * LOG2E
NEG = -1e30
MOE_TM = 256
MOE_PAIRS = TOP_K * T
MOE_TILES = MOE_PAIRS // MOE_TM + N_EXPERTS
PAIRS_TM = TOP_K * TM
RUN_SIZES = tuple(TM >> b for b in range(TM.bit_length()))
LANES = 128
SUB = 8

VMEM_LIMIT = 56 * 1024 * 1024


def _params(*sem):
    return pltpu.CompilerParams(dimension_semantics=sem, vmem_limit_bytes=VMEM_LIMIT)


def _split(x):
    hi = x.astype(BF16)
    lo = (x - hi.astype(F32)).astype(BF16)
    return hi, lo


def _dot(a, b):
    return jnp.dot(a, b, preferred_element_type=F32)


def _dot3(a, b):
    ah, al = _split(a)
    bh, bl = _split(b)
    return _dot(ah, bh) + (_dot(ah, bl) + _dot(al, bh))


def _cond_row(i):
    return jnp.where(i < NT_P, 0, 1 + (i - NT_P) // TILES_PER_GRID)


def _layer_norm(z, g, b):
    mu = jnp.mean(z, axis=-1, keepdims=True)
    zc = z - mu
    var = jnp.mean(zc * zc, axis=-1, keepdims=True)
    return zc * lax.rsqrt(var + LN_EPS) * g + b


def _low_half(rows):
    return lax.broadcasted_iota(jnp.int32, (rows, LANES), 1) < HD


def _mod_kernel(c_ref, w_ref, b_ref, o_ref):
    c = c_ref[...]
    o_ref[...] = _dot3(c * jax.nn.sigmoid(c), w_ref[...]) + b_ref[...]


def _modulation(cond8, w_mod, b_mod):
    return pl.pallas_call(
        _mod_kernel,
        grid=(DEPTH, 6),
        in_specs=[pl.BlockSpec((8, D), lambda l, j: (0, 0)),
                  pl.BlockSpec((None, D, D), lambda l, j: (l, 0, j)),
                  pl.BlockSpec((None, 1, D), lambda l, j: (l, 0, j))],
        out_specs=pl.BlockSpec((None, 8, D), lambda l, j: (l, 0, j)),
        out_shape=jax.ShapeDtypeStruct((DEPTH, 8, 6 * D), F32),
        compiler_params=_params("arbitrary", "arbitrary"),
        name="modulation",
    )(cond8, w_mod, b_mod.reshape(DEPTH, 1, 6 * D))


def _mod_spec(layer, chunk):
    return pl.BlockSpec((None, 8, D), lambda i, _l=layer, _c=chunk: (_l, 0, _c))


def _full(shape):
    return pl.BlockSpec(shape, lambda i: (0,) * len(shape))


def _rope_tables():
    t = np.arange(N_S)
    half = HD // 2
    inv = ROPE_THETA ** (-np.arange(0, half, 2, dtype=np.float64) / half)
    inv_lane = np.tile(np.repeat(inv, 2), 2 * LANES // HD)
    lane = np.arange(LANES)
    by_row = (lane % HD) < half
    ang = np.where(by_row[None], (t // GRID_W)[:, None], (t % GRID_W)[:, None]) * inv_lane[None]
    cos, sin = np.cos(ang), np.sin(ang)
    even = (lane % 2) == 0
    return tuple(jnp.asarray(a, F32) for a in (cos, np.where(even, -sin, 0.0), np.where(even, 0.0, sin)))


def _rope(x, a, b, c):
    return x * a + pltpu.roll(x, LANES - 1, axis=1) * b + pltpu.roll(x, 1, axis=1) * c


def _rope_or_identity(identity, ra_ref, rb_ref, rc_ref):
    return (jnp.where(identity, 1.0, ra_ref[...]), jnp.where(identity, 0.0, rb_ref[...]),
            jnp.where(identity, 0.0, rc_ref[...]))


def _rope_spec():
    return pl.BlockSpec((TM, LANES), lambda i: (jnp.maximum(i - NT_P, 0) % TILES_PER_GRID, 0))


def _x_specs():
    return [pl.BlockSpec((TM, D), lambda i: (jnp.minimum(i, NT_P - 1), 0)),
            pl.BlockSpec((TM, D), lambda i: (jnp.maximum(i - NT_P, 0), 0))]


def _cache_spec(heads, width):
    return pl.BlockSpec((None, None, heads, N_P, width), lambda i: (jnp.minimum(i, NT_P - 1), 0, 0, 0, 0))


def _row_spec(width):
    return pl.BlockSpec((TM, width), lambda i: (i, 0))


def _hm_spec(n):
    return pl.BlockSpec((n, TM, LANES), lambda i: (0, i, 0))


AB_Q0, AB_K0, AB_V0 = 1536, 2048, 2560


def _tile(p, col0, j):
    return p[:, col0 + j * LANES:col0 + (j + 1) * LANES]


def _ab_in_kernel(xp_ref, xs_ref, sh_ref, sc_ref, w_ref, ra_ref, rb_ref, rc_ref,
                  gb_ref, y_ref, q_ref, k_ref, v_ref, kc_ref, vc_ref):
    i = pl.program_id(0)
    is_p = i < NT_P
    ci = _cond_row(i)
    x = jnp.where(is_p, xp_ref[...], xs_ref[...])
    u = x * (1.0 + sc_ref[pl.ds(ci, 1), :]) + sh_ref[pl.ds(ci, 1), :]
    p = _dot(u.astype(BF16), w_ref[...])
    gb_ref[...] = p[:, 0:512]
    y_ref[...] = p[:, 512:1024] * p[:, 1024:1536]
    a, b, c = _rope_or_identity(is_p, ra_ref, rb_ref, rc_ref)
    for h in range(4):
        v_ref[h] = _tile(p, AB_V0, h).astype(BF16)
        q_ref[h] = (_rope(_tile(p, AB_Q0, h), a, b, c) * QK_SCALE).astype(BF16)
        k_ref[h] = _rope(_tile(p, AB_K0, h), a, b, c).astype(BF16)

    @pl.when(is_p)
    def _():
        for h in range(4):
            kc_ref[h] = _tile(p, AB_K0, h)
            vc_ref[h] = _tile(p, AB_V0, h)


def _ab_in(xp, xs, mod, w_in, rope):
    hm = jax.ShapeDtypeStruct((4, T, LANES), BF16)
    cache = jax.ShapeDtypeStruct((B_P, 1, 4, N_P, LANES), F32)
    half = jax.ShapeDtypeStruct((T, 512), F32)
    return pl.pallas_call(
        _ab_in_kernel,
        grid=(NT,),
        in_specs=_x_specs() + [_mod_spec(0, 0), _mod_spec(0, 1), _full((D, 3072)),
                               _rope_spec(), _rope_spec(), _rope_spec()],
        out_specs=[_row_spec(512), _row_spec(512), _hm_spec(4), _hm_spec(4), _hm_spec(4),
                   _cache_spec(4, LANES), _cache_spec(4, LANES)],
        out_shape=[half, half, hm, hm, hm, cache, cache],
        compiler_params=_params("arbitrary"),
        name="ab_in_proj",
    )(xp, xs, mod, mod, w_in, *rope)


CD_NQ, CD_NK, CD_NV, CD_GQ, CD_GK, CD_GV = 0, 512, 1024, 1536, 2048, 2176


def _seg_mean64(s):
    r = lax.broadcasted_iota(jnp.int32, (LANES, LANES), 0) // HD
    c = lax.broadcasted_iota(jnp.int32, (LANES, LANES), 1) // HD
    seg = jnp.where(r == c, 1.0, 0.0).astype(BF16)
    hi, lo = _split(s)
    return (_dot(hi, seg) + _dot(lo, seg)) * (1.0 / HD)


def _rms64(x, g):
    return x * lax.rsqrt(_seg_mean64(x * x) + RMS_EPS) * g


def _dup_halves(x, lo):
    r = pltpu.roll(x, HD, axis=1)
    return jnp.where(lo, x, r), jnp.where(lo, r, x)


def _cd_in_kernel(x_ref, sh_ref, sc_ref, w_ref, qg_ref, kg_ref, ra_ref, rb_ref, rc_ref,
                  nq_ref, nk_ref, nv_ref, gq_ref, gk_ref, gv_ref,
                  nkc_ref, nvc_ref, gkc_ref, gvc_ref):
    i = pl.program_id(0)
    is_p = i < NT_P
    ci = _cond_row(i)
    u = x_ref[...] * (1.0 + sc_ref[pl.ds(ci, 1), :]) + sh_ref[pl.ds(ci, 1), :]
    p = _dot(u.astype(BF16), w_ref[...])
    lo = _low_half(TM)
    for j in range(4):
        nq_ref[j] = (_tile(p, CD_NQ, j) * QK_SCALE).astype(BF16)
        nk_ref[j] = _tile(p, CD_NK, j).astype(BF16)
        nv_ref[j] = _tile(p, CD_NV, j).astype(BF16)
    gq = [_rms64(_tile(p, CD_GQ, j), qg_ref[...]) for j in range(4)]
    gk = _rms64(_tile(p, CD_GK, 0), kg_ref[...])
    gv = _tile(p, CD_GV, 0)
    v0, v1 = _dup_halves(gv, lo)
    gv_ref[0] = v0.astype(BF16)
    gv_ref[1] = v1.astype(BF16)
    a, b, c = _rope_or_identity(is_p, ra_ref, rb_ref, rc_ref)
    for j in range(4):
        gq_ref[j] = (_rope(gq[j], a, b, c) * QK_SCALE).astype(BF16)
    k0, k1 = _dup_halves(_rope(gk, a, b, c), lo)
    gk_ref[0] = k0.astype(BF16)
    gk_ref[1] = k1.astype(BF16)

    @pl.when(is_p)
    def _():
        gkc_ref[0] = k0[:, 0:HD]
        gkc_ref[1] = k1[:, 0:HD]
        gvc_ref[0] = v0[:, 0:HD]
        gvc_ref[1] = v1[:, 0:HD]
        for j in range(4):
            for src, dst in ((CD_NK, nkc_ref), (CD_NV, nvc_ref)):
                a, b = _dup_halves(_tile(p, src, j), lo)
                dst[2 * j] = a[:, 0:HD]
                dst[2 * j + 1] = b[:, 0:HD]


def _cd_in(x, mod, w_in, qg, kg, rope):
    hm4 = jax.ShapeDtypeStruct((4, T, LANES), BF16)
    hm2 = jax.ShapeDtypeStruct((2, T, LANES), BF16)
    c8 = jax.ShapeDtypeStruct((B_P, 1, 8, N_P, HD), F32)
    c2 = jax.ShapeDtypeStruct((B_P, 1, 2, N_P, HD), F32)
    return pl.pallas_call(
        _cd_in_kernel,
        grid=(NT,),
        in_specs=[_row_spec(D), _mod_spec(1, 0), _mod_spec(1, 1), _full((D, 2304)),
                  _full((1, LANES)), _full((1, LANES)), _rope_spec(), _rope_spec(), _rope_spec()],
        out_specs=[_hm_spec(4), _hm_spec(4), _hm_spec(4), _hm_spec(4), _hm_spec(2), _hm_spec(2),
                   _cache_spec(8, HD), _cache_spec(8, HD), _cache_spec(2, HD), _cache_spec(2, HD)],
        out_shape=[hm4, hm4, hm4, hm4, hm2, hm2, c8, c8, c2, c2],
        compiler_params=_params("arbitrary"),
        name="cd_in_proj",
    )(x, mod, mod, w_in, qg, kg, *rope)


def _stack_pairs(q_ref, n_q, tq):
    lo = _low_half(tq)
    parts = []
    for j in range(n_q):
        q = q_ref[j]
        zero = jnp.zeros_like(q)
        parts += [jnp.where(lo, q, zero), jnp.where(lo, zero, q)]
    return jnp.concatenate(parts, axis=0), lo


def _qk(qs, kb):
    return lax.dot_general(qs, kb, (((1,), (1,)), ((), ())), preferred_element_type=F32)


def _ctx_tile(ref, mode):
    if mode == "wide":
        x = ref[...]
    elif mode == "pair":
        x = jnp.concatenate([ref[0], ref[1]], axis=1)
    else:
        x = jnp.concatenate([ref[...], ref[...]], axis=1)
    return x.astype(BF16)


def _with_ones(v):
    lane = lax.broadcasted_iota(jnp.int32, v.shape, 1)
    return jnp.concatenate([v, jnp.where(lane == 0, 1.0, 0.0).astype(v.dtype)], axis=1)


def _flash_pair_kernel(*refs, gs, **kw):
    it = iter(refs)
    q_ref, k_ref, v_ref = next(it), next(it), next(it)
    rest = list(it)
    o_ref = rest.pop()
    n_q = q_ref.shape[0] // gs
    for gi in range(gs):
        _flash_group(q_ref.at[gi * n_q:(gi + 1) * n_q], k_ref.at[gi], v_ref.at[gi], *rest,
                     o_ref.at[:, gi * n_q * LANES:(gi + 1) * n_q * LANES], n_q=n_q, **kw)


def _flash_group(*refs, n_q, tq, nk, tk, ctx_mode, post, lam_init):
    it = iter(refs)
    q_ref, k_ref, v_ref = next(it), next(it), next(it)
    kc_ref, vc_ref = (next(it), next(it)) if ctx_mode else (None, None)
    lam_ref, g_ref = (next(it), next(it)) if post == "diff" else (None, None)
    o_ref = next(it)

    qs, lo = _stack_pairs(q_ref, n_q, tq)
    rows = 2 * n_q * tq

    def step(kb, vb, carry):
        m, acc = carry
        s = _qk(qs, kb)
        m_new = jnp.maximum(m, jnp.max(s, axis=1, keepdims=True))
        p = jnp.exp2((s - m_new).astype(BF16))
        return m_new, jnp.exp2(m - m_new) * acc + _dot(p, _with_ones(vb))

    carry = (jnp.full((rows, 1), NEG, F32), jnp.zeros((rows, 2 * LANES), F32))
    for c in range(nk // tk):
        carry = step(k_ref[c * tk:(c + 1) * tk, :], v_ref[c * tk:(c + 1) * tk, :], carry)
    if ctx_mode:
        carry = step(_ctx_tile(kc_ref, ctx_mode), _ctx_tile(vc_ref, ctx_mode), carry)
    _, acc = carry
    o = acc[:, 0:LANES] / acc[:, LANES:LANES + 1]

    if post == "diff":
        lp = lam_ref[...]
        lam = (jnp.exp(jnp.sum(lp[0:1] * lp[1:2], axis=1, keepdims=True))
               - jnp.exp(jnp.sum(lp[2:3] * lp[3:4], axis=1, keepdims=True)) + lam_init)
        a = o[0:tq] - lam * o[tq:2 * tq]
        ms = jnp.mean(a * a, axis=-1, keepdims=True)
        o_ref[...] = (a * lax.rsqrt(ms + RMS_EPS) * g_ref[...] * (1.0 - lam_init)).astype(o_ref.dtype)
    else:
        for j in range(n_q):
            o_ref[:, j * LANES:(j + 1) * LANES] = jnp.where(
                lo, o[2 * j * tq:(2 * j + 1) * tq], o[(2 * j + 1) * tq:(2 * j + 2) * tq]).astype(o_ref.dtype)


def _flash_pair(q, k, v, *, latent, out_cols, col0, post, ctx=None, ctx_mode=None, diff=None,
                lam_init=0.0, prev=None):
    groups = k.shape[0]
    n_q = q.shape[0] // groups
    if latent:
        tq, nk, tk, gs = TM, N_S, 1024, 1
        grid = (B_S, groups, TILES_PER_GRID)
        qrow = lambda b, g, i: NT_P + b * TILES_PER_GRID + i
        krow = lambda b, g, i: T_P // N_S + b
    else:
        tq, nk, tk, gs = N_P, N_P, N_P, groups
        grid = (B_P, 1, 1)
        qrow = lambda b, g, i: b
        krow = lambda b, g, i: b
    in_specs = [pl.BlockSpec((gs * n_q, tq, LANES), lambda b, g, i: (g, qrow(b, g, i), 0)),
                pl.BlockSpec((gs, nk, LANES), lambda b, g, i: (g, krow(b, g, i), 0)),
                pl.BlockSpec((gs, nk, LANES), lambda b, g, i: (g, krow(b, g, i), 0))]
    args = [q, k, v]
    if ctx is not None:
        if ctx_mode == "wide":
            spec = pl.BlockSpec((None, None, None, PAST, LANES), lambda b, g, i: (b, 0, g, 0, 0))
        elif ctx_mode == "pair":
            spec = pl.BlockSpec((None, None, 2, PAST, HD), lambda b, g, i: (b, 0, g, 0, 0))
        else:
            spec = pl.BlockSpec((None, None, None, PAST, HD), lambda b, g, i: (b, 0, g, 0, 0))
        in_specs += [spec, spec]
        args += list(ctx)
    if diff is not None:
        in_specs += [pl.BlockSpec((4, HD), lambda b, g, i: (0, 0)),
                     pl.BlockSpec((1, LANES), lambda b, g, i: (0, 0))]
        args += list(diff)
    aliases = {}
    if prev is not None:
        aliases = {len(args): 0}
        in_specs.append(pl.BlockSpec(memory_space=pl.ANY))
        args.append(prev)

    def kernel(*refs):
        if prev is not None:
            refs = refs[:-2] + refs[-1:]
        _flash_pair_kernel(*refs, gs=gs, tq=tq, nk=nk, tk=tk, ctx_mode=ctx_mode if ctx is not None else None,
                           post=post, lam_init=lam_init)

    return pl.pallas_call(
        kernel,
        grid=grid,
        in_specs=in_specs,
        out_specs=pl.BlockSpec((tq, gs * n_q * LANES), lambda b, g, i: (qrow(b, g, i), col0 // gs + g)),
        out_shape=jax.ShapeDtypeStruct((T, out_cols), BF16),
        input_output_aliases=aliases,
        compiler_params=_params("arbitrary", "arbitrary", "arbitrary"),
        name="attn_" + post + ("_latent" if latent else "_context") + str(n_q),
    )(*args)


NA_TQ = NA_QROWS * GRID_W
NA_TK = NA_KROWS * GRID_W


NA_BLOCK_POS = ((0, 0), (NA_QROWS, 0), (GRID_H - NA_QROWS, GRID_H - NA_KROWS))
N_DR = 2 * NA_WIN_R - 1
N_DC = 2 * NA_WIN_C - 1


def _na_bias_kernel(rpb_ref, o_ref):
    qc = lax.broadcasted_iota(jnp.int32, (GRID_W, LANES), 0)
    lane = lax.broadcasted_iota(jnp.int32, (GRID_W, LANES), 1)
    kc = lane % GRID_W
    cs = jnp.clip(qc - NA_WIN_C // 2, 0, GRID_W - NA_WIN_C)
    col_ok = jnp.logical_and(kc >= cs, kc < cs + NA_WIN_C)
    lo = lane < GRID_W
    neg = jnp.full((GRID_W, LANES), NEG, F32)
    for head in range(2):
        toeplitz = []
        for dr in range(N_DR):
            r = jnp.broadcast_to(rpb_ref[head, dr:dr + 1, :] * LOG2E, (GRID_W, LANES))
            t = jnp.where(lo, pltpu.roll(r, LANES - (NA_WIN_C - 1), axis=1, stride=1, stride_axis=0),
                          pltpu.roll(r, GRID_W - (NA_WIN_C - 1), axis=1, stride=1, stride_axis=0))
            toeplitz.append(jnp.where(col_ok, t, neg))
        for pos, (r0, k0) in enumerate(NA_BLOCK_POS):
            for i in range(NA_QROWS):
                qr = r0 + i
                rs = min(max(qr - NA_WIN_R // 2, 0), GRID_H - NA_WIN_R)
                tiles = [toeplitz[k0 + j - qr + NA_WIN_R - 1] if rs <= k0 + j < rs + NA_WIN_R else neg
                         for j in range(NA_KROWS)]
                for jp in range(NA_KROWS // 2):
                    o_ref[pos, pl.ds(head * NA_TQ + i * GRID_W, GRID_W), pl.ds(jp * LANES, LANES)] = jnp.where(
                        lo, tiles[2 * jp], tiles[2 * jp + 1])


def _na_bias(rpb):
    rpb_pad = jnp.pad(rpb, ((0, 0), (0, 16 - N_DR), (0, LANES - N_DC)), constant_values=NEG)
    return pl.pallas_call(
        _na_bias_kernel,
        grid=(4,),
        in_specs=[pl.BlockSpec((2, 16, LANES), lambda g: (g, 0, 0))],
        out_specs=pl.BlockSpec((3, None, 2 * NA_TQ, NA_TK), lambda g: (0, g, 0, 0)),
        out_shape=jax.ShapeDtypeStruct((3, 4, 2 * NA_TQ, NA_TK), F32),
        compiler_params=_params("arbitrary"),
        name="na_bias_table",
    )(rpb_pad)


def _na_kernel(q_ref, k_ref, v_ref, kc_ref, vc_ref, bm_ref, _, o_ref):
    i = pl.program_id(2)
    qs, lo = _stack_pairs(q_ref, 1, NA_TQ)
    k0 = jnp.clip(i * NA_QROWS - NA_WIN_R // 2, 0, GRID_H - NA_KROWS)
    start = pl.multiple_of(k0 * GRID_W, GRID_W)
    kw = k_ref[pl.ds(start, NA_TK), :]
    vw = v_ref[pl.ds(start, NA_TK), :]
    s_w = _qk(qs, kw) + bm_ref[...]
    s_c = _qk(qs, _ctx_tile(kc_ref, "pair"))
    m = jnp.maximum(jnp.max(s_w, axis=1, keepdims=True), jnp.max(s_c, axis=1, keepdims=True))
    p_w = jnp.exp2((s_w - m).astype(BF16))
    p_c = jnp.exp2((s_c - m).astype(BF16))
    acc = _dot(p_w, _with_ones(vw)) + _dot(p_c, _with_ones(_ctx_tile(vc_ref, "pair")))
    o = acc[:, 0:LANES] / acc[:, LANES:LANES + 1]
    o_ref[...] = jnp.where(lo, o[0:NA_TQ], o[NA_TQ:2 * NA_TQ]).astype(o_ref.dtype)


def _na_latent(q, k, v, kc, vc, bm, prev):
    nblk = N_S // NA_TQ
    qrow = lambda b, g, i: T_P // NA_TQ + b * nblk + i
    krow = lambda b, g, i: T_P // N_S + b
    cfg = lambda i: jnp.where(i == 0, 0, jnp.where(i == nblk - 1, 2, 1))
    ctx_spec = pl.BlockSpec((None, None, 2, PAST, HD), lambda b, g, i: (b, 0, g, 0, 0))
    return pl.pallas_call(
        _na_kernel,
        grid=(B_S, 4, nblk),
        in_specs=[pl.BlockSpec((1, NA_TQ, LANES), lambda b, g, i: (g, qrow(b, g, i), 0)),
                  pl.BlockSpec((None, N_S, LANES), lambda b, g, i: (g, krow(b, g, i), 0)),
                  pl.BlockSpec((None, N_S, LANES), lambda b, g, i: (g, krow(b, g, i), 0)),
                  ctx_spec, ctx_spec,
                  pl.BlockSpec((None, None, 2 * NA_TQ, NA_TK), lambda b, g, i: (cfg(i), g, 0, 0)),
                  pl.BlockSpec(memory_space=pl.ANY)],
        out_specs=pl.BlockSpec((NA_TQ, LANES), lambda b, g, i: (qrow(b, g, i), g)),
        out_shape=jax.ShapeDtypeStruct((T, D), BF16),
        input_output_aliases={6: 0},
        compiler_params=_params("arbitrary", "arbitrary", "arbitrary"),
        name="attn_window_latent",
    )(q, k, v, kc, vc, bm, prev)


def _top4(logits):
    lane = lax.broadcasted_iota(jnp.int32, logits.shape, 1).astype(F32)
    rest = logits
    tops, firsts = [], []
    for _ in range(TOP_K):
        m = jnp.max(rest, axis=1, keepdims=True)
        first = jnp.min(jnp.where(rest == m, lane, float(LANES)), axis=1, keepdims=True)
        tops.append(m)
        firsts.append(first)
        rest = jnp.where(lane == first, -jnp.inf, rest)
    es = [jnp.exp(m - tops[0]) for m in tops]
    denom = es[0] + es[1] + es[2] + es[3]
    w = jnp.zeros_like(logits)
    for k in range(TOP_K):
        w = jnp.where(lane == float(k), es[k] / denom, w)
    return firsts, w


def _local_sort(firsts):
    lane = lax.broadcasted_iota(jnp.int32, (TM, LANES), 1).astype(F32)
    hots = [lane == f for f in firsts]
    sel = jnp.zeros((TM, LANES), F32)
    for hot in hots:
        sel = jnp.where(hot, 1.0, sel)
    r = lax.broadcasted_iota(jnp.int32, (TM, TM), 0)
    c = lax.broadcasted_iota(jnp.int32, (TM, TM), 1)
    earlier = _dot(jnp.where(c < r, 1.0, 0.0).astype(BF16), sel.astype(BF16))
    cnt = jnp.sum(sel, axis=0, keepdims=True)
    r = lax.broadcasted_iota(jnp.int32, (LANES, LANES), 0)
    c = lax.broadcasted_iota(jnp.int32, (LANES, LANES), 1)
    start = _dot(jnp.broadcast_to(cnt, (SUB, LANES)).astype(BF16), jnp.where(r < c, 1.0, 0.0).astype(BF16))[0:1]
    place = start + earlier
    pos = jnp.zeros((TM, LANES), F32)
    for k, hot in enumerate(hots):
        pos = jnp.where(lane == float(k), jnp.sum(jnp.where(hot, place, 0.0), axis=1, keepdims=True), pos)
    return pos.astype(jnp.int32), cnt.astype(jnp.int32)


def _mixer_tail(x_of, h_of, ci, g1_ref, sh2_ref, sc2_ref, lng_ref, lnb_ref, rw_ref, rb_ref,
                x1_ref, u2_ref, pos_ref, w_ref, cnt_ref):
    rows = slice(0, TM)
    x1 = _layer_norm(DEEPNORM_ALPHA * x_of(rows) + g1_ref[pl.ds(ci, 1), :] * h_of(rows), lng_ref[...], lnb_ref[...])
    x1_ref[...] = x1
    u2 = x1 * (1.0 + sc2_ref[pl.ds(ci, 1), :]) + sh2_ref[pl.ds(ci, 1), :]
    u2_ref[...] = u2.astype(BF16)
    firsts, w_ref[...] = _top4(_dot3(u2, rw_ref[...]) + rb_ref[...])
    pos, cnt = _local_sort(firsts)
    pos_ref[...] = pos
    cnt_ref[...] = jnp.broadcast_to(cnt, (SUB, LANES))


def _ab_out_kernel(xp_ref, xs_ref, gb_ref, y_ref, yp_ref, yn_ref, cv_ref, at_ref, wo_ref,
                   g1_ref, sh2_ref, sc2_ref, lng_ref, lnb_ref, rw_ref, rb_ref, *out_refs):
    i = pl.program_id(0)
    is_p = i < NT_P
    ci = _cond_row(i)
    j = (i - NT_P) % TILES_PER_GRID
    first = jnp.logical_or(is_p, j == 0)
    last = jnp.logical_or(is_p, j == TILES_PER_GRID - 1)
    y = y_ref[...]
    row = lax.broadcasted_iota(jnp.int32, y.shape, 0)
    before = jnp.where(first, 0.0, yp_ref[7:8, :])
    after = jnp.where(last, 0.0, yn_ref[0:1, :])
    y_prev = jnp.where(row == 0, before, pltpu.roll(y, 1, axis=0))
    y_next = jnp.where(row == TM - 1, after, pltpu.roll(y, TM - 1, axis=0))
    cv = cv_ref[...]
    conv = (gb_ref[...] * (y_prev * cv[0:1] + y * cv[1:2] + y_next * cv[2:3])).astype(BF16)
    _mixer_tail(lambda r: jnp.where(is_p, xp_ref[r, :], xs_ref[r, :]),
                lambda r: _dot(conv[r], wo_ref[0:512, :]) + _dot(at_ref[r, :], wo_ref[512:1024, :]),
                ci, g1_ref, sh2_ref, sc2_ref, lng_ref, lnb_ref, rw_ref, rb_ref, *out_refs)


def _cd_out_kernel(x_ref, mg_ref, wo_ref, g1_ref, sh2_ref, sc2_ref, lng_ref, lnb_ref, rw_ref, rb_ref,
                   *out_refs):
    ci = _cond_row(pl.program_id(0))
    _mixer_tail(lambda r: x_ref[r, :], lambda r: _dot(mg_ref[r, :], wo_ref[...]),
                ci, g1_ref, sh2_ref, sc2_ref, lng_ref, lnb_ref, rw_ref, rb_ref, *out_refs)


def _tail_specs(layer):
    return [_mod_spec(layer, 2), _mod_spec(layer, 3), _mod_spec(layer, 4),
            _full((1, D)), _full((1, D)), _full((D, LANES)), _full((1, LANES))]


_TAIL_OUT_SPECS = [_row_spec(D), _row_spec(D), _row_spec(LANES), _row_spec(LANES),
                   pl.BlockSpec((None, SUB, LANES), lambda i: (i, 0, 0))]
_TAIL_OUT_SHAPES = [jax.ShapeDtypeStruct((T, D), F32), jax.ShapeDtypeStruct((T, D), BF16),
                    jax.ShapeDtypeStruct((T, LANES), jnp.int32), jax.ShapeDtypeStruct((T, LANES), F32),
                    jax.ShapeDtypeStruct((NT, SUB, LANES), jnp.int32)]


def _ab_out(xp, xs, gb, y, conv_w, attn, w_out, mod, tail):
    halo_prev = pl.BlockSpec((8, 512), lambda i: (jnp.maximum(i * (TM // 8) - 1, 0), 0))
    halo_next = pl.BlockSpec((8, 512), lambda i: (jnp.minimum((i + 1) * (TM // 8), T // 8 - 1), 0))
    return pl.pallas_call(
        _ab_out_kernel,
        grid=(NT,),
        in_specs=_x_specs() + [_row_spec(512), _row_spec(512), halo_prev, halo_next, _full((3, 512)),
                               _row_spec(512), _full((D, D))] + _tail_specs(0),
        out_specs=_TAIL_OUT_SPECS,
        out_shape=_TAIL_OUT_SHAPES,
        compiler_params=_params("arbitrary"),
        name="ab_out_proj",
    )(xp, xs, gb, y, y, y, conv_w, attn, w_out, mod, mod, mod, *tail)


def _cd_out(x, merged, w_out, mod, tail):
    return pl.pallas_call(
        _cd_out_kernel,
        grid=(NT,),
        in_specs=[_row_spec(D), _row_spec(D), _full((D, D))] + _tail_specs(1),
        out_specs=_TAIL_OUT_SPECS,
        out_shape=_TAIL_OUT_SHAPES,
        compiler_params=_params("arbitrary"),
        name="cd_out_proj",
    )(x, merged, w_out, mod, mod, mod, *tail)


def _mix_down(wd_ref, wmix_ref):
    half = D_FF // 2
    for c in range(D // LANES):
        wmix_ref[c, pl.ds(0, half, stride=2), :] = wd_ref[0:half, c * LANES:(c + 1) * LANES]
        wmix_ref[c, pl.ds(1, half, stride=2), :] = wd_ref[half:D_FF, c * LANES:(c + 1) * LANES]


def _prep_expert(wgu_ref, wd_ref, wgu_bf, wmix_ref, wd_bf):
    for c in range(4):
        wgu_bf[:, c * 512:(c + 1) * 512] = wgu_ref[:, c * 512:(c + 1) * 512].astype(BF16)
    _mix_down(wd_ref, wmix_ref)
    for c in range(D // LANES):
        wd_bf[:, c * LANES:(c + 1) * LANES] = wmix_ref[c].astype(BF16)


def _expert_ffn(u, wgu_bf, bgu_ref, wd_bf, bd_ref):
    rows = u.shape[0]
    ga = _dot(u, wgu_bf[:, 0:D_FF]) + bgu_ref[:, 0:D_FF]
    gb = _dot(u, wgu_bf[:, D_FF:2 * D_FF]) + bgu_ref[:, D_FF:2 * D_FF]
    even = (lax.broadcasted_iota(jnp.int32, (rows, LANES), 1) % 2) == 0
    hid = []
    for c in range(D_FF // LANES):
        a = ga[:, c * LANES:(c + 1) * LANES]
        b = gb[:, c * LANES:(c + 1) * LANES]
        gate = jnp.where(even, a, pltpu.roll(b, 1, axis=1))
        up = jnp.where(even, pltpu.roll(a, LANES - 1, axis=1), b)
        gate = jnp.minimum(gate, SWIGLU_LIMIT)
        up = jnp.clip(up, -SWIGLU_LIMIT, SWIGLU_LIMIT)
        hid.append(((up + 1.0) * gate * jax.nn.sigmoid(SWIGLU_ALPHA * gate)).astype(BF16))
    hid = jnp.concatenate(hid, axis=1)
    return _dot(hid, wd_bf[...]) + bd_ref[...]


def _moe_plan(cnt):
    c = cnt[:, 0, :N_EXPERTS]
    counts = jnp.sum(c, axis=0)
    tiles_e = (counts + MOE_TM - 1) // MOE_TM
    tile_end = jnp.cumsum(tiles_e)
    first_row = (tile_end - tiles_e) * MOE_TM
    run_start = first_row[None] + jnp.cumsum(c, axis=0) - c
    j = jnp.arange(MOE_TILES, dtype=jnp.int32)
    tile_expert = jnp.minimum(jnp.sum((j[:, None] >= tile_end[None]).astype(jnp.int32), axis=1), N_EXPERTS - 1)
    i32 = lambda a: a.astype(jnp.int32)
    experts = jnp.arange(N_EXPERTS, dtype=jnp.int32)
    owns = tiles_e > 0
    seg = jnp.cumsum(owns.astype(jnp.int32)) - owns.astype(jnp.int32)
    later = jnp.where((experts[None] > experts[:, None]) & owns[None], experts[None], N_EXPERTS)
    nxt = jnp.min(later, axis=1)
    nxt = jnp.where(nxt == N_EXPERTS, -1, nxt)
    per_tile = lambda v: jnp.sum(jnp.where(tile_expert[:, None] == experts[None], v[None], 0), axis=1)
    return dict(tile_expert=i32(tile_expert), n_used=i32(tile_end[-1:]),
                seg_parity=i32(per_tile(seg % 2)), next_expert=i32(per_tile(nxt)),
                run_start=i32(run_start)[:, None, :], run_len=i32(c)[:, None, :],
                pad_start=i32(first_row + counts)[None, None, :], pad_len=i32(tiles_e * MOE_TM - counts)[None, None, :])


def _for_each_run(start_ref, len_ref, copy):
    def body(e, local):
        n = len_ref[0, e]
        g = start_ref[0, e]

        def pieces(sizes):
            for size in sizes:
                covered = jnp.bitwise_and(n, -2 * size)

                @pl.when(jnp.bitwise_and(n, size) != 0)
                def _(covered=covered, size=size):
                    copy(local + covered, g + covered, size)

        @pl.when(n >= RUN_SIZES[2])
        def _():
            pieces(RUN_SIZES[:3])

        pieces(RUN_SIZES[3:])
        return local + n
    lax.fori_loop(0, N_EXPERTS, body, jnp.int32(0))


def _rows(ref, start, size):
    start = 0 if isinstance(start, int) and start == 0 else pl.multiple_of(start * SUB, SUB)
    return ref.at[pl.ds(start, size * SUB), :]


def _from_tiles(ref, rows):
    return jnp.concatenate([ref[pl.ds(c, rows, stride=SUB), :] for c in range(SUB)], axis=1)


def _to_tiles(ref, x):
    for c in range(SUB):
        ref[pl.ds(c, x.shape[0], stride=SUB), :] = x[:, c * LANES:(c + 1) * LANES]


def _run_spec(index):
    return pl.BlockSpec((None, 1, N_EXPERTS), index, memory_space=pltpu.SMEM)


def _dispatch_kernel(rs_ref, rl_ref, ps_ref, pn_ref, u_ref, pos_ref, xs_hbm, stage, zeros, sem, zsem):
    i = pl.program_id(0)
    slot = i % 2

    def wait(s):
        pltpu.make_async_copy(stage.at[s], stage.at[s], sem.at[s]).wait()

    @pl.when(i == 0)
    def _():
        zeros[...] = jnp.zeros_like(zeros)

        def pad(local, g, size):
            return pltpu.make_async_copy(_rows(zeros, 0, size), _rows(xs_hbm, g, size), zsem.at[0])
        _for_each_run(ps_ref, pn_ref, lambda *a: pad(*a).start())
        _for_each_run(ps_ref, pn_ref, lambda *a: pad(*a).wait())

    @pl.when(i >= 2)
    def _():
        wait(slot)

    place = pos_ref[...].astype(F32).T
    s = lax.broadcasted_iota(jnp.int32, (PAIRS_TM, TM), 0).astype(F32)
    onehot = jnp.zeros((PAIRS_TM, TM), F32)
    for k in range(TOP_K):
        onehot = jnp.where(s == place[k:k + 1, :], 1.0, onehot)
    _to_tiles(stage.at[slot], _dot(onehot.astype(BF16), u_ref[...]))
    _for_each_run(rs_ref, rl_ref, lambda l, g, size: pltpu.make_async_copy(
        _rows(stage.at[slot], l, size), _rows(xs_hbm, g, size), sem.at[slot]).start())

    @pl.when(i == NT - 1)
    def _():
        wait(1 - slot)
        wait(slot)


def _dispatch(u2, pos, plan):
    return pl.pallas_call(
        _dispatch_kernel,
        grid=(NT,),
        in_specs=[_run_spec(lambda i: (i, 0, 0)), _run_spec(lambda i: (i, 0, 0)),
                  _run_spec(lambda i: (0, 0, 0)), _run_spec(lambda i: (0, 0, 0)),
                  _row_spec(D), _row_spec(LANES)],
        out_specs=pl.BlockSpec(memory_space=pl.ANY),
        out_shape=jax.ShapeDtypeStruct((MOE_TILES * MOE_TM * SUB, LANES), F32),
        scratch_shapes=[pltpu.VMEM((2, PAIRS_TM * SUB, LANES), F32), pltpu.VMEM((TM * SUB, LANES), F32),
                        pltpu.SemaphoreType.DMA((2,)), pltpu.SemaphoreType.DMA((1,))],
        compiler_params=_params("arbitrary"),
        name="moe_dispatch",
    )(plan["run_start"], plan["run_len"], plan["pad_start"], plan["pad_len"], u2, pos)


def _moe_kernel(te_ref, nu_ref, par_ref, nxt_ref, x_ref, bgu_ref, bd_ref, wgu_hbm, wd_hbm, y_ref,
                wgu_f, wd_f, wgu_bf, wmix_ref, wd_bf, sem, *, layer):
    j = pl.program_id(0)
    used = j < nu_ref[0]
    buf = par_ref[j]

    def copies(e, s):
        return (pltpu.make_async_copy(wgu_hbm.at[layer, e], wgu_f.at[s], sem.at[s]),
                pltpu.make_async_copy(wd_hbm.at[layer, e], wd_f.at[s], sem.at[s]))

    @pl.when(jnp.logical_and(used, j == 0))
    def _():
        for cp in copies(te_ref[0], 0):
            cp.start(priority=1)

    @pl.when(jnp.logical_and(used, jnp.logical_or(j == 0, te_ref[j] != te_ref[jnp.maximum(j - 1, 0)])))
    def _():
        for cp in copies(te_ref[j], buf):
            cp.wait()

        @pl.when(nxt_ref[j] >= 0)
        def _():
            for cp in copies(nxt_ref[j], 1 - buf):
                cp.start(priority=1)

        _prep_expert(wgu_f.at[buf], wd_f.at[buf], wgu_bf, wmix_ref, wd_bf)

    @pl.when(used)
    def _():
        x = _from_tiles(x_ref, MOE_TM).astype(BF16)
        _to_tiles(y_ref, _expert_ffn(x, wgu_bf, bgu_ref, wd_bf, bd_ref))


def _moe(layer, xs, plan, w_gate_up, b_gate_up, w_down, b_down):
    rows = pl.BlockSpec((MOE_TM * SUB, LANES), lambda j, te, nu, par, nxt: (jnp.minimum(j, nu[0] - 1), 0))
    grid_spec = pltpu.PrefetchScalarGridSpec(
        num_scalar_prefetch=4,
        grid=(MOE_TILES,),
        in_specs=[rows,
                  pl.BlockSpec((None, None, 1, 2 * D_FF), lambda j, te, nu, par, nxt: (layer, te[j], 0, 0)),
                  pl.BlockSpec((None, None, 1, D), lambda j, te, nu, par, nxt: (layer, te[j], 0, 0)),
                  pl.BlockSpec(memory_space=pl.ANY), pl.BlockSpec(memory_space=pl.ANY)],
        out_specs=rows,
        scratch_shapes=[pltpu.VMEM((2, D, 2 * D_FF), F32), pltpu.VMEM((2, D_FF, D), F32),
                        pltpu.VMEM((D, 2 * D_FF), BF16), pltpu.VMEM((D // LANES, D_FF, LANES), F32),
                        pltpu.VMEM((D_FF, D), BF16), pltpu.SemaphoreType.DMA((2,))])

    def kernel(*refs):
        _moe_kernel(*refs, layer=layer)

    return pl.pallas_call(
        kernel,
        grid_spec=grid_spec,
        out_shape=jax.ShapeDtypeStruct((MOE_TILES * MOE_TM * SUB, LANES), F32),
        compiler_params=_params("arbitrary"),
        name="moe_experts",
    )(plan["tile_expert"], plan["n_used"], plan["seg_parity"], plan["next_expert"], xs,
      b_gate_up.reshape(DEPTH, N_EXPERTS, 1, 2 * D_FF), b_down.reshape(DEPTH, N_EXPERTS, 1, D), w_gate_up, w_down)


def _combine_kernel(rs_ref, rl_ref, rsn_ref, rln_ref, x1_ref, pos_ref, w_ref, g2_ref, lng_ref, lnb_ref, ys_hbm,
                    *rest, final):
    out_refs, (stage, sem) = rest[:-2], rest[-2:]
    i = pl.program_id(0)
    slot = i % 2
    other = 1 - slot

    def fetch(start_ref, len_ref, s):
        _for_each_run(start_ref, len_ref, lambda l, g, size: pltpu.make_async_copy(
            _rows(ys_hbm, g, size), _rows(stage.at[s], l, size), sem.at[s]).start())

    def wait(s):
        pltpu.make_async_copy(stage.at[s], stage.at[s], sem.at[s]).wait()

    @pl.when(i == 0)
    def _():
        fetch(rs_ref, rl_ref, 0)

    fetch(rsn_ref, rln_ref, other)
    wait(slot)
    y = _from_tiles(stage.at[slot], PAIRS_TM).astype(BF16)
    place = pos_ref[...].astype(F32)
    w = w_ref[...]
    s = lax.broadcasted_iota(jnp.int32, (TM, PAIRS_TM), 1).astype(F32)
    pick = jnp.zeros((TM, PAIRS_TM), F32)
    for k in range(TOP_K):
        pick = jnp.where(s == place[:, k:k + 1], w[:, k:k + 1], pick)
    f = _dot(pick.astype(BF16), y)
    ci = _cond_row(i)
    z = DEEPNORM_ALPHA * x1_ref[...] + g2_ref[pl.ds(ci, 1), :] * f
    out = _layer_norm(z, lng_ref[...], lnb_ref[...])
    if final:
        @pl.when(i < NT_P)
        def _():
            out_refs[0][...] = out

        @pl.when(i >= NT_P)
        def _():
            out_refs[1][...] = out
    else:
        out_refs[0][...] = out

    @pl.when(i == NT - 1)
    def _():
        wait(other)


def _combine(layer, x1, ys, pos, w, plan, mod, g, b):
    final = layer == DEPTH - 1
    nxt = lambda i: (jnp.minimum(i + 1, NT - 1), 0, 0)
    cur = lambda i: (i, 0, 0)
    if final:
        out_specs = _x_specs()
        out_shape = [jax.ShapeDtypeStruct((T_P, D), F32), jax.ShapeDtypeStruct((T_S, D), F32)]
    else:
        out_specs = _row_spec(D)
        out_shape = jax.ShapeDtypeStruct((T, D), F32)

    def kernel(*refs):
        _combine_kernel(*refs, final=final)

    return pl.pallas_call(
        kernel,
        grid=(NT,),
        in_specs=[_run_spec(cur), _run_spec(cur), _run_spec(nxt), _run_spec(nxt),
                  _row_spec(D), _row_spec(LANES), _row_spec(LANES), _mod_spec(layer, 5), _full((1, D)), _full((1, D)),
                  pl.BlockSpec(memory_space=pl.ANY)],
        out_specs=out_specs,
        out_shape=out_shape,
        scratch_shapes=[pltpu.VMEM((2, PAIRS_TM * SUB, LANES), F32), pltpu.SemaphoreType.DMA((2,))],
        compiler_params=_params("arbitrary"),
        name="moe_combine_norm" + ("_final" if final else ""),
    )(plan["run_start"], plan["run_len"], plan["run_start"], plan["run_len"], x1, pos, w, mod, g, b, ys)


def _moe_layer(layer, routed, experts, mod, ln2_g, ln2_b):
    x1, u2, pos, w, cnt = routed
    plan = _moe_plan(cnt)
    ys = _moe(layer, _dispatch(u2, pos, plan), plan, *experts)
    return _combine(layer, x1, ys, pos, w, plan, mod, ln2_g[layer][None], ln2_b[layer][None])


def _router_tail(l, ln1_g, ln1_b, router_w, router_b):
    rw = jnp.pad(router_w[l], ((0, 0), (0, LANES - N_EXPERTS)))
    rb = jnp.pad(router_b[l], (0, LANES - N_EXPERTS), constant_values=NEG)
    return ln1_g[l][None], ln1_b[l][None], rw, rb[None]


def kernel(x_prompt, x_sample, c, c_ctx, cache_diff_k, cache_diff_v, cache_na_k, cache_na_v, cache_gqa_k, cache_gqa_v, w_mod, b_mod, ln1_g, ln1_b, ln2_g, ln2_b, ab_w_in, ab_conv_w, ab_lambda_q1, ab_lambda_k1, ab_lambda_q2, ab_lambda_k2, ab_subln_g, ab_w_out, cd_w_in, cd_na_rpb, cd_q_norm_g, cd_k_norm_g, cd_w_out, router_w, router_b, w_gate_up, b_gate_up, w_down, b_down):
    xp = x_prompt.reshape(T_P, D)
    xs = x_sample.reshape(T_S, D)
    cond8 = jnp.concatenate([c_ctx[None], c, jnp.zeros((8 - 1 - B_S, D), F32)], axis=0)
    mod = _modulation(cond8, w_mod, b_mod)
    rope = _rope_tables()
    experts = (w_gate_up, b_gate_up, w_down, b_down)

    gb, y, q, k, v, new_diff_k, new_diff_v = _ab_in(xp, xs, mod, ab_w_in[0].astype(BF16), rope)
    lam_init = 0.8 - 0.6 * 1.0
    diff = (jnp.stack([ab_lambda_q1[0], ab_lambda_k1[0], ab_lambda_q2[0], ab_lambda_k2[0]]), ab_subln_g[0][None])
    attn = _flash_pair(q, k, v, latent=False, out_cols=512, col0=0, post="diff", diff=diff, lam_init=lam_init)
    attn = _flash_pair(q, k, v, latent=True, out_cols=512, col0=0, post="diff", diff=diff, lam_init=lam_init,
                       ctx=(cache_diff_k, cache_diff_v), ctx_mode="wide", prev=attn)
    routed = _ab_out(xp, xs, gb, y, ab_conv_w[0], attn, ab_w_out[0].astype(BF16), mod,
                     _router_tail(0, ln1_g, ln1_b, router_w, router_b))
    x = _moe_layer(0, routed, experts, mod, ln2_g, ln2_b)

    qg = jnp.tile(cd_q_norm_g[0], 2)[None]
    kg = jnp.tile(cd_k_norm_g[0], 2)[None]
    nq, nk, nv, gq, gk, gv, new_na_k, new_na_v, new_gqa_k, new_gqa_v = _cd_in(
        x, mod, cd_w_in[0].astype(BF16), qg, kg, rope)
    merged = _flash_pair(nq, nk, nv, latent=False, out_cols=D, col0=0, post="select")
    merged = _flash_pair(gq, gk, gv, latent=False, out_cols=D, col0=2, post="select", prev=merged)
    merged = _na_latent(nq, nk, nv, cache_na_k, cache_na_v, _na_bias(cd_na_rpb[0]), merged)
    merged = _flash_pair(gq, gk, gv, latent=True, out_cols=D, col0=2, post="select",
                         ctx=(cache_gqa_k, cache_gqa_v), ctx_mode="dup", prev=merged)
    routed = _cd_out(x, merged, cd_w_out[0].astype(BF16), mod,
                     _router_tail(1, ln1_g, ln1_b, router_w, router_b))
    y_p, y_s = _moe_layer(1, routed, experts, mod, ln2_g, ln2_b)

    return (y_p.reshape(B_P, N_P, D), y_s.reshape(B_S, N_S, D), new_diff_k, new_diff_v,
            new_na_k, new_na_v, new_gqa_k, new_gqa_v)
```

```python
import jax
import jax.numpy as jnp
import numpy as np
from jax import lax
from jax.experimental import pallas as pl
from jax.experimental.pallas import tpu as pltpu

F32 = jnp.float32
BF16 = jnp.bfloat16

D = 1024
B_P, N_P = 16, 256
B_S, N_S = 2, 4096
PAST = 256
T_P, T_S = B_P * N_P, B_S * N_S
T = T_P + T_S
TM = 256
NT_P, NT_S, NT = T_P // TM, T_S // TM, T // TM
TILES_PER_GRID = N_S // TM
GRID_W = 64
GRID_H = N_S // GRID_W
HD = 64
DEPTH = 2
N_EXPERTS = 32
TOP_K = 4
D_FF = 1024
NA_WIN_R, NA_WIN_C = 8, 16
NA_QROWS = 4
NA_KROWS = 12
SWIGLU_LIMIT = 7.0
SWIGLU_ALPHA = 1.702
ROPE_THETA = 10000.0
DEEPNORM_ALPHA = (2 * DEPTH) ** 0.25
LN_EPS = 1e-5
RMS_EPS = 1e-6
LOG2E = 1.4426950408889634
QK_SCALE = HD ** -0.5 * LOG2E
NEG = -1e30
MOE_TM = 256
MOE_PAIRS = TOP_K * T
MOE_TILES = MOE_PAIRS // MOE_TM + N_EXPERTS
PAIRS_TM = TOP_K * TM
RUN_SIZES = tuple(TM >> b for b in range(TM.bit_length()))
LANES = 128
SUB = 8

VMEM_LIMIT = 56 * 1024 * 1024


def _params(*sem):
    return pltpu.CompilerParams(dimension_semantics=sem, vmem_limit_bytes=VMEM_LIMIT)


def _split(x):
    hi = x.astype(BF16)
    lo = (x - hi.astype(F32)).astype(BF16)
    return hi, lo


def _dot(a, b):
    return jnp.dot(a, b, preferred_element_type=F32)


def _dot3(a, b):
    ah, al = _split(a)
    bh, bl = _split(b)
    return _dot(ah, bh) + (_dot(ah, bl) + _dot(al, bh))


def _cond_row(i):
    return jnp.where(i < NT_P, 0, 1 + (i - NT_P) // TILES_PER_GRID)


def _layer_norm(z, g, b):
    mu = jnp.mean(z, axis=-1, keepdims=True)
    zc = z - mu
    var = jnp.mean(zc * zc, axis=-1, keepdims=True)
    return zc * lax.rsqrt(var + LN_EPS) * g + b


def _low_half(rows):
    return lax.broadcasted_iota(jnp.int32, (rows, LANES), 1) < HD


def _mod_kernel(c_ref, w_ref, b_ref, o_ref):
    c = c_ref[...]
    o_ref[...] = _dot3(c * jax.nn.sigmoid(c), w_ref[...]) + b_ref[...]


def _modulation(cond8, w_mod, b_mod):
    return pl.pallas_call(
        _mod_kernel,
        grid=(DEPTH, 6),
        in_specs=[pl.BlockSpec((8, D), lambda l, j: (0, 0)),
                  pl.BlockSpec((None, D, D), lambda l, j: (l, 0, j)),
                  pl.BlockSpec((None, 1, D), lambda l, j: (l, 0, j))],
        out_specs=pl.BlockSpec((None, 8, D), lambda l, j: (l, 0, j)),
        out_shape=jax.ShapeDtypeStruct((DEPTH, 8, 6 * D), F32),
        compiler_params=_params("arbitrary", "arbitrary"),
        name="modulation",
    )(cond8, w_mod, b_mod.reshape(DEPTH, 1, 6 * D))


def _mod_spec(layer, chunk):
    return pl.BlockSpec((None, 8, D), lambda i, _l=layer, _c=chunk: (_l, 0, _c))


def _full(shape):
    return pl.BlockSpec(shape, lambda i: (0,) * len(shape))


def _rope_tables():
    t = np.arange(N_S)
    half = HD // 2
    inv = ROPE_THETA ** (-np.arange(0, half, 2, dtype=np.float64) / half)
    inv_lane = np.tile(np.repeat(inv, 2), 2 * LANES // HD)
    lane = np.arange(LANES)
    by_row = (lane % HD) < half
    ang = np.where(by_row[None], (t // GRID_W)[:, None], (t % GRID_W)[:, None]) * inv_lane[None]
    cos, sin = np.cos(ang), np.sin(ang)
    even = (lane % 2) == 0
    return tuple(jnp.asarray(a, F32) for a in (cos, np.where(even, -sin, 0.0), np.where(even, 0.0, sin)))


def _rope(x, a, b, c):
    return x * a + pltpu.roll(x, LANES - 1, axis=1) * b + pltpu.roll(x, 1, axis=1) * c


def _rope_or_identity(identity, ra_ref, rb_ref, rc_ref):
    return (jnp.where(identity, 1.0, ra_ref[...]), jnp.where(identity, 0.0, rb_ref[...]),
            jnp.where(identity, 0.0, rc_ref[...]))


def _rope_spec():
    return pl.BlockSpec((TM, LANES), lambda i: (jnp.maximum(i - NT_P, 0) % TILES_PER_GRID, 0))


def _x_specs():
    return [pl.BlockSpec((TM, D), lambda i: (jnp.minimum(i, NT_P - 1), 0)),
            pl.BlockSpec((TM, D), lambda i: (jnp.maximum(i - NT_P, 0), 0))]


def _cache_spec(heads, width):
    return pl.BlockSpec((None, None, heads, N_P, width), lambda i: (jnp.minimum(i, NT_P - 1), 0, 0, 0, 0))


def _row_spec(width):
    return pl.BlockSpec((TM, width), lambda i: (i, 0))


def _hm_spec(n):
    return pl.BlockSpec((n, TM, LANES), lambda i: (0, i, 0))


AB_Q0, AB_K0, AB_V0 = 1536, 2048, 2560


def _tile(p, col0, j):
    return p[:, col0 + j * LANES:col0 + (j + 1) * LANES]


def _ab_in_kernel(xp_ref, xs_ref, sh_ref, sc_ref, w_ref, ra_ref, rb_ref, rc_ref,
                  gb_ref, y_ref, q_ref, k_ref, v_ref, kc_ref, vc_ref):
    i = pl.program_id(0)
    is_p = i < NT_P
    ci = _cond_row(i)
    x = jnp.where(is_p, xp_ref[...], xs_ref[...])
    u = x * (1.0 + sc_ref[pl.ds(ci, 1), :]) + sh_ref[pl.ds(ci, 1), :]
    p = _dot(u.astype(BF16), w_ref[...])
    gb_ref[...] = p[:, 0:512]
    y_ref[...] = p[:, 512:1024] * p[:, 1024:1536]
    a, b, c = _rope_or_identity(is_p, ra_ref, rb_ref, rc_ref)
    for h in range(4):
        v_ref[h] = _tile(p, AB_V0, h).astype(BF16)
        q_ref[h] = (_rope(_tile(p, AB_Q0, h), a, b, c) * QK_SCALE).astype(BF16)
        k_ref[h] = _rope(_tile(p, AB_K0, h), a, b, c).astype(BF16)

    @pl.when(is_p)
    def _():
        for h in range(4):
            kc_ref[h] = _tile(p, AB_K0, h)
            vc_ref[h] = _tile(p, AB_V0, h)


def _ab_in(xp, xs, mod, w_in, rope):
    hm = jax.ShapeDtypeStruct((4, T, LANES), BF16)
    cache = jax.ShapeDtypeStruct((B_P, 1, 4, N_P, LANES), F32)
    half = jax.ShapeDtypeStruct((T, 512), F32)
    return pl.pallas_call(
        _ab_in_kernel,
        grid=(NT,),
        in_specs=_x_specs() + [_mod_spec(0, 0), _mod_spec(0, 1), _full((D, 3072)),
                               _rope_spec(), _rope_spec(), _rope_spec()],
        out_specs=[_row_spec(512), _row_spec(512), _hm_spec(4), _hm_spec(4), _hm_spec(4),
                   _cache_spec(4, LANES), _cache_spec(4, LANES)],
        out_shape=[half, half, hm, hm, hm, cache, cache],
        compiler_params=_params("arbitrary"),
        name="ab_in_proj",
    )(xp, xs, mod, mod, w_in, *rope)


CD_NQ, CD_NK, CD_NV, CD_GQ, CD_GK, CD_GV = 0, 512, 1024, 1536, 2048, 2176


def _seg_mean64(s):
    r = lax.broadcasted_iota(jnp.int32, (LANES, LANES), 0) // HD
    c = lax.broadcasted_iota(jnp.int32, (LANES, LANES), 1) // HD
    seg = jnp.where(r == c, 1.0, 0.0).astype(BF16)
    hi, lo = _split(s)
    return (_dot(hi, seg) + _dot(lo, seg)) * (1.0 / HD)


def _rms64(x, g):
    return x * lax.rsqrt(_seg_mean64(x * x) + RMS_EPS) * g


def _dup_halves(x, lo):
    r = pltpu.roll(x, HD, axis=1)
    return jnp.where(lo, x, r), jnp.where(lo, r, x)


def _cd_in_kernel(x_ref, sh_ref, sc_ref, w_ref, qg_ref, kg_ref, ra_ref, rb_ref, rc_ref,
                  nq_ref, nk_ref, nv_ref, gq_ref, gk_ref, gv_ref,
                  nkc_ref, nvc_ref, gkc_ref, gvc_ref):
    i = pl.program_id(0)
    is_p = i < NT_P
    ci = _cond_row(i)
    u = x_ref[...] * (1.0 + sc_ref[pl.ds(ci, 1), :]) + sh_ref[pl.ds(ci, 1), :]
    p = _dot(u.astype(BF16), w_ref[...])
    lo = _low_half(TM)
    for j in range(4):
        nq_ref[j] = (_tile(p, CD_NQ, j) * QK_SCALE).astype(BF16)
        nk_ref[j] = _tile(p, CD_NK, j).astype(BF16)
        nv_ref[j] = _tile(p, CD_NV, j).astype(BF16)
    gq = [_rms64(_tile(p, CD_GQ, j), qg_ref[...]) for j in range(4)]
    gk = _rms64(_tile(p, CD_GK, 0), kg_ref[...])
    gv = _tile(p, CD_GV, 0)
    v0, v1 = _dup_halves(gv, lo)
    gv_ref[0] = v0.astype(BF16)
    gv_ref[1] = v1.astype(BF16)
    a, b, c = _rope_or_identity(is_p, ra_ref, rb_ref, rc_ref)
    for j in range(4):
        gq_ref[j] = (_rope(gq[j], a, b, c) * QK_SCALE).astype(BF16)
    k0, k1 = _dup_halves(_rope(gk, a, b, c), lo)
    gk_ref[0] = k0.astype(BF16)
    gk_ref[1] = k1.astype(BF16)

    @pl.when(is_p)
    def _():
        gkc_ref[0] = k0[:, 0:HD]
        gkc_ref[1] = k1[:, 0:HD]
        gvc_ref[0] = v0[:, 0:HD]
        gvc_ref[1] = v1[:, 0:HD]
        for j in range(4):
            for src, dst in ((CD_NK, nkc_ref), (CD_NV, nvc_ref)):
                a, b = _dup_halves(_tile(p, src, j), lo)
                dst[2 * j] = a[:, 0:HD]
                dst[2 * j + 1] = b[:, 0:HD]


def _cd_in(x, mod, w_in, qg, kg, rope):
    hm4 = jax.ShapeDtypeStruct((4, T, LANES), BF16)
    hm2 = jax.ShapeDtypeStruct((2, T, LANES), BF16)
    c8 = jax.ShapeDtypeStruct((B_P, 1, 8, N_P, HD), F32)
    c2 = jax.ShapeDtypeStruct((B_P, 1, 2, N_P, HD), F32)
    return pl.pallas_call(
        _cd_in_kernel,
        grid=(NT,),
        in_specs=[_row_spec(D), _mod_spec(1, 0), _mod_spec(1, 1), _full((D, 2304)),
                  _full((1, LANES)), _full((1, LANES)), _rope_spec(), _rope_spec(), _rope_spec()],
        out_specs=[_hm_spec(4), _hm_spec(4), _hm_spec(4), _hm_spec(4), _hm_spec(2), _hm_spec(2),
                   _cache_spec(8, HD), _cache_spec(8, HD), _cache_spec(2, HD), _cache_spec(2, HD)],
        out_shape=[hm4, hm4, hm4, hm4, hm2, hm2, c8, c8, c2, c2],
        compiler_params=_params("arbitrary"),
        name="cd_in_proj",
    )(x, mod, mod, w_in, qg, kg, *rope)


def _stack_pairs(q_ref, n_q, tq):
    lo = _low_half(tq)
    parts = []
    for j in range(n_q):
        q = q_ref[j]
        zero = jnp.zeros_like(q)
        parts += [jnp.where(lo, q, zero), jnp.where(lo, zero, q)]
    return jnp.concatenate(parts, axis=0), lo


def _qk(qs, kb):
    return lax.dot_general(qs, kb, (((1,), (1,)), ((), ())), preferred_element_type=F32)


def _ctx_tile(ref, mode):
    if mode == "wide":
        x = ref[...]
    elif mode == "pair":
        x = jnp.concatenate([ref[0], ref[1]], axis=1)
    else:
        x = jnp.concatenate([ref[...], ref[...]], axis=1)
    return x.astype(BF16)


def _with_ones(v):
    lane = lax.broadcasted_iota(jnp.int32, v.shape, 1)
    return jnp.concatenate([v, jnp.where(lane == 0, 1.0, 0.0).astype(v.dtype)], axis=1)


VT_ROWS = LANES + 16


def _stack_pairs_t(q_ref, n_q, tq):
    low = lax.broadcasted_iota(jnp.int32, (LANES, tq), 0) < HD
    parts = []
    for j in range(n_q):
        qt = q_ref[j].astype(F32).T
        parts += [jnp.where(low, qt, 0.0), jnp.where(low, 0.0, qt)]
    return jnp.concatenate(parts, axis=1).astype(BF16), low


def _values_t(v):
    row = lax.broadcasted_iota(jnp.int32, (VT_ROWS - LANES, v.shape[0]), 0)
    return jnp.concatenate([v.astype(F32).T, jnp.where(row == 0, 1.0, 0.0)], axis=0).astype(BF16)


def _flash_pair_kernel(*refs, gs, ctx_mode, **kw):
    it = iter(refs)
    q_ref, k_ref, v_ref = next(it), next(it), next(it)
    rest = list(it)
    vct_ref = rest.pop() if ctx_mode else None
    vt_ref = rest.pop()
    o_ref = rest.pop()
    n_q = q_ref.shape[0] // gs
    for gi in range(gs):
        _flash_group(q_ref.at[gi * n_q:(gi + 1) * n_q], k_ref.at[gi], v_ref.at[gi], *rest,
                     o_ref.at[:, gi * n_q * LANES:(gi + 1) * n_q * LANES], vt_ref.at[gi], vct_ref,
                     n_q=n_q, ctx_mode=ctx_mode, **kw)


def _flash_group(*refs, n_q, tq, nk, tk, ctx_mode, post, lam_init):
    it = iter(refs)
    q_ref, k_ref, v_ref = next(it), next(it), next(it)
    kc_ref, vc_ref = (next(it), next(it)) if ctx_mode else (None, None)
    lam_ref, g_ref = (next(it), next(it)) if post == "diff" else (None, None)
    o_ref, vt_ref, vct_ref = next(it), next(it), next(it)

    @pl.when(pl.program_id(2) == 0)
    def _():
        vt_ref[...] = _values_t(v_ref[...])
        if ctx_mode:
            vct_ref[...] = _values_t(_ctx_tile(vc_ref, ctx_mode))

    qs, low = _stack_pairs_t(q_ref, n_q, tq)
    cols = 2 * n_q * tq

    def step(kb, vtb, carry):
        m, acc = carry
        s = _dot(kb, qs)
        m_new = jnp.maximum(m, jnp.max(s, axis=0, keepdims=True))
        p = jnp.exp2((s - m_new).astype(BF16))
        return m_new, jnp.exp2(m - m_new) * acc + _dot(vtb, p)

    carry = (jnp.full((1, cols), NEG, F32), jnp.zeros((VT_ROWS, cols), F32))
    for c in range(nk // tk):
        carry = step(k_ref[c * tk:(c + 1) * tk, :], vt_ref[:, c * tk:(c + 1) * tk], carry)
    if ctx_mode:
        carry = step(_ctx_tile(kc_ref, ctx_mode), vct_ref[...], carry)
    _, acc = carry
    o = acc[0:LANES] / acc[LANES:LANES + 1]

    if post == "diff":
        lp = lam_ref[...]
        lam = (jnp.exp(jnp.sum(lp[0:1] * lp[1:2], axis=1, keepdims=True))
               - jnp.exp(jnp.sum(lp[2:3] * lp[3:4], axis=1, keepdims=True)) + lam_init)
        a = o[:, 0:tq] - lam * o[:, tq:2 * tq]
        ms = jnp.mean(a * a, axis=0, keepdims=True)
        o_ref[...] = ((a * lax.rsqrt(ms + RMS_EPS)).T * g_ref[...] * (1.0 - lam_init)).astype(o_ref.dtype)
    else:
        for j in range(n_q):
            o_ref[:, j * LANES:(j + 1) * LANES] = jnp.where(
                low, o[:, 2 * j * tq:(2 * j + 1) * tq], o[:, (2 * j + 1) * tq:(2 * j + 2) * tq]).T.astype(o_ref.dtype)


def _flash_pair(q, k, v, *, latent, out_cols, col0, post, ctx=None, ctx_mode=None, diff=None,
                lam_init=0.0, prev=None):
    groups = k.shape[0]
    n_q = q.shape[0] // groups
    if latent:
        tq, nk, tk, gs = TM, N_S, 1024, 1
        grid = (B_S, groups, TILES_PER_GRID)
        qrow = lambda b, g, i: NT_P + b * TILES_PER_GRID + i
        krow = lambda b, g, i: T_P // N_S + b
    else:
        tq, nk, tk, gs = N_P, N_P, N_P, groups
        grid = (B_P, 1, 1)
        qrow = lambda b, g, i: b
        krow = lambda b, g, i: b
    in_specs = [pl.BlockSpec((gs * n_q, tq, LANES), lambda b, g, i: (g, qrow(b, g, i), 0)),
                pl.BlockSpec((gs, nk, LANES), lambda b, g, i: (g, krow(b, g, i), 0)),
                pl.BlockSpec((gs, nk, LANES), lambda b, g, i: (g, krow(b, g, i), 0))]
    args = [q, k, v]
    if ctx is not None:
        if ctx_mode == "wide":
            spec = pl.BlockSpec((None, None, None, PAST, LANES), lambda b, g, i: (b, 0, g, 0, 0))
        elif ctx_mode == "pair":
            spec = pl.BlockSpec((None, None, 2, PAST, HD), lambda b, g, i: (b, 0, g, 0, 0))
        else:
            spec = pl.BlockSpec((None, None, None, PAST, HD), lambda b, g, i: (b, 0, g, 0, 0))
        in_specs += [spec, spec]
        args += list(ctx)
    if diff is not None:
        in_specs += [pl.BlockSpec((4, HD), lambda b, g, i: (0, 0)),
                     pl.BlockSpec((1, LANES), lambda b, g, i: (0, 0))]
        args += list(diff)
    aliases = {}
    if prev is not None:
        aliases = {len(args): 0}
        in_specs.append(pl.BlockSpec(memory_space=pl.ANY))
        args.append(prev)

    n_in = len(args) - (prev is not None)
    scratch = [pltpu.VMEM((gs, VT_ROWS, nk), BF16)] + ([pltpu.VMEM((VT_ROWS, PAST), BF16)] if ctx is not None else [])

    def kernel(*refs):
        _flash_pair_kernel(*refs[:n_in], *refs[len(args):], gs=gs, tq=tq, nk=nk, tk=tk,
                           ctx_mode=ctx_mode if ctx is not None else None, post=post, lam_init=lam_init)

    return pl.pallas_call(
        kernel,
        grid=grid,
        in_specs=in_specs,
        out_specs=pl.BlockSpec((tq, gs * n_q * LANES), lambda b, g, i: (qrow(b, g, i), col0 // gs + g)),
        out_shape=jax.ShapeDtypeStruct((T, out_cols), BF16),
        scratch_shapes=scratch,
        input_output_aliases=aliases,
        compiler_params=_params("arbitrary", "arbitrary", "arbitrary"),
        name="attn_" + post + ("_latent" if latent else "_context") + str(n_q),
    )(*args)


NA_TQ = NA_QROWS * GRID_W
NA_TK = NA_KROWS * GRID_W


NA_BLOCK_POS = ((0, 0), (NA_QROWS, 0), (GRID_H - NA_QROWS, GRID_H - NA_KROWS))
N_DR = 2 * NA_WIN_R - 1
N_DC = 2 * NA_WIN_C - 1


def _na_bias_kernel(rpb_ref, o_ref):
    qc = lax.broadcasted_iota(jnp.int32, (GRID_W, LANES), 0)
    lane = lax.broadcasted_iota(jnp.int32, (GRID_W, LANES), 1)
    kc = lane % GRID_W
    cs = jnp.clip(qc - NA_WIN_C // 2, 0, GRID_W - NA_WIN_C)
    col_ok = jnp.logical_and(kc >= cs, kc < cs + NA_WIN_C)
    lo = lane < GRID_W
    neg = jnp.full((GRID_W, LANES), NEG, F32)
    for head in range(2):
        toeplitz = []
        for dr in range(N_DR):
            r = jnp.broadcast_to(rpb_ref[head, dr:dr + 1, :] * LOG2E, (GRID_W, LANES))
            t = jnp.where(lo, pltpu.roll(r, LANES - (NA_WIN_C - 1), axis=1, stride=1, stride_axis=0),
                          pltpu.roll(r, GRID_W - (NA_WIN_C - 1), axis=1, stride=1, stride_axis=0))
            toeplitz.append(jnp.where(col_ok, t, neg))
        for pos, (r0, k0) in enumerate(NA_BLOCK_POS):
            for i in range(NA_QROWS):
                qr = r0 + i
                rs = min(max(qr - NA_WIN_R // 2, 0), GRID_H - NA_WIN_R)
                tiles = [toeplitz[k0 + j - qr + NA_WIN_R - 1] if rs <= k0 + j < rs + NA_WIN_R else neg
                         for j in range(NA_KROWS)]
                for jp in range(NA_KROWS // 2):
                    o_ref[pos, pl.ds(head * NA_TQ + i * GRID_W, GRID_W), pl.ds(jp * LANES, LANES)] = jnp.where(
                        lo, tiles[2 * jp], tiles[2 * jp + 1])


def _na_bias(rpb):
    rpb_pad = jnp.pad(rpb, ((0, 0), (0, 16 - N_DR), (0, LANES - N_DC)), constant_values=NEG)
    return pl.pallas_call(
        _na_bias_kernel,
        grid=(4,),
        in_specs=[pl.BlockSpec((2, 16, LANES), lambda g: (g, 0, 0))],
        out_specs=pl.BlockSpec((3, None, 2 * NA_TQ, NA_TK), lambda g: (0, g, 0, 0)),
        out_shape=jax.ShapeDtypeStruct((3, 4, 2 * NA_TQ, NA_TK), F32),
        compiler_params=_params("arbitrary"),
        name="na_bias_table",
    )(rpb_pad)


def _na_kernel(q_ref, k_ref, v_ref, kc_ref, vc_ref, bm_ref, _, o_ref):
    i = pl.program_id(2)
    qs, lo = _stack_pairs(q_ref, 1, NA_TQ)
    k0 = jnp.clip(i * NA_QROWS - NA_WIN_R // 2, 0, GRID_H - NA_KROWS)
    start = pl.multiple_of(k0 * GRID_W, GRID_W)
    kw = k_ref[pl.ds(start, NA_TK), :]
    vw = v_ref[pl.ds(start, NA_TK), :]
    s_w = _qk(qs, kw) + bm_ref[...]
    s_c = _qk(qs, _ctx_tile(kc_ref, "pair"))
    m = jnp.maximum(jnp.max(s_w, axis=1, keepdims=True), jnp.max(s_c, axis=1, keepdims=True))
    p_w = jnp.exp2((s_w - m).astype(BF16))
    p_c = jnp.exp2((s_c - m).astype(BF16))
    acc = _dot(p_w, _with_ones(vw)) + _dot(p_c, _with_ones(_ctx_tile(vc_ref, "pair")))
    o = acc[:, 0:LANES] / acc[:, LANES:LANES + 1]
    o_ref[...] = jnp.where(lo, o[0:NA_TQ], o[NA_TQ:2 * NA_TQ]).astype(o_ref.dtype)


def _na_latent(q, k, v, kc, vc, bm, prev):
    nblk = N_S // NA_TQ
    qrow = lambda b, g, i: T_P // NA_TQ + b * nblk + i
    krow = lambda b, g, i: T_P // N_S + b
    cfg = lambda i: jnp.where(i == 0, 0, jnp.where(i == nblk - 1, 2, 1))
    ctx_spec = pl.BlockSpec((None, None, 2, PAST, HD), lambda b, g, i: (b, 0, g, 0, 0))
    return pl.pallas_call(
        _na_kernel,
        grid=(B_S, 4, nblk),
        in_specs=[pl.BlockSpec((1, NA_TQ, LANES), lambda b, g, i: (g, qrow(b, g, i), 0)),
                  pl.BlockSpec((None, N_S, LANES), lambda b, g, i: (g, krow(b, g, i), 0)),
                  pl.BlockSpec((None, N_S, LANES), lambda b, g, i: (g, krow(b, g, i), 0)),
                  ctx_spec, ctx_spec,
                  pl.BlockSpec((None, None, 2 * NA_TQ, NA_TK), lambda b, g, i: (cfg(i), g, 0, 0)),
                  pl.BlockSpec(memory_space=pl.ANY)],
        out_specs=pl.BlockSpec((NA_TQ, LANES), lambda b, g, i: (qrow(b, g, i), g)),
        out_shape=jax.ShapeDtypeStruct((T, D), BF16),
        input_output_aliases={6: 0},
        compiler_params=_params("arbitrary", "arbitrary", "arbitrary"),
        name="attn_window_latent",
    )(q, k, v, kc, vc, bm, prev)


def _top4(logits):
    lane = lax.broadcasted_iota(jnp.int32, logits.shape, 1).astype(F32)
    rest = logits
    tops, firsts = [], []
    for _ in range(TOP_K):
        m = jnp.max(rest, axis=1, keepdims=True)
        first = jnp.min(jnp.where(rest == m, lane, float(LANES)), axis=1, keepdims=True)
        tops.append(m)
        firsts.append(first)
        rest = jnp.where(lane == first, -jnp.inf, rest)
    es = [jnp.exp(m - tops[0]) for m in tops]
    denom = es[0] + es[1] + es[2] + es[3]
    w = jnp.zeros_like(logits)
    for k in range(TOP_K):
        w = jnp.where(lane == float(k), es[k] / denom, w)
    return firsts, w


def _local_sort(firsts):
    lane = lax.broadcasted_iota(jnp.int32, (TM, LANES), 1).astype(F32)
    hots = [lane == f for f in firsts]
    sel = jnp.zeros((TM, LANES), F32)
    for hot in hots:
        sel = jnp.where(hot, 1.0, sel)
    r = lax.broadcasted_iota(jnp.int32, (TM, TM), 0)
    c = lax.broadcasted_iota(jnp.int32, (TM, TM), 1)
    earlier = _dot(jnp.where(c < r, 1.0, 0.0).astype(BF16), sel.astype(BF16))
    cnt = jnp.sum(sel, axis=0, keepdims=True)
    r = lax.broadcasted_iota(jnp.int32, (LANES, LANES), 0)
    c = lax.broadcasted_iota(jnp.int32, (LANES, LANES), 1)
    start = _dot(jnp.broadcast_to(cnt, (SUB, LANES)).astype(BF16), jnp.where(r < c, 1.0, 0.0).astype(BF16))[0:1]
    place = start + earlier
    pos = jnp.zeros((TM, LANES), F32)
    for k, hot in enumerate(hots):
        pos = jnp.where(lane == float(k), jnp.sum(jnp.where(hot, place, 0.0), axis=1, keepdims=True), pos)
    return pos.astype(jnp.int32), cnt.astype(jnp.int32)


def _mixer_tail(x_of, h_of, ci, g1_ref, sh2_ref, sc2_ref, lng_ref, lnb_ref, rw_ref, rb_ref,
                x1_ref, u2_ref, pos_ref, w_ref, cnt_ref):
    rows = slice(0, TM)
    x1 = _layer_norm(DEEPNORM_ALPHA * x_of(rows) + g1_ref[pl.ds(ci, 1), :] * h_of(rows), lng_ref[...], lnb_ref[...])
    x1_ref[...] = x1
    u2 = x1 * (1.0 + sc2_ref[pl.ds(ci, 1), :]) + sh2_ref[pl.ds(ci, 1), :]
    u2_ref[...] = u2.astype(BF16)
    firsts, w_ref[...] = _top4(_dot3(u2, rw_ref[...]) + rb_ref[...])
    pos, cnt = _local_sort(firsts)
    pos_ref[...] = pos
    cnt_ref[...] = jnp.broadcast_to(cnt, (SUB, LANES))


def _ab_out_kernel(xp_ref, xs_ref, gb_ref, y_ref, yp_ref, yn_ref, cv_ref, at_ref, wo_ref,
                   g1_ref, sh2_ref, sc2_ref, lng_ref, lnb_ref, rw_ref, rb_ref, *out_refs):
    i = pl.program_id(0)
    is_p = i < NT_P
    ci = _cond_row(i)
    j = (i - NT_P) % TILES_PER_GRID
    first = jnp.logical_or(is_p, j == 0)
    last = jnp.logical_or(is_p, j == TILES_PER_GRID - 1)
    y = y_ref[...]
    row = lax.broadcasted_iota(jnp.int32, y.shape, 0)
    before = jnp.where(first, 0.0, yp_ref[7:8, :])
    after = jnp.where(last, 0.0, yn_ref[0:1, :])
    y_prev = jnp.where(row == 0, before, pltpu.roll(y, 1, axis=0))
    y_next = jnp.where(row == TM - 1, after, pltpu.roll(y, TM - 1, axis=0))
    cv = cv_ref[...]
    conv = (gb_ref[...] * (y_prev * cv[0:1] + y * cv[1:2] + y_next * cv[2:3])).astype(BF16)
    _mixer_tail(lambda r: jnp.where(is_p, xp_ref[r, :], xs_ref[r, :]),
                lambda r: _dot(conv[r], wo_ref[0:512, :]) + _dot(at_ref[r, :], wo_ref[512:1024, :]),
                ci, g1_ref, sh2_ref, sc2_ref, lng_ref, lnb_ref, rw_ref, rb_ref, *out_refs)


def _cd_out_kernel(x_ref, mg_ref, wo_ref, g1_ref, sh2_ref, sc2_ref, lng_ref, lnb_ref, rw_ref, rb_ref,
                   *out_refs):
    ci = _cond_row(pl.program_id(0))
    _mixer_tail(lambda r: x_ref[r, :], lambda r: _dot(mg_ref[r, :], wo_ref[...]),
                ci, g1_ref, sh2_ref, sc2_ref, lng_ref, lnb_ref, rw_ref, rb_ref, *out_refs)


def _tail_specs(layer):
    return [_mod_spec(layer, 2), _mod_spec(layer, 3), _mod_spec(layer, 4),
            _full((1, D)), _full((1, D)), _full((D, LANES)), _full((1, LANES))]


_TAIL_OUT_SPECS = [_row_spec(D), _row_spec(D), _row_spec(LANES), _row_spec(LANES),
                   pl.BlockSpec((None, SUB, LANES), lambda i: (i, 0, 0))]
_TAIL_OUT_SHAPES = [jax.ShapeDtypeStruct((T, D), F32), jax.ShapeDtypeStruct((T, D), BF16),
                    jax.ShapeDtypeStruct((T, LANES), jnp.int32), jax.ShapeDtypeStruct((T, LANES), F32),
                    jax.ShapeDtypeStruct((NT, SUB, LANES), jnp.int32)]


def _ab_out(xp, xs, gb, y, conv_w, attn, w_out, mod, tail):
    halo_prev = pl.BlockSpec((8, 512), lambda i: (jnp.maximum(i * (TM // 8) - 1, 0), 0))
    halo_next = pl.BlockSpec((8, 512), lambda i: (jnp.minimum((i + 1) * (TM // 8), T // 8 - 1), 0))
    return pl.pallas_call(
        _ab_out_kernel,
        grid=(NT,),
        in_specs=_x_specs() + [_row_spec(512), _row_spec(512), halo_prev, halo_next, _full((3, 512)),
                               _row_spec(512), _full((D, D))] + _tail_specs(0),
        out_specs=_TAIL_OUT_SPECS,
        out_shape=_TAIL_OUT_SHAPES,
        compiler_params=_params("arbitrary"),
        name="ab_out_proj",
    )(xp, xs, gb, y, y, y, conv_w, attn, w_out, mod, mod, mod, *tail)


def _cd_out(x, merged, w_out, mod, tail):
    return pl.pallas_call(
        _cd_out_kernel,
        grid=(NT,),
        in_specs=[_row_spec(D), _row_spec(D), _full((D, D))] + _tail_specs(1),
        out_specs=_TAIL_OUT_SPECS,
        out_shape=_TAIL_OUT_SHAPES,
        compiler_params=_params("arbitrary"),
        name="cd_out_proj",
    )(x, merged, w_out, mod, mod, mod, *tail)


def _mix_down(wd_ref, wmix_ref):
    half = D_FF // 2
    for c in range(D // LANES):
        wmix_ref[c, pl.ds(0, half, stride=2), :] = wd_ref[0:half, c * LANES:(c + 1) * LANES]
        wmix_ref[c, pl.ds(1, half, stride=2), :] = wd_ref[half:D_FF, c * LANES:(c + 1) * LANES]


def _prep_expert(wgu_ref, wd_ref, wgu_bf, wmix_ref, wd_bf):
    for c in range(4):
        wgu_bf[:, c * 512:(c + 1) * 512] = wgu_ref[:, c * 512:(c + 1) * 512].astype(BF16)
    _mix_down(wd_ref, wmix_ref)
    for c in range(D // LANES):
        wd_bf[:, c * LANES:(c + 1) * LANES] = wmix_ref[c].astype(BF16)


def _expert_ffn(u, wgu_bf, bgu_ref, wd_bf, bd_ref):
    rows = u.shape[0]
    ga = _dot(u, wgu_bf[:, 0:D_FF]) + bgu_ref[:, 0:D_FF]
    gb = _dot(u, wgu_bf[:, D_FF:2 * D_FF]) + bgu_ref[:, D_FF:2 * D_FF]
    even = (lax.broadcasted_iota(jnp.int32, (rows, LANES), 1) % 2) == 0
    hid = []
    for c in range(D_FF // LANES):
        a = ga[:, c * LANES:(c + 1) * LANES]
        b = gb[:, c * LANES:(c + 1) * LANES]
        gate = jnp.where(even, a, pltpu.roll(b, 1, axis=1))
        up = jnp.where(even, pltpu.roll(a, LANES - 1, axis=1), b)
        gate = jnp.minimum(gate, SWIGLU_LIMIT)
        up = jnp.clip(up, -SWIGLU_LIMIT, SWIGLU_LIMIT)
        hid.append(((up + 1.0) * gate * jax.nn.sigmoid(SWIGLU_ALPHA * gate)).astype(BF16))
    hid = jnp.concatenate(hid, axis=1)
    return _dot(hid, wd_bf[...]) + bd_ref[...]


def _moe_plan(cnt):
    c = cnt[:, 0, :N_EXPERTS]
    counts = jnp.sum(c, axis=0)
    tiles_e = (counts + MOE_TM - 1) // MOE_TM
    tile_end = jnp.cumsum(tiles_e)
    first_row = (tile_end - tiles_e) * MOE_TM
    run_start = first_row[None] + jnp.cumsum(c, axis=0) - c
    j = jnp.arange(MOE_TILES, dtype=jnp.int32)
    tile_expert = jnp.minimum(jnp.sum((j[:, None] >= tile_end[None]).astype(jnp.int32), axis=1), N_EXPERTS - 1)
    i32 = lambda a: a.astype(jnp.int32)
    experts = jnp.arange(N_EXPERTS, dtype=jnp.int32)
    owns = tiles_e > 0
    seg = jnp.cumsum(owns.astype(jnp.int32)) - owns.astype(jnp.int32)
    later = jnp.where((experts[None] > experts[:, None]) & owns[None], experts[None], N_EXPERTS)
    nxt = jnp.min(later, axis=1)
    nxt = jnp.where(nxt == N_EXPERTS, -1, nxt)
    per_tile = lambda v: jnp.sum(jnp.where(tile_expert[:, None] == experts[None], v[None], 0), axis=1)
    return dict(tile_expert=i32(tile_expert), n_used=i32(tile_end[-1:]),
                seg_parity=i32(per_tile(seg % 2)), next_expert=i32(per_tile(nxt)),
                run_start=i32(run_start)[:, None, :], run_len=i32(c)[:, None, :],
                pad_start=i32(first_row + counts)[None, None, :], pad_len=i32(tiles_e * MOE_TM - counts)[None, None, :])


def _for_each_run(start_ref, len_ref, copy):
    def body(e, local):
        n = len_ref[0, e]
        g = start_ref[0, e]

        def pieces(sizes):
            for size in sizes:
                covered = jnp.bitwise_and(n, -2 * size)

                @pl.when(jnp.bitwise_and(n, size) != 0)
                def _(covered=covered, size=size):
                    copy(local + covered, g + covered, size)

        @pl.when(n >= RUN_SIZES[2])
        def _():
            pieces(RUN_SIZES[:3])

        pieces(RUN_SIZES[3:])
        return local + n
    lax.fori_loop(0, N_EXPERTS, body, jnp.int32(0))


def _rows(ref, start, size):
    start = 0 if isinstance(start, int) and start == 0 else pl.multiple_of(start * SUB, SUB)
    return ref.at[pl.ds(start, size * SUB), :]


def _from_tiles(ref, rows):
    return jnp.concatenate([ref[pl.ds(c, rows, stride=SUB), :] for c in range(SUB)], axis=1)


def _to_tiles(ref, x):
    for c in range(SUB):
        ref[pl.ds(c, x.shape[0], stride=SUB), :] = x[:, c * LANES:(c + 1) * LANES]


def _run_spec(index):
    return pl.BlockSpec((None, 1, N_EXPERTS), index, memory_space=pltpu.SMEM)


def _dispatch_kernel(rs_ref, rl_ref, ps_ref, pn_ref, u_ref, pos_ref, xs_hbm, stage, zeros, sem, zsem):
    i = pl.program_id(0)
    slot = i % 2

    def wait(s):
        pltpu.make_async_copy(stage.at[s], stage.at[s], sem.at[s]).wait()

    @pl.when(i == 0)
    def _():
        zeros[...] = jnp.zeros_like(zeros)

        def pad(local, g, size):
            return pltpu.make_async_copy(_rows(zeros, 0, size), _rows(xs_hbm, g, size), zsem.at[0])
        _for_each_run(ps_ref, pn_ref, lambda *a: pad(*a).start())
        _for_each_run(ps_ref, pn_ref, lambda *a: pad(*a).wait())

    @pl.when(i >= 2)
    def _():
        wait(slot)

    place = pos_ref[...].astype(F32).T
    s = lax.broadcasted_iota(jnp.int32, (PAIRS_TM, TM), 0).astype(F32)
    onehot = jnp.zeros((PAIRS_TM, TM), F32)
    for k in range(TOP_K):
        onehot = jnp.where(s == place[k:k + 1, :], 1.0, onehot)
    _to_tiles(stage.at[slot], _dot(onehot.astype(BF16), u_ref[...]))
    _for_each_run(rs_ref, rl_ref, lambda l, g, size: pltpu.make_async_copy(
        _rows(stage.at[slot], l, size), _rows(xs_hbm, g, size), sem.at[slot]).start())

    @pl.when(i == NT - 1)
    def _():
        wait(1 - slot)
        wait(slot)


def _dispatch(u2, pos, plan):
    return pl.pallas_call(
        _dispatch_kernel,
        grid=(NT,),
        in_specs=[_run_spec(lambda i: (i, 0, 0)), _run_spec(lambda i: (i, 0, 0)),
                  _run_spec(lambda i: (0, 0, 0)), _run_spec(lambda i: (0, 0, 0)),
                  _row_spec(D), _row_spec(LANES)],
        out_specs=pl.BlockSpec(memory_space=pl.ANY),
        out_shape=jax.ShapeDtypeStruct((MOE_TILES * MOE_TM * SUB, LANES), F32),
        scratch_shapes=[pltpu.VMEM((2, PAIRS_TM * SUB, LANES), F32), pltpu.VMEM((TM * SUB, LANES), F32),
                        pltpu.SemaphoreType.DMA((2,)), pltpu.SemaphoreType.DMA((1,))],
        compiler_params=_params("arbitrary"),
        name="moe_dispatch",
    )(plan["run_start"], plan["run_len"], plan["pad_start"], plan["pad_len"], u2, pos)


def _moe_kernel(te_ref, nu_ref, par_ref, nxt_ref, x_ref, bgu_ref, bd_ref, wgu_hbm, wd_hbm, y_ref,
                wgu_f, wd_f, wgu_bf, wmix_ref, wd_bf, sem, *, layer):
    j = pl.program_id(0)
    used = j < nu_ref[0]
    buf = par_ref[j]

    def copies(e, s):
        return (pltpu.make_async_copy(wgu_hbm.at[layer, e], wgu_f.at[s], sem.at[s]),
                pltpu.make_async_copy(wd_hbm.at[layer, e], wd_f.at[s], sem.at[s]))

    @pl.when(jnp.logical_and(used, j == 0))
    def _():
        for cp in copies(te_ref[0], 0):
            cp.start(priority=1)

    @pl.when(jnp.logical_and(used, jnp.logical_or(j == 0, te_ref[j] != te_ref[jnp.maximum(j - 1, 0)])))
    def _():
        for cp in copies(te_ref[j], buf):
            cp.wait()

        @pl.when(nxt_ref[j] >= 0)
        def _():
            for cp in copies(nxt_ref[j], 1 - buf):
                cp.start(priority=1)

        _prep_expert(wgu_f.at[buf], wd_f.at[buf], wgu_bf, wmix_ref, wd_bf)

    @pl.when(used)
    def _():
        x = _from_tiles(x_ref, MOE_TM).astype(BF16)
        _to_tiles(y_ref, _expert_ffn(x, wgu_bf, bgu_ref, wd_bf, bd_ref))


def _moe(layer, xs, plan, w_gate_up, b_gate_up, w_down, b_down):
    rows = pl.BlockSpec((MOE_TM * SUB, LANES), lambda j, te, nu, par, nxt: (jnp.minimum(j, nu[0] - 1), 0))
    grid_spec = pltpu.PrefetchScalarGridSpec(
        num_scalar_prefetch=4,
        grid=(MOE_TILES,),
        in_specs=[rows,
                  pl.BlockSpec((None, None, 1, 2 * D_FF), lambda j, te, nu, par, nxt: (layer, te[j], 0, 0)),
                  pl.BlockSpec((None, None, 1, D), lambda j, te, nu, par, nxt: (layer, te[j], 0, 0)),
                  pl.BlockSpec(memory_space=pl.ANY), pl.BlockSpec(memory_space=pl.ANY)],
        out_specs=rows,
        scratch_shapes=[pltpu.VMEM((2, D, 2 * D_FF), F32), pltpu.VMEM((2, D_FF, D), F32),
                        pltpu.VMEM((D, 2 * D_FF), BF16), pltpu.VMEM((D // LANES, D_FF, LANES), F32),
                        pltpu.VMEM((D_FF, D), BF16), pltpu.SemaphoreType.DMA((2,))])

    def kernel(*refs):
        _moe_kernel(*refs, layer=layer)

    return pl.pallas_call(
        kernel,
        grid_spec=grid_spec,
        out_shape=jax.ShapeDtypeStruct((MOE_TILES * MOE_TM * SUB, LANES), F32),
        compiler_params=_params("arbitrary"),
        name="moe_experts",
    )(plan["tile_expert"], plan["n_used"], plan["seg_parity"], plan["next_expert"], xs,
      b_gate_up.reshape(DEPTH, N_EXPERTS, 1, 2 * D_FF), b_down.reshape(DEPTH, N_EXPERTS, 1, D), w_gate_up, w_down)


def _combine_kernel(rs_ref, rl_ref, rsn_ref, rln_ref, x1_ref, pos_ref, w_ref, g2_ref, lng_ref, lnb_ref, ys_hbm,
                    *rest, final):
    out_refs, (stage, sem) = rest[:-2], rest[-2:]
    i = pl.program_id(0)
    slot = i % 2
    other = 1 - slot

    def fetch(start_ref, len_ref, s):
        _for_each_run(start_ref, len_ref, lambda l, g, size: pltpu.make_async_copy(
            _rows(ys_hbm, g, size), _rows(stage.at[s], l, size), sem.at[s]).start())

    def wait(s):
        pltpu.make_async_copy(stage.at[s], stage.at[s], sem.at[s]).wait()

    @pl.when(i == 0)
    def _():
        fetch(rs_ref, rl_ref, 0)

    fetch(rsn_ref, rln_ref, other)
    wait(slot)
    y = _from_tiles(stage.at[slot], PAIRS_TM).astype(BF16)
    place = pos_ref[...].astype(F32)
    w = w_ref[...]
    s = lax.broadcasted_iota(jnp.int32, (TM, PAIRS_TM), 1).astype(F32)
    pick = jnp.zeros((TM, PAIRS_TM), F32)
    for k in range(TOP_K):
        pick = jnp.where(s == place[:, k:k + 1], w[:, k:k + 1], pick)
    f = _dot(pick.astype(BF16), y)
    ci = _cond_row(i)
    z = DEEPNORM_ALPHA * x1_ref[...] + g2_ref[pl.ds(ci, 1), :] * f
    out = _layer_norm(z, lng_ref[...], lnb_ref[...])
    if final:
        @pl.when(i < NT_P)
        def _():
            out_refs[0][...] = out

        @pl.when(i >= NT_P)
        def _():
            out_refs[1][...] = out
    else:
        out_refs[0][...] = out

    @pl.when(i == NT - 1)
    def _():
        wait(other)


def _combine(layer, x1, ys, pos, w, plan, mod, g, b):
    final = layer == DEPTH - 1
    nxt = lambda i: (jnp.minimum(i + 1, NT - 1), 0, 0)
    cur = lambda i: (i, 0, 0)
    if final:
        out_specs = _x_specs()
        out_shape = [jax.ShapeDtypeStruct((T_P, D), F32), jax.ShapeDtypeStruct((T_S, D), F32)]
    else:
        out_specs = _row_spec(D)
        out_shape = jax.ShapeDtypeStruct((T, D), F32)

    def kernel(*refs):
        _combine_kernel(*refs, final=final)

    return pl.pallas_call(
        kernel,
        grid=(NT,),
        in_specs=[_run_spec(cur), _run_spec(cur), _run_spec(nxt), _run_spec(nxt),
                  _row_spec(D), _row_spec(LANES), _row_spec(LANES), _mod_spec(layer, 5), _full((1, D)), _full((1, D)),
                  pl.BlockSpec(memory_space=pl.ANY)],
        out_specs=out_specs,
        out_shape=out_shape,
        scratch_shapes=[pltpu.VMEM((2, PAIRS_TM * SUB, LANES), F32), pltpu.SemaphoreType.DMA((2,))],
        compiler_params=_params("arbitrary"),
        name="moe_combine_norm" + ("_final" if final else ""),
    )(plan["run_start"], plan["run_len"], plan["run_start"], plan["run_len"], x1, pos, w, mod, g, b, ys)


def _moe_layer(layer, routed, experts, mod, ln2_g, ln2_b):
    x1, u2, pos, w, cnt = routed
    plan = _moe_plan(cnt)
    ys = _moe(layer, _dispatch(u2, pos, plan), plan, *experts)
    return _combine(layer, x1, ys, pos, w, plan, mod, ln2_g[layer][None], ln2_b[layer][None])


def _router_tail(l, ln1_g, ln1_b, router_w, router_b):
    rw = jnp.pad(router_w[l], ((0, 0), (0, LANES - N_EXPERTS)))
    rb = jnp.pad(router_b[l], (0, LANES - N_EXPERTS), constant_values=NEG)
    return ln1_g[l][None], ln1_b[l][None], rw, rb[None]


def kernel(x_prompt, x_sample, c, c_ctx, cache_diff_k, cache_diff_v, cache_na_k, cache_na_v, cache_gqa_k, cache_gqa_v, w_mod, b_mod, ln1_g, ln1_b, ln2_g, ln2_b, ab_w_in, ab_conv_w, ab_lambda_q1, ab_lambda_k1, ab_lambda_q2, ab_lambda_k2, ab_subln_g, ab_w_out, cd_w_in, cd_na_rpb, cd_q_norm_g, cd_k_norm_g, cd_w_out, router_w, router_b, w_gate_up, b_gate_up, w_down, b_down):
    xp = x_prompt.reshape(T_P, D)
    xs = x_sample.reshape(T_S, D)
    cond8 = jnp.concatenate([c_ctx[None], c, jnp.zeros((8 - 1 - B_S, D), F32)], axis=0)
    mod = _modulation(cond8, w_mod, b_mod)
    rope = _rope_tables()
    experts = (w_gate_up, b_gate_up, w_down, b_down)

    gb, y, q, k, v, new_diff_k, new_diff_v = _ab_in(xp, xs, mod, ab_w_in[0].astype(BF16), rope)
    lam_init = 0.8 - 0.6 * 1.0
    diff = (jnp.stack([ab_lambda_q1[0], ab_lambda_k1[0], ab_lambda_q2[0], ab_lambda_k2[0]]), ab_subln_g[0][None])
    attn = _flash_pair(q, k, v, latent=False, out_cols=512, col0=0, post="diff", diff=diff, lam_init=lam_init)
    attn = _flash_pair(q, k, v, latent=True, out_cols=512, col0=0, post="diff", diff=diff, lam_init=lam_init,
                       ctx=(cache_diff_k, cache_diff_v), ctx_mode="wide", prev=attn)
    routed = _ab_out(xp, xs, gb, y, ab_conv_w[0], attn, ab_w_out[0].astype(BF16), mod,
                     _router_tail(0, ln1_g, ln1_b, router_w, router_b))
    x = _moe_layer(0, routed, experts, mod, ln2_g, ln2_b)

    qg = jnp.tile(cd_q_norm_g[0], 2)[None]
    kg = jnp.tile(cd_k_norm_g[0], 2)[None]
    nq, nk, nv, gq, gk, gv, new_na_k, new_na_v, new_gqa_k, new_gqa_v = _cd_in(
        x, mod, cd_w_in[0].astype(BF16), qg, kg, rope)
    merged = _flash_pair(nq, nk, nv, latent=False, out_cols=D, col0=0, post="select")
    merged = _flash_pair(gq, gk, gv, latent=False, out_cols=D, col0=2, post="select", prev=merged)
    merged = _na_latent(nq, nk, nv, cache_na_k, cache_na_v, _na_bias(cd_na_rpb[0]), merged)
    merged = _flash_pair(gq, gk, gv, latent=True, out_cols=D, col0=2, post="select",
                         ctx=(cache_gqa_k, cache_gqa_v), ctx_mode="dup", prev=merged)
    routed = _cd_out(x, merged, cd_w_out[0].astype(BF16), mod,
                     _router_tail(1, ln1_g, ln1_b, router_w, router_b))
    y_p, y_s = _moe_layer(1, routed, experts, mod, ln2_g, ln2_b)

    return (y_p.reshape(B_P, N_P, D), y_s.reshape(B_S, N_S, D), new_diff_k, new_diff_v,
            new_na_k, new_na_v, new_gqa_k, new_gqa_v)
```

```python
import jax
import jax.numpy as jnp
import numpy as np
from jax import lax
from jax.experimental import pallas as pl
from jax.experimental.pallas import tpu as pltpu

F32 = jnp.float32
BF16 = jnp.bfloat16

D = 1024
B_P, N_P = 16, 256
B_S, N_S = 2, 4096
PAST = 256
T_P, T_S = B_P * N_P, B_S * N_S
T = T_P + T_S
TM = 256
NT_P, NT_S, NT = T_P // TM, T_S // TM, T // TM
TILES_PER_GRID = N_S // TM
GRID_W = 64
GRID_H = N_S // GRID_W
HD = 64
DEPTH = 2
N_EXPERTS = 32
TOP_K = 4
D_FF = 1024
NA_WIN_R, NA_WIN_C = 8, 16
NA_QROWS = 4
NA_KROWS = 12
SWIGLU_LIMIT = 7.0
SWIGLU_ALPHA = 1.702
ROPE_THETA = 10000.0
DEEPNORM_ALPHA = (2 * DEPTH) ** 0.25
LN_EPS = 1e-5
RMS_EPS = 1e-6
LOG2E = 1.4426950408889634
QK_SCALE = HD ** -0.5 * LOG2E
NEG = -1e30
MOE_TM = 256
MOE_PAIRS = TOP_K * T
MOE_TILES = MOE_PAIRS // MOE_TM + N_EXPERTS
PAIRS_TM = TOP_K * TM
RUN_SIZES = tuple(TM >> b for b in range(TM.bit_length()))
LANES = 128
SUB = 8

VMEM_LIMIT = 56 * 1024 * 1024


def _params(*sem):
    return pltpu.CompilerParams(dimension_semantics=sem, vmem_limit_bytes=VMEM_LIMIT)


def _split(x):
    hi = x.astype(BF16)
    lo = (x - hi.astype(F32)).astype(BF16)
    return hi, lo


def _dot(a, b):
    return jnp.dot(a, b, preferred_element_type=F32)


def _dot3(a, b):
    ah, al = _split(a)
    bh, bl = _split(b)
    return _dot(ah, bh) + (_dot(ah, bl) + _dot(al, bh))


def _cond_row(i):
    return jnp.where(i < NT_P, 0, 1 + (i - NT_P) // TILES_PER_GRID)


def _layer_norm(z, g, b):
    mu = jnp.mean(z, axis=-1, keepdims=True)
    zc = z - mu
    var = jnp.mean(zc * zc, axis=-1, keepdims=True)
    return zc * lax.rsqrt(var + LN_EPS) * g + b


def _low_half(rows):
    return lax.broadcasted_iota(jnp.int32, (rows, LANES), 1) < HD


def _mod_kernel(c_ref, w_ref, b_ref, o_ref):
    c = c_ref[...]
    o_ref[...] = _dot3(c * jax.nn.sigmoid(c), w_ref[...]) + b_ref[...]


def _modulation(cond8, w_mod, b_mod):
    return pl.pallas_call(
        _mod_kernel,
        grid=(DEPTH, 6),
        in_specs=[pl.BlockSpec((8, D), lambda l, j: (0, 0)),
                  pl.BlockSpec((None, D, D), lambda l, j: (l, 0, j)),
                  pl.BlockSpec((None, 1, D), lambda l, j: (l, 0, j))],
        out_specs=pl.BlockSpec((None, 8, D), lambda l, j: (l, 0, j)),
        out_shape=jax.ShapeDtypeStruct((DEPTH, 8, 6 * D), F32),
        compiler_params=_params("arbitrary", "arbitrary"),
        name="modulation",
    )(cond8, w_mod, b_mod.reshape(DEPTH, 1, 6 * D))


def _mod_spec(layer, chunk):
    return pl.BlockSpec((None, 8, D), lambda i, _l=layer, _c=chunk: (_l, 0, _c))


def _full(shape):
    return pl.BlockSpec(shape, lambda i: (0,) * len(shape))


def _rope_tables():
    t = np.arange(N_S)
    half = HD // 2
    inv = ROPE_THETA ** (-np.arange(0, half, 2, dtype=np.float64) / half)
    inv_lane = np.tile(np.repeat(inv, 2), 2 * LANES // HD)
    lane = np.arange(LANES)
    by_row = (lane % HD) < half
    ang = np.where(by_row[None], (t // GRID_W)[:, None], (t % GRID_W)[:, None]) * inv_lane[None]
    cos, sin = np.cos(ang), np.sin(ang)
    even = (lane % 2) == 0
    return tuple(jnp.asarray(a, F32) for a in (cos, np.where(even, -sin, 0.0), np.where(even, 0.0, sin)))


def _rope(x, a, b, c):
    return x * a + pltpu.roll(x, LANES - 1, axis=1) * b + pltpu.roll(x, 1, axis=1) * c


def _rope_or_identity(identity, ra_ref, rb_ref, rc_ref):
    return (jnp.where(identity, 1.0, ra_ref[...]), jnp.where(identity, 0.0, rb_ref[...]),
            jnp.where(identity, 0.0, rc_ref[...]))


def _rope_spec():
    return pl.BlockSpec((TM, LANES), lambda i: (jnp.maximum(i - NT_P, 0) % TILES_PER_GRID, 0))


def _x_specs():
    return [pl.BlockSpec((TM, D), lambda i: (jnp.minimum(i, NT_P - 1), 0)),
            pl.BlockSpec((TM, D), lambda i: (jnp.maximum(i - NT_P, 0), 0))]


def _cache_spec(heads, width):
    return pl.BlockSpec((None, None, heads, N_P, width), lambda i: (jnp.minimum(i, NT_P - 1), 0, 0, 0, 0))


def _row_spec(width):
    return pl.BlockSpec((TM, width), lambda i: (i, 0))


def _hm_spec(n):
    return pl.BlockSpec((n, TM, LANES), lambda i: (0, i, 0))


AB_Q0, AB_K0, AB_V0 = 1536, 2048, 2560


def _tile(p, col0, j):
    return p[:, col0 + j * LANES:col0 + (j + 1) * LANES]


def _ab_in_kernel(xp_ref, xs_ref, sh_ref, sc_ref, w_ref, ra_ref, rb_ref, rc_ref,
                  gb_ref, y_ref, q_ref, k_ref, v_ref, kc_ref, vc_ref):
    i = pl.program_id(0)
    is_p = i < NT_P
    ci = _cond_row(i)
    x = jnp.where(is_p, xp_ref[...], xs_ref[...])
    u = x * (1.0 + sc_ref[pl.ds(ci, 1), :]) + sh_ref[pl.ds(ci, 1), :]
    p = _dot(u.astype(BF16), w_ref[...])
    gb_ref[...] = p[:, 0:512]
    y_ref[...] = p[:, 512:1024] * p[:, 1024:1536]
    a, b, c = _rope_or_identity(is_p, ra_ref, rb_ref, rc_ref)
    for h in range(4):
        v_ref[h] = _tile(p, AB_V0, h).astype(BF16)
        q_ref[h] = (_rope(_tile(p, AB_Q0, h), a, b, c) * QK_SCALE).astype(BF16)
        k_ref[h] = _rope(_tile(p, AB_K0, h), a, b, c).astype(BF16)

    @pl.when(is_p)
    def _():
        for h in range(4):
            kc_ref[h] = _tile(p, AB_K0, h)
            vc_ref[h] = _tile(p, AB_V0, h)


def _ab_in(xp, xs, mod, w_in, rope):
    hm = jax.ShapeDtypeStruct((4, T, LANES), BF16)
    cache = jax.ShapeDtypeStruct((B_P, 1, 4, N_P, LANES), F32)
    half = jax.ShapeDtypeStruct((T, 512), F32)
    return pl.pallas_call(
        _ab_in_kernel,
        grid=(NT,),
        in_specs=_x_specs() + [_mod_spec(0, 0), _mod_spec(0, 1), _full((D, 3072)),
                               _rope_spec(), _rope_spec(), _rope_spec()],
        out_specs=[_row_spec(512), _row_spec(512), _hm_spec(4), _hm_spec(4), _hm_spec(4),
                   _cache_spec(4, LANES), _cache_spec(4, LANES)],
        out_shape=[half, half, hm, hm, hm, cache, cache],
        compiler_params=_params("arbitrary"),
        name="ab_in_proj",
    )(xp, xs, mod, mod, w_in, *rope)


CD_NQ, CD_NK, CD_NV, CD_GQ, CD_GK, CD_GV = 0, 512, 1024, 1536, 2048, 2176


def _seg_mean64(s):
    r = lax.broadcasted_iota(jnp.int32, (LANES, LANES), 0) // HD
    c = lax.broadcasted_iota(jnp.int32, (LANES, LANES), 1) // HD
    seg = jnp.where(r == c, 1.0, 0.0).astype(BF16)
    hi, lo = _split(s)
    return (_dot(hi, seg) + _dot(lo, seg)) * (1.0 / HD)


def _rms64(x, g):
    return x * lax.rsqrt(_seg_mean64(x * x) + RMS_EPS) * g


def _dup_halves(x, lo):
    r = pltpu.roll(x, HD, axis=1)
    return jnp.where(lo, x, r), jnp.where(lo, r, x)


def _cd_in_kernel(x_ref, sh_ref, sc_ref, w_ref, qg_ref, kg_ref, ra_ref, rb_ref, rc_ref,
                  nq_ref, nk_ref, nv_ref, gq_ref, gk_ref, gv_ref,
                  nkc_ref, nvc_ref, gkc_ref, gvc_ref):
    i = pl.program_id(0)
    is_p = i < NT_P
    ci = _cond_row(i)
    u = x_ref[...] * (1.0 + sc_ref[pl.ds(ci, 1), :]) + sh_ref[pl.ds(ci, 1), :]
    p = _dot(u.astype(BF16), w_ref[...])
    lo = _low_half(TM)
    for j in range(4):
        nq_ref[j] = (_tile(p, CD_NQ, j) * QK_SCALE).astype(BF16)
        nk_ref[j] = _tile(p, CD_NK, j).astype(BF16)
        nv_ref[j] = _tile(p, CD_NV, j).astype(BF16)
    gq = [_rms64(_tile(p, CD_GQ, j), qg_ref[...]) for j in range(4)]
    gk = _rms64(_tile(p, CD_GK, 0), kg_ref[...])
    gv = _tile(p, CD_GV, 0)
    v0, v1 = _dup_halves(gv, lo)
    gv_ref[0] = v0.astype(BF16)
    gv_ref[1] = v1.astype(BF16)
    a, b, c = _rope_or_identity(is_p, ra_ref, rb_ref, rc_ref)
    for j in range(4):
        gq_ref[j] = (_rope(gq[j], a, b, c) * QK_SCALE).astype(BF16)
    k0, k1 = _dup_halves(_rope(gk, a, b, c), lo)
    gk_ref[0] = k0.astype(BF16)
    gk_ref[1] = k1.astype(BF16)

    @pl.when(is_p)
    def _():
        gkc_ref[0] = k0[:, 0:HD]
        gkc_ref[1] = k1[:, 0:HD]
        gvc_ref[0] = v0[:, 0:HD]
        gvc_ref[1] = v1[:, 0:HD]
        for j in range(4):
            for src, dst in ((CD_NK, nkc_ref), (CD_NV, nvc_ref)):
                a, b = _dup_halves(_tile(p, src, j), lo)
                dst[2 * j] = a[:, 0:HD]
                dst[2 * j + 1] = b[:, 0:HD]


def _cd_in(x, mod, w_in, qg, kg, rope):
    hm4 = jax.ShapeDtypeStruct((4, T, LANES), BF16)
    hm2 = jax.ShapeDtypeStruct((2, T, LANES), BF16)
    c8 = jax.ShapeDtypeStruct((B_P, 1, 8, N_P, HD), F32)
    c2 = jax.ShapeDtypeStruct((B_P, 1, 2, N_P, HD), F32)
    return pl.pallas_call(
        _cd_in_kernel,
        grid=(NT,),
        in_specs=[_row_spec(D), _mod_spec(1, 0), _mod_spec(1, 1), _full((D, 2304)),
                  _full((1, LANES)), _full((1, LANES)), _rope_spec(), _rope_spec(), _rope_spec()],
        out_specs=[_hm_spec(4), _hm_spec(4), _hm_spec(4), _hm_spec(4), _hm_spec(2), _hm_spec(2),
                   _cache_spec(8, HD), _cache_spec(8, HD), _cache_spec(2, HD), _cache_spec(2, HD)],
        out_shape=[hm4, hm4, hm4, hm4, hm2, hm2, c8, c8, c2, c2],
        compiler_params=_params("arbitrary"),
        name="cd_in_proj",
    )(x, mod, mod, w_in, qg, kg, *rope)


def _stack_pairs(q_ref, n_q, tq):
    lo = _low_half(tq)
    parts = []
    for j in range(n_q):
        q = q_ref[j]
        zero = jnp.zeros_like(q)
        parts += [jnp.where(lo, q, zero), jnp.where(lo, zero, q)]
    return jnp.concatenate(parts, axis=0), lo


def _qk(qs, kb):
    return lax.dot_general(qs, kb, (((1,), (1,)), ((), ())), preferred_element_type=F32)


def _ctx_tile(ref, mode):
    if mode == "wide":
        x = ref[...]
    elif mode == "pair":
        x = jnp.concatenate([ref[0], ref[1]], axis=1)
    else:
        x = jnp.concatenate([ref[...], ref[...]], axis=1)
    return x.astype(BF16)


def _with_ones(v):
    lane = lax.broadcasted_iota(jnp.int32, v.shape, 1)
    return jnp.concatenate([v, jnp.where(lane == 0, 1.0, 0.0).astype(v.dtype)], axis=1)


def _flash_pair_kernel(*refs, gs, ctx_mode, **kw):
    it = iter(refs)
    q_ref, k_ref, v_ref = next(it), next(it), next(it)
    ctx = [next(it), next(it)] if ctx_mode else []
    rest = list(it)
    o_ref = rest.pop()
    n_q = q_ref.shape[0] // gs
    for gi in range(gs):
        _flash_group(q_ref.at[gi * n_q:(gi + 1) * n_q], k_ref.at[gi], v_ref.at[gi], *[r.at[gi] for r in ctx], *rest,
                     o_ref.at[:, gi * n_q * LANES:(gi + 1) * n_q * LANES], n_q=n_q, ctx_mode=ctx_mode, **kw)


def _flash_group(*refs, n_q, tq, nk, tk, ctx_mode, post, lam_init):
    it = iter(refs)
    q_ref, k_ref, v_ref = next(it), next(it), next(it)
    kc_ref, vc_ref = (next(it), next(it)) if ctx_mode else (None, None)
    lam_ref, g_ref = (next(it), next(it)) if post == "diff" else (None, None)
    o_ref = next(it)

    qs, lo = _stack_pairs(q_ref, n_q, tq)
    rows = 2 * n_q * tq

    def step(kb, vb, carry):
        m, acc = carry
        s = _qk(qs, kb)
        m_new = jnp.maximum(m, jnp.max(s, axis=1, keepdims=True))
        p = jnp.exp2((s - m_new).astype(BF16))
        return m_new, jnp.exp2(m - m_new) * acc + _dot(p, _with_ones(vb))

    carry = (jnp.full((rows, 1), NEG, F32), jnp.zeros((rows, 2 * LANES), F32))
    for c in range(nk // tk):
        carry = step(k_ref[c * tk:(c + 1) * tk, :], v_ref[c * tk:(c + 1) * tk, :], carry)
    if ctx_mode:
        carry = step(_ctx_tile(kc_ref, ctx_mode), _ctx_tile(vc_ref, ctx_mode), carry)
    _, acc = carry
    o = acc[:, 0:LANES] / acc[:, LANES:LANES + 1]

    if post == "diff":
        lp = lam_ref[...]
        lam = (jnp.exp(jnp.sum(lp[0:1] * lp[1:2], axis=1, keepdims=True))
               - jnp.exp(jnp.sum(lp[2:3] * lp[3:4], axis=1, keepdims=True)) + lam_init)
        a = o[0:tq] - lam * o[tq:2 * tq]
        ms = jnp.mean(a * a, axis=-1, keepdims=True)
        o_ref[...] = (a * lax.rsqrt(ms + RMS_EPS) * g_ref[...] * (1.0 - lam_init)).astype(o_ref.dtype)
    else:
        for j in range(n_q):
            o_ref[:, j * LANES:(j + 1) * LANES] = jnp.where(
                lo, o[2 * j * tq:(2 * j + 1) * tq], o[(2 * j + 1) * tq:(2 * j + 2) * tq]).astype(o_ref.dtype)


def _flash_pair(q, k, v, *, latent, out_cols, col0, post, ctx=None, ctx_mode=None, diff=None,
                lam_init=0.0, prev=None):
    groups = k.shape[0]
    n_q = q.shape[0] // groups
    if latent:
        tq, nk, tk, gs = TM, N_S, 1024, 2 // n_q
        grid = (B_S, groups // gs, TILES_PER_GRID)
        qrow = lambda b, g, i: NT_P + b * TILES_PER_GRID + i
        krow = lambda b, g, i: T_P // N_S + b
    else:
        tq, nk, tk, gs = N_P, N_P, N_P, groups
        grid = (B_P, 1, 1)
        qrow = lambda b, g, i: b
        krow = lambda b, g, i: b
    in_specs = [pl.BlockSpec((gs * n_q, tq, LANES), lambda b, g, i: (g, qrow(b, g, i), 0)),
                pl.BlockSpec((gs, nk, LANES), lambda b, g, i: (g, krow(b, g, i), 0)),
                pl.BlockSpec((gs, nk, LANES), lambda b, g, i: (g, krow(b, g, i), 0))]
    args = [q, k, v]
    if ctx is not None:
        width = LANES if ctx_mode == "wide" else HD
        spec = pl.BlockSpec((None, None, gs, PAST, width), lambda b, g, i: (b, 0, g, 0, 0))
        in_specs += [spec, spec]
        args += list(ctx)
    if diff is not None:
        in_specs += [pl.BlockSpec((4, HD), lambda b, g, i: (0, 0)),
                     pl.BlockSpec((1, LANES), lambda b, g, i: (0, 0))]
        args += list(diff)
    aliases = {}
    if prev is not None:
        aliases = {len(args): 0}
        in_specs.append(pl.BlockSpec(memory_space=pl.ANY))
        args.append(prev)

    def kernel(*refs):
        if prev is not None:
            refs = refs[:-2] + refs[-1:]
        _flash_pair_kernel(*refs, gs=gs, tq=tq, nk=nk, tk=tk, ctx_mode=ctx_mode if ctx is not None else None,
                           post=post, lam_init=lam_init)

    return pl.pallas_call(
        kernel,
        grid=grid,
        in_specs=in_specs,
        out_specs=pl.BlockSpec((tq, gs * n_q * LANES), lambda b, g, i: (qrow(b, g, i), col0 // gs + g)),
        out_shape=jax.ShapeDtypeStruct((T, out_cols), BF16),
        input_output_aliases=aliases,
        compiler_params=_params("arbitrary", "arbitrary", "arbitrary"),
        name="attn_" + post + ("_latent" if latent else "_context") + str(n_q),
    )(*args)


NA_TQ = NA_QROWS * GRID_W
NA_TK = NA_KROWS * GRID_W


NA_BLOCK_POS = ((0, 0), (NA_QROWS, 0), (GRID_H - NA_QROWS, GRID_H - NA_KROWS))
N_DR = 2 * NA_WIN_R - 1
N_DC = 2 * NA_WIN_C - 1


def _na_bias_kernel(rpb_ref, o_ref):
    qc = lax.broadcasted_iota(jnp.int32, (GRID_W, LANES), 0)
    lane = lax.broadcasted_iota(jnp.int32, (GRID_W, LANES), 1)
    kc = lane % GRID_W
    cs = jnp.clip(qc - NA_WIN_C // 2, 0, GRID_W - NA_WIN_C)
    col_ok = jnp.logical_and(kc >= cs, kc < cs + NA_WIN_C)
    lo = lane < GRID_W
    neg = jnp.full((GRID_W, LANES), NEG, F32)
    for head in range(2):
        toeplitz = []
        for dr in range(N_DR):
            r = jnp.broadcast_to(rpb_ref[head, dr:dr + 1, :] * LOG2E, (GRID_W, LANES))
            t = jnp.where(lo, pltpu.roll(r, LANES - (NA_WIN_C - 1), axis=1, stride=1, stride_axis=0),
                          pltpu.roll(r, GRID_W - (NA_WIN_C - 1), axis=1, stride=1, stride_axis=0))
            toeplitz.append(jnp.where(col_ok, t, neg))
        for pos, (r0, k0) in enumerate(NA_BLOCK_POS):
            for i in range(NA_QROWS):
                qr = r0 + i
                rs = min(max(qr - NA_WIN_R // 2, 0), GRID_H - NA_WIN_R)
                tiles = [toeplitz[k0 + j - qr + NA_WIN_R - 1] if rs <= k0 + j < rs + NA_WIN_R else neg
                         for j in range(NA_KROWS)]
                for jp in range(NA_KROWS // 2):
                    o_ref[pos, pl.ds(head * NA_TQ + i * GRID_W, GRID_W), pl.ds(jp * LANES, LANES)] = jnp.where(
                        lo, tiles[2 * jp], tiles[2 * jp + 1])


def _na_bias(rpb):
    rpb_pad = jnp.pad(rpb, ((0, 0), (0, 16 - N_DR), (0, LANES - N_DC)), constant_values=NEG)
    return pl.pallas_call(
        _na_bias_kernel,
        grid=(4,),
        in_specs=[pl.BlockSpec((2, 16, LANES), lambda g: (g, 0, 0))],
        out_specs=pl.BlockSpec((3, None, 2 * NA_TQ, NA_TK), lambda g: (0, g, 0, 0)),
        out_shape=jax.ShapeDtypeStruct((3, 4, 2 * NA_TQ, NA_TK), F32),
        compiler_params=_params("arbitrary"),
        name="na_bias_table",
    )(rpb_pad)


NA_PAIRS = 4


def _na_kernel(q_ref, k_ref, v_ref, kc_ref, vc_ref, bm_ref, _, o_ref):
    i = pl.program_id(2)
    k0 = jnp.clip(i * NA_QROWS - NA_WIN_R // 2, 0, GRID_H - NA_KROWS)
    start = pl.multiple_of(k0 * GRID_W, GRID_W)
    for g in range(NA_PAIRS):
        qs, lo = _stack_pairs(q_ref.at[g:g + 1], 1, NA_TQ)
        kw = k_ref[g, pl.ds(start, NA_TK), :]
        vw = v_ref[g, pl.ds(start, NA_TK), :]
        s_w = _qk(qs, kw) + bm_ref[g]
        s_c = _qk(qs, _ctx_tile(kc_ref.at[2 * g:2 * g + 2], "pair"))
        m = jnp.maximum(jnp.max(s_w, axis=1, keepdims=True), jnp.max(s_c, axis=1, keepdims=True))
        p_w = jnp.exp2((s_w - m).astype(BF16))
        p_c = jnp.exp2((s_c - m).astype(BF16))
        acc = _dot(p_w, _with_ones(vw)) + _dot(p_c, _with_ones(_ctx_tile(vc_ref.at[2 * g:2 * g + 2], "pair")))
        o = acc[:, 0:LANES] / acc[:, LANES:LANES + 1]
        o_ref[:, g * LANES:(g + 1) * LANES] = jnp.where(lo, o[0:NA_TQ], o[NA_TQ:2 * NA_TQ]).astype(o_ref.dtype)


def _na_latent(q, k, v, kc, vc, bm, prev):
    nblk = N_S // NA_TQ
    qrow = lambda b, g, i: T_P // NA_TQ + b * nblk + i
    krow = lambda b, g, i: T_P // N_S + b
    cfg = lambda i: jnp.where(i == 0, 0, jnp.where(i == nblk - 1, 2, 1))
    ctx_spec = pl.BlockSpec((None, None, 2 * NA_PAIRS, PAST, HD), lambda b, g, i: (b, 0, g, 0, 0))
    return pl.pallas_call(
        _na_kernel,
        grid=(B_S, 4 // NA_PAIRS, nblk),
        in_specs=[pl.BlockSpec((NA_PAIRS, NA_TQ, LANES), lambda b, g, i: (g, qrow(b, g, i), 0)),
                  pl.BlockSpec((NA_PAIRS, N_S, LANES), lambda b, g, i: (g, krow(b, g, i), 0)),
                  pl.BlockSpec((NA_PAIRS, N_S, LANES), lambda b, g, i: (g, krow(b, g, i), 0)),
                  ctx_spec, ctx_spec,
                  pl.BlockSpec((None, NA_PAIRS, 2 * NA_TQ, NA_TK), lambda b, g, i: (cfg(i), g, 0, 0)),
                  pl.BlockSpec(memory_space=pl.ANY)],
        out_specs=pl.BlockSpec((NA_TQ, NA_PAIRS * LANES), lambda b, g, i: (qrow(b, g, i), g)),
        out_shape=jax.ShapeDtypeStruct((T, D), BF16),
        input_output_aliases={6: 0},
        compiler_params=_params("arbitrary", "arbitrary", "arbitrary"),
        name="attn_window_latent",
    )(q, k, v, kc, vc, bm, prev)


def _top4(logits):
    lane = lax.broadcasted_iota(jnp.int32, logits.shape, 1).astype(F32)
    rest = logits
    tops, firsts = [], []
    for _ in range(TOP_K):
        m = jnp.max(rest, axis=1, keepdims=True)
        first = jnp.min(jnp.where(rest == m, lane, float(LANES)), axis=1, keepdims=True)
        tops.append(m)
        firsts.append(first)
        rest = jnp.where(lane == first, -jnp.inf, rest)
    es = [jnp.exp(m - tops[0]) for m in tops]
    denom = es[0] + es[1] + es[2] + es[3]
    w = jnp.zeros_like(logits)
    for k in range(TOP_K):
        w = jnp.where(lane == float(k), es[k] / denom, w)
    return firsts, w


def _local_sort(firsts):
    lane = lax.broadcasted_iota(jnp.int32, (TM, LANES), 1).astype(F32)
    hots = [lane == f for f in firsts]
    sel = jnp.zeros((TM, LANES), F32)
    for hot in hots:
        sel = jnp.where(hot, 1.0, sel)
    r = lax.broadcasted_iota(jnp.int32, (TM, TM), 0)
    c = lax.broadcasted_iota(jnp.int32, (TM, TM), 1)
    earlier = _dot(jnp.where(c < r, 1.0, 0.0).astype(BF16), sel.astype(BF16))
    cnt = jnp.sum(sel, axis=0, keepdims=True)
    r = lax.broadcasted_iota(jnp.int32, (LANES, LANES), 0)
    c = lax.broadcasted_iota(jnp.int32, (LANES, LANES), 1)
    start = _dot(jnp.broadcast_to(cnt, (SUB, LANES)).astype(BF16), jnp.where(r < c, 1.0, 0.0).astype(BF16))[0:1]
    place = start + earlier
    pos = jnp.zeros((TM, LANES), F32)
    for k, hot in enumerate(hots):
        pos = jnp.where(lane == float(k), jnp.sum(jnp.where(hot, place, 0.0), axis=1, keepdims=True), pos)
    return pos.astype(jnp.int32), cnt.astype(jnp.int32)


def _mixer_tail(x_of, h_of, ci, g1_ref, sh2_ref, sc2_ref, lng_ref, lnb_ref, rw_ref, rb_ref,
                x1_ref, u2_ref, pos_ref, w_ref, cnt_ref):
    rows = slice(0, TM)
    x1 = _layer_norm(DEEPNORM_ALPHA * x_of(rows) + g1_ref[pl.ds(ci, 1), :] * h_of(rows), lng_ref[...], lnb_ref[...])
    x1_ref[...] = x1
    u2 = x1 * (1.0 + sc2_ref[pl.ds(ci, 1), :]) + sh2_ref[pl.ds(ci, 1), :]
    u2_ref[...] = u2.astype(BF16)
    firsts, w_ref[...] = _top4(_dot3(u2, rw_ref[...]) + rb_ref[...])
    pos, cnt = _local_sort(firsts)
    pos_ref[...] = pos
    cnt_ref[...] = jnp.broadcast_to(cnt, (SUB, LANES))


def _ab_out_kernel(xp_ref, xs_ref, gb_ref, y_ref, yp_ref, yn_ref, cv_ref, at_ref, wo_ref,
                   g1_ref, sh2_ref, sc2_ref, lng_ref, lnb_ref, rw_ref, rb_ref, *out_refs):
    i = pl.program_id(0)
    is_p = i < NT_P
    ci = _cond_row(i)
    j = (i - NT_P) % TILES_PER_GRID
    first = jnp.logical_or(is_p, j == 0)
    last = jnp.logical_or(is_p, j == TILES_PER_GRID - 1)
    y = y_ref[...]
    row = lax.broadcasted_iota(jnp.int32, y.shape, 0)
    before = jnp.where(first, 0.0, yp_ref[7:8, :])
    after = jnp.where(last, 0.0, yn_ref[0:1, :])
    y_prev = jnp.where(row == 0, before, pltpu.roll(y, 1, axis=0))
    y_next = jnp.where(row == TM - 1, after, pltpu.roll(y, TM - 1, axis=0))
    cv = cv_ref[...]
    conv = (gb_ref[...] * (y_prev * cv[0:1] + y * cv[1:2] + y_next * cv[2:3])).astype(BF16)
    _mixer_tail(lambda r: jnp.where(is_p, xp_ref[r, :], xs_ref[r, :]),
                lambda r: _dot(conv[r], wo_ref[0:512, :]) + _dot(at_ref[r, :], wo_ref[512:1024, :]),
                ci, g1_ref, sh2_ref, sc2_ref, lng_ref, lnb_ref, rw_ref, rb_ref, *out_refs)


def _cd_out_kernel(x_ref, mg_ref, wo_ref, g1_ref, sh2_ref, sc2_ref, lng_ref, lnb_ref, rw_ref, rb_ref,
                   *out_refs):
    ci = _cond_row(pl.program_id(0))
    _mixer_tail(lambda r: x_ref[r, :], lambda r: _dot(mg_ref[r, :], wo_ref[...]),
                ci, g1_ref, sh2_ref, sc2_ref, lng_ref, lnb_ref, rw_ref, rb_ref, *out_refs)


def _tail_specs(layer):
    return [_mod_spec(layer, 2), _mod_spec(layer, 3), _mod_spec(layer, 4),
            _full((1, D)), _full((1, D)), _full((D, LANES)), _full((1, LANES))]


_TAIL_OUT_SPECS = [_row_spec(D), _row_spec(D), _row_spec(LANES), _row_spec(LANES),
                   pl.BlockSpec((None, SUB, LANES), lambda i: (i, 0, 0))]
_TAIL_OUT_SHAPES = [jax.ShapeDtypeStruct((T, D), F32), jax.ShapeDtypeStruct((T, D), BF16),
                    jax.ShapeDtypeStruct((T, LANES), jnp.int32), jax.ShapeDtypeStruct((T, LANES), F32),
                    jax.ShapeDtypeStruct((NT, SUB, LANES), jnp.int32)]


def _ab_out(xp, xs, gb, y, conv_w, attn, w_out, mod, tail):
    halo_prev = pl.BlockSpec((8, 512), lambda i: (jnp.maximum(i * (TM // 8) - 1, 0), 0))
    halo_next = pl.BlockSpec((8, 512), lambda i: (jnp.minimum((i + 1) * (TM // 8), T // 8 - 1), 0))
    return pl.pallas_call(
        _ab_out_kernel,
        grid=(NT,),
        in_specs=_x_specs() + [_row_spec(512), _row_spec(512), halo_prev, halo_next, _full((3, 512)),
                               _row_spec(512), _full((D, D))] + _tail_specs(0),
        out_specs=_TAIL_OUT_SPECS,
        out_shape=_TAIL_OUT_SHAPES,
        compiler_params=_params("arbitrary"),
        name="ab_out_proj",
    )(xp, xs, gb, y, y, y, conv_w, attn, w_out, mod, mod, mod, *tail)


def _cd_out(x, merged, w_out, mod, tail):
    return pl.pallas_call(
        _cd_out_kernel,
        grid=(NT,),
        in_specs=[_row_spec(D), _row_spec(D), _full((D, D))] + _tail_specs(1),
        out_specs=_TAIL_OUT_SPECS,
        out_shape=_TAIL_OUT_SHAPES,
        compiler_params=_params("arbitrary"),
        name="cd_out_proj",
    )(x, merged, w_out, mod, mod, mod, *tail)


def _mix_down(wd_ref, wmix_ref):
    half = D_FF // 2
    for c in range(D // LANES):
        wmix_ref[c, pl.ds(0, half, stride=2), :] = wd_ref[0:half, c * LANES:(c + 1) * LANES]
        wmix_ref[c, pl.ds(1, half, stride=2), :] = wd_ref[half:D_FF, c * LANES:(c + 1) * LANES]


def _prep_expert(wgu_ref, wd_ref, wgu_bf, wmix_ref, wd_bf):
    for c in range(4):
        wgu_bf[:, c * 512:(c + 1) * 512] = wgu_ref[:, c * 512:(c + 1) * 512].astype(BF16)
    _mix_down(wd_ref, wmix_ref)
    for c in range(D // LANES):
        wd_bf[:, c * LANES:(c + 1) * LANES] = wmix_ref[c].astype(BF16)


def _expert_ffn(u, wgu_bf, bgu_ref, wd_bf, bd_ref):
    rows = u.shape[0]
    ga = _dot(u, wgu_bf[:, 0:D_FF]) + bgu_ref[:, 0:D_FF]
    gb = _dot(u, wgu_bf[:, D_FF:2 * D_FF]) + bgu_ref[:, D_FF:2 * D_FF]
    even = (lax.broadcasted_iota(jnp.int32, (rows, LANES), 1) % 2) == 0
    hid = []
    for c in range(D_FF // LANES):
        a = ga[:, c * LANES:(c + 1) * LANES]
        b = gb[:, c * LANES:(c + 1) * LANES]
        gate = jnp.where(even, a, pltpu.roll(b, 1, axis=1))
        up = jnp.where(even, pltpu.roll(a, LANES - 1, axis=1), b)
        gate = jnp.minimum(gate, SWIGLU_LIMIT)
        up = jnp.clip(up, -SWIGLU_LIMIT, SWIGLU_LIMIT)
        hid.append(((up + 1.0) * gate * jax.nn.sigmoid(SWIGLU_ALPHA * gate)).astype(BF16))
    hid = jnp.concatenate(hid, axis=1)
    return _dot(hid, wd_bf[...]) + bd_ref[...]


def _moe_plan(cnt):
    c = cnt[:, 0, :N_EXPERTS]
    counts = jnp.sum(c, axis=0)
    tiles_e = (counts + MOE_TM - 1) // MOE_TM
    tile_end = jnp.cumsum(tiles_e)
    first_row = (tile_end - tiles_e) * MOE_TM
    run_start = first_row[None] + jnp.cumsum(c, axis=0) - c
    j = jnp.arange(MOE_TILES, dtype=jnp.int32)
    tile_expert = jnp.minimum(jnp.sum((j[:, None] >= tile_end[None]).astype(jnp.int32), axis=1), N_EXPERTS - 1)
    i32 = lambda a: a.astype(jnp.int32)
    experts = jnp.arange(N_EXPERTS, dtype=jnp.int32)
    owns = tiles_e > 0
    seg = jnp.cumsum(owns.astype(jnp.int32)) - owns.astype(jnp.int32)
    later = jnp.where((experts[None] > experts[:, None]) & owns[None], experts[None], N_EXPERTS)
    nxt = jnp.min(later, axis=1)
    nxt = jnp.where(nxt == N_EXPERTS, -1, nxt)
    per_tile = lambda v: jnp.sum(jnp.where(tile_expert[:, None] == experts[None], v[None], 0), axis=1)
    return dict(tile_expert=i32(tile_expert), n_used=i32(tile_end[-1:]),
                seg_parity=i32(per_tile(seg % 2)), next_expert=i32(per_tile(nxt)),
                run_start=i32(run_start)[:, None, :], run_len=i32(c)[:, None, :],
                pad_start=i32(first_row + counts)[None, None, :], pad_len=i32(tiles_e * MOE_TM - counts)[None, None, :])


def _for_each_run(start_ref, len_ref, copy):
    def body(e, local):
        n = len_ref[0, e]
        g = start_ref[0, e]

        def pieces(sizes):
            for size in sizes:
                covered = jnp.bitwise_and(n, -2 * size)

                @pl.when(jnp.bitwise_and(n, size) != 0)
                def _(covered=covered, size=size):
                    copy(local + covered, g + covered, size)

        @pl.when(n >= RUN_SIZES[2])
        def _():
            pieces(RUN_SIZES[:3])

        pieces(RUN_SIZES[3:])
        return local + n
    lax.fori_loop(0, N_EXPERTS, body, jnp.int32(0))


def _rows(ref, start, size):
    start = 0 if isinstance(start, int) and start == 0 else pl.multiple_of(start * SUB, SUB)
    return ref.at[pl.ds(start, size * SUB), :]


def _from_tiles(ref, rows):
    return jnp.concatenate([ref[pl.ds(c, rows, stride=SUB), :] for c in range(SUB)], axis=1)


def _to_tiles(ref, x):
    for c in range(SUB):
        ref[pl.ds(c, x.shape[0], stride=SUB), :] = x[:, c * LANES:(c + 1) * LANES]


def _run_spec(index):
    return pl.BlockSpec((None, 1, N_EXPERTS), index, memory_space=pltpu.SMEM)


def _dispatch_kernel(rs_ref, rl_ref, ps_ref, pn_ref, u_ref, pos_ref, xs_hbm, stage, zeros, sem, zsem):
    i = pl.program_id(0)
    slot = i % 2

    def wait(s):
        pltpu.make_async_copy(stage.at[s], stage.at[s], sem.at[s]).wait()

    @pl.when(i == 0)
    def _():
        zeros[...] = jnp.zeros_like(zeros)

        def pad(local, g, size):
            return pltpu.make_async_copy(_rows(zeros, 0, size), _rows(xs_hbm, g, size), zsem.at[0])
        _for_each_run(ps_ref, pn_ref, lambda *a: pad(*a).start())
        _for_each_run(ps_ref, pn_ref, lambda *a: pad(*a).wait())

    @pl.when(i >= 2)
    def _():
        wait(slot)

    place = pos_ref[...].astype(F32).T
    s = lax.broadcasted_iota(jnp.int32, (PAIRS_TM, TM), 0).astype(F32)
    onehot = jnp.zeros((PAIRS_TM, TM), F32)
    for k in range(TOP_K):
        onehot = jnp.where(s == place[k:k + 1, :], 1.0, onehot)
    _to_tiles(stage.at[slot], _dot(onehot.astype(BF16), u_ref[...]))
    _for_each_run(rs_ref, rl_ref, lambda l, g, size: pltpu.make_async_copy(
        _rows(stage.at[slot], l, size), _rows(xs_hbm, g, size), sem.at[slot]).start())

    @pl.when(i == NT - 1)
    def _():
        wait(1 - slot)
        wait(slot)


def _dispatch(u2, pos, plan):
    return pl.pallas_call(
        _dispatch_kernel,
        grid=(NT,),
        in_specs=[_run_spec(lambda i: (i, 0, 0)), _run_spec(lambda i: (i, 0, 0)),
                  _run_spec(lambda i: (0, 0, 0)), _run_spec(lambda i: (0, 0, 0)),
                  _row_spec(D), _row_spec(LANES)],
        out_specs=pl.BlockSpec(memory_space=pl.ANY),
        out_shape=jax.ShapeDtypeStruct((MOE_TILES * MOE_TM * SUB, LANES), F32),
        scratch_shapes=[pltpu.VMEM((2, PAIRS_TM * SUB, LANES), F32), pltpu.VMEM((TM * SUB, LANES), F32),
                        pltpu.SemaphoreType.DMA((2,)), pltpu.SemaphoreType.DMA((1,))],
        compiler_params=_params("arbitrary"),
        name="moe_dispatch",
    )(plan["run_start"], plan["run_len"], plan["pad_start"], plan["pad_len"], u2, pos)


def _moe_kernel(te_ref, nu_ref, par_ref, nxt_ref, x_ref, bgu_ref, bd_ref, wgu_hbm, wd_hbm, y_ref,
                wgu_f, wd_f, wgu_bf, wmix_ref, wd_bf, sem, *, layer):
    j = pl.program_id(0)
    used = j < nu_ref[0]
    buf = par_ref[j]

    def copies(e, s):
        return (pltpu.make_async_copy(wgu_hbm.at[layer, e], wgu_f.at[s], sem.at[s]),
                pltpu.make_async_copy(wd_hbm.at[layer, e], wd_f.at[s], sem.at[s]))

    @pl.when(jnp.logical_and(used, j == 0))
    def _():
        for cp in copies(te_ref[0], 0):
            cp.start(priority=1)

    @pl.when(jnp.logical_and(used, jnp.logical_or(j == 0, te_ref[j] != te_ref[jnp.maximum(j - 1, 0)])))
    def _():
        for cp in copies(te_ref[j], buf):
            cp.wait()

        @pl.when(nxt_ref[j] >= 0)
        def _():
            for cp in copies(nxt_ref[j], 1 - buf):
                cp.start(priority=1)

        _prep_expert(wgu_f.at[buf], wd_f.at[buf], wgu_bf, wmix_ref, wd_bf)

    @pl.when(used)
    def _():
        x = _from_tiles(x_ref, MOE_TM).astype(BF16)
        _to_tiles(y_ref, _expert_ffn(x, wgu_bf, bgu_ref, wd_bf, bd_ref))


def _moe(layer, xs, plan, w_gate_up, b_gate_up, w_down, b_down):
    rows = pl.BlockSpec((MOE_TM * SUB, LANES), lambda j, te, nu, par, nxt: (jnp.minimum(j, nu[0] - 1), 0))
    grid_spec = pltpu.PrefetchScalarGridSpec(
        num_scalar_prefetch=4,
        grid=(MOE_TILES,),
        in_specs=[rows,
                  pl.BlockSpec((None, None, 1, 2 * D_FF), lambda j, te, nu, par, nxt: (layer, te[j], 0, 0)),
                  pl.BlockSpec((None, None, 1, D), lambda j, te, nu, par, nxt: (layer, te[j], 0, 0)),
                  pl.BlockSpec(memory_space=pl.ANY), pl.BlockSpec(memory_space=pl.ANY)],
        out_specs=rows,
        scratch_shapes=[pltpu.VMEM((2, D, 2 * D_FF), F32), pltpu.VMEM((2, D_FF, D), F32),
                        pltpu.VMEM((D, 2 * D_FF), BF16), pltpu.VMEM((D // LANES, D_FF, LANES), F32),
                        pltpu.VMEM((D_FF, D), BF16), pltpu.SemaphoreType.DMA((2,))])

    def kernel(*refs):
        _moe_kernel(*refs, layer=layer)

    return pl.pallas_call(
        kernel,
        grid_spec=grid_spec,
        out_shape=jax.ShapeDtypeStruct((MOE_TILES * MOE_TM * SUB, LANES), F32),
        compiler_params=_params("arbitrary"),
        name="moe_experts",
    )(plan["tile_expert"], plan["n_used"], plan["seg_parity"], plan["next_expert"], xs,
      b_gate_up.reshape(DEPTH, N_EXPERTS, 1, 2 * D_FF), b_down.reshape(DEPTH, N_EXPERTS, 1, D), w_gate_up, w_down)


def _combine_kernel(rs_ref, rl_ref, rsn_ref, rln_ref, x1_ref, pos_ref, w_ref, g2_ref, lng_ref, lnb_ref, ys_hbm,
                    *rest, final):
    out_refs, (stage, sem) = rest[:-2], rest[-2:]
    i = pl.program_id(0)
    slot = i % 2
    other = 1 - slot

    def fetch(start_ref, len_ref, s):
        _for_each_run(start_ref, len_ref, lambda l, g, size: pltpu.make_async_copy(
            _rows(ys_hbm, g, size), _rows(stage.at[s], l, size), sem.at[s]).start())

    def wait(s):
        pltpu.make_async_copy(stage.at[s], stage.at[s], sem.at[s]).wait()

    @pl.when(i == 0)
    def _():
        fetch(rs_ref, rl_ref, 0)

    fetch(rsn_ref, rln_ref, other)
    wait(slot)
    y = _from_tiles(stage.at[slot], PAIRS_TM).astype(BF16)
    place = pos_ref[...].astype(F32)
    w = w_ref[...]
    s = lax.broadcasted_iota(jnp.int32, (TM, PAIRS_TM), 1).astype(F32)
    pick = jnp.zeros((TM, PAIRS_TM), F32)
    for k in range(TOP_K):
        pick = jnp.where(s == place[:, k:k + 1], w[:, k:k + 1], pick)
    f = _dot(pick.astype(BF16), y)
    ci = _cond_row(i)
    z = DEEPNORM_ALPHA * x1_ref[...] + g2_ref[pl.ds(ci, 1), :] * f
    out = _layer_norm(z, lng_ref[...], lnb_ref[...])
    if final:
        @pl.when(i < NT_P)
        def _():
            out_refs[0][...] = out

        @pl.when(i >= NT_P)
        def _():
            out_refs[1][...] = out
    else:
        out_refs[0][...] = out

    @pl.when(i == NT - 1)
    def _():
        wait(other)


def _combine(layer, x1, ys, pos, w, plan, mod, g, b):
    final = layer == DEPTH - 1
    nxt = lambda i: (jnp.minimum(i + 1, NT - 1), 0, 0)
    cur = lambda i: (i, 0, 0)
    if final:
        out_specs = _x_specs()
        out_shape = [jax.ShapeDtypeStruct((T_P, D), F32), jax.ShapeDtypeStruct((T_S, D), F32)]
    else:
        out_specs = _row_spec(D)
        out_shape = jax.ShapeDtypeStruct((T, D), F32)

    def kernel(*refs):
        _combine_kernel(*refs, final=final)

    return pl.pallas_call(
        kernel,
        grid=(NT,),
        in_specs=[_run_spec(cur), _run_spec(cur), _run_spec(nxt), _run_spec(nxt),
                  _row_spec(D), _row_spec(LANES), _row_spec(LANES), _mod_spec(layer, 5), _full((1, D)), _full((1, D)),
                  pl.BlockSpec(memory_space=pl.ANY)],
        out_specs=out_specs,
        out_shape=out_shape,
        scratch_shapes=[pltpu.VMEM((2, PAIRS_TM * SUB, LANES), F32), pltpu.SemaphoreType.DMA((2,))],
        compiler_params=_params("arbitrary"),
        name="moe_combine_norm" + ("_final" if final else ""),
    )(plan["run_start"], plan["run_len"], plan["run_start"], plan["run_len"], x1, pos, w, mod, g, b, ys)


def _moe_layer(layer, routed, experts, mod, ln2_g, ln2_b):
    x1, u2, pos, w, cnt = routed
    plan = _moe_plan(cnt)
    ys = _moe(layer, _dispatch(u2, pos, plan), plan, *experts)
    return _combine(layer, x1, ys, pos, w, plan, mod, ln2_g[layer][None], ln2_b[layer][None])


def _router_tail(l, ln1_g, ln1_b, router_w, router_b):
    rw = jnp.pad(router_w[l], ((0, 0), (0, LANES - N_EXPERTS)))
    rb = jnp.pad(router_b[l], (0, LANES - N_EXPERTS), constant_values=NEG)
    return ln1_g[l][None], ln1_b[l][None], rw, rb[None]


def kernel(x_prompt, x_sample, c, c_ctx, cache_diff_k, cache_diff_v, cache_na_k, cache_na_v, cache_gqa_k, cache_gqa_v, w_mod, b_mod, ln1_g, ln1_b, ln2_g, ln2_b, ab_w_in, ab_conv_w, ab_lambda_q1, ab_lambda_k1, ab_lambda_q2, ab_lambda_k2, ab_subln_g, ab_w_out, cd_w_in, cd_na_rpb, cd_q_norm_g, cd_k_norm_g, cd_w_out, router_w, router_b, w_gate_up, b_gate_up, w_down, b_down):
    xp = x_prompt.reshape(T_P, D)
    xs = x_sample.reshape(T_S, D)
    cond8 = jnp.concatenate([c_ctx[None], c, jnp.zeros((8 - 1 - B_S, D), F32)], axis=0)
    mod = _modulation(cond8, w_mod, b_mod)
    rope = _rope_tables()
    experts = (w_gate_up, b_gate_up, w_down, b_down)

    gb, y, q, k, v, new_diff_k, new_diff_v = _ab_in(xp, xs, mod, ab_w_in[0].astype(BF16), rope)
    lam_init = 0.8 - 0.6 * 1.0
    diff = (jnp.stack([ab_lambda_q1[0], ab_lambda_k1[0], ab_lambda_q2[0], ab_lambda_k2[0]]), ab_subln_g[0][None])
    attn = _flash_pair(q, k, v, latent=False, out_cols=512, col0=0, post="diff", diff=diff, lam_init=lam_init)
    attn = _flash_pair(q, k, v, latent=True, out_cols=512, col0=0, post="diff", diff=diff, lam_init=lam_init,
                       ctx=(cache_diff_k, cache_diff_v), ctx_mode="wide", prev=attn)
    routed = _ab_out(xp, xs, gb, y, ab_conv_w[0], attn, ab_w_out[0].astype(BF16), mod,
                     _router_tail(0, ln1_g, ln1_b, router_w, router_b))
    x = _moe_layer(0, routed, experts, mod, ln2_g, ln2_b)

    qg = jnp.tile(cd_q_norm_g[0], 2)[None]
    kg = jnp.tile(cd_k_norm_g[0], 2)[None]
    nq, nk, nv, gq, gk, gv, new_na_k, new_na_v, new_gqa_k, new_gqa_v = _cd_in(
        x, mod, cd_w_in[0].astype(BF16), qg, kg, rope)
    merged = _flash_pair(nq, nk, nv, latent=False, out_cols=D, col0=0, post="select")
    merged = _flash_pair(gq, gk, gv, latent=False, out_cols=D, col0=2, post="select", prev=merged)
    merged = _na_latent(nq, nk, nv, cache_na_k, cache_na_v, _na_bias(cd_na_rpb[0]), merged)
    merged = _flash_pair(gq, gk, gv, latent=True, out_cols=D, col0=2, post="select",
                         ctx=(cache_gqa_k, cache_gqa_v), ctx_mode="dup", prev=merged)
    routed = _cd_out(x, merged, cd_w_out[0].astype(BF16), mod,
                     _router_tail(1, ln1_g, ln1_b, router_w, router_b))
    y_p, y_s = _moe_layer(1, routed, experts, mod, ln2_g, ln2_b)

    return (y_p.reshape(B_P, N_P, D), y_s.reshape(B_S, N_S, D), new_diff_k, new_diff_v,
            new_na_k, new_na_v, new_gqa_k, new_gqa_v)
```

```python
import jax
import jax.numpy as jnp
import numpy as np
from jax import lax
from jax.experimental import pallas as pl
from jax.experimental.pallas import tpu as pltpu

F32 = jnp.float32
BF16 = jnp.bfloat16

D = 1024
B_P, N_P = 16, 256
B_S, N_S = 2, 4096
PAST = 256
T_P, T_S = B_P * N_P, B_S * N_S
T = T_P + T_S
TM = 256
NT_P, NT_S, NT = T_P // TM, T_S // TM, T // TM
TILES_PER_GRID = N_S // TM
GRID_W = 64
GRID_H = N_S // GRID_W
HD = 64
DEPTH = 2
N_EXPERTS = 32
TOP_K = 4
D_FF = 1024
NA_WIN_R, NA_WIN_C = 8, 16
NA_QROWS = 4
NA_KROWS = 12
SWIGLU_LIMIT = 7.0
SWIGLU_ALPHA = 1.702
ROPE_THETA = 10000.0
DEEPNORM_ALPHA = (2 * DEPTH) ** 0.25
LN_EPS = 1e-5
RMS_EPS = 1e-6
LOG2E = 1.4426950408889634
QK_SCALE = HD ** -0.5 * LOG2E
NEG = -1e30
MOE_TM = 256
MOE_PAIRS = TOP_K * T
MOE_TILES = MOE_PAIRS // MOE_TM + N_EXPERTS
PAIRS_TM = TOP_K * TM
RUN_SIZES = tuple(TM >> b for b in range(TM.bit_length()))
LANES = 128
SUB = 8

VMEM_LIMIT = 56 * 1024 * 1024


def _params(*sem):
    return pltpu.CompilerParams(dimension_semantics=sem, vmem_limit_bytes=VMEM_LIMIT)


def _split(x):
    hi = x.astype(BF16)
    lo = (x - hi.astype(F32)).astype(BF16)
    return hi, lo


def _dot(a, b):
    return jnp.dot(a, b, preferred_element_type=F32)


def _dot3(a, b):
    ah, al = _split(a)
    bh, bl = _split(b)
    return _dot(ah, bh) + (_dot(ah, bl) + _dot(al, bh))


def _cond_row(i):
    return jnp.where(i < NT_P, 0, 1 + (i - NT_P) // TILES_PER_GRID)


def _layer_norm(z, g, b):
    mu = jnp.mean(z, axis=-1, keepdims=True)
    zc = z - mu
    var = jnp.mean(zc * zc, axis=-1, keepdims=True)
    return zc * lax.rsqrt(var + LN_EPS) * g + b


def _low_half(rows):
    return lax.broadcasted_iota(jnp.int32, (rows, LANES), 1) < HD


def _mod_kernel(c_ref, w_ref, b_ref, o_ref):
    c = c_ref[...]
    o_ref[...] = _dot3(c * jax.nn.sigmoid(c), w_ref[...]) + b_ref[...]


def _modulation(cond8, w_mod, b_mod):
    return pl.pallas_call(
        _mod_kernel,
        grid=(DEPTH, 6),
        in_specs=[pl.BlockSpec((8, D), lambda l, j: (0, 0)),
                  pl.BlockSpec((None, D, D), lambda l, j: (l, 0, j)),
                  pl.BlockSpec((None, 1, D), lambda l, j: (l, 0, j))],
        out_specs=pl.BlockSpec((None, 8, D), lambda l, j: (l, 0, j)),
        out_shape=jax.ShapeDtypeStruct((DEPTH, 8, 6 * D), F32),
        compiler_params=_params("arbitrary", "arbitrary"),
        name="modulation",
    )(cond8, w_mod, b_mod.reshape(DEPTH, 1, 6 * D))


def _mod_spec(layer, chunk):
    return pl.BlockSpec((None, 8, D), lambda i, _l=layer, _c=chunk: (_l, 0, _c))


def _full(shape):
    return pl.BlockSpec(shape, lambda i: (0,) * len(shape))


def _rope_tables():
    t = np.arange(N_S)
    half = HD // 2
    inv = ROPE_THETA ** (-np.arange(0, half, 2, dtype=np.float64) / half)
    inv_lane = np.tile(np.repeat(inv, 2), 2 * LANES // HD)
    lane = np.arange(LANES)
    by_row = (lane % HD) < half
    ang = np.where(by_row[None], (t // GRID_W)[:, None], (t % GRID_W)[:, None]) * inv_lane[None]
    cos, sin = np.cos(ang), np.sin(ang)
    even = (lane % 2) == 0
    return tuple(jnp.asarray(a, F32) for a in (cos, np.where(even, -sin, 0.0), np.where(even, 0.0, sin)))


def _rope(x, a, b, c):
    return x * a + pltpu.roll(x, LANES - 1, axis=1) * b + pltpu.roll(x, 1, axis=1) * c


def _rope_or_identity(identity, ra_ref, rb_ref, rc_ref):
    return (jnp.where(identity, 1.0, ra_ref[...]), jnp.where(identity, 0.0, rb_ref[...]),
            jnp.where(identity, 0.0, rc_ref[...]))


def _rope_spec():
    return pl.BlockSpec((TM, LANES), lambda i: (jnp.maximum(i - NT_P, 0) % TILES_PER_GRID, 0))


def _x_specs():
    return [pl.BlockSpec((TM, D), lambda i: (jnp.minimum(i, NT_P - 1), 0)),
            pl.BlockSpec((TM, D), lambda i: (jnp.maximum(i - NT_P, 0), 0))]


def _cache_spec(heads, width):
    return pl.BlockSpec((None, None, heads, N_P, width), lambda i: (jnp.minimum(i, NT_P - 1), 0, 0, 0, 0))


def _row_spec(width):
    return pl.BlockSpec((TM, width), lambda i: (i, 0))


def _hm_spec(n):
    return pl.BlockSpec((n, TM, LANES), lambda i: (0, i, 0))


AB_Q0, AB_K0, AB_V0 = 1536, 2048, 2560


def _tile(p, col0, j):
    return p[:, col0 + j * LANES:col0 + (j + 1) * LANES]


def _ab_in_kernel(xp_ref, xs_ref, sh_ref, sc_ref, w_ref, ra_ref, rb_ref, rc_ref,
                  gb_ref, y_ref, q_ref, k_ref, v_ref, kc_ref, vc_ref):
    i = pl.program_id(0)
    is_p = i < NT_P
    ci = _cond_row(i)
    x = jnp.where(is_p, xp_ref[...], xs_ref[...])
    u = x * (1.0 + sc_ref[pl.ds(ci, 1), :]) + sh_ref[pl.ds(ci, 1), :]
    p = _dot(u.astype(BF16), w_ref[...])
    gb_ref[...] = p[:, 0:512]
    y_ref[...] = p[:, 512:1024] * p[:, 1024:1536]
    a, b, c = _rope_or_identity(is_p, ra_ref, rb_ref, rc_ref)
    for h in range(4):
        v_ref[h] = _tile(p, AB_V0, h).astype(BF16)
        q_ref[h] = (_rope(_tile(p, AB_Q0, h), a, b, c) * QK_SCALE).astype(BF16)
        k_ref[h] = _rope(_tile(p, AB_K0, h), a, b, c).astype(BF16)

    @pl.when(is_p)
    def _():
        for h in range(4):
            kc_ref[h] = _tile(p, AB_K0, h)
            vc_ref[h] = _tile(p, AB_V0, h)


def _ab_in(xp, xs, mod, w_in, rope):
    hm = jax.ShapeDtypeStruct((4, T, LANES), BF16)
    cache = jax.ShapeDtypeStruct((B_P, 1, 4, N_P, LANES), F32)
    half = jax.ShapeDtypeStruct((T, 512), F32)
    return pl.pallas_call(
        _ab_in_kernel,
        grid=(NT,),
        in_specs=_x_specs() + [_mod_spec(0, 0), _mod_spec(0, 1), _full((D, 3072)),
                               _rope_spec(), _rope_spec(), _rope_spec()],
        out_specs=[_row_spec(512), _row_spec(512), _hm_spec(4), _hm_spec(4), _hm_spec(4),
                   _cache_spec(4, LANES), _cache_spec(4, LANES)],
        out_shape=[half, half, hm, hm, hm, cache, cache],
        compiler_params=_params("arbitrary"),
        name="ab_in_proj",
    )(xp, xs, mod, mod, w_in, *rope)


CD_NQ, CD_NK, CD_NV, CD_GQ, CD_GK, CD_GV = 0, 512, 1024, 1536, 2048, 2176


def _seg_mean64(s):
    r = lax.broadcasted_iota(jnp.int32, (LANES, LANES), 0) // HD
    c = lax.broadcasted_iota(jnp.int32, (LANES, LANES), 1) // HD
    seg = jnp.where(r == c, 1.0, 0.0).astype(BF16)
    hi, lo = _split(s)
    return (_dot(hi, seg) + _dot(lo, seg)) * (1.0 / HD)


def _rms64(x, g):
    return x * lax.rsqrt(_seg_mean64(x * x) + RMS_EPS) * g


def _dup_halves(x, lo):
    r = pltpu.roll(x, HD, axis=1)
    return jnp.where(lo, x, r), jnp.where(lo, r, x)


def _cd_in_kernel(x_ref, sh_ref, sc_ref, w_ref, qg_ref, kg_ref, ra_ref, rb_ref, rc_ref,
                  nq_ref, nk_ref, nv_ref, gq_ref, gk_ref, gv_ref,
                  nkc_ref, nvc_ref, gkc_ref, gvc_ref):
    i = pl.program_id(0)
    is_p = i < NT_P
    ci = _cond_row(i)
    u = x_ref[...] * (1.0 + sc_ref[pl.ds(ci, 1), :]) + sh_ref[pl.ds(ci, 1), :]
    p = _dot(u.astype(BF16), w_ref[...])
    lo = _low_half(TM)
    for j in range(4):
        nq_ref[j] = (_tile(p, CD_NQ, j) * QK_SCALE).astype(BF16)
        nk_ref[j] = _tile(p, CD_NK, j).astype(BF16)
        nv_ref[j] = _tile(p, CD_NV, j).astype(BF16)
    gq = [_rms64(_tile(p, CD_GQ, j), qg_ref[...]) for j in range(4)]
    gk = _rms64(_tile(p, CD_GK, 0), kg_ref[...])
    gv = _tile(p, CD_GV, 0)
    v0, v1 = _dup_halves(gv, lo)
    gv_ref[0] = v0.astype(BF16)
    gv_ref[1] = v1.astype(BF16)
    a, b, c = _rope_or_identity(is_p, ra_ref, rb_ref, rc_ref)
    for j in range(4):
        gq_ref[j] = (_rope(gq[j], a, b, c) * QK_SCALE).astype(BF16)
    k0, k1 = _dup_halves(_rope(gk, a, b, c), lo)
    gk_ref[0] = k0.astype(BF16)
    gk_ref[1] = k1.astype(BF16)

    @pl.when(is_p)
    def _():
        gkc_ref[0] = k0[:, 0:HD]
        gkc_ref[1] = k1[:, 0:HD]
        gvc_ref[0] = v0[:, 0:HD]
        gvc_ref[1] = v1[:, 0:HD]
        for j in range(4):
            for src, dst in ((CD_NK, nkc_ref), (CD_NV, nvc_ref)):
                a, b = _dup_halves(_tile(p, src, j), lo)
                dst[2 * j] = a[:, 0:HD]
                dst[2 * j + 1] = b[:, 0:HD]


def _cd_in(x, mod, w_in, qg, kg, rope):
    hm4 = jax.ShapeDtypeStruct((4, T, LANES), BF16)
    hm2 = jax.ShapeDtypeStruct((2, T, LANES), BF16)
    c8 = jax.ShapeDtypeStruct((B_P, 1, 8, N_P, HD), F32)
    c2 = jax.ShapeDtypeStruct((B_P, 1, 2, N_P, HD), F32)
    return pl.pallas_call(
        _cd_in_kernel,
        grid=(NT,),
        in_specs=[_row_spec(D), _mod_spec(1, 0), _mod_spec(1, 1), _full((D, 2304)),
                  _full((1, LANES)), _full((1, LANES)), _rope_spec(), _rope_spec(), _rope_spec()],
        out_specs=[_hm_spec(4), _hm_spec(4), _hm_spec(4), _hm_spec(4), _hm_spec(2), _hm_spec(2),
                   _cache_spec(8, HD), _cache_spec(8, HD), _cache_spec(2, HD), _cache_spec(2, HD)],
        out_shape=[hm4, hm4, hm4, hm4, hm2, hm2, c8, c8, c2, c2],
        compiler_params=_params("arbitrary"),
        name="cd_in_proj",
    )(x, mod, mod, w_in, qg, kg, *rope)


def _stack_pairs(q_ref, n_q, tq):
    lo = _low_half(tq)
    parts = []
    for j in range(n_q):
        q = q_ref[j]
        zero = jnp.zeros_like(q)
        parts += [jnp.where(lo, q, zero), jnp.where(lo, zero, q)]
    return jnp.concatenate(parts, axis=0), lo


def _qk(qs, kb):
    return lax.dot_general(qs, kb, (((1,), (1,)), ((), ())), preferred_element_type=F32)


def _ctx_tile(ref, mode):
    if mode == "wide":
        x = ref[...]
    elif mode == "pair":
        x = jnp.concatenate([ref[0], ref[1]], axis=1)
    else:
        x = jnp.concatenate([ref[...], ref[...]], axis=1)
    return x.astype(BF16)


def _with_ones(v):
    lane = lax.broadcasted_iota(jnp.int32, v.shape, 1)
    return jnp.concatenate([v, jnp.where(lane == 0, 1.0, 0.0).astype(v.dtype)], axis=1)


def _flash_pair_kernel(*refs, gs, ctx_mode, **kw):
    it = iter(refs)
    q_ref, k_ref, v_ref = next(it), next(it), next(it)
    ctx = [next(it), next(it)] if ctx_mode else []
    rest = list(it)
    o_ref = rest.pop()
    n_q = q_ref.shape[0] // gs
    for t in range(gs * n_q):
        gi = t // n_q
        _flash_group(q_ref.at[t:t + 1], k_ref.at[gi], v_ref.at[gi], *[r.at[gi] for r in ctx], *rest,
                     o_ref.at[:, t * LANES:(t + 1) * LANES], n_q=1, ctx_mode=ctx_mode, **kw)


def _flash_group(*refs, n_q, tq, nk, tk, ctx_mode, post, lam_init):
    it = iter(refs)
    q_ref, k_ref, v_ref = next(it), next(it), next(it)
    kc_ref, vc_ref = (next(it), next(it)) if ctx_mode else (None, None)
    lam_ref, g_ref = (next(it), next(it)) if post == "diff" else (None, None)
    o_ref = next(it)

    qs, lo = _stack_pairs(q_ref, n_q, tq)
    rows = 2 * n_q * tq

    def step(kb, vb, carry):
        m, acc = carry
        s = _qk(qs, kb)
        m_new = jnp.maximum(m, jnp.max(s, axis=1, keepdims=True))
        p = jnp.exp2((s - m_new).astype(BF16))
        return m_new, jnp.exp2(m - m_new) * acc + _dot(p, _with_ones(vb))

    carry = (jnp.full((rows, 1), NEG, F32), jnp.zeros((rows, 2 * LANES), F32))
    for c in range(nk // tk):
        carry = step(k_ref[c * tk:(c + 1) * tk, :], v_ref[c * tk:(c + 1) * tk, :], carry)
    if ctx_mode:
        carry = step(_ctx_tile(kc_ref, ctx_mode), _ctx_tile(vc_ref, ctx_mode), carry)
    _, acc = carry
    o = acc[:, 0:LANES] / acc[:, LANES:LANES + 1]

    if post == "diff":
        lp = lam_ref[...]
        lam = (jnp.exp(jnp.sum(lp[0:1] * lp[1:2], axis=1, keepdims=True))
               - jnp.exp(jnp.sum(lp[2:3] * lp[3:4], axis=1, keepdims=True)) + lam_init)
        a = o[0:tq] - lam * o[tq:2 * tq]
        ms = jnp.mean(a * a, axis=-1, keepdims=True)
        o_ref[...] = (a * lax.rsqrt(ms + RMS_EPS) * g_ref[...] * (1.0 - lam_init)).astype(o_ref.dtype)
    else:
        for j in range(n_q):
            o_ref[:, j * LANES:(j + 1) * LANES] = jnp.where(
                lo, o[2 * j * tq:(2 * j + 1) * tq], o[(2 * j + 1) * tq:(2 * j + 2) * tq]).astype(o_ref.dtype)


def _flash_pair(q, k, v, *, latent, out_cols, col0, post, ctx=None, ctx_mode=None, diff=None,
                lam_init=0.0, prev=None):
    groups = k.shape[0]
    n_q = q.shape[0] // groups
    if latent:
        tq, nk, tk, gs = TM, N_S, 1024, 2
        grid = (B_S, groups // gs, TILES_PER_GRID)
        qrow = lambda b, g, i: NT_P + b * TILES_PER_GRID + i
        krow = lambda b, g, i: T_P // N_S + b
    else:
        tq, nk, tk, gs = N_P, N_P, N_P, groups
        grid = (B_P, 1, 1)
        qrow = lambda b, g, i: b
        krow = lambda b, g, i: b
    in_specs = [pl.BlockSpec((gs * n_q, tq, LANES), lambda b, g, i: (g, qrow(b, g, i), 0)),
                pl.BlockSpec((gs, nk, LANES), lambda b, g, i: (g, krow(b, g, i), 0)),
                pl.BlockSpec((gs, nk, LANES), lambda b, g, i: (g, krow(b, g, i), 0))]
    args = [q, k, v]
    if ctx is not None:
        width = LANES if ctx_mode == "wide" else HD
        spec = pl.BlockSpec((None, None, gs, PAST, width), lambda b, g, i: (b, 0, g, 0, 0))
        in_specs += [spec, spec]
        args += list(ctx)
    if diff is not None:
        in_specs += [pl.BlockSpec((4, HD), lambda b, g, i: (0, 0)),
                     pl.BlockSpec((1, LANES), lambda b, g, i: (0, 0))]
        args += list(diff)
    aliases = {}
    if prev is not None:
        aliases = {len(args): 0}
        in_specs.append(pl.BlockSpec(memory_space=pl.ANY))
        args.append(prev)

    def kernel(*refs):
        if prev is not None:
            refs = refs[:-2] + refs[-1:]
        _flash_pair_kernel(*refs, gs=gs, tq=tq, nk=nk, tk=tk, ctx_mode=ctx_mode if ctx is not None else None,
                           post=post, lam_init=lam_init)

    return pl.pallas_call(
        kernel,
        grid=grid,
        in_specs=in_specs,
        out_specs=pl.BlockSpec((tq, gs * n_q * LANES), lambda b, g, i: (qrow(b, g, i), col0 // gs + g)),
        out_shape=jax.ShapeDtypeStruct((T, out_cols), BF16),
        input_output_aliases=aliases,
        compiler_params=_params("arbitrary", "arbitrary", "arbitrary"),
        name="attn_" + post + ("_latent" if latent else "_context") + str(n_q),
    )(*args)


NA_TQ = NA_QROWS * GRID_W
NA_TK = NA_KROWS * GRID_W


NA_BLOCK_POS = ((0, 0), (NA_QROWS, 0), (GRID_H - NA_QROWS, GRID_H - NA_KROWS))
N_DR = 2 * NA_WIN_R - 1
N_DC = 2 * NA_WIN_C - 1


def _na_bias_kernel(rpb_ref, o_ref):
    qc = lax.broadcasted_iota(jnp.int32, (GRID_W, LANES), 0)
    lane = lax.broadcasted_iota(jnp.int32, (GRID_W, LANES), 1)
    kc = lane % GRID_W
    cs = jnp.clip(qc - NA_WIN_C // 2, 0, GRID_W - NA_WIN_C)
    col_ok = jnp.logical_and(kc >= cs, kc < cs + NA_WIN_C)
    lo = lane < GRID_W
    neg = jnp.full((GRID_W, LANES), NEG, F32)
    for head in range(2):
        toeplitz = []
        for dr in range(N_DR):
            r = jnp.broadcast_to(rpb_ref[head, dr:dr + 1, :] * LOG2E, (GRID_W, LANES))
            t = jnp.where(lo, pltpu.roll(r, LANES - (NA_WIN_C - 1), axis=1, stride=1, stride_axis=0),
                          pltpu.roll(r, GRID_W - (NA_WIN_C - 1), axis=1, stride=1, stride_axis=0))
            toeplitz.append(jnp.where(col_ok, t, neg))
        for pos, (r0, k0) in enumerate(NA_BLOCK_POS):
            for i in range(NA_QROWS):
                qr = r0 + i
                rs = min(max(qr - NA_WIN_R // 2, 0), GRID_H - NA_WIN_R)
                tiles = [toeplitz[k0 + j - qr + NA_WIN_R - 1] if rs <= k0 + j < rs + NA_WIN_R else neg
                         for j in range(NA_KROWS)]
                for jp in range(NA_KROWS // 2):
                    o_ref[pos, pl.ds(head * NA_TQ + i * GRID_W, GRID_W), pl.ds(jp * LANES, LANES)] = jnp.where(
                        lo, tiles[2 * jp], tiles[2 * jp + 1])


def _na_bias(rpb):
    rpb_pad = jnp.pad(rpb, ((0, 0), (0, 16 - N_DR), (0, LANES - N_DC)), constant_values=NEG)
    return pl.pallas_call(
        _na_bias_kernel,
        grid=(4,),
        in_specs=[pl.BlockSpec((2, 16, LANES), lambda g: (g, 0, 0))],
        out_specs=pl.BlockSpec((3, None, 2 * NA_TQ, NA_TK), lambda g: (0, g, 0, 0)),
        out_shape=jax.ShapeDtypeStruct((3, 4, 2 * NA_TQ, NA_TK), F32),
        compiler_params=_params("arbitrary"),
        name="na_bias_table",
    )(rpb_pad)


NA_PAIRS = 4


def _na_kernel(q_ref, k_ref, v_ref, kc_ref, vc_ref, bm_ref, _, o_ref):
    i = pl.program_id(2)
    k0 = jnp.clip(i * NA_QROWS - NA_WIN_R // 2, 0, GRID_H - NA_KROWS)
    start = pl.multiple_of(k0 * GRID_W, GRID_W)
    for g in range(NA_PAIRS):
        qs, lo = _stack_pairs(q_ref.at[g:g + 1], 1, NA_TQ)
        kw = k_ref[g, pl.ds(start, NA_TK), :]
        vw = v_ref[g, pl.ds(start, NA_TK), :]
        s_w = _qk(qs, kw) + bm_ref[g]
        s_c = _qk(qs, _ctx_tile(kc_ref.at[2 * g:2 * g + 2], "pair"))
        m = jnp.maximum(jnp.max(s_w, axis=1, keepdims=True), jnp.max(s_c, axis=1, keepdims=True))
        p_w = jnp.exp2((s_w - m).astype(BF16))
        p_c = jnp.exp2((s_c - m).astype(BF16))
        acc = _dot(p_w, _with_ones(vw)) + _dot(p_c, _with_ones(_ctx_tile(vc_ref.at[2 * g:2 * g + 2], "pair")))
        o = acc[:, 0:LANES] / acc[:, LANES:LANES + 1]
        o_ref[:, g * LANES:(g + 1) * LANES] = jnp.where(lo, o[0:NA_TQ], o[NA_TQ:2 * NA_TQ]).astype(o_ref.dtype)


def _na_latent(q, k, v, kc, vc, bm, prev):
    nblk = N_S // NA_TQ
    qrow = lambda b, g, i: T_P // NA_TQ + b * nblk + i
    krow = lambda b, g, i: T_P // N_S + b
    cfg = lambda i: jnp.where(i == 0, 0, jnp.where(i == nblk - 1, 2, 1))
    ctx_spec = pl.BlockSpec((None, None, 2 * NA_PAIRS, PAST, HD), lambda b, g, i: (b, 0, g, 0, 0))
    return pl.pallas_call(
        _na_kernel,
        grid=(B_S, 4 // NA_PAIRS, nblk),
        in_specs=[pl.BlockSpec((NA_PAIRS, NA_TQ, LANES), lambda b, g, i: (g, qrow(b, g, i), 0)),
                  pl.BlockSpec((NA_PAIRS, N_S, LANES), lambda b, g, i: (g, krow(b, g, i), 0)),
                  pl.BlockSpec((NA_PAIRS, N_S, LANES), lambda b, g, i: (g, krow(b, g, i), 0)),
                  ctx_spec, ctx_spec,
                  pl.BlockSpec((None, NA_PAIRS, 2 * NA_TQ, NA_TK), lambda b, g, i: (cfg(i), g, 0, 0)),
                  pl.BlockSpec(memory_space=pl.ANY)],
        out_specs=pl.BlockSpec((NA_TQ, NA_PAIRS * LANES), lambda b, g, i: (qrow(b, g, i), g)),
        out_shape=jax.ShapeDtypeStruct((T, D), BF16),
        input_output_aliases={6: 0},
        compiler_params=_params("arbitrary", "arbitrary", "arbitrary"),
        name="attn_window_latent",
    )(q, k, v, kc, vc, bm, prev)


def _top4(logits):
    lane = lax.broadcasted_iota(jnp.int32, logits.shape, 1).astype(F32)
    rest = logits
    tops, firsts = [], []
    for _ in range(TOP_K):
        m = jnp.max(rest, axis=1, keepdims=True)
        first = jnp.min(jnp.where(rest == m, lane, float(LANES)), axis=1, keepdims=True)
        tops.append(m)
        firsts.append(first)
        rest = jnp.where(lane == first, -jnp.inf, rest)
    es = [jnp.exp(m - tops[0]) for m in tops]
    denom = es[0] + es[1] + es[2] + es[3]
    w = jnp.zeros_like(logits)
    for k in range(TOP_K):
        w = jnp.where(lane == float(k), es[k] / denom, w)
    return firsts, w


def _local_sort(firsts):
    lane = lax.broadcasted_iota(jnp.int32, (TM, LANES), 1).astype(F32)
    hots = [lane == f for f in firsts]
    sel = jnp.zeros((TM, LANES), F32)
    for hot in hots:
        sel = jnp.where(hot, 1.0, sel)
    r = lax.broadcasted_iota(jnp.int32, (TM, TM), 0)
    c = lax.broadcasted_iota(jnp.int32, (TM, TM), 1)
    earlier = _dot(jnp.where(c < r, 1.0, 0.0).astype(BF16), sel.astype(BF16))
    cnt = jnp.sum(sel, axis=0, keepdims=True)
    r = lax.broadcasted_iota(jnp.int32, (LANES, LANES), 0)
    c = lax.broadcasted_iota(jnp.int32, (LANES, LANES), 1)
    start = _dot(jnp.broadcast_to(cnt, (SUB, LANES)).astype(BF16), jnp.where(r < c, 1.0, 0.0).astype(BF16))[0:1]
    place = start + earlier
    pos = jnp.zeros((TM, LANES), F32)
    for k, hot in enumerate(hots):
        pos = jnp.where(lane == float(k), jnp.sum(jnp.where(hot, place, 0.0), axis=1, keepdims=True), pos)
    return pos.astype(jnp.int32), cnt.astype(jnp.int32)


def _mixer_tail(x_of, h_of, ci, g1_ref, sh2_ref, sc2_ref, lng_ref, lnb_ref, rw_ref, rb_ref,
                x1_ref, u2_ref, pos_ref, w_ref, cnt_ref):
    rows = slice(0, TM)
    x1 = _layer_norm(DEEPNORM_ALPHA * x_of(rows) + g1_ref[pl.ds(ci, 1), :] * h_of(rows), lng_ref[...], lnb_ref[...])
    x1_ref[...] = x1
    u2 = x1 * (1.0 + sc2_ref[pl.ds(ci, 1), :]) + sh2_ref[pl.ds(ci, 1), :]
    u2_ref[...] = u2.astype(BF16)
    firsts, w_ref[...] = _top4(_dot3(u2, rw_ref[...]) + rb_ref[...])
    pos, cnt = _local_sort(firsts)
    pos_ref[...] = pos
    cnt_ref[...] = jnp.broadcast_to(cnt, (SUB, LANES))


def _ab_out_kernel(xp_ref, xs_ref, gb_ref, y_ref, yp_ref, yn_ref, cv_ref, at_ref, wo_ref,
                   g1_ref, sh2_ref, sc2_ref, lng_ref, lnb_ref, rw_ref, rb_ref, *out_refs):
    i = pl.program_id(0)
    is_p = i < NT_P
    ci = _cond_row(i)
    j = (i - NT_P) % TILES_PER_GRID
    first = jnp.logical_or(is_p, j == 0)
    last = jnp.logical_or(is_p, j == TILES_PER_GRID - 1)
    y = y_ref[...]
    row = lax.broadcasted_iota(jnp.int32, y.shape, 0)
    before = jnp.where(first, 0.0, yp_ref[7:8, :])
    after = jnp.where(last, 0.0, yn_ref[0:1, :])
    y_prev = jnp.where(row == 0, before, pltpu.roll(y, 1, axis=0))
    y_next = jnp.where(row == TM - 1, after, pltpu.roll(y, TM - 1, axis=0))
    cv = cv_ref[...]
    conv = (gb_ref[...] * (y_prev * cv[0:1] + y * cv[1:2] + y_next * cv[2:3])).astype(BF16)
    _mixer_tail(lambda r: jnp.where(is_p, xp_ref[r, :], xs_ref[r, :]),
                lambda r: _dot(conv[r], wo_ref[0:512, :]) + _dot(at_ref[r, :], wo_ref[512:1024, :]),
                ci, g1_ref, sh2_ref, sc2_ref, lng_ref, lnb_ref, rw_ref, rb_ref, *out_refs)


def _cd_out_kernel(x_ref, mg_ref, wo_ref, g1_ref, sh2_ref, sc2_ref, lng_ref, lnb_ref, rw_ref, rb_ref,
                   *out_refs):
    ci = _cond_row(pl.program_id(0))
    _mixer_tail(lambda r: x_ref[r, :], lambda r: _dot(mg_ref[r, :], wo_ref[...]),
                ci, g1_ref, sh2_ref, sc2_ref, lng_ref, lnb_ref, rw_ref, rb_ref, *out_refs)


def _tail_specs(layer):
    return [_mod_spec(layer, 2), _mod_spec(layer, 3), _mod_spec(layer, 4),
            _full((1, D)), _full((1, D)), _full((D, LANES)), _full((1, LANES))]


_TAIL_OUT_SPECS = [_row_spec(D), _row_spec(D), _row_spec(LANES), _row_spec(LANES),
                   pl.BlockSpec((None, SUB, LANES), lambda i: (i, 0, 0))]
_TAIL_OUT_SHAPES = [jax.ShapeDtypeStruct((T, D), F32), jax.ShapeDtypeStruct((T, D), BF16),
                    jax.ShapeDtypeStruct((T, LANES), jnp.int32), jax.ShapeDtypeStruct((T, LANES), F32),
                    jax.ShapeDtypeStruct((NT, SUB, LANES), jnp.int32)]


def _ab_out(xp, xs, gb, y, conv_w, attn, w_out, mod, tail):
    halo_prev = pl.BlockSpec((8, 512), lambda i: (jnp.maximum(i * (TM // 8) - 1, 0), 0))
    halo_next = pl.BlockSpec((8, 512), lambda i: (jnp.minimum((i + 1) * (TM // 8), T // 8 - 1), 0))
    return pl.pallas_call(
        _ab_out_kernel,
        grid=(NT,),
        in_specs=_x_specs() + [_row_spec(512), _row_spec(512), halo_prev, halo_next, _full((3, 512)),
                               _row_spec(512), _full((D, D))] + _tail_specs(0),
        out_specs=_TAIL_OUT_SPECS,
        out_shape=_TAIL_OUT_SHAPES,
        compiler_params=_params("arbitrary"),
        name="ab_out_proj",
    )(xp, xs, gb, y, y, y, conv_w, attn, w_out, mod, mod, mod, *tail)


def _cd_out(x, merged, w_out, mod, tail):
    return pl.pallas_call(
        _cd_out_kernel,
        grid=(NT,),
        in_specs=[_row_spec(D), _row_spec(D), _full((D, D))] + _tail_specs(1),
        out_specs=_TAIL_OUT_SPECS,
        out_shape=_TAIL_OUT_SHAPES,
        compiler_params=_params("arbitrary"),
        name="cd_out_proj",
    )(x, merged, w_out, mod, mod, mod, *tail)


def _mix_down(wd_ref, wmix_ref):
    half = D_FF // 2
    for c in range(D // LANES):
        wmix_ref[c, pl.ds(0, half, stride=2), :] = wd_ref[0:half, c * LANES:(c + 1) * LANES]
        wmix_ref[c, pl.ds(1, half, stride=2), :] = wd_ref[half:D_FF, c * LANES:(c + 1) * LANES]


def _prep_expert(wgu_ref, wd_ref, wgu_bf, wmix_ref, wd_bf):
    for c in range(4):
        wgu_bf[:, c * 512:(c + 1) * 512] = wgu_ref[:, c * 512:(c + 1) * 512].astype(BF16)
    _mix_down(wd_ref, wmix_ref)
    for c in range(D // LANES):
        wd_bf[:, c * LANES:(c + 1) * LANES] = wmix_ref[c].astype(BF16)


def _expert_ffn(u, wgu_bf, bgu_ref, wd_bf, bd_ref):
    rows = u.shape[0]
    ga = _dot(u, wgu_bf[:, 0:D_FF]) + bgu_ref[:, 0:D_FF]
    gb = _dot(u, wgu_bf[:, D_FF:2 * D_FF]) + bgu_ref[:, D_FF:2 * D_FF]
    even = (lax.broadcasted_iota(jnp.int32, (rows, LANES), 1) % 2) == 0
    hid = []
    for c in range(D_FF // LANES):
        a = ga[:, c * LANES:(c + 1) * LANES]
        b = gb[:, c * LANES:(c + 1) * LANES]
        gate = jnp.where(even, a, pltpu.roll(b, 1, axis=1))
        up = jnp.where(even, pltpu.roll(a, LANES - 1, axis=1), b)
        gate = jnp.minimum(gate, SWIGLU_LIMIT)
        up = jnp.clip(up, -SWIGLU_LIMIT, SWIGLU_LIMIT)
        hid.append(((up + 1.0) * gate * jax.nn.sigmoid(SWIGLU_ALPHA * gate)).astype(BF16))
    hid = jnp.concatenate(hid, axis=1)
    return _dot(hid, wd_bf[...]) + bd_ref[...]


def _moe_plan(cnt):
    c = cnt[:, 0, :N_EXPERTS]
    counts = jnp.sum(c, axis=0)
    tiles_e = (counts + MOE_TM - 1) // MOE_TM
    tile_end = jnp.cumsum(tiles_e)
    first_row = (tile_end - tiles_e) * MOE_TM
    run_start = first_row[None] + jnp.cumsum(c, axis=0) - c
    j = jnp.arange(MOE_TILES, dtype=jnp.int32)
    tile_expert = jnp.minimum(jnp.sum((j[:, None] >= tile_end[None]).astype(jnp.int32), axis=1), N_EXPERTS - 1)
    i32 = lambda a: a.astype(jnp.int32)
    experts = jnp.arange(N_EXPERTS, dtype=jnp.int32)
    owns = tiles_e > 0
    seg = jnp.cumsum(owns.astype(jnp.int32)) - owns.astype(jnp.int32)
    later = jnp.where((experts[None] > experts[:, None]) & owns[None], experts[None], N_EXPERTS)
    nxt = jnp.min(later, axis=1)
    nxt = jnp.where(nxt == N_EXPERTS, -1, nxt)
    per_tile = lambda v: jnp.sum(jnp.where(tile_expert[:, None] == experts[None], v[None], 0), axis=1)
    return dict(tile_expert=i32(tile_expert), n_used=i32(tile_end[-1:]),
                seg_parity=i32(per_tile(seg % 2)), next_expert=i32(per_tile(nxt)),
                run_start=i32(run_start)[:, None, :], run_len=i32(c)[:, None, :],
                pad_start=i32(first_row + counts)[None, None, :], pad_len=i32(tiles_e * MOE_TM - counts)[None, None, :])


def _for_each_run(start_ref, len_ref, copy):
    def body(e, local):
        n = len_ref[0, e]
        g = start_ref[0, e]

        def pieces(sizes):
            for size in sizes:
                covered = jnp.bitwise_and(n, -2 * size)

                @pl.when(jnp.bitwise_and(n, size) != 0)
                def _(covered=covered, size=size):
                    copy(local + covered, g + covered, size)

        @pl.when(n >= RUN_SIZES[2])
        def _():
            pieces(RUN_SIZES[:3])

        pieces(RUN_SIZES[3:])
        return local + n
    lax.fori_loop(0, N_EXPERTS, body, jnp.int32(0))


def _rows(ref, start, size):
    start = 0 if isinstance(start, int) and start == 0 else pl.multiple_of(start * SUB, SUB)
    return ref.at[pl.ds(start, size * SUB), :]


def _from_tiles(ref, rows):
    return jnp.concatenate([ref[pl.ds(c, rows, stride=SUB), :] for c in range(SUB)], axis=1)


def _to_tiles(ref, x):
    for c in range(SUB):
        ref[pl.ds(c, x.shape[0], stride=SUB), :] = x[:, c * LANES:(c + 1) * LANES]


def _run_spec(index):
    return pl.BlockSpec((None, 1, N_EXPERTS), index, memory_space=pltpu.SMEM)


def _dispatch_kernel(rs_ref, rl_ref, ps_ref, pn_ref, u_ref, pos_ref, xs_hbm, stage, zeros, sem, zsem):
    i = pl.program_id(0)
    slot = i % 2

    def wait(s):
        pltpu.make_async_copy(stage.at[s], stage.at[s], sem.at[s]).wait()

    @pl.when(i == 0)
    def _():
        zeros[...] = jnp.zeros_like(zeros)

        def pad(local, g, size):
            return pltpu.make_async_copy(_rows(zeros, 0, size), _rows(xs_hbm, g, size), zsem.at[0])
        _for_each_run(ps_ref, pn_ref, lambda *a: pad(*a).start())
        _for_each_run(ps_ref, pn_ref, lambda *a: pad(*a).wait())

    @pl.when(i >= 2)
    def _():
        wait(slot)

    place = pos_ref[...].astype(F32).T
    s = lax.broadcasted_iota(jnp.int32, (PAIRS_TM, TM), 0).astype(F32)
    onehot = jnp.zeros((PAIRS_TM, TM), F32)
    for k in range(TOP_K):
        onehot = jnp.where(s == place[k:k + 1, :], 1.0, onehot)
    _to_tiles(stage.at[slot], _dot(onehot.astype(BF16), u_ref[...]))
    _for_each_run(rs_ref, rl_ref, lambda l, g, size: pltpu.make_async_copy(
        _rows(stage.at[slot], l, size), _rows(xs_hbm, g, size), sem.at[slot]).start())

    @pl.when(i == NT - 1)
    def _():
        wait(1 - slot)
        wait(slot)


def _dispatch(u2, pos, plan):
    return pl.pallas_call(
        _dispatch_kernel,
        grid=(NT,),
        in_specs=[_run_spec(lambda i: (i, 0, 0)), _run_spec(lambda i: (i, 0, 0)),
                  _run_spec(lambda i: (0, 0, 0)), _run_spec(lambda i: (0, 0, 0)),
                  _row_spec(D), _row_spec(LANES)],
        out_specs=pl.BlockSpec(memory_space=pl.ANY),
        out_shape=jax.ShapeDtypeStruct((MOE_TILES * MOE_TM * SUB, LANES), F32),
        scratch_shapes=[pltpu.VMEM((2, PAIRS_TM * SUB, LANES), F32), pltpu.VMEM((TM * SUB, LANES), F32),
                        pltpu.SemaphoreType.DMA((2,)), pltpu.SemaphoreType.DMA((1,))],
        compiler_params=_params("arbitrary"),
        name="moe_dispatch",
    )(plan["run_start"], plan["run_len"], plan["pad_start"], plan["pad_len"], u2, pos)


def _moe_kernel(te_ref, nu_ref, par_ref, nxt_ref, x_ref, bgu_ref, bd_ref, wgu_hbm, wd_hbm, y_ref,
                wgu_f, wd_f, wgu_bf, wmix_ref, wd_bf, sem, *, layer):
    j = pl.program_id(0)
    used = j < nu_ref[0]
    buf = par_ref[j]

    def copies(e, s):
        return (pltpu.make_async_copy(wgu_hbm.at[layer, e], wgu_f.at[s], sem.at[s]),
                pltpu.make_async_copy(wd_hbm.at[layer, e], wd_f.at[s], sem.at[s]))

    @pl.when(jnp.logical_and(used, j == 0))
    def _():
        for cp in copies(te_ref[0], 0):
            cp.start(priority=1)

    @pl.when(jnp.logical_and(used, jnp.logical_or(j == 0, te_ref[j] != te_ref[jnp.maximum(j - 1, 0)])))
    def _():
        for cp in copies(te_ref[j], buf):
            cp.wait()

        @pl.when(nxt_ref[j] >= 0)
        def _():
            for cp in copies(nxt_ref[j], 1 - buf):
                cp.start(priority=1)

        _prep_expert(wgu_f.at[buf], wd_f.at[buf], wgu_bf, wmix_ref, wd_bf)

    @pl.when(used)
    def _():
        x = _from_tiles(x_ref, MOE_TM).astype(BF16)
        _to_tiles(y_ref, _expert_ffn(x, wgu_bf, bgu_ref, wd_bf, bd_ref))


def _moe(layer, xs, plan, w_gate_up, b_gate_up, w_down, b_down):
    rows = pl.BlockSpec((MOE_TM * SUB, LANES), lambda j, te, nu, par, nxt: (jnp.minimum(j, nu[0] - 1), 0))
    grid_spec = pltpu.PrefetchScalarGridSpec(
        num_scalar_prefetch=4,
        grid=(MOE_TILES,),
        in_specs=[rows,
                  pl.BlockSpec((None, None, 1, 2 * D_FF), lambda j, te, nu, par, nxt: (layer, te[j], 0, 0)),
                  pl.BlockSpec((None, None, 1, D), lambda j, te, nu, par, nxt: (layer, te[j], 0, 0)),
                  pl.BlockSpec(memory_space=pl.ANY), pl.BlockSpec(memory_space=pl.ANY)],
        out_specs=rows,
        scratch_shapes=[pltpu.VMEM((2, D, 2 * D_FF), F32), pltpu.VMEM((2, D_FF, D), F32),
                        pltpu.VMEM((D, 2 * D_FF), BF16), pltpu.VMEM((D // LANES, D_FF, LANES), F32),
                        pltpu.VMEM((D_FF, D), BF16), pltpu.SemaphoreType.DMA((2,))])

    def kernel(*refs):
        _moe_kernel(*refs, layer=layer)

    return pl.pallas_call(
        kernel,
        grid_spec=grid_spec,
        out_shape=jax.ShapeDtypeStruct((MOE_TILES * MOE_TM * SUB, LANES), F32),
        compiler_params=_params("arbitrary"),
        name="moe_experts",
    )(plan["tile_expert"], plan["n_used"], plan["seg_parity"], plan["next_expert"], xs,
      b_gate_up.reshape(DEPTH, N_EXPERTS, 1, 2 * D_FF), b_down.reshape(DEPTH, N_EXPERTS, 1, D), w_gate_up, w_down)


def _combine_kernel(rs_ref, rl_ref, rsn_ref, rln_ref, x1_ref, pos_ref, w_ref, g2_ref, lng_ref, lnb_ref, ys_hbm,
                    *rest, final):
    out_refs, (stage, sem) = rest[:-2], rest[-2:]
    i = pl.program_id(0)
    slot = i % 2
    other = 1 - slot

    def fetch(start_ref, len_ref, s):
        _for_each_run(start_ref, len_ref, lambda l, g, size: pltpu.make_async_copy(
            _rows(ys_hbm, g, size), _rows(stage.at[s], l, size), sem.at[s]).start())

    def wait(s):
        pltpu.make_async_copy(stage.at[s], stage.at[s], sem.at[s]).wait()

    @pl.when(i == 0)
    def _():
        fetch(rs_ref, rl_ref, 0)

    fetch(rsn_ref, rln_ref, other)
    wait(slot)
    y = _from_tiles(stage.at[slot], PAIRS_TM).astype(BF16)
    place = pos_ref[...].astype(F32)
    w = w_ref[...]
    s = lax.broadcasted_iota(jnp.int32, (TM, PAIRS_TM), 1).astype(F32)
    pick = jnp.zeros((TM, PAIRS_TM), F32)
    for k in range(TOP_K):
        pick = jnp.where(s == place[:, k:k + 1], w[:, k:k + 1], pick)
    f = _dot(pick.astype(BF16), y)
    ci = _cond_row(i)
    z = DEEPNORM_ALPHA * x1_ref[...] + g2_ref[pl.ds(ci, 1), :] * f
    out = _layer_norm(z, lng_ref[...], lnb_ref[...])
    if final:
        @pl.when(i < NT_P)
        def _():
            out_refs[0][...] = out

        @pl.when(i >= NT_P)
        def _():
            out_refs[1][...] = out
    else:
        out_refs[0][...] = out

    @pl.when(i == NT - 1)
    def _():
        wait(other)


def _combine(layer, x1, ys, pos, w, plan, mod, g, b):
    final = layer == DEPTH - 1
    nxt = lambda i: (jnp.minimum(i + 1, NT - 1), 0, 0)
    cur = lambda i: (i, 0, 0)
    if final:
        out_specs = _x_specs()
        out_shape = [jax.ShapeDtypeStruct((T_P, D), F32), jax.ShapeDtypeStruct((T_S, D), F32)]
    else:
        out_specs = _row_spec(D)
        out_shape = jax.ShapeDtypeStruct((T, D), F32)

    def kernel(*refs):
        _combine_kernel(*refs, final=final)

    return pl.pallas_call(
        kernel,
        grid=(NT,),
        in_specs=[_run_spec(cur), _run_spec(cur), _run_spec(nxt), _run_spec(nxt),
                  _row_spec(D), _row_spec(LANES), _row_spec(LANES), _mod_spec(layer, 5), _full((1, D)), _full((1, D)),
                  pl.BlockSpec(memory_space=pl.ANY)],
        out_specs=out_specs,
        out_shape=out_shape,
        scratch_shapes=[pltpu.VMEM((2, PAIRS_TM * SUB, LANES), F32), pltpu.SemaphoreType.DMA((2,))],
        compiler_params=_params("arbitrary"),
        name="moe_combine_norm" + ("_final" if final else ""),
    )(plan["run_start"], plan["run_len"], plan["run_start"], plan["run_len"], x1, pos, w, mod, g, b, ys)


def _moe_layer(layer, routed, experts, mod, ln2_g, ln2_b):
    x1, u2, pos, w, cnt = routed
    plan = _moe_plan(cnt)
    ys = _moe(layer, _dispatch(u2, pos, plan), plan, *experts)
    return _combine(layer, x1, ys, pos, w, plan, mod, ln2_g[layer][None], ln2_b[layer][None])


def _router_tail(l, ln1_g, ln1_b, router_w, router_b):
    rw = jnp.pad(router_w[l], ((0, 0), (0, LANES - N_EXPERTS)))
    rb = jnp.pad(router_b[l], (0, LANES - N_EXPERTS), constant_values=NEG)
    return ln1_g[l][None], ln1_b[l][None], rw, rb[None]


def kernel(x_prompt, x_sample, c, c_ctx, cache_diff_k, cache_diff_v, cache_na_k, cache_na_v, cache_gqa_k, cache_gqa_v, w_mod, b_mod, ln1_g, ln1_b, ln2_g, ln2_b, ab_w_in, ab_conv_w, ab_lambda_q1, ab_lambda_k1, ab_lambda_q2, ab_lambda_k2, ab_subln_g, ab_w_out, cd_w_in, cd_na_rpb, cd_q_norm_g, cd_k_norm_g, cd_w_out, router_w, router_b, w_gate_up, b_gate_up, w_down, b_down):
    xp = x_prompt.reshape(T_P, D)
    xs = x_sample.reshape(T_S, D)
    cond8 = jnp.concatenate([c_ctx[None], c, jnp.zeros((8 - 1 - B_S, D), F32)], axis=0)
    mod = _modulation(cond8, w_mod, b_mod)
    rope = _rope_tables()
    experts = (w_gate_up, b_gate_up, w_down, b_down)

    gb, y, q, k, v, new_diff_k, new_diff_v = _ab_in(xp, xs, mod, ab_w_in[0].astype(BF16), rope)
    lam_init = 0.8 - 0.6 * 1.0
    diff = (jnp.stack([ab_lambda_q1[0], ab_lambda_k1[0], ab_lambda_q2[0], ab_lambda_k2[0]]), ab_subln_g[0][None])
    attn = _flash_pair(q, k, v, latent=False, out_cols=512, col0=0, post="diff", diff=diff, lam_init=lam_init)
    attn = _flash_pair(q, k, v, latent=True, out_cols=512, col0=0, post="diff", diff=diff, lam_init=lam_init,
                       ctx=(cache_diff_k, cache_diff_v), ctx_mode="wide", prev=attn)
    routed = _ab_out(xp, xs, gb, y, ab_conv_w[0], attn, ab_w_out[0].astype(BF16), mod,
                     _router_tail(0, ln1_g, ln1_b, router_w, router_b))
    x = _moe_layer(0, routed, experts, mod, ln2_g, ln2_b)

    qg = jnp.tile(cd_q_norm_g[0], 2)[None]
    kg = jnp.tile(cd_k_norm_g[0], 2)[None]
    nq, nk, nv, gq, gk, gv, new_na_k, new_na_v, new_gqa_k, new_gqa_v = _cd_in(
        x, mod, cd_w_in[0].astype(BF16), qg, kg, rope)
    merged = _flash_pair(nq, nk, nv, latent=False, out_cols=D, col0=0, post="select")
    merged = _flash_pair(gq, gk, gv, latent=False, out_cols=D, col0=2, post="select", prev=merged)
    merged = _na_latent(nq, nk, nv, cache_na_k, cache_na_v, _na_bias(cd_na_rpb[0]), merged)
    merged = _flash_pair(gq, gk, gv, latent=True, out_cols=D, col0=2, post="select",
                         ctx=(cache_gqa_k, cache_gqa_v), ctx_mode="dup", prev=merged)
    routed = _cd_out(x, merged, cd_w_out[0].astype(BF16), mod,
                     _router_tail(1, ln1_g, ln1_b, router_w, router_b))
    y_p, y_s = _moe_layer(1, routed, experts, mod, ln2_g, ln2_b)

    return (y_p.reshape(B_P, N_P, D), y_s.reshape(B_S, N_S, D), new_diff_k, new_diff_v,
            new_na_k, new_na_v, new_gqa_k, new_gqa_v)
```

```python
import jax
import jax.numpy as jnp
import numpy as np
from jax import lax
from jax.experimental import pallas as pl
from jax.experimental.pallas import tpu as pltpu

F32 = jnp.float32
BF16 = jnp.bfloat16

D = 1024
B_P, N_P = 16, 256
B_S, N_S = 2, 4096
PAST = 256
T_P, T_S = B_P * N_P, B_S * N_S
T = T_P + T_S
TM = 256
NT_P, NT_S, NT = T_P // TM, T_S // TM, T // TM
TILES_PER_GRID = N_S // TM
GRID_W = 64
GRID_H = N_S // GRID_W
HD = 64
DEPTH = 2
N_EXPERTS = 32
TOP_K = 4
D_FF = 1024
NA_WIN_R, NA_WIN_C = 8, 16
NA_QROWS = 4
NA_KROWS = 12
SWIGLU_LIMIT = 7.0
SWIGLU_ALPHA = 1.702
ROPE_THETA = 10000.0
DEEPNORM_ALPHA = (2 * DEPTH) ** 0.25
LN_EPS = 1e-5
RMS_EPS = 1e-6
LOG2E = 1.4426950408889634
QK_SCALE = HD ** -0.5 * LOG2E
NEG = -1e30
MOE_TM = 512
MOE_PAIRS = TOP_K * T
MOE_TILES = MOE_PAIRS // MOE_TM + N_EXPERTS
PAIRS_TM = TOP_K * TM
RUN_SIZES = tuple(TM >> b for b in range(TM.bit_length()))
LANES = 128
SUB = 8

VMEM_LIMIT = 56 * 1024 * 1024


def _params(*sem):
    return pltpu.CompilerParams(dimension_semantics=sem, vmem_limit_bytes=VMEM_LIMIT)


def _split(x):
    hi = x.astype(BF16)
    lo = (x - hi.astype(F32)).astype(BF16)
    return hi, lo


def _dot(a, b):
    return jnp.dot(a, b, preferred_element_type=F32)


def _dot3(a, b):
    ah, al = _split(a)
    bh, bl = _split(b)
    return _dot(ah, bh) + (_dot(ah, bl) + _dot(al, bh))


def _cond_row(i):
    return jnp.where(i < NT_P, 0, 1 + (i - NT_P) // TILES_PER_GRID)


def _layer_norm(z, g, b):
    mu = jnp.mean(z, axis=-1, keepdims=True)
    zc = z - mu
    var = jnp.mean(zc * zc, axis=-1, keepdims=True)
    return zc * lax.rsqrt(var + LN_EPS) * g + b


def _low_half(rows):
    return lax.broadcasted_iota(jnp.int32, (rows, LANES), 1) < HD


def _mod_kernel(c_ref, w_ref, b_ref, o_ref):
    c = c_ref[...]
    o_ref[...] = _dot3(c * jax.nn.sigmoid(c), w_ref[...]) + b_ref[...]


def _modulation(cond8, w_mod, b_mod):
    return pl.pallas_call(
        _mod_kernel,
        grid=(DEPTH, 6),
        in_specs=[pl.BlockSpec((8, D), lambda l, j: (0, 0)),
                  pl.BlockSpec((None, D, D), lambda l, j: (l, 0, j)),
                  pl.BlockSpec((None, 1, D), lambda l, j: (l, 0, j))],
        out_specs=pl.BlockSpec((None, 8, D), lambda l, j: (l, 0, j)),
        out_shape=jax.ShapeDtypeStruct((DEPTH, 8, 6 * D), F32),
        compiler_params=_params("arbitrary", "arbitrary"),
        name="modulation",
    )(cond8, w_mod, b_mod.reshape(DEPTH, 1, 6 * D))


def _mod_spec(layer, chunk):
    return pl.BlockSpec((None, 8, D), lambda i, _l=layer, _c=chunk: (_l, 0, _c))


def _full(shape):
    return pl.BlockSpec(shape, lambda i: (0,) * len(shape))


def _rope_tables():
    t = np.arange(N_S)
    half = HD // 2
    inv = ROPE_THETA ** (-np.arange(0, half, 2, dtype=np.float64) / half)
    inv_lane = np.tile(np.repeat(inv, 2), 2 * LANES // HD)
    lane = np.arange(LANES)
    by_row = (lane % HD) < half
    ang = np.where(by_row[None], (t // GRID_W)[:, None], (t % GRID_W)[:, None]) * inv_lane[None]
    cos, sin = np.cos(ang), np.sin(ang)
    even = (lane % 2) == 0
    return tuple(jnp.asarray(a, F32) for a in (cos, np.where(even, -sin, 0.0), np.where(even, 0.0, sin)))


def _rope(x, a, b, c):
    return x * a + pltpu.roll(x, LANES - 1, axis=1) * b + pltpu.roll(x, 1, axis=1) * c


def _rope_or_identity(identity, ra_ref, rb_ref, rc_ref):
    return (jnp.where(identity, 1.0, ra_ref[...]), jnp.where(identity, 0.0, rb_ref[...]),
            jnp.where(identity, 0.0, rc_ref[...]))


def _rope_spec():
    return pl.BlockSpec((TM, LANES), lambda i: (jnp.maximum(i - NT_P, 0) % TILES_PER_GRID, 0))


def _x_specs():
    return [pl.BlockSpec((TM, D), lambda i: (jnp.minimum(i, NT_P - 1), 0)),
            pl.BlockSpec((TM, D), lambda i: (jnp.maximum(i - NT_P, 0), 0))]


def _cache_spec(heads, width):
    return pl.BlockSpec((None, None, heads, N_P, width), lambda i: (jnp.minimum(i, NT_P - 1), 0, 0, 0, 0))


def _row_spec(width):
    return pl.BlockSpec((TM, width), lambda i: (i, 0))


def _hm_spec(n):
    return pl.BlockSpec((n, TM, LANES), lambda i: (0, i, 0))


AB_Q0, AB_K0, AB_V0 = 1536, 2048, 2560


def _tile(p, col0, j):
    return p[:, col0 + j * LANES:col0 + (j + 1) * LANES]


def _ab_in_kernel(xp_ref, xs_ref, sh_ref, sc_ref, w_ref, ra_ref, rb_ref, rc_ref,
                  gb_ref, y_ref, q_ref, k_ref, v_ref, kc_ref, vc_ref):
    i = pl.program_id(0)
    is_p = i < NT_P
    ci = _cond_row(i)
    x = jnp.where(is_p, xp_ref[...], xs_ref[...])
    u = x * (1.0 + sc_ref[pl.ds(ci, 1), :]) + sh_ref[pl.ds(ci, 1), :]
    p = _dot(u.astype(BF16), w_ref[...])
    gb_ref[...] = p[:, 0:512]
    y_ref[...] = p[:, 512:1024] * p[:, 1024:1536]
    a, b, c = _rope_or_identity(is_p, ra_ref, rb_ref, rc_ref)
    for h in range(4):
        v_ref[h] = _tile(p, AB_V0, h).astype(BF16)
        q_ref[h] = (_rope(_tile(p, AB_Q0, h), a, b, c) * QK_SCALE).astype(BF16)
        k_ref[h] = _rope(_tile(p, AB_K0, h), a, b, c).astype(BF16)

    @pl.when(is_p)
    def _():
        for h in range(4):
            kc_ref[h] = _tile(p, AB_K0, h)
            vc_ref[h] = _tile(p, AB_V0, h)


def _ab_in(xp, xs, mod, w_in, rope):
    hm = jax.ShapeDtypeStruct((4, T, LANES), BF16)
    cache = jax.ShapeDtypeStruct((B_P, 1, 4, N_P, LANES), F32)
    half = jax.ShapeDtypeStruct((T, 512), F32)
    return pl.pallas_call(
        _ab_in_kernel,
        grid=(NT,),
        in_specs=_x_specs() + [_mod_spec(0, 0), _mod_spec(0, 1), _full((D, 3072)),
                               _rope_spec(), _rope_spec(), _rope_spec()],
        out_specs=[_row_spec(512), _row_spec(512), _hm_spec(4), _hm_spec(4), _hm_spec(4),
                   _cache_spec(4, LANES), _cache_spec(4, LANES)],
        out_shape=[half, half, hm, hm, hm, cache, cache],
        compiler_params=_params("arbitrary"),
        name="ab_in_proj",
    )(xp, xs, mod, mod, w_in, *rope)


CD_NQ, CD_NK, CD_NV, CD_GQ, CD_GK, CD_GV = 0, 512, 1024, 1536, 2048, 2176


def _seg_mean64(s):
    r = lax.broadcasted_iota(jnp.int32, (LANES, LANES), 0) // HD
    c = lax.broadcasted_iota(jnp.int32, (LANES, LANES), 1) // HD
    seg = jnp.where(r == c, 1.0, 0.0).astype(BF16)
    hi, lo = _split(s)
    return (_dot(hi, seg) + _dot(lo, seg)) * (1.0 / HD)


def _rms64(x, g):
    return x * lax.rsqrt(_seg_mean64(x * x) + RMS_EPS) * g


def _dup_halves(x, lo):
    r = pltpu.roll(x, HD, axis=1)
    return jnp.where(lo, x, r), jnp.where(lo, r, x)


def _cd_in_kernel(x_ref, sh_ref, sc_ref, w_ref, qg_ref, kg_ref, ra_ref, rb_ref, rc_ref,
                  nq_ref, nk_ref, nv_ref, gq_ref, gk_ref, gv_ref,
                  nkc_ref, nvc_ref, gkc_ref, gvc_ref):
    i = pl.program_id(0)
    is_p = i < NT_P
    ci = _cond_row(i)
    u = x_ref[...] * (1.0 + sc_ref[pl.ds(ci, 1), :]) + sh_ref[pl.ds(ci, 1), :]
    p = _dot(u.astype(BF16), w_ref[...])
    lo = _low_half(TM)
    for j in range(4):
        nq_ref[j] = (_tile(p, CD_NQ, j) * QK_SCALE).astype(BF16)
        nk_ref[j] = _tile(p, CD_NK, j).astype(BF16)
        nv_ref[j] = _tile(p, CD_NV, j).astype(BF16)
    gq = [_rms64(_tile(p, CD_GQ, j), qg_ref[...]) for j in range(4)]
    gk = _rms64(_tile(p, CD_GK, 0), kg_ref[...])
    gv = _tile(p, CD_GV, 0)
    v0, v1 = _dup_halves(gv, lo)
    gv_ref[0] = v0.astype(BF16)
    gv_ref[1] = v1.astype(BF16)
    a, b, c = _rope_or_identity(is_p, ra_ref, rb_ref, rc_ref)
    for j in range(4):
        gq_ref[j] = (_rope(gq[j], a, b, c) * QK_SCALE).astype(BF16)
    k0, k1 = _dup_halves(_rope(gk, a, b, c), lo)
    gk_ref[0] = k0.astype(BF16)
    gk_ref[1] = k1.astype(BF16)

    @pl.when(is_p)
    def _():
        gkc_ref[0] = k0[:, 0:HD]
        gkc_ref[1] = k1[:, 0:HD]
        gvc_ref[0] = v0[:, 0:HD]
        gvc_ref[1] = v1[:, 0:HD]
        for j in range(4):
            for src, dst in ((CD_NK, nkc_ref), (CD_NV, nvc_ref)):
                a, b = _dup_halves(_tile(p, src, j), lo)
                dst[2 * j] = a[:, 0:HD]
                dst[2 * j + 1] = b[:, 0:HD]


def _cd_in(x, mod, w_in, qg, kg, rope):
    hm4 = jax.ShapeDtypeStruct((4, T, LANES), BF16)
    hm2 = jax.ShapeDtypeStruct((2, T, LANES), BF16)
    c8 = jax.ShapeDtypeStruct((B_P, 1, 8, N_P, HD), F32)
    c2 = jax.ShapeDtypeStruct((B_P, 1, 2, N_P, HD), F32)
    return pl.pallas_call(
        _cd_in_kernel,
        grid=(NT,),
        in_specs=[_row_spec(D), _mod_spec(1, 0), _mod_spec(1, 1), _full((D, 2304)),
                  _full((1, LANES)), _full((1, LANES)), _rope_spec(), _rope_spec(), _rope_spec()],
        out_specs=[_hm_spec(4), _hm_spec(4), _hm_spec(4), _hm_spec(4), _hm_spec(2), _hm_spec(2),
                   _cache_spec(8, HD), _cache_spec(8, HD), _cache_spec(2, HD), _cache_spec(2, HD)],
        out_shape=[hm4, hm4, hm4, hm4, hm2, hm2, c8, c8, c2, c2],
        compiler_params=_params("arbitrary"),
        name="cd_in_proj",
    )(x, mod, mod, w_in, qg, kg, *rope)


def _stack_pairs(q_ref, n_q, tq):
    lo = _low_half(tq)
    parts = []
    for j in range(n_q):
        q = q_ref[j]
        zero = jnp.zeros_like(q)
        parts += [jnp.where(lo, q, zero), jnp.where(lo, zero, q)]
    return jnp.concatenate(parts, axis=0), lo


def _qk(qs, kb):
    return lax.dot_general(qs, kb, (((1,), (1,)), ((), ())), preferred_element_type=F32)


def _ctx_tile(ref, mode):
    if mode == "wide":
        x = ref[...]
    elif mode == "pair":
        x = jnp.concatenate([ref[0], ref[1]], axis=1)
    else:
        x = jnp.concatenate([ref[...], ref[...]], axis=1)
    return x.astype(BF16)


def _with_ones(v):
    lane = lax.broadcasted_iota(jnp.int32, v.shape, 1)
    return jnp.concatenate([v, jnp.where(lane == 0, 1.0, 0.0).astype(v.dtype)], axis=1)


def _flash_pair_kernel(*refs, gs, ctx_mode, **kw):
    it = iter(refs)
    q_ref, k_ref, v_ref = next(it), next(it), next(it)
    ctx = [next(it), next(it)] if ctx_mode else []
    rest = list(it)
    o_ref = rest.pop()
    n_q = q_ref.shape[0] // gs
    for t in range(gs * n_q):
        gi = t // n_q
        _flash_group(q_ref.at[t:t + 1], k_ref.at[gi], v_ref.at[gi], *[r.at[gi] for r in ctx], *rest,
                     o_ref.at[:, t * LANES:(t + 1) * LANES], n_q=1, ctx_mode=ctx_mode, **kw)


def _flash_group(*refs, n_q, tq, nk, tk, ctx_mode, post, lam_init):
    it = iter(refs)
    q_ref, k_ref, v_ref = next(it), next(it), next(it)
    kc_ref, vc_ref = (next(it), next(it)) if ctx_mode else (None, None)
    lam_ref, g_ref = (next(it), next(it)) if post == "diff" else (None, None)
    o_ref = next(it)

    qs, lo = _stack_pairs(q_ref, n_q, tq)
    rows = 2 * n_q * tq

    def step(kb, vb, carry):
        m, acc = carry
        s = _qk(qs, kb)
        m_new = jnp.maximum(m, jnp.max(s, axis=1, keepdims=True))
        p = jnp.exp2((s - m_new).astype(BF16))
        return m_new, jnp.exp2(m - m_new) * acc + _dot(p, _with_ones(vb))

    carry = (jnp.full((rows, 1), NEG, F32), jnp.zeros((rows, 2 * LANES), F32))
    for c in range(nk // tk):
        carry = step(k_ref[c * tk:(c + 1) * tk, :], v_ref[c * tk:(c + 1) * tk, :], carry)
    if ctx_mode:
        carry = step(_ctx_tile(kc_ref, ctx_mode), _ctx_tile(vc_ref, ctx_mode), carry)
    _, acc = carry
    o = acc[:, 0:LANES] / acc[:, LANES:LANES + 1]

    if post == "diff":
        lp = lam_ref[...]
        lam = (jnp.exp(jnp.sum(lp[0:1] * lp[1:2], axis=1, keepdims=True))
               - jnp.exp(jnp.sum(lp[2:3] * lp[3:4], axis=1, keepdims=True)) + lam_init)
        a = o[0:tq] - lam * o[tq:2 * tq]
        ms = jnp.mean(a * a, axis=-1, keepdims=True)
        o_ref[...] = (a * lax.rsqrt(ms + RMS_EPS) * g_ref[...] * (1.0 - lam_init)).astype(o_ref.dtype)
    else:
        for j in range(n_q):
            o_ref[:, j * LANES:(j + 1) * LANES] = jnp.where(
                lo, o[2 * j * tq:(2 * j + 1) * tq], o[(2 * j + 1) * tq:(2 * j + 2) * tq]).astype(o_ref.dtype)


def _flash_pair(q, k, v, *, latent, out_cols, col0, post, ctx=None, ctx_mode=None, diff=None,
                lam_init=0.0, prev=None):
    groups = k.shape[0]
    n_q = q.shape[0] // groups
    if latent:
        tq, nk, tk, gs = TM, N_S, 1024, 2
        grid = (B_S, groups // gs, TILES_PER_GRID)
        qrow = lambda b, g, i: NT_P + b * TILES_PER_GRID + i
        krow = lambda b, g, i: T_P // N_S + b
    else:
        tq, nk, tk, gs = N_P, N_P, N_P, groups
        grid = (B_P, 1, 1)
        qrow = lambda b, g, i: b
        krow = lambda b, g, i: b
    in_specs = [pl.BlockSpec((gs * n_q, tq, LANES), lambda b, g, i: (g, qrow(b, g, i), 0)),
                pl.BlockSpec((gs, nk, LANES), lambda b, g, i: (g, krow(b, g, i), 0)),
                pl.BlockSpec((gs, nk, LANES), lambda b, g, i: (g, krow(b, g, i), 0))]
    args = [q, k, v]
    if ctx is not None:
        width = LANES if ctx_mode == "wide" else HD
        spec = pl.BlockSpec((None, None, gs, PAST, width), lambda b, g, i: (b, 0, g, 0, 0))
        in_specs += [spec, spec]
        args += list(ctx)
    if diff is not None:
        in_specs += [pl.BlockSpec((4, HD), lambda b, g, i: (0, 0)),
                     pl.BlockSpec((1, LANES), lambda b, g, i: (0, 0))]
        args += list(diff)
    aliases = {}
    if prev is not None:
        aliases = {len(args): 0}
        in_specs.append(pl.BlockSpec(memory_space=pl.ANY))
        args.append(prev)

    def kernel(*refs):
        if prev is not None:
            refs = refs[:-2] + refs[-1:]
        _flash_pair_kernel(*refs, gs=gs, tq=tq, nk=nk, tk=tk, ctx_mode=ctx_mode if ctx is not None else None,
                           post=post, lam_init=lam_init)

    return pl.pallas_call(
        kernel,
        grid=grid,
        in_specs=in_specs,
        out_specs=pl.BlockSpec((tq, gs * n_q * LANES), lambda b, g, i: (qrow(b, g, i), col0 // gs + g)),
        out_shape=jax.ShapeDtypeStruct((T, out_cols), BF16),
        input_output_aliases=aliases,
        compiler_params=_params("arbitrary", "arbitrary", "arbitrary"),
        name="attn_" + post + ("_latent" if latent else "_context") + str(n_q),
    )(*args)


NA_TQ = NA_QROWS * GRID_W
NA_TK = NA_KROWS * GRID_W


NA_BLOCK_POS = ((0, 0), (NA_QROWS, 0), (GRID_H - NA_QROWS, GRID_H - NA_KROWS))
N_DR = 2 * NA_WIN_R - 1
N_DC = 2 * NA_WIN_C - 1


def _na_bias_kernel(rpb_ref, o_ref):
    qc = lax.broadcasted_iota(jnp.int32, (GRID_W, LANES), 0)
    lane = lax.broadcasted_iota(jnp.int32, (GRID_W, LANES), 1)
    kc = lane % GRID_W
    cs = jnp.clip(qc - NA_WIN_C // 2, 0, GRID_W - NA_WIN_C)
    col_ok = jnp.logical_and(kc >= cs, kc < cs + NA_WIN_C)
    lo = lane < GRID_W
    neg = jnp.full((GRID_W, LANES), NEG, F32)
    for head in range(2):
        toeplitz = []
        for dr in range(N_DR):
            r = jnp.broadcast_to(rpb_ref[head, dr:dr + 1, :] * LOG2E, (GRID_W, LANES))
            t = jnp.where(lo, pltpu.roll(r, LANES - (NA_WIN_C - 1), axis=1, stride=1, stride_axis=0),
                          pltpu.roll(r, GRID_W - (NA_WIN_C - 1), axis=1, stride=1, stride_axis=0))
            toeplitz.append(jnp.where(col_ok, t, neg))
        for pos, (r0, k0) in enumerate(NA_BLOCK_POS):
            for i in range(NA_QROWS):
                qr = r0 + i
                rs = min(max(qr - NA_WIN_R // 2, 0), GRID_H - NA_WIN_R)
                tiles = [toeplitz[k0 + j - qr + NA_WIN_R - 1] if rs <= k0 + j < rs + NA_WIN_R else neg
                         for j in range(NA_KROWS)]
                for jp in range(NA_KROWS // 2):
                    o_ref[pos, pl.ds(head * NA_TQ + i * GRID_W, GRID_W), pl.ds(jp * LANES, LANES)] = jnp.where(
                        lo, tiles[2 * jp], tiles[2 * jp + 1])


def _na_bias(rpb):
    rpb_pad = jnp.pad(rpb, ((0, 0), (0, 16 - N_DR), (0, LANES - N_DC)), constant_values=NEG)
    return pl.pallas_call(
        _na_bias_kernel,
        grid=(4,),
        in_specs=[pl.BlockSpec((2, 16, LANES), lambda g: (g, 0, 0))],
        out_specs=pl.BlockSpec((3, None, 2 * NA_TQ, NA_TK), lambda g: (0, g, 0, 0)),
        out_shape=jax.ShapeDtypeStruct((3, 4, 2 * NA_TQ, NA_TK), F32),
        compiler_params=_params("arbitrary"),
        name="na_bias_table",
    )(rpb_pad)


NA_PAIRS = 4


def _na_kernel(q_ref, k_ref, v_ref, kc_ref, vc_ref, bm_ref, _, o_ref):
    i = pl.program_id(2)
    k0 = jnp.clip(i * NA_QROWS - NA_WIN_R // 2, 0, GRID_H - NA_KROWS)
    start = pl.multiple_of(k0 * GRID_W, GRID_W)
    for g in range(NA_PAIRS):
        qs, lo = _stack_pairs(q_ref.at[g:g + 1], 1, NA_TQ)
        kw = k_ref[g, pl.ds(start, NA_TK), :]
        vw = v_ref[g, pl.ds(start, NA_TK), :]
        s_w = _qk(qs, kw) + bm_ref[g]
        s_c = _qk(qs, _ctx_tile(kc_ref.at[2 * g:2 * g + 2], "pair"))
        m = jnp.maximum(jnp.max(s_w, axis=1, keepdims=True), jnp.max(s_c, axis=1, keepdims=True))
        p_w = jnp.exp2((s_w - m).astype(BF16))
        p_c = jnp.exp2((s_c - m).astype(BF16))
        acc = _dot(p_w, _with_ones(vw)) + _dot(p_c, _with_ones(_ctx_tile(vc_ref.at[2 * g:2 * g + 2], "pair")))
        o = acc[:, 0:LANES] / acc[:, LANES:LANES + 1]
        o_ref[:, g * LANES:(g + 1) * LANES] = jnp.where(lo, o[0:NA_TQ], o[NA_TQ:2 * NA_TQ]).astype(o_ref.dtype)


def _na_latent(q, k, v, kc, vc, bm, prev):
    nblk = N_S // NA_TQ
    qrow = lambda b, g, i: T_P // NA_TQ + b * nblk + i
    krow = lambda b, g, i: T_P // N_S + b
    cfg = lambda i: jnp.where(i == 0, 0, jnp.where(i == nblk - 1, 2, 1))
    ctx_spec = pl.BlockSpec((None, None, 2 * NA_PAIRS, PAST, HD), lambda b, g, i: (b, 0, g, 0, 0))
    return pl.pallas_call(
        _na_kernel,
        grid=(B_S, 4 // NA_PAIRS, nblk),
        in_specs=[pl.BlockSpec((NA_PAIRS, NA_TQ, LANES), lambda b, g, i: (g, qrow(b, g, i), 0)),
                  pl.BlockSpec((NA_PAIRS, N_S, LANES), lambda b, g, i: (g, krow(b, g, i), 0)),
                  pl.BlockSpec((NA_PAIRS, N_S, LANES), lambda b, g, i: (g, krow(b, g, i), 0)),
                  ctx_spec, ctx_spec,
                  pl.BlockSpec((None, NA_PAIRS, 2 * NA_TQ, NA_TK), lambda b, g, i: (cfg(i), g, 0, 0)),
                  pl.BlockSpec(memory_space=pl.ANY)],
        out_specs=pl.BlockSpec((NA_TQ, NA_PAIRS * LANES), lambda b, g, i: (qrow(b, g, i), g)),
        out_shape=jax.ShapeDtypeStruct((T, D), BF16),
        input_output_aliases={6: 0},
        compiler_params=_params("arbitrary", "arbitrary", "arbitrary"),
        name="attn_window_latent",
    )(q, k, v, kc, vc, bm, prev)


def _top4(logits):
    lane = lax.broadcasted_iota(jnp.int32, logits.shape, 1).astype(F32)
    rest = logits
    tops, firsts = [], []
    for _ in range(TOP_K):
        m = jnp.max(rest, axis=1, keepdims=True)
        first = jnp.min(jnp.where(rest == m, lane, float(LANES)), axis=1, keepdims=True)
        tops.append(m)
        firsts.append(first)
        rest = jnp.where(lane == first, -jnp.inf, rest)
    es = [jnp.exp(m - tops[0]) for m in tops]
    denom = es[0] + es[1] + es[2] + es[3]
    w = jnp.zeros_like(logits)
    for k in range(TOP_K):
        w = jnp.where(lane == float(k), es[k] / denom, w)
    return firsts, w


def _local_sort(firsts):
    lane = lax.broadcasted_iota(jnp.int32, (TM, LANES), 1).astype(F32)
    hots = [lane == f for f in firsts]
    sel = jnp.zeros((TM, LANES), F32)
    for hot in hots:
        sel = jnp.where(hot, 1.0, sel)
    r = lax.broadcasted_iota(jnp.int32, (TM, TM), 0)
    c = lax.broadcasted_iota(jnp.int32, (TM, TM), 1)
    earlier = _dot(jnp.where(c < r, 1.0, 0.0).astype(BF16), sel.astype(BF16))
    cnt = jnp.sum(sel, axis=0, keepdims=True)
    r = lax.broadcasted_iota(jnp.int32, (LANES, LANES), 0)
    c = lax.broadcasted_iota(jnp.int32, (LANES, LANES), 1)
    start = _dot(jnp.broadcast_to(cnt, (SUB, LANES)).astype(BF16), jnp.where(r < c, 1.0, 0.0).astype(BF16))[0:1]
    place = start + earlier
    pos = jnp.zeros((TM, LANES), F32)
    for k, hot in enumerate(hots):
        pos = jnp.where(lane == float(k), jnp.sum(jnp.where(hot, place, 0.0), axis=1, keepdims=True), pos)
    return pos.astype(jnp.int32), cnt.astype(jnp.int32)


def _mixer_tail(x_of, h_of, ci, g1_ref, sh2_ref, sc2_ref, lng_ref, lnb_ref, rw_ref, rb_ref,
                x1_ref, u2_ref, pos_ref, w_ref, cnt_ref):
    rows = slice(0, TM)
    x1 = _layer_norm(DEEPNORM_ALPHA * x_of(rows) + g1_ref[pl.ds(ci, 1), :] * h_of(rows), lng_ref[...], lnb_ref[...])
    x1_ref[...] = x1
    u2 = x1 * (1.0 + sc2_ref[pl.ds(ci, 1), :]) + sh2_ref[pl.ds(ci, 1), :]
    u2_ref[...] = u2.astype(BF16)
    firsts, w_ref[...] = _top4(_dot3(u2, rw_ref[...]) + rb_ref[...])
    pos, cnt = _local_sort(firsts)
    pos_ref[...] = pos
    cnt_ref[...] = jnp.broadcast_to(cnt, (SUB, LANES))


def _ab_out_kernel(xp_ref, xs_ref, gb_ref, y_ref, yp_ref, yn_ref, cv_ref, at_ref, wo_ref,
                   g1_ref, sh2_ref, sc2_ref, lng_ref, lnb_ref, rw_ref, rb_ref, *out_refs):
    i = pl.program_id(0)
    is_p = i < NT_P
    ci = _cond_row(i)
    j = (i - NT_P) % TILES_PER_GRID
    first = jnp.logical_or(is_p, j == 0)
    last = jnp.logical_or(is_p, j == TILES_PER_GRID - 1)
    y = y_ref[...]
    row = lax.broadcasted_iota(jnp.int32, y.shape, 0)
    before = jnp.where(first, 0.0, yp_ref[7:8, :])
    after = jnp.where(last, 0.0, yn_ref[0:1, :])
    y_prev = jnp.where(row == 0, before, pltpu.roll(y, 1, axis=0))
    y_next = jnp.where(row == TM - 1, after, pltpu.roll(y, TM - 1, axis=0))
    cv = cv_ref[...]
    conv = (gb_ref[...] * (y_prev * cv[0:1] + y * cv[1:2] + y_next * cv[2:3])).astype(BF16)
    _mixer_tail(lambda r: jnp.where(is_p, xp_ref[r, :], xs_ref[r, :]),
                lambda r: _dot(conv[r], wo_ref[0:512, :]) + _dot(at_ref[r, :], wo_ref[512:1024, :]),
                ci, g1_ref, sh2_ref, sc2_ref, lng_ref, lnb_ref, rw_ref, rb_ref, *out_refs)


def _cd_out_kernel(x_ref, mg_ref, wo_ref, g1_ref, sh2_ref, sc2_ref, lng_ref, lnb_ref, rw_ref, rb_ref,
                   *out_refs):
    ci = _cond_row(pl.program_id(0))
    _mixer_tail(lambda r: x_ref[r, :], lambda r: _dot(mg_ref[r, :], wo_ref[...]),
                ci, g1_ref, sh2_ref, sc2_ref, lng_ref, lnb_ref, rw_ref, rb_ref, *out_refs)


def _tail_specs(layer):
    return [_mod_spec(layer, 2), _mod_spec(layer, 3), _mod_spec(layer, 4),
            _full((1, D)), _full((1, D)), _full((D, LANES)), _full((1, LANES))]


_TAIL_OUT_SPECS = [_row_spec(D), _row_spec(D), _row_spec(LANES), _row_spec(LANES),
                   pl.BlockSpec((None, SUB, LANES), lambda i: (i, 0, 0))]
_TAIL_OUT_SHAPES = [jax.ShapeDtypeStruct((T, D), F32), jax.ShapeDtypeStruct((T, D), BF16),
                    jax.ShapeDtypeStruct((T, LANES), jnp.int32), jax.ShapeDtypeStruct((T, LANES), F32),
                    jax.ShapeDtypeStruct((NT, SUB, LANES), jnp.int32)]


def _ab_out(xp, xs, gb, y, conv_w, attn, w_out, mod, tail):
    halo_prev = pl.BlockSpec((8, 512), lambda i: (jnp.maximum(i * (TM // 8) - 1, 0), 0))
    halo_next = pl.BlockSpec((8, 512), lambda i: (jnp.minimum((i + 1) * (TM // 8), T // 8 - 1), 0))
    return pl.pallas_call(
        _ab_out_kernel,
        grid=(NT,),
        in_specs=_x_specs() + [_row_spec(512), _row_spec(512), halo_prev, halo_next, _full((3, 512)),
                               _row_spec(512), _full((D, D))] + _tail_specs(0),
        out_specs=_TAIL_OUT_SPECS,
        out_shape=_TAIL_OUT_SHAPES,
        compiler_params=_params("arbitrary"),
        name="ab_out_proj",
    )(xp, xs, gb, y, y, y, conv_w, attn, w_out, mod, mod, mod, *tail)


def _cd_out(x, merged, w_out, mod, tail):
    return pl.pallas_call(
        _cd_out_kernel,
        grid=(NT,),
        in_specs=[_row_spec(D), _row_spec(D), _full((D, D))] + _tail_specs(1),
        out_specs=_TAIL_OUT_SPECS,
        out_shape=_TAIL_OUT_SHAPES,
        compiler_params=_params("arbitrary"),
        name="cd_out_proj",
    )(x, merged, w_out, mod, mod, mod, *tail)


def _mix_down(wd_ref, wmix_ref):
    half = D_FF // 2
    for c in range(D // LANES):
        wmix_ref[c, pl.ds(0, half, stride=2), :] = wd_ref[0:half, c * LANES:(c + 1) * LANES]
        wmix_ref[c, pl.ds(1, half, stride=2), :] = wd_ref[half:D_FF, c * LANES:(c + 1) * LANES]


def _prep_expert(wgu_ref, wd_ref, wgu_bf, wmix_ref, wd_bf):
    for c in range(4):
        wgu_bf[:, c * 512:(c + 1) * 512] = wgu_ref[:, c * 512:(c + 1) * 512].astype(BF16)
    _mix_down(wd_ref, wmix_ref)
    for c in range(D // LANES):
        wd_bf[:, c * LANES:(c + 1) * LANES] = wmix_ref[c].astype(BF16)


def _expert_ffn(u, wgu_bf, bgu_ref, wd_bf, bd_ref):
    rows = u.shape[0]
    ga = _dot(u, wgu_bf[:, 0:D_FF]) + bgu_ref[:, 0:D_FF]
    gb = _dot(u, wgu_bf[:, D_FF:2 * D_FF]) + bgu_ref[:, D_FF:2 * D_FF]
    even = (lax.broadcasted_iota(jnp.int32, (rows, LANES), 1) % 2) == 0
    hid = []
    for c in range(D_FF // LANES):
        a = ga[:, c * LANES:(c + 1) * LANES]
        b = gb[:, c * LANES:(c + 1) * LANES]
        gate = jnp.where(even, a, pltpu.roll(b, 1, axis=1))
        up = jnp.where(even, pltpu.roll(a, LANES - 1, axis=1), b)
        gate = jnp.minimum(gate, SWIGLU_LIMIT)
        up = jnp.clip(up, -SWIGLU_LIMIT, SWIGLU_LIMIT)
        hid.append(((up + 1.0) * gate * jax.nn.sigmoid(SWIGLU_ALPHA * gate)).astype(BF16))
    hid = jnp.concatenate(hid, axis=1)
    return _dot(hid, wd_bf[...]) + bd_ref[...]


def _moe_plan(cnt):
    c = cnt[:, 0, :N_EXPERTS]
    counts = jnp.sum(c, axis=0)
    tiles_e = (counts + MOE_TM - 1) // MOE_TM
    tile_end = jnp.cumsum(tiles_e)
    first_row = (tile_end - tiles_e) * MOE_TM
    run_start = first_row[None] + jnp.cumsum(c, axis=0) - c
    j = jnp.arange(MOE_TILES, dtype=jnp.int32)
    tile_expert = jnp.minimum(jnp.sum((j[:, None] >= tile_end[None]).astype(jnp.int32), axis=1), N_EXPERTS - 1)
    i32 = lambda a: a.astype(jnp.int32)
    experts = jnp.arange(N_EXPERTS, dtype=jnp.int32)
    owns = tiles_e > 0
    seg = jnp.cumsum(owns.astype(jnp.int32)) - owns.astype(jnp.int32)
    later = jnp.where((experts[None] > experts[:, None]) & owns[None], experts[None], N_EXPERTS)
    nxt = jnp.min(later, axis=1)
    nxt = jnp.where(nxt == N_EXPERTS, -1, nxt)
    per_tile = lambda v: jnp.sum(jnp.where(tile_expert[:, None] == experts[None], v[None], 0), axis=1)
    return dict(tile_expert=i32(tile_expert), n_used=i32(tile_end[-1:]),
                seg_parity=i32(per_tile(seg % 2)), next_expert=i32(per_tile(nxt)),
                run_start=i32(run_start)[:, None, :], run_len=i32(c)[:, None, :],
                pad_start=i32(first_row + counts)[None, None, :], pad_len=i32(tiles_e * MOE_TM - counts)[None, None, :])


def _for_each_run(start_ref, len_ref, copy):
    def body(e, local):
        n = len_ref[0, e]
        g = start_ref[0, e]

        def pieces(sizes):
            for size in sizes:
                covered = jnp.bitwise_and(n, -2 * size)

                @pl.when(jnp.bitwise_and(n, size) != 0)
                def _(covered=covered, size=size):
                    copy(local + covered, g + covered, size)

        @pl.when(n >= RUN_SIZES[2])
        def _():
            pieces(RUN_SIZES[:3])

        pieces(RUN_SIZES[3:])
        return local + n
    lax.fori_loop(0, N_EXPERTS, body, jnp.int32(0))


def _rows(ref, start, size):
    start = 0 if isinstance(start, int) and start == 0 else pl.multiple_of(start * SUB, SUB)
    return ref.at[pl.ds(start, size * SUB), :]


def _from_tiles(ref, rows):
    return jnp.concatenate([ref[pl.ds(c, rows, stride=SUB), :] for c in range(SUB)], axis=1)


def _to_tiles(ref, x):
    for c in range(SUB):
        ref[pl.ds(c, x.shape[0], stride=SUB), :] = x[:, c * LANES:(c + 1) * LANES]


def _run_spec(index):
    return pl.BlockSpec((None, 1, N_EXPERTS), index, memory_space=pltpu.SMEM)


def _dispatch_kernel(rs_ref, rl_ref, ps_ref, pn_ref, u_ref, pos_ref, xs_hbm, stage, zeros, sem, zsem):
    i = pl.program_id(0)
    slot = i % 2

    def wait(s):
        pltpu.make_async_copy(stage.at[s], stage.at[s], sem.at[s]).wait()

    @pl.when(i == 0)
    def _():
        zeros[...] = jnp.zeros_like(zeros)

        def pad(local, g, size):
            return pltpu.make_async_copy(_rows(zeros, 0, size), _rows(xs_hbm, g, size), zsem.at[0])
        _for_each_run(ps_ref, pn_ref, lambda *a: pad(*a).start())
        _for_each_run(ps_ref, pn_ref, lambda *a: pad(*a).wait())

    @pl.when(i >= 2)
    def _():
        wait(slot)

    place = pos_ref[...].astype(F32).T
    s = lax.broadcasted_iota(jnp.int32, (PAIRS_TM, TM), 0).astype(F32)
    onehot = jnp.zeros((PAIRS_TM, TM), F32)
    for k in range(TOP_K):
        onehot = jnp.where(s == place[k:k + 1, :], 1.0, onehot)
    _to_tiles(stage.at[slot], _dot(onehot.astype(BF16), u_ref[...]))
    _for_each_run(rs_ref, rl_ref, lambda l, g, size: pltpu.make_async_copy(
        _rows(stage.at[slot], l, size), _rows(xs_hbm, g, size), sem.at[slot]).start())

    @pl.when(i == NT - 1)
    def _():
        wait(1 - slot)
        wait(slot)


def _dispatch(u2, pos, plan):
    return pl.pallas_call(
        _dispatch_kernel,
        grid=(NT,),
        in_specs=[_run_spec(lambda i: (i, 0, 0)), _run_spec(lambda i: (i, 0, 0)),
                  _run_spec(lambda i: (0, 0, 0)), _run_spec(lambda i: (0, 0, 0)),
                  _row_spec(D), _row_spec(LANES)],
        out_specs=pl.BlockSpec(memory_space=pl.ANY),
        out_shape=jax.ShapeDtypeStruct((MOE_TILES * MOE_TM * SUB, LANES), F32),
        scratch_shapes=[pltpu.VMEM((2, PAIRS_TM * SUB, LANES), F32), pltpu.VMEM((TM * SUB, LANES), F32),
                        pltpu.SemaphoreType.DMA((2,)), pltpu.SemaphoreType.DMA((1,))],
        compiler_params=_params("arbitrary"),
        name="moe_dispatch",
    )(plan["run_start"], plan["run_len"], plan["pad_start"], plan["pad_len"], u2, pos)


def _moe_kernel(te_ref, nu_ref, par_ref, nxt_ref, x_ref, bgu_ref, bd_ref, wgu_hbm, wd_hbm, y_ref,
                wgu_f, wd_f, wgu_bf, wmix_ref, wd_bf, sem, *, layer):
    j = pl.program_id(0)
    used = j < nu_ref[0]
    buf = par_ref[j]

    def copies(e, s):
        return (pltpu.make_async_copy(wgu_hbm.at[layer, e], wgu_f.at[s], sem.at[s]),
                pltpu.make_async_copy(wd_hbm.at[layer, e], wd_f.at[s], sem.at[s]))

    @pl.when(jnp.logical_and(used, j == 0))
    def _():
        for cp in copies(te_ref[0], 0):
            cp.start(priority=1)

    @pl.when(jnp.logical_and(used, jnp.logical_or(j == 0, te_ref[j] != te_ref[jnp.maximum(j - 1, 0)])))
    def _():
        for cp in copies(te_ref[j], buf):
            cp.wait()

        @pl.when(nxt_ref[j] >= 0)
        def _():
            for cp in copies(nxt_ref[j], 1 - buf):
                cp.start(priority=1)

        _prep_expert(wgu_f.at[buf], wd_f.at[buf], wgu_bf, wmix_ref, wd_bf)

    @pl.when(used)
    def _():
        x = _from_tiles(x_ref, MOE_TM).astype(BF16)
        _to_tiles(y_ref, _expert_ffn(x, wgu_bf, bgu_ref, wd_bf, bd_ref))


def _moe(layer, xs, plan, w_gate_up, b_gate_up, w_down, b_down):
    rows = pl.BlockSpec((MOE_TM * SUB, LANES), lambda j, te, nu, par, nxt: (jnp.minimum(j, nu[0] - 1), 0))
    grid_spec = pltpu.PrefetchScalarGridSpec(
        num_scalar_prefetch=4,
        grid=(MOE_TILES,),
        in_specs=[rows,
                  pl.BlockSpec((None, None, 1, 2 * D_FF), lambda j, te, nu, par, nxt: (layer, te[j], 0, 0)),
                  pl.BlockSpec((None, None, 1, D), lambda j, te, nu, par, nxt: (layer, te[j], 0, 0)),
                  pl.BlockSpec(memory_space=pl.ANY), pl.BlockSpec(memory_space=pl.ANY)],
        out_specs=rows,
        scratch_shapes=[pltpu.VMEM((2, D, 2 * D_FF), F32), pltpu.VMEM((2, D_FF, D), F32),
                        pltpu.VMEM((D, 2 * D_FF), BF16), pltpu.VMEM((D // LANES, D_FF, LANES), F32),
                        pltpu.VMEM((D_FF, D), BF16), pltpu.SemaphoreType.DMA((2,))])

    def kernel(*refs):
        _moe_kernel(*refs, layer=layer)

    return pl.pallas_call(
        kernel,
        grid_spec=grid_spec,
        out_shape=jax.ShapeDtypeStruct((MOE_TILES * MOE_TM * SUB, LANES), F32),
        compiler_params=_params("arbitrary"),
        name="moe_experts",
    )(plan["tile_expert"], plan["n_used"], plan["seg_parity"], plan["next_expert"], xs,
      b_gate_up.reshape(DEPTH, N_EXPERTS, 1, 2 * D_FF), b_down.reshape(DEPTH, N_EXPERTS, 1, D), w_gate_up, w_down)


def _combine_kernel(rs_ref, rl_ref, rsn_ref, rln_ref, x1_ref, pos_ref, w_ref, g2_ref, lng_ref, lnb_ref, ys_hbm,
                    *rest, final):
    out_refs, (stage, sem) = rest[:-2], rest[-2:]
    i = pl.program_id(0)
    slot = i % 2
    other = 1 - slot

    def fetch(start_ref, len_ref, s):
        _for_each_run(start_ref, len_ref, lambda l, g, size: pltpu.make_async_copy(
            _rows(ys_hbm, g, size), _rows(stage.at[s], l, size), sem.at[s]).start())

    def wait(s):
        pltpu.make_async_copy(stage.at[s], stage.at[s], sem.at[s]).wait()

    @pl.when(i == 0)
    def _():
        fetch(rs_ref, rl_ref, 0)

    fetch(rsn_ref, rln_ref, other)
    wait(slot)
    y = _from_tiles(stage.at[slot], PAIRS_TM).astype(BF16)
    place = pos_ref[...].astype(F32)
    w = w_ref[...]
    s = lax.broadcasted_iota(jnp.int32, (TM, PAIRS_TM), 1).astype(F32)
    pick = jnp.zeros((TM, PAIRS_TM), F32)
    for k in range(TOP_K):
        pick = jnp.where(s == place[:, k:k + 1], w[:, k:k + 1], pick)
    f = _dot(pick.astype(BF16), y)
    ci = _cond_row(i)
    z = DEEPNORM_ALPHA * x1_ref[...] + g2_ref[pl.ds(ci, 1), :] * f
    out = _layer_norm(z, lng_ref[...], lnb_ref[...])
    if final:
        @pl.when(i < NT_P)
        def _():
            out_refs[0][...] = out

        @pl.when(i >= NT_P)
        def _():
            out_refs[1][...] = out
    else:
        out_refs[0][...] = out

    @pl.when(i == NT - 1)
    def _():
        wait(other)


def _combine(layer, x1, ys, pos, w, plan, mod, g, b):
    final = layer == DEPTH - 1
    nxt = lambda i: (jnp.minimum(i + 1, NT - 1), 0, 0)
    cur = lambda i: (i, 0, 0)
    if final:
        out_specs = _x_specs()
        out_shape = [jax.ShapeDtypeStruct((T_P, D), F32), jax.ShapeDtypeStruct((T_S, D), F32)]
    else:
        out_specs = _row_spec(D)
        out_shape = jax.ShapeDtypeStruct((T, D), F32)

    def kernel(*refs):
        _combine_kernel(*refs, final=final)

    return pl.pallas_call(
        kernel,
        grid=(NT,),
        in_specs=[_run_spec(cur), _run_spec(cur), _run_spec(nxt), _run_spec(nxt),
                  _row_spec(D), _row_spec(LANES), _row_spec(LANES), _mod_spec(layer, 5), _full((1, D)), _full((1, D)),
                  pl.BlockSpec(memory_space=pl.ANY)],
        out_specs=out_specs,
        out_shape=out_shape,
        scratch_shapes=[pltpu.VMEM((2, PAIRS_TM * SUB, LANES), F32), pltpu.SemaphoreType.DMA((2,))],
        compiler_params=_params("arbitrary"),
        name="moe_combine_norm" + ("_final" if final else ""),
    )(plan["run_start"], plan["run_len"], plan["run_start"], plan["run_len"], x1, pos, w, mod, g, b, ys)


def _moe_layer(layer, routed, experts, mod, ln2_g, ln2_b):
    x1, u2, pos, w, cnt = routed
    plan = _moe_plan(cnt)
    ys = _moe(layer, _dispatch(u2, pos, plan), plan, *experts)
    return _combine(layer, x1, ys, pos, w, plan, mod, ln2_g[layer][None], ln2_b[layer][None])


def _router_tail(l, ln1_g, ln1_b, router_w, router_b):
    rw = jnp.pad(router_w[l], ((0, 0), (0, LANES - N_EXPERTS)))
    rb = jnp.pad(router_b[l], (0, LANES - N_EXPERTS), constant_values=NEG)
    return ln1_g[l][None], ln1_b[l][None], rw, rb[None]


def kernel(x_prompt, x_sample, c, c_ctx, cache_diff_k, cache_diff_v, cache_na_k, cache_na_v, cache_gqa_k, cache_gqa_v, w_mod, b_mod, ln1_g, ln1_b, ln2_g, ln2_b, ab_w_in, ab_conv_w, ab_lambda_q1, ab_lambda_k1, ab_lambda_q2, ab_lambda_k2, ab_subln_g, ab_w_out, cd_w_in, cd_na_rpb, cd_q_norm_g, cd_k_norm_g, cd_w_out, router_w, router_b, w_gate_up, b_gate_up, w_down, b_down):
    xp = x_prompt.reshape(T_P, D)
    xs = x_sample.reshape(T_S, D)
    cond8 = jnp.concatenate([c_ctx[None], c, jnp.zeros((8 - 1 - B_S, D), F32)], axis=0)
    mod = _modulation(cond8, w_mod, b_mod)
    rope = _rope_tables()
    experts = (w_gate_up, b_gate_up, w_down, b_down)

    gb, y, q, k, v, new_diff_k, new_diff_v = _ab_in(xp, xs, mod, ab_w_in[0].astype(BF16), rope)
    lam_init = 0.8 - 0.6 * 1.0
    diff = (jnp.stack([ab_lambda_q1[0], ab_lambda_k1[0], ab_lambda_q2[0], ab_lambda_k2[0]]), ab_subln_g[0][None])
    attn = _flash_pair(q, k, v, latent=False, out_cols=512, col0=0, post="diff", diff=diff, lam_init=lam_init)
    attn = _flash_pair(q, k, v, latent=True, out_cols=512, col0=0, post="diff", diff=diff, lam_init=lam_init,
                       ctx=(cache_diff_k, cache_diff_v), ctx_mode="wide", prev=attn)
    routed = _ab_out(xp, xs, gb, y, ab_conv_w[0], attn, ab_w_out[0].astype(BF16), mod,
                     _router_tail(0, ln1_g, ln1_b, router_w, router_b))
    x = _moe_layer(0, routed, experts, mod, ln2_g, ln2_b)

    qg = jnp.tile(cd_q_norm_g[0], 2)[None]
    kg = jnp.tile(cd_k_norm_g[0], 2)[None]
    nq, nk, nv, gq, gk, gv, new_na_k, new_na_v, new_gqa_k, new_gqa_v = _cd_in(
        x, mod, cd_w_in[0].astype(BF16), qg, kg, rope)
    merged = _flash_pair(nq, nk, nv, latent=False, out_cols=D, col0=0, post="select")
    merged = _flash_pair(gq, gk, gv, latent=False, out_cols=D, col0=2, post="select", prev=merged)
    merged = _na_latent(nq, nk, nv, cache_na_k, cache_na_v, _na_bias(cd_na_rpb[0]), merged)
    merged = _flash_pair(gq, gk, gv, latent=True, out_cols=D, col0=2, post="select",
                         ctx=(cache_gqa_k, cache_gqa_v), ctx_mode="dup", prev=merged)
    routed = _cd_out(x, merged, cd_w_out[0].astype(BF16), mod,
                     _router_tail(1, ln1_g, ln1_b, router_w, router_b))
    y_p, y_s = _moe_layer(1, routed, experts, mod, ln2_g, ln2_b)

    return (y_p.reshape(B_P, N_P, D), y_s.reshape(B_S, N_S, D), new_diff_k, new_diff_v,
            new_na_k, new_na_v, new_gqa_k, new_gqa_v)
```

```python
import jax
import jax.numpy as jnp
import numpy as np
from jax import lax
from jax.experimental import pallas as pl
from jax.experimental.pallas import tpu as pltpu

F32 = jnp.float32
BF16 = jnp.bfloat16

D = 1024
B_P, N_P = 16, 256
B_S, N_S = 2, 4096
PAST = 256
T_P, T_S = B_P * N_P, B_S * N_S
T = T_P + T_S
TM = 256
NT_P, NT_S, NT = T_P // TM, T_S // TM, T // TM
TILES_PER_GRID = N_S // TM
GRID_W = 64
GRID_H = N_S // GRID_W
HD = 64
DEPTH = 2
N_EXPERTS = 32
TOP_K = 4
D_FF = 1024
NA_WIN_R, NA_WIN_C = 8, 16
NA_QROWS = 4
NA_KROWS = 12
SWIGLU_LIMIT = 7.0
SWIGLU_ALPHA = 1.702
ROPE_THETA = 10000.0
DEEPNORM_ALPHA = (2 * DEPTH) ** 0.25
LN_EPS = 1e-5
RMS_EPS = 1e-6
LOG2E = 1.4426950408889634
QK_SCALE = HD ** -0.5 * LOG2E
NEG = -1e30
MOE_TM = 512
MOE_PAIRS = TOP_K * T
MOE_TILES = MOE_PAIRS // MOE_TM + N_EXPERTS
PAIRS_TM = TOP_K * TM
RUN_SIZES = tuple(TM >> b for b in range(TM.bit_length()))
LANES = 128
SUB = 8

VMEM_LIMIT = 56 * 1024 * 1024


def _params(*sem):
    return pltpu.CompilerParams(dimension_semantics=sem, vmem_limit_bytes=VMEM_LIMIT)


def _split(x):
    hi = x.astype(BF16)
    lo = (x - hi.astype(F32)).astype(BF16)
    return hi, lo


def _dot(a, b):
    return jnp.dot(a, b, preferred_element_type=F32)


def _dot3(a, b):
    ah, al = _split(a)
    bh, bl = _split(b)
    return _dot(ah, bh) + (_dot(ah, bl) + _dot(al, bh))


def _cond_row(i):
    return jnp.where(i < NT_P, 0, 1 + (i - NT_P) // TILES_PER_GRID)


def _layer_norm(z, g, b):
    mu = jnp.mean(z, axis=-1, keepdims=True)
    zc = z - mu
    var = jnp.mean(zc * zc, axis=-1, keepdims=True)
    return zc * lax.rsqrt(var + LN_EPS) * g + b


def _low_half(rows):
    return lax.broadcasted_iota(jnp.int32, (rows, LANES), 1) < HD


def _mod_kernel(c_ref, w_ref, b_ref, o_ref):
    c = c_ref[...]
    o_ref[...] = _dot3(c * jax.nn.sigmoid(c), w_ref[...]) + b_ref[...]


def _modulation(cond8, w_mod, b_mod):
    return pl.pallas_call(
        _mod_kernel,
        grid=(DEPTH, 6),
        in_specs=[pl.BlockSpec((8, D), lambda l, j: (0, 0)),
                  pl.BlockSpec((None, D, D), lambda l, j: (l, 0, j)),
                  pl.BlockSpec((None, 1, D), lambda l, j: (l, 0, j))],
        out_specs=pl.BlockSpec((None, 8, D), lambda l, j: (l, 0, j)),
        out_shape=jax.ShapeDtypeStruct((DEPTH, 8, 6 * D), F32),
        compiler_params=_params("arbitrary", "arbitrary"),
        name="modulation",
    )(cond8, w_mod, b_mod.reshape(DEPTH, 1, 6 * D))


def _mod_spec(layer, chunk):
    return pl.BlockSpec((None, 8, D), lambda i, _l=layer, _c=chunk: (_l, 0, _c))


def _full(shape):
    return pl.BlockSpec(shape, lambda i: (0,) * len(shape))


def _rope_tables():
    t = np.arange(N_S)
    half = HD // 2
    inv = ROPE_THETA ** (-np.arange(0, half, 2, dtype=np.float64) / half)
    inv_lane = np.tile(np.repeat(inv, 2), 2 * LANES // HD)
    lane = np.arange(LANES)
    by_row = (lane % HD) < half
    ang = np.where(by_row[None], (t // GRID_W)[:, None], (t % GRID_W)[:, None]) * inv_lane[None]
    cos, sin = np.cos(ang), np.sin(ang)
    even = (lane % 2) == 0
    return tuple(jnp.asarray(a, F32) for a in (cos, np.where(even, -sin, 0.0), np.where(even, 0.0, sin)))


def _rope(x, a, b, c):
    return x * a + pltpu.roll(x, LANES - 1, axis=1) * b + pltpu.roll(x, 1, axis=1) * c


def _rope_or_identity(identity, ra_ref, rb_ref, rc_ref):
    return (jnp.where(identity, 1.0, ra_ref[...]), jnp.where(identity, 0.0, rb_ref[...]),
            jnp.where(identity, 0.0, rc_ref[...]))


def _rope_spec():
    return pl.BlockSpec((TM, LANES), lambda i: (jnp.maximum(i - NT_P, 0) % TILES_PER_GRID, 0))


def _x_specs():
    return [pl.BlockSpec((TM, D), lambda i: (jnp.minimum(i, NT_P - 1), 0)),
            pl.BlockSpec((TM, D), lambda i: (jnp.maximum(i - NT_P, 0), 0))]


def _cache_spec(heads, width):
    return pl.BlockSpec((None, None, heads, N_P, width), lambda i: (jnp.minimum(i, NT_P - 1), 0, 0, 0, 0))


def _row_spec(width):
    return pl.BlockSpec((TM, width), lambda i: (i, 0))


def _hm_spec(n):
    return pl.BlockSpec((n, TM, LANES), lambda i: (0, i, 0))


AB_Q0, AB_K0, AB_V0 = 1536, 2048, 2560


def _tile(p, col0, j):
    return p[:, col0 + j * LANES:col0 + (j + 1) * LANES]


def _ab_in_kernel(xp_ref, xs_ref, sh_ref, sc_ref, w_ref, ra_ref, rb_ref, rc_ref,
                  gb_ref, y_ref, q_ref, k_ref, v_ref, kc_ref, vc_ref):
    i = pl.program_id(0)
    is_p = i < NT_P
    ci = _cond_row(i)
    x = jnp.where(is_p, xp_ref[...], xs_ref[...])
    u = x * (1.0 + sc_ref[pl.ds(ci, 1), :]) + sh_ref[pl.ds(ci, 1), :]
    p = _dot(u.astype(BF16), w_ref[...])
    gb_ref[...] = p[:, 0:512]
    y_ref[...] = p[:, 512:1024] * p[:, 1024:1536]
    a, b, c = _rope_or_identity(is_p, ra_ref, rb_ref, rc_ref)
    for h in range(4):
        v_ref[h] = _tile(p, AB_V0, h).astype(BF16)
        q_ref[h] = (_rope(_tile(p, AB_Q0, h), a, b, c) * QK_SCALE).astype(BF16)
        k_ref[h] = _rope(_tile(p, AB_K0, h), a, b, c).astype(BF16)

    @pl.when(is_p)
    def _():
        for h in range(4):
            kc_ref[h] = _tile(p, AB_K0, h)
            vc_ref[h] = _tile(p, AB_V0, h)


def _ab_in(xp, xs, mod, w_in, rope):
    hm = jax.ShapeDtypeStruct((4, T, LANES), BF16)
    cache = jax.ShapeDtypeStruct((B_P, 1, 4, N_P, LANES), F32)
    half = jax.ShapeDtypeStruct((T, 512), F32)
    return pl.pallas_call(
        _ab_in_kernel,
        grid=(NT,),
        in_specs=_x_specs() + [_mod_spec(0, 0), _mod_spec(0, 1), _full((D, 3072)),
                               _rope_spec(), _rope_spec(), _rope_spec()],
        out_specs=[_row_spec(512), _row_spec(512), _hm_spec(4), _hm_spec(4), _hm_spec(4),
                   _cache_spec(4, LANES), _cache_spec(4, LANES)],
        out_shape=[half, half, hm, hm, hm, cache, cache],
        compiler_params=_params("arbitrary"),
        name="ab_in_proj",
    )(xp, xs, mod, mod, w_in, *rope)


CD_NQ, CD_NK, CD_NV, CD_GQ, CD_GK, CD_GV = 0, 512, 1024, 1536, 2048, 2176


def _seg_mean64(s):
    r = lax.broadcasted_iota(jnp.int32, (LANES, LANES), 0) // HD
    c = lax.broadcasted_iota(jnp.int32, (LANES, LANES), 1) // HD
    seg = jnp.where(r == c, 1.0, 0.0).astype(BF16)
    hi, lo = _split(s)
    return (_dot(hi, seg) + _dot(lo, seg)) * (1.0 / HD)


def _rms64(x, g):
    return x * lax.rsqrt(_seg_mean64(x * x) + RMS_EPS) * g


def _dup_halves(x, lo):
    r = pltpu.roll(x, HD, axis=1)
    return jnp.where(lo, x, r), jnp.where(lo, r, x)


def _cd_in_kernel(x_ref, sh_ref, sc_ref, w_ref, qg_ref, kg_ref, ra_ref, rb_ref, rc_ref,
                  nq_ref, nk_ref, nv_ref, gq_ref, gk_ref, gv_ref,
                  nkc_ref, nvc_ref, gkc_ref, gvc_ref):
    i = pl.program_id(0)
    is_p = i < NT_P
    ci = _cond_row(i)
    u = x_ref[...] * (1.0 + sc_ref[pl.ds(ci, 1), :]) + sh_ref[pl.ds(ci, 1), :]
    p = _dot(u.astype(BF16), w_ref[...])
    lo = _low_half(TM)
    for j in range(4):
        nq_ref[j] = (_tile(p, CD_NQ, j) * QK_SCALE).astype(BF16)
        nk_ref[j] = _tile(p, CD_NK, j).astype(BF16)
        nv_ref[j] = _tile(p, CD_NV, j).astype(BF16)
    gq = [_rms64(_tile(p, CD_GQ, j), qg_ref[...]) for j in range(4)]
    gk = _rms64(_tile(p, CD_GK, 0), kg_ref[...])
    gv = _tile(p, CD_GV, 0)
    v0, v1 = _dup_halves(gv, lo)
    gv_ref[0] = v0.astype(BF16)
    gv_ref[1] = v1.astype(BF16)
    a, b, c = _rope_or_identity(is_p, ra_ref, rb_ref, rc_ref)
    for j in range(4):
        gq_ref[j] = (_rope(gq[j], a, b, c) * QK_SCALE).astype(BF16)
    k0, k1 = _dup_halves(_rope(gk, a, b, c), lo)
    gk_ref[0] = k0.astype(BF16)
    gk_ref[1] = k1.astype(BF16)

    @pl.when(is_p)
    def _():
        gkc_ref[0] = k0[:, 0:HD]
        gkc_ref[1] = k1[:, 0:HD]
        gvc_ref[0] = v0[:, 0:HD]
        gvc_ref[1] = v1[:, 0:HD]
        for j in range(4):
            for src, dst in ((CD_NK, nkc_ref), (CD_NV, nvc_ref)):
                a, b = _dup_halves(_tile(p, src, j), lo)
                dst[2 * j] = a[:, 0:HD]
                dst[2 * j + 1] = b[:, 0:HD]


def _cd_in(x, mod, w_in, qg, kg, rope):
    hm4 = jax.ShapeDtypeStruct((4, T, LANES), BF16)
    hm2 = jax.ShapeDtypeStruct((2, T, LANES), BF16)
    c8 = jax.ShapeDtypeStruct((B_P, 1, 8, N_P, HD), F32)
    c2 = jax.ShapeDtypeStruct((B_P, 1, 2, N_P, HD), F32)
    return pl.pallas_call(
        _cd_in_kernel,
        grid=(NT,),
        in_specs=[_row_spec(D), _mod_spec(1, 0), _mod_spec(1, 1), _full((D, 2304)),
                  _full((1, LANES)), _full((1, LANES)), _rope_spec(), _rope_spec(), _rope_spec()],
        out_specs=[_hm_spec(4), _hm_spec(4), _hm_spec(4), _hm_spec(4), _hm_spec(2), _hm_spec(2),
                   _cache_spec(8, HD), _cache_spec(8, HD), _cache_spec(2, HD), _cache_spec(2, HD)],
        out_shape=[hm4, hm4, hm4, hm4, hm2, hm2, c8, c8, c2, c2],
        compiler_params=_params("arbitrary"),
        name="cd_in_proj",
    )(x, mod, mod, w_in, qg, kg, *rope)


def _stack_pairs(q_ref, n_q, tq):
    lo = _low_half(tq)
    parts = []
    for j in range(n_q):
        q = q_ref[j]
        zero = jnp.zeros_like(q)
        parts += [jnp.where(lo, q, zero), jnp.where(lo, zero, q)]
    return jnp.concatenate(parts, axis=0), lo


def _qk(qs, kb):
    return lax.dot_general(qs, kb, (((1,), (1,)), ((), ())), preferred_element_type=F32)


def _ctx_tile(ref, mode):
    if mode == "wide":
        x = ref[...]
    elif mode == "pair":
        x = jnp.concatenate([ref[0], ref[1]], axis=1)
    else:
        x = jnp.concatenate([ref[...], ref[...]], axis=1)
    return x.astype(BF16)


def _with_ones(v):
    lane = lax.broadcasted_iota(jnp.int32, v.shape, 1)
    return jnp.concatenate([v, jnp.where(lane == 0, 1.0, 0.0).astype(v.dtype)], axis=1)


def _flash_pair_kernel(*refs, gs, ctx_mode, **kw):
    it = iter(refs)
    q_ref, k_ref, v_ref = next(it), next(it), next(it)
    ctx = [next(it), next(it)] if ctx_mode else []
    rest = list(it)
    o_ref = rest.pop()
    n_q = q_ref.shape[0] // gs
    for t in range(gs * n_q):
        gi = t // n_q
        _flash_group(q_ref.at[t:t + 1], k_ref.at[gi], v_ref.at[gi], *[r.at[gi] for r in ctx], *rest,
                     o_ref.at[:, t * LANES:(t + 1) * LANES], n_q=1, ctx_mode=ctx_mode, **kw)


def _flash_group(*refs, n_q, tq, nk, tk, ctx_mode, post, lam_init):
    it = iter(refs)
    q_ref, k_ref, v_ref = next(it), next(it), next(it)
    kc_ref, vc_ref = (next(it), next(it)) if ctx_mode else (None, None)
    lam_ref, g_ref = (next(it), next(it)) if post == "diff" else (None, None)
    o_ref = next(it)

    qs, lo = _stack_pairs(q_ref, n_q, tq)
    rows = 2 * n_q * tq

    def step(kb, vb, carry):
        m, acc = carry
        s = _qk(qs, kb)
        m_new = jnp.maximum(m, jnp.max(s, axis=1, keepdims=True))
        p = jnp.exp2((s - m_new).astype(BF16))
        return m_new, jnp.exp2(m - m_new) * acc + _dot(p, _with_ones(vb))

    carry = (jnp.full((rows, 1), NEG, F32), jnp.zeros((rows, 2 * LANES), F32))
    for c in range(nk // tk):
        carry = step(k_ref[c * tk:(c + 1) * tk, :], v_ref[c * tk:(c + 1) * tk, :], carry)
    if ctx_mode:
        carry = step(_ctx_tile(kc_ref, ctx_mode), _ctx_tile(vc_ref, ctx_mode), carry)
    _, acc = carry
    o = acc[:, 0:LANES] / acc[:, LANES:LANES + 1]

    if post == "diff":
        lp = lam_ref[...]
        lam = (jnp.exp(jnp.sum(lp[0:1] * lp[1:2], axis=1, keepdims=True))
               - jnp.exp(jnp.sum(lp[2:3] * lp[3:4], axis=1, keepdims=True)) + lam_init)
        a = o[0:tq] - lam * o[tq:2 * tq]
        ms = jnp.mean(a * a, axis=-1, keepdims=True)
        o_ref[...] = (a * lax.rsqrt(ms + RMS_EPS) * g_ref[...] * (1.0 - lam_init)).astype(o_ref.dtype)
    else:
        for j in range(n_q):
            o_ref[:, j * LANES:(j + 1) * LANES] = jnp.where(
                lo, o[2 * j * tq:(2 * j + 1) * tq], o[(2 * j + 1) * tq:(2 * j + 2) * tq]).astype(o_ref.dtype)


def _flash_pair(q, k, v, *, latent, out_cols, col0, post, ctx=None, ctx_mode=None, diff=None,
                lam_init=0.0, prev=None):
    groups = k.shape[0]
    n_q = q.shape[0] // groups
    if latent:
        tq, nk, tk, gs = TM, N_S, 1024, 2
        grid = (B_S, groups // gs, TILES_PER_GRID)
        qrow = lambda b, g, i: NT_P + b * TILES_PER_GRID + i
        krow = lambda b, g, i: T_P // N_S + b
    else:
        tq, nk, tk, gs = N_P, N_P, N_P, groups
        grid = (B_P, 1, 1)
        qrow = lambda b, g, i: b
        krow = lambda b, g, i: b
    in_specs = [pl.BlockSpec((gs * n_q, tq, LANES), lambda b, g, i: (g, qrow(b, g, i), 0)),
                pl.BlockSpec((gs, nk, LANES), lambda b, g, i: (g, krow(b, g, i), 0)),
                pl.BlockSpec((gs, nk, LANES), lambda b, g, i: (g, krow(b, g, i), 0))]
    args = [q, k, v]
    if ctx is not None:
        width = LANES if ctx_mode == "wide" else HD
        spec = pl.BlockSpec((None, None, gs, PAST, width), lambda b, g, i: (b, 0, g, 0, 0))
        in_specs += [spec, spec]
        args += list(ctx)
    if diff is not None:
        in_specs += [pl.BlockSpec((4, HD), lambda b, g, i: (0, 0)),
                     pl.BlockSpec((1, LANES), lambda b, g, i: (0, 0))]
        args += list(diff)
    aliases = {}
    if prev is not None:
        aliases = {len(args): 0}
        in_specs.append(pl.BlockSpec(memory_space=pl.ANY))
        args.append(prev)

    def kernel(*refs):
        if prev is not None:
            refs = refs[:-2] + refs[-1:]
        _flash_pair_kernel(*refs, gs=gs, tq=tq, nk=nk, tk=tk, ctx_mode=ctx_mode if ctx is not None else None,
                           post=post, lam_init=lam_init)

    return pl.pallas_call(
        kernel,
        grid=grid,
        in_specs=in_specs,
        out_specs=pl.BlockSpec((tq, gs * n_q * LANES), lambda b, g, i: (qrow(b, g, i), col0 // gs + g)),
        out_shape=jax.ShapeDtypeStruct((T, out_cols), BF16),
        input_output_aliases=aliases,
        compiler_params=_params("arbitrary", "arbitrary", "arbitrary"),
        name="attn_" + post + ("_latent" if latent else "_context") + str(n_q),
    )(*args)


NA_TQ = NA_QROWS * GRID_W
NA_TK = NA_KROWS * GRID_W


NA_BLOCK_POS = ((0, 0), (NA_QROWS, 0), (GRID_H - NA_QROWS, GRID_H - NA_KROWS))
N_DR = 2 * NA_WIN_R - 1
N_DC = 2 * NA_WIN_C - 1


def _na_bias_kernel(rpb_ref, o_ref):
    qc = lax.broadcasted_iota(jnp.int32, (GRID_W, LANES), 0)
    lane = lax.broadcasted_iota(jnp.int32, (GRID_W, LANES), 1)
    kc = lane % GRID_W
    cs = jnp.clip(qc - NA_WIN_C // 2, 0, GRID_W - NA_WIN_C)
    col_ok = jnp.logical_and(kc >= cs, kc < cs + NA_WIN_C)
    lo = lane < GRID_W
    neg = jnp.full((GRID_W, LANES), NEG, F32)
    for head in range(2):
        toeplitz = []
        for dr in range(N_DR):
            r = jnp.broadcast_to(rpb_ref[head, dr:dr + 1, :] * LOG2E, (GRID_W, LANES))
            t = jnp.where(lo, pltpu.roll(r, LANES - (NA_WIN_C - 1), axis=1, stride=1, stride_axis=0),
                          pltpu.roll(r, GRID_W - (NA_WIN_C - 1), axis=1, stride=1, stride_axis=0))
            toeplitz.append(jnp.where(col_ok, t, neg))
        for pos, (r0, k0) in enumerate(NA_BLOCK_POS):
            for i in range(NA_QROWS):
                qr = r0 + i
                rs = min(max(qr - NA_WIN_R // 2, 0), GRID_H - NA_WIN_R)
                tiles = [toeplitz[k0 + j - qr + NA_WIN_R - 1] if rs <= k0 + j < rs + NA_WIN_R else neg
                         for j in range(NA_KROWS)]
                for jp in range(NA_KROWS // 2):
                    o_ref[pos, pl.ds(head * NA_TQ + i * GRID_W, GRID_W), pl.ds(jp * LANES, LANES)] = jnp.where(
                        lo, tiles[2 * jp], tiles[2 * jp + 1])


def _na_bias(rpb):
    rpb_pad = jnp.pad(rpb, ((0, 0), (0, 16 - N_DR), (0, LANES - N_DC)), constant_values=NEG)
    return pl.pallas_call(
        _na_bias_kernel,
        grid=(4,),
        in_specs=[pl.BlockSpec((2, 16, LANES), lambda g: (g, 0, 0))],
        out_specs=pl.BlockSpec((3, None, 2 * NA_TQ, NA_TK), lambda g: (0, g, 0, 0)),
        out_shape=jax.ShapeDtypeStruct((3, 4, 2 * NA_TQ, NA_TK), F32),
        compiler_params=_params("arbitrary"),
        name="na_bias_table",
    )(rpb_pad)


NA_PAIRS = 4


def _na_kernel(q_ref, k_ref, v_ref, kc_ref, vc_ref, bm_ref, _, o_ref):
    i = pl.program_id(2)
    k0 = jnp.clip(i * NA_QROWS - NA_WIN_R // 2, 0, GRID_H - NA_KROWS)
    start = pl.multiple_of(k0 * GRID_W, GRID_W)
    for g in range(NA_PAIRS):
        qs, lo = _stack_pairs(q_ref.at[g:g + 1], 1, NA_TQ)
        kw = k_ref[g, pl.ds(start, NA_TK), :]
        vw = v_ref[g, pl.ds(start, NA_TK), :]
        s_w = _qk(qs, kw) + bm_ref[g]
        s_c = _qk(qs, _ctx_tile(kc_ref.at[2 * g:2 * g + 2], "pair"))
        m = jnp.maximum(jnp.max(s_w, axis=1, keepdims=True), jnp.max(s_c, axis=1, keepdims=True))
        p_w = jnp.exp2((s_w - m).astype(BF16))
        p_c = jnp.exp2((s_c - m).astype(BF16))
        acc = _dot(p_w, _with_ones(vw)) + _dot(p_c, _with_ones(_ctx_tile(vc_ref.at[2 * g:2 * g + 2], "pair")))
        o = acc[:, 0:LANES] / acc[:, LANES:LANES + 1]
        o_ref[:, g * LANES:(g + 1) * LANES] = jnp.where(lo, o[0:NA_TQ], o[NA_TQ:2 * NA_TQ]).astype(o_ref.dtype)


def _na_latent(q, k, v, kc, vc, bm, prev):
    nblk = N_S // NA_TQ
    qrow = lambda b, g, i: T_P // NA_TQ + b * nblk + i
    krow = lambda b, g, i: T_P // N_S + b
    cfg = lambda i: jnp.where(i == 0, 0, jnp.where(i == nblk - 1, 2, 1))
    ctx_spec = pl.BlockSpec((None, None, 2 * NA_PAIRS, PAST, HD), lambda b, g, i: (b, 0, g, 0, 0))
    return pl.pallas_call(
        _na_kernel,
        grid=(B_S, 4 // NA_PAIRS, nblk),
        in_specs=[pl.BlockSpec((NA_PAIRS, NA_TQ, LANES), lambda b, g, i: (g, qrow(b, g, i), 0)),
                  pl.BlockSpec((NA_PAIRS, N_S, LANES), lambda b, g, i: (g, krow(b, g, i), 0)),
                  pl.BlockSpec((NA_PAIRS, N_S, LANES), lambda b, g, i: (g, krow(b, g, i), 0)),
                  ctx_spec, ctx_spec,
                  pl.BlockSpec((None, NA_PAIRS, 2 * NA_TQ, NA_TK), lambda b, g, i: (cfg(i), g, 0, 0)),
                  pl.BlockSpec(memory_space=pl.ANY)],
        out_specs=pl.BlockSpec((NA_TQ, NA_PAIRS * LANES), lambda b, g, i: (qrow(b, g, i), g)),
        out_shape=jax.ShapeDtypeStruct((T, D), BF16),
        input_output_aliases={6: 0},
        compiler_params=_params("arbitrary", "arbitrary", "arbitrary"),
        name="attn_window_latent",
    )(q, k, v, kc, vc, bm, prev)


def _top4(logits):
    lane = lax.broadcasted_iota(jnp.int32, logits.shape, 1).astype(F32)
    rest = logits
    tops, firsts = [], []
    for _ in range(TOP_K):
        m = jnp.max(rest, axis=1, keepdims=True)
        first = jnp.min(jnp.where(rest == m, lane, float(LANES)), axis=1, keepdims=True)
        tops.append(m)
        firsts.append(first)
        rest = jnp.where(lane == first, -jnp.inf, rest)
    es = [jnp.exp(m - tops[0]) for m in tops]
    denom = es[0] + es[1] + es[2] + es[3]
    w = jnp.zeros_like(logits)
    for k in range(TOP_K):
        w = jnp.where(lane == float(k), es[k] / denom, w)
    return firsts, w


def _local_sort(firsts):
    lane = lax.broadcasted_iota(jnp.int32, (TM, LANES), 1).astype(F32)
    hots = [lane == f for f in firsts]
    sel = jnp.zeros((TM, LANES), F32)
    for hot in hots:
        sel = jnp.where(hot, 1.0, sel)
    r = lax.broadcasted_iota(jnp.int32, (TM, TM), 0)
    c = lax.broadcasted_iota(jnp.int32, (TM, TM), 1)
    earlier = _dot(jnp.where(c < r, 1.0, 0.0).astype(BF16), sel.astype(BF16))
    cnt = jnp.sum(sel, axis=0, keepdims=True)
    r = lax.broadcasted_iota(jnp.int32, (LANES, LANES), 0)
    c = lax.broadcasted_iota(jnp.int32, (LANES, LANES), 1)
    start = _dot(jnp.broadcast_to(cnt, (SUB, LANES)).astype(BF16), jnp.where(r < c, 1.0, 0.0).astype(BF16))[0:1]
    place = start + earlier
    pos = jnp.zeros((TM, LANES), F32)
    for k, hot in enumerate(hots):
        pos = jnp.where(lane == float(k), jnp.sum(jnp.where(hot, place, 0.0), axis=1, keepdims=True), pos)
    return pos.astype(jnp.int32), cnt.astype(jnp.int32)


def _mixer_tail(x_of, h_of, ci, g1_ref, sh2_ref, sc2_ref, lng_ref, lnb_ref, rw_ref, rb_ref,
                x1_ref, u2_ref, pos_ref, w_ref, cnt_ref):
    rows = slice(0, TM)
    x1 = _layer_norm(DEEPNORM_ALPHA * x_of(rows) + g1_ref[pl.ds(ci, 1), :] * h_of(rows), lng_ref[...], lnb_ref[...])
    x1_ref[...] = x1
    u2 = x1 * (1.0 + sc2_ref[pl.ds(ci, 1), :]) + sh2_ref[pl.ds(ci, 1), :]
    u2_ref[...] = u2.astype(BF16)
    firsts, w_ref[...] = _top4(_dot3(u2, rw_ref[...]) + rb_ref[...])
    pos, cnt = _local_sort(firsts)
    pos_ref[...] = pos
    cnt_ref[...] = jnp.broadcast_to(cnt, (SUB, LANES))


def _ab_out_kernel(xp_ref, xs_ref, gb_ref, y_ref, yp_ref, yn_ref, cv_ref, at_ref, wo_ref,
                   g1_ref, sh2_ref, sc2_ref, lng_ref, lnb_ref, rw_ref, rb_ref, *out_refs):
    i = pl.program_id(0)
    is_p = i < NT_P
    ci = _cond_row(i)
    j = (i - NT_P) % TILES_PER_GRID
    first = jnp.logical_or(is_p, j == 0)
    last = jnp.logical_or(is_p, j == TILES_PER_GRID - 1)
    y = y_ref[...]
    row = lax.broadcasted_iota(jnp.int32, y.shape, 0)
    before = jnp.where(first, 0.0, yp_ref[7:8, :])
    after = jnp.where(last, 0.0, yn_ref[0:1, :])
    y_prev = jnp.where(row == 0, before, pltpu.roll(y, 1, axis=0))
    y_next = jnp.where(row == TM - 1, after, pltpu.roll(y, TM - 1, axis=0))
    cv = cv_ref[...]
    conv = (gb_ref[...] * (y_prev * cv[0:1] + y * cv[1:2] + y_next * cv[2:3])).astype(BF16)
    _mixer_tail(lambda r: jnp.where(is_p, xp_ref[r, :], xs_ref[r, :]),
                lambda r: _dot(conv[r], wo_ref[0:512, :]) + _dot(at_ref[r, :], wo_ref[512:1024, :]),
                ci, g1_ref, sh2_ref, sc2_ref, lng_ref, lnb_ref, rw_ref, rb_ref, *out_refs)


def _cd_out_kernel(x_ref, mg_ref, wo_ref, g1_ref, sh2_ref, sc2_ref, lng_ref, lnb_ref, rw_ref, rb_ref,
                   *out_refs):
    ci = _cond_row(pl.program_id(0))
    _mixer_tail(lambda r: x_ref[r, :], lambda r: _dot(mg_ref[r, :], wo_ref[...]),
                ci, g1_ref, sh2_ref, sc2_ref, lng_ref, lnb_ref, rw_ref, rb_ref, *out_refs)


def _tail_specs(layer):
    return [_mod_spec(layer, 2), _mod_spec(layer, 3), _mod_spec(layer, 4),
            _full((1, D)), _full((1, D)), _full((D, LANES)), _full((1, LANES))]


_TAIL_OUT_SPECS = [_row_spec(D), _row_spec(D), _row_spec(LANES), _row_spec(LANES),
                   pl.BlockSpec((None, SUB, LANES), lambda i: (i, 0, 0))]
_TAIL_OUT_SHAPES = [jax.ShapeDtypeStruct((T, D), F32), jax.ShapeDtypeStruct((T, D), BF16),
                    jax.ShapeDtypeStruct((T, LANES), jnp.int32), jax.ShapeDtypeStruct((T, LANES), F32),
                    jax.ShapeDtypeStruct((NT, SUB, LANES), jnp.int32)]


def _ab_out(xp, xs, gb, y, conv_w, attn, w_out, mod, tail):
    halo_prev = pl.BlockSpec((8, 512), lambda i: (jnp.maximum(i * (TM // 8) - 1, 0), 0))
    halo_next = pl.BlockSpec((8, 512), lambda i: (jnp.minimum((i + 1) * (TM // 8), T // 8 - 1), 0))
    return pl.pallas_call(
        _ab_out_kernel,
        grid=(NT,),
        in_specs=_x_specs() + [_row_spec(512), _row_spec(512), halo_prev, halo_next, _full((3, 512)),
                               _row_spec(512), _full((D, D))] + _tail_specs(0),
        out_specs=_TAIL_OUT_SPECS,
        out_shape=_TAIL_OUT_SHAPES,
        compiler_params=_params("arbitrary"),
        name="ab_out_proj",
    )(xp, xs, gb, y, y, y, conv_w, attn, w_out, mod, mod, mod, *tail)


def _cd_out(x, merged, w_out, mod, tail):
    return pl.pallas_call(
        _cd_out_kernel,
        grid=(NT,),
        in_specs=[_row_spec(D), _row_spec(D), _full((D, D))] + _tail_specs(1),
        out_specs=_TAIL_OUT_SPECS,
        out_shape=_TAIL_OUT_SHAPES,
        compiler_params=_params("arbitrary"),
        name="cd_out_proj",
    )(x, merged, w_out, mod, mod, mod, *tail)


def _mix_down(wd_ref, wmix_ref):
    half = D_FF // 2
    for c in range(D // LANES):
        wmix_ref[c, pl.ds(0, half, stride=2), :] = wd_ref[0:half, c * LANES:(c + 1) * LANES]
        wmix_ref[c, pl.ds(1, half, stride=2), :] = wd_ref[half:D_FF, c * LANES:(c + 1) * LANES]


def _prep_expert(wgu_ref, wd_ref, wgu_bf, wmix_ref, wd_bf):
    for c in range(4):
        wgu_bf[:, c * 512:(c + 1) * 512] = wgu_ref[:, c * 512:(c + 1) * 512].astype(BF16)
    _mix_down(wd_ref, wmix_ref)
    for c in range(D // LANES):
        wd_bf[:, c * LANES:(c + 1) * LANES] = wmix_ref[c].astype(BF16)


def _expert_ffn(u, wgu_bf, bgu_ref, wd_bf, bd_ref):
    rows = u.shape[0]
    ga = _dot(u, wgu_bf[:, 0:D_FF]) + bgu_ref[:, 0:D_FF]
    gb = _dot(u, wgu_bf[:, D_FF:2 * D_FF]) + bgu_ref[:, D_FF:2 * D_FF]
    even = (lax.broadcasted_iota(jnp.int32, (rows, LANES), 1) % 2) == 0
    hid = []
    for c in range(D_FF // LANES):
        a = ga[:, c * LANES:(c + 1) * LANES]
        b = gb[:, c * LANES:(c + 1) * LANES]
        gate = jnp.where(even, a, pltpu.roll(b, 1, axis=1))
        up = jnp.where(even, pltpu.roll(a, LANES - 1, axis=1), b)
        gate = jnp.minimum(gate, SWIGLU_LIMIT)
        up = jnp.clip(up, -SWIGLU_LIMIT, SWIGLU_LIMIT)
        hid.append(((up + 1.0) * gate * jax.nn.sigmoid(SWIGLU_ALPHA * gate)).astype(BF16))
    hid = jnp.concatenate(hid, axis=1)
    return _dot(hid, wd_bf[...]) + bd_ref[...]


def _moe_plan(cnt):
    c = cnt[:, 0, :N_EXPERTS]
    counts = jnp.sum(c, axis=0)
    tiles_e = (counts + MOE_TM - 1) // MOE_TM
    tile_end = jnp.cumsum(tiles_e)
    first_row = (tile_end - tiles_e) * MOE_TM
    run_start = first_row[None] + jnp.cumsum(c, axis=0) - c
    j = jnp.arange(MOE_TILES, dtype=jnp.int32)
    tile_expert = jnp.minimum(jnp.sum((j[:, None] >= tile_end[None]).astype(jnp.int32), axis=1), N_EXPERTS - 1)
    i32 = lambda a: a.astype(jnp.int32)
    experts = jnp.arange(N_EXPERTS, dtype=jnp.int32)
    owns = tiles_e > 0
    seg = jnp.cumsum(owns.astype(jnp.int32)) - owns.astype(jnp.int32)
    later = jnp.where((experts[None] > experts[:, None]) & owns[None], experts[None], N_EXPERTS)
    nxt = jnp.min(later, axis=1)
    nxt = jnp.where(nxt == N_EXPERTS, -1, nxt)
    per_tile = lambda v: jnp.sum(jnp.where(tile_expert[:, None] == experts[None], v[None], 0), axis=1)
    return dict(tile_expert=i32(tile_expert), n_used=i32(tile_end[-1:]),
                seg_parity=i32(per_tile(seg % 2)), next_expert=i32(per_tile(nxt)),
                run_start=i32(run_start)[:, None, :], run_len=i32(c)[:, None, :],
                pad_start=i32(first_row + counts)[None, None, :], pad_len=i32(tiles_e * MOE_TM - counts)[None, None, :])


def _for_each_run(start_ref, len_ref, copy):
    def body(e, local):
        n = len_ref[0, e]
        g = start_ref[0, e]

        def pieces(sizes):
            for size in sizes:
                covered = jnp.bitwise_and(n, -2 * size)

                @pl.when(jnp.bitwise_and(n, size) != 0)
                def _(covered=covered, size=size):
                    copy(local + covered, g + covered, size)

        @pl.when(n >= RUN_SIZES[2])
        def _():
            pieces(RUN_SIZES[:3])

        pieces(RUN_SIZES[3:])
        return local + n
    lax.fori_loop(0, N_EXPERTS, body, jnp.int32(0))


def _rows(ref, start, size):
    start = 0 if isinstance(start, int) and start == 0 else pl.multiple_of(start * SUB, SUB)
    return ref.at[pl.ds(start, size * SUB), :]


def _from_tiles(ref, rows):
    return jnp.concatenate([ref[pl.ds(c, rows, stride=SUB), :] for c in range(SUB)], axis=1)


def _to_tiles(ref, x):
    for c in range(SUB):
        ref[pl.ds(c, x.shape[0], stride=SUB), :] = x[:, c * LANES:(c + 1) * LANES]


def _run_spec(index):
    return pl.BlockSpec((None, 1, N_EXPERTS), index, memory_space=pltpu.SMEM)


def _dispatch_kernel(rs_ref, rl_ref, ps_ref, pn_ref, u_ref, pos_ref, xs_hbm, stage, zeros, sem, zsem):
    i = pl.program_id(0)
    slot = i % 2

    def wait(s):
        pltpu.make_async_copy(stage.at[s], stage.at[s], sem.at[s]).wait()

    @pl.when(i == 0)
    def _():
        zeros[...] = jnp.zeros_like(zeros)

        def pad(local, g, size):
            return pltpu.make_async_copy(_rows(zeros, 0, size), _rows(xs_hbm, g, size), zsem.at[0])
        _for_each_run(ps_ref, pn_ref, lambda *a: pad(*a).start())
        _for_each_run(ps_ref, pn_ref, lambda *a: pad(*a).wait())

    @pl.when(i >= 2)
    def _():
        wait(slot)

    place = pos_ref[...].astype(F32).T
    s = lax.broadcasted_iota(jnp.int32, (PAIRS_TM, TM), 0).astype(F32)
    onehot = jnp.zeros((PAIRS_TM, TM), F32)
    for k in range(TOP_K):
        onehot = jnp.where(s == place[k:k + 1, :], 1.0, onehot)
    _to_tiles(stage.at[slot], _dot(onehot.astype(BF16), u_ref[...]))
    _for_each_run(rs_ref, rl_ref, lambda l, g, size: pltpu.make_async_copy(
        _rows(stage.at[slot], l, size), _rows(xs_hbm, g, size), sem.at[slot]).start())

    @pl.when(i == NT - 1)
    def _():
        wait(1 - slot)
        wait(slot)


def _dispatch(u2, pos, plan):
    return pl.pallas_call(
        _dispatch_kernel,
        grid=(NT,),
        in_specs=[_run_spec(lambda i: (i, 0, 0)), _run_spec(lambda i: (i, 0, 0)),
                  _run_spec(lambda i: (0, 0, 0)), _run_spec(lambda i: (0, 0, 0)),
                  _row_spec(D), _row_spec(LANES)],
        out_specs=pl.BlockSpec(memory_space=pl.ANY),
        out_shape=jax.ShapeDtypeStruct((MOE_TILES * MOE_TM * SUB, LANES), F32),
        scratch_shapes=[pltpu.VMEM((2, PAIRS_TM * SUB, LANES), F32), pltpu.VMEM((TM * SUB, LANES), F32),
                        pltpu.SemaphoreType.DMA((2,)), pltpu.SemaphoreType.DMA((1,))],
        compiler_params=_params("arbitrary"),
        name="moe_dispatch",
    )(plan["run_start"], plan["run_len"], plan["pad_start"], plan["pad_len"], u2, pos)


def _moe_kernel(te_ref, nu_ref, par_ref, nxt_ref, x_ref, bgu_ref, bd_ref, wgu_hbm, wd_hbm, y_ref,
                wgu_f, wd_f, wgu_bf, wmix_ref, wd_bf, sem, *, layer):
    j = pl.program_id(0)
    used = j < nu_ref[0]
    buf = par_ref[j]

    def copies(e, s):
        return (pltpu.make_async_copy(wgu_hbm.at[layer, e], wgu_f.at[s], sem.at[s]),
                pltpu.make_async_copy(wd_hbm.at[layer, e], wd_f.at[s], sem.at[s]))

    @pl.when(jnp.logical_and(used, j == 0))
    def _():
        for cp in copies(te_ref[0], 0):
            cp.start(priority=1)

    @pl.when(jnp.logical_and(used, jnp.logical_or(j == 0, te_ref[j] != te_ref[jnp.maximum(j - 1, 0)])))
    def _():
        for cp in copies(te_ref[j], buf):
            cp.wait()

        @pl.when(nxt_ref[j] >= 0)
        def _():
            for cp in copies(nxt_ref[j], 1 - buf):
                cp.start(priority=1)

        _prep_expert(wgu_f.at[buf], wd_f.at[buf], wgu_bf, wmix_ref, wd_bf)

    @pl.when(used)
    def _():
        half = MOE_TM // 2
        for h in range(2):
            rows = pl.ds(h * half * SUB, half * SUB)
            x = _from_tiles(x_ref.at[rows, :], half).astype(BF16)
            _to_tiles(y_ref.at[rows, :], _expert_ffn(x, wgu_bf, bgu_ref, wd_bf, bd_ref))


def _moe(layer, xs, plan, w_gate_up, b_gate_up, w_down, b_down):
    rows = pl.BlockSpec((MOE_TM * SUB, LANES), lambda j, te, nu, par, nxt: (jnp.minimum(j, nu[0] - 1), 0))
    grid_spec = pltpu.PrefetchScalarGridSpec(
        num_scalar_prefetch=4,
        grid=(MOE_TILES,),
        in_specs=[rows,
                  pl.BlockSpec((None, None, 1, 2 * D_FF), lambda j, te, nu, par, nxt: (layer, te[j], 0, 0)),
                  pl.BlockSpec((None, None, 1, D), lambda j, te, nu, par, nxt: (layer, te[j], 0, 0)),
                  pl.BlockSpec(memory_space=pl.ANY), pl.BlockSpec(memory_space=pl.ANY)],
        out_specs=rows,
        scratch_shapes=[pltpu.VMEM((2, D, 2 * D_FF), F32), pltpu.VMEM((2, D_FF, D), F32),
                        pltpu.VMEM((D, 2 * D_FF), BF16), pltpu.VMEM((D // LANES, D_FF, LANES), F32),
                        pltpu.VMEM((D_FF, D), BF16), pltpu.SemaphoreType.DMA((2,))])

    def kernel(*refs):
        _moe_kernel(*refs, layer=layer)

    return pl.pallas_call(
        kernel,
        grid_spec=grid_spec,
        out_shape=jax.ShapeDtypeStruct((MOE_TILES * MOE_TM * SUB, LANES), F32),
        compiler_params=_params("arbitrary"),
        name="moe_experts",
    )(plan["tile_expert"], plan["n_used"], plan["seg_parity"], plan["next_expert"], xs,
      b_gate_up.reshape(DEPTH, N_EXPERTS, 1, 2 * D_FF), b_down.reshape(DEPTH, N_EXPERTS, 1, D), w_gate_up, w_down)


def _combine_kernel(rs_ref, rl_ref, rsn_ref, rln_ref, x1_ref, pos_ref, w_ref, g2_ref, lng_ref, lnb_ref, ys_hbm,
                    *rest, final):
    out_refs, (stage, sem) = rest[:-2], rest[-2:]
    i = pl.program_id(0)
    slot = i % 2
    other = 1 - slot

    def fetch(start_ref, len_ref, s):
        _for_each_run(start_ref, len_ref, lambda l, g, size: pltpu.make_async_copy(
            _rows(ys_hbm, g, size), _rows(stage.at[s], l, size), sem.at[s]).start())

    def wait(s):
        pltpu.make_async_copy(stage.at[s], stage.at[s], sem.at[s]).wait()

    @pl.when(i == 0)
    def _():
        fetch(rs_ref, rl_ref, 0)

    fetch(rsn_ref, rln_ref, other)
    wait(slot)
    y = _from_tiles(stage.at[slot], PAIRS_TM).astype(BF16)
    place = pos_ref[...].astype(F32)
    w = w_ref[...]
    s = lax.broadcasted_iota(jnp.int32, (TM, PAIRS_TM), 1).astype(F32)
    pick = jnp.zeros((TM, PAIRS_TM), F32)
    for k in range(TOP_K):
        pick = jnp.where(s == place[:, k:k + 1], w[:, k:k + 1], pick)
    f = _dot(pick.astype(BF16), y)
    ci = _cond_row(i)
    z = DEEPNORM_ALPHA * x1_ref[...] + g2_ref[pl.ds(ci, 1), :] * f
    out = _layer_norm(z, lng_ref[...], lnb_ref[...])
    if final:
        @pl.when(i < NT_P)
        def _():
            out_refs[0][...] = out

        @pl.when(i >= NT_P)
        def _():
            out_refs[1][...] = out
    else:
        out_refs[0][...] = out

    @pl.when(i == NT - 1)
    def _():
        wait(other)


def _combine(layer, x1, ys, pos, w, plan, mod, g, b):
    final = layer == DEPTH - 1
    nxt = lambda i: (jnp.minimum(i + 1, NT - 1), 0, 0)
    cur = lambda i: (i, 0, 0)
    if final:
        out_specs = _x_specs()
        out_shape = [jax.ShapeDtypeStruct((T_P, D), F32), jax.ShapeDtypeStruct((T_S, D), F32)]
    else:
        out_specs = _row_spec(D)
        out_shape = jax.ShapeDtypeStruct((T, D), F32)

    def kernel(*refs):
        _combine_kernel(*refs, final=final)

    return pl.pallas_call(
        kernel,
        grid=(NT,),
        in_specs=[_run_spec(cur), _run_spec(cur), _run_spec(nxt), _run_spec(nxt),
                  _row_spec(D), _row_spec(LANES), _row_spec(LANES), _mod_spec(layer, 5), _full((1, D)), _full((1, D)),
                  pl.BlockSpec(memory_space=pl.ANY)],
        out_specs=out_specs,
        out_shape=out_shape,
        scratch_shapes=[pltpu.VMEM((2, PAIRS_TM * SUB, LANES), F32), pltpu.SemaphoreType.DMA((2,))],
        compiler_params=_params("arbitrary"),
        name="moe_combine_norm" + ("_final" if final else ""),
    )(plan["run_start"], plan["run_len"], plan["run_start"], plan["run_len"], x1, pos, w, mod, g, b, ys)


def _moe_layer(layer, routed, experts, mod, ln2_g, ln2_b):
    x1, u2, pos, w, cnt = routed
    plan = _moe_plan(cnt)
    ys = _moe(layer, _dispatch(u2, pos, plan), plan, *experts)
    return _combine(layer, x1, ys, pos, w, plan, mod, ln2_g[layer][None], ln2_b[layer][None])


def _router_tail(l, ln1_g, ln1_b, router_w, router_b):
    rw = jnp.pad(router_w[l], ((0, 0), (0, LANES - N_EXPERTS)))
    rb = jnp.pad(router_b[l], (0, LANES - N_EXPERTS), constant_values=NEG)
    return ln1_g[l][None], ln1_b[l][None], rw, rb[None]


def kernel(x_prompt, x_sample, c, c_ctx, cache_diff_k, cache_diff_v, cache_na_k, cache_na_v, cache_gqa_k, cache_gqa_v, w_mod, b_mod, ln1_g, ln1_b, ln2_g, ln2_b, ab_w_in, ab_conv_w, ab_lambda_q1, ab_lambda_k1, ab_lambda_q2, ab_lambda_k2, ab_subln_g, ab_w_out, cd_w_in, cd_na_rpb, cd_q_norm_g, cd_k_norm_g, cd_w_out, router_w, router_b, w_gate_up, b_gate_up, w_down, b_down):
    xp = x_prompt.reshape(T_P, D)
    xs = x_sample.reshape(T_S, D)
    cond8 = jnp.concatenate([c_ctx[None], c, jnp.zeros((8 - 1 - B_S, D), F32)], axis=0)
    mod = _modulation(cond8, w_mod, b_mod)
    rope = _rope_tables()
    experts = (w_gate_up, b_gate_up, w_down, b_down)

    gb, y, q, k, v, new_diff_k, new_diff_v = _ab_in(xp, xs, mod, ab_w_in[0].astype(BF16), rope)
    lam_init = 0.8 - 0.6 * 1.0
    diff = (jnp.stack([ab_lambda_q1[0], ab_lambda_k1[0], ab_lambda_q2[0], ab_lambda_k2[0]]), ab_subln_g[0][None])
    attn = _flash_pair(q, k, v, latent=False, out_cols=512, col0=0, post="diff", diff=diff, lam_init=lam_init)
    attn = _flash_pair(q, k, v, latent=True, out_cols=512, col0=0, post="diff", diff=diff, lam_init=lam_init,
                       ctx=(cache_diff_k, cache_diff_v), ctx_mode="wide", prev=attn)
    routed = _ab_out(xp, xs, gb, y, ab_conv_w[0], attn, ab_w_out[0].astype(BF16), mod,
                     _router_tail(0, ln1_g, ln1_b, router_w, router_b))
    x = _moe_layer(0, routed, experts, mod, ln2_g, ln2_b)

    qg = jnp.tile(cd_q_norm_g[0], 2)[None]
    kg = jnp.tile(cd_k_norm_g[0], 2)[None]
    nq, nk, nv, gq, gk, gv, new_na_k, new_na_v, new_gqa_k, new_gqa_v = _cd_in(
        x, mod, cd_w_in[0].astype(BF16), qg, kg, rope)
    merged = _flash_pair(nq, nk, nv, latent=False, out_cols=D, col0=0, post="select")
    merged = _flash_pair(gq, gk, gv, latent=False, out_cols=D, col0=2, post="select", prev=merged)
    merged = _na_latent(nq, nk, nv, cache_na_k, cache_na_v, _na_bias(cd_na_rpb[0]), merged)
    merged = _flash_pair(gq, gk, gv, latent=True, out_cols=D, col0=2, post="select",
                         ctx=(cache_gqa_k, cache_gqa_v), ctx_mode="dup", prev=merged)
    routed = _cd_out(x, merged, cd_w_out[0].astype(BF16), mod,
                     _router_tail(1, ln1_g, ln1_b, router_w, router_b))
    y_p, y_s = _moe_layer(1, routed, experts, mod, ln2_g, ln2_b)

    return (y_p.reshape(B_P, N_P, D), y_s.reshape(B_S, N_S, D), new_diff_k, new_diff_v,
            new_na_k, new_na_v, new_gqa_k, new_gqa_v)
```

```python
import jax
import jax.numpy as jnp
import numpy as np
from jax import lax
from jax.experimental import pallas as pl
from jax.experimental.pallas import tpu as pltpu

F32 = jnp.float32
BF16 = jnp.bfloat16

D = 1024
B_P, N_P = 16, 256
B_S, N_S = 2, 4096
PAST = 256
T_P, T_S = B_P * N_P, B_S * N_S
T = T_P + T_S
TM = 256
NT_P, NT_S, NT = T_P // TM, T_S // TM, T // TM
TILES_PER_GRID = N_S // TM
GRID_W = 64
GRID_H = N_S // GRID_W
HD = 64
DEPTH = 2
N_EXPERTS = 32
TOP_K = 4
D_FF = 1024
NA_WIN_R, NA_WIN_C = 8, 16
NA_QROWS = 4
NA_KROWS = 12
SWIGLU_LIMIT = 7.0
SWIGLU_ALPHA = 1.702
ROPE_THETA = 10000.0
DEEPNORM_ALPHA = (2 * DEPTH) ** 0.25
LN_EPS = 1e-5
RMS_EPS = 1e-6
LOG2E = 1.4426950408889634
QK_SCALE = HD ** -0.5 * LOG2E
NEG = -1e30
MOE_TM = 256
MOE_PAIRS = TOP_K * T
MOE_TILES = MOE_PAIRS // MOE_TM + N_EXPERTS
PAIRS_TM = TOP_K * TM
RUN_SIZES = tuple(TM >> b for b in range(TM.bit_length()))
LANES = 128
SUB = 8

VMEM_LIMIT = 56 * 1024 * 1024


def _params(*sem):
    return pltpu.CompilerParams(dimension_semantics=sem, vmem_limit_bytes=VMEM_LIMIT)


def _split(x):
    hi = x.astype(BF16)
    lo = (x - hi.astype(F32)).astype(BF16)
    return hi, lo


def _dot(a, b):
    return jnp.dot(a, b, preferred_element_type=F32)


def _dot3(a, b):
    ah, al = _split(a)
    bh, bl = _split(b)
    return _dot(ah, bh) + (_dot(ah, bl) + _dot(al, bh))


def _dot3_narrow(a, b):
    ah, al = _split(a)
    bh, bl = _split(b)
    both = _dot(ah, jnp.concatenate([bh, bl], axis=1))
    return both[:, 0:LANES] + (both[:, LANES:2 * LANES] + _dot(al, bh))


def _cond_row(i):
    return jnp.where(i < NT_P, 0, 1 + (i - NT_P) // TILES_PER_GRID)


def _layer_norm(z, g, b):
    mu = jnp.mean(z, axis=-1, keepdims=True)
    zc = z - mu
    var = jnp.mean(zc * zc, axis=-1, keepdims=True)
    return zc * lax.rsqrt(var + LN_EPS) * g + b


def _low_half(rows):
    return lax.broadcasted_iota(jnp.int32, (rows, LANES), 1) < HD


def _mod_kernel(c_ref, w_ref, b_ref, o_ref):
    c = c_ref[...]
    o_ref[...] = _dot3(c * jax.nn.sigmoid(c), w_ref[...]) + b_ref[...]


def _modulation(cond8, w_mod, b_mod):
    return pl.pallas_call(
        _mod_kernel,
        grid=(DEPTH, 6),
        in_specs=[pl.BlockSpec((8, D), lambda l, j: (0, 0)),
                  pl.BlockSpec((None, D, D), lambda l, j: (l, 0, j)),
                  pl.BlockSpec((None, 1, D), lambda l, j: (l, 0, j))],
        out_specs=pl.BlockSpec((None, 8, D), lambda l, j: (l, 0, j)),
        out_shape=jax.ShapeDtypeStruct((DEPTH, 8, 6 * D), F32),
        compiler_params=_params("arbitrary", "arbitrary"),
        name="modulation",
    )(cond8, w_mod, b_mod.reshape(DEPTH, 1, 6 * D))


def _mod_spec(layer, chunk):
    return pl.BlockSpec((None, 8, D), lambda i, _l=layer, _c=chunk: (_l, 0, _c))


def _full(shape):
    return pl.BlockSpec(shape, lambda i: (0,) * len(shape))


def _rope_tables():
    t = np.arange(N_S)
    half = HD // 2
    inv = ROPE_THETA ** (-np.arange(0, half, 2, dtype=np.float64) / half)
    inv_lane = np.tile(np.repeat(inv, 2), 2 * LANES // HD)
    lane = np.arange(LANES)
    by_row = (lane % HD) < half
    ang = np.where(by_row[None], (t // GRID_W)[:, None], (t % GRID_W)[:, None]) * inv_lane[None]
    cos, sin = np.cos(ang), np.sin(ang)
    even = (lane % 2) == 0
    return tuple(jnp.asarray(a, F32) for a in (cos, np.where(even, -sin, 0.0), np.where(even, 0.0, sin)))


def _rope(x, a, b, c):
    return x * a + pltpu.roll(x, LANES - 1, axis=1) * b + pltpu.roll(x, 1, axis=1) * c


def _rope_or_identity(identity, ra_ref, rb_ref, rc_ref):
    return (jnp.where(identity, 1.0, ra_ref[...]), jnp.where(identity, 0.0, rb_ref[...]),
            jnp.where(identity, 0.0, rc_ref[...]))


def _rope_spec():
    return pl.BlockSpec((TM, LANES), lambda i: (jnp.maximum(i - NT_P, 0) % TILES_PER_GRID, 0))


def _x_specs():
    return [pl.BlockSpec((TM, D), lambda i: (jnp.minimum(i, NT_P - 1), 0)),
            pl.BlockSpec((TM, D), lambda i: (jnp.maximum(i - NT_P, 0), 0))]


def _cache_spec(heads, width):
    return pl.BlockSpec((None, None, heads, N_P, width), lambda i: (jnp.minimum(i, NT_P - 1), 0, 0, 0, 0))


def _row_spec(width):
    return pl.BlockSpec((TM, width), lambda i: (i, 0))


def _hm_spec(n):
    return pl.BlockSpec((n, TM, LANES), lambda i: (0, i, 0))


AB_Q0, AB_K0, AB_V0 = 1536, 2048, 2560


def _tile(p, col0, j):
    return p[:, col0 + j * LANES:col0 + (j + 1) * LANES]


def _ab_in_kernel(xp_ref, xs_ref, sh_ref, sc_ref, w_ref, ra_ref, rb_ref, rc_ref,
                  gb_ref, y_ref, q_ref, k_ref, v_ref, kc_ref, vc_ref):
    i = pl.program_id(0)
    is_p = i < NT_P
    ci = _cond_row(i)
    x = jnp.where(is_p, xp_ref[...], xs_ref[...])
    u = x * (1.0 + sc_ref[pl.ds(ci, 1), :]) + sh_ref[pl.ds(ci, 1), :]
    p = _dot(u.astype(BF16), w_ref[...])
    gb_ref[...] = p[:, 0:512]
    y_ref[...] = p[:, 512:1024] * p[:, 1024:1536]
    a, b, c = _rope_or_identity(is_p, ra_ref, rb_ref, rc_ref)
    for h in range(4):
        v_ref[h] = _tile(p, AB_V0, h).astype(BF16)
        q_ref[h] = (_rope(_tile(p, AB_Q0, h), a, b, c) * QK_SCALE).astype(BF16)
        k_ref[h] = _rope(_tile(p, AB_K0, h), a, b, c).astype(BF16)

    @pl.when(is_p)
    def _():
        for h in range(4):
            kc_ref[h] = _tile(p, AB_K0, h)
            vc_ref[h] = _tile(p, AB_V0, h)


def _ab_in(xp, xs, mod, w_in, rope):
    hm = jax.ShapeDtypeStruct((4, T, LANES), BF16)
    cache = jax.ShapeDtypeStruct((B_P, 1, 4, N_P, LANES), F32)
    half = jax.ShapeDtypeStruct((T, 512), F32)
    return pl.pallas_call(
        _ab_in_kernel,
        grid=(NT,),
        in_specs=_x_specs() + [_mod_spec(0, 0), _mod_spec(0, 1), _full((D, 3072)),
                               _rope_spec(), _rope_spec(), _rope_spec()],
        out_specs=[_row_spec(512), _row_spec(512), _hm_spec(4), _hm_spec(4), _hm_spec(4),
                   _cache_spec(4, LANES), _cache_spec(4, LANES)],
        out_shape=[half, half, hm, hm, hm, cache, cache],
        compiler_params=_params("arbitrary"),
        name="ab_in_proj",
    )(xp, xs, mod, mod, w_in, *rope)


CD_NQ, CD_NK, CD_NV, CD_GQ, CD_GK, CD_GV = 0, 512, 1024, 1536, 2048, 2176


def _seg_mean64(s):
    r = lax.broadcasted_iota(jnp.int32, (LANES, LANES), 0) // HD
    c = lax.broadcasted_iota(jnp.int32, (LANES, LANES), 1) // HD
    seg = jnp.where(r == c, 1.0, 0.0).astype(BF16)
    hi, lo = _split(s)
    return (_dot(hi, seg) + _dot(lo, seg)) * (1.0 / HD)


def _rms64(x, g):
    return x * lax.rsqrt(_seg_mean64(x * x) + RMS_EPS) * g


def _dup_halves(x, lo):
    r = pltpu.roll(x, HD, axis=1)
    return jnp.where(lo, x, r), jnp.where(lo, r, x)


def _cd_in_kernel(x_ref, sh_ref, sc_ref, w_ref, qg_ref, kg_ref, ra_ref, rb_ref, rc_ref,
                  nq_ref, nk_ref, nv_ref, gq_ref, gk_ref, gv_ref,
                  nkc_ref, nvc_ref, gkc_ref, gvc_ref):
    i = pl.program_id(0)
    is_p = i < NT_P
    ci = _cond_row(i)
    u = x_ref[...] * (1.0 + sc_ref[pl.ds(ci, 1), :]) + sh_ref[pl.ds(ci, 1), :]
    p = _dot(u.astype(BF16), w_ref[...])
    lo = _low_half(TM)
    for j in range(4):
        nq_ref[j] = (_tile(p, CD_NQ, j) * QK_SCALE).astype(BF16)
        nk_ref[j] = _tile(p, CD_NK, j).astype(BF16)
        nv_ref[j] = _tile(p, CD_NV, j).astype(BF16)
    gq = [_rms64(_tile(p, CD_GQ, j), qg_ref[...]) for j in range(4)]
    gk = _rms64(_tile(p, CD_GK, 0), kg_ref[...])
    gv = _tile(p, CD_GV, 0)
    v0, v1 = _dup_halves(gv, lo)
    gv_ref[0] = v0.astype(BF16)
    gv_ref[1] = v1.astype(BF16)
    a, b, c = _rope_or_identity(is_p, ra_ref, rb_ref, rc_ref)
    for j in range(4):
        gq_ref[j] = (_rope(gq[j], a, b, c) * QK_SCALE).astype(BF16)
    k0, k1 = _dup_halves(_rope(gk, a, b, c), lo)
    gk_ref[0] = k0.astype(BF16)
    gk_ref[1] = k1.astype(BF16)

    @pl.when(is_p)
    def _():
        gkc_ref[0] = k0[:, 0:HD]
        gkc_ref[1] = k1[:, 0:HD]
        gvc_ref[0] = v0[:, 0:HD]
        gvc_ref[1] = v1[:, 0:HD]
        for j in range(4):
            for src, dst in ((CD_NK, nkc_ref), (CD_NV, nvc_ref)):
                a, b = _dup_halves(_tile(p, src, j), lo)
                dst[2 * j] = a[:, 0:HD]
                dst[2 * j + 1] = b[:, 0:HD]


def _cd_in(x, mod, w_in, qg, kg, rope):
    hm4 = jax.ShapeDtypeStruct((4, T, LANES), BF16)
    hm2 = jax.ShapeDtypeStruct((2, T, LANES), BF16)
    c8 = jax.ShapeDtypeStruct((B_P, 1, 8, N_P, HD), F32)
    c2 = jax.ShapeDtypeStruct((B_P, 1, 2, N_P, HD), F32)
    return pl.pallas_call(
        _cd_in_kernel,
        grid=(NT,),
        in_specs=[_row_spec(D), _mod_spec(1, 0), _mod_spec(1, 1), _full((D, 2304)),
                  _full((1, LANES)), _full((1, LANES)), _rope_spec(), _rope_spec(), _rope_spec()],
        out_specs=[_hm_spec(4), _hm_spec(4), _hm_spec(4), _hm_spec(4), _hm_spec(2), _hm_spec(2),
                   _cache_spec(8, HD), _cache_spec(8, HD), _cache_spec(2, HD), _cache_spec(2, HD)],
        out_shape=[hm4, hm4, hm4, hm4, hm2, hm2, c8, c8, c2, c2],
        compiler_params=_params("arbitrary"),
        name="cd_in_proj",
    )(x, mod, mod, w_in, qg, kg, *rope)


def _stack_pairs(q_ref, n_q, tq):
    lo = _low_half(tq)
    parts = []
    for j in range(n_q):
        q = q_ref[j]
        zero = jnp.zeros_like(q)
        parts += [jnp.where(lo, q, zero), jnp.where(lo, zero, q)]
    return jnp.concatenate(parts, axis=0), lo


def _qk(qs, kb):
    return lax.dot_general(qs, kb, (((1,), (1,)), ((), ())), preferred_element_type=F32)


def _ctx_tile(ref, mode):
    if mode == "wide":
        x = ref[...]
    elif mode == "pair":
        x = jnp.concatenate([ref[0], ref[1]], axis=1)
    else:
        x = jnp.concatenate([ref[...], ref[...]], axis=1)
    return x.astype(BF16)


def _with_ones(v):
    lane = lax.broadcasted_iota(jnp.int32, v.shape, 1)
    return jnp.concatenate([v, jnp.where(lane == 0, 1.0, 0.0).astype(v.dtype)], axis=1)


def _flash_pair_kernel(*refs, gs, ctx_mode, **kw):
    it = iter(refs)
    q_ref, k_ref, v_ref = next(it), next(it), next(it)
    ctx = [next(it), next(it)] if ctx_mode else []
    rest = list(it)
    o_ref = rest.pop()
    n_q = q_ref.shape[0] // gs
    for t in range(gs * n_q):
        gi = t // n_q
        _flash_group(q_ref.at[t:t + 1], k_ref.at[gi], v_ref.at[gi], *[r.at[gi] for r in ctx], *rest,
                     o_ref.at[:, t * LANES:(t + 1) * LANES], n_q=1, ctx_mode=ctx_mode, **kw)


def _flash_group(*refs, n_q, tq, nk, tk, ctx_mode, post, lam_init):
    it = iter(refs)
    q_ref, k_ref, v_ref = next(it), next(it), next(it)
    kc_ref, vc_ref = (next(it), next(it)) if ctx_mode else (None, None)
    lam_ref, g_ref = (next(it), next(it)) if post == "diff" else (None, None)
    o_ref = next(it)

    qs, lo = _stack_pairs(q_ref, n_q, tq)
    rows = 2 * n_q * tq

    def step(kb, vb, carry):
        m, acc = carry
        s = _qk(qs, kb)
        m_new = jnp.maximum(m, jnp.max(s, axis=1, keepdims=True))
        p = jnp.exp2((s - m_new).astype(BF16))
        return m_new, jnp.exp2(m - m_new) * acc + _dot(p, _with_ones(vb))

    carry = (jnp.full((rows, 1), NEG, F32), jnp.zeros((rows, 2 * LANES), F32))
    for c in range(nk // tk):
        carry = step(k_ref[c * tk:(c + 1) * tk, :], v_ref[c * tk:(c + 1) * tk, :], carry)
    if ctx_mode:
        carry = step(_ctx_tile(kc_ref, ctx_mode), _ctx_tile(vc_ref, ctx_mode), carry)
    _, acc = carry
    o = acc[:, 0:LANES] / acc[:, LANES:LANES + 1]

    if post == "diff":
        lp = lam_ref[...]
        lam = (jnp.exp(jnp.sum(lp[0:1] * lp[1:2], axis=1, keepdims=True))
               - jnp.exp(jnp.sum(lp[2:3] * lp[3:4], axis=1, keepdims=True)) + lam_init)
        a = o[0:tq] - lam * o[tq:2 * tq]
        ms = jnp.mean(a * a, axis=-1, keepdims=True)
        o_ref[...] = (a * lax.rsqrt(ms + RMS_EPS) * g_ref[...] * (1.0 - lam_init)).astype(o_ref.dtype)
    else:
        for j in range(n_q):
            o_ref[:, j * LANES:(j + 1) * LANES] = jnp.where(
                lo, o[2 * j * tq:(2 * j + 1) * tq], o[(2 * j + 1) * tq:(2 * j + 2) * tq]).astype(o_ref.dtype)


def _flash_pair(q, k, v, *, latent, out_cols, col0, post, ctx=None, ctx_mode=None, diff=None,
                lam_init=0.0, prev=None):
    groups = k.shape[0]
    n_q = q.shape[0] // groups
    if latent:
        tq, nk, tk, gs = TM, N_S, 1024, 2
        grid = (B_S, groups // gs, TILES_PER_GRID)
        qrow = lambda b, g, i: NT_P + b * TILES_PER_GRID + i
        krow = lambda b, g, i: T_P // N_S + b
    else:
        tq, nk, tk, gs = N_P, N_P, N_P, groups
        grid = (B_P, 1, 1)
        qrow = lambda b, g, i: b
        krow = lambda b, g, i: b
    in_specs = [pl.BlockSpec((gs * n_q, tq, LANES), lambda b, g, i: (g, qrow(b, g, i), 0)),
                pl.BlockSpec((gs, nk, LANES), lambda b, g, i: (g, krow(b, g, i), 0)),
                pl.BlockSpec((gs, nk, LANES), lambda b, g, i: (g, krow(b, g, i), 0))]
    args = [q, k, v]
    if ctx is not None:
        width = LANES if ctx_mode == "wide" else HD
        spec = pl.BlockSpec((None, None, gs, PAST, width), lambda b, g, i: (b, 0, g, 0, 0))
        in_specs += [spec, spec]
        args += list(ctx)
    if diff is not None:
        in_specs += [pl.BlockSpec((4, HD), lambda b, g, i: (0, 0)),
                     pl.BlockSpec((1, LANES), lambda b, g, i: (0, 0))]
        args += list(diff)
    aliases = {}
    if prev is not None:
        aliases = {len(args): 0}
        in_specs.append(pl.BlockSpec(memory_space=pl.ANY))
        args.append(prev)

    def kernel(*refs):
        if prev is not None:
            refs = refs[:-2] + refs[-1:]
        _flash_pair_kernel(*refs, gs=gs, tq=tq, nk=nk, tk=tk, ctx_mode=ctx_mode if ctx is not None else None,
                           post=post, lam_init=lam_init)

    return pl.pallas_call(
        kernel,
        grid=grid,
        in_specs=in_specs,
        out_specs=pl.BlockSpec((tq, gs * n_q * LANES), lambda b, g, i: (qrow(b, g, i), col0 // gs + g)),
        out_shape=jax.ShapeDtypeStruct((T, out_cols), BF16),
        input_output_aliases=aliases,
        compiler_params=_params("arbitrary", "arbitrary", "arbitrary"),
        name="attn_" + post + ("_latent" if latent else "_context") + str(n_q),
    )(*args)


NA_TQ = NA_QROWS * GRID_W
NA_TK = NA_KROWS * GRID_W


NA_BLOCK_POS = ((0, 0), (NA_QROWS, 0), (GRID_H - NA_QROWS, GRID_H - NA_KROWS))
N_DR = 2 * NA_WIN_R - 1
N_DC = 2 * NA_WIN_C - 1


def _na_bias_kernel(rpb_ref, o_ref):
    qc = lax.broadcasted_iota(jnp.int32, (GRID_W, LANES), 0)
    lane = lax.broadcasted_iota(jnp.int32, (GRID_W, LANES), 1)
    kc = lane % GRID_W
    cs = jnp.clip(qc - NA_WIN_C // 2, 0, GRID_W - NA_WIN_C)
    col_ok = jnp.logical_and(kc >= cs, kc < cs + NA_WIN_C)
    lo = lane < GRID_W
    neg = jnp.full((GRID_W, LANES), NEG, F32)
    for head in range(2):
        toeplitz = []
        for dr in range(N_DR):
            r = jnp.broadcast_to(rpb_ref[head, dr:dr + 1, :] * LOG2E, (GRID_W, LANES))
            t = jnp.where(lo, pltpu.roll(r, LANES - (NA_WIN_C - 1), axis=1, stride=1, stride_axis=0),
                          pltpu.roll(r, GRID_W - (NA_WIN_C - 1), axis=1, stride=1, stride_axis=0))
            toeplitz.append(jnp.where(col_ok, t, neg))
        for pos, (r0, k0) in enumerate(NA_BLOCK_POS):
            for i in range(NA_QROWS):
                qr = r0 + i
                rs = min(max(qr - NA_WIN_R // 2, 0), GRID_H - NA_WIN_R)
                tiles = [toeplitz[k0 + j - qr + NA_WIN_R - 1] if rs <= k0 + j < rs + NA_WIN_R else neg
                         for j in range(NA_KROWS)]
                for jp in range(NA_KROWS // 2):
                    o_ref[pos, pl.ds(head * NA_TQ + i * GRID_W, GRID_W), pl.ds(jp * LANES, LANES)] = jnp.where(
                        lo, tiles[2 * jp], tiles[2 * jp + 1])


def _na_bias(rpb):
    rpb_pad = jnp.pad(rpb, ((0, 0), (0, 16 - N_DR), (0, LANES - N_DC)), constant_values=NEG)
    return pl.pallas_call(
        _na_bias_kernel,
        grid=(4,),
        in_specs=[pl.BlockSpec((2, 16, LANES), lambda g: (g, 0, 0))],
        out_specs=pl.BlockSpec((3, None, 2 * NA_TQ, NA_TK), lambda g: (0, g, 0, 0)),
        out_shape=jax.ShapeDtypeStruct((3, 4, 2 * NA_TQ, NA_TK), F32),
        compiler_params=_params("arbitrary"),
        name="na_bias_table",
    )(rpb_pad)


NA_PAIRS = 4


def _na_kernel(q_ref, k_ref, v_ref, kc_ref, vc_ref, bm_ref, _, o_ref):
    i = pl.program_id(2)
    k0 = jnp.clip(i * NA_QROWS - NA_WIN_R // 2, 0, GRID_H - NA_KROWS)
    start = pl.multiple_of(k0 * GRID_W, GRID_W)
    for g in range(NA_PAIRS):
        qs, lo = _stack_pairs(q_ref.at[g:g + 1], 1, NA_TQ)
        kw = k_ref[g, pl.ds(start, NA_TK), :]
        vw = v_ref[g, pl.ds(start, NA_TK), :]
        s_w = _qk(qs, kw) + bm_ref[g]
        s_c = _qk(qs, _ctx_tile(kc_ref.at[2 * g:2 * g + 2], "pair"))
        m = jnp.maximum(jnp.max(s_w, axis=1, keepdims=True), jnp.max(s_c, axis=1, keepdims=True))
        p_w = jnp.exp2((s_w - m).astype(BF16))
        p_c = jnp.exp2((s_c - m).astype(BF16))
        acc = _dot(p_w, _with_ones(vw)) + _dot(p_c, _with_ones(_ctx_tile(vc_ref.at[2 * g:2 * g + 2], "pair")))
        o = acc[:, 0:LANES] / acc[:, LANES:LANES + 1]
        o_ref[:, g * LANES:(g + 1) * LANES] = jnp.where(lo, o[0:NA_TQ], o[NA_TQ:2 * NA_TQ]).astype(o_ref.dtype)


def _na_latent(q, k, v, kc, vc, bm, prev):
    nblk = N_S // NA_TQ
    qrow = lambda b, g, i: T_P // NA_TQ + b * nblk + i
    krow = lambda b, g, i: T_P // N_S + b
    cfg = lambda i: jnp.where(i == 0, 0, jnp.where(i == nblk - 1, 2, 1))
    ctx_spec = pl.BlockSpec((None, None, 2 * NA_PAIRS, PAST, HD), lambda b, g, i: (b, 0, g, 0, 0))
    return pl.pallas_call(
        _na_kernel,
        grid=(B_S, 4 // NA_PAIRS, nblk),
        in_specs=[pl.BlockSpec((NA_PAIRS, NA_TQ, LANES), lambda b, g, i: (g, qrow(b, g, i), 0)),
                  pl.BlockSpec((NA_PAIRS, N_S, LANES), lambda b, g, i: (g, krow(b, g, i), 0)),
                  pl.BlockSpec((NA_PAIRS, N_S, LANES), lambda b, g, i: (g, krow(b, g, i), 0)),
                  ctx_spec, ctx_spec,
                  pl.BlockSpec((None, NA_PAIRS, 2 * NA_TQ, NA_TK), lambda b, g, i: (cfg(i), g, 0, 0)),
                  pl.BlockSpec(memory_space=pl.ANY)],
        out_specs=pl.BlockSpec((NA_TQ, NA_PAIRS * LANES), lambda b, g, i: (qrow(b, g, i), g)),
        out_shape=jax.ShapeDtypeStruct((T, D), BF16),
        input_output_aliases={6: 0},
        compiler_params=_params("arbitrary", "arbitrary", "arbitrary"),
        name="attn_window_latent",
    )(q, k, v, kc, vc, bm, prev)


def _top4(logits):
    lane = lax.broadcasted_iota(jnp.int32, logits.shape, 1).astype(F32)
    rest = logits
    tops, firsts = [], []
    for _ in range(TOP_K):
        m = jnp.max(rest, axis=1, keepdims=True)
        first = jnp.argmax(rest, axis=1, keepdims=True).astype(F32)
        tops.append(m)
        firsts.append(first)
        rest = jnp.where(lane == first, -jnp.inf, rest)
    es = [jnp.exp(m - tops[0]) for m in tops]
    denom = es[0] + es[1] + es[2] + es[3]
    w = jnp.zeros_like(logits)
    for k in range(TOP_K):
        w = jnp.where(lane == float(k), es[k] / denom, w)
    return firsts, w


def _local_sort(firsts):
    lane = lax.broadcasted_iota(jnp.int32, (TM, LANES), 1).astype(F32)
    hots = [lane == f for f in firsts]
    sel = jnp.zeros((TM, LANES), F32)
    for hot in hots:
        sel = jnp.where(hot, 1.0, sel)
    r = lax.broadcasted_iota(jnp.int32, (TM, TM), 0)
    c = lax.broadcasted_iota(jnp.int32, (TM, TM), 1)
    earlier = _dot(jnp.where(c < r, 1.0, 0.0).astype(BF16), sel.astype(BF16))
    cnt = jnp.sum(sel, axis=0, keepdims=True)
    r = lax.broadcasted_iota(jnp.int32, (LANES, LANES), 0)
    c = lax.broadcasted_iota(jnp.int32, (LANES, LANES), 1)
    start = _dot(jnp.broadcast_to(cnt, (SUB, LANES)).astype(BF16), jnp.where(r < c, 1.0, 0.0).astype(BF16))[0:1]
    place = start + earlier
    pos = jnp.zeros((TM, LANES), F32)
    for k, hot in enumerate(hots):
        pos = jnp.where(lane == float(k), jnp.sum(jnp.where(hot, place, 0.0), axis=1, keepdims=True), pos)
    return pos.astype(jnp.int32), cnt.astype(jnp.int32)


def _mixer_tail(x_of, h_of, ci, g1_ref, sh2_ref, sc2_ref, lng_ref, lnb_ref, rw_ref, rb_ref,
                x1_ref, u2_ref, pos_ref, w_ref, cnt_ref):
    rows = slice(0, TM)
    x1 = _layer_norm(DEEPNORM_ALPHA * x_of(rows) + g1_ref[pl.ds(ci, 1), :] * h_of(rows), lng_ref[...], lnb_ref[...])
    x1_ref[...] = x1
    u2 = x1 * (1.0 + sc2_ref[pl.ds(ci, 1), :]) + sh2_ref[pl.ds(ci, 1), :]
    u2_ref[...] = u2.astype(BF16)
    firsts, w_ref[...] = _top4(_dot3_narrow(u2, rw_ref[...]) + rb_ref[...])
    pos, cnt = _local_sort(firsts)
    pos_ref[...] = pos
    cnt_ref[...] = jnp.broadcast_to(cnt, (SUB, LANES))


def _ab_out_kernel(xp_ref, xs_ref, gb_ref, y_ref, yp_ref, yn_ref, cv_ref, at_ref, wo_ref,
                   g1_ref, sh2_ref, sc2_ref, lng_ref, lnb_ref, rw_ref, rb_ref, *out_refs):
    i = pl.program_id(0)
    is_p = i < NT_P
    ci = _cond_row(i)
    j = (i - NT_P) % TILES_PER_GRID
    first = jnp.logical_or(is_p, j == 0)
    last = jnp.logical_or(is_p, j == TILES_PER_GRID - 1)
    y = y_ref[...]
    row = lax.broadcasted_iota(jnp.int32, y.shape, 0)
    before = jnp.where(first, 0.0, yp_ref[7:8, :])
    after = jnp.where(last, 0.0, yn_ref[0:1, :])
    y_prev = jnp.where(row == 0, before, pltpu.roll(y, 1, axis=0))
    y_next = jnp.where(row == TM - 1, after, pltpu.roll(y, TM - 1, axis=0))
    cv = cv_ref[...]
    conv = (gb_ref[...] * (y_prev * cv[0:1] + y * cv[1:2] + y_next * cv[2:3])).astype(BF16)
    _mixer_tail(lambda r: jnp.where(is_p, xp_ref[r, :], xs_ref[r, :]),
                lambda r: _dot(conv[r], wo_ref[0:512, :]) + _dot(at_ref[r, :], wo_ref[512:1024, :]),
                ci, g1_ref, sh2_ref, sc2_ref, lng_ref, lnb_ref, rw_ref, rb_ref, *out_refs)


def _cd_out_kernel(x_ref, mg_ref, wo_ref, g1_ref, sh2_ref, sc2_ref, lng_ref, lnb_ref, rw_ref, rb_ref,
                   *out_refs):
    ci = _cond_row(pl.program_id(0))
    _mixer_tail(lambda r: x_ref[r, :], lambda r: _dot(mg_ref[r, :], wo_ref[...]),
                ci, g1_ref, sh2_ref, sc2_ref, lng_ref, lnb_ref, rw_ref, rb_ref, *out_refs)


def _tail_specs(layer):
    return [_mod_spec(layer, 2), _mod_spec(layer, 3), _mod_spec(layer, 4),
            _full((1, D)), _full((1, D)), _full((D, LANES)), _full((1, LANES))]


_TAIL_OUT_SPECS = [_row_spec(D), _row_spec(D), _row_spec(LANES), _row_spec(LANES),
                   pl.BlockSpec((None, SUB, LANES), lambda i: (i, 0, 0))]
_TAIL_OUT_SHAPES = [jax.ShapeDtypeStruct((T, D), F32), jax.ShapeDtypeStruct((T, D), BF16),
                    jax.ShapeDtypeStruct((T, LANES), jnp.int32), jax.ShapeDtypeStruct((T, LANES), F32),
                    jax.ShapeDtypeStruct((NT, SUB, LANES), jnp.int32)]


def _ab_out(xp, xs, gb, y, conv_w, attn, w_out, mod, tail):
    halo_prev = pl.BlockSpec((8, 512), lambda i: (jnp.maximum(i * (TM // 8) - 1, 0), 0))
    halo_next = pl.BlockSpec((8, 512), lambda i: (jnp.minimum((i + 1) * (TM // 8), T // 8 - 1), 0))
    return pl.pallas_call(
        _ab_out_kernel,
        grid=(NT,),
        in_specs=_x_specs() + [_row_spec(512), _row_spec(512), halo_prev, halo_next, _full((3, 512)),
                               _row_spec(512), _full((D, D))] + _tail_specs(0),
        out_specs=_TAIL_OUT_SPECS,
        out_shape=_TAIL_OUT_SHAPES,
        compiler_params=_params("arbitrary"),
        name="ab_out_proj",
    )(xp, xs, gb, y, y, y, conv_w, attn, w_out, mod, mod, mod, *tail)


def _cd_out(x, merged, w_out, mod, tail):
    return pl.pallas_call(
        _cd_out_kernel,
        grid=(NT,),
        in_specs=[_row_spec(D), _row_spec(D), _full((D, D))] + _tail_specs(1),
        out_specs=_TAIL_OUT_SPECS,
        out_shape=_TAIL_OUT_SHAPES,
        compiler_params=_params("arbitrary"),
        name="cd_out_proj",
    )(x, merged, w_out, mod, mod, mod, *tail)


def _mix_down(wd_ref, wmix_ref):
    half = D_FF // 2
    for c in range(D // LANES):
        wmix_ref[c, pl.ds(0, half, stride=2), :] = wd_ref[0:half, c * LANES:(c + 1) * LANES]
        wmix_ref[c, pl.ds(1, half, stride=2), :] = wd_ref[half:D_FF, c * LANES:(c + 1) * LANES]


def _prep_expert(wgu_ref, wd_ref, wgu_bf, wmix_ref, wd_bf):
    for c in range(4):
        wgu_bf[:, c * 512:(c + 1) * 512] = wgu_ref[:, c * 512:(c + 1) * 512].astype(BF16)
    _mix_down(wd_ref, wmix_ref)
    for c in range(D // LANES):
        wd_bf[:, c * LANES:(c + 1) * LANES] = wmix_ref[c].astype(BF16)


def _expert_ffn(u, wgu_bf, bgu_ref, wd_bf, bd_ref):
    rows = u.shape[0]
    ga = _dot(u, wgu_bf[:, 0:D_FF]) + bgu_ref[:, 0:D_FF]
    gb = _dot(u, wgu_bf[:, D_FF:2 * D_FF]) + bgu_ref[:, D_FF:2 * D_FF]
    even = (lax.broadcasted_iota(jnp.int32, (rows, LANES), 1) % 2) == 0
    hid = []
    for c in range(D_FF // LANES):
        a = ga[:, c * LANES:(c + 1) * LANES]
        b = gb[:, c * LANES:(c + 1) * LANES]
        gate = jnp.where(even, a, pltpu.roll(b, 1, axis=1))
        up = jnp.where(even, pltpu.roll(a, LANES - 1, axis=1), b)
        gate = jnp.minimum(gate, SWIGLU_LIMIT)
        up = jnp.clip(up, -SWIGLU_LIMIT, SWIGLU_LIMIT)
        hid.append(((up + 1.0) * gate * jax.nn.sigmoid(SWIGLU_ALPHA * gate)).astype(BF16))
    hid = jnp.concatenate(hid, axis=1)
    return _dot(hid, wd_bf[...]) + bd_ref[...]


def _moe_plan(cnt):
    c = cnt[:, 0, :N_EXPERTS]
    counts = jnp.sum(c, axis=0)
    tiles_e = (counts + MOE_TM - 1) // MOE_TM
    tile_end = jnp.cumsum(tiles_e)
    first_row = (tile_end - tiles_e) * MOE_TM
    run_start = first_row[None] + jnp.cumsum(c, axis=0) - c
    j = jnp.arange(MOE_TILES, dtype=jnp.int32)
    tile_expert = jnp.minimum(jnp.sum((j[:, None] >= tile_end[None]).astype(jnp.int32), axis=1), N_EXPERTS - 1)
    i32 = lambda a: a.astype(jnp.int32)
    experts = jnp.arange(N_EXPERTS, dtype=jnp.int32)
    owns = tiles_e > 0
    seg = jnp.cumsum(owns.astype(jnp.int32)) - owns.astype(jnp.int32)
    later = jnp.where((experts[None] > experts[:, None]) & owns[None], experts[None], N_EXPERTS)
    nxt = jnp.min(later, axis=1)
    nxt = jnp.where(nxt == N_EXPERTS, -1, nxt)
    per_tile = lambda v: jnp.sum(jnp.where(tile_expert[:, None] == experts[None], v[None], 0), axis=1)
    return dict(tile_expert=i32(tile_expert), n_used=i32(tile_end[-1:]),
                seg_parity=i32(per_tile(seg % 2)), next_expert=i32(per_tile(nxt)),
                run_start=i32(run_start)[:, None, :], run_len=i32(c)[:, None, :],
                pad_start=i32(first_row + counts)[None, None, :], pad_len=i32(tiles_e * MOE_TM - counts)[None, None, :])


def _for_each_run(start_ref, len_ref, copy):
    def body(e, local):
        n = len_ref[0, e]
        g = start_ref[0, e]

        def pieces(sizes):
            for size in sizes:
                covered = jnp.bitwise_and(n, -2 * size)

                @pl.when(jnp.bitwise_and(n, size) != 0)
                def _(covered=covered, size=size):
                    copy(local + covered, g + covered, size)

        @pl.when(n >= RUN_SIZES[2])
        def _():
            pieces(RUN_SIZES[:3])

        pieces(RUN_SIZES[3:])
        return local + n
    lax.fori_loop(0, N_EXPERTS, body, jnp.int32(0))


def _rows(ref, start, size):
    start = 0 if isinstance(start, int) and start == 0 else pl.multiple_of(start * SUB, SUB)
    return ref.at[pl.ds(start, size * SUB), :]


def _from_tiles(ref, rows):
    return jnp.concatenate([ref[pl.ds(c, rows, stride=SUB), :] for c in range(SUB)], axis=1)


def _to_tiles(ref, x):
    for c in range(SUB):
        ref[pl.ds(c, x.shape[0], stride=SUB), :] = x[:, c * LANES:(c + 1) * LANES]


def _run_spec(index):
    return pl.BlockSpec((None, 1, N_EXPERTS), index, memory_space=pltpu.SMEM)


def _dispatch_kernel(rs_ref, rl_ref, ps_ref, pn_ref, u_ref, pos_ref, xs_hbm, stage, zeros, sem, zsem):
    i = pl.program_id(0)
    slot = i % 2

    def wait(s):
        pltpu.make_async_copy(stage.at[s], stage.at[s], sem.at[s]).wait()

    @pl.when(i == 0)
    def _():
        zeros[...] = jnp.zeros_like(zeros)

        def pad(local, g, size):
            return pltpu.make_async_copy(_rows(zeros, 0, size), _rows(xs_hbm, g, size), zsem.at[0])
        _for_each_run(ps_ref, pn_ref, lambda *a: pad(*a).start())
        _for_each_run(ps_ref, pn_ref, lambda *a: pad(*a).wait())

    @pl.when(i >= 2)
    def _():
        wait(slot)

    place = pos_ref[...].astype(F32).T
    s = lax.broadcasted_iota(jnp.int32, (PAIRS_TM, TM), 0).astype(F32)
    onehot = jnp.zeros((PAIRS_TM, TM), F32)
    for k in range(TOP_K):
        onehot = jnp.where(s == place[k:k + 1, :], 1.0, onehot)
    _to_tiles(stage.at[slot], _dot(onehot.astype(BF16), u_ref[...]))
    _for_each_run(rs_ref, rl_ref, lambda l, g, size: pltpu.make_async_copy(
        _rows(stage.at[slot], l, size), _rows(xs_hbm, g, size), sem.at[slot]).start())

    @pl.when(i == NT - 1)
    def _():
        wait(1 - slot)
        wait(slot)


def _dispatch(u2, pos, plan):
    return pl.pallas_call(
        _dispatch_kernel,
        grid=(NT,),
        in_specs=[_run_spec(lambda i: (i, 0, 0)), _run_spec(lambda i: (i, 0, 0)),
                  _run_spec(lambda i: (0, 0, 0)), _run_spec(lambda i: (0, 0, 0)),
                  _row_spec(D), _row_spec(LANES)],
        out_specs=pl.BlockSpec(memory_space=pl.ANY),
        out_shape=jax.ShapeDtypeStruct((MOE_TILES * MOE_TM * SUB, LANES), F32),
        scratch_shapes=[pltpu.VMEM((2, PAIRS_TM * SUB, LANES), F32), pltpu.VMEM((TM * SUB, LANES), F32),
                        pltpu.SemaphoreType.DMA((2,)), pltpu.SemaphoreType.DMA((1,))],
        compiler_params=_params("arbitrary"),
        name="moe_dispatch",
    )(plan["run_start"], plan["run_len"], plan["pad_start"], plan["pad_len"], u2, pos)


def _moe_kernel(te_ref, nu_ref, par_ref, nxt_ref, x_ref, bgu_ref, bd_ref, wgu_hbm, wd_hbm, y_ref,
                wgu_f, wd_f, wgu_bf, wmix_ref, wd_bf, sem, *, layer):
    j = pl.program_id(0)
    used = j < nu_ref[0]
    buf = par_ref[j]

    def copies(e, s):
        return (pltpu.make_async_copy(wgu_hbm.at[layer, e], wgu_f.at[s], sem.at[s]),
                pltpu.make_async_copy(wd_hbm.at[layer, e], wd_f.at[s], sem.at[s]))

    @pl.when(jnp.logical_and(used, j == 0))
    def _():
        for cp in copies(te_ref[0], 0):
            cp.start(priority=1)

    @pl.when(jnp.logical_and(used, jnp.logical_or(j == 0, te_ref[j] != te_ref[jnp.maximum(j - 1, 0)])))
    def _():
        for cp in copies(te_ref[j], buf):
            cp.wait()

        @pl.when(nxt_ref[j] >= 0)
        def _():
            for cp in copies(nxt_ref[j], 1 - buf):
                cp.start(priority=1)

        _prep_expert(wgu_f.at[buf], wd_f.at[buf], wgu_bf, wmix_ref, wd_bf)

    @pl.when(used)
    def _():
        x = _from_tiles(x_ref, MOE_TM).astype(BF16)
        _to_tiles(y_ref, _expert_ffn(x, wgu_bf, bgu_ref, wd_bf, bd_ref))


def _moe(layer, xs, plan, w_gate_up, b_gate_up, w_down, b_down):
    rows = pl.BlockSpec((MOE_TM * SUB, LANES), lambda j, te, nu, par, nxt: (jnp.minimum(j, nu[0] - 1), 0))
    grid_spec = pltpu.PrefetchScalarGridSpec(
        num_scalar_prefetch=4,
        grid=(MOE_TILES,),
        in_specs=[rows,
                  pl.BlockSpec((None, None, 1, 2 * D_FF), lambda j, te, nu, par, nxt: (layer, te[j], 0, 0)),
                  pl.BlockSpec((None, None, 1, D), lambda j, te, nu, par, nxt: (layer, te[j], 0, 0)),
                  pl.BlockSpec(memory_space=pl.ANY), pl.BlockSpec(memory_space=pl.ANY)],
        out_specs=rows,
        scratch_shapes=[pltpu.VMEM((2, D, 2 * D_FF), F32), pltpu.VMEM((2, D_FF, D), F32),
                        pltpu.VMEM((D, 2 * D_FF), BF16), pltpu.VMEM((D // LANES, D_FF, LANES), F32),
                        pltpu.VMEM((D_FF, D), BF16), pltpu.SemaphoreType.DMA((2,))])

    def kernel(*refs):
        _moe_kernel(*refs, layer=layer)

    return pl.pallas_call(
        kernel,
        grid_spec=grid_spec,
        out_shape=jax.ShapeDtypeStruct((MOE_TILES * MOE_TM * SUB, LANES), F32),
        compiler_params=_params("arbitrary"),
        name="moe_experts",
    )(plan["tile_expert"], plan["n_used"], plan["seg_parity"], plan["next_expert"], xs,
      b_gate_up.reshape(DEPTH, N_EXPERTS, 1, 2 * D_FF), b_down.reshape(DEPTH, N_EXPERTS, 1, D), w_gate_up, w_down)


def _combine_kernel(rs_ref, rl_ref, rsn_ref, rln_ref, x1_ref, pos_ref, w_ref, g2_ref, lng_ref, lnb_ref, ys_hbm,
                    *rest, final):
    out_refs, (stage, sem) = rest[:-2], rest[-2:]
    i = pl.program_id(0)
    slot = i % 2
    other = 1 - slot

    def fetch(start_ref, len_ref, s):
        _for_each_run(start_ref, len_ref, lambda l, g, size: pltpu.make_async_copy(
            _rows(ys_hbm, g, size), _rows(stage.at[s], l, size), sem.at[s]).start())

    def wait(s):
        pltpu.make_async_copy(stage.at[s], stage.at[s], sem.at[s]).wait()

    @pl.when(i == 0)
    def _():
        fetch(rs_ref, rl_ref, 0)

    fetch(rsn_ref, rln_ref, other)
    wait(slot)
    y = _from_tiles(stage.at[slot], PAIRS_TM).astype(BF16)
    place = pos_ref[...].astype(F32)
    w = w_ref[...]
    s = lax.broadcasted_iota(jnp.int32, (TM, PAIRS_TM), 1).astype(F32)
    pick = jnp.zeros((TM, PAIRS_TM), F32)
    for k in range(TOP_K):
        pick = jnp.where(s == place[:, k:k + 1], w[:, k:k + 1], pick)
    f = _dot(pick.astype(BF16), y)
    ci = _cond_row(i)
    z = DEEPNORM_ALPHA * x1_ref[...] + g2_ref[pl.ds(ci, 1), :] * f
    out = _layer_norm(z, lng_ref[...], lnb_ref[...])
    if final:
        @pl.when(i < NT_P)
        def _():
            out_refs[0][...] = out

        @pl.when(i >= NT_P)
        def _():
            out_refs[1][...] = out
    else:
        out_refs[0][...] = out

    @pl.when(i == NT - 1)
    def _():
        wait(other)


def _combine(layer, x1, ys, pos, w, plan, mod, g, b):
    final = layer == DEPTH - 1
    nxt = lambda i: (jnp.minimum(i + 1, NT - 1), 0, 0)
    cur = lambda i: (i, 0, 0)
    if final:
        out_specs = _x_specs()
        out_shape = [jax.ShapeDtypeStruct((T_P, D), F32), jax.ShapeDtypeStruct((T_S, D), F32)]
    else:
        out_specs = _row_spec(D)
        out_shape = jax.ShapeDtypeStruct((T, D), F32)

    def kernel(*refs):
        _combine_kernel(*refs, final=final)

    return pl.pallas_call(
        kernel,
        grid=(NT,),
        in_specs=[_run_spec(cur), _run_spec(cur), _run_spec(nxt), _run_spec(nxt),
                  _row_spec(D), _row_spec(LANES), _row_spec(LANES), _mod_spec(layer, 5), _full((1, D)), _full((1, D)),
                  pl.BlockSpec(memory_space=pl.ANY)],
        out_specs=out_specs,
        out_shape=out_shape,
        scratch_shapes=[pltpu.VMEM((2, PAIRS_TM * SUB, LANES), F32), pltpu.SemaphoreType.DMA((2,))],
        compiler_params=_params("arbitrary"),
        name="moe_combine_norm" + ("_final" if final else ""),
    )(plan["run_start"], plan["run_len"], plan["run_start"], plan["run_len"], x1, pos, w, mod, g, b, ys)


def _moe_layer(layer, routed, experts, mod, ln2_g, ln2_b):
    x1, u2, pos, w, cnt = routed
    plan = _moe_plan(cnt)
    ys = _moe(layer, _dispatch(u2, pos, plan), plan, *experts)
    return _combine(layer, x1, ys, pos, w, plan, mod, ln2_g[layer][None], ln2_b[layer][None])


def _router_tail(l, ln1_g, ln1_b, router_w, router_b):
    rw = jnp.pad(router_w[l], ((0, 0), (0, LANES - N_EXPERTS)))
    rb = jnp.pad(router_b[l], (0, LANES - N_EXPERTS), constant_values=NEG)
    return ln1_g[l][None], ln1_b[l][None], rw, rb[None]


def kernel(x_prompt, x_sample, c, c_ctx, cache_diff_k, cache_diff_v, cache_na_k, cache_na_v, cache_gqa_k, cache_gqa_v, w_mod, b_mod, ln1_g, ln1_b, ln2_g, ln2_b, ab_w_in, ab_conv_w, ab_lambda_q1, ab_lambda_k1, ab_lambda_q2, ab_lambda_k2, ab_subln_g, ab_w_out, cd_w_in, cd_na_rpb, cd_q_norm_g, cd_k_norm_g, cd_w_out, router_w, router_b, w_gate_up, b_gate_up, w_down, b_down):
    xp = x_prompt.reshape(T_P, D)
    xs = x_sample.reshape(T_S, D)
    cond8 = jnp.concatenate([c_ctx[None], c, jnp.zeros((8 - 1 - B_S, D), F32)], axis=0)
    mod = _modulation(cond8, w_mod, b_mod)
    rope = _rope_tables()
    experts = (w_gate_up, b_gate_up, w_down, b_down)

    gb, y, q, k, v, new_diff_k, new_diff_v = _ab_in(xp, xs, mod, ab_w_in[0].astype(BF16), rope)
    lam_init = 0.8 - 0.6 * 1.0
    diff = (jnp.stack([ab_lambda_q1[0], ab_lambda_k1[0], ab_lambda_q2[0], ab_lambda_k2[0]]), ab_subln_g[0][None])
    attn = _flash_pair(q, k, v, latent=False, out_cols=512, col0=0, post="diff", diff=diff, lam_init=lam_init)
    attn = _flash_pair(q, k, v, latent=True, out_cols=512, col0=0, post="diff", diff=diff, lam_init=lam_init,
                       ctx=(cache_diff_k, cache_diff_v), ctx_mode="wide", prev=attn)
    routed = _ab_out(xp, xs, gb, y, ab_conv_w[0], attn, ab_w_out[0].astype(BF16), mod,
                     _router_tail(0, ln1_g, ln1_b, router_w, router_b))
    x = _moe_layer(0, routed, experts, mod, ln2_g, ln2_b)

    qg = jnp.tile(cd_q_norm_g[0], 2)[None]
    kg = jnp.tile(cd_k_norm_g[0], 2)[None]
    nq, nk, nv, gq, gk, gv, new_na_k, new_na_v, new_gqa_k, new_gqa_v = _cd_in(
        x, mod, cd_w_in[0].astype(BF16), qg, kg, rope)
    merged = _flash_pair(nq, nk, nv, latent=False, out_cols=D, col0=0, post="select")
    merged = _flash_pair(gq, gk, gv, latent=False, out_cols=D, col0=2, post="select", prev=merged)
    merged = _na_latent(nq, nk, nv, cache_na_k, cache_na_v, _na_bias(cd_na_rpb[0]), merged)
    merged = _flash_pair(gq, gk, gv, latent=True, out_cols=D, col0=2, post="select",
                         ctx=(cache_gqa_k, cache_gqa_v), ctx_mode="dup", prev=merged)
    routed = _cd_out(x, merged, cd_w_out[0].astype(BF16), mod,
                     _router_tail(1, ln1_g, ln1_b, router_w, router_b))
    y_p, y_s = _moe_layer(1, routed, experts, mod, ln2_g, ln2_b)

    return (y_p.reshape(B_P, N_P, D), y_s.reshape(B_S, N_S, D), new_diff_k, new_diff_v,
            new_na_k, new_na_v, new_gqa_k, new_gqa_v)
```

```python
import jax
import jax.numpy as jnp
import numpy as np
from jax import lax
from jax.experimental import pallas as pl
from jax.experimental.pallas import tpu as pltpu

F32 = jnp.float32
BF16 = jnp.bfloat16

D = 1024
B_P, N_P = 16, 256
B_S, N_S = 2, 4096
PAST = 256
T_P, T_S = B_P * N_P, B_S * N_S
T = T_P + T_S
TM = 256
NT_P, NT_S, NT = T_P // TM, T_S // TM, T // TM
TILES_PER_GRID = N_S // TM
GRID_W = 64
GRID_H = N_S // GRID_W
HD = 64
DEPTH = 2
N_EXPERTS = 32
TOP_K = 4
D_FF = 1024
NA_WIN_R, NA_WIN_C = 8, 16
NA_QROWS = 4
NA_KROWS = 12
SWIGLU_LIMIT = 7.0
SWIGLU_ALPHA = 1.702
ROPE_THETA = 10000.0
DEEPNORM_ALPHA = (2 * DEPTH) ** 0.25
LN_EPS = 1e-5
RMS_EPS = 1e-6
LOG2E = 1.4426950408889634
QK_SCALE = HD ** -0.5 * LOG2E
NEG = -1e30
MOE_TM = 256
MOE_PAIRS = TOP_K * T
MOE_TILES = MOE_PAIRS // MOE_TM + N_EXPERTS
PAIRS_TM = TOP_K * TM
RUN_SIZES = tuple(TM >> b for b in range(TM.bit_length()))
LANES = 128
SUB = 8

VMEM_LIMIT = 56 * 1024 * 1024


def _params(*sem):
    return pltpu.CompilerParams(dimension_semantics=sem, vmem_limit_bytes=VMEM_LIMIT)


def _split(x):
    hi = x.astype(BF16)
    lo = (x - hi.astype(F32)).astype(BF16)
    return hi, lo


def _dot(a, b):
    return jnp.dot(a, b, preferred_element_type=F32)


def _dot3(a, b):
    ah, al = _split(a)
    bh, bl = _split(b)
    return _dot(ah, bh) + (_dot(ah, bl) + _dot(al, bh))


def _cond_row(i):
    return jnp.where(i < NT_P, 0, 1 + (i - NT_P) // TILES_PER_GRID)


def _layer_norm(z, g, b):
    mu = jnp.mean(z, axis=-1, keepdims=True)
    zc = z - mu
    var = jnp.mean(zc * zc, axis=-1, keepdims=True)
    return zc * lax.rsqrt(var + LN_EPS) * g + b


def _low_half(rows):
    return lax.broadcasted_iota(jnp.int32, (rows, LANES), 1) < HD


def _mod_kernel(c_ref, w_ref, b_ref, o_ref):
    c = c_ref[...]
    o_ref[...] = _dot3(c * jax.nn.sigmoid(c), w_ref[...]) + b_ref[...]


def _modulation(cond8, w_mod, b_mod):
    return pl.pallas_call(
        _mod_kernel,
        grid=(DEPTH, 6),
        in_specs=[pl.BlockSpec((8, D), lambda l, j: (0, 0)),
                  pl.BlockSpec((None, D, D), lambda l, j: (l, 0, j)),
                  pl.BlockSpec((None, 1, D), lambda l, j: (l, 0, j))],
        out_specs=pl.BlockSpec((None, 8, D), lambda l, j: (l, 0, j)),
        out_shape=jax.ShapeDtypeStruct((DEPTH, 8, 6 * D), F32),
        compiler_params=_params("arbitrary", "arbitrary"),
        name="modulation",
    )(cond8, w_mod, b_mod.reshape(DEPTH, 1, 6 * D))


def _mod_spec(layer, chunk):
    return pl.BlockSpec((None, 8, D), lambda i, _l=layer, _c=chunk: (_l, 0, _c))


def _full(shape):
    return pl.BlockSpec(shape, lambda i: (0,) * len(shape))


def _rope_tables():
    t = np.arange(N_S)
    half = HD // 2
    inv = ROPE_THETA ** (-np.arange(0, half, 2, dtype=np.float64) / half)
    inv_lane = np.tile(np.repeat(inv, 2), 2 * LANES // HD)
    lane = np.arange(LANES)
    by_row = (lane % HD) < half
    ang = np.where(by_row[None], (t // GRID_W)[:, None], (t % GRID_W)[:, None]) * inv_lane[None]
    cos, sin = np.cos(ang), np.sin(ang)
    even = (lane % 2) == 0
    return tuple(jnp.asarray(a, F32) for a in (cos, np.where(even, -sin, 0.0), np.where(even, 0.0, sin)))


def _rope(x, a, b, c):
    return x * a + pltpu.roll(x, LANES - 1, axis=1) * b + pltpu.roll(x, 1, axis=1) * c


def _rope_or_identity(identity, ra_ref, rb_ref, rc_ref):
    return (jnp.where(identity, 1.0, ra_ref[...]), jnp.where(identity, 0.0, rb_ref[...]),
            jnp.where(identity, 0.0, rc_ref[...]))


def _rope_spec():
    return pl.BlockSpec((TM, LANES), lambda i: (jnp.maximum(i - NT_P, 0) % TILES_PER_GRID, 0))


def _x_specs():
    return [pl.BlockSpec((TM, D), lambda i: (jnp.minimum(i, NT_P - 1), 0)),
            pl.BlockSpec((TM, D), lambda i: (jnp.maximum(i - NT_P, 0), 0))]


def _cache_spec(heads, width):
    return pl.BlockSpec((None, None, heads, N_P, width), lambda i: (jnp.minimum(i, NT_P - 1), 0, 0, 0, 0))


def _row_spec(width):
    return pl.BlockSpec((TM, width), lambda i: (i, 0))


def _hm_spec(n):
    return pl.BlockSpec((n, TM, LANES), lambda i: (0, i, 0))


AB_Q0, AB_K0, AB_V0 = 1536, 2048, 2560


def _tile(p, col0, j):
    return p[:, col0 + j * LANES:col0 + (j + 1) * LANES]


def _ab_in_kernel(xp_ref, xs_ref, sh_ref, sc_ref, w_ref, ra_ref, rb_ref, rc_ref,
                  gb_ref, y_ref, q_ref, k_ref, v_ref, kc_ref, vc_ref):
    i = pl.program_id(0)
    is_p = i < NT_P
    ci = _cond_row(i)
    x = jnp.where(is_p, xp_ref[...], xs_ref[...])
    u = x * (1.0 + sc_ref[pl.ds(ci, 1), :]) + sh_ref[pl.ds(ci, 1), :]
    p = _dot(u.astype(BF16), w_ref[...])
    gb_ref[...] = p[:, 0:512]
    y_ref[...] = p[:, 512:1024] * p[:, 1024:1536]
    a, b, c = _rope_or_identity(is_p, ra_ref, rb_ref, rc_ref)
    for h in range(4):
        v_ref[h] = _tile(p, AB_V0, h).astype(BF16)
        q_ref[h] = (_rope(_tile(p, AB_Q0, h), a, b, c) * QK_SCALE).astype(BF16)
        k_ref[h] = _rope(_tile(p, AB_K0, h), a, b, c).astype(BF16)

    @pl.when(is_p)
    def _():
        for h in range(4):
            kc_ref[h] = _tile(p, AB_K0, h)
            vc_ref[h] = _tile(p, AB_V0, h)


def _ab_in(xp, xs, mod, w_in, rope):
    hm = jax.ShapeDtypeStruct((4, T, LANES), BF16)
    cache = jax.ShapeDtypeStruct((B_P, 1, 4, N_P, LANES), F32)
    half = jax.ShapeDtypeStruct((T, 512), F32)
    return pl.pallas_call(
        _ab_in_kernel,
        grid=(NT,),
        in_specs=_x_specs() + [_mod_spec(0, 0), _mod_spec(0, 1), _full((D, 3072)),
                               _rope_spec(), _rope_spec(), _rope_spec()],
        out_specs=[_row_spec(512), _row_spec(512), _hm_spec(4), _hm_spec(4), _hm_spec(4),
                   _cache_spec(4, LANES), _cache_spec(4, LANES)],
        out_shape=[half, half, hm, hm, hm, cache, cache],
        compiler_params=_params("arbitrary"),
        name="ab_in_proj",
    )(xp, xs, mod, mod, w_in, *rope)


CD_NQ, CD_NK, CD_NV, CD_GQ, CD_GK, CD_GV = 0, 512, 1024, 1536, 2048, 2176


def _seg_mean64(s):
    r = lax.broadcasted_iota(jnp.int32, (LANES, LANES), 0) // HD
    c = lax.broadcasted_iota(jnp.int32, (LANES, LANES), 1) // HD
    seg = jnp.where(r == c, 1.0, 0.0).astype(BF16)
    hi, lo = _split(s)
    return (_dot(hi, seg) + _dot(lo, seg)) * (1.0 / HD)


def _rms64(x, g):
    return x * lax.rsqrt(_seg_mean64(x * x) + RMS_EPS) * g


def _dup_halves(x, lo):
    r = pltpu.roll(x, HD, axis=1)
    return jnp.where(lo, x, r), jnp.where(lo, r, x)


def _cd_in_kernel(x_ref, sh_ref, sc_ref, w_ref, qg_ref, kg_ref, ra_ref, rb_ref, rc_ref,
                  nq_ref, nk_ref, nv_ref, gq_ref, gk_ref, gv_ref,
                  nkc_ref, nvc_ref, gkc_ref, gvc_ref):
    i = pl.program_id(0)
    is_p = i < NT_P
    ci = _cond_row(i)
    u = x_ref[...] * (1.0 + sc_ref[pl.ds(ci, 1), :]) + sh_ref[pl.ds(ci, 1), :]
    p = _dot(u.astype(BF16), w_ref[...])
    lo = _low_half(TM)
    for j in range(4):
        nq_ref[j] = (_tile(p, CD_NQ, j) * QK_SCALE).astype(BF16)
        nk_ref[j] = _tile(p, CD_NK, j).astype(BF16)
        nv_ref[j] = _tile(p, CD_NV, j).astype(BF16)
    gq = [_rms64(_tile(p, CD_GQ, j), qg_ref[...]) for j in range(4)]
    gk = _rms64(_tile(p, CD_GK, 0), kg_ref[...])
    gv = _tile(p, CD_GV, 0)
    v0, v1 = _dup_halves(gv, lo)
    gv_ref[0] = v0.astype(BF16)
    gv_ref[1] = v1.astype(BF16)
    a, b, c = _rope_or_identity(is_p, ra_ref, rb_ref, rc_ref)
    for j in range(4):
        gq_ref[j] = (_rope(gq[j], a, b, c) * QK_SCALE).astype(BF16)
    k0, k1 = _dup_halves(_rope(gk, a, b, c), lo)
    gk_ref[0] = k0.astype(BF16)
    gk_ref[1] = k1.astype(BF16)

    @pl.when(is_p)
    def _():
        gkc_ref[0] = k0[:, 0:HD]
        gkc_ref[1] = k1[:, 0:HD]
        gvc_ref[0] = v0[:, 0:HD]
        gvc_ref[1] = v1[:, 0:HD]
        for j in range(4):
            for src, dst in ((CD_NK, nkc_ref), (CD_NV, nvc_ref)):
                a, b = _dup_halves(_tile(p, src, j), lo)
                dst[2 * j] = a[:, 0:HD]
                dst[2 * j + 1] = b[:, 0:HD]


def _cd_in(x, mod, w_in, qg, kg, rope):
    hm4 = jax.ShapeDtypeStruct((4, T, LANES), BF16)
    hm2 = jax.ShapeDtypeStruct((2, T, LANES), BF16)
    c8 = jax.ShapeDtypeStruct((B_P, 1, 8, N_P, HD), F32)
    c2 = jax.ShapeDtypeStruct((B_P, 1, 2, N_P, HD), F32)
    return pl.pallas_call(
        _cd_in_kernel,
        grid=(NT,),
        in_specs=[_row_spec(D), _mod_spec(1, 0), _mod_spec(1, 1), _full((D, 2304)),
                  _full((1, LANES)), _full((1, LANES)), _rope_spec(), _rope_spec(), _rope_spec()],
        out_specs=[_hm_spec(4), _hm_spec(4), _hm_spec(4), _hm_spec(4), _hm_spec(2), _hm_spec(2),
                   _cache_spec(8, HD), _cache_spec(8, HD), _cache_spec(2, HD), _cache_spec(2, HD)],
        out_shape=[hm4, hm4, hm4, hm4, hm2, hm2, c8, c8, c2, c2],
        compiler_params=_params("arbitrary"),
        name="cd_in_proj",
    )(x, mod, mod, w_in, qg, kg, *rope)


def _stack_pairs(q_ref, n_q, tq):
    lo = _low_half(tq)
    parts = []
    for j in range(n_q):
        q = q_ref[j]
        zero = jnp.zeros_like(q)
        parts += [jnp.where(lo, q, zero), jnp.where(lo, zero, q)]
    return jnp.concatenate(parts, axis=0), lo


def _qk(qs, kb):
    return lax.dot_general(qs, kb, (((1,), (1,)), ((), ())), preferred_element_type=F32)


def _ctx_tile(ref, mode):
    if mode == "wide":
        x = ref[...]
    elif mode == "pair":
        x = jnp.concatenate([ref[0], ref[1]], axis=1)
    else:
        x = jnp.concatenate([ref[...], ref[...]], axis=1)
    return x.astype(BF16)


def _with_ones(v):
    lane = lax.broadcasted_iota(jnp.int32, v.shape, 1)
    return jnp.concatenate([v, jnp.where(lane == 0, 1.0, 0.0).astype(v.dtype)], axis=1)


def _flash_pair_kernel(*refs, gs, ctx_mode, **kw):
    it = iter(refs)
    q_ref, k_ref, v_ref = next(it), next(it), next(it)
    ctx = [next(it), next(it)] if ctx_mode else []
    rest = list(it)
    o_ref = rest.pop()
    n_q = q_ref.shape[0] // gs
    for t in range(gs * n_q):
        gi = t // n_q
        _flash_group(q_ref.at[t:t + 1], k_ref.at[gi], v_ref.at[gi], *[r.at[gi] for r in ctx], *rest,
                     o_ref.at[:, t * LANES:(t + 1) * LANES], n_q=1, ctx_mode=ctx_mode, **kw)


def _flash_group(*refs, n_q, tq, nk, tk, ctx_mode, post, lam_init):
    it = iter(refs)
    q_ref, k_ref, v_ref = next(it), next(it), next(it)
    kc_ref, vc_ref = (next(it), next(it)) if ctx_mode else (None, None)
    lam_ref, g_ref = (next(it), next(it)) if post == "diff" else (None, None)
    o_ref = next(it)

    qs, lo = _stack_pairs(q_ref, n_q, tq)
    rows = 2 * n_q * tq

    def step(kb, vb, carry):
        m, acc = carry
        s = _qk(qs, kb)
        m_new = jnp.maximum(m, jnp.max(s, axis=1, keepdims=True))
        p = jnp.exp2((s - m_new).astype(BF16))
        return m_new, jnp.exp2(m - m_new) * acc + _dot(p, _with_ones(vb))

    carry = (jnp.full((rows, 1), NEG, F32), jnp.zeros((rows, 2 * LANES), F32))
    for c in range(nk // tk):
        carry = step(k_ref[c * tk:(c + 1) * tk, :], v_ref[c * tk:(c + 1) * tk, :], carry)
    if ctx_mode:
        carry = step(_ctx_tile(kc_ref, ctx_mode), _ctx_tile(vc_ref, ctx_mode), carry)
    _, acc = carry
    o = acc[:, 0:LANES] / acc[:, LANES:LANES + 1]

    if post == "diff":
        lp = lam_ref[...]
        lam = (jnp.exp(jnp.sum(lp[0:1] * lp[1:2], axis=1, keepdims=True))
               - jnp.exp(jnp.sum(lp[2:3] * lp[3:4], axis=1, keepdims=True)) + lam_init)
        a = o[0:tq] - lam * o[tq:2 * tq]
        ms = jnp.mean(a * a, axis=-1, keepdims=True)
        o_ref[...] = (a * lax.rsqrt(ms + RMS_EPS) * g_ref[...] * (1.0 - lam_init)).astype(o_ref.dtype)
    else:
        for j in range(n_q):
            o_ref[:, j * LANES:(j + 1) * LANES] = jnp.where(
                lo, o[2 * j * tq:(2 * j + 1) * tq], o[(2 * j + 1) * tq:(2 * j + 2) * tq]).astype(o_ref.dtype)


def _flash_pair(q, k, v, *, latent, out_cols, col0, post, ctx=None, ctx_mode=None, diff=None,
                lam_init=0.0, prev=None):
    groups = k.shape[0]
    n_q = q.shape[0] // groups
    if latent:
        tq, nk, tk, gs = TM, N_S, 1024, 2
        grid = (B_S, groups // gs, TILES_PER_GRID)
        qrow = lambda b, g, i: NT_P + b * TILES_PER_GRID + i
        krow = lambda b, g, i: T_P // N_S + b
    else:
        tq, nk, tk, gs = N_P, N_P, N_P, groups
        grid = (B_P, 1, 1)
        qrow = lambda b, g, i: b
        krow = lambda b, g, i: b
    in_specs = [pl.BlockSpec((gs * n_q, tq, LANES), lambda b, g, i: (g, qrow(b, g, i), 0)),
                pl.BlockSpec((gs, nk, LANES), lambda b, g, i: (g, krow(b, g, i), 0)),
                pl.BlockSpec((gs, nk, LANES), lambda b, g, i: (g, krow(b, g, i), 0))]
    args = [q, k, v]
    if ctx is not None:
        width = LANES if ctx_mode == "wide" else HD
        spec = pl.BlockSpec((None, None, gs, PAST, width), lambda b, g, i: (b, 0, g, 0, 0))
        in_specs += [spec, spec]
        args += list(ctx)
    if diff is not None:
        in_specs += [pl.BlockSpec((4, HD), lambda b, g, i: (0, 0)),
                     pl.BlockSpec((1, LANES), lambda b, g, i: (0, 0))]
        args += list(diff)
    aliases = {}
    if prev is not None:
        aliases = {len(args): 0}
        in_specs.append(pl.BlockSpec(memory_space=pl.ANY))
        args.append(prev)

    def kernel(*refs):
        if prev is not None:
            refs = refs[:-2] + refs[-1:]
        _flash_pair_kernel(*refs, gs=gs, tq=tq, nk=nk, tk=tk, ctx_mode=ctx_mode if ctx is not None else None,
                           post=post, lam_init=lam_init)

    return pl.pallas_call(
        kernel,
        grid=grid,
        in_specs=in_specs,
        out_specs=pl.BlockSpec((tq, gs * n_q * LANES), lambda b, g, i: (qrow(b, g, i), col0 // gs + g)),
        out_shape=jax.ShapeDtypeStruct((T, out_cols), BF16),
        input_output_aliases=aliases,
        compiler_params=_params("arbitrary", "arbitrary", "arbitrary"),
        name="attn_" + post + ("_latent" if latent else "_context") + str(n_q),
    )(*args)


NA_TQ = NA_QROWS * GRID_W
NA_TK = NA_KROWS * GRID_W


NA_BLOCK_POS = ((0, 0), (NA_QROWS, 0), (GRID_H - NA_QROWS, GRID_H - NA_KROWS))
N_DR = 2 * NA_WIN_R - 1
N_DC = 2 * NA_WIN_C - 1


def _na_bias_kernel(rpb_ref, o_ref):
    qc = lax.broadcasted_iota(jnp.int32, (GRID_W, LANES), 0)
    lane = lax.broadcasted_iota(jnp.int32, (GRID_W, LANES), 1)
    kc = lane % GRID_W
    cs = jnp.clip(qc - NA_WIN_C // 2, 0, GRID_W - NA_WIN_C)
    col_ok = jnp.logical_and(kc >= cs, kc < cs + NA_WIN_C)
    lo = lane < GRID_W
    neg = jnp.full((GRID_W, LANES), NEG, F32)
    for head in range(2):
        toeplitz = []
        for dr in range(N_DR):
            r = jnp.broadcast_to(rpb_ref[head, dr:dr + 1, :] * LOG2E, (GRID_W, LANES))
            t = jnp.where(lo, pltpu.roll(r, LANES - (NA_WIN_C - 1), axis=1, stride=1, stride_axis=0),
                          pltpu.roll(r, GRID_W - (NA_WIN_C - 1), axis=1, stride=1, stride_axis=0))
            toeplitz.append(jnp.where(col_ok, t, neg))
        for pos, (r0, k0) in enumerate(NA_BLOCK_POS):
            for i in range(NA_QROWS):
                qr = r0 + i
                rs = min(max(qr - NA_WIN_R // 2, 0), GRID_H - NA_WIN_R)
                tiles = [toeplitz[k0 + j - qr + NA_WIN_R - 1] if rs <= k0 + j < rs + NA_WIN_R else neg
                         for j in range(NA_KROWS)]
                for jp in range(NA_KROWS // 2):
                    o_ref[pos, pl.ds(head * NA_TQ + i * GRID_W, GRID_W), pl.ds(jp * LANES, LANES)] = jnp.where(
                        lo, tiles[2 * jp], tiles[2 * jp + 1])


def _na_bias(rpb):
    rpb_pad = jnp.pad(rpb, ((0, 0), (0, 16 - N_DR), (0, LANES - N_DC)), constant_values=NEG)
    return pl.pallas_call(
        _na_bias_kernel,
        grid=(4,),
        in_specs=[pl.BlockSpec((2, 16, LANES), lambda g: (g, 0, 0))],
        out_specs=pl.BlockSpec((3, None, 2 * NA_TQ, NA_TK), lambda g: (0, g, 0, 0)),
        out_shape=jax.ShapeDtypeStruct((3, 4, 2 * NA_TQ, NA_TK), F32),
        compiler_params=_params("arbitrary"),
        name="na_bias_table",
    )(rpb_pad)


NA_PAIRS = 4


def _na_kernel(q_ref, k_ref, v_ref, kc_ref, vc_ref, bm_ref, _, o_ref):
    i = pl.program_id(2)
    k0 = jnp.clip(i * NA_QROWS - NA_WIN_R // 2, 0, GRID_H - NA_KROWS)
    start = pl.multiple_of(k0 * GRID_W, GRID_W)
    for g in range(NA_PAIRS):
        qs, lo = _stack_pairs(q_ref.at[g:g + 1], 1, NA_TQ)
        kw = k_ref[g, pl.ds(start, NA_TK), :]
        vw = v_ref[g, pl.ds(start, NA_TK), :]
        s_w = _qk(qs, kw) + bm_ref[g]
        s_c = _qk(qs, _ctx_tile(kc_ref.at[2 * g:2 * g + 2], "pair"))
        m = jnp.maximum(jnp.max(s_w, axis=1, keepdims=True), jnp.max(s_c, axis=1, keepdims=True))
        p_w = jnp.exp2((s_w - m).astype(BF16))
        p_c = jnp.exp2((s_c - m).astype(BF16))
        acc = _dot(p_w, _with_ones(vw)) + _dot(p_c, _with_ones(_ctx_tile(vc_ref.at[2 * g:2 * g + 2], "pair")))
        o = acc[:, 0:LANES] / acc[:, LANES:LANES + 1]
        o_ref[:, g * LANES:(g + 1) * LANES] = jnp.where(lo, o[0:NA_TQ], o[NA_TQ:2 * NA_TQ]).astype(o_ref.dtype)


def _na_latent(q, k, v, kc, vc, bm, prev):
    nblk = N_S // NA_TQ
    qrow = lambda b, g, i: T_P // NA_TQ + b * nblk + i
    krow = lambda b, g, i: T_P // N_S + b
    cfg = lambda i: jnp.where(i == 0, 0, jnp.where(i == nblk - 1, 2, 1))
    ctx_spec = pl.BlockSpec((None, None, 2 * NA_PAIRS, PAST, HD), lambda b, g, i: (b, 0, g, 0, 0))
    return pl.pallas_call(
        _na_kernel,
        grid=(B_S, 4 // NA_PAIRS, nblk),
        in_specs=[pl.BlockSpec((NA_PAIRS, NA_TQ, LANES), lambda b, g, i: (g, qrow(b, g, i), 0)),
                  pl.BlockSpec((NA_PAIRS, N_S, LANES), lambda b, g, i: (g, krow(b, g, i), 0)),
                  pl.BlockSpec((NA_PAIRS, N_S, LANES), lambda b, g, i: (g, krow(b, g, i), 0)),
                  ctx_spec, ctx_spec,
                  pl.BlockSpec((None, NA_PAIRS, 2 * NA_TQ, NA_TK), lambda b, g, i: (cfg(i), g, 0, 0)),
                  pl.BlockSpec(memory_space=pl.ANY)],
        out_specs=pl.BlockSpec((NA_TQ, NA_PAIRS * LANES), lambda b, g, i: (qrow(b, g, i), g)),
        out_shape=jax.ShapeDtypeStruct((T, D), BF16),
        input_output_aliases={6: 0},
        compiler_params=_params("arbitrary", "arbitrary", "arbitrary"),
        name="attn_window_latent",
    )(q, k, v, kc, vc, bm, prev)


def _top4(logits):
    rows = logits.shape[0]
    rest = logits.T[0:N_EXPERTS]
    expert = lax.broadcasted_iota(jnp.int32, rest.shape, 0).astype(F32)
    tops, firsts = [], []
    for _ in range(TOP_K):
        m = jnp.max(rest, axis=0, keepdims=True)
        first = jnp.min(jnp.where(rest == m, expert, float(N_EXPERTS)), axis=0, keepdims=True)
        tops.append(m)
        firsts.append(first)
        rest = jnp.where(expert == first, -jnp.inf, rest)
    es = [jnp.exp(m - tops[0]) for m in tops]
    denom = es[0] + es[1] + es[2] + es[3]
    r = lax.broadcasted_iota(jnp.int32, (2 * TOP_K, rows), 0)
    packed = jnp.zeros((2 * TOP_K, rows), F32)
    for k in range(TOP_K):
        packed = jnp.where(r == k, firsts[k], packed)
        packed = jnp.where(r == TOP_K + k, es[k] / denom, packed)
    back = jnp.concatenate([packed, jnp.zeros((LANES - 2 * TOP_K, rows), F32)], axis=0).T
    lane = lax.broadcasted_iota(jnp.int32, back.shape, 1)
    w = jnp.where(lane < TOP_K, pltpu.roll(back, LANES - TOP_K, axis=1), 0.0)
    return [back[:, k:k + 1] for k in range(TOP_K)], w


def _local_sort(firsts):
    lane = lax.broadcasted_iota(jnp.int32, (TM, LANES), 1).astype(F32)
    hots = [lane == f for f in firsts]
    sel = jnp.zeros((TM, LANES), F32)
    for hot in hots:
        sel = jnp.where(hot, 1.0, sel)
    r = lax.broadcasted_iota(jnp.int32, (TM, TM), 0)
    c = lax.broadcasted_iota(jnp.int32, (TM, TM), 1)
    earlier = _dot(jnp.where(c < r, 1.0, 0.0).astype(BF16), sel.astype(BF16))
    cnt = jnp.sum(sel, axis=0, keepdims=True)
    r = lax.broadcasted_iota(jnp.int32, (LANES, LANES), 0)
    c = lax.broadcasted_iota(jnp.int32, (LANES, LANES), 1)
    start = _dot(jnp.broadcast_to(cnt, (SUB, LANES)).astype(BF16), jnp.where(r < c, 1.0, 0.0).astype(BF16))[0:1]
    place = start + earlier
    pos = jnp.zeros((TM, LANES), F32)
    for k, hot in enumerate(hots):
        pos = jnp.where(lane == float(k), jnp.sum(jnp.where(hot, place, 0.0), axis=1, keepdims=True), pos)
    return pos.astype(jnp.int32), cnt.astype(jnp.int32)


def _mixer_tail(x_of, h_of, ci, g1_ref, sh2_ref, sc2_ref, lng_ref, lnb_ref, rw_ref, rb_ref,
                x1_ref, u2_ref, pos_ref, w_ref, cnt_ref):
    rows = slice(0, TM)
    x1 = _layer_norm(DEEPNORM_ALPHA * x_of(rows) + g1_ref[pl.ds(ci, 1), :] * h_of(rows), lng_ref[...], lnb_ref[...])
    x1_ref[...] = x1
    u2 = x1 * (1.0 + sc2_ref[pl.ds(ci, 1), :]) + sh2_ref[pl.ds(ci, 1), :]
    u2_ref[...] = u2.astype(BF16)
    firsts, w_ref[...] = _top4(_dot3(u2, rw_ref[...]) + rb_ref[...])
    pos, cnt = _local_sort(firsts)
    pos_ref[...] = pos
    cnt_ref[...] = jnp.broadcast_to(cnt, (SUB, LANES))


def _ab_out_kernel(xp_ref, xs_ref, gb_ref, y_ref, yp_ref, yn_ref, cv_ref, at_ref, wo_ref,
                   g1_ref, sh2_ref, sc2_ref, lng_ref, lnb_ref, rw_ref, rb_ref, *out_refs):
    i = pl.program_id(0)
    is_p = i < NT_P
    ci = _cond_row(i)
    j = (i - NT_P) % TILES_PER_GRID
    first = jnp.logical_or(is_p, j == 0)
    last = jnp.logical_or(is_p, j == TILES_PER_GRID - 1)
    y = y_ref[...]
    row = lax.broadcasted_iota(jnp.int32, y.shape, 0)
    before = jnp.where(first, 0.0, yp_ref[7:8, :])
    after = jnp.where(last, 0.0, yn_ref[0:1, :])
    y_prev = jnp.where(row == 0, before, pltpu.roll(y, 1, axis=0))
    y_next = jnp.where(row == TM - 1, after, pltpu.roll(y, TM - 1, axis=0))
    cv = cv_ref[...]
    conv = (gb_ref[...] * (y_prev * cv[0:1] + y * cv[1:2] + y_next * cv[2:3])).astype(BF16)
    _mixer_tail(lambda r: jnp.where(is_p, xp_ref[r, :], xs_ref[r, :]),
                lambda r: _dot(conv[r], wo_ref[0:512, :]) + _dot(at_ref[r, :], wo_ref[512:1024, :]),
                ci, g1_ref, sh2_ref, sc2_ref, lng_ref, lnb_ref, rw_ref, rb_ref, *out_refs)


def _cd_out_kernel(x_ref, mg_ref, wo_ref, g1_ref, sh2_ref, sc2_ref, lng_ref, lnb_ref, rw_ref, rb_ref,
                   *out_refs):
    ci = _cond_row(pl.program_id(0))
    _mixer_tail(lambda r: x_ref[r, :], lambda r: _dot(mg_ref[r, :], wo_ref[...]),
                ci, g1_ref, sh2_ref, sc2_ref, lng_ref, lnb_ref, rw_ref, rb_ref, *out_refs)


def _tail_specs(layer):
    return [_mod_spec(layer, 2), _mod_spec(layer, 3), _mod_spec(layer, 4),
            _full((1, D)), _full((1, D)), _full((D, LANES)), _full((1, LANES))]


_TAIL_OUT_SPECS = [_row_spec(D), _row_spec(D), _row_spec(LANES), _row_spec(LANES),
                   pl.BlockSpec((None, SUB, LANES), lambda i: (i, 0, 0))]
_TAIL_OUT_SHAPES = [jax.ShapeDtypeStruct((T, D), F32), jax.ShapeDtypeStruct((T, D), BF16),
                    jax.ShapeDtypeStruct((T, LANES), jnp.int32), jax.ShapeDtypeStruct((T, LANES), F32),
                    jax.ShapeDtypeStruct((NT, SUB, LANES), jnp.int32)]


def _ab_out(xp, xs, gb, y, conv_w, attn, w_out, mod, tail):
    halo_prev = pl.BlockSpec((8, 512), lambda i: (jnp.maximum(i * (TM // 8) - 1, 0), 0))
    halo_next = pl.BlockSpec((8, 512), lambda i: (jnp.minimum((i + 1) * (TM // 8), T // 8 - 1), 0))
    return pl.pallas_call(
        _ab_out_kernel,
        grid=(NT,),
        in_specs=_x_specs() + [_row_spec(512), _row_spec(512), halo_prev, halo_next, _full((3, 512)),
                               _row_spec(512), _full((D, D))] + _tail_specs(0),
        out_specs=_TAIL_OUT_SPECS,
        out_shape=_TAIL_OUT_SHAPES,
        compiler_params=_params("arbitrary"),
        name="ab_out_proj",
    )(xp, xs, gb, y, y, y, conv_w, attn, w_out, mod, mod, mod, *tail)


def _cd_out(x, merged, w_out, mod, tail):
    return pl.pallas_call(
        _cd_out_kernel,
        grid=(NT,),
        in_specs=[_row_spec(D), _row_spec(D), _full((D, D))] + _tail_specs(1),
        out_specs=_TAIL_OUT_SPECS,
        out_shape=_TAIL_OUT_SHAPES,
        compiler_params=_params("arbitrary"),
        name="cd_out_proj",
    )(x, merged, w_out, mod, mod, mod, *tail)


def _mix_down(wd_ref, wmix_ref):
    half = D_FF // 2
    for c in range(D // LANES):
        wmix_ref[c, pl.ds(0, half, stride=2), :] = wd_ref[0:half, c * LANES:(c + 1) * LANES]
        wmix_ref[c, pl.ds(1, half, stride=2), :] = wd_ref[half:D_FF, c * LANES:(c + 1) * LANES]


def _prep_expert(wgu_ref, wd_ref, wgu_bf, wmix_ref, wd_bf):
    for c in range(4):
        wgu_bf[:, c * 512:(c + 1) * 512] = wgu_ref[:, c * 512:(c + 1) * 512].astype(BF16)
    _mix_down(wd_ref, wmix_ref)
    for c in range(D // LANES):
        wd_bf[:, c * LANES:(c + 1) * LANES] = wmix_ref[c].astype(BF16)


def _expert_ffn(u, wgu_bf, bgu_ref, wd_bf, bd_ref):
    rows = u.shape[0]
    ga = _dot(u, wgu_bf[:, 0:D_FF]) + bgu_ref[:, 0:D_FF]
    gb = _dot(u, wgu_bf[:, D_FF:2 * D_FF]) + bgu_ref[:, D_FF:2 * D_FF]
    even = (lax.broadcasted_iota(jnp.int32, (rows, LANES), 1) % 2) == 0
    hid = []
    for c in range(D_FF // LANES):
        a = ga[:, c * LANES:(c + 1) * LANES]
        b = gb[:, c * LANES:(c + 1) * LANES]
        gate = jnp.where(even, a, pltpu.roll(b, 1, axis=1))
        up = jnp.where(even, pltpu.roll(a, LANES - 1, axis=1), b)
        gate = jnp.minimum(gate, SWIGLU_LIMIT)
        up = jnp.clip(up, -SWIGLU_LIMIT, SWIGLU_LIMIT)
        hid.append(((up + 1.0) * gate * jax.nn.sigmoid(SWIGLU_ALPHA * gate)).astype(BF16))
    hid = jnp.concatenate(hid, axis=1)
    return _dot(hid, wd_bf[...]) + bd_ref[...]


def _moe_plan(cnt):
    c = cnt[:, 0, :N_EXPERTS]
    counts = jnp.sum(c, axis=0)
    tiles_e = (counts + MOE_TM - 1) // MOE_TM
    tile_end = jnp.cumsum(tiles_e)
    first_row = (tile_end - tiles_e) * MOE_TM
    run_start = first_row[None] + jnp.cumsum(c, axis=0) - c
    j = jnp.arange(MOE_TILES, dtype=jnp.int32)
    tile_expert = jnp.minimum(jnp.sum((j[:, None] >= tile_end[None]).astype(jnp.int32), axis=1), N_EXPERTS - 1)
    i32 = lambda a: a.astype(jnp.int32)
    experts = jnp.arange(N_EXPERTS, dtype=jnp.int32)
    owns = tiles_e > 0
    seg = jnp.cumsum(owns.astype(jnp.int32)) - owns.astype(jnp.int32)
    later = jnp.where((experts[None] > experts[:, None]) & owns[None], experts[None], N_EXPERTS)
    nxt = jnp.min(later, axis=1)
    nxt = jnp.where(nxt == N_EXPERTS, -1, nxt)
    per_tile = lambda v: jnp.sum(jnp.where(tile_expert[:, None] == experts[None], v[None], 0), axis=1)
    return dict(tile_expert=i32(tile_expert), n_used=i32(tile_end[-1:]),
                seg_parity=i32(per_tile(seg % 2)), next_expert=i32(per_tile(nxt)),
                run_start=i32(run_start)[:, None, :], run_len=i32(c)[:, None, :],
                pad_start=i32(first_row + counts)[None, None, :], pad_len=i32(tiles_e * MOE_TM - counts)[None, None, :])


def _for_each_run(start_ref, len_ref, copy):
    def body(e, local):
        n = len_ref[0, e]
        g = start_ref[0, e]

        def pieces(sizes):
            for size in sizes:
                covered = jnp.bitwise_and(n, -2 * size)

                @pl.when(jnp.bitwise_and(n, size) != 0)
                def _(covered=covered, size=size):
                    copy(local + covered, g + covered, size)

        @pl.when(n >= RUN_SIZES[2])
        def _():
            pieces(RUN_SIZES[:3])

        pieces(RUN_SIZES[3:])
        return local + n
    lax.fori_loop(0, N_EXPERTS, body, jnp.int32(0))


def _rows(ref, start, size):
    start = 0 if isinstance(start, int) and start == 0 else pl.multiple_of(start * SUB, SUB)
    return ref.at[pl.ds(start, size * SUB), :]


def _from_tiles(ref, rows):
    return jnp.concatenate([ref[pl.ds(c, rows, stride=SUB), :] for c in range(SUB)], axis=1)


def _to_tiles(ref, x):
    for c in range(SUB):
        ref[pl.ds(c, x.shape[0], stride=SUB), :] = x[:, c * LANES:(c + 1) * LANES]


def _run_spec(index):
    return pl.BlockSpec((None, 1, N_EXPERTS), index, memory_space=pltpu.SMEM)


def _dispatch_kernel(rs_ref, rl_ref, ps_ref, pn_ref, u_ref, pos_ref, xs_hbm, stage, zeros, sem, zsem):
    i = pl.program_id(0)
    slot = i % 2

    def wait(s):
        pltpu.make_async_copy(stage.at[s], stage.at[s], sem.at[s]).wait()

    @pl.when(i == 0)
    def _():
        zeros[...] = jnp.zeros_like(zeros)

        def pad(local, g, size):
            return pltpu.make_async_copy(_rows(zeros, 0, size), _rows(xs_hbm, g, size), zsem.at[0])
        _for_each_run(ps_ref, pn_ref, lambda *a: pad(*a).start())
        _for_each_run(ps_ref, pn_ref, lambda *a: pad(*a).wait())

    @pl.when(i >= 2)
    def _():
        wait(slot)

    place = pos_ref[...].astype(F32).T
    s = lax.broadcasted_iota(jnp.int32, (PAIRS_TM, TM), 0).astype(F32)
    onehot = jnp.zeros((PAIRS_TM, TM), F32)
    for k in range(TOP_K):
        onehot = jnp.where(s == place[k:k + 1, :], 1.0, onehot)
    _to_tiles(stage.at[slot], _dot(onehot.astype(BF16), u_ref[...]))
    _for_each_run(rs_ref, rl_ref, lambda l, g, size: pltpu.make_async_copy(
        _rows(stage.at[slot], l, size), _rows(xs_hbm, g, size), sem.at[slot]).start())

    @pl.when(i == NT - 1)
    def _():
        wait(1 - slot)
        wait(slot)


def _dispatch(u2, pos, plan):
    return pl.pallas_call(
        _dispatch_kernel,
        grid=(NT,),
        in_specs=[_run_spec(lambda i: (i, 0, 0)), _run_spec(lambda i: (i, 0, 0)),
                  _run_spec(lambda i: (0, 0, 0)), _run_spec(lambda i: (0, 0, 0)),
                  _row_spec(D), _row_spec(LANES)],
        out_specs=pl.BlockSpec(memory_space=pl.ANY),
        out_shape=jax.ShapeDtypeStruct((MOE_TILES * MOE_TM * SUB, LANES), F32),
        scratch_shapes=[pltpu.VMEM((2, PAIRS_TM * SUB, LANES), F32), pltpu.VMEM((TM * SUB, LANES), F32),
                        pltpu.SemaphoreType.DMA((2,)), pltpu.SemaphoreType.DMA((1,))],
        compiler_params=_params("arbitrary"),
        name="moe_dispatch",
    )(plan["run_start"], plan["run_len"], plan["pad_start"], plan["pad_len"], u2, pos)


def _moe_kernel(te_ref, nu_ref, par_ref, nxt_ref, x_ref, bgu_ref, bd_ref, wgu_hbm, wd_hbm, y_ref,
                wgu_f, wd_f, wgu_bf, wmix_ref, wd_bf, sem, *, layer):
    j = pl.program_id(0)
    used = j < nu_ref[0]
    buf = par_ref[j]

    def copies(e, s):
        return (pltpu.make_async_copy(wgu_hbm.at[layer, e], wgu_f.at[s], sem.at[s]),
                pltpu.make_async_copy(wd_hbm.at[layer, e], wd_f.at[s], sem.at[s]))

    @pl.when(jnp.logical_and(used, j == 0))
    def _():
        for cp in copies(te_ref[0], 0):
            cp.start(priority=1)

    @pl.when(jnp.logical_and(used, jnp.logical_or(j == 0, te_ref[j] != te_ref[jnp.maximum(j - 1, 0)])))
    def _():
        for cp in copies(te_ref[j], buf):
            cp.wait()

        @pl.when(nxt_ref[j] >= 0)
        def _():
            for cp in copies(nxt_ref[j], 1 - buf):
                cp.start(priority=1)

        _prep_expert(wgu_f.at[buf], wd_f.at[buf], wgu_bf, wmix_ref, wd_bf)

    @pl.when(used)
    def _():
        x = _from_tiles(x_ref, MOE_TM).astype(BF16)
        _to_tiles(y_ref, _expert_ffn(x, wgu_bf, bgu_ref, wd_bf, bd_ref))


def _moe(layer, xs, plan, w_gate_up, b_gate_up, w_down, b_down):
    rows = pl.BlockSpec((MOE_TM * SUB, LANES), lambda j, te, nu, par, nxt: (jnp.minimum(j, nu[0] - 1), 0))
    grid_spec = pltpu.PrefetchScalarGridSpec(
        num_scalar_prefetch=4,
        grid=(MOE_TILES,),
        in_specs=[rows,
                  pl.BlockSpec((None, None, 1, 2 * D_FF), lambda j, te, nu, par, nxt: (layer, te[j], 0, 0)),
                  pl.BlockSpec((None, None, 1, D), lambda j, te, nu, par, nxt: (layer, te[j], 0, 0)),
                  pl.BlockSpec(memory_space=pl.ANY), pl.BlockSpec(memory_space=pl.ANY)],
        out_specs=rows,
        scratch_shapes=[pltpu.VMEM((2, D, 2 * D_FF), F32), pltpu.VMEM((2, D_FF, D), F32),
                        pltpu.VMEM((D, 2 * D_FF), BF16), pltpu.VMEM((D // LANES, D_FF, LANES), F32),
                        pltpu.VMEM((D_FF, D), BF16), pltpu.SemaphoreType.DMA((2,))])

    def kernel(*refs):
        _moe_kernel(*refs, layer=layer)

    return pl.pallas_call(
        kernel,
        grid_spec=grid_spec,
        out_shape=jax.ShapeDtypeStruct((MOE_TILES * MOE_TM * SUB, LANES), F32),
        compiler_params=_params("arbitrary"),
        name="moe_experts",
    )(plan["tile_expert"], plan["n_used"], plan["seg_parity"], plan["next_expert"], xs,
      b_gate_up.reshape(DEPTH, N_EXPERTS, 1, 2 * D_FF), b_down.reshape(DEPTH, N_EXPERTS, 1, D), w_gate_up, w_down)


def _combine_kernel(rs_ref, rl_ref, rsn_ref, rln_ref, x1_ref, pos_ref, w_ref, g2_ref, lng_ref, lnb_ref, ys_hbm,
                    *rest, final):
    out_refs, (stage, sem) = rest[:-2], rest[-2:]
    i = pl.program_id(0)
    slot = i % 2
    other = 1 - slot

    def fetch(start_ref, len_ref, s):
        _for_each_run(start_ref, len_ref, lambda l, g, size: pltpu.make_async_copy(
            _rows(ys_hbm, g, size), _rows(stage.at[s], l, size), sem.at[s]).start())

    def wait(s):
        pltpu.make_async_copy(stage.at[s], stage.at[s], sem.at[s]).wait()

    @pl.when(i == 0)
    def _():
        fetch(rs_ref, rl_ref, 0)

    fetch(rsn_ref, rln_ref, other)
    wait(slot)
    y = _from_tiles(stage.at[slot], PAIRS_TM).astype(BF16)
    place = pos_ref[...].astype(F32)
    w = w_ref[...]
    s = lax.broadcasted_iota(jnp.int32, (TM, PAIRS_TM), 1).astype(F32)
    pick = jnp.zeros((TM, PAIRS_TM), F32)
    for k in range(TOP_K):
        pick = jnp.where(s == place[:, k:k + 1], w[:, k:k + 1], pick)
    f = _dot(pick.astype(BF16), y)
    ci = _cond_row(i)
    z = DEEPNORM_ALPHA * x1_ref[...] + g2_ref[pl.ds(ci, 1), :] * f
    out = _layer_norm(z, lng_ref[...], lnb_ref[...])
    if final:
        @pl.when(i < NT_P)
        def _():
            out_refs[0][...] = out

        @pl.when(i >= NT_P)
        def _():
            out_refs[1][...] = out
    else:
        out_refs[0][...] = out

    @pl.when(i == NT - 1)
    def _():
        wait(other)


def _combine(layer, x1, ys, pos, w, plan, mod, g, b):
    final = layer == DEPTH - 1
    nxt = lambda i: (jnp.minimum(i + 1, NT - 1), 0, 0)
    cur = lambda i: (i, 0, 0)
    if final:
        out_specs = _x_specs()
        out_shape = [jax.ShapeDtypeStruct((T_P, D), F32), jax.ShapeDtypeStruct((T_S, D), F32)]
    else:
        out_specs = _row_spec(D)
        out_shape = jax.ShapeDtypeStruct((T, D), F32)

    def kernel(*refs):
        _combine_kernel(*refs, final=final)

    return pl.pallas_call(
        kernel,
        grid=(NT,),
        in_specs=[_run_spec(cur), _run_spec(cur), _run_spec(nxt), _run_spec(nxt),
                  _row_spec(D), _row_spec(LANES), _row_spec(LANES), _mod_spec(layer, 5), _full((1, D)), _full((1, D)),
                  pl.BlockSpec(memory_space=pl.ANY)],
        out_specs=out_specs,
        out_shape=out_shape,
        scratch_shapes=[pltpu.VMEM((2, PAIRS_TM * SUB, LANES), F32), pltpu.SemaphoreType.DMA((2,))],
        compiler_params=_params("arbitrary"),
        name="moe_combine_norm" + ("_final" if final else ""),
    )(plan["run_start"], plan["run_len"], plan["run_start"], plan["run_len"], x1, pos, w, mod, g, b, ys)


def _moe_layer(layer, routed, experts, mod, ln2_g, ln2_b):
    x1, u2, pos, w, cnt = routed
    plan = _moe_plan(cnt)
    ys = _moe(layer, _dispatch(u2, pos, plan), plan, *experts)
    return _combine(layer, x1, ys, pos, w, plan, mod, ln2_g[layer][None], ln2_b[layer][None])


def _router_tail(l, ln1_g, ln1_b, router_w, router_b):
    rw = jnp.pad(router_w[l], ((0, 0), (0, LANES - N_EXPERTS)))
    rb = jnp.pad(router_b[l], (0, LANES - N_EXPERTS), constant_values=NEG)
    return ln1_g[l][None], ln1_b[l][None], rw, rb[None]


def kernel(x_prompt, x_sample, c, c_ctx, cache_diff_k, cache_diff_v, cache_na_k, cache_na_v, cache_gqa_k, cache_gqa_v, w_mod, b_mod, ln1_g, ln1_b, ln2_g, ln2_b, ab_w_in, ab_conv_w, ab_lambda_q1, ab_lambda_k1, ab_lambda_q2, ab_lambda_k2, ab_subln_g, ab_w_out, cd_w_in, cd_na_rpb, cd_q_norm_g, cd_k_norm_g, cd_w_out, router_w, router_b, w_gate_up, b_gate_up, w_down, b_down):
    xp = x_prompt.reshape(T_P, D)
    xs = x_sample.reshape(T_S, D)
    cond8 = jnp.concatenate([c_ctx[None], c, jnp.zeros((8 - 1 - B_S, D), F32)], axis=0)
    mod = _modulation(cond8, w_mod, b_mod)
    rope = _rope_tables()
    experts = (w_gate_up, b_gate_up, w_down, b_down)

    gb, y, q, k, v, new_diff_k, new_diff_v = _ab_in(xp, xs, mod, ab_w_in[0].astype(BF16), rope)
    lam_init = 0.8 - 0.6 * 1.0
    diff = (jnp.stack([ab_lambda_q1[0], ab_lambda_k1[0], ab_lambda_q2[0], ab_lambda_k2[0]]), ab_subln_g[0][None])
    attn = _flash_pair(q, k, v, latent=False, out_cols=512, col0=0, post="diff", diff=diff, lam_init=lam_init)
    attn = _flash_pair(q, k, v, latent=True, out_cols=512, col0=0, post="diff", diff=diff, lam_init=lam_init,
                       ctx=(cache_diff_k, cache_diff_v), ctx_mode="wide", prev=attn)
    routed = _ab_out(xp, xs, gb, y, ab_conv_w[0], attn, ab_w_out[0].astype(BF16), mod,
                     _router_tail(0, ln1_g, ln1_b, router_w, router_b))
    x = _moe_layer(0, routed, experts, mod, ln2_g, ln2_b)

    qg = jnp.tile(cd_q_norm_g[0], 2)[None]
    kg = jnp.tile(cd_k_norm_g[0], 2)[None]
    nq, nk, nv, gq, gk, gv, new_na_k, new_na_v, new_gqa_k, new_gqa_v = _cd_in(
        x, mod, cd_w_in[0].astype(BF16), qg, kg, rope)
    merged = _flash_pair(nq, nk, nv, latent=False, out_cols=D, col0=0, post="select")
    merged = _flash_pair(gq, gk, gv, latent=False, out_cols=D, col0=2, post="select", prev=merged)
    merged = _na_latent(nq, nk, nv, cache_na_k, cache_na_v, _na_bias(cd_na_rpb[0]), merged)
    merged = _flash_pair(gq, gk, gv, latent=True, out_cols=D, col0=2, post="select",
                         ctx=(cache_gqa_k, cache_gqa_v), ctx_mode="dup", prev=merged)
    routed = _cd_out(x, merged, cd_w_out[0].astype(BF16), mod,
                     _router_tail(1, ln1_g, ln1_b, router_w, router_b))
    y_p, y_s = _moe_layer(1, routed, experts, mod, ln2_g, ln2_b)

    return (y_p.reshape(B_P, N_P, D), y_s.reshape(B_S, N_S, D), new_diff_k, new_diff_v,
            new_na_k, new_na_v, new_gqa_k, new_gqa_v)
```

```python
import jax
import jax.numpy as jnp
import numpy as np
from jax import lax
from jax.experimental import pallas as pl
from jax.experimental.pallas import tpu as pltpu

F32 = jnp.float32
BF16 = jnp.bfloat16

D = 1024
B_P, N_P = 16, 256
B_S, N_S = 2, 4096
PAST = 256
T_P, T_S = B_P * N_P, B_S * N_S
T = T_P + T_S
TM = 256
NT_P, NT_S, NT = T_P // TM, T_S // TM, T // TM
TILES_PER_GRID = N_S // TM
GRID_W = 64
GRID_H = N_S // GRID_W
HD = 64
DEPTH = 2
N_EXPERTS = 32
TOP_K = 4
D_FF = 1024
NA_WIN_R, NA_WIN_C = 8, 16
NA_QROWS = 4
NA_KROWS = 12
SWIGLU_LIMIT = 7.0
SWIGLU_ALPHA = 1.702
ROPE_THETA = 10000.0
DEEPNORM_ALPHA = (2 * DEPTH) ** 0.25
LN_EPS = 1e-5
RMS_EPS = 1e-6
LOG2E = 1.4426950408889634
QK_SCALE = HD ** -0.5 * LOG2E
NEG = -1e30
MOE_TM = 256
MOE_PAIRS = TOP_K * T
MOE_TILES = MOE_PAIRS // MOE_TM + N_EXPERTS
PAIRS_TM = TOP_K * TM
RUN_SIZES = tuple(TM >> b for b in range(TM.bit_length()))
LANES = 128
SUB = 8

VMEM_LIMIT = 56 * 1024 * 1024


def _params(*sem):
    return pltpu.CompilerParams(dimension_semantics=sem, vmem_limit_bytes=VMEM_LIMIT)


def _split(x):
    hi = x.astype(BF16)
    lo = (x - hi.astype(F32)).astype(BF16)
    return hi, lo


def _dot(a, b):
    return jnp.dot(a, b, preferred_element_type=F32)


def _dot3(a, b):
    ah, al = _split(a)
    bh, bl = _split(b)
    return _dot(ah, bh) + (_dot(ah, bl) + _dot(al, bh))


def _cond_row(i):
    return jnp.where(i < NT_P, 0, 1 + (i - NT_P) // TILES_PER_GRID)


def _layer_norm(z, g, b):
    mu = jnp.mean(z, axis=-1, keepdims=True)
    zc = z - mu
    var = jnp.mean(zc * zc, axis=-1, keepdims=True)
    return zc * lax.rsqrt(var + LN_EPS) * g + b


def _low_half(rows):
    return lax.broadcasted_iota(jnp.int32, (rows, LANES), 1) < HD


def _mod_kernel(c_ref, w_ref, b_ref, o_ref):
    c = c_ref[...]
    o_ref[...] = _dot3(c * jax.nn.sigmoid(c), w_ref[...]) + b_ref[...]


def _modulation(cond8, w_mod, b_mod):
    return pl.pallas_call(
        _mod_kernel,
        grid=(DEPTH, 6),
        in_specs=[pl.BlockSpec((8, D), lambda l, j: (0, 0)),
                  pl.BlockSpec((None, D, D), lambda l, j: (l, 0, j)),
                  pl.BlockSpec((None, 1, D), lambda l, j: (l, 0, j))],
        out_specs=pl.BlockSpec((None, 8, D), lambda l, j: (l, 0, j)),
        out_shape=jax.ShapeDtypeStruct((DEPTH, 8, 6 * D), F32),
        compiler_params=_params("arbitrary", "arbitrary"),
        name="modulation",
    )(cond8, w_mod, b_mod.reshape(DEPTH, 1, 6 * D))


def _mod_spec(layer, chunk):
    return pl.BlockSpec((None, 8, D), lambda i, _l=layer, _c=chunk: (_l, 0, _c))


def _full(shape):
    return pl.BlockSpec(shape, lambda i: (0,) * len(shape))


def _rope_tables():
    t = np.arange(N_S)
    half = HD // 2
    inv = ROPE_THETA ** (-np.arange(0, half, 2, dtype=np.float64) / half)
    inv_lane = np.tile(np.repeat(inv, 2), 2 * LANES // HD)
    lane = np.arange(LANES)
    by_row = (lane % HD) < half
    ang = np.where(by_row[None], (t // GRID_W)[:, None], (t % GRID_W)[:, None]) * inv_lane[None]
    cos, sin = np.cos(ang), np.sin(ang)
    even = (lane % 2) == 0
    return tuple(jnp.asarray(a, F32) for a in (cos, np.where(even, -sin, 0.0), np.where(even, 0.0, sin)))


def _rope(x, a, b, c):
    return x * a + pltpu.roll(x, LANES - 1, axis=1) * b + pltpu.roll(x, 1, axis=1) * c


def _rope_or_identity(identity, ra_ref, rb_ref, rc_ref):
    return (jnp.where(identity, 1.0, ra_ref[...]), jnp.where(identity, 0.0, rb_ref[...]),
            jnp.where(identity, 0.0, rc_ref[...]))


def _rope_spec():
    return pl.BlockSpec((TM, LANES), lambda i: (jnp.maximum(i - NT_P, 0) % TILES_PER_GRID, 0))


def _x_specs():
    return [pl.BlockSpec((TM, D), lambda i: (jnp.minimum(i, NT_P - 1), 0)),
            pl.BlockSpec((TM, D), lambda i: (jnp.maximum(i - NT_P, 0), 0))]


def _cache_spec(heads, width):
    return pl.BlockSpec((None, None, heads, N_P, width), lambda i: (jnp.minimum(i, NT_P - 1), 0, 0, 0, 0))


def _row_spec(width):
    return pl.BlockSpec((TM, width), lambda i: (i, 0))


def _hm_spec(n):
    return pl.BlockSpec((n, TM, LANES), lambda i: (0, i, 0))


AB_Q0, AB_K0, AB_V0 = 1536, 2048, 2560


def _tile(p, col0, j):
    return p[:, col0 + j * LANES:col0 + (j + 1) * LANES]


def _ab_in_kernel(xp_ref, xs_ref, sh_ref, sc_ref, w_ref, ra_ref, rb_ref, rc_ref,
                  gb_ref, y_ref, q_ref, k_ref, v_ref, kc_ref, vc_ref):
    i = pl.program_id(0)
    is_p = i < NT_P
    ci = _cond_row(i)
    x = jnp.where(is_p, xp_ref[...], xs_ref[...])
    u = x * (1.0 + sc_ref[pl.ds(ci, 1), :]) + sh_ref[pl.ds(ci, 1), :]
    p = _dot(u.astype(BF16), w_ref[...])
    gb_ref[...] = p[:, 0:512]
    y_ref[...] = p[:, 512:1024] * p[:, 1024:1536]
    a, b, c = _rope_or_identity(is_p, ra_ref, rb_ref, rc_ref)
    for h in range(4):
        v_ref[h] = _tile(p, AB_V0, h).astype(BF16)
        q_ref[h] = (_rope(_tile(p, AB_Q0, h), a, b, c) * QK_SCALE).astype(BF16)
        k_ref[h] = _rope(_tile(p, AB_K0, h), a, b, c).astype(BF16)

    @pl.when(is_p)
    def _():
        for h in range(4):
            kc_ref[h] = _tile(p, AB_K0, h)
            vc_ref[h] = _tile(p, AB_V0, h)


def _ab_in(xp, xs, mod, w_in, rope):
    hm = jax.ShapeDtypeStruct((4, T, LANES), BF16)
    cache = jax.ShapeDtypeStruct((B_P, 1, 4, N_P, LANES), F32)
    half = jax.ShapeDtypeStruct((T, 512), F32)
    return pl.pallas_call(
        _ab_in_kernel,
        grid=(NT,),
        in_specs=_x_specs() + [_mod_spec(0, 0), _mod_spec(0, 1), _full((D, 3072)),
                               _rope_spec(), _rope_spec(), _rope_spec()],
        out_specs=[_row_spec(512), _row_spec(512), _hm_spec(4), _hm_spec(4), _hm_spec(4),
                   _cache_spec(4, LANES), _cache_spec(4, LANES)],
        out_shape=[half, half, hm, hm, hm, cache, cache],
        compiler_params=_params("arbitrary"),
        name="ab_in_proj",
    )(xp, xs, mod, mod, w_in, *rope)


CD_NQ, CD_NK, CD_NV, CD_GQ, CD_GK, CD_GV = 0, 512, 1024, 1536, 2048, 2176


def _seg_mean64(s):
    r = lax.broadcasted_iota(jnp.int32, (LANES, LANES), 0) // HD
    c = lax.broadcasted_iota(jnp.int32, (LANES, LANES), 1) // HD
    seg = jnp.where(r == c, 1.0, 0.0).astype(BF16)
    hi, lo = _split(s)
    return (_dot(hi, seg) + _dot(lo, seg)) * (1.0 / HD)


def _rms64(x, g):
    return x * lax.rsqrt(_seg_mean64(x * x) + RMS_EPS) * g


def _dup_halves(x, lo):
    r = pltpu.roll(x, HD, axis=1)
    return jnp.where(lo, x, r), jnp.where(lo, r, x)


def _cd_in_kernel(x_ref, sh_ref, sc_ref, w_ref, qg_ref, kg_ref, ra_ref, rb_ref, rc_ref,
                  nq_ref, nk_ref, nv_ref, gq_ref, gk_ref, gv_ref,
                  nkc_ref, nvc_ref, gkc_ref, gvc_ref):
    i = pl.program_id(0)
    is_p = i < NT_P
    ci = _cond_row(i)
    u = x_ref[...] * (1.0 + sc_ref[pl.ds(ci, 1), :]) + sh_ref[pl.ds(ci, 1), :]
    p = _dot(u.astype(BF16), w_ref[...])
    lo = _low_half(TM)
    for j in range(4):
        nq_ref[j] = (_tile(p, CD_NQ, j) * QK_SCALE).astype(BF16)
        nk_ref[j] = _tile(p, CD_NK, j).astype(BF16)
        nv_ref[j] = _tile(p, CD_NV, j).astype(BF16)
    gq = [_rms64(_tile(p, CD_GQ, j), qg_ref[...]) for j in range(4)]
    gk = _rms64(_tile(p, CD_GK, 0), kg_ref[...])
    gv = _tile(p, CD_GV, 0)
    v0, v1 = _dup_halves(gv, lo)
    gv_ref[0] = v0.astype(BF16)
    gv_ref[1] = v1.astype(BF16)
    a, b, c = _rope_or_identity(is_p, ra_ref, rb_ref, rc_ref)
    for j in range(4):
        gq_ref[j] = (_rope(gq[j], a, b, c) * QK_SCALE).astype(BF16)
    k0, k1 = _dup_halves(_rope(gk, a, b, c), lo)
    gk_ref[0] = k0.astype(BF16)
    gk_ref[1] = k1.astype(BF16)

    @pl.when(is_p)
    def _():
        gkc_ref[0] = k0[:, 0:HD]
        gkc_ref[1] = k1[:, 0:HD]
        gvc_ref[0] = v0[:, 0:HD]
        gvc_ref[1] = v1[:, 0:HD]
        for j in range(4):
            for src, dst in ((CD_NK, nkc_ref), (CD_NV, nvc_ref)):
                a, b = _dup_halves(_tile(p, src, j), lo)
                dst[2 * j] = a[:, 0:HD]
                dst[2 * j + 1] = b[:, 0:HD]


def _cd_in(x, mod, w_in, qg, kg, rope):
    hm4 = jax.ShapeDtypeStruct((4, T, LANES), BF16)
    hm2 = jax.ShapeDtypeStruct((2, T, LANES), BF16)
    c8 = jax.ShapeDtypeStruct((B_P, 1, 8, N_P, HD), F32)
    c2 = jax.ShapeDtypeStruct((B_P, 1, 2, N_P, HD), F32)
    return pl.pallas_call(
        _cd_in_kernel,
        grid=(NT,),
        in_specs=[_row_spec(D), _mod_spec(1, 0), _mod_spec(1, 1), _full((D, 2304)),
                  _full((1, LANES)), _full((1, LANES)), _rope_spec(), _rope_spec(), _rope_spec()],
        out_specs=[_hm_spec(4), _hm_spec(4), _hm_spec(4), _hm_spec(4), _hm_spec(2), _hm_spec(2),
                   _cache_spec(8, HD), _cache_spec(8, HD), _cache_spec(2, HD), _cache_spec(2, HD)],
        out_shape=[hm4, hm4, hm4, hm4, hm2, hm2, c8, c8, c2, c2],
        compiler_params=_params("arbitrary"),
        name="cd_in_proj",
    )(x, mod, mod, w_in, qg, kg, *rope)


def _stack_pairs(q_ref, n_q, tq):
    lo = _low_half(tq)
    parts = []
    for j in range(n_q):
        q = q_ref[j]
        zero = jnp.zeros_like(q)
        parts += [jnp.where(lo, q, zero), jnp.where(lo, zero, q)]
    return jnp.concatenate(parts, axis=0), lo


def _qk(qs, kb):
    return lax.dot_general(qs, kb, (((1,), (1,)), ((), ())), preferred_element_type=F32)


def _ctx_tile(ref, mode):
    if mode == "wide":
        x = ref[...]
    elif mode == "pair":
        x = jnp.concatenate([ref[0], ref[1]], axis=1)
    else:
        x = jnp.concatenate([ref[...], ref[...]], axis=1)
    return x.astype(BF16)


def _with_ones(v):
    lane = lax.broadcasted_iota(jnp.int32, v.shape, 1)
    return jnp.concatenate([v, jnp.where(lane == 0, 1.0, 0.0).astype(v.dtype)], axis=1)


def _flash_pair_kernel(*refs, gs, ctx_mode, **kw):
    it = iter(refs)
    q_ref, k_ref, v_ref = next(it), next(it), next(it)
    ctx = [next(it), next(it)] if ctx_mode else []
    rest = list(it)
    o_ref = rest.pop()
    n_q = q_ref.shape[0] // gs
    for t in range(gs * n_q):
        gi = t // n_q
        _flash_group(q_ref.at[t:t + 1], k_ref.at[gi], v_ref.at[gi], *[r.at[gi] for r in ctx], *rest,
                     o_ref.at[:, t * LANES:(t + 1) * LANES], n_q=1, ctx_mode=ctx_mode, **kw)


def _flash_group(*refs, n_q, tq, nk, tk, ctx_mode, post, lam_init):
    it = iter(refs)
    q_ref, k_ref, v_ref = next(it), next(it), next(it)
    kc_ref, vc_ref = (next(it), next(it)) if ctx_mode else (None, None)
    lam_ref, g_ref = (next(it), next(it)) if post == "diff" else (None, None)
    o_ref = next(it)

    qs, lo = _stack_pairs(q_ref, n_q, tq)
    rows = 2 * n_q * tq

    def step(kb, vb, carry):
        m, acc = carry
        s = _qk(qs, kb)
        m_new = jnp.maximum(m, jnp.max(s, axis=1, keepdims=True))
        p = jnp.exp2((s - m_new).astype(BF16))
        return m_new, jnp.exp2(m - m_new) * acc + _dot(p, _with_ones(vb))

    carry = (jnp.full((rows, 1), NEG, F32), jnp.zeros((rows, 2 * LANES), F32))
    for c in range(nk // tk):
        carry = step(k_ref[c * tk:(c + 1) * tk, :], v_ref[c * tk:(c + 1) * tk, :], carry)
    if ctx_mode:
        carry = step(_ctx_tile(kc_ref, ctx_mode), _ctx_tile(vc_ref, ctx_mode), carry)
    _, acc = carry
    o = acc[:, 0:LANES] / acc[:, LANES:LANES + 1]

    if post == "diff":
        lp = lam_ref[...]
        lam = (jnp.exp(jnp.sum(lp[0:1] * lp[1:2], axis=1, keepdims=True))
               - jnp.exp(jnp.sum(lp[2:3] * lp[3:4], axis=1, keepdims=True)) + lam_init)
        a = o[0:tq] - lam * o[tq:2 * tq]
        ms = jnp.mean(a * a, axis=-1, keepdims=True)
        o_ref[...] = (a * lax.rsqrt(ms + RMS_EPS) * g_ref[...] * (1.0 - lam_init)).astype(o_ref.dtype)
    else:
        for j in range(n_q):
            o_ref[:, j * LANES:(j + 1) * LANES] = jnp.where(
                lo, o[2 * j * tq:(2 * j + 1) * tq], o[(2 * j + 1) * tq:(2 * j + 2) * tq]).astype(o_ref.dtype)


def _flash_pair(q, k, v, *, latent, out_cols, col0, post, ctx=None, ctx_mode=None, diff=None,
                lam_init=0.0, prev=None):
    groups = k.shape[0]
    n_q = q.shape[0] // groups
    if latent:
        tq, nk, tk, gs = TM, N_S, 1024, 4 // n_q
        grid = (B_S, groups // gs, TILES_PER_GRID)
        qrow = lambda b, g, i: NT_P + b * TILES_PER_GRID + i
        krow = lambda b, g, i: T_P // N_S + b
    else:
        tq, nk, tk, gs = N_P, N_P, N_P, groups
        grid = (B_P, 1, 1)
        qrow = lambda b, g, i: b
        krow = lambda b, g, i: b
    in_specs = [pl.BlockSpec((gs * n_q, tq, LANES), lambda b, g, i: (g, qrow(b, g, i), 0)),
                pl.BlockSpec((gs, nk, LANES), lambda b, g, i: (g, krow(b, g, i), 0)),
                pl.BlockSpec((gs, nk, LANES), lambda b, g, i: (g, krow(b, g, i), 0))]
    args = [q, k, v]
    if ctx is not None:
        width = LANES if ctx_mode == "wide" else HD
        spec = pl.BlockSpec((None, None, gs, PAST, width), lambda b, g, i: (b, 0, g, 0, 0))
        in_specs += [spec, spec]
        args += list(ctx)
    if diff is not None:
        in_specs += [pl.BlockSpec((4, HD), lambda b, g, i: (0, 0)),
                     pl.BlockSpec((1, LANES), lambda b, g, i: (0, 0))]
        args += list(diff)
    aliases = {}
    if prev is not None:
        aliases = {len(args): 0}
        in_specs.append(pl.BlockSpec(memory_space=pl.ANY))
        args.append(prev)

    def kernel(*refs):
        if prev is not None:
            refs = refs[:-2] + refs[-1:]
        _flash_pair_kernel(*refs, gs=gs, tq=tq, nk=nk, tk=tk, ctx_mode=ctx_mode if ctx is not None else None,
                           post=post, lam_init=lam_init)

    return pl.pallas_call(
        kernel,
        grid=grid,
        in_specs=in_specs,
        out_specs=pl.BlockSpec((tq, gs * n_q * LANES), lambda b, g, i: (qrow(b, g, i), col0 // gs + g)),
        out_shape=jax.ShapeDtypeStruct((T, out_cols), BF16),
        input_output_aliases=aliases,
        compiler_params=_params("arbitrary", "arbitrary", "arbitrary"),
        name="attn_" + post + ("_latent" if latent else "_context") + str(n_q),
    )(*args)


NA_TQ = NA_QROWS * GRID_W
NA_TK = NA_KROWS * GRID_W


NA_BLOCK_POS = ((0, 0), (NA_QROWS, 0), (GRID_H - NA_QROWS, GRID_H - NA_KROWS))
N_DR = 2 * NA_WIN_R - 1
N_DC = 2 * NA_WIN_C - 1


def _na_bias_kernel(rpb_ref, o_ref):
    qc = lax.broadcasted_iota(jnp.int32, (GRID_W, LANES), 0)
    lane = lax.broadcasted_iota(jnp.int32, (GRID_W, LANES), 1)
    kc = lane % GRID_W
    cs = jnp.clip(qc - NA_WIN_C // 2, 0, GRID_W - NA_WIN_C)
    col_ok = jnp.logical_and(kc >= cs, kc < cs + NA_WIN_C)
    lo = lane < GRID_W
    neg = jnp.full((GRID_W, LANES), NEG, F32)
    for head in range(2):
        toeplitz = []
        for dr in range(N_DR):
            r = jnp.broadcast_to(rpb_ref[head, dr:dr + 1, :] * LOG2E, (GRID_W, LANES))
            t = jnp.where(lo, pltpu.roll(r, LANES - (NA_WIN_C - 1), axis=1, stride=1, stride_axis=0),
                          pltpu.roll(r, GRID_W - (NA_WIN_C - 1), axis=1, stride=1, stride_axis=0))
            toeplitz.append(jnp.where(col_ok, t, neg))
        for pos, (r0, k0) in enumerate(NA_BLOCK_POS):
            for i in range(NA_QROWS):
                qr = r0 + i
                rs = min(max(qr - NA_WIN_R // 2, 0), GRID_H - NA_WIN_R)
                tiles = [toeplitz[k0 + j - qr + NA_WIN_R - 1] if rs <= k0 + j < rs + NA_WIN_R else neg
                         for j in range(NA_KROWS)]
                for jp in range(NA_KROWS // 2):
                    o_ref[pos, pl.ds(head * NA_TQ + i * GRID_W, GRID_W), pl.ds(jp * LANES, LANES)] = jnp.where(
                        lo, tiles[2 * jp], tiles[2 * jp + 1])


def _na_bias(rpb):
    rpb_pad = jnp.pad(rpb, ((0, 0), (0, 16 - N_DR), (0, LANES - N_DC)), constant_values=NEG)
    return pl.pallas_call(
        _na_bias_kernel,
        grid=(4,),
        in_specs=[pl.BlockSpec((2, 16, LANES), lambda g: (g, 0, 0))],
        out_specs=pl.BlockSpec((3, None, 2 * NA_TQ, NA_TK), lambda g: (0, g, 0, 0)),
        out_shape=jax.ShapeDtypeStruct((3, 4, 2 * NA_TQ, NA_TK), F32),
        compiler_params=_params("arbitrary"),
        name="na_bias_table",
    )(rpb_pad)


NA_PAIRS = 4


def _na_kernel(q_ref, k_ref, v_ref, kc_ref, vc_ref, bm_ref, _, o_ref):
    i = pl.program_id(2)
    k0 = jnp.clip(i * NA_QROWS - NA_WIN_R // 2, 0, GRID_H - NA_KROWS)
    start = pl.multiple_of(k0 * GRID_W, GRID_W)
    for g in range(NA_PAIRS):
        qs, lo = _stack_pairs(q_ref.at[g:g + 1], 1, NA_TQ)
        kw = k_ref[g, pl.ds(start, NA_TK), :]
        vw = v_ref[g, pl.ds(start, NA_TK), :]
        s_w = _qk(qs, kw) + bm_ref[g]
        s_c = _qk(qs, _ctx_tile(kc_ref.at[2 * g:2 * g + 2], "pair"))
        m = jnp.maximum(jnp.max(s_w, axis=1, keepdims=True), jnp.max(s_c, axis=1, keepdims=True))
        p_w = jnp.exp2((s_w - m).astype(BF16))
        p_c = jnp.exp2((s_c - m).astype(BF16))
        acc = _dot(p_w, _with_ones(vw)) + _dot(p_c, _with_ones(_ctx_tile(vc_ref.at[2 * g:2 * g + 2], "pair")))
        o = acc[:, 0:LANES] / acc[:, LANES:LANES + 1]
        o_ref[:, g * LANES:(g + 1) * LANES] = jnp.where(lo, o[0:NA_TQ], o[NA_TQ:2 * NA_TQ]).astype(o_ref.dtype)


def _na_latent(q, k, v, kc, vc, bm, prev):
    nblk = N_S // NA_TQ
    qrow = lambda b, g, i: T_P // NA_TQ + b * nblk + i
    krow = lambda b, g, i: T_P // N_S + b
    cfg = lambda i: jnp.where(i == 0, 0, jnp.where(i == nblk - 1, 2, 1))
    ctx_spec = pl.BlockSpec((None, None, 2 * NA_PAIRS, PAST, HD), lambda b, g, i: (b, 0, g, 0, 0))
    return pl.pallas_call(
        _na_kernel,
        grid=(B_S, 4 // NA_PAIRS, nblk),
        in_specs=[pl.BlockSpec((NA_PAIRS, NA_TQ, LANES), lambda b, g, i: (g, qrow(b, g, i), 0)),
                  pl.BlockSpec((NA_PAIRS, N_S, LANES), lambda b, g, i: (g, krow(b, g, i), 0)),
                  pl.BlockSpec((NA_PAIRS, N_S, LANES), lambda b, g, i: (g, krow(b, g, i), 0)),
                  ctx_spec, ctx_spec,
                  pl.BlockSpec((None, NA_PAIRS, 2 * NA_TQ, NA_TK), lambda b, g, i: (cfg(i), g, 0, 0)),
                  pl.BlockSpec(memory_space=pl.ANY)],
        out_specs=pl.BlockSpec((NA_TQ, NA_PAIRS * LANES), lambda b, g, i: (qrow(b, g, i), g)),
        out_shape=jax.ShapeDtypeStruct((T, D), BF16),
        input_output_aliases={6: 0},
        compiler_params=_params("arbitrary", "arbitrary", "arbitrary"),
        name="attn_window_latent",
    )(q, k, v, kc, vc, bm, prev)


def _top4(logits):
    lane = lax.broadcasted_iota(jnp.int32, logits.shape, 1).astype(F32)
    rest = logits
    tops, firsts = [], []
    for _ in range(TOP_K):
        m = jnp.max(rest, axis=1, keepdims=True)
        first = jnp.min(jnp.where(rest == m, lane, float(LANES)), axis=1, keepdims=True)
        tops.append(m)
        firsts.append(first)
        rest = jnp.where(lane == first, -jnp.inf, rest)
    es = [jnp.exp(m - tops[0]) for m in tops]
    denom = es[0] + es[1] + es[2] + es[3]
    w = jnp.zeros_like(logits)
    for k in range(TOP_K):
        w = jnp.where(lane == float(k), es[k] / denom, w)
    return firsts, w


def _local_sort(firsts):
    lane = lax.broadcasted_iota(jnp.int32, (TM, LANES), 1).astype(F32)
    hots = [lane == f for f in firsts]
    sel = jnp.zeros((TM, LANES), F32)
    for hot in hots:
        sel = jnp.where(hot, 1.0, sel)
    r = lax.broadcasted_iota(jnp.int32, (TM, TM), 0)
    c = lax.broadcasted_iota(jnp.int32, (TM, TM), 1)
    earlier = _dot(jnp.where(c < r, 1.0, 0.0).astype(BF16), sel.astype(BF16))
    cnt = jnp.sum(sel, axis=0, keepdims=True)
    r = lax.broadcasted_iota(jnp.int32, (LANES, LANES), 0)
    c = lax.broadcasted_iota(jnp.int32, (LANES, LANES), 1)
    start = _dot(jnp.broadcast_to(cnt, (SUB, LANES)).astype(BF16), jnp.where(r < c, 1.0, 0.0).astype(BF16))[0:1]
    place = start + earlier
    pos = jnp.zeros((TM, LANES), F32)
    for k, hot in enumerate(hots):
        pos = jnp.where(lane == float(k), jnp.sum(jnp.where(hot, place, 0.0), axis=1, keepdims=True), pos)
    return pos.astype(jnp.int32), cnt.astype(jnp.int32)


def _mixer_tail(x_of, h_of, ci, g1_ref, sh2_ref, sc2_ref, lng_ref, lnb_ref, rw_ref, rb_ref,
                x1_ref, u2_ref, pos_ref, w_ref, cnt_ref):
    rows = slice(0, TM)
    x1 = _layer_norm(DEEPNORM_ALPHA * x_of(rows) + g1_ref[pl.ds(ci, 1), :] * h_of(rows), lng_ref[...], lnb_ref[...])
    x1_ref[...] = x1
    u2 = x1 * (1.0 + sc2_ref[pl.ds(ci, 1), :]) + sh2_ref[pl.ds(ci, 1), :]
    u2_ref[...] = u2.astype(BF16)
    firsts, w_ref[...] = _top4(_dot3(u2, rw_ref[...]) + rb_ref[...])
    pos, cnt = _local_sort(firsts)
    pos_ref[...] = pos
    cnt_ref[...] = jnp.broadcast_to(cnt, (SUB, LANES))


def _ab_out_kernel(xp_ref, xs_ref, gb_ref, y_ref, yp_ref, yn_ref, cv_ref, at_ref, wo_ref,
                   g1_ref, sh2_ref, sc2_ref, lng_ref, lnb_ref, rw_ref, rb_ref, *out_refs):
    i = pl.program_id(0)
    is_p = i < NT_P
    ci = _cond_row(i)
    j = (i - NT_P) % TILES_PER_GRID
    first = jnp.logical_or(is_p, j == 0)
    last = jnp.logical_or(is_p, j == TILES_PER_GRID - 1)
    y = y_ref[...]
    row = lax.broadcasted_iota(jnp.int32, y.shape, 0)
    before = jnp.where(first, 0.0, yp_ref[7:8, :])
    after = jnp.where(last, 0.0, yn_ref[0:1, :])
    y_prev = jnp.where(row == 0, before, pltpu.roll(y, 1, axis=0))
    y_next = jnp.where(row == TM - 1, after, pltpu.roll(y, TM - 1, axis=0))
    cv = cv_ref[...]
    conv = (gb_ref[...] * (y_prev * cv[0:1] + y * cv[1:2] + y_next * cv[2:3])).astype(BF16)
    _mixer_tail(lambda r: jnp.where(is_p, xp_ref[r, :], xs_ref[r, :]),
                lambda r: _dot(conv[r], wo_ref[0:512, :]) + _dot(at_ref[r, :], wo_ref[512:1024, :]),
                ci, g1_ref, sh2_ref, sc2_ref, lng_ref, lnb_ref, rw_ref, rb_ref, *out_refs)


def _cd_out_kernel(x_ref, mg_ref, wo_ref, g1_ref, sh2_ref, sc2_ref, lng_ref, lnb_ref, rw_ref, rb_ref,
                   *out_refs):
    ci = _cond_row(pl.program_id(0))
    _mixer_tail(lambda r: x_ref[r, :], lambda r: _dot(mg_ref[r, :], wo_ref[...]),
                ci, g1_ref, sh2_ref, sc2_ref, lng_ref, lnb_ref, rw_ref, rb_ref, *out_refs)


def _tail_specs(layer):
    return [_mod_spec(layer, 2), _mod_spec(layer, 3), _mod_spec(layer, 4),
            _full((1, D)), _full((1, D)), _full((D, LANES)), _full((1, LANES))]


_TAIL_OUT_SPECS = [_row_spec(D), _row_spec(D), _row_spec(LANES), _row_spec(LANES),
                   pl.BlockSpec((None, SUB, LANES), lambda i: (i, 0, 0))]
_TAIL_OUT_SHAPES = [jax.ShapeDtypeStruct((T, D), F32), jax.ShapeDtypeStruct((T, D), BF16),
                    jax.ShapeDtypeStruct((T, LANES), jnp.int32), jax.ShapeDtypeStruct((T, LANES), F32),
                    jax.ShapeDtypeStruct((NT, SUB, LANES), jnp.int32)]


def _ab_out(xp, xs, gb, y, conv_w, attn, w_out, mod, tail):
    halo_prev = pl.BlockSpec((8, 512), lambda i: (jnp.maximum(i * (TM // 8) - 1, 0), 0))
    halo_next = pl.BlockSpec((8, 512), lambda i: (jnp.minimum((i + 1) * (TM // 8), T // 8 - 1), 0))
    return pl.pallas_call(
        _ab_out_kernel,
        grid=(NT,),
        in_specs=_x_specs() + [_row_spec(512), _row_spec(512), halo_prev, halo_next, _full((3, 512)),
                               _row_spec(512), _full((D, D))] + _tail_specs(0),
        out_specs=_TAIL_OUT_SPECS,
        out_shape=_TAIL_OUT_SHAPES,
        compiler_params=_params("arbitrary"),
        name="ab_out_proj",
    )(xp, xs, gb, y, y, y, conv_w, attn, w_out, mod, mod, mod, *tail)


def _cd_out(x, merged, w_out, mod, tail):
    return pl.pallas_call(
        _cd_out_kernel,
        grid=(NT,),
        in_specs=[_row_spec(D), _row_spec(D), _full((D, D))] + _tail_specs(1),
        out_specs=_TAIL_OUT_SPECS,
        out_shape=_TAIL_OUT_SHAPES,
        compiler_params=_params("arbitrary"),
        name="cd_out_proj",
    )(x, merged, w_out, mod, mod, mod, *tail)


def _mix_down(wd_ref, wmix_ref):
    half = D_FF // 2
    for c in range(D // LANES):
        wmix_ref[c, pl.ds(0, half, stride=2), :] = wd_ref[0:half, c * LANES:(c + 1) * LANES]
        wmix_ref[c, pl.ds(1, half, stride=2), :] = wd_ref[half:D_FF, c * LANES:(c + 1) * LANES]


def _prep_expert(wgu_ref, wd_ref, wgu_bf, wmix_ref, wd_bf):
    for c in range(4):
        wgu_bf[:, c * 512:(c + 1) * 512] = wgu_ref[:, c * 512:(c + 1) * 512].astype(BF16)
    _mix_down(wd_ref, wmix_ref)
    for c in range(D // LANES):
        wd_bf[:, c * LANES:(c + 1) * LANES] = wmix_ref[c].astype(BF16)


def _expert_ffn(u, wgu_bf, bgu_ref, wd_bf, bd_ref):
    rows = u.shape[0]
    ga = _dot(u, wgu_bf[:, 0:D_FF]) + bgu_ref[:, 0:D_FF]
    gb = _dot(u, wgu_bf[:, D_FF:2 * D_FF]) + bgu_ref[:, D_FF:2 * D_FF]
    even = (lax.broadcasted_iota(jnp.int32, (rows, LANES), 1) % 2) == 0
    hid = []
    for c in range(D_FF // LANES):
        a = ga[:, c * LANES:(c + 1) * LANES]
        b = gb[:, c * LANES:(c + 1) * LANES]
        gate = jnp.where(even, a, pltpu.roll(b, 1, axis=1))
        up = jnp.where(even, pltpu.roll(a, LANES - 1, axis=1), b)
        gate = jnp.minimum(gate, SWIGLU_LIMIT)
        up = jnp.clip(up, -SWIGLU_LIMIT, SWIGLU_LIMIT)
        hid.append(((up + 1.0) * gate * jax.nn.sigmoid(SWIGLU_ALPHA * gate)).astype(BF16))
    hid = jnp.concatenate(hid, axis=1)
    return _dot(hid, wd_bf[...]) + bd_ref[...]


def _moe_plan(cnt):
    c = cnt[:, 0, :N_EXPERTS]
    counts = jnp.sum(c, axis=0)
    tiles_e = (counts + MOE_TM - 1) // MOE_TM
    tile_end = jnp.cumsum(tiles_e)
    first_row = (tile_end - tiles_e) * MOE_TM
    run_start = first_row[None] + jnp.cumsum(c, axis=0) - c
    j = jnp.arange(MOE_TILES, dtype=jnp.int32)
    tile_expert = jnp.minimum(jnp.sum((j[:, None] >= tile_end[None]).astype(jnp.int32), axis=1), N_EXPERTS - 1)
    i32 = lambda a: a.astype(jnp.int32)
    experts = jnp.arange(N_EXPERTS, dtype=jnp.int32)
    owns = tiles_e > 0
    seg = jnp.cumsum(owns.astype(jnp.int32)) - owns.astype(jnp.int32)
    later = jnp.where((experts[None] > experts[:, None]) & owns[None], experts[None], N_EXPERTS)
    nxt = jnp.min(later, axis=1)
    nxt = jnp.where(nxt == N_EXPERTS, -1, nxt)
    per_tile = lambda v: jnp.sum(jnp.where(tile_expert[:, None] == experts[None], v[None], 0), axis=1)
    return dict(tile_expert=i32(tile_expert), n_used=i32(tile_end[-1:]),
                seg_parity=i32(per_tile(seg % 2)), next_expert=i32(per_tile(nxt)),
                run_start=i32(run_start)[:, None, :], run_len=i32(c)[:, None, :],
                pad_start=i32(first_row + counts)[None, None, :], pad_len=i32(tiles_e * MOE_TM - counts)[None, None, :])


def _for_each_run(start_ref, len_ref, copy):
    def body(e, local):
        n = len_ref[0, e]
        g = start_ref[0, e]

        def pieces(sizes):
            for size in sizes:
                covered = jnp.bitwise_and(n, -2 * size)

                @pl.when(jnp.bitwise_and(n, size) != 0)
                def _(covered=covered, size=size):
                    copy(local + covered, g + covered, size)

        @pl.when(n >= RUN_SIZES[2])
        def _():
            pieces(RUN_SIZES[:3])

        pieces(RUN_SIZES[3:])
        return local + n
    lax.fori_loop(0, N_EXPERTS, body, jnp.int32(0))


def _rows(ref, start, size):
    start = 0 if isinstance(start, int) and start == 0 else pl.multiple_of(start * SUB, SUB)
    return ref.at[pl.ds(start, size * SUB), :]


def _from_tiles(ref, rows):
    return jnp.concatenate([ref[pl.ds(c, rows, stride=SUB), :] for c in range(SUB)], axis=1)


def _to_tiles(ref, x):
    for c in range(SUB):
        ref[pl.ds(c, x.shape[0], stride=SUB), :] = x[:, c * LANES:(c + 1) * LANES]


def _run_spec(index):
    return pl.BlockSpec((None, 1, N_EXPERTS), index, memory_space=pltpu.SMEM)


def _dispatch_kernel(rs_ref, rl_ref, ps_ref, pn_ref, u_ref, pos_ref, xs_hbm, stage, zeros, sem, zsem):
    i = pl.program_id(0)
    slot = i % 2

    def wait(s):
        pltpu.make_async_copy(stage.at[s], stage.at[s], sem.at[s]).wait()

    @pl.when(i == 0)
    def _():
        zeros[...] = jnp.zeros_like(zeros)

        def pad(local, g, size):
            return pltpu.make_async_copy(_rows(zeros, 0, size), _rows(xs_hbm, g, size), zsem.at[0])
        _for_each_run(ps_ref, pn_ref, lambda *a: pad(*a).start())
        _for_each_run(ps_ref, pn_ref, lambda *a: pad(*a).wait())

    @pl.when(i >= 2)
    def _():
        wait(slot)

    place = pos_ref[...].astype(F32).T
    s = lax.broadcasted_iota(jnp.int32, (PAIRS_TM, TM), 0).astype(F32)
    onehot = jnp.zeros((PAIRS_TM, TM), F32)
    for k in range(TOP_K):
        onehot = jnp.where(s == place[k:k + 1, :], 1.0, onehot)
    _to_tiles(stage.at[slot], _dot(onehot.astype(BF16), u_ref[...]))
    _for_each_run(rs_ref, rl_ref, lambda l, g, size: pltpu.make_async_copy(
        _rows(stage.at[slot], l, size), _rows(xs_hbm, g, size), sem.at[slot]).start())

    @pl.when(i == NT - 1)
    def _():
        wait(1 - slot)
        wait(slot)


def _dispatch(u2, pos, plan):
    return pl.pallas_call(
        _dispatch_kernel,
        grid=(NT,),
        in_specs=[_run_spec(lambda i: (i, 0, 0)), _run_spec(lambda i: (i, 0, 0)),
                  _run_spec(lambda i: (0, 0, 0)), _run_spec(lambda i: (0, 0, 0)),
                  _row_spec(D), _row_spec(LANES)],
        out_specs=pl.BlockSpec(memory_space=pl.ANY),
        out_shape=jax.ShapeDtypeStruct((MOE_TILES * MOE_TM * SUB, LANES), F32),
        scratch_shapes=[pltpu.VMEM((2, PAIRS_TM * SUB, LANES), F32), pltpu.VMEM((TM * SUB, LANES), F32),
                        pltpu.SemaphoreType.DMA((2,)), pltpu.SemaphoreType.DMA((1,))],
        compiler_params=_params("arbitrary"),
        name="moe_dispatch",
    )(plan["run_start"], plan["run_len"], plan["pad_start"], plan["pad_len"], u2, pos)


def _moe_kernel(te_ref, nu_ref, par_ref, nxt_ref, x_ref, bgu_ref, bd_ref, wgu_hbm, wd_hbm, y_ref,
                wgu_f, wd_f, wgu_bf, wmix_ref, wd_bf, sem, *, layer):
    j = pl.program_id(0)
    used = j < nu_ref[0]
    buf = par_ref[j]

    def copies(e, s):
        return (pltpu.make_async_copy(wgu_hbm.at[layer, e], wgu_f.at[s], sem.at[s]),
                pltpu.make_async_copy(wd_hbm.at[layer, e], wd_f.at[s], sem.at[s]))

    @pl.when(jnp.logical_and(used, j == 0))
    def _():
        for cp in copies(te_ref[0], 0):
            cp.start(priority=1)

    @pl.when(jnp.logical_and(used, jnp.logical_or(j == 0, te_ref[j] != te_ref[jnp.maximum(j - 1, 0)])))
    def _():
        for cp in copies(te_ref[j], buf):
            cp.wait()

        @pl.when(nxt_ref[j] >= 0)
        def _():
            for cp in copies(nxt_ref[j], 1 - buf):
                cp.start(priority=1)

        _prep_expert(wgu_f.at[buf], wd_f.at[buf], wgu_bf, wmix_ref, wd_bf)

    @pl.when(used)
    def _():
        x = _from_tiles(x_ref, MOE_TM).astype(BF16)
        _to_tiles(y_ref, _expert_ffn(x, wgu_bf, bgu_ref, wd_bf, bd_ref))


def _moe(layer, xs, plan, w_gate_up, b_gate_up, w_down, b_down):
    rows = pl.BlockSpec((MOE_TM * SUB, LANES), lambda j, te, nu, par, nxt: (jnp.minimum(j, nu[0] - 1), 0))
    grid_spec = pltpu.PrefetchScalarGridSpec(
        num_scalar_prefetch=4,
        grid=(MOE_TILES,),
        in_specs=[rows,
                  pl.BlockSpec((None, None, 1, 2 * D_FF), lambda j, te, nu, par, nxt: (layer, te[j], 0, 0)),
                  pl.BlockSpec((None, None, 1, D), lambda j, te, nu, par, nxt: (layer, te[j], 0, 0)),
                  pl.BlockSpec(memory_space=pl.ANY), pl.BlockSpec(memory_space=pl.ANY)],
        out_specs=rows,
        scratch_shapes=[pltpu.VMEM((2, D, 2 * D_FF), F32), pltpu.VMEM((2, D_FF, D), F32),
                        pltpu.VMEM((D, 2 * D_FF), BF16), pltpu.VMEM((D // LANES, D_FF, LANES), F32),
                        pltpu.VMEM((D_FF, D), BF16), pltpu.SemaphoreType.DMA((2,))])

    def kernel(*refs):
        _moe_kernel(*refs, layer=layer)

    return pl.pallas_call(
        kernel,
        grid_spec=grid_spec,
        out_shape=jax.ShapeDtypeStruct((MOE_TILES * MOE_TM * SUB, LANES), F32),
        compiler_params=_params("arbitrary"),
        name="moe_experts",
    )(plan["tile_expert"], plan["n_used"], plan["seg_parity"], plan["next_expert"], xs,
      b_gate_up.reshape(DEPTH, N_EXPERTS, 1, 2 * D_FF), b_down.reshape(DEPTH, N_EXPERTS, 1, D), w_gate_up, w_down)


def _combine_kernel(rs_ref, rl_ref, rsn_ref, rln_ref, x1_ref, pos_ref, w_ref, g2_ref, lng_ref, lnb_ref, ys_hbm,
                    *rest, final):
    out_refs, (stage, sem) = rest[:-2], rest[-2:]
    i = pl.program_id(0)
    slot = i % 2
    other = 1 - slot

    def fetch(start_ref, len_ref, s):
        _for_each_run(start_ref, len_ref, lambda l, g, size: pltpu.make_async_copy(
            _rows(ys_hbm, g, size), _rows(stage.at[s], l, size), sem.at[s]).start())

    def wait(s):
        pltpu.make_async_copy(stage.at[s], stage.at[s], sem.at[s]).wait()

    @pl.when(i == 0)
    def _():
        fetch(rs_ref, rl_ref, 0)

    fetch(rsn_ref, rln_ref, other)
    wait(slot)
    y = _from_tiles(stage.at[slot], PAIRS_TM).astype(BF16)
    place = pos_ref[...].astype(F32)
    w = w_ref[...]
    s = lax.broadcasted_iota(jnp.int32, (TM, PAIRS_TM), 1).astype(F32)
    pick = jnp.zeros((TM, PAIRS_TM), F32)
    for k in range(TOP_K):
        pick = jnp.where(s == place[:, k:k + 1], w[:, k:k + 1], pick)
    f = _dot(pick.astype(BF16), y)
    ci = _cond_row(i)
    z = DEEPNORM_ALPHA * x1_ref[...] + g2_ref[pl.ds(ci, 1), :] * f
    out = _layer_norm(z, lng_ref[...], lnb_ref[...])
    if final:
        @pl.when(i < NT_P)
        def _():
            out_refs[0][...] = out

        @pl.when(i >= NT_P)
        def _():
            out_refs[1][...] = out
    else:
        out_refs[0][...] = out

    @pl.when(i == NT - 1)
    def _():
        wait(other)


def _combine(layer, x1, ys, pos, w, plan, mod, g, b):
    final = layer == DEPTH - 1
    nxt = lambda i: (jnp.minimum(i + 1, NT - 1), 0, 0)
    cur = lambda i: (i, 0, 0)
    if final:
        out_specs = _x_specs()
        out_shape = [jax.ShapeDtypeStruct((T_P, D), F32), jax.ShapeDtypeStruct((T_S, D), F32)]
    else:
        out_specs = _row_spec(D)
        out_shape = jax.ShapeDtypeStruct((T, D), F32)

    def kernel(*refs):
        _combine_kernel(*refs, final=final)

    return pl.pallas_call(
        kernel,
        grid=(NT,),
        in_specs=[_run_spec(cur), _run_spec(cur), _run_spec(nxt), _run_spec(nxt),
                  _row_spec(D), _row_spec(LANES), _row_spec(LANES), _mod_spec(layer, 5), _full((1, D)), _full((1, D)),
                  pl.BlockSpec(memory_space=pl.ANY)],
        out_specs=out_specs,
        out_shape=out_shape,
        scratch_shapes=[pltpu.VMEM((2, PAIRS_TM * SUB, LANES), F32), pltpu.SemaphoreType.DMA((2,))],
        compiler_params=_params("arbitrary"),
        name="moe_combine_norm" + ("_final" if final else ""),
    )(plan["run_start"], plan["run_len"], plan["run_start"], plan["run_len"], x1, pos, w, mod, g, b, ys)


def _moe_layer(layer, routed, experts, mod, ln2_g, ln2_b):
    x1, u2, pos, w, cnt = routed
    plan = _moe_plan(cnt)
    ys = _moe(layer, _dispatch(u2, pos, plan), plan, *experts)
    return _combine(layer, x1, ys, pos, w, plan, mod, ln2_g[layer][None], ln2_b[layer][None])


def _router_tail(l, ln1_g, ln1_b, router_w, router_b):
    rw = jnp.pad(router_w[l], ((0, 0), (0, LANES - N_EXPERTS)))
    rb = jnp.pad(router_b[l], (0, LANES - N_EXPERTS), constant_values=NEG)
    return ln1_g[l][None], ln1_b[l][None], rw, rb[None]


def kernel(x_prompt, x_sample, c, c_ctx, cache_diff_k, cache_diff_v, cache_na_k, cache_na_v, cache_gqa_k, cache_gqa_v, w_mod, b_mod, ln1_g, ln1_b, ln2_g, ln2_b, ab_w_in, ab_conv_w, ab_lambda_q1, ab_lambda_k1, ab_lambda_q2, ab_lambda_k2, ab_subln_g, ab_w_out, cd_w_in, cd_na_rpb, cd_q_norm_g, cd_k_norm_g, cd_w_out, router_w, router_b, w_gate_up, b_gate_up, w_down, b_down):
    xp = x_prompt.reshape(T_P, D)
    xs = x_sample.reshape(T_S, D)
    cond8 = jnp.concatenate([c_ctx[None], c, jnp.zeros((8 - 1 - B_S, D), F32)], axis=0)
    mod = _modulation(cond8, w_mod, b_mod)
    rope = _rope_tables()
    experts = (w_gate_up, b_gate_up, w_down, b_down)

    gb, y, q, k, v, new_diff_k, new_diff_v = _ab_in(xp, xs, mod, ab_w_in[0].astype(BF16), rope)
    lam_init = 0.8 - 0.6 * 1.0
    diff = (jnp.stack([ab_lambda_q1[0], ab_lambda_k1[0], ab_lambda_q2[0], ab_lambda_k2[0]]), ab_subln_g[0][None])
    attn = _flash_pair(q, k, v, latent=False, out_cols=512, col0=0, post="diff", diff=diff, lam_init=lam_init)
    attn = _flash_pair(q, k, v, latent=True, out_cols=512, col0=0, post="diff", diff=diff, lam_init=lam_init,
                       ctx=(cache_diff_k, cache_diff_v), ctx_mode="wide", prev=attn)
    routed = _ab_out(xp, xs, gb, y, ab_conv_w[0], attn, ab_w_out[0].astype(BF16), mod,
                     _router_tail(0, ln1_g, ln1_b, router_w, router_b))
    x = _moe_layer(0, routed, experts, mod, ln2_g, ln2_b)

    qg = jnp.tile(cd_q_norm_g[0], 2)[None]
    kg = jnp.tile(cd_k_norm_g[0], 2)[None]
    nq, nk, nv, gq, gk, gv, new_na_k, new_na_v, new_gqa_k, new_gqa_v = _cd_in(
        x, mod, cd_w_in[0].astype(BF16), qg, kg, rope)
    merged = _flash_pair(nq, nk, nv, latent=False, out_cols=D, col0=0, post="select")
    merged = _flash_pair(gq, gk, gv, latent=False, out_cols=D, col0=2, post="select", prev=merged)
    merged = _na_latent(nq, nk, nv, cache_na_k, cache_na_v, _na_bias(cd_na_rpb[0]), merged)
    merged = _flash_pair(gq, gk, gv, latent=True, out_cols=D, col0=2, post="select",
                         ctx=(cache_gqa_k, cache_gqa_v), ctx_mode="dup", prev=merged)
    routed = _cd_out(x, merged, cd_w_out[0].astype(BF16), mod,
                     _router_tail(1, ln1_g, ln1_b, router_w, router_b))
    y_p, y_s = _moe_layer(1, routed, experts, mod, ln2_g, ln2_b)

    return (y_p.reshape(B_P, N_P, D), y_s.reshape(B_S, N_S, D), new_diff_k, new_diff_v,
            new_na_k, new_na_v, new_gqa_k, new_gqa_v)
```

```python
import jax
import jax.numpy as jnp
import numpy as np
from jax import lax
from jax.experimental import pallas as pl
from jax.experimental.pallas import tpu as pltpu

F32 = jnp.float32
BF16 = jnp.bfloat16

D = 1024
B_P, N_P = 16, 256
B_S, N_S = 2, 4096
PAST = 256
T_P, T_S = B_P * N_P, B_S * N_S
T = T_P + T_S
TM = 256
NT_P, NT_S, NT = T_P // TM, T_S // TM, T // TM
TILES_PER_GRID = N_S // TM
GRID_W = 64
GRID_H = N_S // GRID_W
HD = 64
DEPTH = 2
N_EXPERTS = 32
TOP_K = 4
D_FF = 1024
NA_WIN_R, NA_WIN_C = 8, 16
NA_QROWS = 4
NA_KROWS = 12
SWIGLU_LIMIT = 7.0
SWIGLU_ALPHA = 1.702
ROPE_THETA = 10000.0
DEEPNORM_ALPHA = (2 * DEPTH) ** 0.25
LN_EPS = 1e-5
RMS_EPS = 1e-6
LOG2E = 1.4426950408889634
QK_SCALE = HD ** -0.5 * LOG2E
NEG = -1e30
MOE_TM = 256
MOE_PAIRS = TOP_K * T
MOE_TILES = MOE_PAIRS // MOE_TM + N_EXPERTS
PAIRS_TM = TOP_K * TM
RUN_SIZES = tuple(TM >> b for b in range(TM.bit_length()))
LANES = 128
SUB = 8

VMEM_LIMIT = 56 * 1024 * 1024


def _params(*sem):
    return pltpu.CompilerParams(dimension_semantics=sem, vmem_limit_bytes=VMEM_LIMIT)


def _split(x):
    hi = x.astype(BF16)
    lo = (x - hi.astype(F32)).astype(BF16)
    return hi, lo


def _dot(a, b):
    return jnp.dot(a, b, preferred_element_type=F32)


def _dot3(a, b):
    ah, al = _split(a)
    bh, bl = _split(b)
    return _dot(ah, bh) + (_dot(ah, bl) + _dot(al, bh))


def _cond_row(i):
    return jnp.where(i < NT_P, 0, 1 + (i - NT_P) // TILES_PER_GRID)


def _layer_norm(z, g, b):
    mu = jnp.mean(z, axis=-1, keepdims=True)
    zc = z - mu
    var = jnp.mean(zc * zc, axis=-1, keepdims=True)
    return zc * lax.rsqrt(var + LN_EPS) * g + b


def _low_half(rows):
    return lax.broadcasted_iota(jnp.int32, (rows, LANES), 1) < HD


def _mod_kernel(c_ref, w_ref, b_ref, o_ref):
    c = c_ref[...]
    o_ref[...] = _dot3(c * jax.nn.sigmoid(c), w_ref[...]) + b_ref[...]


def _modulation(cond8, w_mod, b_mod):
    return pl.pallas_call(
        _mod_kernel,
        grid=(DEPTH, 6),
        in_specs=[pl.BlockSpec((8, D), lambda l, j: (0, 0)),
                  pl.BlockSpec((None, D, D), lambda l, j: (l, 0, j)),
                  pl.BlockSpec((None, 1, D), lambda l, j: (l, 0, j))],
        out_specs=pl.BlockSpec((None, 8, D), lambda l, j: (l, 0, j)),
        out_shape=jax.ShapeDtypeStruct((DEPTH, 8, 6 * D), F32),
        compiler_params=_params("arbitrary", "arbitrary"),
        name="modulation",
    )(cond8, w_mod, b_mod.reshape(DEPTH, 1, 6 * D))


def _mod_spec(layer, chunk):
    return pl.BlockSpec((None, 8, D), lambda i, _l=layer, _c=chunk: (_l, 0, _c))


def _full(shape):
    return pl.BlockSpec(shape, lambda i: (0,) * len(shape))


def _rope_tables():
    t = np.arange(N_S)
    half = HD // 2
    inv = ROPE_THETA ** (-np.arange(0, half, 2, dtype=np.float64) / half)
    inv_lane = np.tile(np.repeat(inv, 2), 2 * LANES // HD)
    lane = np.arange(LANES)
    by_row = (lane % HD) < half
    ang = np.where(by_row[None], (t // GRID_W)[:, None], (t % GRID_W)[:, None]) * inv_lane[None]
    cos, sin = np.cos(ang), np.sin(ang)
    even = (lane % 2) == 0
    return tuple(jnp.asarray(a, F32) for a in (cos, np.where(even, -sin, 0.0), np.where(even, 0.0, sin)))


def _rope(x, a, b, c):
    return x * a + pltpu.roll(x, LANES - 1, axis=1) * b + pltpu.roll(x, 1, axis=1) * c


def _rope_or_identity(identity, ra_ref, rb_ref, rc_ref):
    return (jnp.where(identity, 1.0, ra_ref[...]), jnp.where(identity, 0.0, rb_ref[...]),
            jnp.where(identity, 0.0, rc_ref[...]))


def _rope_spec():
    return pl.BlockSpec((TM, LANES), lambda i: (jnp.maximum(i - NT_P, 0) % TILES_PER_GRID, 0))


def _x_specs():
    return [pl.BlockSpec((TM, D), lambda i: (jnp.minimum(i, NT_P - 1), 0)),
            pl.BlockSpec((TM, D), lambda i: (jnp.maximum(i - NT_P, 0), 0))]


def _cache_spec(heads, width):
    return pl.BlockSpec((None, None, heads, N_P, width), lambda i: (jnp.minimum(i, NT_P - 1), 0, 0, 0, 0))


def _row_spec(width):
    return pl.BlockSpec((TM, width), lambda i: (i, 0))


def _hm_spec(n):
    return pl.BlockSpec((n, TM, LANES), lambda i: (0, i, 0))


AB_Q0, AB_K0, AB_V0 = 1536, 2048, 2560


def _tile(p, col0, j):
    return p[:, col0 + j * LANES:col0 + (j + 1) * LANES]


def _ab_in_kernel(xp_ref, xs_ref, sh_ref, sc_ref, w_ref, ra_ref, rb_ref, rc_ref,
                  gb_ref, y_ref, q_ref, k_ref, v_ref, kc_ref, vc_ref):
    i = pl.program_id(0)
    is_p = i < NT_P
    ci = _cond_row(i)
    x = jnp.where(is_p, xp_ref[...], xs_ref[...])
    u = x * (1.0 + sc_ref[pl.ds(ci, 1), :]) + sh_ref[pl.ds(ci, 1), :]
    p = _dot(u.astype(BF16), w_ref[...])
    gb_ref[...] = p[:, 0:512]
    y_ref[...] = p[:, 512:1024] * p[:, 1024:1536]
    a, b, c = _rope_or_identity(is_p, ra_ref, rb_ref, rc_ref)
    for h in range(4):
        v_ref[h] = _tile(p, AB_V0, h).astype(BF16)
        q_ref[h] = (_rope(_tile(p, AB_Q0, h), a, b, c) * QK_SCALE).astype(BF16)
        k_ref[h] = _rope(_tile(p, AB_K0, h), a, b, c).astype(BF16)

    @pl.when(is_p)
    def _():
        for h in range(4):
            kc_ref[h] = _tile(p, AB_K0, h)
            vc_ref[h] = _tile(p, AB_V0, h)


def _ab_in(xp, xs, mod, w_in, rope):
    hm = jax.ShapeDtypeStruct((4, T, LANES), BF16)
    cache = jax.ShapeDtypeStruct((B_P, 1, 4, N_P, LANES), F32)
    half = jax.ShapeDtypeStruct((T, 512), F32)
    return pl.pallas_call(
        _ab_in_kernel,
        grid=(NT,),
        in_specs=_x_specs() + [_mod_spec(0, 0), _mod_spec(0, 1), _full((D, 3072)),
                               _rope_spec(), _rope_spec(), _rope_spec()],
        out_specs=[_row_spec(512), _row_spec(512), _hm_spec(4), _hm_spec(4), _hm_spec(4),
                   _cache_spec(4, LANES), _cache_spec(4, LANES)],
        out_shape=[half, half, hm, hm, hm, cache, cache],
        compiler_params=_params("arbitrary"),
        name="ab_in_proj",
    )(xp, xs, mod, mod, w_in, *rope)


CD_NQ, CD_NK, CD_NV, CD_GQ, CD_GK, CD_GV = 0, 512, 1024, 1536, 2048, 2176


def _seg_mean64(s):
    r = lax.broadcasted_iota(jnp.int32, (LANES, LANES), 0) // HD
    c = lax.broadcasted_iota(jnp.int32, (LANES, LANES), 1) // HD
    seg = jnp.where(r == c, 1.0, 0.0).astype(BF16)
    hi, lo = _split(s)
    return (_dot(hi, seg) + _dot(lo, seg)) * (1.0 / HD)


def _rms64(x, g):
    return x * lax.rsqrt(_seg_mean64(x * x) + RMS_EPS) * g


def _dup_halves(x, lo):
    r = pltpu.roll(x, HD, axis=1)
    return jnp.where(lo, x, r), jnp.where(lo, r, x)


def _cd_in_kernel(x_ref, sh_ref, sc_ref, w_ref, qg_ref, kg_ref, ra_ref, rb_ref, rc_ref,
                  nq_ref, nk_ref, nv_ref, gq_ref, gk_ref, gv_ref,
                  nkc_ref, nvc_ref, gkc_ref, gvc_ref):
    i = pl.program_id(0)
    is_p = i < NT_P
    ci = _cond_row(i)
    u = x_ref[...] * (1.0 + sc_ref[pl.ds(ci, 1), :]) + sh_ref[pl.ds(ci, 1), :]
    p = _dot(u.astype(BF16), w_ref[...])
    lo = _low_half(TM)
    for j in range(4):
        nq_ref[j] = (_tile(p, CD_NQ, j) * QK_SCALE).astype(BF16)
        nk_ref[j] = _tile(p, CD_NK, j).astype(BF16)
        nv_ref[j] = _tile(p, CD_NV, j).astype(BF16)
    gq = [_rms64(_tile(p, CD_GQ, j), qg_ref[...]) for j in range(4)]
    gk = _rms64(_tile(p, CD_GK, 0), kg_ref[...])
    gv = _tile(p, CD_GV, 0)
    v0, v1 = _dup_halves(gv, lo)
    gv_ref[0] = v0.astype(BF16)
    gv_ref[1] = v1.astype(BF16)
    a, b, c = _rope_or_identity(is_p, ra_ref, rb_ref, rc_ref)
    for j in range(4):
        gq_ref[j] = (_rope(gq[j], a, b, c) * QK_SCALE).astype(BF16)
    k0, k1 = _dup_halves(_rope(gk, a, b, c), lo)
    gk_ref[0] = k0.astype(BF16)
    gk_ref[1] = k1.astype(BF16)

    @pl.when(is_p)
    def _():
        gkc_ref[0] = k0[:, 0:HD]
        gkc_ref[1] = k1[:, 0:HD]
        gvc_ref[0] = v0[:, 0:HD]
        gvc_ref[1] = v1[:, 0:HD]
        for j in range(4):
            for src, dst in ((CD_NK, nkc_ref), (CD_NV, nvc_ref)):
                a, b = _dup_halves(_tile(p, src, j), lo)
                dst[2 * j] = a[:, 0:HD]
                dst[2 * j + 1] = b[:, 0:HD]


def _cd_in(x, mod, w_in, qg, kg, rope):
    hm4 = jax.ShapeDtypeStruct((4, T, LANES), BF16)
    hm2 = jax.ShapeDtypeStruct((2, T, LANES), BF16)
    c8 = jax.ShapeDtypeStruct((B_P, 1, 8, N_P, HD), F32)
    c2 = jax.ShapeDtypeStruct((B_P, 1, 2, N_P, HD), F32)
    return pl.pallas_call(
        _cd_in_kernel,
        grid=(NT,),
        in_specs=[_row_spec(D), _mod_spec(1, 0), _mod_spec(1, 1), _full((D, 2304)),
                  _full((1, LANES)), _full((1, LANES)), _rope_spec(), _rope_spec(), _rope_spec()],
        out_specs=[_hm_spec(4), _hm_spec(4), _hm_spec(4), _hm_spec(4), _hm_spec(2), _hm_spec(2),
                   _cache_spec(8, HD), _cache_spec(8, HD), _cache_spec(2, HD), _cache_spec(2, HD)],
        out_shape=[hm4, hm4, hm4, hm4, hm2, hm2, c8, c8, c2, c2],
        compiler_params=_params("arbitrary"),
        name="cd_in_proj",
    )(x, mod, mod, w_in, qg, kg, *rope)


def _stack_pairs(q_ref, n_q, tq):
    lo = _low_half(tq)
    parts = []
    for j in range(n_q):
        q = q_ref[j]
        zero = jnp.zeros_like(q)
        parts += [jnp.where(lo, q, zero), jnp.where(lo, zero, q)]
    return jnp.concatenate(parts, axis=0), lo


def _qk(qs, kb):
    return lax.dot_general(qs, kb, (((1,), (1,)), ((), ())), preferred_element_type=F32)


def _ctx_tile(ref, mode):
    if mode == "wide":
        x = ref[...]
    elif mode == "pair":
        x = jnp.concatenate([ref[0], ref[1]], axis=1)
    else:
        x = jnp.concatenate([ref[...], ref[...]], axis=1)
    return x.astype(BF16)


def _with_ones(v):
    lane = lax.broadcasted_iota(jnp.int32, v.shape, 1)
    return jnp.concatenate([v, jnp.where(lane == 0, 1.0, 0.0).astype(v.dtype)], axis=1)


def _flash_pair_kernel(*refs, gs, ctx_mode, **kw):
    it = iter(refs)
    q_ref, k_ref, v_ref = next(it), next(it), next(it)
    ctx = [next(it), next(it)] if ctx_mode else []
    rest = list(it)
    o_ref = rest.pop()
    n_q = q_ref.shape[0] // gs
    for t in range(gs * n_q):
        gi = t // n_q
        _flash_group(q_ref.at[t:t + 1], k_ref.at[gi], v_ref.at[gi], *[r.at[gi] for r in ctx], *rest,
                     o_ref.at[:, t * LANES:(t + 1) * LANES], n_q=1, ctx_mode=ctx_mode, **kw)


def _flash_group(*refs, n_q, tq, nk, tk, ctx_mode, post, lam_init):
    it = iter(refs)
    q_ref, k_ref, v_ref = next(it), next(it), next(it)
    kc_ref, vc_ref = (next(it), next(it)) if ctx_mode else (None, None)
    lam_ref, g_ref = (next(it), next(it)) if post == "diff" else (None, None)
    o_ref = next(it)

    qs, lo = _stack_pairs(q_ref, n_q, tq)
    rows = 2 * n_q * tq

    def step(kb, vb, carry):
        m, acc = carry
        s = _qk(qs, kb)
        m_new = jnp.maximum(m, jnp.max(s, axis=1, keepdims=True))
        p = jnp.exp2((s - m_new).astype(BF16))
        return m_new, jnp.exp2(m - m_new) * acc + _dot(p, _with_ones(vb))

    carry = (jnp.full((rows, 1), NEG, F32), jnp.zeros((rows, 2 * LANES), F32))
    for c in range(nk // tk):
        carry = step(k_ref[c * tk:(c + 1) * tk, :], v_ref[c * tk:(c + 1) * tk, :], carry)
    if ctx_mode:
        carry = step(_ctx_tile(kc_ref, ctx_mode), _ctx_tile(vc_ref, ctx_mode), carry)
    _, acc = carry
    o = acc[:, 0:LANES] / acc[:, LANES:LANES + 1]

    if post == "diff":
        lp = lam_ref[...]
        lam = (jnp.exp(jnp.sum(lp[0:1] * lp[1:2], axis=1, keepdims=True))
               - jnp.exp(jnp.sum(lp[2:3] * lp[3:4], axis=1, keepdims=True)) + lam_init)
        a = o[0:tq] - lam * o[tq:2 * tq]
        ms = jnp.mean(a * a, axis=-1, keepdims=True)
        o_ref[...] = (a * lax.rsqrt(ms + RMS_EPS) * g_ref[...] * (1.0 - lam_init)).astype(o_ref.dtype)
    else:
        for j in range(n_q):
            o_ref[:, j * LANES:(j + 1) * LANES] = jnp.where(
                lo, o[2 * j * tq:(2 * j + 1) * tq], o[(2 * j + 1) * tq:(2 * j + 2) * tq]).astype(o_ref.dtype)


def _flash_pair(q, k, v, *, latent, out_cols, col0, post, ctx=None, ctx_mode=None, diff=None,
                lam_init=0.0, prev=None):
    groups = k.shape[0]
    n_q = q.shape[0] // groups
    if latent:
        tq, nk, tk, gs = TM, N_S, 2048, 4 // n_q
        grid = (B_S, groups // gs, TILES_PER_GRID)
        qrow = lambda b, g, i: NT_P + b * TILES_PER_GRID + i
        krow = lambda b, g, i: T_P // N_S + b
    else:
        tq, nk, tk, gs = N_P, N_P, N_P, groups
        grid = (B_P, 1, 1)
        qrow = lambda b, g, i: b
        krow = lambda b, g, i: b
    in_specs = [pl.BlockSpec((gs * n_q, tq, LANES), lambda b, g, i: (g, qrow(b, g, i), 0)),
                pl.BlockSpec((gs, nk, LANES), lambda b, g, i: (g, krow(b, g, i), 0)),
                pl.BlockSpec((gs, nk, LANES), lambda b, g, i: (g, krow(b, g, i), 0))]
    args = [q, k, v]
    if ctx is not None:
        width = LANES if ctx_mode == "wide" else HD
        spec = pl.BlockSpec((None, None, gs, PAST, width), lambda b, g, i: (b, 0, g, 0, 0))
        in_specs += [spec, spec]
        args += list(ctx)
    if diff is not None:
        in_specs += [pl.BlockSpec((4, HD), lambda b, g, i: (0, 0)),
                     pl.BlockSpec((1, LANES), lambda b, g, i: (0, 0))]
        args += list(diff)
    aliases = {}
    if prev is not None:
        aliases = {len(args): 0}
        in_specs.append(pl.BlockSpec(memory_space=pl.ANY))
        args.append(prev)

    def kernel(*refs):
        if prev is not None:
            refs = refs[:-2] + refs[-1:]
        _flash_pair_kernel(*refs, gs=gs, tq=tq, nk=nk, tk=tk, ctx_mode=ctx_mode if ctx is not None else None,
                           post=post, lam_init=lam_init)

    return pl.pallas_call(
        kernel,
        grid=grid,
        in_specs=in_specs,
        out_specs=pl.BlockSpec((tq, gs * n_q * LANES), lambda b, g, i: (qrow(b, g, i), col0 // gs + g)),
        out_shape=jax.ShapeDtypeStruct((T, out_cols), BF16),
        input_output_aliases=aliases,
        compiler_params=_params("arbitrary", "arbitrary", "arbitrary"),
        name="attn_" + post + ("_latent" if latent else "_context") + str(n_q),
    )(*args)


NA_TQ = NA_QROWS * GRID_W
NA_TK = NA_KROWS * GRID_W


NA_BLOCK_POS = ((0, 0), (NA_QROWS, 0), (GRID_H - NA_QROWS, GRID_H - NA_KROWS))
N_DR = 2 * NA_WIN_R - 1
N_DC = 2 * NA_WIN_C - 1


def _na_bias_kernel(rpb_ref, o_ref):
    qc = lax.broadcasted_iota(jnp.int32, (GRID_W, LANES), 0)
    lane = lax.broadcasted_iota(jnp.int32, (GRID_W, LANES), 1)
    kc = lane % GRID_W
    cs = jnp.clip(qc - NA_WIN_C // 2, 0, GRID_W - NA_WIN_C)
    col_ok = jnp.logical_and(kc >= cs, kc < cs + NA_WIN_C)
    lo = lane < GRID_W
    neg = jnp.full((GRID_W, LANES), NEG, F32)
    for head in range(2):
        toeplitz = []
        for dr in range(N_DR):
            r = jnp.broadcast_to(rpb_ref[head, dr:dr + 1, :] * LOG2E, (GRID_W, LANES))
            t = jnp.where(lo, pltpu.roll(r, LANES - (NA_WIN_C - 1), axis=1, stride=1, stride_axis=0),
                          pltpu.roll(r, GRID_W - (NA_WIN_C - 1), axis=1, stride=1, stride_axis=0))
            toeplitz.append(jnp.where(col_ok, t, neg))
        for pos, (r0, k0) in enumerate(NA_BLOCK_POS):
            for i in range(NA_QROWS):
                qr = r0 + i
                rs = min(max(qr - NA_WIN_R // 2, 0), GRID_H - NA_WIN_R)
                tiles = [toeplitz[k0 + j - qr + NA_WIN_R - 1] if rs <= k0 + j < rs + NA_WIN_R else neg
                         for j in range(NA_KROWS)]
                for jp in range(NA_KROWS // 2):
                    o_ref[pos, pl.ds(head * NA_TQ + i * GRID_W, GRID_W), pl.ds(jp * LANES, LANES)] = jnp.where(
                        lo, tiles[2 * jp], tiles[2 * jp + 1])


def _na_bias(rpb):
    rpb_pad = jnp.pad(rpb, ((0, 0), (0, 16 - N_DR), (0, LANES - N_DC)), constant_values=NEG)
    return pl.pallas_call(
        _na_bias_kernel,
        grid=(4,),
        in_specs=[pl.BlockSpec((2, 16, LANES), lambda g: (g, 0, 0))],
        out_specs=pl.BlockSpec((3, None, 2 * NA_TQ, NA_TK), lambda g: (0, g, 0, 0)),
        out_shape=jax.ShapeDtypeStruct((3, 4, 2 * NA_TQ, NA_TK), F32),
        compiler_params=_params("arbitrary"),
        name="na_bias_table",
    )(rpb_pad)


NA_PAIRS = 4


def _na_kernel(q_ref, k_ref, v_ref, kc_ref, vc_ref, bm_ref, _, o_ref):
    i = pl.program_id(2)
    k0 = jnp.clip(i * NA_QROWS - NA_WIN_R // 2, 0, GRID_H - NA_KROWS)
    start = pl.multiple_of(k0 * GRID_W, GRID_W)
    for g in range(NA_PAIRS):
        qs, lo = _stack_pairs(q_ref.at[g:g + 1], 1, NA_TQ)
        kw = k_ref[g, pl.ds(start, NA_TK), :]
        vw = v_ref[g, pl.ds(start, NA_TK), :]
        s_w = _qk(qs, kw) + bm_ref[g]
        s_c = _qk(qs, _ctx_tile(kc_ref.at[2 * g:2 * g + 2], "pair"))
        m = jnp.maximum(jnp.max(s_w, axis=1, keepdims=True), jnp.max(s_c, axis=1, keepdims=True))
        p_w = jnp.exp2((s_w - m).astype(BF16))
        p_c = jnp.exp2((s_c - m).astype(BF16))
        acc = _dot(p_w, _with_ones(vw)) + _dot(p_c, _with_ones(_ctx_tile(vc_ref.at[2 * g:2 * g + 2], "pair")))
        o = acc[:, 0:LANES] / acc[:, LANES:LANES + 1]
        o_ref[:, g * LANES:(g + 1) * LANES] = jnp.where(lo, o[0:NA_TQ], o[NA_TQ:2 * NA_TQ]).astype(o_ref.dtype)


def _na_latent(q, k, v, kc, vc, bm, prev):
    nblk = N_S // NA_TQ
    qrow = lambda b, g, i: T_P // NA_TQ + b * nblk + i
    krow = lambda b, g, i: T_P // N_S + b
    cfg = lambda i: jnp.where(i == 0, 0, jnp.where(i == nblk - 1, 2, 1))
    ctx_spec = pl.BlockSpec((None, None, 2 * NA_PAIRS, PAST, HD), lambda b, g, i: (b, 0, g, 0, 0))
    return pl.pallas_call(
        _na_kernel,
        grid=(B_S, 4 // NA_PAIRS, nblk),
        in_specs=[pl.BlockSpec((NA_PAIRS, NA_TQ, LANES), lambda b, g, i: (g, qrow(b, g, i), 0)),
                  pl.BlockSpec((NA_PAIRS, N_S, LANES), lambda b, g, i: (g, krow(b, g, i), 0)),
                  pl.BlockSpec((NA_PAIRS, N_S, LANES), lambda b, g, i: (g, krow(b, g, i), 0)),
                  ctx_spec, ctx_spec,
                  pl.BlockSpec((None, NA_PAIRS, 2 * NA_TQ, NA_TK), lambda b, g, i: (cfg(i), g, 0, 0)),
                  pl.BlockSpec(memory_space=pl.ANY)],
        out_specs=pl.BlockSpec((NA_TQ, NA_PAIRS * LANES), lambda b, g, i: (qrow(b, g, i), g)),
        out_shape=jax.ShapeDtypeStruct((T, D), BF16),
        input_output_aliases={6: 0},
        compiler_params=_params("arbitrary", "arbitrary", "arbitrary"),
        name="attn_window_latent",
    )(q, k, v, kc, vc, bm, prev)


def _top4(logits):
    lane = lax.broadcasted_iota(jnp.int32, logits.shape, 1).astype(F32)
    rest = logits
    tops, firsts = [], []
    for _ in range(TOP_K):
        m = jnp.max(rest, axis=1, keepdims=True)
        first = jnp.min(jnp.where(rest == m, lane, float(LANES)), axis=1, keepdims=True)
        tops.append(m)
        firsts.append(first)
        rest = jnp.where(lane == first, -jnp.inf, rest)
    es = [jnp.exp(m - tops[0]) for m in tops]
    denom = es[0] + es[1] + es[2] + es[3]
    w = jnp.zeros_like(logits)
    for k in range(TOP_K):
        w = jnp.where(lane == float(k), es[k] / denom, w)
    return firsts, w


def _local_sort(firsts):
    lane = lax.broadcasted_iota(jnp.int32, (TM, LANES), 1).astype(F32)
    hots = [lane == f for f in firsts]
    sel = jnp.zeros((TM, LANES), F32)
    for hot in hots:
        sel = jnp.where(hot, 1.0, sel)
    r = lax.broadcasted_iota(jnp.int32, (TM, TM), 0)
    c = lax.broadcasted_iota(jnp.int32, (TM, TM), 1)
    earlier = _dot(jnp.where(c < r, 1.0, 0.0).astype(BF16), sel.astype(BF16))
    cnt = jnp.sum(sel, axis=0, keepdims=True)
    r = lax.broadcasted_iota(jnp.int32, (LANES, LANES), 0)
    c = lax.broadcasted_iota(jnp.int32, (LANES, LANES), 1)
    start = _dot(jnp.broadcast_to(cnt, (SUB, LANES)).astype(BF16), jnp.where(r < c, 1.0, 0.0).astype(BF16))[0:1]
    place = start + earlier
    pos = jnp.zeros((TM, LANES), F32)
    for k, hot in enumerate(hots):
        pos = jnp.where(lane == float(k), jnp.sum(jnp.where(hot, place, 0.0), axis=1, keepdims=True), pos)
    return pos.astype(jnp.int32), cnt.astype(jnp.int32)


def _mixer_tail(x_of, h_of, ci, g1_ref, sh2_ref, sc2_ref, lng_ref, lnb_ref, rw_ref, rb_ref,
                x1_ref, u2_ref, pos_ref, w_ref, cnt_ref):
    rows = slice(0, TM)
    x1 = _layer_norm(DEEPNORM_ALPHA * x_of(rows) + g1_ref[pl.ds(ci, 1), :] * h_of(rows), lng_ref[...], lnb_ref[...])
    x1_ref[...] = x1
    u2 = x1 * (1.0 + sc2_ref[pl.ds(ci, 1), :]) + sh2_ref[pl.ds(ci, 1), :]
    u2_ref[...] = u2.astype(BF16)
    firsts, w_ref[...] = _top4(_dot3(u2, rw_ref[...]) + rb_ref[...])
    pos, cnt = _local_sort(firsts)
    pos_ref[...] = pos
    cnt_ref[...] = jnp.broadcast_to(cnt, (SUB, LANES))


def _ab_out_kernel(xp_ref, xs_ref, gb_ref, y_ref, yp_ref, yn_ref, cv_ref, at_ref, wo_ref,
                   g1_ref, sh2_ref, sc2_ref, lng_ref, lnb_ref, rw_ref, rb_ref, *out_refs):
    i = pl.program_id(0)
    is_p = i < NT_P
    ci = _cond_row(i)
    j = (i - NT_P) % TILES_PER_GRID
    first = jnp.logical_or(is_p, j == 0)
    last = jnp.logical_or(is_p, j == TILES_PER_GRID - 1)
    y = y_ref[...]
    row = lax.broadcasted_iota(jnp.int32, y.shape, 0)
    before = jnp.where(first, 0.0, yp_ref[7:8, :])
    after = jnp.where(last, 0.0, yn_ref[0:1, :])
    y_prev = jnp.where(row == 0, before, pltpu.roll(y, 1, axis=0))
    y_next = jnp.where(row == TM - 1, after, pltpu.roll(y, TM - 1, axis=0))
    cv = cv_ref[...]
    conv = (gb_ref[...] * (y_prev * cv[0:1] + y * cv[1:2] + y_next * cv[2:3])).astype(BF16)
    _mixer_tail(lambda r: jnp.where(is_p, xp_ref[r, :], xs_ref[r, :]),
                lambda r: _dot(conv[r], wo_ref[0:512, :]) + _dot(at_ref[r, :], wo_ref[512:1024, :]),
                ci, g1_ref, sh2_ref, sc2_ref, lng_ref, lnb_ref, rw_ref, rb_ref, *out_refs)


def _cd_out_kernel(x_ref, mg_ref, wo_ref, g1_ref, sh2_ref, sc2_ref, lng_ref, lnb_ref, rw_ref, rb_ref,
                   *out_refs):
    ci = _cond_row(pl.program_id(0))
    _mixer_tail(lambda r: x_ref[r, :], lambda r: _dot(mg_ref[r, :], wo_ref[...]),
                ci, g1_ref, sh2_ref, sc2_ref, lng_ref, lnb_ref, rw_ref, rb_ref, *out_refs)


def _tail_specs(layer):
    return [_mod_spec(layer, 2), _mod_spec(layer, 3), _mod_spec(layer, 4),
            _full((1, D)), _full((1, D)), _full((D, LANES)), _full((1, LANES))]


_TAIL_OUT_SPECS = [_row_spec(D), _row_spec(D), _row_spec(LANES), _row_spec(LANES),
                   pl.BlockSpec((None, SUB, LANES), lambda i: (i, 0, 0))]
_TAIL_OUT_SHAPES = [jax.ShapeDtypeStruct((T, D), F32), jax.ShapeDtypeStruct((T, D), BF16),
                    jax.ShapeDtypeStruct((T, LANES), jnp.int32), jax.ShapeDtypeStruct((T, LANES), F32),
                    jax.ShapeDtypeStruct((NT, SUB, LANES), jnp.int32)]


def _ab_out(xp, xs, gb, y, conv_w, attn, w_out, mod, tail):
    halo_prev = pl.BlockSpec((8, 512), lambda i: (jnp.maximum(i * (TM // 8) - 1, 0), 0))
    halo_next = pl.BlockSpec((8, 512), lambda i: (jnp.minimum((i + 1) * (TM // 8), T // 8 - 1), 0))
    return pl.pallas_call(
        _ab_out_kernel,
        grid=(NT,),
        in_specs=_x_specs() + [_row_spec(512), _row_spec(512), halo_prev, halo_next, _full((3, 512)),
                               _row_spec(512), _full((D, D))] + _tail_specs(0),
        out_specs=_TAIL_OUT_SPECS,
        out_shape=_TAIL_OUT_SHAPES,
        compiler_params=_params("arbitrary"),
        name="ab_out_proj",
    )(xp, xs, gb, y, y, y, conv_w, attn, w_out, mod, mod, mod, *tail)


def _cd_out(x, merged, w_out, mod, tail):
    return pl.pallas_call(
        _cd_out_kernel,
        grid=(NT,),
        in_specs=[_row_spec(D), _row_spec(D), _full((D, D))] + _tail_specs(1),
        out_specs=_TAIL_OUT_SPECS,
        out_shape=_TAIL_OUT_SHAPES,
        compiler_params=_params("arbitrary"),
        name="cd_out_proj",
    )(x, merged, w_out, mod, mod, mod, *tail)


def _mix_down(wd_ref, wmix_ref):
    half = D_FF // 2
    for c in range(D // LANES):
        wmix_ref[c, pl.ds(0, half, stride=2), :] = wd_ref[0:half, c * LANES:(c + 1) * LANES]
        wmix_ref[c, pl.ds(1, half, stride=2), :] = wd_ref[half:D_FF, c * LANES:(c + 1) * LANES]


def _prep_expert(wgu_ref, wd_ref, wgu_bf, wmix_ref, wd_bf):
    for c in range(4):
        wgu_bf[:, c * 512:(c + 1) * 512] = wgu_ref[:, c * 512:(c + 1) * 512].astype(BF16)
    _mix_down(wd_ref, wmix_ref)
    for c in range(D // LANES):
        wd_bf[:, c * LANES:(c + 1) * LANES] = wmix_ref[c].astype(BF16)


def _expert_ffn(u, wgu_bf, bgu_ref, wd_bf, bd_ref):
    rows = u.shape[0]
    ga = _dot(u, wgu_bf[:, 0:D_FF]) + bgu_ref[:, 0:D_FF]
    gb = _dot(u, wgu_bf[:, D_FF:2 * D_FF]) + bgu_ref[:, D_FF:2 * D_FF]
    even = (lax.broadcasted_iota(jnp.int32, (rows, LANES), 1) % 2) == 0
    hid = []
    for c in range(D_FF // LANES):
        a = ga[:, c * LANES:(c + 1) * LANES]
        b = gb[:, c * LANES:(c + 1) * LANES]
        gate = jnp.where(even, a, pltpu.roll(b, 1, axis=1))
        up = jnp.where(even, pltpu.roll(a, LANES - 1, axis=1), b)
        gate = jnp.minimum(gate, SWIGLU_LIMIT)
        up = jnp.clip(up, -SWIGLU_LIMIT, SWIGLU_LIMIT)
        hid.append(((up + 1.0) * gate * jax.nn.sigmoid(SWIGLU_ALPHA * gate)).astype(BF16))
    hid = jnp.concatenate(hid, axis=1)
    return _dot(hid, wd_bf[...]) + bd_ref[...]


def _moe_plan(cnt):
    c = cnt[:, 0, :N_EXPERTS]
    counts = jnp.sum(c, axis=0)
    tiles_e = (counts + MOE_TM - 1) // MOE_TM
    tile_end = jnp.cumsum(tiles_e)
    first_row = (tile_end - tiles_e) * MOE_TM
    run_start = first_row[None] + jnp.cumsum(c, axis=0) - c
    j = jnp.arange(MOE_TILES, dtype=jnp.int32)
    tile_expert = jnp.minimum(jnp.sum((j[:, None] >= tile_end[None]).astype(jnp.int32), axis=1), N_EXPERTS - 1)
    i32 = lambda a: a.astype(jnp.int32)
    experts = jnp.arange(N_EXPERTS, dtype=jnp.int32)
    owns = tiles_e > 0
    seg = jnp.cumsum(owns.astype(jnp.int32)) - owns.astype(jnp.int32)
    later = jnp.where((experts[None] > experts[:, None]) & owns[None], experts[None], N_EXPERTS)
    nxt = jnp.min(later, axis=1)
    nxt = jnp.where(nxt == N_EXPERTS, -1, nxt)
    per_tile = lambda v: jnp.sum(jnp.where(tile_expert[:, None] == experts[None], v[None], 0), axis=1)
    return dict(tile_expert=i32(tile_expert), n_used=i32(tile_end[-1:]),
                seg_parity=i32(per_tile(seg % 2)), next_expert=i32(per_tile(nxt)),
                run_start=i32(run_start)[:, None, :], run_len=i32(c)[:, None, :],
                pad_start=i32(first_row + counts)[None, None, :], pad_len=i32(tiles_e * MOE_TM - counts)[None, None, :])


def _for_each_run(start_ref, len_ref, copy):
    def body(e, local):
        n = len_ref[0, e]
        g = start_ref[0, e]

        def pieces(sizes):
            for size in sizes:
                covered = jnp.bitwise_and(n, -2 * size)

                @pl.when(jnp.bitwise_and(n, size) != 0)
                def _(covered=covered, size=size):
                    copy(local + covered, g + covered, size)

        @pl.when(n >= RUN_SIZES[2])
        def _():
            pieces(RUN_SIZES[:3])

        pieces(RUN_SIZES[3:])
        return local + n
    lax.fori_loop(0, N_EXPERTS, body, jnp.int32(0))


def _rows(ref, start, size):
    start = 0 if isinstance(start, int) and start == 0 else pl.multiple_of(start * SUB, SUB)
    return ref.at[pl.ds(start, size * SUB), :]


def _from_tiles(ref, rows):
    return jnp.concatenate([ref[pl.ds(c, rows, stride=SUB), :] for c in range(SUB)], axis=1)


def _to_tiles(ref, x):
    for c in range(SUB):
        ref[pl.ds(c, x.shape[0], stride=SUB), :] = x[:, c * LANES:(c + 1) * LANES]


def _run_spec(index):
    return pl.BlockSpec((None, 1, N_EXPERTS), index, memory_space=pltpu.SMEM)


def _dispatch_kernel(rs_ref, rl_ref, ps_ref, pn_ref, u_ref, pos_ref, xs_hbm, stage, zeros, sem, zsem):
    i = pl.program_id(0)
    slot = i % 2

    def wait(s):
        pltpu.make_async_copy(stage.at[s], stage.at[s], sem.at[s]).wait()

    @pl.when(i == 0)
    def _():
        zeros[...] = jnp.zeros_like(zeros)

        def pad(local, g, size):
            return pltpu.make_async_copy(_rows(zeros, 0, size), _rows(xs_hbm, g, size), zsem.at[0])
        _for_each_run(ps_ref, pn_ref, lambda *a: pad(*a).start())
        _for_each_run(ps_ref, pn_ref, lambda *a: pad(*a).wait())

    @pl.when(i >= 2)
    def _():
        wait(slot)

    place = pos_ref[...].astype(F32).T
    s = lax.broadcasted_iota(jnp.int32, (PAIRS_TM, TM), 0).astype(F32)
    onehot = jnp.zeros((PAIRS_TM, TM), F32)
    for k in range(TOP_K):
        onehot = jnp.where(s == place[k:k + 1, :], 1.0, onehot)
    _to_tiles(stage.at[slot], _dot(onehot.astype(BF16), u_ref[...]))
    _for_each_run(rs_ref, rl_ref, lambda l, g, size: pltpu.make_async_copy(
        _rows(stage.at[slot], l, size), _rows(xs_hbm, g, size), sem.at[slot]).start())

    @pl.when(i == NT - 1)
    def _():
        wait(1 - slot)
        wait(slot)


def _dispatch(u2, pos, plan):
    return pl.pallas_call(
        _dispatch_kernel,
        grid=(NT,),
        in_specs=[_run_spec(lambda i: (i, 0, 0)), _run_spec(lambda i: (i, 0, 0)),
                  _run_spec(lambda i: (0, 0, 0)), _run_spec(lambda i: (0, 0, 0)),
                  _row_spec(D), _row_spec(LANES)],
        out_specs=pl.BlockSpec(memory_space=pl.ANY),
        out_shape=jax.ShapeDtypeStruct((MOE_TILES * MOE_TM * SUB, LANES), F32),
        scratch_shapes=[pltpu.VMEM((2, PAIRS_TM * SUB, LANES), F32), pltpu.VMEM((TM * SUB, LANES), F32),
                        pltpu.SemaphoreType.DMA((2,)), pltpu.SemaphoreType.DMA((1,))],
        compiler_params=_params("arbitrary"),
        name="moe_dispatch",
    )(plan["run_start"], plan["run_len"], plan["pad_start"], plan["pad_len"], u2, pos)


def _moe_kernel(te_ref, nu_ref, par_ref, nxt_ref, x_ref, bgu_ref, bd_ref, wgu_hbm, wd_hbm, y_ref,
                wgu_f, wd_f, wgu_bf, wmix_ref, wd_bf, sem, *, layer):
    j = pl.program_id(0)
    used = j < nu_ref[0]
    buf = par_ref[j]

    def copies(e, s):
        return (pltpu.make_async_copy(wgu_hbm.at[layer, e], wgu_f.at[s], sem.at[s]),
                pltpu.make_async_copy(wd_hbm.at[layer, e], wd_f.at[s], sem.at[s]))

    @pl.when(jnp.logical_and(used, j == 0))
    def _():
        for cp in copies(te_ref[0], 0):
            cp.start(priority=1)

    @pl.when(jnp.logical_and(used, jnp.logical_or(j == 0, te_ref[j] != te_ref[jnp.maximum(j - 1, 0)])))
    def _():
        for cp in copies(te_ref[j], buf):
            cp.wait()

        @pl.when(nxt_ref[j] >= 0)
        def _():
            for cp in copies(nxt_ref[j], 1 - buf):
                cp.start(priority=1)

        _prep_expert(wgu_f.at[buf], wd_f.at[buf], wgu_bf, wmix_ref, wd_bf)

    @pl.when(used)
    def _():
        x = _from_tiles(x_ref, MOE_TM).astype(BF16)
        _to_tiles(y_ref, _expert_ffn(x, wgu_bf, bgu_ref, wd_bf, bd_ref))


def _moe(layer, xs, plan, w_gate_up, b_gate_up, w_down, b_down):
    rows = pl.BlockSpec((MOE_TM * SUB, LANES), lambda j, te, nu, par, nxt: (jnp.minimum(j, nu[0] - 1), 0))
    grid_spec = pltpu.PrefetchScalarGridSpec(
        num_scalar_prefetch=4,
        grid=(MOE_TILES,),
        in_specs=[rows,
                  pl.BlockSpec((None, None, 1, 2 * D_FF), lambda j, te, nu, par, nxt: (layer, te[j], 0, 0)),
                  pl.BlockSpec((None, None, 1, D), lambda j, te, nu, par, nxt: (layer, te[j], 0, 0)),
                  pl.BlockSpec(memory_space=pl.ANY), pl.BlockSpec(memory_space=pl.ANY)],
        out_specs=rows,
        scratch_shapes=[pltpu.VMEM((2, D, 2 * D_FF), F32), pltpu.VMEM((2, D_FF, D), F32),
                        pltpu.VMEM((D, 2 * D_FF), BF16), pltpu.VMEM((D // LANES, D_FF, LANES), F32),
                        pltpu.VMEM((D_FF, D), BF16), pltpu.SemaphoreType.DMA((2,))])

    def kernel(*refs):
        _moe_kernel(*refs, layer=layer)

    return pl.pallas_call(
        kernel,
        grid_spec=grid_spec,
        out_shape=jax.ShapeDtypeStruct((MOE_TILES * MOE_TM * SUB, LANES), F32),
        compiler_params=_params("arbitrary"),
        name="moe_experts",
    )(plan["tile_expert"], plan["n_used"], plan["seg_parity"], plan["next_expert"], xs,
      b_gate_up.reshape(DEPTH, N_EXPERTS, 1, 2 * D_FF), b_down.reshape(DEPTH, N_EXPERTS, 1, D), w_gate_up, w_down)


def _combine_kernel(rs_ref, rl_ref, rsn_ref, rln_ref, x1_ref, pos_ref, w_ref, g2_ref, lng_ref, lnb_ref, ys_hbm,
                    *rest, final):
    out_refs, (stage, sem) = rest[:-2], rest[-2:]
    i = pl.program_id(0)
    slot = i % 2
    other = 1 - slot

    def fetch(start_ref, len_ref, s):
        _for_each_run(start_ref, len_ref, lambda l, g, size: pltpu.make_async_copy(
            _rows(ys_hbm, g, size), _rows(stage.at[s], l, size), sem.at[s]).start())

    def wait(s):
        pltpu.make_async_copy(stage.at[s], stage.at[s], sem.at[s]).wait()

    @pl.when(i == 0)
    def _():
        fetch(rs_ref, rl_ref, 0)

    fetch(rsn_ref, rln_ref, other)
    wait(slot)
    y = _from_tiles(stage.at[slot], PAIRS_TM).astype(BF16)
    place = pos_ref[...].astype(F32)
    w = w_ref[...]
    s = lax.broadcasted_iota(jnp.int32, (TM, PAIRS_TM), 1).astype(F32)
    pick = jnp.zeros((TM, PAIRS_TM), F32)
    for k in range(TOP_K):
        pick = jnp.where(s == place[:, k:k + 1], w[:, k:k + 1], pick)
    f = _dot(pick.astype(BF16), y)
    ci = _cond_row(i)
    z = DEEPNORM_ALPHA * x1_ref[...] + g2_ref[pl.ds(ci, 1), :] * f
    out = _layer_norm(z, lng_ref[...], lnb_ref[...])
    if final:
        @pl.when(i < NT_P)
        def _():
            out_refs[0][...] = out

        @pl.when(i >= NT_P)
        def _():
            out_refs[1][...] = out
    else:
        out_refs[0][...] = out

    @pl.when(i == NT - 1)
    def _():
        wait(other)


def _combine(layer, x1, ys, pos, w, plan, mod, g, b):
    final = layer == DEPTH - 1
    nxt = lambda i: (jnp.minimum(i + 1, NT - 1), 0, 0)
    cur = lambda i: (i, 0, 0)
    if final:
        out_specs = _x_specs()
        out_shape = [jax.ShapeDtypeStruct((T_P, D), F32), jax.ShapeDtypeStruct((T_S, D), F32)]
    else:
        out_specs = _row_spec(D)
        out_shape = jax.ShapeDtypeStruct((T, D), F32)

    def kernel(*refs):
        _combine_kernel(*refs, final=final)

    return pl.pallas_call(
        kernel,
        grid=(NT,),
        in_specs=[_run_spec(cur), _run_spec(cur), _run_spec(nxt), _run_spec(nxt),
                  _row_spec(D), _row_spec(LANES), _row_spec(LANES), _mod_spec(layer, 5), _full((1, D)), _full((1, D)),
                  pl.BlockSpec(memory_space=pl.ANY)],
        out_specs=out_specs,
        out_shape=out_shape,
        scratch_shapes=[pltpu.VMEM((2, PAIRS_TM * SUB, LANES), F32), pltpu.SemaphoreType.DMA((2,))],
        compiler_params=_params("arbitrary"),
        name="moe_combine_norm" + ("_final" if final else ""),
    )(plan["run_start"], plan["run_len"], plan["run_start"], plan["run_len"], x1, pos, w, mod, g, b, ys)


def _moe_layer(layer, routed, experts, mod, ln2_g, ln2_b):
    x1, u2, pos, w, cnt = routed
    plan = _moe_plan(cnt)
    ys = _moe(layer, _dispatch(u2, pos, plan), plan, *experts)
    return _combine(layer, x1, ys, pos, w, plan, mod, ln2_g[layer][None], ln2_b[layer][None])


def _router_tail(l, ln1_g, ln1_b, router_w, router_b):
    rw = jnp.pad(router_w[l], ((0, 0), (0, LANES - N_EXPERTS)))
    rb = jnp.pad(router_b[l], (0, LANES - N_EXPERTS), constant_values=NEG)
    return ln1_g[l][None], ln1_b[l][None], rw, rb[None]


def kernel(x_prompt, x_sample, c, c_ctx, cache_diff_k, cache_diff_v, cache_na_k, cache_na_v, cache_gqa_k, cache_gqa_v, w_mod, b_mod, ln1_g, ln1_b, ln2_g, ln2_b, ab_w_in, ab_conv_w, ab_lambda_q1, ab_lambda_k1, ab_lambda_q2, ab_lambda_k2, ab_subln_g, ab_w_out, cd_w_in, cd_na_rpb, cd_q_norm_g, cd_k_norm_g, cd_w_out, router_w, router_b, w_gate_up, b_gate_up, w_down, b_down):
    xp = x_prompt.reshape(T_P, D)
    xs = x_sample.reshape(T_S, D)
    cond8 = jnp.concatenate([c_ctx[None], c, jnp.zeros((8 - 1 - B_S, D), F32)], axis=0)
    mod = _modulation(cond8, w_mod, b_mod)
    rope = _rope_tables()
    experts = (w_gate_up, b_gate_up, w_down, b_down)

    gb, y, q, k, v, new_diff_k, new_diff_v = _ab_in(xp, xs, mod, ab_w_in[0].astype(BF16), rope)
    lam_init = 0.8 - 0.6 * 1.0
    diff = (jnp.stack([ab_lambda_q1[0], ab_lambda_k1[0], ab_lambda_q2[0], ab_lambda_k2[0]]), ab_subln_g[0][None])
    attn = _flash_pair(q, k, v, latent=False, out_cols=512, col0=0, post="diff", diff=diff, lam_init=lam_init)
    attn = _flash_pair(q, k, v, latent=True, out_cols=512, col0=0, post="diff", diff=diff, lam_init=lam_init,
                       ctx=(cache_diff_k, cache_diff_v), ctx_mode="wide", prev=attn)
    routed = _ab_out(xp, xs, gb, y, ab_conv_w[0], attn, ab_w_out[0].astype(BF16), mod,
                     _router_tail(0, ln1_g, ln1_b, router_w, router_b))
    x = _moe_layer(0, routed, experts, mod, ln2_g, ln2_b)

    qg = jnp.tile(cd_q_norm_g[0], 2)[None]
    kg = jnp.tile(cd_k_norm_g[0], 2)[None]
    nq, nk, nv, gq, gk, gv, new_na_k, new_na_v, new_gqa_k, new_gqa_v = _cd_in(
        x, mod, cd_w_in[0].astype(BF16), qg, kg, rope)
    merged = _flash_pair(nq, nk, nv, latent=False, out_cols=D, col0=0, post="select")
    merged = _flash_pair(gq, gk, gv, latent=False, out_cols=D, col0=2, post="select", prev=merged)
    merged = _na_latent(nq, nk, nv, cache_na_k, cache_na_v, _na_bias(cd_na_rpb[0]), merged)
    merged = _flash_pair(gq, gk, gv, latent=True, out_cols=D, col0=2, post="select",
                         ctx=(cache_gqa_k, cache_gqa_v), ctx_mode="dup", prev=merged)
    routed = _cd_out(x, merged, cd_w_out[0].astype(BF16), mod,
                     _router_tail(1, ln1_g, ln1_b, router_w, router_b))
    y_p, y_s = _moe_layer(1, routed, experts, mod, ln2_g, ln2_b)

    return (y_p.reshape(B_P, N_P, D), y_s.reshape(B_S, N_S, D), new_diff_k, new_diff_v,
            new_na_k, new_na_v, new_gqa_k, new_gqa_v)
```
